```python
import jax, jax.numpy as jnp
from jax import lax
import numpy as np

D_MODEL = 2048
BATCH = 8
SEQ = 2048
DEPTH = 1

GRID_W = 64
CTX_LEN = 256
HEAD_DIM = 128
N_Q_HEADS = 16
N_KV_HEADS = 4
Q_GROUP = N_Q_HEADS // N_KV_HEADS
ATTN_WIDTH = N_Q_HEADS * HEAD_DIM
KV_WIDTH = N_KV_HEADS * HEAD_DIM
D_RNN = D_MODEL
N_RNN_BLOCKS = 16
RNN_BLOCK = D_RNN // N_RNN_BLOCKS
CONV_WIDTH = 4
CONV_PAD_LO = 1
CONV_PAD_HI = 2
LRU_C = 8.0
D_FF = 4 * D_MODEL
N_BRANCH = 2
Q_BLOCK = 128
ROPE_THETA = 10000.0
NORM_EPS = 1e-6
N_MOD = 6
N_IN = ATTN_WIDTH + 2 * KV_WIDTH + 2 * D_RNN + N_BRANCH * D_MODEL
IN_SPLITS = (ATTN_WIDTH,
             ATTN_WIDTH + KV_WIDTH,
             ATTN_WIDTH + 2 * KV_WIDTH,
             ATTN_WIDTH + 2 * KV_WIDTH + D_RNN,
             ATTN_WIDTH + 2 * KV_WIDTH + 2 * D_RNN)

kernel_name = "hybrid_gqa_rglru_parallel_dit_block"


def rms_norm(x, g):
    xf = x.astype(jnp.float32)
    y = xf * lax.rsqrt(jnp.mean(xf * xf, axis=-1, keepdims=True) + NORM_EPS)
    return (y * g.astype(jnp.float32)).astype(x.dtype)


def modulate(h, shift, scale):
    return h * (1 + scale) + shift


def rope_tables(row_idx, col_idx):
    n_freq = HEAD_DIM // 4
    inv_freq = ROPE_THETA ** (-jnp.arange(n_freq, dtype=jnp.float32) / n_freq)
    ang = jnp.concatenate([row_idx.astype(jnp.float32)[:, None] * inv_freq,
                           col_idx.astype(jnp.float32)[:, None] * inv_freq], axis=-1)
    return jnp.cos(ang), jnp.sin(ang)


def apply_rope(x, cos, sin):
    xf = x.astype(jnp.float32).reshape(*x.shape[:-1], HEAD_DIM // 2, 2)
    x1, x2 = xf[..., 0], xf[..., 1]
    cs, sn = cos[None, :, None, :], sin[None, :, None, :]
    out = jnp.stack([x1 * cs - x2 * sn, x1 * sn + x2 * cs], axis=-1).reshape(x.shape)
    return out.astype(x.dtype)


def gqa_softmax(q_blk, k, v):
    s = jnp.einsum('bqhgd,bkhd->bhgqk', q_blk, k).astype(jnp.float32) * (HEAD_DIM ** -0.5)
    p = jax.nn.softmax(s, axis=-1)
    return jnp.einsum('bhgqk,bkhd->bqhgd', p.astype(v.dtype), v)


def latent_attention(q, k_lat, v_lat, k_ctx, v_ctx):
    b, s = q.shape[:2]
    k_all = jnp.concatenate([k_ctx, k_lat], axis=1)
    v_all = jnp.concatenate([v_ctx, v_lat], axis=1)
    nb = s // Q_BLOCK
    qb = q.reshape(b, nb, Q_BLOCK, N_KV_HEADS, Q_GROUP, HEAD_DIM).transpose(1, 0, 2, 3, 4, 5)
    o = lax.map(lambda q_blk: gqa_softmax(q_blk, k_all, v_all), qb)
    return o.transpose(1, 0, 2, 3, 4, 5).reshape(b, s, ATTN_WIDTH)


def context_attention(q_c, k_c, v_c):
    b, l = q_c.shape[:2]
    o = gqa_softmax(q_c.reshape(b, l, N_KV_HEADS, Q_GROUP, HEAD_DIM), k_c, v_c)
    return o.reshape(b, l, ATTN_WIDTH)


def depthwise_conv(x, w, bias):
    s = x.shape[1]
    xp = jnp.pad(x, ((0, 0), (CONV_PAD_LO, CONV_PAD_HI), (0, 0)))
    y = bias
    for k in range(CONV_WIDTH):
        y = y + xp[:, k:k + s] * w[k]
    return y


def rglru_coeffs(x, w_r, b_r, w_i, b_i, lam):
    b, s, _ = x.shape
    xb = x.reshape(b, s, N_RNN_BLOCKS, RNN_BLOCK)
    r = jax.nn.sigmoid(jnp.einsum('bshi,hij->bshj', xb, w_r).reshape(b, s, D_RNN) + b_r)
    i = jax.nn.sigmoid(jnp.einsum('bshi,hij->bshj', xb, w_i).reshape(b, s, D_RNN) + b_i)
    log_a = -LRU_C * r.astype(jnp.float32) * jax.nn.softplus(-lam.astype(jnp.float32))
    a = jnp.exp(log_a)
    mult = jnp.sqrt(-jnp.expm1(2.0 * log_a))
    return a, mult * (i * x).astype(jnp.float32)


def linear_scan(a, bx, reverse):
    def combine(e1, e2):
        a1, b1 = e1
        a2, b2 = e2
        return a1 * a2, a2 * b1 + b2
    _, h = lax.associative_scan(combine, (a, bx), axis=1, reverse=reverse)
    return h


def rglru_direction(x_ctx, x_lat, w_r, b_r, w_i, b_i, lam, reverse):
    a_c, bx_c = rglru_coeffs(x_ctx, w_r, b_r, w_i, b_i, lam)
    h_c = linear_scan(a_c, bx_c, reverse)
    idx = 0 if reverse else -1
    h0 = h_c[:, idx]
    a_l, bx_l = rglru_coeffs(x_lat, w_r, b_r, w_i, b_i, lam)
    first = -1 if reverse else 0
    bx_l = bx_l.at[:, first].add(a_l[:, first] * h0)
    h_l = linear_scan(a_l, bx_l, reverse)
    return h_c, h_l


def merge_branches(attn_o, rnn_h, xg, gl, w_o_attn, w_o_rnn, w_out):
    y_attn = attn_o @ w_o_attn
    y_rnn = (rnn_h * jax.nn.gelu(xg)) @ w_o_rnn
    g_attn, g_rnn = jnp.split(jax.nn.sigmoid(gl), N_BRANCH, axis=-1)
    return (g_attn * y_attn + g_rnn * y_rnn) @ w_out


def mixer(h_lat, h_ctx, cos, sin, w_in, q_gain, k_gain, conv_w, conv_b, w_rg, b_rg, w_ig, b_ig,
          lam, w_o_attn, w_o_rnn, w_out, update_ctx):
    b, s, _ = h_lat.shape
    l = h_ctx.shape[1]
    q, k, v, xr, xg, gl = jnp.split(h_lat @ w_in, IN_SPLITS, axis=-1)
    qc, kc, vc, xrc, xgc, glc = jnp.split(h_ctx @ w_in, IN_SPLITS, axis=-1)

    q = apply_rope(rms_norm(q.reshape(b, s, N_Q_HEADS, HEAD_DIM), q_gain), cos, sin)
    k = apply_rope(rms_norm(k.reshape(b, s, N_KV_HEADS, HEAD_DIM), k_gain), cos, sin)
    v = v.reshape(b, s, N_KV_HEADS, HEAD_DIM)
    kc = rms_norm(kc.reshape(b, l, N_KV_HEADS, HEAD_DIM), k_gain)
    vc = vc.reshape(b, l, N_KV_HEADS, HEAD_DIM)
    attn_lat = latent_attention(q, k, v, kc, vc)

    xr_lat = depthwise_conv(xr, conv_w, conv_b)
    xr_ctx = depthwise_conv(xrc, conv_w, conv_b)
    hc_f, hl_f = rglru_direction(xr_ctx, xr_lat, w_rg[0], b_rg[0], w_ig[0], b_ig[0], lam[0], False)
    hc_b, hl_b = rglru_direction(xr_ctx, xr_lat, w_rg[1], b_rg[1], w_ig[1], b_ig[1], lam[1], True)
    rnn_lat = (hl_f + hl_b).astype(h_lat.dtype)

    out_lat = merge_branches(attn_lat, rnn_lat, xg, gl, w_o_attn, w_o_rnn, w_out)
    out_ctx = None
    if update_ctx:
        qc = rms_norm(qc.reshape(b, l, N_Q_HEADS, HEAD_DIM), q_gain)
        attn_ctx = context_attention(qc, kc, vc)
        rnn_ctx = (hc_f + hc_b).astype(h_ctx.dtype)
        out_ctx = merge_branches(attn_ctx, rnn_ctx, xgc, glc, w_o_attn, w_o_rnn, w_out)
    return out_lat, out_ctx


def sq_relu_mlp(h, w_up, w_down):
    return jnp.square(jax.nn.relu(h @ w_up)) @ w_down


def _fwd_setup_inputs(seed: int = 0) -> dict:
    key = jax.random.key(seed)
    ks = jax.random.split(key, 24)
    f32 = jnp.float32
    nrm = lambda k, shape, scale: jax.random.normal(k, shape, f32) * scale
    a0 = jax.random.uniform(ks[17], (DEPTH, 2, D_RNN), f32, 0.9, 0.999)
    sig = a0 ** (1.0 / LRU_C)
    lru_lambda = jnp.log(sig) - jnp.log1p(-sig)
    return {
        "x": nrm(ks[0], (BATCH, SEQ, D_MODEL), 1.0),
        "c": nrm(ks[1], (BATCH, D_MODEL), 1.0),
        "ctx": nrm(ks[2], (BATCH, CTX_LEN, D_MODEL), 1.0),
        "c_ctx": nrm(ks[3], (D_MODEL,), 1.0),
        "w_mod": nrm(ks[4], (DEPTH, D_MODEL, N_MOD * D_MODEL), 0.5 * D_MODEL ** -0.5),
        "b_mod": nrm(ks[5], (DEPTH, N_MOD * D_MODEL), 0.01),
        "g_mix": 1.0 + nrm(ks[6], (DEPTH, D_MODEL), 0.1),
        "g_mlp": 1.0 + nrm(ks[7], (DEPTH, D_MODEL), 0.1),
        "w_in": nrm(ks[8], (DEPTH, D_MODEL, N_IN), D_MODEL ** -0.5),
        "q_gain": 1.0 + nrm(ks[9], (DEPTH, HEAD_DIM), 0.1),
        "k_gain": 1.0 + nrm(ks[10], (DEPTH, HEAD_DIM), 0.1),
        "conv_w": nrm(ks[11], (DEPTH, CONV_WIDTH, D_RNN), CONV_WIDTH ** -0.5),
        "conv_b": nrm(ks[12], (DEPTH, D_RNN), 0.01),
        "w_rg": nrm(ks[13], (DEPTH, 2, N_RNN_BLOCKS, RNN_BLOCK, RNN_BLOCK), RNN_BLOCK ** -0.5),
        "b_rg": nrm(ks[14], (DEPTH, 2, D_RNN), 0.01),
        "w_ig": nrm(ks[15], (DEPTH, 2, N_RNN_BLOCKS, RNN_BLOCK, RNN_BLOCK), RNN_BLOCK ** -0.5),
        "b_ig": nrm(ks[16], (DEPTH, 2, D_RNN), 0.01),
        "lru_lambda": lru_lambda,
        "w_o_attn": nrm(ks[18], (DEPTH, ATTN_WIDTH, D_MODEL), ATTN_WIDTH ** -0.5),
        "w_o_rnn": nrm(ks[19], (DEPTH, D_RNN, D_MODEL), D_RNN ** -0.5),
        "w_out": nrm(ks[20], (DEPTH, D_MODEL, D_MODEL), D_MODEL ** -0.5),
        "w_up": nrm(ks[21], (DEPTH, D_MODEL, D_FF), D_MODEL ** -0.5),
        "w_down": nrm(ks[22], (DEPTH, D_FF, D_MODEL), D_FF ** -0.5),
        "g_final": 1.0 + nrm(ks[23], (D_MODEL,), 0.1),
    }


def _fwd_reference(x, c, ctx, c_ctx, w_mod, b_mod, g_mix, g_mlp, w_in, q_gain, k_gain, conv_w, conv_b,
              w_rg, b_rg, w_ig, b_ig, lru_lambda, w_o_attn, w_o_rnn, w_out, w_up, w_down, g_final):
    n_lat = x.shape[1]
    rows = n_lat // GRID_W
    row_idx = jnp.repeat(jnp.arange(rows), GRID_W)
    col_idx = jnp.tile(jnp.arange(GRID_W), rows)
    cos, sin = rope_tables(row_idx, col_idx)

    for layer in range(DEPTH):
        update_ctx = layer < DEPTH - 1
        mod_lat = (jax.nn.silu(c) @ w_mod[layer] + b_mod[layer])[:, None, :]
        mod_ctx = jax.nn.silu(c_ctx) @ w_mod[layer] + b_mod[layer]
        sh_a, sc_a, ga_a, sh_f, sc_f, ga_f = jnp.split(mod_lat, N_MOD, axis=-1)
        csh_a, csc_a, cga_a, csh_f, csc_f, cga_f = jnp.split(mod_ctx, N_MOD, axis=-1)

        h_lat = modulate(rms_norm(x, g_mix[layer]), sh_a, sc_a)
        h_ctx = modulate(rms_norm(ctx, g_mix[layer]), csh_a, csc_a)
        mix_lat, mix_ctx = mixer(h_lat, h_ctx, cos, sin, w_in[layer], q_gain[layer], k_gain[layer],
                                 conv_w[layer], conv_b[layer], w_rg[layer], b_rg[layer],
                                 w_ig[layer], b_ig[layer], lru_lambda[layer], w_o_attn[layer],
                                 w_o_rnn[layer], w_out[layer], update_ctx)
        x = x + ga_a * mix_lat
        x = x + ga_f * sq_relu_mlp(modulate(rms_norm(x, g_mlp[layer]), sh_f, sc_f),
                                   w_up[layer], w_down[layer])
        if update_ctx:
            ctx = ctx + cga_a * mix_ctx
            ctx = ctx + cga_f * sq_relu_mlp(modulate(rms_norm(ctx, g_mlp[layer]), csh_f, csc_f),
                                            w_up[layer], w_down[layer])
    return rms_norm(x, g_final)


import jax as _jax
import jax.numpy as _jnp

TWIN_FORMAT = 'train_step'
FWD_PARAMS = ['x', 'c', 'ctx', 'c_ctx', 'w_mod', 'b_mod', 'g_mix', 'g_mlp', 'w_in', 'q_gain', 'k_gain', 'conv_w', 'conv_b', 'w_rg', 'b_rg', 'w_ig', 'b_ig', 'lru_lambda', 'w_o_attn', 'w_o_rnn', 'w_out', 'w_up', 'w_down', 'g_final']
TWIN_WEIGHTS = ['c_ctx', 'w_mod', 'b_mod', 'g_mix', 'g_mlp', 'w_in', 'q_gain', 'k_gain', 'conv_w', 'conv_b', 'w_rg', 'b_rg', 'w_ig', 'b_ig', 'lru_lambda', 'w_o_attn', 'w_o_rnn', 'w_out', 'w_up', 'w_down', 'g_final']
TWIN_DIFF_INPUT = 'x'
TWIN_INPUTS = ['x', 'c', 'ctx', 'c_ctx', 'w_mod', 'b_mod', 'g_mix', 'g_mlp', 'w_in', 'q_gain', 'k_gain', 'conv_w', 'conv_b', 'w_rg', 'b_rg', 'w_ig', 'b_ig', 'lru_lambda', 'w_o_attn', 'w_o_rnn', 'w_out', 'w_up', 'w_down', 'g_final', 'loss_target', 'm_c_ctx', 'm_w_mod', 'm_b_mod', 'm_g_mix', 'm_g_mlp', 'm_w_in', 'm_q_gain', 'm_k_gain', 'm_conv_w', 'm_conv_b', 'm_w_rg', 'm_b_rg', 'm_w_ig', 'm_b_ig', 'm_lru_lambda', 'm_w_o_attn', 'm_w_o_rnn', 'm_w_out', 'm_w_up', 'm_w_down', 'm_g_final', 'v_c_ctx', 'v_w_mod', 'v_b_mod', 'v_g_mix', 'v_g_mlp', 'v_w_in', 'v_q_gain', 'v_k_gain', 'v_conv_w', 'v_conv_b', 'v_w_rg', 'v_b_rg', 'v_w_ig', 'v_b_ig', 'v_lru_lambda', 'v_w_o_attn', 'v_w_o_rnn', 'v_w_out', 'v_w_up', 'v_w_down', 'v_g_final']
TWIN_OUTPUTS = ['loss', 'grad_x', 'grad_c_ctx', 'grad_w_mod', 'grad_b_mod', 'grad_g_mix', 'grad_g_mlp', 'grad_w_in', 'grad_q_gain', 'grad_k_gain', 'grad_conv_w', 'grad_conv_b', 'grad_w_rg', 'grad_b_rg', 'grad_w_ig', 'grad_b_ig', 'grad_lru_lambda', 'grad_w_o_attn', 'grad_w_o_rnn', 'grad_w_out', 'grad_w_up', 'grad_w_down', 'grad_g_final', 'delta_c_ctx', 'delta_w_mod', 'delta_b_mod', 'delta_g_mix', 'delta_g_mlp', 'delta_w_in', 'delta_q_gain', 'delta_k_gain', 'delta_conv_w', 'delta_conv_b', 'delta_w_rg', 'delta_b_rg', 'delta_w_ig', 'delta_b_ig', 'delta_lru_lambda', 'delta_w_o_attn', 'delta_w_o_rnn', 'delta_w_out', 'delta_w_up', 'delta_w_down', 'delta_g_final', 'new_m_c_ctx', 'new_m_w_mod', 'new_m_b_mod', 'new_m_g_mix', 'new_m_g_mlp', 'new_m_w_in', 'new_m_q_gain', 'new_m_k_gain', 'new_m_conv_w', 'new_m_conv_b', 'new_m_w_rg', 'new_m_b_rg', 'new_m_w_ig', 'new_m_b_ig', 'new_m_lru_lambda', 'new_m_w_o_attn', 'new_m_w_o_rnn', 'new_m_w_out', 'new_m_w_up', 'new_m_w_down', 'new_m_g_final', 'new_v_c_ctx', 'new_v_w_mod', 'new_v_b_mod', 'new_v_g_mix', 'new_v_g_mlp', 'new_v_w_in', 'new_v_q_gain', 'new_v_k_gain', 'new_v_conv_w', 'new_v_conv_b', 'new_v_w_rg', 'new_v_b_rg', 'new_v_w_ig', 'new_v_b_ig', 'new_v_lru_lambda', 'new_v_w_o_attn', 'new_v_w_o_rnn', 'new_v_w_out', 'new_v_w_up', 'new_v_w_down', 'new_v_g_final']
TWIN_LEAF_KINDS = {'loss': 'loss', 'grad_x': 'grad_x', 'grad_c_ctx': 'grad_w', 'grad_w_mod': 'grad_w', 'grad_b_mod': 'grad_w', 'grad_g_mix': 'grad_w', 'grad_g_mlp': 'grad_w', 'grad_w_in': 'grad_w', 'grad_q_gain': 'grad_w', 'grad_k_gain': 'grad_w', 'grad_conv_w': 'grad_w', 'grad_conv_b': 'grad_w', 'grad_w_rg': 'grad_w', 'grad_b_rg': 'grad_w', 'grad_w_ig': 'grad_w', 'grad_b_ig': 'grad_w', 'grad_lru_lambda': 'grad_w', 'grad_w_o_attn': 'grad_w', 'grad_w_o_rnn': 'grad_w', 'grad_w_out': 'grad_w', 'grad_w_up': 'grad_w', 'grad_w_down': 'grad_w', 'grad_g_final': 'grad_w', 'delta_c_ctx': 'delta_w', 'delta_w_mod': 'delta_w', 'delta_b_mod': 'delta_w', 'delta_g_mix': 'delta_w', 'delta_g_mlp': 'delta_w', 'delta_w_in': 'delta_w', 'delta_q_gain': 'delta_w', 'delta_k_gain': 'delta_w', 'delta_conv_w': 'delta_w', 'delta_conv_b': 'delta_w', 'delta_w_rg': 'delta_w', 'delta_b_rg': 'delta_w', 'delta_w_ig': 'delta_w', 'delta_b_ig': 'delta_w', 'delta_lru_lambda': 'delta_w', 'delta_w_o_attn': 'delta_w', 'delta_w_o_rnn': 'delta_w', 'delta_w_out': 'delta_w', 'delta_w_up': 'delta_w', 'delta_w_down': 'delta_w', 'delta_g_final': 'delta_w', 'new_m_c_ctx': 'new_m', 'new_m_w_mod': 'new_m', 'new_m_b_mod': 'new_m', 'new_m_g_mix': 'new_m', 'new_m_g_mlp': 'new_m', 'new_m_w_in': 'new_m', 'new_m_q_gain': 'new_m', 'new_m_k_gain': 'new_m', 'new_m_conv_w': 'new_m', 'new_m_conv_b': 'new_m', 'new_m_w_rg': 'new_m', 'new_m_b_rg': 'new_m', 'new_m_w_ig': 'new_m', 'new_m_b_ig': 'new_m', 'new_m_lru_lambda': 'new_m', 'new_m_w_o_attn': 'new_m', 'new_m_w_o_rnn': 'new_m', 'new_m_w_out': 'new_m', 'new_m_w_up': 'new_m', 'new_m_w_down': 'new_m', 'new_m_g_final': 'new_m', 'new_v_c_ctx': 'new_v', 'new_v_w_mod': 'new_v', 'new_v_b_mod': 'new_v', 'new_v_g_mix': 'new_v', 'new_v_g_mlp': 'new_v', 'new_v_w_in': 'new_v', 'new_v_q_gain': 'new_v', 'new_v_k_gain': 'new_v', 'new_v_conv_w': 'new_v', 'new_v_conv_b': 'new_v', 'new_v_w_rg': 'new_v', 'new_v_b_rg': 'new_v', 'new_v_w_ig': 'new_v', 'new_v_b_ig': 'new_v', 'new_v_lru_lambda': 'new_v', 'new_v_w_o_attn': 'new_v', 'new_v_w_o_rnn': 'new_v', 'new_v_w_out': 'new_v', 'new_v_w_up': 'new_v', 'new_v_w_down': 'new_v', 'new_v_g_final': 'new_v'}


def _forward(args):
    return _fwd_reference(*[args[k] for k in FWD_PARAMS])


def _output_shape():
    out = _jax.eval_shape(lambda: _forward(_fwd_setup_inputs(0)))
    return out.shape, out.dtype

N_MICROBATCH = 1
ADAM_LR = 0.001
ADAM_B1 = 0.9
ADAM_B2 = 0.999
ADAM_EPS = 1e-08
ADAM_WD = 0.01
ADAM_STEP = 10
PER_EXAMPLE_BATCH_AXIS = {'x': 0, 'c': 0, 'ctx': 0, 'loss_target': 0}
SHARED_INPUTS = []
_WEIGHT_DTYPES = {'c_ctx': _jnp.float32, 'w_mod': _jnp.float32, 'b_mod': _jnp.float32, 'g_mix': _jnp.float32, 'g_mlp': _jnp.float32, 'w_in': _jnp.float32, 'q_gain': _jnp.float32, 'k_gain': _jnp.float32, 'conv_w': _jnp.float32, 'conv_b': _jnp.float32, 'w_rg': _jnp.float32, 'b_rg': _jnp.float32, 'w_ig': _jnp.float32, 'b_ig': _jnp.float32, 'lru_lambda': _jnp.float32, 'w_o_attn': _jnp.float32, 'w_o_rnn': _jnp.float32, 'w_out': _jnp.float32, 'w_up': _jnp.float32, 'w_down': _jnp.float32, 'g_final': _jnp.float32}
MOMENT_SCALE = {'c_ctx': 7.148467e-03, 'w_mod': 1.205099e-01, 'b_mod': 2.223597e-01, 'g_mix': 2.756403e-02, 'g_mlp': 2.936125e-02, 'w_in': 2.135493e-02, 'q_gain': 3.302639e-03, 'k_gain': 3.391340e-03, 'conv_w': 4.374025e-02, 'conv_b': 1.327092e-01, 'w_rg': 2.138608e-03, 'b_rg': 3.270382e-03, 'w_ig': 4.299325e-03, 'b_ig': 8.320268e-03, 'lru_lambda': 8.606897e-03, 'w_o_attn': 8.306167e-03, 'w_o_rnn': 3.982971e-02, 'w_out': 3.995275e-02, 'w_up': 1.597246e-02, 'w_down': 4.891719e-02, 'g_final': 8.098548e+00}


def _to_microbatches(a, axis):
    t = _jnp.moveaxis(a, axis, 0)
    t = t.reshape((N_MICROBATCH, t.shape[0] // N_MICROBATCH) + t.shape[1:])
    return _jnp.moveaxis(t, 1, axis + 1)


def setup_inputs(seed: int = 0) -> dict:
    inp = _fwd_setup_inputs(seed)
    key = _jax.random.fold_in(_jax.random.key(seed), 7919)
    shape, _ = _output_shape()
    out = dict(inp)
    out["loss_target"] = _jax.random.normal(_jax.random.fold_in(key, 0), shape, _jnp.float32)
    for i, name in enumerate(TWIN_WEIGHTS):
        w = inp[name].astype(_jnp.float32)
        if MOMENT_SCALE is None:
            s = _jnp.sqrt(_jnp.mean(_jnp.square(w)) + 1e-30)
        else:
            s = MOMENT_SCALE[name]
        km, kv = _jax.random.split(_jax.random.fold_in(key, i + 1))
        out[name] = w
        out["m_" + name] = s * _jax.random.normal(km, w.shape, _jnp.float32)
        out["v_" + name] = (s * s) * _jax.random.uniform(kv, w.shape, _jnp.float32, 0.5, 1.5)
    if N_MICROBATCH > 1:
        for name, axis in PER_EXAMPLE_BATCH_AXIS.items():
            out[name] = _to_microbatches(out[name], axis)
    return {'x': out['x'], 'c': out['c'], 'ctx': out['ctx'], 'c_ctx': out['c_ctx'], 'w_mod': out['w_mod'], 'b_mod': out['b_mod'], 'g_mix': out['g_mix'], 'g_mlp': out['g_mlp'], 'w_in': out['w_in'], 'q_gain': out['q_gain'], 'k_gain': out['k_gain'], 'conv_w': out['conv_w'], 'conv_b': out['conv_b'], 'w_rg': out['w_rg'], 'b_rg': out['b_rg'], 'w_ig': out['w_ig'], 'b_ig': out['b_ig'], 'lru_lambda': out['lru_lambda'], 'w_o_attn': out['w_o_attn'], 'w_o_rnn': out['w_o_rnn'], 'w_out': out['w_out'], 'w_up': out['w_up'], 'w_down': out['w_down'], 'g_final': out['g_final'], 'loss_target': out['loss_target'], 'm_c_ctx': out['m_c_ctx'], 'm_w_mod': out['m_w_mod'], 'm_b_mod': out['m_b_mod'], 'm_g_mix': out['m_g_mix'], 'm_g_mlp': out['m_g_mlp'], 'm_w_in': out['m_w_in'], 'm_q_gain': out['m_q_gain'], 'm_k_gain': out['m_k_gain'], 'm_conv_w': out['m_conv_w'], 'm_conv_b': out['m_conv_b'], 'm_w_rg': out['m_w_rg'], 'm_b_rg': out['m_b_rg'], 'm_w_ig': out['m_w_ig'], 'm_b_ig': out['m_b_ig'], 'm_lru_lambda': out['m_lru_lambda'], 'm_w_o_attn': out['m_w_o_attn'], 'm_w_o_rnn': out['m_w_o_rnn'], 'm_w_out': out['m_w_out'], 'm_w_up': out['m_w_up'], 'm_w_down': out['m_w_down'], 'm_g_final': out['m_g_final'], 'v_c_ctx': out['v_c_ctx'], 'v_w_mod': out['v_w_mod'], 'v_b_mod': out['v_b_mod'], 'v_g_mix': out['v_g_mix'], 'v_g_mlp': out['v_g_mlp'], 'v_w_in': out['v_w_in'], 'v_q_gain': out['v_q_gain'], 'v_k_gain': out['v_k_gain'], 'v_conv_w': out['v_conv_w'], 'v_conv_b': out['v_conv_b'], 'v_w_rg': out['v_w_rg'], 'v_b_rg': out['v_b_rg'], 'v_w_ig': out['v_w_ig'], 'v_b_ig': out['v_b_ig'], 'v_lru_lambda': out['v_lru_lambda'], 'v_w_o_attn': out['v_w_o_attn'], 'v_w_o_rnn': out['v_w_o_rnn'], 'v_w_out': out['v_w_out'], 'v_w_up': out['v_w_up'], 'v_w_down': out['v_w_down'], 'v_g_final': out['v_g_final']}


def _loss(weights, diff, rest, loss_target):
    with _jax.named_scope("forward"):
        args = {**rest, TWIN_DIFF_INPUT: diff, **{k: w.astype(_WEIGHT_DTYPES[k]) for k, w in weights.items()}}
        y = _forward(args)
    with _jax.named_scope("loss_head"):
        err = _jnp.square(y.astype(_jnp.float32) - loss_target)
        return 0.5 * _jnp.sum(_jnp.mean(err, axis=-1)) if err.ndim else 0.5 * err


def _adamw(w, g, m, v):
    m = ADAM_B1 * m + (1.0 - ADAM_B1) * g
    v = ADAM_B2 * v + (1.0 - ADAM_B2) * _jnp.square(g)
    m_hat = m / (1.0 - ADAM_B1 ** ADAM_STEP)
    v_hat = v / (1.0 - ADAM_B2 ** ADAM_STEP)
    delta = -ADAM_LR * (m_hat / (_jnp.sqrt(v_hat) + ADAM_EPS) + ADAM_WD * w)
    return delta, m, v


def reference(x, c, ctx, c_ctx, w_mod, b_mod, g_mix, g_mlp, w_in, q_gain, k_gain, conv_w, conv_b, w_rg, b_rg, w_ig, b_ig, lru_lambda, w_o_attn, w_o_rnn, w_out, w_up, w_down, g_final, loss_target, m_c_ctx, m_w_mod, m_b_mod, m_g_mix, m_g_mlp, m_w_in, m_q_gain, m_k_gain, m_conv_w, m_conv_b, m_w_rg, m_b_rg, m_w_ig, m_b_ig, m_lru_lambda, m_w_o_attn, m_w_o_rnn, m_w_out, m_w_up, m_w_down, m_g_final, v_c_ctx, v_w_mod, v_b_mod, v_g_mix, v_g_mlp, v_w_in, v_q_gain, v_k_gain, v_conv_w, v_conv_b, v_w_rg, v_b_rg, v_w_ig, v_b_ig, v_lru_lambda, v_w_o_attn, v_w_o_rnn, v_w_out, v_w_up, v_w_down, v_g_final):
    given = dict(x=x, c=c, ctx=ctx, c_ctx=c_ctx, w_mod=w_mod, b_mod=b_mod, g_mix=g_mix, g_mlp=g_mlp, w_in=w_in, q_gain=q_gain, k_gain=k_gain, conv_w=conv_w, conv_b=conv_b, w_rg=w_rg, b_rg=b_rg, w_ig=w_ig, b_ig=b_ig, lru_lambda=lru_lambda, w_o_attn=w_o_attn, w_o_rnn=w_o_rnn, w_out=w_out, w_up=w_up, w_down=w_down, g_final=g_final, loss_target=loss_target, m_c_ctx=m_c_ctx, m_w_mod=m_w_mod, m_b_mod=m_b_mod, m_g_mix=m_g_mix, m_g_mlp=m_g_mlp, m_w_in=m_w_in, m_q_gain=m_q_gain, m_k_gain=m_k_gain, m_conv_w=m_conv_w, m_conv_b=m_conv_b, m_w_rg=m_w_rg, m_b_rg=m_b_rg, m_w_ig=m_w_ig, m_b_ig=m_b_ig, m_lru_lambda=m_lru_lambda, m_w_o_attn=m_w_o_attn, m_w_o_rnn=m_w_o_rnn, m_w_out=m_w_out, m_w_up=m_w_up, m_w_down=m_w_down, m_g_final=m_g_final, v_c_ctx=v_c_ctx, v_w_mod=v_w_mod, v_b_mod=v_b_mod, v_g_mix=v_g_mix, v_g_mlp=v_g_mlp, v_w_in=v_w_in, v_q_gain=v_q_gain, v_k_gain=v_k_gain, v_conv_w=v_conv_w, v_conv_b=v_conv_b, v_w_rg=v_w_rg, v_b_rg=v_b_rg, v_w_ig=v_w_ig, v_b_ig=v_b_ig, v_lru_lambda=v_lru_lambda, v_w_o_attn=v_w_o_attn, v_w_o_rnn=v_w_o_rnn, v_w_out=v_w_out, v_w_up=v_w_up, v_w_down=v_w_down, v_g_final=v_g_final)
    weights = {n: given[n] for n in TWIN_WEIGHTS}
    shared = {n: given[n] for n in SHARED_INPUTS}
    per_example = {n: given[n] for n in ['x', 'c', 'ctx']}
    grad_fn = _jax.value_and_grad(_loss, argnums=(0, 1))

    def one_microbatch(ex, loss_target):
        ex = dict(ex)
        diff = ex.pop(TWIN_DIFF_INPUT)
        return grad_fn(weights, diff, {**shared, **ex}, loss_target)

    if N_MICROBATCH == 1:
        loss, (grad_w, grad_x) = one_microbatch(per_example, given["loss_target"])
    else:
        def body(carry, xs):
            loss_sum, grad_sum = carry
            l_k, (gw_k, gx_k) = one_microbatch(xs[0], xs[1])
            with _jax.named_scope("update"):
                return (loss_sum + l_k, _jax.tree.map(_jnp.add, grad_sum, gw_k)), gx_k

        init = (_jnp.zeros((), _jnp.float32), _jax.tree.map(_jnp.zeros_like, weights))
        (loss, grad_w), grad_x = _jax.lax.scan(body, init, (per_example, given["loss_target"]))
    with _jax.named_scope("update"):
        delta_w, new_m, new_v = {}, {}, {}
        for n in TWIN_WEIGHTS:
            delta_w[n], new_m[n], new_v[n] = _adamw(weights[n], grad_w[n], given["m_" + n], given["v_" + n])
    return (loss, grad_x, *[grad_w[n] for n in TWIN_WEIGHTS], *[delta_w[n] for n in TWIN_WEIGHTS],
            *[new_m[n] for n in TWIN_WEIGHTS], *[new_v[n] for n in TWIN_WEIGHTS])
```

```python
import functools
import math

import jax
import jax.numpy as jnp
from jax import lax
from jax.experimental import pallas as pl
from jax.experimental.pallas import tpu as pltpu

F32 = jnp.float32
BF16 = jnp.bfloat16
MESH_ID = pl.DeviceIdType.MESH
ANY = pl.BlockSpec(memory_space=pl.ANY)

NORM_EPS = 1e-6
LRU_C = 8.0
GRID_W = 64
ROPE_THETA = 10000.0
N_MOD = 6
CONV_WIDTH = 4
ADAM_LR = 0.001
ADAM_B1 = 0.9
ADAM_B2 = 0.999
ADAM_EPS = 1e-08
ADAM_WD = 0.01
ADAM_STEP = 10

LANES = 128
SUBLANES = 8
V7X_VMEM_LIMIT = 48 * 1024 * 1024
N_CHIPS = 4
N_DEV = 8
GELU_C = math.sqrt(2.0 / math.pi)
GELU_A = 0.044715


def _tile(dim, pref, align):
    t = min(pref, dim)
    t -= t % align
    while t >= align:
        if dim % t == 0:
            return t
        t -= align
    return dim


def _params(*sem):
    return pltpu.CompilerParams(dimension_semantics=sem, vmem_limit_bytes=V7X_VMEM_LIMIT)


def _sds(shape, dtype=F32):
    return jax.ShapeDtypeStruct(shape, dtype)


def _matmul(a, b, *, ta=False, tb=False, out_dtype=F32, name, tm=1024, tn=1024, tk=512):
    k_dim, m = a.shape if ta else a.shape[::-1]
    n, k2 = b.shape if tb else b.shape[::-1]
    assert k_dim == k2, (a.shape, b.shape, ta, tb)
    tm = _tile(m, tm, LANES if ta else 16)
    tn = _tile(n, tn, 16 if tb else LANES)
    tk = _tile(k_dim, tk, LANES)
    nk = k_dim // tk
    dims = (((0 if ta else 1,), (1 if tb else 0,)), ((), ()))

    def body(a_ref, b_ref, o_ref, acc_ref):
        k = pl.program_id(2)

        @pl.when(k == 0)
        def _():
            acc_ref[...] = jnp.zeros_like(acc_ref)

        acc_ref[...] += lax.dot_general(a_ref[...].astype(BF16), b_ref[...].astype(BF16), dims,
                                        preferred_element_type=F32)

        @pl.when(k == nk - 1)
        def _():
            o_ref[...] = acc_ref[...].astype(o_ref.dtype)

    a_spec = pl.BlockSpec((tk, tm), lambda i, j, k: (k, i)) if ta else pl.BlockSpec((tm, tk), lambda i, j, k: (i, k))
    b_spec = pl.BlockSpec((tn, tk), lambda i, j, k: (j, k)) if tb else pl.BlockSpec((tk, tn), lambda i, j, k: (k, j))
    return pl.pallas_call(
        body, name=name, grid=(m // tm, n // tn, nk), in_specs=[a_spec, b_spec],
        out_specs=pl.BlockSpec((tm, tn), lambda i, j, k: (i, j)), out_shape=_sds((m, n), out_dtype),
        scratch_shapes=[pltpu.VMEM((tm, tn), F32)], compiler_params=_params("parallel", "parallel", "arbitrary"),
    )(a, b)


def _silu(x):
    return x * jax.nn.sigmoid(x)


def _gelu(x):
    return 0.5 * x * (1.0 + jnp.tanh(GELU_C * (x + GELU_A * x * x * x)))


def _gelu_grad(x):
    t = jnp.tanh(GELU_C * (x + GELU_A * x * x * x))
    return 0.5 * (1.0 + t) + 0.5 * x * (1.0 - t * t) * GELU_C * (1.0 + 3.0 * GELU_A * x * x)


def _expm1_nonpos(x):
    series = x * (1.0 + x * (1.0 / 2 + x * (1.0 / 6 + x * (1.0 / 24 + x * (1.0 / 120 + x * (1.0 / 720 + x / 5040))))))
    return jnp.where(x > -0.25, series, jnp.exp(x) - 1.0)


def _softplus(x):
    return jnp.maximum(x, 0.0) + jnp.log1p(jnp.exp(-jnp.abs(x)))


def _rms_stats(x):
    return lax.rsqrt(jnp.mean(x * x, axis=-1, keepdims=True) + NORM_EPS)


def _rms_bwd(dxhat, xhat, rstd):
    return rstd * (dxhat - xhat * jnp.mean(dxhat * xhat, axis=-1, keepdims=True))


def _colsum(v):
    return jnp.sum(v, axis=0, keepdims=True)


def _mod_fwd(c16, w_mod, b_mod_shard):
    r, d = c16.shape
    n = w_mod.shape[1]
    tn = _tile(n, 512, LANES)

    def body(c_ref, w_ref, b_ref, o_ref, s_ref):
        s = _silu(c_ref[...])
        s_ref[...] = s
        o_ref[...] = jnp.dot(s.astype(BF16), w_ref[...].astype(BF16), preferred_element_type=F32) + b_ref[...]

    return pl.pallas_call(
        body, name="mod_fwd", grid=(n // tn,),
        in_specs=[pl.BlockSpec((r, d), lambda j: (0, 0)), pl.BlockSpec((d, tn), lambda j: (0, j)),
                  pl.BlockSpec((1, tn), lambda j: (0, j))],
        out_specs=[pl.BlockSpec((r, tn), lambda j: (0, j)), pl.BlockSpec((r, d), lambda j: (0, 0))],
        out_shape=[_sds((r, n)), _sds((r, d))], compiler_params=_params("arbitrary"),
    )(c16, w_mod, b_mod_shard)


def _c_ctx_grad(parts, c_ctx_row):
    d = c_ctx_row.shape[1]

    def body(p_ref, c_ref, o_ref):
        tot = p_ref[0, 0:1, :]
        for chip in range(1, N_CHIPS):
            tot = tot + p_ref[2 * chip, 0:1, :]
        c = c_ref[...]
        sg = jax.nn.sigmoid(c)
        o_ref[...] = tot * (sg * (1.0 + c * (1.0 - sg)))

    return pl.pallas_call(body, name="c_ctx_grad", out_shape=_sds((1, d)), compiler_params=_params())(parts, c_ctx_row)


def _norm_mod_fwd(xcat, g, mod4, n_ctx_rows, tr):
    t, d = xcat.shape
    nctx = n_ctx_rows // tr

    def body(x_ref, g_ref, mod_ref, h_ref):
        x = x_ref[...]
        n = x * _rms_stats(x) * g_ref[...]
        is_ctx = pl.program_id(0) < nctx
        sh = jnp.where(is_ctx, mod_ref[0:1, :], mod_ref[2:3, :])
        sc = jnp.where(is_ctx, mod_ref[1:2, :], mod_ref[3:4, :])
        h_ref[...] = (n * (1.0 + sc) + sh).astype(BF16)

    return pl.pallas_call(
        body, name="norm_mod_fwd", grid=(t // tr,),
        in_specs=[pl.BlockSpec((tr, d), lambda i: (i, 0)), pl.BlockSpec((1, d), lambda i: (0, 0)),
                  pl.BlockSpec((4, d), lambda i: (0, 0))],
        out_specs=pl.BlockSpec((tr, d), lambda i: (i, 0)), out_shape=_sds((t, d), BF16),
        compiler_params=_params("parallel"),
    )(xcat, g, mod4)


def _norm_mod_bwd1(xcat, dh, dx1, g, mod4, n_ctx_rows, tr):
    t, d = xcat.shape
    nctx = n_ctx_rows // tr
    s = t - n_ctx_rows

    def body(x_ref, dh_ref, dx1_ref, g_ref, mod_ref, dx_ref, sums_ref):
        i = pl.program_id(0)
        is_ctx = i < nctx

        @pl.when(i == 0)
        def _():
            sums_ref[...] = jnp.zeros_like(sums_ref)

        x = x_ref[...]
        dh_ = dh_ref[...]
        rstd = _rms_stats(x)
        xhat = x * rstd
        gg = g_ref[...]
        sc = jnp.where(is_ctx, mod_ref[1:2, :], mod_ref[3:4, :])
        dxhat = dh_ * (1.0 + sc) * gg
        dx_ref[...] = dx1_ref[...] + _rms_bwd(dxhat, xhat, rstd)
        part = [_colsum(dh_), _colsum(dh_ * xhat * gg), _colsum(dh_ * (1.0 + sc) * xhat)]

        @pl.when(is_ctx)
        def _():
            for k, row in enumerate(part):
                sums_ref[3 + k:4 + k, :] += row

        @pl.when(jnp.logical_not(is_ctx))
        def _():
            for k, row in enumerate(part):
                sums_ref[k:k + 1, :] += row

    lat = lambda i: (jnp.maximum(i - nctx, 0), 0)
    return pl.pallas_call(
        body, name="norm_mod_bwd1", grid=(t // tr,),
        in_specs=[pl.BlockSpec((tr, d), lambda i: (i, 0)), pl.BlockSpec((tr, d), lambda i: (i, 0)),
                  pl.BlockSpec((tr, d), lat), pl.BlockSpec((1, d), lambda i: (0, 0)),
                  pl.BlockSpec((4, d), lambda i: (0, 0))],
        out_specs=[pl.BlockSpec((tr, d), lat), pl.BlockSpec((8, d), lambda i: (0, 0))],
        out_shape=[_sds((s, d)), _sds((8, d))], compiler_params=_params("arbitrary"),
    )(xcat, dh, dx1, g, mod4)


def _resid_norm_mod_fwd(x, mix, g, mod3, tr):
    s, d = x.shape

    def body(x_ref, mix_ref, g_ref, mod_ref, x1_ref, h_ref):
        x1 = x_ref[...] + mod_ref[0:1, :] * mix_ref[...]
        x1_ref[...] = x1
        n = x1 * _rms_stats(x1) * g_ref[...]
        h_ref[...] = (n * (1.0 + mod_ref[2:3, :]) + mod_ref[1:2, :]).astype(BF16)

    row = pl.BlockSpec((tr, d), lambda i: (i, 0))
    return pl.pallas_call(
        body, name="resid_norm_mod_fwd", grid=(s // tr,),
        in_specs=[row, row, pl.BlockSpec((1, d), lambda i: (0, 0)), pl.BlockSpec((3, d), lambda i: (0, 0))],
        out_specs=[row, row], out_shape=[_sds((s, d)), _sds((s, d), BF16)], compiler_params=_params("parallel"),
    )(x, mix, g, mod3)


def _norm_mod_bwd2(x1, dh2, dx2, mix, g, mod3, tr):
    s, d = x1.shape

    def body(x_ref, dh_ref, dx2_ref, mix_ref, g_ref, mod_ref, dx1_ref, dmix_ref, sums_ref):
        @pl.when(pl.program_id(0) == 0)
        def _():
            sums_ref[...] = jnp.zeros_like(sums_ref)

        x = x_ref[...]
        dh_ = dh_ref[...]
        rstd = _rms_stats(x)
        xhat = x * rstd
        gg = g_ref[...]
        sc = mod_ref[2:3, :]
        dx1 = dx2_ref[...] + _rms_bwd(dh_ * (1.0 + sc) * gg, xhat, rstd)
        dx1_ref[...] = dx1
        dmix_ref[...] = (dx1 * mod_ref[0:1, :]).astype(BF16)
        part = [_colsum(dh_), _colsum(dh_ * xhat * gg), _colsum(dh_ * (1.0 + sc) * xhat), _colsum(dx1 * mix_ref[...])]
        for k, row in enumerate(part):
            sums_ref[k:k + 1, :] += row

    row = pl.BlockSpec((tr, d), lambda i: (i, 0))
    return pl.pallas_call(
        body, name="norm_mod_bwd2", grid=(s // tr,),
        in_specs=[row, row, row, row, pl.BlockSpec((1, d), lambda i: (0, 0)), pl.BlockSpec((3, d), lambda i: (0, 0))],
        out_specs=[row, row, pl.BlockSpec((8, d), lambda i: (0, 0))],
        out_shape=[_sds((s, d)), _sds((s, d), BF16), _sds((8, d))], compiler_params=_params("arbitrary"),
    )(x1, dh2, dx2, mix, g, mod3)


def _final_fwd_bwd(x1, down, target, g_final, gate, tr):
    s, d = x1.shape

    def body(x1_ref, down_ref, t_ref, g_ref, gate_ref, dx2_ref, ddown_ref, sums_ref, loss_ref):
        @pl.when(pl.program_id(0) == 0)
        def _():
            sums_ref[...] = jnp.zeros_like(sums_ref)
            loss_ref[...] = jnp.zeros_like(loss_ref)

        down_ = down_ref[...]
        gate_ = gate_ref[...]
        x2 = x1_ref[...] + gate_ * down_
        rstd = _rms_stats(x2)
        xhat = x2 * rstd
        gg = g_ref[...]
        err = xhat * gg - t_ref[...]
        loss_ref[...] += 0.5 * jnp.sum(jnp.mean(err * err, axis=-1, keepdims=True))
        dy = err * (1.0 / d)
        dx2 = _rms_bwd(dy * gg, xhat, rstd)
        dx2_ref[...] = dx2
        ddown_ref[...] = (dx2 * gate_).astype(BF16)
        sums_ref[0:1, :] += _colsum(dy * xhat)
        sums_ref[1:2, :] += _colsum(dx2 * down_)

    row = pl.BlockSpec((tr, d), lambda i: (i, 0))
    vec = pl.BlockSpec((1, d), lambda i: (0, 0))
    return pl.pallas_call(
        body, name="final_fwd_bwd", grid=(s // tr,), in_specs=[row, row, row, vec, vec],
        out_specs=[row, row, pl.BlockSpec((8, d), lambda i: (0, 0)), pl.BlockSpec((8, LANES), lambda i: (0, 0))],
        out_shape=[_sds((s, d)), _sds((s, d), BF16), _sds((8, d)), _sds((8, LANES))],
        compiler_params=_params("arbitrary"),
    )(x1, down, target, g_final, gate)


def _swap_pairs(v):
    lane = lax.broadcasted_iota(jnp.int32, v.shape, 1)
    return jnp.where(lane % 2 == 0, pltpu.roll(v, LANES - 1, 1), pltpu.roll(v, 1, 1))


def _head_prep_fwd(z, col_off, n_heads, gain, cos, sin, tr, name):
    t = z.shape[0]
    hb = col_off // LANES

    def body(z_ref, g_ref, cos_ref, sin_ref, o_ref):
        x = z_ref[...]
        y = x * _rms_stats(x) * g_ref[...]
        o_ref[...] = (y * cos_ref[...] + _swap_pairs(y) * sin_ref[...]).astype(BF16)

    tab = pl.BlockSpec((tr, LANES), lambda i, j: (i, 0))
    return pl.pallas_call(
        body, name=name, grid=(t // tr, n_heads),
        in_specs=[pl.BlockSpec((tr, LANES), lambda i, j: (i, hb + j)), pl.BlockSpec((1, LANES), lambda i, j: (0, 0)),
                  tab, tab],
        out_specs=pl.BlockSpec((tr, LANES), lambda i, j: (i, j)), out_shape=_sds((t, n_heads * LANES), BF16),
        compiler_params=_params("parallel", "parallel"),
    )(z, gain, cos, sin)


def _head_prep_bwd(z, col_off, n_heads, gain, cos, sin, dout, row_off, tr, name):
    r = dout.shape[0]
    hb = col_off // LANES
    rb = row_off // tr

    def body(z_ref, g_ref, cos_ref, sin_ref, d_ref, dz_ref, dg_ref):
        @pl.when(jnp.logical_and(pl.program_id(0) == 0, pl.program_id(1) == 0))
        def _():
            dg_ref[...] = jnp.zeros_like(dg_ref)

        x = z_ref[...]
        rstd = _rms_stats(x)
        xhat = x * rstd
        dd = d_ref[...]
        dy = dd * cos_ref[...] - _swap_pairs(dd) * sin_ref[...]
        dg_ref[0:1, :] += _colsum(dy * xhat)
        dz_ref[...] = _rms_bwd(dy * g_ref[...], xhat, rstd).astype(BF16)

    tab = pl.BlockSpec((tr, LANES), lambda i, j: (rb + i, 0))
    return pl.pallas_call(
        body, name=name, grid=(r // tr, n_heads),
        in_specs=[pl.BlockSpec((tr, LANES), lambda i, j: (rb + i, hb + j)),
                  pl.BlockSpec((1, LANES), lambda i, j: (0, 0)), tab, tab,
                  pl.BlockSpec((tr, LANES), lambda i, j: (i, j))],
        out_specs=[pl.BlockSpec((tr, LANES), lambda i, j: (i, j)), pl.BlockSpec((8, LANES), lambda i, j: (0, 0))],
        out_shape=[_sds((r, n_heads * LANES), BF16), _sds((8, LANES))],
        compiler_params=_params("arbitrary", "arbitrary"),
    )(z, gain, cos, sin, dout)


def _softmax_rows(q, k, scale):
    s = lax.dot_general(q, k, (((1,), (1,)), ((), ())), preferred_element_type=F32) * scale
    p = jnp.exp(s - jnp.max(s, axis=-1, keepdims=True))
    return p / jnp.sum(p, axis=-1, keepdims=True)


def _attn_fwd(qr, kr, z, v_off, n_ctx_rows, group, tq):
    t, kvw = kr.shape
    s = t - n_ctx_rows
    n_kv = kvw // LANES
    scale = LANES ** -0.5
    qb0 = n_ctx_rows // tq
    vb = v_off // LANES

    def body(q_ref, k_ref, v_ref, o_ref):
        k = k_ref[...]
        v = v_ref[...].astype(BF16)
        for g in range(group):
            cols = slice(g * LANES, (g + 1) * LANES)
            p = _softmax_rows(q_ref[:, cols], k, scale)
            o_ref[:, cols] = jnp.dot(p.astype(BF16), v, preferred_element_type=F32)

    return pl.pallas_call(
        body, name="attn_fwd", grid=(n_kv, s // tq),
        in_specs=[pl.BlockSpec((tq, group * LANES), lambda h, i: (qb0 + i, h)),
                  pl.BlockSpec((t, LANES), lambda h, i: (0, h)), pl.BlockSpec((t, LANES), lambda h, i: (0, vb + h))],
        out_specs=pl.BlockSpec((tq, group * LANES), lambda h, i: (i, h)),
        out_shape=_sds((s, n_kv * group * LANES)), compiler_params=_params("parallel", "parallel"),
    )(qr, kr, z)


def _attn_bwd(qr, kr, z, v_off, d_o, n_ctx_rows, group, tq):
    t, kvw = kr.shape
    s = t - n_ctx_rows
    n_kv = kvw // LANES
    scale = LANES ** -0.5
    qb0 = n_ctx_rows // tq
    vb = v_off // LANES
    tn_dims = (((0,), (0,)), ((), ()))
    nt_dims = (((1,), (1,)), ((), ()))

    def body(q_ref, k_ref, v_ref, do_ref, dq_ref, dk_ref, dv_ref):
        @pl.when(pl.program_id(1) == 0)
        def _():
            dk_ref[...] = jnp.zeros_like(dk_ref)
            dv_ref[...] = jnp.zeros_like(dv_ref)

        k = k_ref[...]
        v = v_ref[...].astype(BF16)
        for g in range(group):
            cols = slice(g * LANES, (g + 1) * LANES)
            q = q_ref[:, cols]
            do_ = do_ref[:, cols]
            p = _softmax_rows(q, k, scale)
            dv_ref[...] += lax.dot_general(p.astype(BF16), do_, tn_dims, preferred_element_type=F32)
            dp = lax.dot_general(do_, v, nt_dims, preferred_element_type=F32)
            ds = (p * (dp - jnp.sum(p * dp, axis=-1, keepdims=True)) * scale).astype(BF16)
            dq_ref[:, cols] = jnp.dot(ds, k, preferred_element_type=F32)
            dk_ref[...] += lax.dot_general(ds, q, tn_dims, preferred_element_type=F32)

    qspec = pl.BlockSpec((tq, group * LANES), lambda h, i: (qb0 + i, h))
    ospec = pl.BlockSpec((tq, group * LANES), lambda h, i: (i, h))
    kspec = pl.BlockSpec((t, LANES), lambda h, i: (0, h))
    return pl.pallas_call(
        body, name="attn_bwd", grid=(n_kv, s // tq),
        in_specs=[qspec, kspec, pl.BlockSpec((t, LANES), lambda h, i: (0, vb + h)), ospec],
        out_specs=[ospec, kspec, kspec],
        out_shape=[_sds((s, n_kv * group * LANES)), _sds((t, kvw)), _sds((t, kvw))],
        compiler_params=_params("parallel", "arbitrary"),
    )(qr, kr, z, d_o)


def _row_mask(shape, rows):
    r = lax.broadcasted_iota(jnp.int32, shape, 0)
    m = r == rows[0]
    for v in rows[1:]:
        m = jnp.logical_or(m, r == v)
    return m


def _shift_rows(x, k, n_ctx_rows):
    t = x.shape[0]
    if k == 0:
        return x
    rolled = pltpu.roll(x, (-k) % t, 0)
    if k > 0:
        dead = [n_ctx_rows - 1 - i for i in range(k)] + [t - 1 - i for i in range(k)]
    else:
        dead = [i for i in range(-k)] + [n_ctx_rows + i for i in range(-k)]
    return jnp.where(_row_mask(x.shape, dead), 0.0, rolled)


def _conv(x, w, b, n_ctx_rows):
    y = b
    for k in range(CONV_WIDTH):
        y = y + _shift_rows(x, k - 1, n_ctx_rows) * w[k:k + 1, :]
    return y


def _gates(xc_bf, w_r, b_r, w_i, b_i, lam):
    r = jax.nn.sigmoid(jnp.dot(xc_bf, w_r.astype(BF16), preferred_element_type=F32) + b_r)
    i = jax.nn.sigmoid(jnp.dot(xc_bf, w_i.astype(BF16), preferred_element_type=F32) + b_i)
    log_a = -LRU_C * r * _softplus(-lam)
    a = jnp.exp(log_a)
    mult = jnp.sqrt(-_expm1_nonpos(2.0 * log_a))
    return r, i, a, mult


def _rnn_specs(t, xr_off):
    xb = xr_off // LANES
    return dict(
        zcol=pl.BlockSpec((t, LANES), lambda j: (0, xb + j)), col=pl.BlockSpec((t, LANES), lambda j: (0, j)),
        conv_w=pl.BlockSpec((CONV_WIDTH, LANES), lambda j: (0, j)), vec=pl.BlockSpec((1, LANES), lambda j: (0, j)),
        gate_w=pl.BlockSpec((2, 1, LANES, LANES), lambda j: (0, j, 0, 0)), two=pl.BlockSpec((2, LANES), lambda j: (0, j)))


def _rnn_prep(z, xr_off, conv_w, conv_b, w_rg, b_rg, w_ig, b_ig, lam, n_ctx_rows):
    t = z.shape[0]
    d = conv_b.shape[1]
    sp = _rnn_specs(t, xr_off)

    def body(z_ref, cw_ref, cb_ref, wr_ref, br_ref, wi_ref, bi_ref, lam_ref, xc_ref, af_ref, bf_ref, ab_ref, bb_ref):
        xc = _conv(z_ref[...], cw_ref[...], cb_ref[...], n_ctx_rows)
        xc_ref[...] = xc
        xc_bf = xc.astype(BF16)
        for dr, (a_ref, b_ref) in enumerate(((af_ref, bf_ref), (ab_ref, bb_ref))):
            _, i, a, mult = _gates(xc_bf, wr_ref[dr, 0], br_ref[dr:dr + 1, :], wi_ref[dr, 0], bi_ref[dr:dr + 1, :],
                                   lam_ref[dr:dr + 1, :])
            a_ref[...] = a
            b_ref[...] = mult * (i * xc)

    return pl.pallas_call(
        body, name="rnn_prep", grid=(d // LANES,),
        in_specs=[sp["zcol"], sp["conv_w"], sp["vec"], sp["gate_w"], sp["two"], sp["gate_w"], sp["two"], sp["two"]],
        out_specs=[sp["col"]] * 5, out_shape=[_sds((t, d))] * 5, compiler_params=_params("parallel"),
    )(z, conv_w, conv_b, w_rg, b_rg, w_ig, b_ig, lam)


def _scan(a, b, *, order, post, n_ctx_rows, name, tc=256):
    t, d = a.shape
    tc = _tile(math.gcd(n_ctx_rows, t - n_ctx_rows), tc, SUBLANES)
    nt, nctx = t // tc, n_ctx_rows // tc
    nlat = nt - nctx
    b_lat_only = b.shape[0] != t
    up = order.endswith("up")

    def chunk(i):
        if order == "ctx_lat_up":
            return i
        if order == "lat_ctx_down":
            return nt - 1 - i
        if order == "ctx_lat_down":
            return jnp.where(i < nctx, nctx - 1 - i, nt - 1 - (i - nctx))
        return jnp.where(i < nlat, nctx + i, i - nlat)

    def body(a_ref, b_ref, o_ref, carry_ref):
        @pl.when(pl.program_id(0) == 0)
        def _():
            carry_ref[...] = jnp.zeros_like(carry_ref)

        live = jnp.where(chunk(pl.program_id(0)) >= nctx, 1.0, 0.0) if b_lat_only else None

        def group(gi, carry):
            base = pl.multiple_of((gi if up else tc // SUBLANES - 1 - gi) * SUBLANES, SUBLANES)
            for r in (range(SUBLANES) if up else range(SUBLANES - 1, -1, -1)):
                a_r = a_ref[pl.ds(base + r, 1), :]
                b_r = b_ref[pl.ds(base + r, 1), :]
                if live is not None:
                    b_r = b_r * live
                if post:
                    out = b_r + carry
                    carry = a_r * out
                else:
                    out = a_r * carry + b_r
                    carry = out
                o_ref[pl.ds(base + r, 1), :] = out
            return carry

        carry_ref[0:1, :] = lax.fori_loop(0, tc // SUBLANES, group, carry_ref[0:1, :])

    full = pl.BlockSpec((tc, d), lambda i: (chunk(i), 0))
    b_spec = pl.BlockSpec((tc, d), lambda i: (jnp.maximum(chunk(i) - nctx, 0), 0)) if b_lat_only else full
    return pl.pallas_call(
        body, name=name, grid=(nt,), in_specs=[full, b_spec], out_specs=full, out_shape=_sds((t, d)),
        scratch_shapes=[pltpu.VMEM((SUBLANES, d), F32)], compiler_params=_params("arbitrary"),
    )(a, b)


def _rnn_bwd(z, xr_off, xc, g_f, g_b, h_f, h_b, conv_w, w_rg, b_rg, w_ig, b_ig, lam, n_ctx_rows):
    t, d = xc.shape
    sp = _rnn_specs(t, xr_off)
    tn_dims = (((0,), (0,)), ((), ()))
    nt_dims = (((1,), (1,)), ((), ()))

    def body(z_ref, xc_ref, gf_ref, gb_ref, hf_ref, hb_ref, cw_ref, wr_ref, br_ref, wi_ref, bi_ref, lam_ref,
             dxr_ref, dwr_ref, dwi_ref, sums_ref):
        xc_ = xc_ref[...]
        xc_bf = xc_.astype(BF16)
        dxc = jnp.zeros_like(xc_)
        sums = [None] * 6
        for dr, (g_ref, h_ref) in enumerate(((gf_ref, hf_ref), (gb_ref, hb_ref))):
            w_r, w_i, lam_ = wr_ref[dr, 0], wi_ref[dr, 0], lam_ref[dr:dr + 1, :]
            r, i, a, mult = _gates(xc_bf, w_r, br_ref[dr:dr + 1, :], w_i, bi_ref[dr:dr + 1, :], lam_)
            g = g_ref[...]
            h = h_ref[...]
            if dr == 0:
                h_prev = jnp.where(_row_mask(h.shape, [0]), 0.0, pltpu.roll(h, 1, 0))
            else:
                h_prev = jnp.where(_row_mask(h.shape, [n_ctx_rows - 1]), 0.0, pltpu.roll(h, t - 1, 0))
            d_mult = g * i * xc_
            d_i = g * mult * xc_
            dxc = dxc + g * mult * i
            d_log_a = g * h_prev * a - d_mult * a * a / mult
            sp_ = _softplus(-lam_)
            d_r = d_log_a * (-LRU_C) * sp_
            d_sp = _colsum(d_log_a * (-LRU_C) * r)
            du_r = (d_r * r * (1.0 - r))
            du_i = (d_i * i * (1.0 - i))
            sums[dr] = _colsum(du_r)
            sums[2 + dr] = _colsum(du_i)
            sums[4 + dr] = d_sp * (-jax.nn.sigmoid(-lam_))
            du_r_bf, du_i_bf = du_r.astype(BF16), du_i.astype(BF16)
            dwr_ref[dr, 0] = lax.dot_general(xc_bf, du_r_bf, tn_dims, preferred_element_type=F32)
            dwi_ref[dr, 0] = lax.dot_general(xc_bf, du_i_bf, tn_dims, preferred_element_type=F32)
            dxc = dxc + lax.dot_general(du_r_bf, w_r.astype(BF16), nt_dims, preferred_element_type=F32)
            dxc = dxc + lax.dot_general(du_i_bf, w_i.astype(BF16), nt_dims, preferred_element_type=F32)
        xr = z_ref[...]
        cw = cw_ref[...]
        dxr = jnp.zeros_like(dxc)
        rows = list(sums)
        for k in range(CONV_WIDTH):
            dxr = dxr + _shift_rows(dxc, 1 - k, n_ctx_rows) * cw[k:k + 1, :]
            rows.append(_colsum(dxc * _shift_rows(xr, k - 1, n_ctx_rows)))
        rows.append(_colsum(dxc))
        dxr_ref[...] = dxr.astype(BF16)
        sums_ref[...] = jnp.zeros_like(sums_ref)
        for k, row in enumerate(rows):
            sums_ref[k:k + 1, :] = row

    return pl.pallas_call(
        body, name="rnn_bwd", grid=(d // LANES,),
        in_specs=[sp["zcol"]] + [sp["col"]] * 5 + [sp["conv_w"], sp["gate_w"], sp["two"], sp["gate_w"], sp["two"],
                                                  sp["two"]],
        out_specs=[sp["col"], sp["gate_w"], sp["gate_w"], pl.BlockSpec((16, LANES), lambda j: (0, j))],
        out_shape=[_sds((t, d), BF16), _sds(w_rg.shape), _sds(w_ig.shape), _sds((16, d))],
        compiler_params=_params("parallel"),
    )(z, xc, g_f, g_b, h_f, h_b, conv_w, w_rg, b_rg, w_ig, b_ig, lam)


def _tiles2d(s, d, tr, tcol):
    return (s // tr, d // tcol), pl.BlockSpec((tr, tcol), lambda i, j: (i, j))


def _zspec(tr, tcol, row_off, col_off):
    rb, cb = row_off // tr, col_off // tcol
    return pl.BlockSpec((tr, tcol), lambda i, j: (rb + i, cb + j))


def _rnn_gate_fwd(h_f, h_b, z, xg_off, n_ctx_rows, tr, tcol):
    t, d = h_f.shape
    s = t - n_ctx_rows
    grid, out = _tiles2d(s, d, tr, tcol)
    hs = _zspec(tr, tcol, n_ctx_rows, 0)

    def body(hf_ref, hb_ref, xg_ref, u_ref):
        u_ref[...] = ((hf_ref[...] + hb_ref[...]) * _gelu(xg_ref[...])).astype(BF16)

    return pl.pallas_call(body, name="rnn_gate_fwd", grid=grid, in_specs=[hs, hs, _zspec(tr, tcol, n_ctx_rows, xg_off)],
                          out_specs=out, out_shape=_sds((s, d), BF16), compiler_params=_params("parallel", "parallel"),
                          )(h_f, h_b, z)


def _rnn_gate_bwd(d_u, h_f, h_b, z, xg_off, n_ctx_rows, tr, tcol):
    t, d = h_f.shape
    s = t - n_ctx_rows
    grid, out = _tiles2d(s, d, tr, tcol)
    hs = _zspec(tr, tcol, n_ctx_rows, 0)

    def body(du_ref, hf_ref, hb_ref, xg_ref, dr_ref, dxg_ref):
        du = du_ref[...]
        xg = xg_ref[...]
        dr_ref[...] = du * _gelu(xg)
        dxg_ref[...] = (du * (hf_ref[...] + hb_ref[...]) * _gelu_grad(xg)).astype(BF16)

    return pl.pallas_call(body, name="rnn_gate_bwd", grid=grid,
                          in_specs=[out, hs, hs, _zspec(tr, tcol, n_ctx_rows, xg_off)], out_specs=[out, out],
                          out_shape=[_sds((s, d)), _sds((s, d), BF16)],
                          compiler_params=_params("parallel", "parallel"))(d_u, h_f, h_b, z)


def _merge_fwd(y_attn, y_rnn, z, gl_off, n_ctx_rows, tr, tcol):
    s, d = y_attn.shape
    grid, out = _tiles2d(s, d, tr, tcol)

    def body(ya_ref, yr_ref, ga_ref, gr_ref, o_ref):
        o_ref[...] = (jax.nn.sigmoid(ga_ref[...]) * ya_ref[...] + jax.nn.sigmoid(gr_ref[...]) * yr_ref[...]).astype(BF16)

    return pl.pallas_call(
        body, name="merge_fwd", grid=grid,
        in_specs=[out, out, _zspec(tr, tcol, n_ctx_rows, gl_off), _zspec(tr, tcol, n_ctx_rows, gl_off + d)],
        out_specs=out, out_shape=_sds((s, d), BF16), compiler_params=_params("parallel", "parallel"),
    )(y_attn, y_rnn, z, z)


def _merge_bwd(d_mrg, y_attn, y_rnn, z, gl_off, n_ctx_rows, tr, tcol):
    s, d = y_attn.shape
    grid, out = _tiles2d(s, d, tr, tcol)

    def body(dm_ref, ya_ref, yr_ref, ga_ref, gr_ref, dya_ref, dyr_ref, dga_ref, dgr_ref):
        dm = dm_ref[...]
        ga = jax.nn.sigmoid(ga_ref[...])
        gr = jax.nn.sigmoid(gr_ref[...])
        dya_ref[...] = (dm * ga).astype(BF16)
        dyr_ref[...] = (dm * gr).astype(BF16)
        dga_ref[...] = (dm * ya_ref[...] * ga * (1.0 - ga)).astype(BF16)
        dgr_ref[...] = (dm * yr_ref[...] * gr * (1.0 - gr)).astype(BF16)

    return pl.pallas_call(
        body, name="merge_bwd", grid=grid,
        in_specs=[out, out, out, _zspec(tr, tcol, n_ctx_rows, gl_off), _zspec(tr, tcol, n_ctx_rows, gl_off + d)],
        out_specs=[out] * 4, out_shape=[_sds((s, d), BF16)] * 4, compiler_params=_params("parallel", "parallel"),
    )(d_mrg, y_attn, y_rnn, z, z)


def _sq_relu(up, tr, tcol):
    grid, out = _tiles2d(*up.shape, tr, tcol)

    def body(u_ref, o_ref):
        r = jnp.maximum(u_ref[...], 0.0)
        o_ref[...] = (r * r).astype(BF16)

    return pl.pallas_call(body, name="sq_relu", grid=grid, in_specs=[out], out_specs=out,
                          out_shape=_sds(up.shape, BF16), compiler_params=_params("parallel", "parallel"))(up)


def _sq_relu_bwd(d_act, up, tr, tcol):
    grid, out = _tiles2d(*up.shape, tr, tcol)

    def body(d_ref, u_ref, o_ref):
        o_ref[...] = (d_ref[...] * 2.0 * jnp.maximum(u_ref[...], 0.0)).astype(BF16)

    return pl.pallas_call(body, name="sq_relu_bwd", grid=grid, in_specs=[out, out], out_specs=out,
                          out_shape=_sds(up.shape, BF16), compiler_params=_params("parallel", "parallel"))(d_act, up)


def _cast_bf16(w, name):
    grid, out = _tiles2d(*w.shape, _tile(w.shape[0], 512, 16), _tile(w.shape[1], 1024, LANES))

    def body(w_ref, o_ref):
        o_ref[...] = w_ref[...].astype(BF16)

    return pl.pallas_call(body, name=name, grid=grid, in_specs=[out], out_specs=out, out_shape=_sds(w.shape, BF16),
                          compiler_params=_params("parallel", "parallel"))(w)


def _sum_leading(parts, name):
    n, r, c = parts.shape
    tr, tcol = _tile(r, 512, SUBLANES), _tile(c, 1024, LANES)

    def body(p_ref, o_ref):
        tot = p_ref[0]
        for k in range(1, n):
            tot = tot + p_ref[k]
        o_ref[...] = tot

    return pl.pallas_call(
        body, name=name, grid=(r // tr, c // tcol), in_specs=[pl.BlockSpec((n, tr, tcol), lambda i, j: (0, i, j))],
        out_specs=pl.BlockSpec((tr, tcol), lambda i, j: (i, j)), out_shape=_sds((r, c)),
        compiler_params=_params("parallel", "parallel"),
    )(parts)


def _add_half(full, other, core, split_rows, name):
    r, c = other.shape
    tr, tcol = _tile(r, 512, SUBLANES), _tile(c, 1024, LANES)
    nrb, ncb = r // tr, c // tcol

    def body(core_ref, f_ref, o_ref, out_ref):
        out_ref[...] = f_ref[...] + o_ref[...]

    if split_rows:
        fmap = lambda i, j, core_ref: (core_ref[0] * nrb + i, j)
    else:
        fmap = lambda i, j, core_ref: (i, core_ref[0] * ncb + j)
    same = lambda i, j, core_ref: (i, j)
    return pl.pallas_call(
        body, name=name,
        grid_spec=pltpu.PrefetchScalarGridSpec(
            num_scalar_prefetch=1, grid=(nrb, ncb),
            in_specs=[pl.BlockSpec((tr, tcol), fmap), pl.BlockSpec((tr, tcol), same)],
            out_specs=pl.BlockSpec((tr, tcol), same)),
        out_shape=_sds((r, c)), compiler_params=_params("parallel", "parallel"),
    )(core, full, other)


def _adamw(w, g, m, v, name):
    r, c = w.shape
    tr, tcol = _tile(r, 512, SUBLANES), _tile(c, 1024, LANES)
    blk = pl.BlockSpec((tr, tcol), lambda i, j: (i, j))

    def body(w_ref, g_ref, m_ref, v_ref, d_ref, nm_ref, nv_ref):
        g_ = g_ref[...]
        m_ = ADAM_B1 * m_ref[...] + (1.0 - ADAM_B1) * g_
        v_ = ADAM_B2 * v_ref[...] + (1.0 - ADAM_B2) * (g_ * g_)
        m_hat = m_ / (1.0 - ADAM_B1 ** ADAM_STEP)
        v_hat = v_ / (1.0 - ADAM_B2 ** ADAM_STEP)
        d_ref[...] = -ADAM_LR * (m_hat / (jnp.sqrt(v_hat) + ADAM_EPS) + ADAM_WD * w_ref[...])
        nm_ref[...] = m_
        nv_ref[...] = v_

    return pl.pallas_call(body, name=name, grid=(r // tr, c // tcol), in_specs=[blk] * 4, out_specs=[blk] * 3,
                          out_shape=[_sds((r, c))] * 3, compiler_params=_params("parallel", "parallel"))(w, g, m, v)


def _place():
    x, y, c = lax.axis_index("x"), lax.axis_index("y"), lax.axis_index("c")
    chips = [(1 - x, y), (x, 1 - y), (1 - x, 1 - y)]
    return x, y, c, chips


def _all_gather8(blk, name):
    m, n = blk.shape

    def body(x_ref, out_ref, send_sems, recv_sems, local_sem):
        x, y, c, chips = _place()
        me, sibling = (x, y, c), (x, y, 1 - c)

        def rows(px, py, pc):
            return out_ref.at[pl.ds((4 * px + 2 * py + pc) * m, m), :]

        def copy(k, block, to, src=None):
            return pltpu.make_async_remote_copy(
                src_ref=rows(*block) if src is None else src, dst_ref=rows(*block), send_sem=send_sems.at[k],
                recv_sem=recv_sems.at[k], device_id=to, device_id_type=MESH_ID)

        mine = pltpu.make_async_copy(x_ref, rows(*me), local_sem)
        mine.start()
        first = [copy(0, me, sibling, src=x_ref)]
        first += [copy(1 + j, me, (*chip, c), src=x_ref) for j, chip in enumerate(chips)]
        for cp in first:
            cp.start()
        passed = [copy(4 + j, (*chip, c), sibling) for j, chip in enumerate(chips)]
        for j, chip in enumerate(chips):
            copy(1 + j, (*chip, c), me).wait_recv()
            passed[j].start()
        copy(0, sibling, me).wait_recv()
        for j, chip in enumerate(chips):
            copy(4 + j, (*chip, 1 - c), me).wait_recv()
        for cp in first + passed:
            cp.wait_send()
        mine.wait()

    return pl.pallas_call(
        body, name=name, out_shape=_sds((N_DEV * m, n), blk.dtype), in_specs=[ANY], out_specs=ANY,
        scratch_shapes=[pltpu.SemaphoreType.DMA((7,)), pltpu.SemaphoreType.DMA((7,)), pltpu.SemaphoreType.DMA],
    )(blk)


def _half(ref, core, split_rows):
    r, c = ref.shape
    if split_rows:
        return ref.at[pl.ds(core * (r // 2), r // 2), :]
    return ref.at[:, pl.ds(core * (c // 2), c // 2)]


def _gather_weights(shards, col_sharded):
    nw = len(shards)
    out_shapes = [_sds((s.shape[0], s.shape[1] * N_CHIPS) if col else (s.shape[0] * N_CHIPS, s.shape[1]), s.dtype)
                  for s, col in zip(shards, col_sharded)]

    def body(*refs):
        s_refs, o_refs = refs[:nw], refs[nw:2 * nw]
        send_sems, recv_sems, local_sems = refs[2 * nw:]
        x, y, c, chips = _place()
        jme = 2 * x + y

        def window(w, j, half):
            kr, nc = shards[w].shape
            if col_sharded[w]:
                full = o_refs[w].at[:, pl.ds(j * nc, nc)]
            else:
                full = o_refs[w].at[pl.ds(j * kr, kr), :]
            return full if half is None else _half(full, half, True)

        def remote(k, src, dst, to):
            return pltpu.make_async_remote_copy(src_ref=src, dst_ref=dst, send_sem=send_sems.at[k],
                                                recv_sem=recv_sems.at[k], device_id=to, device_id_type=MESH_ID)

        local = [pltpu.make_async_copy(s_refs[w], window(w, jme, None), local_sems.at[w]) for w in range(nw)]
        for cp in local:
            cp.start()
        sends = []
        for w in range(nw):
            for k, chip in enumerate(chips):
                sends.append(remote(6 * w + k, _half(s_refs[w], c, True), window(w, jme, c), (*chip, c)))
                sends[-1].start()
        for w in range(nw):
            for k, (cx, cy) in enumerate(chips):
                got = window(w, 2 * cx + cy, c)
                remote(6 * w + k, got, got, (cx, cy, c)).wait_recv()
                sends.append(remote(6 * w + 3 + k, got, got, (x, y, 1 - c)))
                sends[-1].start()
        for w in range(nw):
            for k, (cx, cy) in enumerate(chips):
                got = window(w, 2 * cx + cy, 1 - c)
                remote(6 * w + 3 + k, got, got, (x, y, 1 - c)).wait_recv()
        for cp in sends:
            cp.wait_send()
        for cp in local:
            cp.wait()

    return pl.pallas_call(
        body, name="gather_weights", out_shape=out_shapes, in_specs=[ANY] * nw, out_specs=[ANY] * nw,
        scratch_shapes=[pltpu.SemaphoreType.DMA((6 * nw,)), pltpu.SemaphoreType.DMA((6 * nw,)),
                        pltpu.SemaphoreType.DMA((nw,))],
    )(*shards)


def _swap_halves(grads, col_sharded):
    nw = len(grads)
    out_shapes = [_sds((g.shape[0] // 2, g.shape[1]) if col else (g.shape[0], g.shape[1] // 2))
                  for g, col in zip(grads, col_sharded)]

    def body(*refs):
        g_refs, o_refs = refs[:nw], refs[nw:2 * nw]
        send_sems, recv_sems = refs[2 * nw:]
        x, y, c, _ = _place()
        copies = [pltpu.make_async_remote_copy(
            src_ref=_half(g_refs[w], 1 - c, col_sharded[w]), dst_ref=o_refs[w], send_sem=send_sems.at[w],
            recv_sem=recv_sems.at[w], device_id=(x, y, 1 - c), device_id_type=MESH_ID) for w in range(nw)]
        for cp in copies:
            cp.start()
        for cp in copies:
            cp.wait()

    return pl.pallas_call(
        body, name="swap_halves", out_shape=out_shapes, in_specs=[ANY] * nw, out_specs=[ANY] * nw,
        scratch_shapes=[pltpu.SemaphoreType.DMA((nw,)), pltpu.SemaphoreType.DMA((nw,))],
    )(*grads)


def _scatter_regions(pairs, col_sharded):
    nw = len(pairs)

    def region_shape(p, col):
        return (p.shape[0], p.shape[1] // N_CHIPS) if col else (p.shape[0] // N_CHIPS, p.shape[1])

    out_shapes = [_sds((N_CHIPS, *region_shape(p, col))) for p, col in zip(pairs, col_sharded)]

    def body(*refs):
        p_refs, o_refs = refs[:nw], refs[nw:2 * nw]
        send_sems, recv_sems, local_sems = refs[2 * nw:]
        x, y, c, chips = _place()
        jme = 2 * x + y

        def region(w, j):
            rr, rc = region_shape(pairs[w], col_sharded[w])
            if col_sharded[w]:
                return p_refs[w].at[:, pl.ds(j * rc, rc)]
            return p_refs[w].at[pl.ds(j * rr, rr), :]

        local = [pltpu.make_async_copy(region(w, jme), o_refs[w].at[jme], local_sems.at[w]) for w in range(nw)]
        for cp in local:
            cp.start()
        sends = []
        for w in range(nw):
            for k, (cx, cy) in enumerate(chips):
                sends.append(pltpu.make_async_remote_copy(
                    src_ref=region(w, 2 * cx + cy), dst_ref=o_refs[w].at[jme], send_sem=send_sems.at[3 * w + k],
                    recv_sem=recv_sems.at[3 * w + k], device_id=(cx, cy, c), device_id_type=MESH_ID))
                sends[-1].start()
        for w in range(nw):
            for k, (cx, cy) in enumerate(chips):
                slot = o_refs[w].at[2 * cx + cy]
                pltpu.make_async_remote_copy(
                    src_ref=slot, dst_ref=slot, send_sem=send_sems.at[3 * w + k], recv_sem=recv_sems.at[3 * w + k],
                    device_id=(cx, cy, c), device_id_type=MESH_ID).wait_recv()
        for cp in sends:
            cp.wait_send()
        for cp in local:
            cp.wait()

    return pl.pallas_call(
        body, name="scatter_regions", out_shape=out_shapes, in_specs=[ANY] * nw, out_specs=[ANY] * nw,
        scratch_shapes=[pltpu.SemaphoreType.DMA((3 * nw,)), pltpu.SemaphoreType.DMA((3 * nw,)),
                        pltpu.SemaphoreType.DMA((nw,))],
    )(*pairs)


def _join_halves(halves, col_sharded):
    nw = len(halves)
    out_shapes = [_sds((h.shape[0] * 2, h.shape[1]) if col else (h.shape[0], h.shape[1] * 2))
                  for h, col in zip(halves, col_sharded)]

    def body(*refs):
        h_refs, o_refs = refs[:nw], refs[nw:2 * nw]
        send_sems, recv_sems, local_sems = refs[2 * nw:]
        x, y, c, _ = _place()
        local, remote = [], []
        for w in range(nw):
            mine = _half(o_refs[w], c, col_sharded[w])
            local.append(pltpu.make_async_copy(h_refs[w], mine, local_sems.at[w]))
            remote.append(pltpu.make_async_remote_copy(
                src_ref=h_refs[w], dst_ref=mine, send_sem=send_sems.at[w], recv_sem=recv_sems.at[w],
                device_id=(x, y, 1 - c), device_id_type=MESH_ID))
        for cp in local + remote:
            cp.start()
        for w in range(nw):
            theirs = _half(o_refs[w], 1 - c, col_sharded[w])
            pltpu.make_async_remote_copy(
                src_ref=theirs, dst_ref=theirs, send_sem=send_sems.at[w], recv_sem=recv_sems.at[w],
                device_id=(x, y, 1 - c), device_id_type=MESH_ID).wait_recv()
        for cp in remote:
            cp.wait_send()
        for cp in local:
            cp.wait()

    return pl.pallas_call(
        body, name="join_halves", out_shape=out_shapes, in_specs=[ANY] * nw, out_specs=[ANY] * nw,
        scratch_shapes=[pltpu.SemaphoreType.DMA((nw,)), pltpu.SemaphoreType.DMA((nw,)),
                        pltpu.SemaphoreType.DMA((nw,))],
    )(*halves)


def _pack(arrays, pad_rows_to=SUBLANES):
    flat = [a.reshape(-1, LANES) if a.size % LANES == 0 else
            jnp.pad(a.reshape(-1), (0, LANES - a.size % LANES)).reshape(-1, LANES) for a in arrays]
    rows = sum(f.shape[0] for f in flat)
    pad = (-rows) % pad_rows_to
    if pad:
        flat.append(jnp.zeros((pad, LANES), F32))
    return jnp.concatenate(flat, axis=0)


def _unpack(packed, shapes):
    out, r = [], 0
    for shp in shapes:
        size = math.prod(shp)
        nr = -(-size // LANES)
        out.append(packed[r:r + nr].reshape(-1)[:size].reshape(shp))
        r += nr
    return out


def _rope_tables(n_ctx_rows, s):
    rows = s // GRID_W
    row_idx = jnp.repeat(jnp.arange(rows), GRID_W)
    col_idx = jnp.tile(jnp.arange(GRID_W), rows)
    n_freq = LANES // 4
    inv_freq = ROPE_THETA ** (-jnp.arange(n_freq, dtype=F32) / n_freq)
    ang = jnp.concatenate([row_idx.astype(F32)[:, None] * inv_freq, col_idx.astype(F32)[:, None] * inv_freq], axis=-1)
    cos = jnp.repeat(jnp.cos(ang), 2, axis=-1)
    sin = jnp.repeat(jnp.sin(ang), 2, axis=-1) * jnp.tile(jnp.array([-1.0, 1.0], F32), LANES // 2)
    cos = jnp.concatenate([jnp.ones((n_ctx_rows, LANES), F32), cos], axis=0)
    sin = jnp.concatenate([jnp.zeros((n_ctx_rows, LANES), F32), sin], axis=0)
    return cos, sin


WEIGHT_NAMES = ['c_ctx', 'w_mod', 'b_mod', 'g_mix', 'g_mlp', 'w_in', 'q_gain', 'k_gain', 'conv_w', 'conv_b', 'w_rg',
                'b_rg', 'w_ig', 'b_ig', 'lru_lambda', 'w_o_attn', 'w_o_rnn', 'w_out', 'w_up', 'w_down', 'g_final']
BIG = ['w_in', 'w_o_attn', 'w_o_rnn', 'w_out', 'w_up', 'w_down']
BIG_COL_SHARDED = [True, False, False, False, True, False]
SMALL = ['c_ctx', 'b_mod', 'g_mix', 'g_mlp', 'q_gain', 'k_gain', 'conv_b', 'w_rg', 'w_ig', 'g_final',
         'conv_w', 'b_rg', 'b_ig', 'lru_lambda']


def kernel(x, c, ctx, c_ctx, w_mod, b_mod, g_mix, g_mlp, w_in, q_gain, k_gain, conv_w, conv_b, w_rg, b_rg, w_ig, b_ig, lru_lambda, w_o_attn, w_o_rnn, w_out, w_up, w_down, g_final, loss_target, m_c_ctx, m_w_mod, m_b_mod, m_g_mix, m_g_mlp, m_w_in, m_q_gain, m_k_gain, m_conv_w, m_conv_b, m_w_rg, m_b_rg, m_w_ig, m_b_ig, m_lru_lambda, m_w_o_attn, m_w_o_rnn, m_w_out, m_w_up, m_w_down, m_g_final, v_c_ctx, v_w_mod, v_b_mod, v_g_mix, v_g_mlp, v_w_in, v_q_gain, v_k_gain, v_conv_w, v_conv_b, v_w_rg, v_b_rg, v_w_ig, v_b_ig, v_lru_lambda, v_w_o_attn, v_w_o_rnn, v_w_out, v_w_up, v_w_down, v_g_final):
    given = dict(locals())
    weights = {n: given[n] for n in WEIGHT_NAMES}
    moms = {n: given["m_" + n] for n in WEIGHT_NAMES}
    vars_ = {n: given["v_" + n] for n in WEIGHT_NAMES}

    s, d = x.shape[1], x.shape[2]
    n_ctx = ctx.shape[1]
    t = n_ctx + s
    hd = q_gain.shape[1]
    assert hd == LANES and w_rg.shape[-1] == LANES
    attn_w = w_o_attn.shape[1] * N_CHIPS
    n_in = w_in.shape[2] * N_CHIPS
    kv_w = (n_in - attn_w - 4 * d) // 2
    group = attn_w // kv_w
    k_off, v_off, xr_off = attn_w, attn_w + kv_w, attn_w + 2 * kv_w
    xg_off, gl_off = xr_off + d, xr_off + 2 * d
    d_mod = N_MOD * d
    tr = _tile(math.gcd(n_ctx, s), 256, 16)
    tcol = _tile(math.gcd(d, xr_off), 512, LANES)
    xi, yi, ci = lax.axis_index("x"), lax.axis_index("y"), lax.axis_index("c")
    chip = 2 * xi + yi
    core = ci.astype(jnp.int32).reshape(1)

    sharded_small = [conv_w[0], b_rg[0], b_ig[0], lru_lambda[0]]
    pack0 = _pack([c[0]] + sharded_small)
    got0 = _all_gather8(pack0, "gather_small_inputs").reshape(N_DEV, -1, LANES)
    c_all = got0[:, :d // LANES].reshape(N_DEV, d)
    per_chip = [_unpack(got0[2 * j, d // LANES:], [a.shape for a in sharded_small]) for j in range(N_CHIPS)]
    conv_w_f, b_rg_f, b_ig_f, lam_f = (jnp.concatenate([per_chip[j][i] for j in range(N_CHIPS)], axis=-1)
                                       for i in range(4))
    c16 = jnp.concatenate([c_all, c_ctx[None, :], jnp.zeros((16 - N_DEV - 1, d), F32)], axis=0)
    b_mod_shard = lax.dynamic_slice(b_mod, (0, chip * (d_mod // N_CHIPS)), (1, d_mod // N_CHIPS))
    mod_part, silu16 = _mod_fwd(c16, w_mod[0], b_mod_shard)
    mod_all = _all_gather8(mod_part, "gather_mod").reshape(N_DEV, 16, d_mod // N_CHIPS)
    mod16 = jnp.concatenate([mod_all[2 * j] for j in range(N_CHIPS)], axis=-1)
    me = 4 * xi + 2 * yi + ci
    mod_lat = lax.dynamic_slice(mod16, (me, 0), (1, d_mod)).reshape(N_MOD, d)
    mod_ctx = mod16[N_DEV].reshape(N_MOD, d)
    mod4 = jnp.stack([mod_ctx[0], mod_ctx[1], mod_lat[0], mod_lat[1]])
    mod3 = jnp.stack([mod_lat[2], mod_lat[3], mod_lat[4]])
    gate_f = mod_lat[5][None, :]

    shards_bf = [_cast_bf16(weights[n][0], "cast_" + n) for n in BIG]
    w_in_f, w_oa_f, w_or_f, w_out_f, w_up_f, w_down_f = _gather_weights(shards_bf, BIG_COL_SHARDED)

    xcat = jnp.concatenate([ctx[0], x[0]], axis=0)
    cos, sin = _rope_tables(n_ctx, s)
    h = _norm_mod_fwd(xcat, g_mix, mod4, n_ctx, tr)
    z = _matmul(h, w_in_f, name="mm_in")
    qr = _head_prep_fwd(z, 0, attn_w // LANES, q_gain, cos, sin, tr, "q_prep")
    kr = _head_prep_fwd(z, k_off, kv_w // LANES, k_gain, cos, sin, tr, "k_prep")
    attn_o = _attn_fwd(qr, kr, z, v_off, n_ctx, group, tr)
    xc, a_f, bx_f, a_b, bx_b = _rnn_prep(z, xr_off, conv_w_f, conv_b, w_rg[0], b_rg_f, w_ig[0], b_ig_f, lam_f, n_ctx)
    h_f = _scan(a_f, bx_f, order="ctx_lat_up", post=False, n_ctx_rows=n_ctx, name="scan_f")
    h_b = _scan(a_b, bx_b, order="ctx_lat_down", post=False, n_ctx_rows=n_ctx, name="scan_b")
    u = _rnn_gate_fwd(h_f, h_b, z, xg_off, n_ctx, tr, tcol)
    y_attn = _matmul(attn_o, w_oa_f, name="mm_o_attn")
    y_rnn = _matmul(u, w_or_f, name="mm_o_rnn")
    mrg = _merge_fwd(y_attn, y_rnn, z, gl_off, n_ctx, tr, tcol)
    mix = _matmul(mrg, w_out_f, name="mm_out")
    x1, h2 = _resid_norm_mod_fwd(x[0], mix, g_mlp, mod3, tr)
    up = _matmul(h2, w_up_f, name="mm_up")
    act = _sq_relu(up, tr, tcol)
    down = _matmul(act, w_down_f, name="mm_down")
    dx2, d_down, sums_fin, loss_blk = _final_fwd_bwd(x1, down, loss_target[0], g_final[None, :], gate_f, tr)

    d_act = _matmul(d_down, w_down_f, tb=True, name="mm_d_act")
    g_w_down = _matmul(act, d_down, ta=True, name="mm_g_down")
    d_up = _sq_relu_bwd(d_act, up, tr, tcol)
    dh2 = _matmul(d_up, w_up_f, tb=True, name="mm_d_h2")
    g_w_up = _matmul(h2, d_up, ta=True, name="mm_g_up")
    dx1, d_mix, sums2 = _norm_mod_bwd2(x1, dh2, dx2, mix, g_mlp, mod3, tr)
    d_mrg = _matmul(d_mix, w_out_f, tb=True, name="mm_d_mrg")
    g_w_out = _matmul(mrg, d_mix, ta=True, name="mm_g_out")
    d_ya, d_yr, d_gla, d_glr = _merge_bwd(d_mrg, y_attn, y_rnn, z, gl_off, n_ctx, tr, tcol)
    d_o = _matmul(d_ya, w_oa_f, tb=True, out_dtype=BF16, name="mm_d_o")
    g_w_oa = _matmul(attn_o, d_ya, ta=True, name="mm_g_o_attn")
    d_u = _matmul(d_yr, w_or_f, tb=True, name="mm_d_u")
    g_w_or = _matmul(u, d_yr, ta=True, name="mm_g_o_rnn")
    d_rnn, d_xg = _rnn_gate_bwd(d_u, h_f, h_b, z, xg_off, n_ctx, tr, tcol)
    gs_f = _scan(a_f, d_rnn, order="lat_ctx_down", post=True, n_ctx_rows=n_ctx, name="scan_f_bwd")
    gs_b = _scan(a_b, d_rnn, order="lat_ctx_up", post=True, n_ctx_rows=n_ctx, name="scan_b_bwd")
    d_xr, g_w_rg, g_w_ig, sums_rnn = _rnn_bwd(z, xr_off, xc, gs_f, gs_b, h_f, h_b, conv_w_f, w_rg[0], b_rg_f, w_ig[0],
                                               b_ig_f, lam_f, n_ctx)
    dq, dk, dv = _attn_bwd(qr, kr, z, v_off, d_o, n_ctx, group, tr)
    d_q_raw, g_q_gain = _head_prep_bwd(z, 0, attn_w // LANES, q_gain, cos, sin, dq, n_ctx, tr, "q_prep_bwd")
    d_k_raw, g_k_gain = _head_prep_bwd(z, k_off, kv_w // LANES, k_gain, cos, sin, dk, 0, tr, "k_prep_bwd")
    zero_ctx = lambda w: jnp.zeros((n_ctx, w), BF16)
    dz = jnp.concatenate([
        jnp.concatenate([zero_ctx(attn_w), d_q_raw], axis=0), d_k_raw, dv.astype(BF16), d_xr,
        jnp.concatenate([zero_ctx(d), d_xg], axis=0), jnp.concatenate([zero_ctx(d), d_gla], axis=0),
        jnp.concatenate([zero_ctx(d), d_glr], axis=0)], axis=1)
    dh = _matmul(dz, w_in_f, tb=True, name="mm_d_h")
    g_w_in = _matmul(h, dz, ta=True, name="mm_g_in")
    grad_x, sums1 = _norm_mod_bwd1(xcat, dh, dx1, g_mix, mod4, n_ctx, tr)

    zeros_d = jnp.zeros((d,), F32)
    dmod_lat = jnp.concatenate([sums1[0], sums1[1], sums2[3], sums2[0], sums2[1], sums_fin[1]])
    dmod_ctx = jnp.concatenate([sums1[3], sums1[4]] + [zeros_d] * 4)
    small_parts = [dmod_lat, dmod_ctx, loss_blk[0, 0:1], sums1[2] + sums1[5], sums2[2], g_q_gain[0], g_k_gain[0],
                   sums_rnn[10], g_w_rg, g_w_ig, sums_fin[0], sums_rnn[6:10], sums_rnn[0:2], sums_rnn[2:4],
                   sums_rnn[4:6]]
    pack1 = _pack(small_parts)
    got1 = _all_gather8(pack1, "gather_small_grads").reshape(N_DEV, -1, LANES)
    tot1 = _sum_leading(got1, "sum_small_grads")
    part_shapes = [a.shape for a in small_parts]
    (s_dmod_lat, s_dmod_ctx, s_loss, g_g_mix, g_g_mlp, g_q_gain, g_k_gain, g_conv_b, g_w_rg, g_w_ig, g_g_final,
     g_conv_w_f, g_b_rg_f, g_b_ig_f, g_lam_f) = _unpack(tot1, part_shapes)
    loss = s_loss[0]
    g_b_mod = (s_dmod_lat + s_dmod_ctx)[None, :]
    n_mod_rows = d_mod // LANES
    dmod16 = jnp.concatenate([got1[:, :n_mod_rows].reshape(N_DEV, d_mod), s_dmod_ctx[None, :],
                              jnp.zeros((16 - N_DEV - 1, d_mod), F32)], axis=0)
    dmod16_shard = lax.dynamic_slice(dmod16, (0, chip * (d_mod // N_CHIPS)), (16, d_mod // N_CHIPS))
    g_w_mod = _matmul(silu16, dmod16_shard, ta=True, name="mm_g_mod")
    dsilu_part = _matmul(dmod16_shard[N_DEV:], w_mod[0], tb=True, name="mm_d_silu")
    dsilu_all = _all_gather8(dsilu_part, "gather_d_silu").reshape(N_DEV, 8, d)
    g_c_ctx = _c_ctx_grad(dsilu_all, c_ctx[None, :])[0]

    def shard_of(full):
        w = full.shape[-1] // N_CHIPS
        return lax.dynamic_slice(full, (0, chip * w), (full.shape[0], w))

    grads = {
        'c_ctx': g_c_ctx, 'b_mod': g_b_mod, 'g_mix': g_g_mix[None, :], 'g_mlp': g_g_mlp[None, :],
        'q_gain': g_q_gain[None, :], 'k_gain': g_k_gain[None, :], 'conv_b': g_conv_b[None, :],
        'w_rg': g_w_rg[None], 'w_ig': g_w_ig[None], 'g_final': g_g_final,
        'conv_w': shard_of(g_conv_w_f)[None], 'b_rg': shard_of(g_b_rg_f)[None], 'b_ig': shard_of(g_b_ig_f)[None],
        'lru_lambda': shard_of(g_lam_f)[None], 'w_mod': g_w_mod[None],
    }

    full_grads = [g_w_in, g_w_oa, g_w_or, g_w_out, g_w_up, g_w_down]
    from_sibling = _swap_halves(full_grads, BIG_COL_SHARDED)
    pair_sums = [_add_half(g, o, core, col, "pair_sum_" + n)
                 for g, o, col, n in zip(full_grads, from_sibling, BIG_COL_SHARDED, BIG)]
    regions = _scatter_regions(pair_sums, BIG_COL_SHARDED)
    half_sums = [_sum_leading(r, "chip_sum_" + n) for r, n in zip(regions, BIG)]
    for n, g in zip(BIG, _join_halves(half_sums, BIG_COL_SHARDED)):
        grads[n] = g[None]

    delta, new_m, new_v = {}, {}, {}
    for n in BIG + ['w_mod']:
        shp = weights[n].shape
        dl, nm, nv = _adamw(weights[n][0], grads[n][0], moms[n][0], vars_[n][0], "adamw_" + n)
        delta[n], new_m[n], new_v[n] = dl.reshape(shp), nm.reshape(shp), nv.reshape(shp)
    small_shapes = [weights[n].shape for n in SMALL]
    packed = [_pack([src[n] for n in SMALL]) for src in (weights, grads, moms, vars_)]
    outs = _adamw(*packed, "adamw_small")
    for res, out in zip((delta, new_m, new_v), outs):
        for n, a in zip(SMALL, _unpack(out, small_shapes)):
            res[n] = a
    return (loss, grad_x[None], *[grads[n] for n in WEIGHT_NAMES], *[delta[n] for n in WEIGHT_NAMES],
            *[new_m[n] for n in WEIGHT_NAMES], *[new_v[n] for n in WEIGHT_NAMES])
```

```python
import functools
import math

import jax
import jax.numpy as jnp
from jax import lax
from jax.experimental import pallas as pl
from jax.experimental.pallas import tpu as pltpu

F32 = jnp.float32
BF16 = jnp.bfloat16
MESH_ID = pl.DeviceIdType.MESH
ANY = pl.BlockSpec(memory_space=pl.ANY)

NORM_EPS = 1e-6
LRU_C = 8.0
GRID_W = 64
ROPE_THETA = 10000.0
N_MOD = 6
CONV_WIDTH = 4
ADAM_LR = 0.001
ADAM_B1 = 0.9
ADAM_B2 = 0.999
ADAM_EPS = 1e-08
ADAM_WD = 0.01
ADAM_STEP = 10

LANES = 128
SUBLANES = 8
V7X_VMEM_LIMIT = 48 * 1024 * 1024
N_CHIPS = 4
N_DEV = 8
GELU_C = math.sqrt(2.0 / math.pi)
GELU_A = 0.044715


def _tile(dim, pref, align):
    t = min(pref, dim)
    t -= t % align
    while t >= align:
        if dim % t == 0:
            return t
        t -= align
    return dim


def _params(*sem):
    return pltpu.CompilerParams(dimension_semantics=sem, vmem_limit_bytes=V7X_VMEM_LIMIT)


def _sds(shape, dtype=F32):
    return jax.ShapeDtypeStruct(shape, dtype)


def _matmul(a, b, *, ta=False, tb=False, out_dtype=F32, name, tm=1024, tn=1024, tk=512):
    k_dim, m = a.shape if ta else a.shape[::-1]
    n, k2 = b.shape if tb else b.shape[::-1]
    assert k_dim == k2, (a.shape, b.shape, ta, tb)
    tm = _tile(m, tm, LANES if ta else 16)
    tn = _tile(n, tn, 16 if tb else LANES)
    tk = _tile(k_dim, tk, LANES)
    nk = k_dim // tk
    dims = (((0 if ta else 1,), (1 if tb else 0,)), ((), ()))

    def body(a_ref, b_ref, o_ref, acc_ref):
        k = pl.program_id(2)

        @pl.when(k == 0)
        def _():
            acc_ref[...] = jnp.zeros_like(acc_ref)

        acc_ref[...] += lax.dot_general(a_ref[...].astype(BF16), b_ref[...].astype(BF16), dims,
                                        preferred_element_type=F32)

        @pl.when(k == nk - 1)
        def _():
            o_ref[...] = acc_ref[...].astype(o_ref.dtype)

    a_spec = pl.BlockSpec((tk, tm), lambda i, j, k: (k, i)) if ta else pl.BlockSpec((tm, tk), lambda i, j, k: (i, k))
    b_spec = pl.BlockSpec((tn, tk), lambda i, j, k: (j, k)) if tb else pl.BlockSpec((tk, tn), lambda i, j, k: (k, j))
    return pl.pallas_call(
        body, name=name, grid=(m // tm, n // tn, nk), in_specs=[a_spec, b_spec],
        out_specs=pl.BlockSpec((tm, tn), lambda i, j, k: (i, j)), out_shape=_sds((m, n), out_dtype),
        scratch_shapes=[pltpu.VMEM((tm, tn), F32)], compiler_params=_params("parallel", "parallel", "arbitrary"),
    )(a, b)


def _silu(x):
    return x * jax.nn.sigmoid(x)


def _gelu(x):
    return 0.5 * x * (1.0 + jnp.tanh(GELU_C * (x + GELU_A * x * x * x)))


def _gelu_grad(x):
    t = jnp.tanh(GELU_C * (x + GELU_A * x * x * x))
    return 0.5 * (1.0 + t) + 0.5 * x * (1.0 - t * t) * GELU_C * (1.0 + 3.0 * GELU_A * x * x)


def _expm1_nonpos(x):
    series = x * (1.0 + x * (1.0 / 2 + x * (1.0 / 6 + x * (1.0 / 24 + x * (1.0 / 120 + x * (1.0 / 720 + x / 5040))))))
    return jnp.where(x > -0.25, series, jnp.exp(x) - 1.0)


def _softplus(x):
    return jnp.maximum(x, 0.0) + jnp.log1p(jnp.exp(-jnp.abs(x)))


def _rms_stats(x):
    return lax.rsqrt(jnp.mean(x * x, axis=-1, keepdims=True) + NORM_EPS)


def _rms_bwd(dxhat, xhat, rstd):
    return rstd * (dxhat - xhat * jnp.mean(dxhat * xhat, axis=-1, keepdims=True))


def _colsum(v):
    return jnp.sum(v, axis=0, keepdims=True)


def _mod_fwd(c16, w_mod, b_mod_shard):
    r, d = c16.shape
    n = w_mod.shape[1]
    tn = _tile(n, 512, LANES)

    def body(c_ref, w_ref, b_ref, o_ref, s_ref):
        s = _silu(c_ref[...])
        s_ref[...] = s
        o_ref[...] = jnp.dot(s.astype(BF16), w_ref[...].astype(BF16), preferred_element_type=F32) + b_ref[...]

    return pl.pallas_call(
        body, name="mod_fwd", grid=(n // tn,),
        in_specs=[pl.BlockSpec((r, d), lambda j: (0, 0)), pl.BlockSpec((d, tn), lambda j: (0, j)),
                  pl.BlockSpec((1, tn), lambda j: (0, j))],
        out_specs=[pl.BlockSpec((r, tn), lambda j: (0, j)), pl.BlockSpec((r, d), lambda j: (0, 0))],
        out_shape=[_sds((r, n)), _sds((r, d))], compiler_params=_params("arbitrary"),
    )(c16, w_mod, b_mod_shard)


def _c_ctx_grad(parts, c_ctx_row):
    d = c_ctx_row.shape[1]

    def body(p_ref, c_ref, o_ref):
        tot = p_ref[0, 0:1, :]
        for chip in range(1, N_CHIPS):
            tot = tot + p_ref[2 * chip, 0:1, :]
        c = c_ref[...]
        sg = jax.nn.sigmoid(c)
        o_ref[...] = tot * (sg * (1.0 + c * (1.0 - sg)))

    return pl.pallas_call(body, name="c_ctx_grad", out_shape=_sds((1, d)), compiler_params=_params())(parts, c_ctx_row)


def _norm_mod_fwd(xcat, g, mod4, n_ctx_rows, tr):
    t, d = xcat.shape
    nctx = n_ctx_rows // tr

    def body(x_ref, g_ref, mod_ref, h_ref):
        x = x_ref[...]
        n = x * _rms_stats(x) * g_ref[...]
        is_ctx = pl.program_id(0) < nctx
        sh = jnp.where(is_ctx, mod_ref[0:1, :], mod_ref[2:3, :])
        sc = jnp.where(is_ctx, mod_ref[1:2, :], mod_ref[3:4, :])
        h_ref[...] = (n * (1.0 + sc) + sh).astype(BF16)

    return pl.pallas_call(
        body, name="norm_mod_fwd", grid=(t // tr,),
        in_specs=[pl.BlockSpec((tr, d), lambda i: (i, 0)), pl.BlockSpec((1, d), lambda i: (0, 0)),
                  pl.BlockSpec((4, d), lambda i: (0, 0))],
        out_specs=pl.BlockSpec((tr, d), lambda i: (i, 0)), out_shape=_sds((t, d), BF16),
        compiler_params=_params("parallel"),
    )(xcat, g, mod4)


def _norm_mod_bwd1(xcat, dh, dx1, g, mod4, n_ctx_rows, tr):
    t, d = xcat.shape
    nctx = n_ctx_rows // tr
    s = t - n_ctx_rows

    def body(x_ref, dh_ref, dx1_ref, g_ref, mod_ref, dx_ref, sums_ref):
        i = pl.program_id(0)
        is_ctx = i < nctx

        @pl.when(i == 0)
        def _():
            sums_ref[...] = jnp.zeros_like(sums_ref)

        x = x_ref[...]
        dh_ = dh_ref[...]
        rstd = _rms_stats(x)
        xhat = x * rstd
        gg = g_ref[...]
        sc = jnp.where(is_ctx, mod_ref[1:2, :], mod_ref[3:4, :])
        dxhat = dh_ * (1.0 + sc) * gg
        dx_ref[...] = dx1_ref[...] + _rms_bwd(dxhat, xhat, rstd)
        part = [_colsum(dh_), _colsum(dh_ * xhat * gg), _colsum(dh_ * (1.0 + sc) * xhat)]

        @pl.when(is_ctx)
        def _():
            for k, row in enumerate(part):
                sums_ref[3 + k:4 + k, :] += row

        @pl.when(jnp.logical_not(is_ctx))
        def _():
            for k, row in enumerate(part):
                sums_ref[k:k + 1, :] += row

    lat = lambda i: (jnp.maximum(i - nctx, 0), 0)
    return pl.pallas_call(
        body, name="norm_mod_bwd1", grid=(t // tr,),
        in_specs=[pl.BlockSpec((tr, d), lambda i: (i, 0)), pl.BlockSpec((tr, d), lambda i: (i, 0)),
                  pl.BlockSpec((tr, d), lat), pl.BlockSpec((1, d), lambda i: (0, 0)),
                  pl.BlockSpec((4, d), lambda i: (0, 0))],
        out_specs=[pl.BlockSpec((tr, d), lat), pl.BlockSpec((8, d), lambda i: (0, 0))],
        out_shape=[_sds((s, d)), _sds((8, d))], compiler_params=_params("arbitrary"),
    )(xcat, dh, dx1, g, mod4)


def _resid_norm_mod_fwd(x, mix, g, mod3, tr):
    s, d = x.shape

    def body(x_ref, mix_ref, g_ref, mod_ref, x1_ref, h_ref):
        x1 = x_ref[...] + mod_ref[0:1, :] * mix_ref[...]
        x1_ref[...] = x1
        n = x1 * _rms_stats(x1) * g_ref[...]
        h_ref[...] = (n * (1.0 + mod_ref[2:3, :]) + mod_ref[1:2, :]).astype(BF16)

    row = pl.BlockSpec((tr, d), lambda i: (i, 0))
    return pl.pallas_call(
        body, name="resid_norm_mod_fwd", grid=(s // tr,),
        in_specs=[row, row, pl.BlockSpec((1, d), lambda i: (0, 0)), pl.BlockSpec((3, d), lambda i: (0, 0))],
        out_specs=[row, row], out_shape=[_sds((s, d)), _sds((s, d), BF16)], compiler_params=_params("parallel"),
    )(x, mix, g, mod3)


def _norm_mod_bwd2(x1, dh2, dx2, mix, g, mod3, tr):
    s, d = x1.shape

    def body(x_ref, dh_ref, dx2_ref, mix_ref, g_ref, mod_ref, dx1_ref, dmix_ref, sums_ref):
        @pl.when(pl.program_id(0) == 0)
        def _():
            sums_ref[...] = jnp.zeros_like(sums_ref)

        x = x_ref[...]
        dh_ = dh_ref[...]
        rstd = _rms_stats(x)
        xhat = x * rstd
        gg = g_ref[...]
        sc = mod_ref[2:3, :]
        dx1 = dx2_ref[...] + _rms_bwd(dh_ * (1.0 + sc) * gg, xhat, rstd)
        dx1_ref[...] = dx1
        dmix_ref[...] = (dx1 * mod_ref[0:1, :]).astype(BF16)
        part = [_colsum(dh_), _colsum(dh_ * xhat * gg), _colsum(dh_ * (1.0 + sc) * xhat), _colsum(dx1 * mix_ref[...])]
        for k, row in enumerate(part):
            sums_ref[k:k + 1, :] += row

    row = pl.BlockSpec((tr, d), lambda i: (i, 0))
    return pl.pallas_call(
        body, name="norm_mod_bwd2", grid=(s // tr,),
        in_specs=[row, row, row, row, pl.BlockSpec((1, d), lambda i: (0, 0)), pl.BlockSpec((3, d), lambda i: (0, 0))],
        out_specs=[row, row, pl.BlockSpec((8, d), lambda i: (0, 0))],
        out_shape=[_sds((s, d)), _sds((s, d), BF16), _sds((8, d))], compiler_params=_params("arbitrary"),
    )(x1, dh2, dx2, mix, g, mod3)


def _final_fwd_bwd(x1, down, target, g_final, gate, tr):
    s, d = x1.shape

    def body(x1_ref, down_ref, t_ref, g_ref, gate_ref, dx2_ref, ddown_ref, sums_ref, loss_ref):
        @pl.when(pl.program_id(0) == 0)
        def _():
            sums_ref[...] = jnp.zeros_like(sums_ref)
            loss_ref[...] = jnp.zeros_like(loss_ref)

        down_ = down_ref[...]
        gate_ = gate_ref[...]
        x2 = x1_ref[...] + gate_ * down_
        rstd = _rms_stats(x2)
        xhat = x2 * rstd
        gg = g_ref[...]
        err = xhat * gg - t_ref[...]
        loss_ref[...] += 0.5 * jnp.sum(jnp.mean(err * err, axis=-1, keepdims=True))
        dy = err * (1.0 / d)
        dx2 = _rms_bwd(dy * gg, xhat, rstd)
        dx2_ref[...] = dx2
        ddown_ref[...] = (dx2 * gate_).astype(BF16)
        sums_ref[0:1, :] += _colsum(dy * xhat)
        sums_ref[1:2, :] += _colsum(dx2 * down_)

    row = pl.BlockSpec((tr, d), lambda i: (i, 0))
    vec = pl.BlockSpec((1, d), lambda i: (0, 0))
    return pl.pallas_call(
        body, name="final_fwd_bwd", grid=(s // tr,), in_specs=[row, row, row, vec, vec],
        out_specs=[row, row, pl.BlockSpec((8, d), lambda i: (0, 0)), pl.BlockSpec((8, LANES), lambda i: (0, 0))],
        out_shape=[_sds((s, d)), _sds((s, d), BF16), _sds((8, d)), _sds((8, LANES))],
        compiler_params=_params("arbitrary"),
    )(x1, down, target, g_final, gate)


def _swap_pairs(v):
    lane = lax.broadcasted_iota(jnp.int32, v.shape, 1)
    return jnp.where(lane % 2 == 0, pltpu.roll(v, LANES - 1, 1), pltpu.roll(v, 1, 1))


def _head_prep_fwd(z, col_off, n_heads, gain, cos, sin, tr, name):
    t = z.shape[0]
    hb = col_off // LANES

    def body(z_ref, g_ref, cos_ref, sin_ref, o_ref):
        x = z_ref[...]
        y = x * _rms_stats(x) * g_ref[...]
        o_ref[...] = (y * cos_ref[...] + _swap_pairs(y) * sin_ref[...]).astype(BF16)

    tab = pl.BlockSpec((tr, LANES), lambda i, j: (i, 0))
    return pl.pallas_call(
        body, name=name, grid=(t // tr, n_heads),
        in_specs=[pl.BlockSpec((tr, LANES), lambda i, j: (i, hb + j)), pl.BlockSpec((1, LANES), lambda i, j: (0, 0)),
                  tab, tab],
        out_specs=pl.BlockSpec((tr, LANES), lambda i, j: (i, j)), out_shape=_sds((t, n_heads * LANES), BF16),
        compiler_params=_params("parallel", "parallel"),
    )(z, gain, cos, sin)


def _head_prep_bwd(z, col_off, n_heads, gain, cos, sin, dout, row_off, tr, name):
    r = dout.shape[0]
    hb = col_off // LANES
    rb = row_off // tr

    def body(z_ref, g_ref, cos_ref, sin_ref, d_ref, dz_ref, dg_ref):
        @pl.when(jnp.logical_and(pl.program_id(0) == 0, pl.program_id(1) == 0))
        def _():
            dg_ref[...] = jnp.zeros_like(dg_ref)

        x = z_ref[...]
        rstd = _rms_stats(x)
        xhat = x * rstd
        dd = d_ref[...]
        dy = dd * cos_ref[...] - _swap_pairs(dd) * sin_ref[...]
        dg_ref[0:1, :] += _colsum(dy * xhat)
        dz_ref[...] = _rms_bwd(dy * g_ref[...], xhat, rstd).astype(BF16)

    tab = pl.BlockSpec((tr, LANES), lambda i, j: (rb + i, 0))
    return pl.pallas_call(
        body, name=name, grid=(r // tr, n_heads),
        in_specs=[pl.BlockSpec((tr, LANES), lambda i, j: (rb + i, hb + j)),
                  pl.BlockSpec((1, LANES), lambda i, j: (0, 0)), tab, tab,
                  pl.BlockSpec((tr, LANES), lambda i, j: (i, j))],
        out_specs=[pl.BlockSpec((tr, LANES), lambda i, j: (i, j)), pl.BlockSpec((8, LANES), lambda i, j: (0, 0))],
        out_shape=[_sds((r, n_heads * LANES), BF16), _sds((8, LANES))],
        compiler_params=_params("arbitrary", "arbitrary"),
    )(z, gain, cos, sin, dout)


def _softmax_rows(q, k, scale):
    s = lax.dot_general(q, k, (((1,), (1,)), ((), ())), preferred_element_type=F32) * scale
    p = jnp.exp(s - jnp.max(s, axis=-1, keepdims=True))
    return p / jnp.sum(p, axis=-1, keepdims=True)


def _attn_fwd(qr, kr, z, v_off, n_ctx_rows, group, tq):
    t, kvw = kr.shape
    s = t - n_ctx_rows
    n_kv = kvw // LANES
    scale = LANES ** -0.5
    qb0 = n_ctx_rows // tq
    vb = v_off // LANES

    def body(q_ref, k_ref, v_ref, o_ref):
        k = k_ref[...]
        v = v_ref[...].astype(BF16)
        for g in range(group):
            cols = slice(g * LANES, (g + 1) * LANES)
            p = _softmax_rows(q_ref[:, cols], k, scale)
            o_ref[:, cols] = jnp.dot(p.astype(BF16), v, preferred_element_type=F32)

    return pl.pallas_call(
        body, name="attn_fwd", grid=(n_kv, s // tq),
        in_specs=[pl.BlockSpec((tq, group * LANES), lambda h, i: (qb0 + i, h)),
                  pl.BlockSpec((t, LANES), lambda h, i: (0, h)), pl.BlockSpec((t, LANES), lambda h, i: (0, vb + h))],
        out_specs=pl.BlockSpec((tq, group * LANES), lambda h, i: (i, h)),
        out_shape=_sds((s, n_kv * group * LANES)), compiler_params=_params("parallel", "parallel"),
    )(qr, kr, z)


def _attn_bwd(qr, kr, z, v_off, d_o, n_ctx_rows, group, tq):
    t, kvw = kr.shape
    s = t - n_ctx_rows
    n_kv = kvw // LANES
    scale = LANES ** -0.5
    qb0 = n_ctx_rows // tq
    vb = v_off // LANES
    tn_dims = (((0,), (0,)), ((), ()))
    nt_dims = (((1,), (1,)), ((), ()))

    def body(q_ref, k_ref, v_ref, do_ref, dq_ref, dk_ref, dv_ref):
        @pl.when(pl.program_id(1) == 0)
        def _():
            dk_ref[...] = jnp.zeros_like(dk_ref)
            dv_ref[...] = jnp.zeros_like(dv_ref)

        k = k_ref[...]
        v = v_ref[...].astype(BF16)
        for g in range(group):
            cols = slice(g * LANES, (g + 1) * LANES)
            q = q_ref[:, cols]
            do_ = do_ref[:, cols]
            p = _softmax_rows(q, k, scale)
            dv_ref[...] += lax.dot_general(p.astype(BF16), do_, tn_dims, preferred_element_type=F32)
            dp = lax.dot_general(do_, v, nt_dims, preferred_element_type=F32)
            ds = (p * (dp - jnp.sum(p * dp, axis=-1, keepdims=True)) * scale).astype(BF16)
            dq_ref[:, cols] = jnp.dot(ds, k, preferred_element_type=F32)
            dk_ref[...] += lax.dot_general(ds, q, tn_dims, preferred_element_type=F32)

    qspec = pl.BlockSpec((tq, group * LANES), lambda h, i: (qb0 + i, h))
    ospec = pl.BlockSpec((tq, group * LANES), lambda h, i: (i, h))
    kspec = pl.BlockSpec((t, LANES), lambda h, i: (0, h))
    return pl.pallas_call(
        body, name="attn_bwd", grid=(n_kv, s // tq),
        in_specs=[qspec, kspec, pl.BlockSpec((t, LANES), lambda h, i: (0, vb + h)), ospec],
        out_specs=[ospec, kspec, kspec],
        out_shape=[_sds((s, n_kv * group * LANES)), _sds((t, kvw)), _sds((t, kvw))],
        compiler_params=_params("parallel", "arbitrary"),
    )(qr, kr, z, d_o)


def _row_mask(shape, rows):
    r = lax.broadcasted_iota(jnp.int32, shape, 0)
    m = r == rows[0]
    for v in rows[1:]:
        m = jnp.logical_or(m, r == v)
    return m


def _shift_rows(x, k, n_ctx_rows):
    t = x.shape[0]
    if k == 0:
        return x
    rolled = pltpu.roll(x, (-k) % t, 0)
    if k > 0:
        dead = [n_ctx_rows - 1 - i for i in range(k)] + [t - 1 - i for i in range(k)]
    else:
        dead = [i for i in range(-k)] + [n_ctx_rows + i for i in range(-k)]
    return jnp.where(_row_mask(x.shape, dead), 0.0, rolled)


def _conv(x, w, b, n_ctx_rows):
    y = b
    for k in range(CONV_WIDTH):
        y = y + _shift_rows(x, k - 1, n_ctx_rows) * w[k:k + 1, :]
    return y


def _gates(xc_bf, w_r, b_r, w_i, b_i, lam):
    r = jax.nn.sigmoid(jnp.dot(xc_bf, w_r.astype(BF16), preferred_element_type=F32) + b_r)
    i = jax.nn.sigmoid(jnp.dot(xc_bf, w_i.astype(BF16), preferred_element_type=F32) + b_i)
    log_a = -LRU_C * r * _softplus(-lam)
    a = jnp.exp(log_a)
    mult = jnp.sqrt(-_expm1_nonpos(2.0 * log_a))
    return r, i, a, mult


def _rnn_specs(t, xr_off):
    xb = xr_off // LANES
    return dict(
        zcol=pl.BlockSpec((t, LANES), lambda j: (0, xb + j)), col=pl.BlockSpec((t, LANES), lambda j: (0, j)),
        conv_w=pl.BlockSpec((CONV_WIDTH, LANES), lambda j: (0, j)), vec=pl.BlockSpec((1, LANES), lambda j: (0, j)),
        gate_w=pl.BlockSpec((2, 1, LANES, LANES), lambda j: (0, j, 0, 0)), two=pl.BlockSpec((2, LANES), lambda j: (0, j)))


def _rnn_prep(z, xr_off, conv_w, conv_b, w_rg, b_rg, w_ig, b_ig, lam, n_ctx_rows):
    t = z.shape[0]
    d = conv_b.shape[1]
    sp = _rnn_specs(t, xr_off)

    def body(z_ref, cw_ref, cb_ref, wr_ref, br_ref, wi_ref, bi_ref, lam_ref, xc_ref, af_ref, bf_ref, ab_ref, bb_ref):
        xc = _conv(z_ref[...], cw_ref[...], cb_ref[...], n_ctx_rows)
        xc_ref[...] = xc
        xc_bf = xc.astype(BF16)
        for dr, (a_ref, b_ref) in enumerate(((af_ref, bf_ref), (ab_ref, bb_ref))):
            _, i, a, mult = _gates(xc_bf, wr_ref[dr, 0], br_ref[dr:dr + 1, :], wi_ref[dr, 0], bi_ref[dr:dr + 1, :],
                                   lam_ref[dr:dr + 1, :])
            a_ref[...] = a
            b_ref[...] = mult * (i * xc)

    return pl.pallas_call(
        body, name="rnn_prep", grid=(d // LANES,),
        in_specs=[sp["zcol"], sp["conv_w"], sp["vec"], sp["gate_w"], sp["two"], sp["gate_w"], sp["two"], sp["two"]],
        out_specs=[sp["col"]] * 5, out_shape=[_sds((t, d))] * 5, compiler_params=_params("parallel"),
    )(z, conv_w, conv_b, w_rg, b_rg, w_ig, b_ig, lam)


def _scan(a, b, *, order, post, n_ctx_rows, name, tc=256):
    t, d = a.shape
    tc = _tile(math.gcd(n_ctx_rows, t - n_ctx_rows), tc, SUBLANES)
    nt, nctx = t // tc, n_ctx_rows // tc
    nlat = nt - nctx
    b_lat_only = b.shape[0] != t
    up = order.endswith("up")

    def chunk(i):
        if order == "ctx_lat_up":
            return i
        if order == "lat_ctx_down":
            return nt - 1 - i
        if order == "ctx_lat_down":
            return jnp.where(i < nctx, nctx - 1 - i, nt - 1 - (i - nctx))
        return jnp.where(i < nlat, nctx + i, i - nlat)

    def body(a_ref, b_ref, o_ref, carry_ref):
        @pl.when(pl.program_id(0) == 0)
        def _():
            carry_ref[...] = jnp.zeros_like(carry_ref)

        live = jnp.where(chunk(pl.program_id(0)) >= nctx, 1.0, 0.0) if b_lat_only else None

        def group(gi, carry):
            base = pl.multiple_of((gi if up else tc // SUBLANES - 1 - gi) * SUBLANES, SUBLANES)
            for r in (range(SUBLANES) if up else range(SUBLANES - 1, -1, -1)):
                a_r = a_ref[pl.ds(base + r, 1), :]
                b_r = b_ref[pl.ds(base + r, 1), :]
                if live is not None:
                    b_r = b_r * live
                if post:
                    out = b_r + carry
                    carry = a_r * out
                else:
                    out = a_r * carry + b_r
                    carry = out
                o_ref[pl.ds(base + r, 1), :] = out
            return carry

        carry_ref[0:1, :] = lax.fori_loop(0, tc // SUBLANES, group, carry_ref[0:1, :])

    full = pl.BlockSpec((tc, d), lambda i: (chunk(i), 0))
    b_spec = pl.BlockSpec((tc, d), lambda i: (jnp.maximum(chunk(i) - nctx, 0), 0)) if b_lat_only else full
    return pl.pallas_call(
        body, name=name, grid=(nt,), in_specs=[full, b_spec], out_specs=full, out_shape=_sds((t, d)),
        scratch_shapes=[pltpu.VMEM((SUBLANES, d), F32)], compiler_params=_params("arbitrary"),
    )(a, b)


def _rnn_bwd(z, xr_off, xc, g_f, g_b, h_f, h_b, conv_w, w_rg, b_rg, w_ig, b_ig, lam, n_ctx_rows):
    t, d = xc.shape
    sp = _rnn_specs(t, xr_off)
    tn_dims = (((0,), (0,)), ((), ()))
    nt_dims = (((1,), (1,)), ((), ()))

    def body(z_ref, xc_ref, gf_ref, gb_ref, hf_ref, hb_ref, cw_ref, wr_ref, br_ref, wi_ref, bi_ref, lam_ref,
             dxr_ref, dwr_ref, dwi_ref, sums_ref):
        xc_ = xc_ref[...]
        xc_bf = xc_.astype(BF16)
        dxc = jnp.zeros_like(xc_)
        sums = [None] * 6
        for dr, (g_ref, h_ref) in enumerate(((gf_ref, hf_ref), (gb_ref, hb_ref))):
            w_r, w_i, lam_ = wr_ref[dr, 0], wi_ref[dr, 0], lam_ref[dr:dr + 1, :]
            r, i, a, mult = _gates(xc_bf, w_r, br_ref[dr:dr + 1, :], w_i, bi_ref[dr:dr + 1, :], lam_)
            g = g_ref[...]
            h = h_ref[...]
            if dr == 0:
                h_prev = jnp.where(_row_mask(h.shape, [0]), 0.0, pltpu.roll(h, 1, 0))
            else:
                h_prev = jnp.where(_row_mask(h.shape, [n_ctx_rows - 1]), 0.0, pltpu.roll(h, t - 1, 0))
            d_mult = g * i * xc_
            d_i = g * mult * xc_
            dxc = dxc + g * mult * i
            d_log_a = g * h_prev * a - d_mult * a * a / mult
            sp_ = _softplus(-lam_)
            d_r = d_log_a * (-LRU_C) * sp_
            d_sp = _colsum(d_log_a * (-LRU_C) * r)
            du_r = (d_r * r * (1.0 - r))
            du_i = (d_i * i * (1.0 - i))
            sums[dr] = _colsum(du_r)
            sums[2 + dr] = _colsum(du_i)
            sums[4 + dr] = d_sp * (-jax.nn.sigmoid(-lam_))
            du_r_bf, du_i_bf = du_r.astype(BF16), du_i.astype(BF16)
            dwr_ref[dr, 0] = lax.dot_general(xc_bf, du_r_bf, tn_dims, preferred_element_type=F32).astype(BF16)
            dwi_ref[dr, 0] = lax.dot_general(xc_bf, du_i_bf, tn_dims, preferred_element_type=F32).astype(BF16)
            dxc = dxc + lax.dot_general(du_r_bf, w_r.astype(BF16), nt_dims, preferred_element_type=F32)
            dxc = dxc + lax.dot_general(du_i_bf, w_i.astype(BF16), nt_dims, preferred_element_type=F32)
        xr = z_ref[...]
        cw = cw_ref[...]
        dxr = jnp.zeros_like(dxc)
        rows = list(sums)
        for k in range(CONV_WIDTH):
            dxr = dxr + _shift_rows(dxc, 1 - k, n_ctx_rows) * cw[k:k + 1, :]
            rows.append(_colsum(dxc * _shift_rows(xr, k - 1, n_ctx_rows)))
        rows.append(_colsum(dxc))
        dxr_ref[...] = dxr.astype(BF16)
        sums_ref[...] = jnp.zeros_like(sums_ref)
        for k, row in enumerate(rows):
            sums_ref[k:k + 1, :] = row

    return pl.pallas_call(
        body, name="rnn_bwd", grid=(d // LANES,),
        in_specs=[sp["zcol"]] + [sp["col"]] * 5 + [sp["conv_w"], sp["gate_w"], sp["two"], sp["gate_w"], sp["two"],
                                                  sp["two"]],
        out_specs=[sp["col"], sp["gate_w"], sp["gate_w"], pl.BlockSpec((16, LANES), lambda j: (0, j))],
        out_shape=[_sds((t, d), BF16), _sds(w_rg.shape, BF16), _sds(w_ig.shape, BF16), _sds((16, d))],
        compiler_params=_params("parallel"),
    )(z, xc, g_f, g_b, h_f, h_b, conv_w, w_rg, b_rg, w_ig, b_ig, lam)


def _tiles2d(s, d, tr, tcol):
    return (s // tr, d // tcol), pl.BlockSpec((tr, tcol), lambda i, j: (i, j))


def _zspec(tr, tcol, row_off, col_off):
    rb, cb = row_off // tr, col_off // tcol
    return pl.BlockSpec((tr, tcol), lambda i, j: (rb + i, cb + j))


def _rnn_gate_fwd(h_f, h_b, z, xg_off, n_ctx_rows, tr, tcol):
    t, d = h_f.shape
    s = t - n_ctx_rows
    grid, out = _tiles2d(s, d, tr, tcol)
    hs = _zspec(tr, tcol, n_ctx_rows, 0)

    def body(hf_ref, hb_ref, xg_ref, u_ref):
        u_ref[...] = ((hf_ref[...] + hb_ref[...]) * _gelu(xg_ref[...])).astype(BF16)

    return pl.pallas_call(body, name="rnn_gate_fwd", grid=grid, in_specs=[hs, hs, _zspec(tr, tcol, n_ctx_rows, xg_off)],
                          out_specs=out, out_shape=_sds((s, d), BF16), compiler_params=_params("parallel", "parallel"),
                          )(h_f, h_b, z)


def _rnn_gate_bwd(d_u, h_f, h_b, z, xg_off, n_ctx_rows, tr, tcol):
    t, d = h_f.shape
    s = t - n_ctx_rows
    grid, out = _tiles2d(s, d, tr, tcol)
    hs = _zspec(tr, tcol, n_ctx_rows, 0)

    def body(du_ref, hf_ref, hb_ref, xg_ref, dr_ref, dxg_ref):
        du = du_ref[...]
        xg = xg_ref[...]
        dr_ref[...] = du * _gelu(xg)
        dxg_ref[...] = (du * (hf_ref[...] + hb_ref[...]) * _gelu_grad(xg)).astype(BF16)

    return pl.pallas_call(body, name="rnn_gate_bwd", grid=grid,
                          in_specs=[out, hs, hs, _zspec(tr, tcol, n_ctx_rows, xg_off)], out_specs=[out, out],
                          out_shape=[_sds((s, d)), _sds((s, d), BF16)],
                          compiler_params=_params("parallel", "parallel"))(d_u, h_f, h_b, z)


def _merge_fwd(y_attn, y_rnn, z, gl_off, n_ctx_rows, tr, tcol):
    s, d = y_attn.shape
    grid, out = _tiles2d(s, d, tr, tcol)

    def body(ya_ref, yr_ref, ga_ref, gr_ref, o_ref):
        o_ref[...] = (jax.nn.sigmoid(ga_ref[...]) * ya_ref[...] + jax.nn.sigmoid(gr_ref[...]) * yr_ref[...]).astype(BF16)

    return pl.pallas_call(
        body, name="merge_fwd", grid=grid,
        in_specs=[out, out, _zspec(tr, tcol, n_ctx_rows, gl_off), _zspec(tr, tcol, n_ctx_rows, gl_off + d)],
        out_specs=out, out_shape=_sds((s, d), BF16), compiler_params=_params("parallel", "parallel"),
    )(y_attn, y_rnn, z, z)


def _merge_bwd(d_mrg, y_attn, y_rnn, z, gl_off, n_ctx_rows, tr, tcol):
    s, d = y_attn.shape
    grid, out = _tiles2d(s, d, tr, tcol)

    def body(dm_ref, ya_ref, yr_ref, ga_ref, gr_ref, dya_ref, dyr_ref, dga_ref, dgr_ref):
        dm = dm_ref[...]
        ga = jax.nn.sigmoid(ga_ref[...])
        gr = jax.nn.sigmoid(gr_ref[...])
        dya_ref[...] = (dm * ga).astype(BF16)
        dyr_ref[...] = (dm * gr).astype(BF16)
        dga_ref[...] = (dm * ya_ref[...] * ga * (1.0 - ga)).astype(BF16)
        dgr_ref[...] = (dm * yr_ref[...] * gr * (1.0 - gr)).astype(BF16)

    return pl.pallas_call(
        body, name="merge_bwd", grid=grid,
        in_specs=[out, out, out, _zspec(tr, tcol, n_ctx_rows, gl_off), _zspec(tr, tcol, n_ctx_rows, gl_off + d)],
        out_specs=[out] * 4, out_shape=[_sds((s, d), BF16)] * 4, compiler_params=_params("parallel", "parallel"),
    )(d_mrg, y_attn, y_rnn, z, z)


def _sq_relu(up, tr, tcol):
    grid, out = _tiles2d(*up.shape, tr, tcol)

    def body(u_ref, o_ref):
        r = jnp.maximum(u_ref[...], 0.0)
        o_ref[...] = (r * r).astype(BF16)

    return pl.pallas_call(body, name="sq_relu", grid=grid, in_specs=[out], out_specs=out,
                          out_shape=_sds(up.shape, BF16), compiler_params=_params("parallel", "parallel"))(up)


def _sq_relu_bwd(d_act, up, tr, tcol):
    grid, out = _tiles2d(*up.shape, tr, tcol)

    def body(d_ref, u_ref, o_ref):
        o_ref[...] = (d_ref[...] * 2.0 * jnp.maximum(u_ref[...], 0.0)).astype(BF16)

    return pl.pallas_call(body, name="sq_relu_bwd", grid=grid, in_specs=[out, out], out_specs=out,
                          out_shape=_sds(up.shape, BF16), compiler_params=_params("parallel", "parallel"))(d_act, up)


def _cast_into_window(w, chip, col_sharded, name):
    r, c = w.shape
    tr, tcol = _tile(r, 512, 16), _tile(c, 1024, LANES)
    nrb, ncb = r // tr, c // tcol

    def body(chip_ref, w_ref, o_ref):
        o_ref[...] = w_ref[...].astype(BF16)

    if col_sharded:
        omap = lambda i, j, chip_ref: (i, chip_ref[0] * ncb + j)
    else:
        omap = lambda i, j, chip_ref: (chip_ref[0] * nrb + i, j)
    return pl.pallas_call(
        body, name=name,
        grid_spec=pltpu.PrefetchScalarGridSpec(
            num_scalar_prefetch=1, grid=(nrb, ncb),
            in_specs=[pl.BlockSpec((tr, tcol), lambda i, j, chip_ref: (i, j))], out_specs=pl.BlockSpec((tr, tcol), omap)),
        out_shape=_sds((r, c * N_CHIPS) if col_sharded else (r * N_CHIPS, c), BF16),
        compiler_params=_params("parallel", "parallel"),
    )(chip, w)


def _sum_leading(parts, name):
    n, r, c = parts.shape
    tr, tcol = _tile(r, 512, SUBLANES), _tile(c, 1024, LANES)

    def body(p_ref, o_ref):
        tot = p_ref[0]
        for k in range(1, n):
            tot = tot + p_ref[k]
        o_ref[...] = tot

    return pl.pallas_call(
        body, name=name, grid=(r // tr, c // tcol), in_specs=[pl.BlockSpec((n, tr, tcol), lambda i, j: (0, i, j))],
        out_specs=pl.BlockSpec((tr, tcol), lambda i, j: (i, j)), out_shape=_sds((r, c)),
        compiler_params=_params("parallel", "parallel"),
    )(parts)


def _add_half(full, other, core, split_rows, name):
    r, c = other.shape
    tr, tcol = _tile(r, 512, 16), _tile(c, 1024, LANES)
    nrb, ncb = r // tr, c // tcol

    def body(core_ref, f_ref, o_ref, out_ref):
        out_ref[...] = (f_ref[...].astype(F32) + o_ref[...].astype(F32)).astype(out_ref.dtype)

    if split_rows:
        fmap = lambda i, j, core_ref: (core_ref[0] * nrb + i, j)
    else:
        fmap = lambda i, j, core_ref: (i, core_ref[0] * ncb + j)
    same = lambda i, j, core_ref: (i, j)
    return pl.pallas_call(
        body, name=name,
        grid_spec=pltpu.PrefetchScalarGridSpec(
            num_scalar_prefetch=1, grid=(nrb, ncb),
            in_specs=[pl.BlockSpec((tr, tcol), fmap), pl.BlockSpec((tr, tcol), same)],
            out_specs=pl.BlockSpec((tr, tcol), same)),
        out_shape=_sds((r, c), BF16), compiler_params=_params("parallel", "parallel"),
    )(core, full, other)


def _sum_regions(pair, got, place, col_sharded, name):
    _, r, c = got.shape
    tr, tcol = _tile(r, 512, 16), _tile(c, 1024, LANES)
    nrb, ncb = r // tr, c // tcol

    def body(place_ref, p_ref, g_ref, out_ref):
        tot = p_ref[...].astype(F32)
        for k in range(N_CHIPS - 1):
            tot = tot + g_ref[k].astype(F32)
        out_ref[...] = tot

    if col_sharded:
        pmap = lambda i, j, pr: (i, pr[0] * ncb + j)
        omap = lambda i, j, pr: (pr[1] * nrb + i, j)
        out_shape = (2 * r, c)
    else:
        pmap = lambda i, j, pr: (pr[0] * nrb + i, j)
        omap = lambda i, j, pr: (i, pr[1] * ncb + j)
        out_shape = (r, 2 * c)
    return pl.pallas_call(
        body, name=name,
        grid_spec=pltpu.PrefetchScalarGridSpec(
            num_scalar_prefetch=1, grid=(nrb, ncb),
            in_specs=[pl.BlockSpec((tr, tcol), pmap), pl.BlockSpec((N_CHIPS - 1, tr, tcol), lambda i, j, pr: (0, i, j))],
            out_specs=pl.BlockSpec((tr, tcol), omap)),
        out_shape=_sds(out_shape), compiler_params=_params("parallel", "parallel"),
    )(place, pair, got)


def _adamw(w, g, m, v, name):
    r, c = w.shape
    tr, tcol = _tile(r, 512, SUBLANES), _tile(c, 1024, LANES)
    blk = pl.BlockSpec((tr, tcol), lambda i, j: (i, j))

    def body(w_ref, g_ref, m_ref, v_ref, d_ref, nm_ref, nv_ref):
        g_ = g_ref[...]
        m_ = ADAM_B1 * m_ref[...] + (1.0 - ADAM_B1) * g_
        v_ = ADAM_B2 * v_ref[...] + (1.0 - ADAM_B2) * (g_ * g_)
        m_hat = m_ / (1.0 - ADAM_B1 ** ADAM_STEP)
        v_hat = v_ / (1.0 - ADAM_B2 ** ADAM_STEP)
        d_ref[...] = -ADAM_LR * (m_hat / (jnp.sqrt(v_hat) + ADAM_EPS) + ADAM_WD * w_ref[...])
        nm_ref[...] = m_
        nv_ref[...] = v_

    return pl.pallas_call(body, name=name, grid=(r // tr, c // tcol), in_specs=[blk] * 4, out_specs=[blk] * 3,
                          out_shape=[_sds((r, c))] * 3, compiler_params=_params("parallel", "parallel"))(w, g, m, v)


def _place():
    x, y, c = lax.axis_index("x"), lax.axis_index("y"), lax.axis_index("c")
    chips = [(1 - x, y), (x, 1 - y), (1 - x, 1 - y)]
    return x, y, c, chips


def _all_gather8(blk, name):
    m, n = blk.shape

    def body(x_ref, out_ref, send_sems, recv_sems, local_sem):
        x, y, c, chips = _place()
        me, sibling = (x, y, c), (x, y, 1 - c)

        def rows(px, py, pc):
            return out_ref.at[pl.ds((4 * px + 2 * py + pc) * m, m), :]

        def copy(k, block, to, src=None):
            return pltpu.make_async_remote_copy(
                src_ref=rows(*block) if src is None else src, dst_ref=rows(*block), send_sem=send_sems.at[k],
                recv_sem=recv_sems.at[k], device_id=to, device_id_type=MESH_ID)

        mine = pltpu.make_async_copy(x_ref, rows(*me), local_sem)
        mine.start()
        first = [copy(0, me, sibling, src=x_ref)]
        first += [copy(1 + j, me, (*chip, c), src=x_ref) for j, chip in enumerate(chips)]
        for cp in first:
            cp.start()
        passed = [copy(4 + j, (*chip, c), sibling) for j, chip in enumerate(chips)]
        for j, chip in enumerate(chips):
            copy(1 + j, (*chip, c), me).wait_recv()
            passed[j].start()
        copy(0, sibling, me).wait_recv()
        for j, chip in enumerate(chips):
            copy(4 + j, (*chip, 1 - c), me).wait_recv()
        for cp in first + passed:
            cp.wait_send()
        mine.wait()

    return pl.pallas_call(
        body, name=name, out_shape=_sds((N_DEV * m, n), blk.dtype), in_specs=[ANY], out_specs=ANY,
        scratch_shapes=[pltpu.SemaphoreType.DMA((7,)), pltpu.SemaphoreType.DMA((7,)), pltpu.SemaphoreType.DMA],
    )(blk)


def _half(ref, core, split_rows):
    r, c = ref.shape
    if split_rows:
        return ref.at[pl.ds(core * (r // 2), r // 2), :]
    return ref.at[:, pl.ds(core * (c // 2), c // 2)]


def _gather_weights(shards, col_sharded):
    nw = len(shards)

    def body(*refs):
        o_refs = refs[nw:2 * nw]
        send_sems, recv_sems = refs[2 * nw:]
        x, y, c, chips = _place()
        jme = 2 * x + y

        def window(w, j, half):
            kr, nc = shards[w].shape
            if col_sharded[w]:
                nc //= N_CHIPS
                full = o_refs[w].at[:, pl.ds(j * nc, nc)]
            else:
                kr //= N_CHIPS
                full = o_refs[w].at[pl.ds(j * kr, kr), :]
            return _half(full, half, True)

        def remote(k, win, to):
            return pltpu.make_async_remote_copy(src_ref=win, dst_ref=win, send_sem=send_sems.at[k],
                                                recv_sem=recv_sems.at[k], device_id=to, device_id_type=MESH_ID)

        sends = []
        for w in range(nw):
            for k, chip in enumerate(chips):
                sends.append(remote(6 * w + k, window(w, jme, c), (*chip, c)))
                sends[-1].start()
        for w in range(nw):
            for k, (cx, cy) in enumerate(chips):
                got = window(w, 2 * cx + cy, c)
                remote(6 * w + k, got, (cx, cy, c)).wait_recv()
                sends.append(remote(6 * w + 3 + k, got, (x, y, 1 - c)))
                sends[-1].start()
        for w in range(nw):
            for k, (cx, cy) in enumerate(chips):
                remote(6 * w + 3 + k, window(w, 2 * cx + cy, 1 - c), (x, y, 1 - c)).wait_recv()
        for cp in sends:
            cp.wait_send()

    return pl.pallas_call(
        body, name="gather_weights", out_shape=[_sds(s.shape, s.dtype) for s in shards], in_specs=[ANY] * nw,
        out_specs=[ANY] * nw, input_output_aliases={w: w for w in range(nw)},
        scratch_shapes=[pltpu.SemaphoreType.DMA((6 * nw,)), pltpu.SemaphoreType.DMA((6 * nw,))],
    )(*shards)


def _swap_halves(grads, col_sharded):
    nw = len(grads)
    out_shapes = [_sds((g.shape[0] // 2, g.shape[1]) if col else (g.shape[0], g.shape[1] // 2), g.dtype)
                  for g, col in zip(grads, col_sharded)]

    def body(*refs):
        g_refs, o_refs = refs[:nw], refs[nw:2 * nw]
        send_sems, recv_sems = refs[2 * nw:]
        x, y, c, _ = _place()
        copies = [pltpu.make_async_remote_copy(
            src_ref=_half(g_refs[w], 1 - c, col_sharded[w]), dst_ref=o_refs[w], send_sem=send_sems.at[w],
            recv_sem=recv_sems.at[w], device_id=(x, y, 1 - c), device_id_type=MESH_ID) for w in range(nw)]
        for cp in copies:
            cp.start()
        for cp in copies:
            cp.wait()

    return pl.pallas_call(
        body, name="swap_halves", out_shape=out_shapes, in_specs=[ANY] * nw, out_specs=[ANY] * nw,
        scratch_shapes=[pltpu.SemaphoreType.DMA((nw,)), pltpu.SemaphoreType.DMA((nw,))],
    )(*grads)


def _scatter_regions(pairs, col_sharded):
    nw = len(pairs)

    def region_shape(p, col):
        return (p.shape[0], p.shape[1] // N_CHIPS) if col else (p.shape[0] // N_CHIPS, p.shape[1])

    out_shapes = [_sds((N_CHIPS - 1, *region_shape(p, col)), p.dtype) for p, col in zip(pairs, col_sharded)]

    def body(*refs):
        p_refs, o_refs = refs[:nw], refs[nw:2 * nw]
        send_sems, recv_sems = refs[2 * nw:]
        x, y, c, chips = _place()

        def region(w, j):
            rr, rc = region_shape(pairs[w], col_sharded[w])
            if col_sharded[w]:
                return p_refs[w].at[:, pl.ds(j * rc, rc)]
            return p_refs[w].at[pl.ds(j * rr, rr), :]

        sends = []
        for w in range(nw):
            for k, (cx, cy) in enumerate(chips):
                sends.append(pltpu.make_async_remote_copy(
                    src_ref=region(w, 2 * cx + cy), dst_ref=o_refs[w].at[k], send_sem=send_sems.at[3 * w + k],
                    recv_sem=recv_sems.at[3 * w + k], device_id=(cx, cy, c), device_id_type=MESH_ID))
                sends[-1].start()
        for cp in sends:
            cp.wait()

    return pl.pallas_call(
        body, name="scatter_regions", out_shape=out_shapes, in_specs=[ANY] * nw, out_specs=[ANY] * nw,
        scratch_shapes=[pltpu.SemaphoreType.DMA((3 * nw,)), pltpu.SemaphoreType.DMA((3 * nw,))],
    )(*pairs)


def _join_halves(halves, col_sharded):
    nw = len(halves)

    def body(*refs):
        o_refs = refs[nw:2 * nw]
        send_sems, recv_sems = refs[2 * nw:]
        x, y, c, _ = _place()
        sends = []
        for w in range(nw):
            mine = _half(o_refs[w], c, col_sharded[w])
            sends.append(pltpu.make_async_remote_copy(
                src_ref=mine, dst_ref=mine, send_sem=send_sems.at[w], recv_sem=recv_sems.at[w],
                device_id=(x, y, 1 - c), device_id_type=MESH_ID))
            sends[-1].start()
        for w in range(nw):
            theirs = _half(o_refs[w], 1 - c, col_sharded[w])
            pltpu.make_async_remote_copy(
                src_ref=theirs, dst_ref=theirs, send_sem=send_sems.at[w], recv_sem=recv_sems.at[w],
                device_id=(x, y, 1 - c), device_id_type=MESH_ID).wait_recv()
        for cp in sends:
            cp.wait_send()

    return pl.pallas_call(
        body, name="join_halves", out_shape=[_sds(h.shape, h.dtype) for h in halves], in_specs=[ANY] * nw,
        out_specs=[ANY] * nw, input_output_aliases={w: w for w in range(nw)},
        scratch_shapes=[pltpu.SemaphoreType.DMA((nw,)), pltpu.SemaphoreType.DMA((nw,))],
    )(*halves)


def _part_rows(size):
    return -(-size // (SUBLANES * LANES)) * SUBLANES


def _pack(arrays, pad_rows_to=SUBLANES):
    flat = [jnp.pad(a.reshape(-1), (0, _part_rows(a.size) * LANES - a.size)).reshape(-1, LANES) for a in arrays]
    rows = sum(f.shape[0] for f in flat)
    pad = (-rows) % pad_rows_to
    if pad:
        flat.append(jnp.zeros((pad, LANES), F32))
    return jnp.concatenate(flat, axis=0)


def _unpack(packed, shapes):
    out, r = [], 0
    for shp in shapes:
        size = math.prod(shp)
        out.append(packed[r:r + _part_rows(size)].reshape(-1)[:size].reshape(shp))
        r += _part_rows(size)
    return out


def _rope_tables(n_ctx_rows, s):
    rows = s // GRID_W
    row_idx = jnp.repeat(jnp.arange(rows), GRID_W)
    col_idx = jnp.tile(jnp.arange(GRID_W), rows)
    n_freq = LANES // 4
    inv_freq = ROPE_THETA ** (-jnp.arange(n_freq, dtype=F32) / n_freq)
    ang = jnp.concatenate([row_idx.astype(F32)[:, None] * inv_freq, col_idx.astype(F32)[:, None] * inv_freq], axis=-1)
    cos = jnp.repeat(jnp.cos(ang), 2, axis=-1)
    sin = jnp.repeat(jnp.sin(ang), 2, axis=-1) * jnp.tile(jnp.array([-1.0, 1.0], F32), LANES // 2)
    cos = jnp.concatenate([jnp.ones((n_ctx_rows, LANES), F32), cos], axis=0)
    sin = jnp.concatenate([jnp.zeros((n_ctx_rows, LANES), F32), sin], axis=0)
    return cos, sin


WEIGHT_NAMES = ['c_ctx', 'w_mod', 'b_mod', 'g_mix', 'g_mlp', 'w_in', 'q_gain', 'k_gain', 'conv_w', 'conv_b', 'w_rg',
                'b_rg', 'w_ig', 'b_ig', 'lru_lambda', 'w_o_attn', 'w_o_rnn', 'w_out', 'w_up', 'w_down', 'g_final']
BIG = ['w_in', 'w_o_attn', 'w_o_rnn', 'w_out', 'w_up', 'w_down']
BIG_COL_SHARDED = [True, False, False, False, True, False]
GATES = ['w_rg', 'w_ig']
SMALL = ['c_ctx', 'b_mod', 'g_mix', 'g_mlp', 'q_gain', 'k_gain', 'conv_b', 'g_final',
         'conv_w', 'b_rg', 'b_ig', 'lru_lambda']


def kernel(x, c, ctx, c_ctx, w_mod, b_mod, g_mix, g_mlp, w_in, q_gain, k_gain, conv_w, conv_b, w_rg, b_rg, w_ig, b_ig, lru_lambda, w_o_attn, w_o_rnn, w_out, w_up, w_down, g_final, loss_target, m_c_ctx, m_w_mod, m_b_mod, m_g_mix, m_g_mlp, m_w_in, m_q_gain, m_k_gain, m_conv_w, m_conv_b, m_w_rg, m_b_rg, m_w_ig, m_b_ig, m_lru_lambda, m_w_o_attn, m_w_o_rnn, m_w_out, m_w_up, m_w_down, m_g_final, v_c_ctx, v_w_mod, v_b_mod, v_g_mix, v_g_mlp, v_w_in, v_q_gain, v_k_gain, v_conv_w, v_conv_b, v_w_rg, v_b_rg, v_w_ig, v_b_ig, v_lru_lambda, v_w_o_attn, v_w_o_rnn, v_w_out, v_w_up, v_w_down, v_g_final):
    given = dict(locals())
    weights = {n: given[n] for n in WEIGHT_NAMES}
    moms = {n: given["m_" + n] for n in WEIGHT_NAMES}
    vars_ = {n: given["v_" + n] for n in WEIGHT_NAMES}

    s, d = x.shape[1], x.shape[2]
    n_ctx = ctx.shape[1]
    t = n_ctx + s
    hd = q_gain.shape[1]
    assert hd == LANES and w_rg.shape[-1] == LANES
    attn_w = w_o_attn.shape[1] * N_CHIPS
    n_in = w_in.shape[2] * N_CHIPS
    kv_w = (n_in - attn_w - 4 * d) // 2
    group = attn_w // kv_w
    k_off, v_off, xr_off = attn_w, attn_w + kv_w, attn_w + 2 * kv_w
    xg_off, gl_off = xr_off + d, xr_off + 2 * d
    d_mod = N_MOD * d
    tr = _tile(math.gcd(n_ctx, s), 256, 16)
    tcol = _tile(math.gcd(d, xr_off), 512, LANES)
    xi, yi, ci = lax.axis_index("x"), lax.axis_index("y"), lax.axis_index("c")
    chip = 2 * xi + yi
    core = ci.astype(jnp.int32).reshape(1)

    sharded_small = [conv_w[0], b_rg[0], b_ig[0], lru_lambda[0]]
    pack0 = _pack([c[0]] + sharded_small)
    got0 = _all_gather8(pack0, "gather_small_inputs").reshape(N_DEV, -1, LANES)
    c_all = got0[:, :_part_rows(d)].reshape(N_DEV, -1)[:, :d]
    per_chip = [_unpack(got0[2 * j, _part_rows(d):], [a.shape for a in sharded_small]) for j in range(N_CHIPS)]
    conv_w_f, b_rg_f, b_ig_f, lam_f = (jnp.concatenate([per_chip[j][i] for j in range(N_CHIPS)], axis=-1)
                                       for i in range(4))
    c16 = jnp.concatenate([c_all, c_ctx[None, :], jnp.zeros((16 - N_DEV - 1, d), F32)], axis=0)
    b_mod_shard = lax.dynamic_slice(b_mod, (0, chip * (d_mod // N_CHIPS)), (1, d_mod // N_CHIPS))
    mod_part, silu16 = _mod_fwd(c16, w_mod[0], b_mod_shard)
    mod_all = _all_gather8(mod_part, "gather_mod").reshape(N_DEV, 16, d_mod // N_CHIPS)
    mod16 = jnp.concatenate([mod_all[2 * j] for j in range(N_CHIPS)], axis=-1)
    me = 4 * xi + 2 * yi + ci
    mod_lat = lax.dynamic_slice(mod16, (me, 0), (1, d_mod)).reshape(N_MOD, d)
    mod_ctx = mod16[N_DEV].reshape(N_MOD, d)
    mod4 = jnp.stack([mod_ctx[0], mod_ctx[1], mod_lat[0], mod_lat[1]])
    mod3 = jnp.stack([mod_lat[2], mod_lat[3], mod_lat[4]])
    gate_f = mod_lat[5][None, :]

    chip_arr = chip.astype(jnp.int32).reshape(1)
    own_only = [_cast_into_window(weights[n][0], chip_arr, col, "cast_" + n) for n, col in zip(BIG, BIG_COL_SHARDED)]
    w_in_f, w_oa_f, w_or_f, w_out_f, w_up_f, w_down_f = _gather_weights(own_only, BIG_COL_SHARDED)

    xcat = jnp.concatenate([ctx[0], x[0]], axis=0)
    cos, sin = _rope_tables(n_ctx, s)
    h = _norm_mod_fwd(xcat, g_mix, mod4, n_ctx, tr)
    z = _matmul(h, w_in_f, name="mm_in")
    qr = _head_prep_fwd(z, 0, attn_w // LANES, q_gain, cos, sin, tr, "q_prep")
    kr = _head_prep_fwd(z, k_off, kv_w // LANES, k_gain, cos, sin, tr, "k_prep")
    attn_o = _attn_fwd(qr, kr, z, v_off, n_ctx, group, tr)
    xc, a_f, bx_f, a_b, bx_b = _rnn_prep(z, xr_off, conv_w_f, conv_b, w_rg[0], b_rg_f, w_ig[0], b_ig_f, lam_f, n_ctx)
    h_f = _scan(a_f, bx_f, order="ctx_lat_up", post=False, n_ctx_rows=n_ctx, name="scan_f")
    h_b = _scan(a_b, bx_b, order="ctx_lat_down", post=False, n_ctx_rows=n_ctx, name="scan_b")
    u = _rnn_gate_fwd(h_f, h_b, z, xg_off, n_ctx, tr, tcol)
    y_attn = _matmul(attn_o, w_oa_f, name="mm_o_attn")
    y_rnn = _matmul(u, w_or_f, name="mm_o_rnn")
    mrg = _merge_fwd(y_attn, y_rnn, z, gl_off, n_ctx, tr, tcol)
    mix = _matmul(mrg, w_out_f, name="mm_out")
    x1, h2 = _resid_norm_mod_fwd(x[0], mix, g_mlp, mod3, tr)
    up = _matmul(h2, w_up_f, name="mm_up")
    act = _sq_relu(up, tr, tcol)
    down = _matmul(act, w_down_f, name="mm_down")
    dx2, d_down, sums_fin, loss_blk = _final_fwd_bwd(x1, down, loss_target[0], g_final[None, :], gate_f, tr)

    d_act = _matmul(d_down, w_down_f, tb=True, name="mm_d_act")
    g_w_down = _matmul(act, d_down, ta=True, out_dtype=BF16, name="mm_g_down")
    d_up = _sq_relu_bwd(d_act, up, tr, tcol)
    dh2 = _matmul(d_up, w_up_f, tb=True, name="mm_d_h2")
    g_w_up = _matmul(h2, d_up, ta=True, out_dtype=BF16, name="mm_g_up")
    dx1, d_mix, sums2 = _norm_mod_bwd2(x1, dh2, dx2, mix, g_mlp, mod3, tr)
    d_mrg = _matmul(d_mix, w_out_f, tb=True, name="mm_d_mrg")
    g_w_out = _matmul(mrg, d_mix, ta=True, out_dtype=BF16, name="mm_g_out")
    d_ya, d_yr, d_gla, d_glr = _merge_bwd(d_mrg, y_attn, y_rnn, z, gl_off, n_ctx, tr, tcol)
    d_o = _matmul(d_ya, w_oa_f, tb=True, out_dtype=BF16, name="mm_d_o")
    g_w_oa = _matmul(attn_o, d_ya, ta=True, out_dtype=BF16, name="mm_g_o_attn")
    d_u = _matmul(d_yr, w_or_f, tb=True, name="mm_d_u")
    g_w_or = _matmul(u, d_yr, ta=True, out_dtype=BF16, name="mm_g_o_rnn")
    d_rnn, d_xg = _rnn_gate_bwd(d_u, h_f, h_b, z, xg_off, n_ctx, tr, tcol)
    gs_f = _scan(a_f, d_rnn, order="lat_ctx_down", post=True, n_ctx_rows=n_ctx, name="scan_f_bwd")
    gs_b = _scan(a_b, d_rnn, order="lat_ctx_up", post=True, n_ctx_rows=n_ctx, name="scan_b_bwd")
    d_xr, g_w_rg, g_w_ig, sums_rnn = _rnn_bwd(z, xr_off, xc, gs_f, gs_b, h_f, h_b, conv_w_f, w_rg[0], b_rg_f, w_ig[0],
                                               b_ig_f, lam_f, n_ctx)
    dq, dk, dv = _attn_bwd(qr, kr, z, v_off, d_o, n_ctx, group, tr)
    d_q_raw, g_q_gain = _head_prep_bwd(z, 0, attn_w // LANES, q_gain, cos, sin, dq, n_ctx, tr, "q_prep_bwd")
    d_k_raw, g_k_gain = _head_prep_bwd(z, k_off, kv_w // LANES, k_gain, cos, sin, dk, 0, tr, "k_prep_bwd")
    zero_ctx = lambda w: jnp.zeros((n_ctx, w), BF16)
    dz = jnp.concatenate([
        jnp.concatenate([zero_ctx(attn_w), d_q_raw], axis=0), d_k_raw, dv.astype(BF16), d_xr,
        jnp.concatenate([zero_ctx(d), d_xg], axis=0), jnp.concatenate([zero_ctx(d), d_gla], axis=0),
        jnp.concatenate([zero_ctx(d), d_glr], axis=0)], axis=1)
    dh = _matmul(dz, w_in_f, tb=True, name="mm_d_h")
    g_w_in = _matmul(h, dz, ta=True, out_dtype=BF16, name="mm_g_in")
    grad_x, sums1 = _norm_mod_bwd1(xcat, dh, dx1, g_mix, mod4, n_ctx, tr)

    zeros_d = jnp.zeros((d,), F32)
    dmod_lat = jnp.concatenate([sums1[0], sums1[1], sums2[3], sums2[0], sums2[1], sums_fin[1]])
    dmod_ctx = jnp.concatenate([sums1[3], sums1[4]] + [zeros_d] * 4)
    small_parts = [dmod_lat, dmod_ctx, loss_blk[0, 0:1], sums1[2] + sums1[5], sums2[2], g_q_gain[0], g_k_gain[0],
                   sums_rnn[10], sums_fin[0], sums_rnn[6:10], sums_rnn[0:2], sums_rnn[2:4], sums_rnn[4:6]]
    pack1 = _pack(small_parts)
    got1 = _all_gather8(pack1, "gather_small_grads").reshape(N_DEV, -1, LANES)
    tot1 = _sum_leading(got1, "sum_small_grads")
    part_shapes = [a.shape for a in small_parts]
    (s_dmod_lat, s_dmod_ctx, s_loss, g_g_mix, g_g_mlp, g_q_gain, g_k_gain, g_conv_b, g_g_final,
     g_conv_w_f, g_b_rg_f, g_b_ig_f, g_lam_f) = _unpack(tot1, part_shapes)
    loss = s_loss[0]
    g_b_mod = (s_dmod_lat + s_dmod_ctx)[None, :]
    n_mod_rows = _part_rows(d_mod)
    dmod16 = jnp.concatenate([got1[:, :n_mod_rows].reshape(N_DEV, -1)[:, :d_mod], s_dmod_ctx[None, :],
                              jnp.zeros((16 - N_DEV - 1, d_mod), F32)], axis=0)
    dmod16_shard = lax.dynamic_slice(dmod16, (0, chip * (d_mod // N_CHIPS)), (16, d_mod // N_CHIPS))
    g_w_mod = _matmul(silu16, dmod16_shard, ta=True, name="mm_g_mod")
    dsilu_part = _matmul(dmod16_shard[N_DEV:], w_mod[0], tb=True, name="mm_d_silu")
    dsilu_all = _all_gather8(dsilu_part, "gather_d_silu").reshape(N_DEV, 8, d)
    g_c_ctx = _c_ctx_grad(dsilu_all, c_ctx[None, :])[0]

    def shard_of(full):
        w = full.shape[-1] // N_CHIPS
        return lax.dynamic_slice(full, (0, chip * w), (full.shape[0], w))

    grads = {
        'c_ctx': g_c_ctx, 'b_mod': g_b_mod, 'g_mix': g_g_mix[None, :], 'g_mlp': g_g_mlp[None, :],
        'q_gain': g_q_gain[None, :], 'k_gain': g_k_gain[None, :], 'conv_b': g_conv_b[None, :],
        'g_final': g_g_final,
        'conv_w': shard_of(g_conv_w_f)[None], 'b_rg': shard_of(g_b_rg_f)[None], 'b_ig': shard_of(g_b_ig_f)[None],
        'lru_lambda': shard_of(g_lam_f)[None], 'w_mod': g_w_mod[None],
    }

    gate_cols = 8 * LANES if g_w_rg.size % (8 * LANES * N_CHIPS * 16) == 0 else 2 * LANES
    gate_rows = g_w_rg.size // gate_cols
    rs_names = BIG + GATES
    rs_col = BIG_COL_SHARDED + [False] * len(GATES)
    full_grads = [g_w_in, g_w_oa, g_w_or, g_w_out, g_w_up, g_w_down,
                  g_w_rg.reshape(gate_rows, -1), g_w_ig.reshape(gate_rows, -1)]
    from_sibling = _swap_halves(full_grads, rs_col)
    pair_sums = [_add_half(g, o, core, col, "pair_sum_" + n)
                 for g, o, col, n in zip(full_grads, from_sibling, rs_col, rs_names)]
    regions = _scatter_regions(pair_sums, rs_col)
    place = jnp.stack([chip, ci]).astype(jnp.int32)
    half_sums = [_sum_regions(p, r, place, col, "chip_sum_" + n)
                 for p, r, col, n in zip(pair_sums, regions, rs_col, rs_names)]
    shard_grads = _join_halves(half_sums, rs_col)
    for n, g in zip(BIG, shard_grads):
        grads[n] = g[None]
    gate_all = _all_gather8(jnp.concatenate(shard_grads[len(BIG):], axis=0), "gather_gate_grads")
    gate_all = gate_all.reshape(N_CHIPS, 2, len(GATES), gate_rows // N_CHIPS, gate_cols)[:, 0]
    for i, n in enumerate(GATES):
        grads[n] = gate_all[:, i].reshape(weights[n].shape)

    delta, new_m, new_v = {}, {}, {}
    for n in BIG + ['w_mod'] + GATES:
        shp = weights[n].shape
        as2d = (lambda a: a[0]) if n not in GATES else (lambda a: a.reshape(-1, LANES))
        dl, nm, nv = _adamw(as2d(weights[n]), as2d(grads[n]), as2d(moms[n]), as2d(vars_[n]), "adamw_" + n)
        delta[n], new_m[n], new_v[n] = dl.reshape(shp), nm.reshape(shp), nv.reshape(shp)
    small_shapes = [weights[n].shape for n in SMALL]
    packed = [_pack([src[n] for n in SMALL], 512) for src in (weights, grads, moms, vars_)]
    outs = _adamw(*packed, "adamw_small")
    for res, out in zip((delta, new_m, new_v), outs):
        for n, a in zip(SMALL, _unpack(out, small_shapes)):
            res[n] = a
    return (loss, grad_x[None], *[grads[n] for n in WEIGHT_NAMES], *[delta[n] for n in WEIGHT_NAMES],
            *[new_m[n] for n in WEIGHT_NAMES], *[new_v[n] for n in WEIGHT_NAMES])
```

```python
import functools
import math
from typing import Callable, NamedTuple

import jax
import jax.numpy as jnp
from jax import lax
from jax.experimental import pallas as pl
from jax.experimental.pallas import tpu as pltpu

F32 = jnp.float32
BF16 = jnp.bfloat16
MESH_ID = pl.DeviceIdType.MESH
ANY = pl.BlockSpec(memory_space=pl.ANY)

NORM_EPS = 1e-6
LRU_C = 8.0
GRID_W = 64
ROPE_THETA = 10000.0
N_MOD = 6
CONV_WIDTH = 4
ADAM_LR = 0.001
ADAM_B1 = 0.9
ADAM_B2 = 0.999
ADAM_EPS = 1e-08
ADAM_WD = 0.01
ADAM_STEP = 10

LANES = 128
SUBLANES = 8
V7X_VMEM_LIMIT = 48 * 1024 * 1024
N_CHIPS = 4
N_DEV = 8
GELU_C = math.sqrt(2.0 / math.pi)
GELU_A = 0.044715


def _tile(dim, pref, align):
    t = min(pref, dim)
    t -= t % align
    while t >= align:
        if dim % t == 0:
            return t
        t -= align
    return dim


def _params(*sem):
    return pltpu.CompilerParams(dimension_semantics=sem, vmem_limit_bytes=V7X_VMEM_LIMIT)


def _sds(shape, dtype=F32):
    return jax.ShapeDtypeStruct(shape, dtype)


class _Side(NamedTuple):
    operands: tuple
    results: tuple
    aliases: dict
    n_sems: int
    build: Callable


def _sides(*sides):
    ops, res, aliases, spans, n = [], [], {}, [], 0
    for s in sides:
        spans.append((len(ops), len(res), n))
        aliases.update({len(ops) + i: len(res) + j for i, j in s.aliases.items()})
        ops += s.operands
        res += s.results
        n += s.n_sems

    def build(op_refs, res_refs, send_sems, recv_sems, sem0):
        sends, recvs = [], []
        for s, (o, r, k) in zip(sides, spans):
            a, b = s.build(op_refs[o:o + len(s.operands)], res_refs[r:r + len(s.results)], send_sems, recv_sems,
                           sem0 + k)
            sends += a
            recvs += b
        return sends, recvs

    return _Side(tuple(ops), tuple(res), aliases, n, build)


def _call(body, *, side=None, sem=(), grid=(), in_specs=(), out_specs=(), out_shape=(), scratch_shapes=(), **kw):
    if side is None:
        return pl.pallas_call(body, grid=grid, in_specs=list(in_specs), out_specs=out_specs, out_shape=out_shape,
                              scratch_shapes=list(scratch_shapes), compiler_params=_params(*sem), **kw)
    many = isinstance(out_shape, (list, tuple))
    out_specs_l, out_shape_l = (list(out_specs), list(out_shape)) if many else ([out_specs], [out_shape])
    n_in, n_out, n_scr = len(in_specs), len(out_shape_l), len(scratch_shapes)
    n_op, n_res = len(side.operands), len(side.results)

    def hosted(*refs):
        ins, ops = refs[:n_in], refs[n_in:n_in + n_op]
        outs = refs[n_in + n_op:n_in + n_op + n_out]
        res = refs[n_in + n_op + n_out:n_in + n_op + n_out + n_res]
        scr = refs[n_in + n_op + n_out + n_res:-2]
        send_sems, recv_sems = refs[-2:]

        def start():
            for cp in side.build(ops, res, send_sems, recv_sems, 0)[0]:
                cp.start()

        def finish():
            sends, recvs = side.build(ops, res, send_sems, recv_sems, 0)
            for cp in recvs:
                cp.wait_recv()
            for cp in sends:
                cp.wait_send()

        if not grid:
            start()
            finish()
            return
        ids = [pl.program_id(a) for a in range(len(grid))]
        first = functools.reduce(jnp.logical_and, [i == 0 for i in ids])
        last = functools.reduce(jnp.logical_and, [i == g - 1 for i, g in zip(ids, grid)])
        pl.when(first)(start)
        body(*ins, *outs, *scr)
        pl.when(last)(finish)

    def run(*args):
        got = pl.pallas_call(
            hosted, grid=grid, in_specs=[*in_specs, *[ANY] * n_op], out_specs=[*out_specs_l, *[ANY] * n_res],
            out_shape=[*out_shape_l, *side.results],
            scratch_shapes=[*scratch_shapes, pltpu.SemaphoreType.DMA((side.n_sems,)),
                            pltpu.SemaphoreType.DMA((side.n_sems,))],
            input_output_aliases={n_in + i: n_out + j for i, j in side.aliases.items()},
            compiler_params=_params(*["arbitrary"] * len(grid)), **kw)(*args, *side.operands)
        own = list(got[:n_out]) if many else got[0]
        return own, list(got[n_out:])

    return run


def _matmul(a, b, *, ta=False, tb=False, out_dtype=F32, name, tm=1024, tn=1024, tk=512, side=None):
    k_dim, m = a.shape if ta else a.shape[::-1]
    n, k2 = b.shape if tb else b.shape[::-1]
    assert k_dim == k2, (a.shape, b.shape, ta, tb)
    tm = _tile(m, tm, LANES if ta else 16)
    tn = _tile(n, tn, 16 if tb else LANES)
    tk = _tile(k_dim, tk, LANES)
    nk = k_dim // tk
    dims = (((0 if ta else 1,), (1 if tb else 0,)), ((), ()))

    def body(a_ref, b_ref, o_ref, acc_ref):
        k = pl.program_id(2)

        @pl.when(k == 0)
        def _():
            acc_ref[...] = jnp.zeros_like(acc_ref)

        acc_ref[...] += lax.dot_general(a_ref[...].astype(BF16), b_ref[...].astype(BF16), dims,
                                        preferred_element_type=F32)

        @pl.when(k == nk - 1)
        def _():
            o_ref[...] = acc_ref[...].astype(o_ref.dtype)

    a_spec = pl.BlockSpec((tk, tm), lambda i, j, k: (k, i)) if ta else pl.BlockSpec((tm, tk), lambda i, j, k: (i, k))
    b_spec = pl.BlockSpec((tn, tk), lambda i, j, k: (j, k)) if tb else pl.BlockSpec((tk, tn), lambda i, j, k: (k, j))
    return _call(
        body, side=side, name=name, grid=(m // tm, n // tn, nk), in_specs=[a_spec, b_spec],
        out_specs=pl.BlockSpec((tm, tn), lambda i, j, k: (i, j)), out_shape=_sds((m, n), out_dtype),
        scratch_shapes=[pltpu.VMEM((tm, tn), F32)], sem=("parallel", "parallel", "arbitrary"),
    )(a, b)


def _silu(x):
    return x * jax.nn.sigmoid(x)


def _gelu(x):
    return 0.5 * x * (1.0 + jnp.tanh(GELU_C * (x + GELU_A * x * x * x)))


def _gelu_grad(x):
    t = jnp.tanh(GELU_C * (x + GELU_A * x * x * x))
    return 0.5 * (1.0 + t) + 0.5 * x * (1.0 - t * t) * GELU_C * (1.0 + 3.0 * GELU_A * x * x)


def _expm1_nonpos(x):
    series = x * (1.0 + x * (1.0 / 2 + x * (1.0 / 6 + x * (1.0 / 24 + x * (1.0 / 120 + x * (1.0 / 720 + x / 5040))))))
    return jnp.where(x > -0.25, series, jnp.exp(x) - 1.0)


def _softplus(x):
    return jnp.maximum(x, 0.0) + jnp.log1p(jnp.exp(-jnp.abs(x)))


def _rms_stats(x):
    return lax.rsqrt(jnp.mean(x * x, axis=-1, keepdims=True) + NORM_EPS)


def _rms_bwd(dxhat, xhat, rstd):
    return rstd * (dxhat - xhat * jnp.mean(dxhat * xhat, axis=-1, keepdims=True))


def _colsum(v):
    return jnp.sum(v, axis=0, keepdims=True)


def _mod_fwd(c16, w_mod, b_mod_shard):
    r, d = c16.shape
    n = w_mod.shape[1]
    tn = _tile(n, 512, LANES)

    def body(c_ref, w_ref, b_ref, o_ref, s_ref):
        s = _silu(c_ref[...])
        s_ref[...] = s
        o_ref[...] = jnp.dot(s.astype(BF16), w_ref[...].astype(BF16), preferred_element_type=F32) + b_ref[...]

    return pl.pallas_call(
        body, name="mod_fwd", grid=(n // tn,),
        in_specs=[pl.BlockSpec((r, d), lambda j: (0, 0)), pl.BlockSpec((d, tn), lambda j: (0, j)),
                  pl.BlockSpec((1, tn), lambda j: (0, j))],
        out_specs=[pl.BlockSpec((r, tn), lambda j: (0, j)), pl.BlockSpec((r, d), lambda j: (0, 0))],
        out_shape=[_sds((r, n)), _sds((r, d))], compiler_params=_params("arbitrary"),
    )(c16, w_mod, b_mod_shard)


def _c_ctx_grad(parts, c_ctx_row):
    d = c_ctx_row.shape[1]

    def body(p_ref, c_ref, o_ref):
        tot = p_ref[0, 0:1, :]
        for chip in range(1, N_CHIPS):
            tot = tot + p_ref[2 * chip, 0:1, :]
        c = c_ref[...]
        sg = jax.nn.sigmoid(c)
        o_ref[...] = tot * (sg * (1.0 + c * (1.0 - sg)))

    return pl.pallas_call(body, name="c_ctx_grad", out_shape=_sds((1, d)), compiler_params=_params())(parts, c_ctx_row)


def _norm_mod_fwd(xcat, g, mod4, n_ctx_rows, tr):
    t, d = xcat.shape
    nctx = n_ctx_rows // tr

    def body(x_ref, g_ref, mod_ref, h_ref):
        x = x_ref[...]
        n = x * _rms_stats(x) * g_ref[...]
        is_ctx = pl.program_id(0) < nctx
        sh = jnp.where(is_ctx, mod_ref[0:1, :], mod_ref[2:3, :])
        sc = jnp.where(is_ctx, mod_ref[1:2, :], mod_ref[3:4, :])
        h_ref[...] = (n * (1.0 + sc) + sh).astype(BF16)

    return pl.pallas_call(
        body, name="norm_mod_fwd", grid=(t // tr,),
        in_specs=[pl.BlockSpec((tr, d), lambda i: (i, 0)), pl.BlockSpec((1, d), lambda i: (0, 0)),
                  pl.BlockSpec((4, d), lambda i: (0, 0))],
        out_specs=pl.BlockSpec((tr, d), lambda i: (i, 0)), out_shape=_sds((t, d), BF16),
        compiler_params=_params("parallel"),
    )(xcat, g, mod4)


def _norm_mod_bwd1(xcat, dh, dx1, g, mod4, n_ctx_rows, tr):
    t, d = xcat.shape
    nctx = n_ctx_rows // tr
    s = t - n_ctx_rows

    def body(x_ref, dh_ref, dx1_ref, g_ref, mod_ref, dx_ref, sums_ref):
        i = pl.program_id(0)
        is_ctx = i < nctx

        @pl.when(i == 0)
        def _():
            sums_ref[...] = jnp.zeros_like(sums_ref)

        x = x_ref[...]
        dh_ = dh_ref[...]
        rstd = _rms_stats(x)
        xhat = x * rstd
        gg = g_ref[...]
        sc = jnp.where(is_ctx, mod_ref[1:2, :], mod_ref[3:4, :])
        dxhat = dh_ * (1.0 + sc) * gg
        dx_ref[...] = dx1_ref[...] + _rms_bwd(dxhat, xhat, rstd)
        part = [_colsum(dh_), _colsum(dh_ * xhat * gg), _colsum(dh_ * (1.0 + sc) * xhat)]

        @pl.when(is_ctx)
        def _():
            for k, row in enumerate(part):
                sums_ref[3 + k:4 + k, :] += row

        @pl.when(jnp.logical_not(is_ctx))
        def _():
            for k, row in enumerate(part):
                sums_ref[k:k + 1, :] += row

    lat = lambda i: (jnp.maximum(i - nctx, 0), 0)
    return pl.pallas_call(
        body, name="norm_mod_bwd1", grid=(t // tr,),
        in_specs=[pl.BlockSpec((tr, d), lambda i: (i, 0)), pl.BlockSpec((tr, d), lambda i: (i, 0)),
                  pl.BlockSpec((tr, d), lat), pl.BlockSpec((1, d), lambda i: (0, 0)),
                  pl.BlockSpec((4, d), lambda i: (0, 0))],
        out_specs=[pl.BlockSpec((tr, d), lat), pl.BlockSpec((8, d), lambda i: (0, 0))],
        out_shape=[_sds((s, d)), _sds((8, d))], compiler_params=_params("arbitrary"),
    )(xcat, dh, dx1, g, mod4)


def _resid_norm_mod_fwd(x, mix, g, mod3, tr):
    s, d = x.shape

    def body(x_ref, mix_ref, g_ref, mod_ref, x1_ref, h_ref):
        x1 = x_ref[...] + mod_ref[0:1, :] * mix_ref[...]
        x1_ref[...] = x1
        n = x1 * _rms_stats(x1) * g_ref[...]
        h_ref[...] = (n * (1.0 + mod_ref[2:3, :]) + mod_ref[1:2, :]).astype(BF16)

    row = pl.BlockSpec((tr, d), lambda i: (i, 0))
    return pl.pallas_call(
        body, name="resid_norm_mod_fwd", grid=(s // tr,),
        in_specs=[row, row, pl.BlockSpec((1, d), lambda i: (0, 0)), pl.BlockSpec((3, d), lambda i: (0, 0))],
        out_specs=[row, row], out_shape=[_sds((s, d)), _sds((s, d), BF16)], compiler_params=_params("parallel"),
    )(x, mix, g, mod3)


def _norm_mod_bwd2(x1, dh2, dx2, mix, g, mod3, tr):
    s, d = x1.shape

    def body(x_ref, dh_ref, dx2_ref, mix_ref, g_ref, mod_ref, dx1_ref, dmix_ref, sums_ref):
        @pl.when(pl.program_id(0) == 0)
        def _():
            sums_ref[...] = jnp.zeros_like(sums_ref)

        x = x_ref[...]
        dh_ = dh_ref[...]
        rstd = _rms_stats(x)
        xhat = x * rstd
        gg = g_ref[...]
        sc = mod_ref[2:3, :]
        dx1 = dx2_ref[...] + _rms_bwd(dh_ * (1.0 + sc) * gg, xhat, rstd)
        dx1_ref[...] = dx1
        dmix_ref[...] = (dx1 * mod_ref[0:1, :]).astype(BF16)
        part = [_colsum(dh_), _colsum(dh_ * xhat * gg), _colsum(dh_ * (1.0 + sc) * xhat), _colsum(dx1 * mix_ref[...])]
        for k, row in enumerate(part):
            sums_ref[k:k + 1, :] += row

    row = pl.BlockSpec((tr, d), lambda i: (i, 0))
    return pl.pallas_call(
        body, name="norm_mod_bwd2", grid=(s // tr,),
        in_specs=[row, row, row, row, pl.BlockSpec((1, d), lambda i: (0, 0)), pl.BlockSpec((3, d), lambda i: (0, 0))],
        out_specs=[row, row, pl.BlockSpec((8, d), lambda i: (0, 0))],
        out_shape=[_sds((s, d)), _sds((s, d), BF16), _sds((8, d))], compiler_params=_params("arbitrary"),
    )(x1, dh2, dx2, mix, g, mod3)


def _final_fwd_bwd(x1, down, target, g_final, gate, tr):
    s, d = x1.shape

    def body(x1_ref, down_ref, t_ref, g_ref, gate_ref, dx2_ref, ddown_ref, sums_ref, loss_ref):
        @pl.when(pl.program_id(0) == 0)
        def _():
            sums_ref[...] = jnp.zeros_like(sums_ref)
            loss_ref[...] = jnp.zeros_like(loss_ref)

        down_ = down_ref[...]
        gate_ = gate_ref[...]
        x2 = x1_ref[...] + gate_ * down_
        rstd = _rms_stats(x2)
        xhat = x2 * rstd
        gg = g_ref[...]
        err = xhat * gg - t_ref[...]
        loss_ref[...] += 0.5 * jnp.sum(jnp.mean(err * err, axis=-1, keepdims=True))
        dy = err * (1.0 / d)
        dx2 = _rms_bwd(dy * gg, xhat, rstd)
        dx2_ref[...] = dx2
        ddown_ref[...] = (dx2 * gate_).astype(BF16)
        sums_ref[0:1, :] += _colsum(dy * xhat)
        sums_ref[1:2, :] += _colsum(dx2 * down_)

    row = pl.BlockSpec((tr, d), lambda i: (i, 0))
    vec = pl.BlockSpec((1, d), lambda i: (0, 0))
    return pl.pallas_call(
        body, name="final_fwd_bwd", grid=(s // tr,), in_specs=[row, row, row, vec, vec],
        out_specs=[row, row, pl.BlockSpec((8, d), lambda i: (0, 0)), pl.BlockSpec((8, LANES), lambda i: (0, 0))],
        out_shape=[_sds((s, d)), _sds((s, d), BF16), _sds((8, d)), _sds((8, LANES))],
        compiler_params=_params("arbitrary"),
    )(x1, down, target, g_final, gate)


def _swap_pairs(v):
    lane = lax.broadcasted_iota(jnp.int32, v.shape, 1)
    return jnp.where(lane % 2 == 0, pltpu.roll(v, LANES - 1, 1), pltpu.roll(v, 1, 1))


def _head_prep_fwd(z, col_off, n_heads, gain, cos, sin, tr, name, side=None):
    t = z.shape[0]
    hb = col_off // LANES

    def body(z_ref, g_ref, cos_ref, sin_ref, o_ref):
        x = z_ref[...]
        y = x * _rms_stats(x) * g_ref[...]
        o_ref[...] = (y * cos_ref[...] + _swap_pairs(y) * sin_ref[...]).astype(BF16)

    tab = pl.BlockSpec((tr, LANES), lambda i, j: (i, 0))
    return _call(
        body, side=side, name=name, grid=(t // tr, n_heads),
        in_specs=[pl.BlockSpec((tr, LANES), lambda i, j: (i, hb + j)), pl.BlockSpec((1, LANES), lambda i, j: (0, 0)),
                  tab, tab],
        out_specs=pl.BlockSpec((tr, LANES), lambda i, j: (i, j)), out_shape=_sds((t, n_heads * LANES), BF16),
        sem=("parallel", "parallel"),
    )(z, gain, cos, sin)


def _head_prep_bwd(z, col_off, n_heads, gain, cos, sin, dout, row_off, tr, name, side=None):
    r = dout.shape[0]
    hb = col_off // LANES
    rb = row_off // tr

    def body(z_ref, g_ref, cos_ref, sin_ref, d_ref, dz_ref, dg_ref):
        @pl.when(jnp.logical_and(pl.program_id(0) == 0, pl.program_id(1) == 0))
        def _():
            dg_ref[...] = jnp.zeros_like(dg_ref)

        x = z_ref[...]
        rstd = _rms_stats(x)
        xhat = x * rstd
        dd = d_ref[...]
        dy = dd * cos_ref[...] - _swap_pairs(dd) * sin_ref[...]
        dg_ref[0:1, :] += _colsum(dy * xhat)
        dz_ref[...] = _rms_bwd(dy * g_ref[...], xhat, rstd).astype(BF16)

    tab = pl.BlockSpec((tr, LANES), lambda i, j: (rb + i, 0))
    return _call(
        body, side=side, name=name, grid=(r // tr, n_heads),
        in_specs=[pl.BlockSpec((tr, LANES), lambda i, j: (rb + i, hb + j)),
                  pl.BlockSpec((1, LANES), lambda i, j: (0, 0)), tab, tab,
                  pl.BlockSpec((tr, LANES), lambda i, j: (i, j))],
        out_specs=[pl.BlockSpec((tr, LANES), lambda i, j: (i, j)), pl.BlockSpec((8, LANES), lambda i, j: (0, 0))],
        out_shape=[_sds((r, n_heads * LANES), BF16), _sds((8, LANES))], sem=("arbitrary", "arbitrary"),
    )(z, gain, cos, sin, dout)


def _softmax_rows(q, k, scale):
    s = lax.dot_general(q, k, (((1,), (1,)), ((), ())), preferred_element_type=F32) * scale
    p = jnp.exp(s - jnp.max(s, axis=-1, keepdims=True))
    return p / jnp.sum(p, axis=-1, keepdims=True)


def _attn_fwd(qr, kr, z, v_off, n_ctx_rows, group, tq, side=None):
    t, kvw = kr.shape
    s = t - n_ctx_rows
    n_kv = kvw // LANES
    scale = LANES ** -0.5
    qb0 = n_ctx_rows // tq
    vb = v_off // LANES

    def body(q_ref, k_ref, v_ref, o_ref):
        k = k_ref[...]
        v = v_ref[...].astype(BF16)
        for g in range(group):
            cols = slice(g * LANES, (g + 1) * LANES)
            p = _softmax_rows(q_ref[:, cols], k, scale)
            o_ref[:, cols] = jnp.dot(p.astype(BF16), v, preferred_element_type=F32)

    return _call(
        body, side=side, name="attn_fwd", grid=(n_kv, s // tq),
        in_specs=[pl.BlockSpec((tq, group * LANES), lambda h, i: (qb0 + i, h)),
                  pl.BlockSpec((t, LANES), lambda h, i: (0, h)), pl.BlockSpec((t, LANES), lambda h, i: (0, vb + h))],
        out_specs=pl.BlockSpec((tq, group * LANES), lambda h, i: (i, h)),
        out_shape=_sds((s, n_kv * group * LANES)), sem=("parallel", "parallel"),
    )(qr, kr, z)


def _attn_bwd(qr, kr, z, v_off, d_o, n_ctx_rows, group, tq, side=None):
    t, kvw = kr.shape
    s = t - n_ctx_rows
    n_kv = kvw // LANES
    scale = LANES ** -0.5
    qb0 = n_ctx_rows // tq
    vb = v_off // LANES
    tn_dims = (((0,), (0,)), ((), ()))
    nt_dims = (((1,), (1,)), ((), ()))

    def body(q_ref, k_ref, v_ref, do_ref, dq_ref, dk_ref, dv_ref):
        @pl.when(pl.program_id(1) == 0)
        def _():
            dk_ref[...] = jnp.zeros_like(dk_ref)
            dv_ref[...] = jnp.zeros_like(dv_ref)

        k = k_ref[...]
        v = v_ref[...].astype(BF16)
        for g in range(group):
            cols = slice(g * LANES, (g + 1) * LANES)
            q = q_ref[:, cols]
            do_ = do_ref[:, cols]
            p = _softmax_rows(q, k, scale)
            dv_ref[...] += lax.dot_general(p.astype(BF16), do_, tn_dims, preferred_element_type=F32)
            dp = lax.dot_general(do_, v, nt_dims, preferred_element_type=F32)
            ds = (p * (dp - jnp.sum(p * dp, axis=-1, keepdims=True)) * scale).astype(BF16)
            dq_ref[:, cols] = jnp.dot(ds, k, preferred_element_type=F32)
            dk_ref[...] += lax.dot_general(ds, q, tn_dims, preferred_element_type=F32)

    qspec = pl.BlockSpec((tq, group * LANES), lambda h, i: (qb0 + i, h))
    ospec = pl.BlockSpec((tq, group * LANES), lambda h, i: (i, h))
    kspec = pl.BlockSpec((t, LANES), lambda h, i: (0, h))
    return _call(
        body, side=side, name="attn_bwd", grid=(n_kv, s // tq),
        in_specs=[qspec, kspec, pl.BlockSpec((t, LANES), lambda h, i: (0, vb + h)), ospec],
        out_specs=[ospec, kspec, kspec],
        out_shape=[_sds((s, n_kv * group * LANES)), _sds((t, kvw)), _sds((t, kvw))], sem=("parallel", "arbitrary"),
    )(qr, kr, z, d_o)


def _row_mask(shape, rows):
    r = lax.broadcasted_iota(jnp.int32, shape, 0)
    m = r == rows[0]
    for v in rows[1:]:
        m = jnp.logical_or(m, r == v)
    return m


def _shift_rows(x, k, n_ctx_rows):
    t = x.shape[0]
    if k == 0:
        return x
    rolled = pltpu.roll(x, (-k) % t, 0)
    if k > 0:
        dead = [n_ctx_rows - 1 - i for i in range(k)] + [t - 1 - i for i in range(k)]
    else:
        dead = [i for i in range(-k)] + [n_ctx_rows + i for i in range(-k)]
    return jnp.where(_row_mask(x.shape, dead), 0.0, rolled)


def _conv(x, w, b, n_ctx_rows):
    y = b
    for k in range(CONV_WIDTH):
        y = y + _shift_rows(x, k - 1, n_ctx_rows) * w[k:k + 1, :]
    return y


def _gates(xc_bf, w_r, b_r, w_i, b_i, lam):
    r = jax.nn.sigmoid(jnp.dot(xc_bf, w_r.astype(BF16), preferred_element_type=F32) + b_r)
    i = jax.nn.sigmoid(jnp.dot(xc_bf, w_i.astype(BF16), preferred_element_type=F32) + b_i)
    log_a = -LRU_C * r * _softplus(-lam)
    a = jnp.exp(log_a)
    mult = jnp.sqrt(-_expm1_nonpos(2.0 * log_a))
    return r, i, a, mult


def _rnn_specs(t, xr_off):
    xb = xr_off // LANES
    return dict(
        zcol=pl.BlockSpec((t, LANES), lambda j: (0, xb + j)), col=pl.BlockSpec((t, LANES), lambda j: (0, j)),
        conv_w=pl.BlockSpec((CONV_WIDTH, LANES), lambda j: (0, j)), vec=pl.BlockSpec((1, LANES), lambda j: (0, j)),
        gate_w=pl.BlockSpec((2, 1, LANES, LANES), lambda j: (0, j, 0, 0)), two=pl.BlockSpec((2, LANES), lambda j: (0, j)))


def _rnn_prep(z, xr_off, conv_w, conv_b, w_rg, b_rg, w_ig, b_ig, lam, n_ctx_rows, side=None):
    t = z.shape[0]
    d = conv_b.shape[1]
    sp = _rnn_specs(t, xr_off)

    def body(z_ref, cw_ref, cb_ref, wr_ref, br_ref, wi_ref, bi_ref, lam_ref, xc_ref, af_ref, bf_ref, ab_ref, bb_ref):
        xc = _conv(z_ref[...], cw_ref[...], cb_ref[...], n_ctx_rows)
        xc_ref[...] = xc
        xc_bf = xc.astype(BF16)
        for dr, (a_ref, b_ref) in enumerate(((af_ref, bf_ref), (ab_ref, bb_ref))):
            _, i, a, mult = _gates(xc_bf, wr_ref[dr, 0], br_ref[dr:dr + 1, :], wi_ref[dr, 0], bi_ref[dr:dr + 1, :],
                                   lam_ref[dr:dr + 1, :])
            a_ref[...] = a
            b_ref[...] = mult * (i * xc)

    return _call(
        body, side=side, name="rnn_prep", grid=(d // LANES,),
        in_specs=[sp["zcol"], sp["conv_w"], sp["vec"], sp["gate_w"], sp["two"], sp["gate_w"], sp["two"], sp["two"]],
        out_specs=[sp["col"]] * 5, out_shape=[_sds((t, d))] * 5, sem=("parallel",),
    )(z, conv_w, conv_b, w_rg, b_rg, w_ig, b_ig, lam)


def _scan(a, b, *, order, post, n_ctx_rows, name, tc=256):
    t, d = a.shape
    tc = _tile(math.gcd(n_ctx_rows, t - n_ctx_rows), tc, SUBLANES)
    nt, nctx = t // tc, n_ctx_rows // tc
    nlat = nt - nctx
    b_lat_only = b.shape[0] != t
    up = order.endswith("up")

    def chunk(i):
        if order == "ctx_lat_up":
            return i
        if order == "lat_ctx_down":
            return nt - 1 - i
        if order == "ctx_lat_down":
            return jnp.where(i < nctx, nctx - 1 - i, nt - 1 - (i - nctx))
        return jnp.where(i < nlat, nctx + i, i - nlat)

    def body(a_ref, b_ref, o_ref, carry_ref):
        @pl.when(pl.program_id(0) == 0)
        def _():
            carry_ref[...] = jnp.zeros_like(carry_ref)

        live = jnp.where(chunk(pl.program_id(0)) >= nctx, 1.0, 0.0) if b_lat_only else None

        def group(gi, carry):
            base = pl.multiple_of((gi if up else tc // SUBLANES - 1 - gi) * SUBLANES, SUBLANES)
            for r in (range(SUBLANES) if up else range(SUBLANES - 1, -1, -1)):
                a_r = a_ref[pl.ds(base + r, 1), :]
                b_r = b_ref[pl.ds(base + r, 1), :]
                if live is not None:
                    b_r = b_r * live
                if post:
                    out = b_r + carry
                    carry = a_r * out
                else:
                    out = a_r * carry + b_r
                    carry = out
                o_ref[pl.ds(base + r, 1), :] = out
            return carry

        carry_ref[0:1, :] = lax.fori_loop(0, tc // SUBLANES, group, carry_ref[0:1, :])

    full = pl.BlockSpec((tc, d), lambda i: (chunk(i), 0))
    b_spec = pl.BlockSpec((tc, d), lambda i: (jnp.maximum(chunk(i) - nctx, 0), 0)) if b_lat_only else full
    return pl.pallas_call(
        body, name=name, grid=(nt,), in_specs=[full, b_spec], out_specs=full, out_shape=_sds((t, d)),
        scratch_shapes=[pltpu.VMEM((SUBLANES, d), F32)], compiler_params=_params("arbitrary"),
    )(a, b)


def _rnn_bwd(z, xr_off, xc, g_f, g_b, h_f, h_b, conv_w, w_rg, b_rg, w_ig, b_ig, lam, n_ctx_rows, side=None):
    t, d = xc.shape
    sp = _rnn_specs(t, xr_off)
    tn_dims = (((0,), (0,)), ((), ()))
    nt_dims = (((1,), (1,)), ((), ()))

    def body(z_ref, xc_ref, gf_ref, gb_ref, hf_ref, hb_ref, cw_ref, wr_ref, br_ref, wi_ref, bi_ref, lam_ref,
             dxr_ref, dwr_ref, dwi_ref, sums_ref):
        xc_ = xc_ref[...]
        xc_bf = xc_.astype(BF16)
        dxc = jnp.zeros_like(xc_)
        sums = [None] * 6
        for dr, (g_ref, h_ref) in enumerate(((gf_ref, hf_ref), (gb_ref, hb_ref))):
            w_r, w_i, lam_ = wr_ref[dr, 0], wi_ref[dr, 0], lam_ref[dr:dr + 1, :]
            r, i, a, mult = _gates(xc_bf, w_r, br_ref[dr:dr + 1, :], w_i, bi_ref[dr:dr + 1, :], lam_)
            g = g_ref[...]
            h = h_ref[...]
            if dr == 0:
                h_prev = jnp.where(_row_mask(h.shape, [0]), 0.0, pltpu.roll(h, 1, 0))
            else:
                h_prev = jnp.where(_row_mask(h.shape, [n_ctx_rows - 1]), 0.0, pltpu.roll(h, t - 1, 0))
            d_mult = g * i * xc_
            d_i = g * mult * xc_
            dxc = dxc + g * mult * i
            d_log_a = g * h_prev * a - d_mult * a * a / mult
            sp_ = _softplus(-lam_)
            d_r = d_log_a * (-LRU_C) * sp_
            d_sp = _colsum(d_log_a * (-LRU_C) * r)
            du_r = (d_r * r * (1.0 - r))
            du_i = (d_i * i * (1.0 - i))
            sums[dr] = _colsum(du_r)
            sums[2 + dr] = _colsum(du_i)
            sums[4 + dr] = d_sp * (-jax.nn.sigmoid(-lam_))
            du_r_bf, du_i_bf = du_r.astype(BF16), du_i.astype(BF16)
            dwr_ref[dr, 0] = lax.dot_general(xc_bf, du_r_bf, tn_dims, preferred_element_type=F32).astype(BF16)
            dwi_ref[dr, 0] = lax.dot_general(xc_bf, du_i_bf, tn_dims, preferred_element_type=F32).astype(BF16)
            dxc = dxc + lax.dot_general(du_r_bf, w_r.astype(BF16), nt_dims, preferred_element_type=F32)
            dxc = dxc + lax.dot_general(du_i_bf, w_i.astype(BF16), nt_dims, preferred_element_type=F32)
        xr = z_ref[...]
        cw = cw_ref[...]
        dxr = jnp.zeros_like(dxc)
        rows = list(sums)
        for k in range(CONV_WIDTH):
            dxr = dxr + _shift_rows(dxc, 1 - k, n_ctx_rows) * cw[k:k + 1, :]
            rows.append(_colsum(dxc * _shift_rows(xr, k - 1, n_ctx_rows)))
        rows.append(_colsum(dxc))
        dxr_ref[...] = dxr.astype(BF16)
        sums_ref[...] = jnp.zeros_like(sums_ref)
        for k, row in enumerate(rows):
            sums_ref[k:k + 1, :] = row

    return _call(
        body, side=side, name="rnn_bwd", grid=(d // LANES,),
        in_specs=[sp["zcol"]] + [sp["col"]] * 5 + [sp["conv_w"], sp["gate_w"], sp["two"], sp["gate_w"], sp["two"],
                                                  sp["two"]],
        out_specs=[sp["col"], sp["gate_w"], sp["gate_w"], pl.BlockSpec((16, LANES), lambda j: (0, j))],
        out_shape=[_sds((t, d), BF16), _sds(w_rg.shape, BF16), _sds(w_ig.shape, BF16), _sds((16, d))],
        sem=("parallel",),
    )(z, xc, g_f, g_b, h_f, h_b, conv_w, w_rg, b_rg, w_ig, b_ig, lam)


def _tiles2d(s, d, tr, tcol):
    return (s // tr, d // tcol), pl.BlockSpec((tr, tcol), lambda i, j: (i, j))


def _zspec(tr, tcol, row_off, col_off):
    rb, cb = row_off // tr, col_off // tcol
    return pl.BlockSpec((tr, tcol), lambda i, j: (rb + i, cb + j))


def _rnn_gate_fwd(h_f, h_b, z, xg_off, n_ctx_rows, tr, tcol):
    t, d = h_f.shape
    s = t - n_ctx_rows
    grid, out = _tiles2d(s, d, tr, tcol)
    hs = _zspec(tr, tcol, n_ctx_rows, 0)

    def body(hf_ref, hb_ref, xg_ref, u_ref):
        u_ref[...] = ((hf_ref[...] + hb_ref[...]) * _gelu(xg_ref[...])).astype(BF16)

    return pl.pallas_call(body, name="rnn_gate_fwd", grid=grid, in_specs=[hs, hs, _zspec(tr, tcol, n_ctx_rows, xg_off)],
                          out_specs=out, out_shape=_sds((s, d), BF16), compiler_params=_params("parallel", "parallel"),
                          )(h_f, h_b, z)


def _rnn_gate_bwd(d_u, h_f, h_b, z, xg_off, n_ctx_rows, tr, tcol):
    t, d = h_f.shape
    s = t - n_ctx_rows
    grid, out = _tiles2d(s, d, tr, tcol)
    hs = _zspec(tr, tcol, n_ctx_rows, 0)

    def body(du_ref, hf_ref, hb_ref, xg_ref, dr_ref, dxg_ref):
        du = du_ref[...]
        xg = xg_ref[...]
        dr_ref[...] = du * _gelu(xg)
        dxg_ref[...] = (du * (hf_ref[...] + hb_ref[...]) * _gelu_grad(xg)).astype(BF16)

    return pl.pallas_call(body, name="rnn_gate_bwd", grid=grid,
                          in_specs=[out, hs, hs, _zspec(tr, tcol, n_ctx_rows, xg_off)], out_specs=[out, out],
                          out_shape=[_sds((s, d)), _sds((s, d), BF16)],
                          compiler_params=_params("parallel", "parallel"))(d_u, h_f, h_b, z)


def _merge_fwd(y_attn, y_rnn, z, gl_off, n_ctx_rows, tr, tcol):
    s, d = y_attn.shape
    grid, out = _tiles2d(s, d, tr, tcol)

    def body(ya_ref, yr_ref, ga_ref, gr_ref, o_ref):
        o_ref[...] = (jax.nn.sigmoid(ga_ref[...]) * ya_ref[...] + jax.nn.sigmoid(gr_ref[...]) * yr_ref[...]).astype(BF16)

    return pl.pallas_call(
        body, name="merge_fwd", grid=grid,
        in_specs=[out, out, _zspec(tr, tcol, n_ctx_rows, gl_off), _zspec(tr, tcol, n_ctx_rows, gl_off + d)],
        out_specs=out, out_shape=_sds((s, d), BF16), compiler_params=_params("parallel", "parallel"),
    )(y_attn, y_rnn, z, z)


def _merge_bwd(d_mrg, y_attn, y_rnn, z, gl_off, n_ctx_rows, tr, tcol):
    s, d = y_attn.shape
    grid, out = _tiles2d(s, d, tr, tcol)

    def body(dm_ref, ya_ref, yr_ref, ga_ref, gr_ref, dya_ref, dyr_ref, dga_ref, dgr_ref):
        dm = dm_ref[...]
        ga = jax.nn.sigmoid(ga_ref[...])
        gr = jax.nn.sigmoid(gr_ref[...])
        dya_ref[...] = (dm * ga).astype(BF16)
        dyr_ref[...] = (dm * gr).astype(BF16)
        dga_ref[...] = (dm * ya_ref[...] * ga * (1.0 - ga)).astype(BF16)
        dgr_ref[...] = (dm * yr_ref[...] * gr * (1.0 - gr)).astype(BF16)

    return pl.pallas_call(
        body, name="merge_bwd", grid=grid,
        in_specs=[out, out, out, _zspec(tr, tcol, n_ctx_rows, gl_off), _zspec(tr, tcol, n_ctx_rows, gl_off + d)],
        out_specs=[out] * 4, out_shape=[_sds((s, d), BF16)] * 4, compiler_params=_params("parallel", "parallel"),
    )(d_mrg, y_attn, y_rnn, z, z)


def _sq_relu(up, tr, tcol, side=None):
    grid, out = _tiles2d(*up.shape, tr, tcol)

    def body(u_ref, o_ref):
        r = jnp.maximum(u_ref[...], 0.0)
        o_ref[...] = (r * r).astype(BF16)

    return _call(body, side=side, name="sq_relu", grid=grid, in_specs=[out], out_specs=out,
                 out_shape=_sds(up.shape, BF16), sem=("parallel", "parallel"))(up)


def _sq_relu_bwd(d_act, up, tr, tcol, side=None):
    grid, out = _tiles2d(*up.shape, tr, tcol)

    def body(d_ref, u_ref, o_ref):
        o_ref[...] = (d_ref[...] * 2.0 * jnp.maximum(u_ref[...], 0.0)).astype(BF16)

    return _call(body, side=side, name="sq_relu_bwd", grid=grid, in_specs=[out, out], out_specs=out,
                 out_shape=_sds(up.shape, BF16), sem=("parallel", "parallel"))(d_act, up)


def _cast_into_window(w, chip, col_sharded, name):
    r, c = w.shape
    tr, tcol = _tile(r, 512, 16), _tile(c, 1024, LANES)
    nrb, ncb = r // tr, c // tcol

    def body(chip_ref, w_ref, o_ref):
        o_ref[...] = w_ref[...].astype(BF16)

    if col_sharded:
        omap = lambda i, j, chip_ref: (i, chip_ref[0] * ncb + j)
    else:
        omap = lambda i, j, chip_ref: (chip_ref[0] * nrb + i, j)
    return pl.pallas_call(
        body, name=name,
        grid_spec=pltpu.PrefetchScalarGridSpec(
            num_scalar_prefetch=1, grid=(nrb, ncb),
            in_specs=[pl.BlockSpec((tr, tcol), lambda i, j, chip_ref: (i, j))], out_specs=pl.BlockSpec((tr, tcol), omap)),
        out_shape=_sds((r, c * N_CHIPS) if col_sharded else (r * N_CHIPS, c), BF16),
        compiler_params=_params("parallel", "parallel"),
    )(chip, w)


def _sum_leading(parts, name):
    n, r, c = parts.shape
    tr, tcol = _tile(r, 512, SUBLANES), _tile(c, 1024, LANES)

    def body(p_ref, o_ref):
        tot = p_ref[0]
        for k in range(1, n):
            tot = tot + p_ref[k]
        o_ref[...] = tot

    return pl.pallas_call(
        body, name=name, grid=(r // tr, c // tcol), in_specs=[pl.BlockSpec((n, tr, tcol), lambda i, j: (0, i, j))],
        out_specs=pl.BlockSpec((tr, tcol), lambda i, j: (i, j)), out_shape=_sds((r, c)),
        compiler_params=_params("parallel", "parallel"),
    )(parts)


def _add_half(full, other, core, split_rows, name):
    r, c = other.shape
    tr, tcol = _tile(r, 512, 16), _tile(c, 1024, LANES)
    nrb, ncb = r // tr, c // tcol

    def body(core_ref, f_ref, o_ref, out_ref):
        out_ref[...] = (f_ref[...].astype(F32) + o_ref[...].astype(F32)).astype(out_ref.dtype)

    if split_rows:
        fmap = lambda i, j, core_ref: (core_ref[0] * nrb + i, j)
    else:
        fmap = lambda i, j, core_ref: (i, core_ref[0] * ncb + j)
    same = lambda i, j, core_ref: (i, j)
    return pl.pallas_call(
        body, name=name,
        grid_spec=pltpu.PrefetchScalarGridSpec(
            num_scalar_prefetch=1, grid=(nrb, ncb),
            in_specs=[pl.BlockSpec((tr, tcol), fmap), pl.BlockSpec((tr, tcol), same)],
            out_specs=pl.BlockSpec((tr, tcol), same)),
        out_shape=_sds((r, c), BF16), compiler_params=_params("parallel", "parallel"),
    )(core, full, other)


def _sum_regions(pair, got, place, col_sharded, name):
    _, r, c = got.shape
    tr, tcol = _tile(r, 512, 16), _tile(c, 1024, LANES)
    nrb, ncb = r // tr, c // tcol

    def body(place_ref, p_ref, g_ref, out_ref):
        tot = p_ref[...].astype(F32)
        for k in range(N_CHIPS - 1):
            tot = tot + g_ref[k].astype(F32)
        out_ref[...] = tot

    if col_sharded:
        pmap = lambda i, j, pr: (i, pr[0] * ncb + j)
        omap = lambda i, j, pr: (pr[1] * nrb + i, j)
        out_shape = (2 * r, c)
    else:
        pmap = lambda i, j, pr: (pr[0] * nrb + i, j)
        omap = lambda i, j, pr: (i, pr[1] * ncb + j)
        out_shape = (r, 2 * c)
    return pl.pallas_call(
        body, name=name,
        grid_spec=pltpu.PrefetchScalarGridSpec(
            num_scalar_prefetch=1, grid=(nrb, ncb),
            in_specs=[pl.BlockSpec((tr, tcol), pmap), pl.BlockSpec((N_CHIPS - 1, tr, tcol), lambda i, j, pr: (0, i, j))],
            out_specs=pl.BlockSpec((tr, tcol), omap)),
        out_shape=_sds(out_shape), compiler_params=_params("parallel", "parallel"),
    )(place, pair, got)


def _adamw(w, g, m, v, name, side=None):
    r, c = w.shape
    tr, tcol = _tile(r, 512, SUBLANES), _tile(c, 1024, LANES)
    blk = pl.BlockSpec((tr, tcol), lambda i, j: (i, j))

    def body(w_ref, g_ref, m_ref, v_ref, d_ref, nm_ref, nv_ref):
        g_ = g_ref[...]
        m_ = ADAM_B1 * m_ref[...] + (1.0 - ADAM_B1) * g_
        v_ = ADAM_B2 * v_ref[...] + (1.0 - ADAM_B2) * (g_ * g_)
        m_hat = m_ / (1.0 - ADAM_B1 ** ADAM_STEP)
        v_hat = v_ / (1.0 - ADAM_B2 ** ADAM_STEP)
        d_ref[...] = -ADAM_LR * (m_hat / (jnp.sqrt(v_hat) + ADAM_EPS) + ADAM_WD * w_ref[...])
        nm_ref[...] = m_
        nv_ref[...] = v_

    return _call(body, side=side, name=name, grid=(r // tr, c // tcol), in_specs=[blk] * 4, out_specs=[blk] * 3,
                 out_shape=[_sds((r, c))] * 3, sem=("parallel", "parallel"))(w, g, m, v)


def _place():
    x, y, c = lax.axis_index("x"), lax.axis_index("y"), lax.axis_index("c")
    chips = [(1 - x, y), (x, 1 - y), (1 - x, 1 - y)]
    return x, y, c, chips


def _all_gather8(blk, name):
    m, n = blk.shape

    def body(x_ref, out_ref, send_sems, recv_sems, local_sem):
        x, y, c, chips = _place()
        me, sibling = (x, y, c), (x, y, 1 - c)

        def rows(px, py, pc):
            return out_ref.at[pl.ds((4 * px + 2 * py + pc) * m, m), :]

        def copy(k, block, to, src=None):
            return pltpu.make_async_remote_copy(
                src_ref=rows(*block) if src is None else src, dst_ref=rows(*block), send_sem=send_sems.at[k],
                recv_sem=recv_sems.at[k], device_id=to, device_id_type=MESH_ID)

        mine = pltpu.make_async_copy(x_ref, rows(*me), local_sem)
        mine.start()
        first = [copy(0, me, sibling, src=x_ref)]
        first += [copy(1 + j, me, (*chip, c), src=x_ref) for j, chip in enumerate(chips)]
        for cp in first:
            cp.start()
        passed = [copy(4 + j, (*chip, c), sibling) for j, chip in enumerate(chips)]
        for j, chip in enumerate(chips):
            copy(1 + j, (*chip, c), me).wait_recv()
            passed[j].start()
        copy(0, sibling, me).wait_recv()
        for j, chip in enumerate(chips):
            copy(4 + j, (*chip, 1 - c), me).wait_recv()
        for cp in first + passed:
            cp.wait_send()
        mine.wait()

    return pl.pallas_call(
        body, name=name, out_shape=_sds((N_DEV * m, n), blk.dtype), in_specs=[ANY], out_specs=ANY,
        scratch_shapes=[pltpu.SemaphoreType.DMA((7,)), pltpu.SemaphoreType.DMA((7,)), pltpu.SemaphoreType.DMA],
    )(blk)


def _half(ref, core, split_rows):
    r, c = ref.shape
    if split_rows:
        return ref.at[pl.ds(core * (r // 2), r // 2), :]
    return ref.at[:, pl.ds(core * (c // 2), c // 2)]


def _chip_block(ref, j, col_sharded):
    r, c = ref.shape
    if col_sharded:
        return ref.at[:, pl.ds(j * (c // N_CHIPS), c // N_CHIPS)]
    return ref.at[pl.ds(j * (r // N_CHIPS), r // N_CHIPS), :]


def _rows_part(ref, part):
    lo, hi, n = part
    r = ref.shape[0]
    return ref if (lo, hi) == (0, n) else ref.at[pl.ds(lo * (r // n), (hi - lo) * (r // n)), :]


def _copy(send_sems, recv_sems, k, src, dst, to):
    return pltpu.make_async_remote_copy(src_ref=src, dst_ref=dst, send_sem=send_sems.at[k], recv_sem=recv_sems.at[k],
                                        device_id=to, device_id_type=MESH_ID)


def _in_place(arrays):
    return tuple(arrays), tuple(_sds(a.shape, a.dtype) for a in arrays), {i: i for i in range(len(arrays))}


def _gather_ici(fulls, col_sharded, part=(0, 1, 1)):
    nw = len(fulls)

    def build(_, refs, send_sems, recv_sems, sem0):
        x, y, c, chips = _place()
        sends, recvs = [], []
        for w in range(nw):
            win = lambda j: _rows_part(_half(_chip_block(refs[w], j, col_sharded[w]), c, True), part)
            for k, (cx, cy) in enumerate(chips):
                sem = sem0 + 3 * w + k
                sends.append(_copy(send_sems, recv_sems, sem, win(2 * x + y), win(2 * x + y), (cx, cy, c)))
                recvs.append(_copy(send_sems, recv_sems, sem, win(2 * cx + cy), win(2 * cx + cy), (cx, cy, c)))
        return sends, recvs

    return _Side(*_in_place(fulls), 3 * nw, build)


def _gather_d2d(fulls, col_sharded):
    nw = len(fulls)

    def build(_, refs, send_sems, recv_sems, sem0):
        x, y, c, chips = _place()
        sends, recvs = [], []
        for w in range(nw):
            win = lambda j, core: _half(_chip_block(refs[w], j, col_sharded[w]), core, True)
            for k, (cx, cy) in enumerate(chips):
                sem = sem0 + 3 * w + k
                sends.append(_copy(send_sems, recv_sems, sem, win(2 * cx + cy, c), win(2 * cx + cy, c), (x, y, 1 - c)))
                recvs.append(_copy(send_sems, recv_sems, sem, win(2 * cx + cy, 1 - c), win(2 * cx + cy, 1 - c),
                                   (x, y, 1 - c)))
        return sends, recvs

    return _Side(*_in_place(fulls), 3 * nw, build)


def _exchange(side, name):
    return _call(None, side=side, name=name)()[1]


def _swap_halves(grads, col_sharded):
    nw = len(grads)
    out_shapes = [_sds((g.shape[0] // 2, g.shape[1]) if col else (g.shape[0], g.shape[1] // 2), g.dtype)
                  for g, col in zip(grads, col_sharded)]

    def build(g_refs, o_refs, send_sems, recv_sems, sem0):
        x, y, c, _ = _place()
        copies = [_copy(send_sems, recv_sems, sem0 + w, _half(g_refs[w], 1 - c, col_sharded[w]), o_refs[w],
                        (x, y, 1 - c)) for w in range(nw)]
        return copies, copies

    return _Side(tuple(grads), tuple(out_shapes), {}, nw, build)


def _scatter_regions(pairs, col_sharded, part=(0, 1, 1), into=None):
    nw = len(pairs)

    def region_shape(p, col):
        return (p.shape[0], p.shape[1] // N_CHIPS) if col else (p.shape[0] // N_CHIPS, p.shape[1])

    out_shapes = tuple(_sds((N_CHIPS - 1, *region_shape(p, col)), p.dtype) for p, col in zip(pairs, col_sharded))

    def build(refs, o_refs, send_sems, recv_sems, sem0):
        x, y, c, chips = _place()
        copies = []
        for w in range(nw):
            for k, (cx, cy) in enumerate(chips):
                copies.append(_copy(
                    send_sems, recv_sems, sem0 + 3 * w + k,
                    _rows_part(_chip_block(refs[w], 2 * cx + cy, col_sharded[w]), part),
                    _rows_part(o_refs[w].at[k], part), (cx, cy, c)))
        return copies, copies

    if into is None:
        return _Side(tuple(pairs), out_shapes, {}, 3 * nw, build)
    return _Side((*pairs, *into), out_shapes, {nw + w: w for w in range(nw)}, 3 * nw, build)


def _join_halves(halves, col_sharded):
    nw = len(halves)

    def build(_, refs, send_sems, recv_sems, sem0):
        x, y, c, _ = _place()
        sends, recvs = [], []
        for w in range(nw):
            mine, theirs = _half(refs[w], c, col_sharded[w]), _half(refs[w], 1 - c, col_sharded[w])
            sends.append(_copy(send_sems, recv_sems, sem0 + w, mine, mine, (x, y, 1 - c)))
            recvs.append(_copy(send_sems, recv_sems, sem0 + w, theirs, theirs, (x, y, 1 - c)))
        return sends, recvs

    return _Side(*_in_place(halves), nw, build)


def _part_rows(size):
    return -(-size // (SUBLANES * LANES)) * SUBLANES


def _pack(arrays, pad_rows_to=SUBLANES):
    flat = [jnp.pad(a.reshape(-1), (0, _part_rows(a.size) * LANES - a.size)).reshape(-1, LANES) for a in arrays]
    rows = sum(f.shape[0] for f in flat)
    pad = (-rows) % pad_rows_to
    if pad:
        flat.append(jnp.zeros((pad, LANES), F32))
    return jnp.concatenate(flat, axis=0)


def _unpack(packed, shapes):
    out, r = [], 0
    for shp in shapes:
        size = math.prod(shp)
        out.append(packed[r:r + _part_rows(size)].reshape(-1)[:size].reshape(shp))
        r += _part_rows(size)
    return out


def _rope_tables(n_ctx_rows, s):
    rows = s // GRID_W
    row_idx = jnp.repeat(jnp.arange(rows), GRID_W)
    col_idx = jnp.tile(jnp.arange(GRID_W), rows)
    n_freq = LANES // 4
    inv_freq = ROPE_THETA ** (-jnp.arange(n_freq, dtype=F32) / n_freq)
    ang = jnp.concatenate([row_idx.astype(F32)[:, None] * inv_freq, col_idx.astype(F32)[:, None] * inv_freq], axis=-1)
    cos = jnp.repeat(jnp.cos(ang), 2, axis=-1)
    sin = jnp.repeat(jnp.sin(ang), 2, axis=-1) * jnp.tile(jnp.array([-1.0, 1.0], F32), LANES // 2)
    cos = jnp.concatenate([jnp.ones((n_ctx_rows, LANES), F32), cos], axis=0)
    sin = jnp.concatenate([jnp.zeros((n_ctx_rows, LANES), F32), sin], axis=0)
    return cos, sin


WEIGHT_NAMES = ['c_ctx', 'w_mod', 'b_mod', 'g_mix', 'g_mlp', 'w_in', 'q_gain', 'k_gain', 'conv_w', 'conv_b', 'w_rg',
                'b_rg', 'w_ig', 'b_ig', 'lru_lambda', 'w_o_attn', 'w_o_rnn', 'w_out', 'w_up', 'w_down', 'g_final']
BIG = ['w_in', 'w_o_attn', 'w_o_rnn', 'w_out', 'w_up', 'w_down']
BIG_COL_SHARDED = [True, False, False, False, True, False]
GATES = ['w_rg', 'w_ig']
SMALL = ['c_ctx', 'b_mod', 'g_mix', 'g_mlp', 'q_gain', 'k_gain', 'conv_b', 'g_final',
         'conv_w', 'b_rg', 'b_ig', 'lru_lambda']


def kernel(x, c, ctx, c_ctx, w_mod, b_mod, g_mix, g_mlp, w_in, q_gain, k_gain, conv_w, conv_b, w_rg, b_rg, w_ig, b_ig, lru_lambda, w_o_attn, w_o_rnn, w_out, w_up, w_down, g_final, loss_target, m_c_ctx, m_w_mod, m_b_mod, m_g_mix, m_g_mlp, m_w_in, m_q_gain, m_k_gain, m_conv_w, m_conv_b, m_w_rg, m_b_rg, m_w_ig, m_b_ig, m_lru_lambda, m_w_o_attn, m_w_o_rnn, m_w_out, m_w_up, m_w_down, m_g_final, v_c_ctx, v_w_mod, v_b_mod, v_g_mix, v_g_mlp, v_w_in, v_q_gain, v_k_gain, v_conv_w, v_conv_b, v_w_rg, v_b_rg, v_w_ig, v_b_ig, v_lru_lambda, v_w_o_attn, v_w_o_rnn, v_w_out, v_w_up, v_w_down, v_g_final):
    given = dict(locals())
    weights = {n: given[n] for n in WEIGHT_NAMES}
    moms = {n: given["m_" + n] for n in WEIGHT_NAMES}
    vars_ = {n: given["v_" + n] for n in WEIGHT_NAMES}

    s, d = x.shape[1], x.shape[2]
    n_ctx = ctx.shape[1]
    t = n_ctx + s
    hd = q_gain.shape[1]
    assert hd == LANES and w_rg.shape[-1] == LANES
    attn_w = w_o_attn.shape[1] * N_CHIPS
    n_in = w_in.shape[2] * N_CHIPS
    kv_w = (n_in - attn_w - 4 * d) // 2
    group = attn_w // kv_w
    k_off, v_off, xr_off = attn_w, attn_w + kv_w, attn_w + 2 * kv_w
    xg_off, gl_off = xr_off + d, xr_off + 2 * d
    d_mod = N_MOD * d
    tr = _tile(math.gcd(n_ctx, s), 256, 16)
    tcol = _tile(math.gcd(d, xr_off), 512, LANES)
    xi, yi, ci = lax.axis_index("x"), lax.axis_index("y"), lax.axis_index("c")
    chip = 2 * xi + yi
    core = ci.astype(jnp.int32).reshape(1)

    sharded_small = [conv_w[0], b_rg[0], b_ig[0], lru_lambda[0]]
    pack0 = _pack([c[0]] + sharded_small)
    got0 = _all_gather8(pack0, "gather_small_inputs").reshape(N_DEV, -1, LANES)
    c_all = got0[:, :_part_rows(d)].reshape(N_DEV, -1)[:, :d]
    per_chip = [_unpack(got0[2 * j, _part_rows(d):], [a.shape for a in sharded_small]) for j in range(N_CHIPS)]
    conv_w_f, b_rg_f, b_ig_f, lam_f = (jnp.concatenate([per_chip[j][i] for j in range(N_CHIPS)], axis=-1)
                                       for i in range(4))
    c16 = jnp.concatenate([c_all, c_ctx[None, :], jnp.zeros((16 - N_DEV - 1, d), F32)], axis=0)
    b_mod_shard = lax.dynamic_slice(b_mod, (0, chip * (d_mod // N_CHIPS)), (1, d_mod // N_CHIPS))
    mod_part, silu16 = _mod_fwd(c16, w_mod[0], b_mod_shard)
    mod_all = _all_gather8(mod_part, "gather_mod").reshape(N_DEV, 16, d_mod // N_CHIPS)
    mod16 = jnp.concatenate([mod_all[2 * j] for j in range(N_CHIPS)], axis=-1)
    me = 4 * xi + 2 * yi + ci
    mod_lat = lax.dynamic_slice(mod16, (me, 0), (1, d_mod)).reshape(N_MOD, d)
    mod_ctx = mod16[N_DEV].reshape(N_MOD, d)
    mod4 = jnp.stack([mod_ctx[0], mod_ctx[1], mod_lat[0], mod_lat[1]])
    mod3 = jnp.stack([mod_lat[2], mod_lat[3], mod_lat[4]])
    gate_f = mod_lat[5][None, :]

    chip_arr = chip.astype(jnp.int32).reshape(1)
    own = {n: _cast_into_window(weights[n][0], chip_arr, col, "cast_" + n) for n, col in zip(BIG, BIG_COL_SHARDED)}
    place = jnp.stack([chip, ci]).astype(jnp.int32)
    row3 = [False] * 3

    def pair_sum(n, full, other, col):
        return _add_half(full, other, core, col, "pair_sum_" + n)

    def chip_sum(n, pair, got, col):
        return _sum_regions(pair, got, place, col, "chip_sum_" + n)

    (w_in_f,) = _exchange(_gather_ici([own['w_in']], [True]), "gather_w_in_ici")
    (w_in_f,) = _exchange(_gather_d2d([w_in_f], [True]), "gather_w_in_d2d")
    xcat = jnp.concatenate([ctx[0], x[0]], axis=0)
    cos, sin = _rope_tables(n_ctx, s)
    h = _norm_mod_fwd(xcat, g_mix, mod4, n_ctx, tr)
    z, (w_oa_f, w_or_f, w_out_f, w_down_f) = _matmul(h, w_in_f, name="mm_in", side=_sides(
        _gather_ici([own['w_o_attn'], own['w_o_rnn'], own['w_out']], row3), _gather_ici([own['w_down']], [False], (0, 1, 2))))
    qr, (w_oa_f, w_or_f, w_out_f) = _head_prep_fwd(z, 0, attn_w // LANES, q_gain, cos, sin, tr, "q_prep",
                                                   side=_gather_d2d([w_oa_f, w_or_f, w_out_f], row3))
    kr = _head_prep_fwd(z, k_off, kv_w // LANES, k_gain, cos, sin, tr, "k_prep")
    attn_o, (w_up_f,) = _attn_fwd(qr, kr, z, v_off, n_ctx, group, tr, side=_gather_ici([own['w_up']], [True]))
    (xc, a_f, bx_f, a_b, bx_b), (w_up_f,) = _rnn_prep(z, xr_off, conv_w_f, conv_b, w_rg[0], b_rg_f, w_ig[0], b_ig_f, lam_f,
                                                      n_ctx, side=_gather_d2d([w_up_f], [True]))
    h_f = _scan(a_f, bx_f, order="ctx_lat_up", post=False, n_ctx_rows=n_ctx, name="scan_f")
    h_b = _scan(a_b, bx_b, order="ctx_lat_down", post=False, n_ctx_rows=n_ctx, name="scan_b")
    u = _rnn_gate_fwd(h_f, h_b, z, xg_off, n_ctx, tr, tcol)
    y_attn = _matmul(attn_o, w_oa_f, name="mm_o_attn")
    y_rnn = _matmul(u, w_or_f, name="mm_o_rnn")
    mrg = _merge_fwd(y_attn, y_rnn, z, gl_off, n_ctx, tr, tcol)
    mix = _matmul(mrg, w_out_f, name="mm_out")
    x1, h2 = _resid_norm_mod_fwd(x[0], mix, g_mlp, mod3, tr)
    up, (w_down_f,) = _matmul(h2, w_up_f, name="mm_up", side=_gather_ici([w_down_f], [False], (1, 2, 2)))
    act, (w_down_f,) = _sq_relu(up, tr, tcol, side=_gather_d2d([w_down_f], [False]))
    down = _matmul(act, w_down_f, name="mm_down")
    dx2, d_down, sums_fin, loss_blk = _final_fwd_bwd(x1, down, loss_target[0], g_final[None, :], gate_f, tr)

    d_act = _matmul(d_down, w_down_f, tb=True, name="mm_d_act")
    g_w_down = _matmul(act, d_down, ta=True, out_dtype=BF16, name="mm_g_down")
    d_up, (got,) = _sq_relu_bwd(d_act, up, tr, tcol, side=_swap_halves([g_w_down], [False]))
    p_down = pair_sum('w_down', g_w_down, got, False)
    dh2 = _matmul(d_up, w_up_f, tb=True, name="mm_d_h2")
    g_w_up = _matmul(h2, d_up, ta=True, out_dtype=BF16, name="mm_g_up")
    dx1, d_mix, sums2 = _norm_mod_bwd2(x1, dh2, dx2, mix, g_mlp, mod3, tr)
    d_mrg, (got,) = _matmul(d_mix, w_out_f, tb=True, name="mm_d_mrg", side=_swap_halves([g_w_up], [True]))
    p_up = pair_sum('w_up', g_w_up, got, True)
    g_w_out = _matmul(mrg, d_mix, ta=True, out_dtype=BF16, name="mm_g_out")
    d_ya, d_yr, d_gla, d_glr = _merge_bwd(d_mrg, y_attn, y_rnn, z, gl_off, n_ctx, tr, tcol)
    d_o = _matmul(d_ya, w_oa_f, tb=True, out_dtype=BF16, name="mm_d_o")
    g_w_oa = _matmul(attn_o, d_ya, ta=True, out_dtype=BF16, name="mm_g_o_attn")
    d_u = _matmul(d_yr, w_or_f, tb=True, name="mm_d_u")
    g_w_or = _matmul(u, d_yr, ta=True, out_dtype=BF16, name="mm_g_o_rnn")
    d_rnn, d_xg = _rnn_gate_bwd(d_u, h_f, h_b, z, xg_off, n_ctx, tr, tcol)
    gs_f = _scan(a_f, d_rnn, order="lat_ctx_down", post=True, n_ctx_rows=n_ctx, name="scan_f_bwd")
    gs_b = _scan(a_b, d_rnn, order="lat_ctx_up", post=True, n_ctx_rows=n_ctx, name="scan_b_bwd")
    (d_xr, g_w_rg, g_w_ig, sums_rnn), (got,) = _rnn_bwd(
        z, xr_off, xc, gs_f, gs_b, h_f, h_b, conv_w_f, w_rg[0], b_rg_f, w_ig[0], b_ig_f, lam_f, n_ctx,
        side=_scatter_regions([p_down], [False]))
    hs_down = chip_sum('w_down', p_down, got, False)
    gate_cols = 8 * LANES if g_w_rg.size % (8 * LANES * N_CHIPS * 16) == 0 else 2 * LANES
    gate_rows = g_w_rg.size // gate_cols
    mid_names = ['w_o_attn', 'w_o_rnn', 'w_out'] + GATES
    mid_grads = [g_w_oa, g_w_or, g_w_out, g_w_rg.reshape(gate_rows, gate_cols), g_w_ig.reshape(gate_rows, gate_cols)]
    row5 = [False] * 5
    (dq, dk, dv), (got_up, *got_mid, gs_down) = _attn_bwd(qr, kr, z, v_off, d_o, n_ctx, group, tr, side=_sides(
        _scatter_regions([p_up], [True]), _swap_halves(mid_grads, row5), _join_halves([hs_down], [False])))
    hs_up = chip_sum('w_up', p_up, got_up, True)
    p_mid = [pair_sum(n, g, o, False) for n, g, o in zip(mid_names, mid_grads, got_mid)]
    d_q_raw, g_q_gain = _head_prep_bwd(z, 0, attn_w // LANES, q_gain, cos, sin, dq, n_ctx, tr, "q_prep_bwd")
    d_k_raw, g_k_gain = _head_prep_bwd(z, k_off, kv_w // LANES, k_gain, cos, sin, dk, 0, tr, "k_prep_bwd")
    zero_ctx = lambda w: jnp.zeros((n_ctx, w), BF16)
    dz = jnp.concatenate([
        jnp.concatenate([zero_ctx(attn_w), d_q_raw], axis=0), d_k_raw, dv.astype(BF16), d_xr,
        jnp.concatenate([zero_ctx(d), d_xg], axis=0), jnp.concatenate([zero_ctx(d), d_gla], axis=0),
        jnp.concatenate([zero_ctx(d), d_glr], axis=0)], axis=1)
    g_w_in, (*got_mid, gs_up) = _matmul(h, dz, ta=True, out_dtype=BF16, name="mm_g_in", side=_sides(
        _scatter_regions(p_mid, row5), _join_halves([hs_up], [True])))
    hs_mid = [chip_sum(n, p, o, False) for n, p, o in zip(mid_names, p_mid, got_mid)]
    (got,) = _exchange(_swap_halves([g_w_in], [True]), "swap_w_in")
    p_in = pair_sum('w_in', g_w_in, got, True)
    dh, (got_in,) = _matmul(dz, w_in_f, tb=True, name="mm_d_h", side=_scatter_regions([p_in], [True], (0, 3, 4)))
    grad_x, sums1 = _norm_mod_bwd1(xcat, dh, dx1, g_mix, mod4, n_ctx, tr)

    zeros_d = jnp.zeros((d,), F32)
    dmod_lat = jnp.concatenate([sums1[0], sums1[1], sums2[3], sums2[0], sums2[1], sums_fin[1]])
    dmod_ctx = jnp.concatenate([sums1[3], sums1[4]] + [zeros_d] * 4)
    small_parts = [dmod_lat, dmod_ctx, loss_blk[0, 0:1], sums1[2] + sums1[5], sums2[2], g_q_gain[0], g_k_gain[0],
                   sums_rnn[10], sums_fin[0], sums_rnn[6:10], sums_rnn[0:2], sums_rnn[2:4], sums_rnn[4:6]]
    pack1 = _pack(small_parts)
    got1 = _all_gather8(pack1, "gather_small_grads").reshape(N_DEV, -1, LANES)
    tot1 = _sum_leading(got1, "sum_small_grads")
    part_shapes = [a.shape for a in small_parts]
    (s_dmod_lat, s_dmod_ctx, s_loss, g_g_mix, g_g_mlp, g_q_gain, g_k_gain, g_conv_b, g_g_final,
     g_conv_w_f, g_b_rg_f, g_b_ig_f, g_lam_f) = _unpack(tot1, part_shapes)
    loss = s_loss[0]
    g_b_mod = (s_dmod_lat + s_dmod_ctx)[None, :]
    n_mod_rows = _part_rows(d_mod)
    dmod16 = jnp.concatenate([got1[:, :n_mod_rows].reshape(N_DEV, -1)[:, :d_mod], s_dmod_ctx[None, :],
                              jnp.zeros((16 - N_DEV - 1, d_mod), F32)], axis=0)
    dmod16_shard = lax.dynamic_slice(dmod16, (0, chip * (d_mod // N_CHIPS)), (16, d_mod // N_CHIPS))
    g_w_mod = _matmul(silu16, dmod16_shard, ta=True, name="mm_g_mod")
    dsilu_part = _matmul(dmod16_shard[N_DEV:], w_mod[0], tb=True, name="mm_d_silu")
    dsilu_all = _all_gather8(dsilu_part, "gather_d_silu").reshape(N_DEV, 8, d)
    g_c_ctx = _c_ctx_grad(dsilu_all, c_ctx[None, :])[0]

    def shard_of(full):
        w = full.shape[-1] // N_CHIPS
        return lax.dynamic_slice(full, (0, chip * w), (full.shape[0], w))

    grads = {
        'c_ctx': g_c_ctx, 'b_mod': g_b_mod, 'g_mix': g_g_mix[None, :], 'g_mlp': g_g_mlp[None, :],
        'q_gain': g_q_gain[None, :], 'k_gain': g_k_gain[None, :], 'conv_b': g_conv_b[None, :],
        'g_final': g_g_final,
        'conv_w': shard_of(g_conv_w_f)[None], 'b_rg': shard_of(g_b_rg_f)[None], 'b_ig': shard_of(g_b_ig_f)[None],
        'lru_lambda': shard_of(g_lam_f)[None], 'w_mod': g_w_mod[None],
    }

    delta, new_m, new_v = {}, {}, {}

    def adamw(n, side=None):
        shp = weights[n].shape
        as2d = (lambda a: a[0]) if n not in GATES else (lambda a: a.reshape(-1, LANES))
        out = _adamw(as2d(weights[n]), as2d(grads[n]), as2d(moms[n]), as2d(vars_[n]), "adamw_" + n, side=side)
        (dl, nm, nv), got = out if side is not None else (out, None)
        delta[n], new_m[n], new_v[n] = dl.reshape(shp), nm.reshape(shp), nv.reshape(shp)
        return got

    (got_in,) = adamw('w_mod', side=_scatter_regions([p_in], [True], (3, 4, 4), into=[got_in]))
    hs_in = chip_sum('w_in', p_in, got_in, True)
    rest = _exchange(_join_halves([hs_in] + hs_mid, [True] + row5), "join_rest")
    for n, g in zip(['w_in', 'w_o_attn', 'w_o_rnn', 'w_out', 'w_up', 'w_down'], [*rest[:4], gs_up, gs_down]):
        grads[n] = g[None]
    gate_all = _all_gather8(jnp.concatenate(rest[4:], axis=0), "gather_gate_grads")
    gate_all = gate_all.reshape(N_CHIPS, 2, len(GATES), gate_rows // N_CHIPS, gate_cols)[:, 0]
    for i, n in enumerate(GATES):
        grads[n] = gate_all[:, i].reshape(weights[n].shape)
    for n in BIG + GATES:
        adamw(n)
    small_shapes = [weights[n].shape for n in SMALL]
    packed = [_pack([src[n] for n in SMALL], 512) for src in (weights, grads, moms, vars_)]
    outs = _adamw(*packed, "adamw_small")
    for res, out in zip((delta, new_m, new_v), outs):
        for n, a in zip(SMALL, _unpack(out, small_shapes)):
            res[n] = a
    return (loss, grad_x[None], *[grads[n] for n in WEIGHT_NAMES], *[delta[n] for n in WEIGHT_NAMES],
            *[new_m[n] for n in WEIGHT_NAMES], *[new_v[n] for n in WEIGHT_NAMES])
```

```python
import functools
import math
from typing import Callable, NamedTuple

import jax
import jax.numpy as jnp
from jax import lax
from jax.experimental import pallas as pl
from jax.experimental.pallas import tpu as pltpu

F32 = jnp.float32
BF16 = jnp.bfloat16
MESH_ID = pl.DeviceIdType.MESH
ANY = pl.BlockSpec(memory_space=pl.ANY)

NORM_EPS = 1e-6
LRU_C = 8.0
GRID_W = 64
ROPE_THETA = 10000.0
N_MOD = 6
CONV_WIDTH = 4
ADAM_LR = 0.001
ADAM_B1 = 0.9
ADAM_B2 = 0.999
ADAM_EPS = 1e-08
ADAM_WD = 0.01
ADAM_STEP = 10

LANES = 128
SUBLANES = 8
V7X_VMEM_LIMIT = 48 * 1024 * 1024
N_CHIPS = 4
N_DEV = 8
GELU_C = math.sqrt(2.0 / math.pi)
GELU_A = 0.044715


def _tile(dim, pref, align):
    t = min(pref, dim)
    t -= t % align
    while t >= align:
        if dim % t == 0:
            return t
        t -= align
    return dim


def _params(*sem):
    return pltpu.CompilerParams(dimension_semantics=sem, vmem_limit_bytes=V7X_VMEM_LIMIT)


def _sds(shape, dtype=F32):
    return jax.ShapeDtypeStruct(shape, dtype)


class _Side(NamedTuple):
    operands: tuple
    results: tuple
    aliases: dict
    n_sems: int
    build: Callable


def _sides(*sides):
    ops, res, aliases, spans, n = [], [], {}, [], 0
    for s in sides:
        spans.append((len(ops), len(res), n))
        aliases.update({len(ops) + i: len(res) + j for i, j in s.aliases.items()})
        ops += s.operands
        res += s.results
        n += s.n_sems

    def build(op_refs, res_refs, send_sems, recv_sems, sem0):
        sends, recvs = [], []
        for s, (o, r, k) in zip(sides, spans):
            a, b = s.build(op_refs[o:o + len(s.operands)], res_refs[r:r + len(s.results)], send_sems, recv_sems,
                           sem0 + k)
            sends += a
            recvs += b
        return sends, recvs

    return _Side(tuple(ops), tuple(res), aliases, n, build)


def _call(body, *, side=None, sem=(), grid=(), in_specs=(), out_specs=(), out_shape=(), scratch_shapes=(), **kw):
    if side is None:
        return pl.pallas_call(body, grid=grid, in_specs=list(in_specs), out_specs=out_specs, out_shape=out_shape,
                              scratch_shapes=list(scratch_shapes), compiler_params=_params(*sem), **kw)
    many = isinstance(out_shape, (list, tuple))
    out_specs_l, out_shape_l = (list(out_specs), list(out_shape)) if many else ([out_specs], [out_shape])
    n_in, n_out, n_scr = len(in_specs), len(out_shape_l), len(scratch_shapes)
    n_op, n_res = len(side.operands), len(side.results)

    def hosted(*refs):
        ins, ops = refs[:n_in], refs[n_in:n_in + n_op]
        outs = refs[n_in + n_op:n_in + n_op + n_out]
        res = refs[n_in + n_op + n_out:n_in + n_op + n_out + n_res]
        scr = refs[n_in + n_op + n_out + n_res:-2]
        send_sems, recv_sems = refs[-2:]

        def start():
            for cp in side.build(ops, res, send_sems, recv_sems, 0)[0]:
                cp.start()

        def finish():
            sends, recvs = side.build(ops, res, send_sems, recv_sems, 0)
            for cp in recvs:
                cp.wait_recv()
            for cp in sends:
                cp.wait_send()

        if not grid:
            start()
            finish()
            return
        ids = [pl.program_id(a) for a in range(len(grid))]
        first = functools.reduce(jnp.logical_and, [i == 0 for i in ids])
        last = functools.reduce(jnp.logical_and, [i == g - 1 for i, g in zip(ids, grid)])
        pl.when(first)(start)
        body(*ins, *outs, *scr)
        pl.when(last)(finish)

    def run(*args):
        got = pl.pallas_call(
            hosted, grid=grid, in_specs=[*in_specs, *[ANY] * n_op], out_specs=[*out_specs_l, *[ANY] * n_res],
            out_shape=[*out_shape_l, *side.results],
            scratch_shapes=[*scratch_shapes, pltpu.SemaphoreType.DMA((side.n_sems,)),
                            pltpu.SemaphoreType.DMA((side.n_sems,))],
            input_output_aliases={n_in + i: n_out + j for i, j in side.aliases.items()},
            compiler_params=_params(*["arbitrary"] * len(grid)), **kw)(*args, *side.operands)
        own = list(got[:n_out]) if many else got[0]
        return own, list(got[n_out:])

    return run


def _matmul(a, b, *, ta=False, tb=False, out_dtype=F32, name, tm=1024, tn=1024, tk=2816, side=None):
    k_dim, m = a.shape if ta else a.shape[::-1]
    n, k2 = b.shape if tb else b.shape[::-1]
    assert k_dim == k2, (a.shape, b.shape, ta, tb)
    tm = _tile(m, tm, LANES if ta else 16)
    tn = _tile(n, tn, 16 if tb else LANES)
    tk = _tile(k_dim, tk, LANES)
    nk = k_dim // tk
    dims = (((0 if ta else 1,), (1 if tb else 0,)), ((), ()))
    if nk == 1:
        def whole(a_ref, b_ref, o_ref):
            o_ref[...] = lax.dot_general(a_ref[...].astype(BF16), b_ref[...].astype(BF16), dims,
                                         preferred_element_type=F32).astype(o_ref.dtype)

        a_spec = pl.BlockSpec((tk, tm), lambda i, j: (0, i)) if ta else pl.BlockSpec((tm, tk), lambda i, j: (i, 0))
        b_spec = pl.BlockSpec((tn, tk), lambda i, j: (j, 0)) if tb else pl.BlockSpec((tk, tn), lambda i, j: (0, j))
        return _call(
            whole, side=side, name=name, grid=(m // tm, n // tn), in_specs=[a_spec, b_spec],
            out_specs=pl.BlockSpec((tm, tn), lambda i, j: (i, j)), out_shape=_sds((m, n), out_dtype),
            sem=("parallel", "parallel"),
        )(a, b)

    def body(a_ref, b_ref, o_ref, acc_ref):
        k = pl.program_id(2)

        @pl.when(k == 0)
        def _():
            acc_ref[...] = jnp.zeros_like(acc_ref)

        acc_ref[...] += lax.dot_general(a_ref[...].astype(BF16), b_ref[...].astype(BF16), dims,
                                        preferred_element_type=F32)

        @pl.when(k == nk - 1)
        def _():
            o_ref[...] = acc_ref[...].astype(o_ref.dtype)

    a_spec = pl.BlockSpec((tk, tm), lambda i, j, k: (k, i)) if ta else pl.BlockSpec((tm, tk), lambda i, j, k: (i, k))
    b_spec = pl.BlockSpec((tn, tk), lambda i, j, k: (j, k)) if tb else pl.BlockSpec((tk, tn), lambda i, j, k: (k, j))
    return _call(
        body, side=side, name=name, grid=(m // tm, n // tn, nk), in_specs=[a_spec, b_spec],
        out_specs=pl.BlockSpec((tm, tn), lambda i, j, k: (i, j)), out_shape=_sds((m, n), out_dtype),
        scratch_shapes=[pltpu.VMEM((tm, tn), F32)], sem=("parallel", "parallel", "arbitrary"),
    )(a, b)


def _silu(x):
    return x * jax.nn.sigmoid(x)


def _gelu(x):
    return 0.5 * x * (1.0 + jnp.tanh(GELU_C * (x + GELU_A * x * x * x)))


def _gelu_grad(x):
    t = jnp.tanh(GELU_C * (x + GELU_A * x * x * x))
    return 0.5 * (1.0 + t) + 0.5 * x * (1.0 - t * t) * GELU_C * (1.0 + 3.0 * GELU_A * x * x)


def _expm1_nonpos(x):
    series = x * (1.0 + x * (1.0 / 2 + x * (1.0 / 6 + x * (1.0 / 24 + x * (1.0 / 120 + x * (1.0 / 720 + x / 5040))))))
    return jnp.where(x > -0.25, series, jnp.exp(x) - 1.0)


def _softplus(x):
    return jnp.maximum(x, 0.0) + jnp.log1p(jnp.exp(-jnp.abs(x)))


def _rms_stats(x):
    return lax.rsqrt(jnp.mean(x * x, axis=-1, keepdims=True) + NORM_EPS)


def _rms_bwd(dxhat, xhat, rstd):
    return rstd * (dxhat - xhat * jnp.mean(dxhat * xhat, axis=-1, keepdims=True))


def _colsum(v):
    return jnp.sum(v, axis=0, keepdims=True)


def _mod_fwd(c16, w_mod, b_mod_shard):
    r, d = c16.shape
    n = w_mod.shape[1]
    tn = _tile(n, 512, LANES)

    def body(c_ref, w_ref, b_ref, o_ref, s_ref):
        s = _silu(c_ref[...])
        s_ref[...] = s
        o_ref[...] = jnp.dot(s.astype(BF16), w_ref[...].astype(BF16), preferred_element_type=F32) + b_ref[...]

    return pl.pallas_call(
        body, name="mod_fwd", grid=(n // tn,),
        in_specs=[pl.BlockSpec((r, d), lambda j: (0, 0)), pl.BlockSpec((d, tn), lambda j: (0, j)),
                  pl.BlockSpec((1, tn), lambda j: (0, j))],
        out_specs=[pl.BlockSpec((r, tn), lambda j: (0, j)), pl.BlockSpec((r, d), lambda j: (0, 0))],
        out_shape=[_sds((r, n)), _sds((r, d))], compiler_params=_params("arbitrary"),
    )(c16, w_mod, b_mod_shard)


def _c_ctx_grad(parts, c_ctx_row):
    d = c_ctx_row.shape[1]

    def body(p_ref, c_ref, o_ref):
        tot = p_ref[0, 0:1, :]
        for chip in range(1, N_CHIPS):
            tot = tot + p_ref[2 * chip, 0:1, :]
        c = c_ref[...]
        sg = jax.nn.sigmoid(c)
        o_ref[...] = tot * (sg * (1.0 + c * (1.0 - sg)))

    return pl.pallas_call(body, name="c_ctx_grad", out_shape=_sds((1, d)), compiler_params=_params())(parts, c_ctx_row)


def _norm_mod_fwd(xcat, g, mod4, n_ctx_rows, tr):
    t, d = xcat.shape
    nctx = n_ctx_rows // tr

    def body(x_ref, g_ref, mod_ref, h_ref):
        x = x_ref[...]
        n = x * _rms_stats(x) * g_ref[...]
        is_ctx = pl.program_id(0) < nctx
        sh = jnp.where(is_ctx, mod_ref[0:1, :], mod_ref[2:3, :])
        sc = jnp.where(is_ctx, mod_ref[1:2, :], mod_ref[3:4, :])
        h_ref[...] = (n * (1.0 + sc) + sh).astype(BF16)

    return pl.pallas_call(
        body, name="norm_mod_fwd", grid=(t // tr,),
        in_specs=[pl.BlockSpec((tr, d), lambda i: (i, 0)), pl.BlockSpec((1, d), lambda i: (0, 0)),
                  pl.BlockSpec((4, d), lambda i: (0, 0))],
        out_specs=pl.BlockSpec((tr, d), lambda i: (i, 0)), out_shape=_sds((t, d), BF16),
        compiler_params=_params("parallel"),
    )(xcat, g, mod4)


def _norm_mod_bwd1(xcat, dh, dx1, g, mod4, n_ctx_rows, tr):
    t, d = xcat.shape
    nctx = n_ctx_rows // tr
    s = t - n_ctx_rows

    def body(x_ref, dh_ref, dx1_ref, g_ref, mod_ref, dx_ref, sums_ref):
        i = pl.program_id(0)
        is_ctx = i < nctx

        @pl.when(i == 0)
        def _():
            sums_ref[...] = jnp.zeros_like(sums_ref)

        x = x_ref[...]
        dh_ = dh_ref[...]
        rstd = _rms_stats(x)
        xhat = x * rstd
        gg = g_ref[...]
        sc = jnp.where(is_ctx, mod_ref[1:2, :], mod_ref[3:4, :])
        dxhat = dh_ * (1.0 + sc) * gg
        dx_ref[...] = dx1_ref[...] + _rms_bwd(dxhat, xhat, rstd)
        part = [_colsum(dh_), _colsum(dh_ * xhat * gg), _colsum(dh_ * (1.0 + sc) * xhat)]

        @pl.when(is_ctx)
        def _():
            for k, row in enumerate(part):
                sums_ref[3 + k:4 + k, :] += row

        @pl.when(jnp.logical_not(is_ctx))
        def _():
            for k, row in enumerate(part):
                sums_ref[k:k + 1, :] += row

    lat = lambda i: (jnp.maximum(i - nctx, 0), 0)
    return pl.pallas_call(
        body, name="norm_mod_bwd1", grid=(t // tr,),
        in_specs=[pl.BlockSpec((tr, d), lambda i: (i, 0)), pl.BlockSpec((tr, d), lambda i: (i, 0)),
                  pl.BlockSpec((tr, d), lat), pl.BlockSpec((1, d), lambda i: (0, 0)),
                  pl.BlockSpec((4, d), lambda i: (0, 0))],
        out_specs=[pl.BlockSpec((tr, d), lat), pl.BlockSpec((8, d), lambda i: (0, 0))],
        out_shape=[_sds((s, d)), _sds((8, d))], compiler_params=_params("arbitrary"),
    )(xcat, dh, dx1, g, mod4)


def _resid_norm_mod_fwd(x, mix, g, mod3, tr):
    s, d = x.shape

    def body(x_ref, mix_ref, g_ref, mod_ref, x1_ref, h_ref):
        x1 = x_ref[...] + mod_ref[0:1, :] * mix_ref[...]
        x1_ref[...] = x1
        n = x1 * _rms_stats(x1) * g_ref[...]
        h_ref[...] = (n * (1.0 + mod_ref[2:3, :]) + mod_ref[1:2, :]).astype(BF16)

    row = pl.BlockSpec((tr, d), lambda i: (i, 0))
    return pl.pallas_call(
        body, name="resid_norm_mod_fwd", grid=(s // tr,),
        in_specs=[row, row, pl.BlockSpec((1, d), lambda i: (0, 0)), pl.BlockSpec((3, d), lambda i: (0, 0))],
        out_specs=[row, row], out_shape=[_sds((s, d)), _sds((s, d), BF16)], compiler_params=_params("parallel"),
    )(x, mix, g, mod3)


def _norm_mod_bwd2(x1, dh2, dx2, mix, g, mod3, tr):
    s, d = x1.shape

    def body(x_ref, dh_ref, dx2_ref, mix_ref, g_ref, mod_ref, dx1_ref, dmix_ref, sums_ref):
        @pl.when(pl.program_id(0) == 0)
        def _():
            sums_ref[...] = jnp.zeros_like(sums_ref)

        x = x_ref[...]
        dh_ = dh_ref[...]
        rstd = _rms_stats(x)
        xhat = x * rstd
        gg = g_ref[...]
        sc = mod_ref[2:3, :]
        dx1 = dx2_ref[...] + _rms_bwd(dh_ * (1.0 + sc) * gg, xhat, rstd)
        dx1_ref[...] = dx1
        dmix_ref[...] = (dx1 * mod_ref[0:1, :]).astype(BF16)
        part = [_colsum(dh_), _colsum(dh_ * xhat * gg), _colsum(dh_ * (1.0 + sc) * xhat), _colsum(dx1 * mix_ref[...])]
        for k, row in enumerate(part):
            sums_ref[k:k + 1, :] += row

    row = pl.BlockSpec((tr, d), lambda i: (i, 0))
    return pl.pallas_call(
        body, name="norm_mod_bwd2", grid=(s // tr,),
        in_specs=[row, row, row, row, pl.BlockSpec((1, d), lambda i: (0, 0)), pl.BlockSpec((3, d), lambda i: (0, 0))],
        out_specs=[row, row, pl.BlockSpec((8, d), lambda i: (0, 0))],
        out_shape=[_sds((s, d)), _sds((s, d), BF16), _sds((8, d))], compiler_params=_params("arbitrary"),
    )(x1, dh2, dx2, mix, g, mod3)


def _final_fwd_bwd(x1, down, target, g_final, gate, tr):
    s, d = x1.shape

    def body(x1_ref, down_ref, t_ref, g_ref, gate_ref, dx2_ref, ddown_ref, sums_ref, loss_ref):
        @pl.when(pl.program_id(0) == 0)
        def _():
            sums_ref[...] = jnp.zeros_like(sums_ref)
            loss_ref[...] = jnp.zeros_like(loss_ref)

        down_ = down_ref[...]
        gate_ = gate_ref[...]
        x2 = x1_ref[...] + gate_ * down_
        rstd = _rms_stats(x2)
        xhat = x2 * rstd
        gg = g_ref[...]
        err = xhat * gg - t_ref[...]
        loss_ref[...] += 0.5 * jnp.sum(jnp.mean(err * err, axis=-1, keepdims=True))
        dy = err * (1.0 / d)
        dx2 = _rms_bwd(dy * gg, xhat, rstd)
        dx2_ref[...] = dx2
        ddown_ref[...] = (dx2 * gate_).astype(BF16)
        sums_ref[0:1, :] += _colsum(dy * xhat)
        sums_ref[1:2, :] += _colsum(dx2 * down_)

    row = pl.BlockSpec((tr, d), lambda i: (i, 0))
    vec = pl.BlockSpec((1, d), lambda i: (0, 0))
    return pl.pallas_call(
        body, name="final_fwd_bwd", grid=(s // tr,), in_specs=[row, row, row, vec, vec],
        out_specs=[row, row, pl.BlockSpec((8, d), lambda i: (0, 0)), pl.BlockSpec((8, LANES), lambda i: (0, 0))],
        out_shape=[_sds((s, d)), _sds((s, d), BF16), _sds((8, d)), _sds((8, LANES))],
        compiler_params=_params("arbitrary"),
    )(x1, down, target, g_final, gate)


def _swap_pairs(v):
    lane = lax.broadcasted_iota(jnp.int32, v.shape, 1)
    return jnp.where(lane % 2 == 0, pltpu.roll(v, LANES - 1, 1), pltpu.roll(v, 1, 1))


def _head_prep_fwd(z, col_off, n_heads, gain, cos, sin, tr, name, side=None):
    t = z.shape[0]
    per = math.gcd(4, n_heads, col_off // LANES)
    w = per * LANES
    hb = col_off // w

    def body(z_ref, g_ref, cos_ref, sin_ref, o_ref):
        for hh in range(per):
            cols = slice(hh * LANES, (hh + 1) * LANES)
            x = z_ref[:, cols]
            y = x * _rms_stats(x) * g_ref[...]
            o_ref[:, cols] = (y * cos_ref[...] + _swap_pairs(y) * sin_ref[...]).astype(BF16)

    tab = pl.BlockSpec((tr, LANES), lambda i, j: (i, 0))
    return _call(
        body, side=side, name=name, grid=(t // tr, n_heads // per),
        in_specs=[pl.BlockSpec((tr, w), lambda i, j: (i, hb + j)), pl.BlockSpec((1, LANES), lambda i, j: (0, 0)),
                  tab, tab],
        out_specs=pl.BlockSpec((tr, w), lambda i, j: (i, j)), out_shape=_sds((t, n_heads * LANES), BF16),
        sem=("parallel", "parallel"),
    )(z, gain, cos, sin)


def _head_prep_bwd(z, col_off, n_heads, gain, cos, sin, dout, row_off, tr, name, side=None):
    r = dout.shape[0]
    per = math.gcd(4, n_heads, col_off // LANES)
    w = per * LANES
    hb = col_off // w
    rb = row_off // tr

    def body(z_ref, g_ref, cos_ref, sin_ref, d_ref, dz_ref, dg_ref):
        @pl.when(jnp.logical_and(pl.program_id(0) == 0, pl.program_id(1) == 0))
        def _():
            dg_ref[...] = jnp.zeros_like(dg_ref)

        for hh in range(per):
            cols = slice(hh * LANES, (hh + 1) * LANES)
            x = z_ref[:, cols]
            rstd = _rms_stats(x)
            xhat = x * rstd
            dd = d_ref[:, cols]
            dy = dd * cos_ref[...] - _swap_pairs(dd) * sin_ref[...]
            dg_ref[0:1, :] += _colsum(dy * xhat)
            dz_ref[:, cols] = _rms_bwd(dy * g_ref[...], xhat, rstd).astype(BF16)

    tab = pl.BlockSpec((tr, LANES), lambda i, j: (rb + i, 0))
    return _call(
        body, side=side, name=name, grid=(r // tr, n_heads // per),
        in_specs=[pl.BlockSpec((tr, w), lambda i, j: (rb + i, hb + j)),
                  pl.BlockSpec((1, LANES), lambda i, j: (0, 0)), tab, tab,
                  pl.BlockSpec((tr, w), lambda i, j: (i, j))],
        out_specs=[pl.BlockSpec((tr, w), lambda i, j: (i, j)), pl.BlockSpec((8, LANES), lambda i, j: (0, 0))],
        out_shape=[_sds((r, n_heads * LANES), BF16), _sds((8, LANES))], sem=("arbitrary", "arbitrary"),
    )(z, gain, cos, sin, dout)


def _softmax_rows(q, k, scale):
    s = lax.dot_general(q, k, (((1,), (1,)), ((), ())), preferred_element_type=F32) * scale
    p = jnp.exp(s - jnp.max(s, axis=-1, keepdims=True))
    return p / jnp.sum(p, axis=-1, keepdims=True)


def _attn_fwd(qr, kr, z, v_off, n_ctx_rows, group, tq, side=None):
    t, kvw = kr.shape
    s = t - n_ctx_rows
    n_kv = kvw // LANES
    scale = LANES ** -0.5
    qb0 = n_ctx_rows // tq
    vb = v_off // LANES

    def body(q_ref, k_ref, v_ref, o_ref):
        k = k_ref[...]
        v = v_ref[...].astype(BF16)
        for g in range(group):
            cols = slice(g * LANES, (g + 1) * LANES)
            p = _softmax_rows(q_ref[:, cols], k, scale)
            o_ref[:, cols] = jnp.dot(p.astype(BF16), v, preferred_element_type=F32).astype(BF16)

    return _call(
        body, side=side, name="attn_fwd", grid=(n_kv, s // tq),
        in_specs=[pl.BlockSpec((tq, group * LANES), lambda h, i: (qb0 + i, h)),
                  pl.BlockSpec((t, LANES), lambda h, i: (0, h)), pl.BlockSpec((t, LANES), lambda h, i: (0, vb + h))],
        out_specs=pl.BlockSpec((tq, group * LANES), lambda h, i: (i, h)),
        out_shape=_sds((s, n_kv * group * LANES), BF16), sem=("parallel", "parallel"),
    )(qr, kr, z)


def _attn_bwd(qr, kr, z, v_off, d_o, n_ctx_rows, group, tq, side=None):
    t, kvw = kr.shape
    s = t - n_ctx_rows
    n_kv = kvw // LANES
    scale = LANES ** -0.5
    qb0 = n_ctx_rows // tq
    vb = v_off // LANES
    tn_dims = (((0,), (0,)), ((), ()))
    nt_dims = (((1,), (1,)), ((), ()))

    def body(q_ref, k_ref, v_ref, do_ref, dq_ref, dk_ref, dv_ref):
        @pl.when(pl.program_id(1) == 0)
        def _():
            dk_ref[...] = jnp.zeros_like(dk_ref)
            dv_ref[...] = jnp.zeros_like(dv_ref)

        k = k_ref[...]
        v = v_ref[...].astype(BF16)
        for g in range(group):
            cols = slice(g * LANES, (g + 1) * LANES)
            q = q_ref[:, cols]
            do_ = do_ref[:, cols]
            p = _softmax_rows(q, k, scale)
            dv_ref[...] += lax.dot_general(p.astype(BF16), do_, tn_dims, preferred_element_type=F32)
            dp = lax.dot_general(do_, v, nt_dims, preferred_element_type=F32)
            ds = (p * (dp - jnp.sum(p * dp, axis=-1, keepdims=True)) * scale).astype(BF16)
            dq_ref[:, cols] = jnp.dot(ds, k, preferred_element_type=F32)
            dk_ref[...] += lax.dot_general(ds, q, tn_dims, preferred_element_type=F32)

    qspec = pl.BlockSpec((tq, group * LANES), lambda h, i: (qb0 + i, h))
    ospec = pl.BlockSpec((tq, group * LANES), lambda h, i: (i, h))
    kspec = pl.BlockSpec((t, LANES), lambda h, i: (0, h))
    return _call(
        body, side=side, name="attn_bwd", grid=(n_kv, s // tq),
        in_specs=[qspec, kspec, pl.BlockSpec((t, LANES), lambda h, i: (0, vb + h)), ospec],
        out_specs=[ospec, kspec, kspec],
        out_shape=[_sds((s, n_kv * group * LANES)), _sds((t, kvw)), _sds((t, kvw))], sem=("parallel", "arbitrary"),
    )(qr, kr, z, d_o)


def _row_mask(shape, rows):
    r = lax.broadcasted_iota(jnp.int32, shape, 0)
    m = r == rows[0]
    for v in rows[1:]:
        m = jnp.logical_or(m, r == v)
    return m


def _shift_rows(x, k, n_ctx_rows):
    t = x.shape[0]
    if k == 0:
        return x
    rolled = pltpu.roll(x, (-k) % t, 0)
    if k > 0:
        dead = [n_ctx_rows - 1 - i for i in range(k)] + [t - 1 - i for i in range(k)]
    else:
        dead = [i for i in range(-k)] + [n_ctx_rows + i for i in range(-k)]
    return jnp.where(_row_mask(x.shape, dead), 0.0, rolled)


def _conv(x, w, b, n_ctx_rows):
    y = b
    for k in range(CONV_WIDTH):
        y = y + _shift_rows(x, k - 1, n_ctx_rows) * w[k:k + 1, :]
    return y


def _gates(xc_bf, w_r, b_r, w_i, b_i, lam):
    r = jax.nn.sigmoid(jnp.dot(xc_bf, w_r.astype(BF16), preferred_element_type=F32) + b_r)
    i = jax.nn.sigmoid(jnp.dot(xc_bf, w_i.astype(BF16), preferred_element_type=F32) + b_i)
    log_a = -LRU_C * r * _softplus(-lam)
    a = jnp.exp(log_a)
    mult = jnp.sqrt(-_expm1_nonpos(2.0 * log_a))
    return r, i, a, mult


def _rnn_specs(t, xr_off):
    xb = xr_off // LANES
    return dict(
        zcol=pl.BlockSpec((t, LANES), lambda j: (0, xb + j)), col=pl.BlockSpec((t, LANES), lambda j: (0, j)),
        conv_w=pl.BlockSpec((CONV_WIDTH, LANES), lambda j: (0, j)), vec=pl.BlockSpec((1, LANES), lambda j: (0, j)),
        gate_w=pl.BlockSpec((2, 1, LANES, LANES), lambda j: (0, j, 0, 0)), two=pl.BlockSpec((2, LANES), lambda j: (0, j)))


def _rnn_prep(z, xr_off, conv_w, conv_b, w_rg, b_rg, w_ig, b_ig, lam, n_ctx_rows, side=None):
    t = z.shape[0]
    d = conv_b.shape[1]
    sp = _rnn_specs(t, xr_off)

    def body(z_ref, cw_ref, cb_ref, wr_ref, br_ref, wi_ref, bi_ref, lam_ref, xc_ref, af_ref, bf_ref, ab_ref, bb_ref):
        xc = _conv(z_ref[...], cw_ref[...], cb_ref[...], n_ctx_rows)
        xc_ref[...] = xc
        xc_bf = xc.astype(BF16)
        for dr, (a_ref, b_ref) in enumerate(((af_ref, bf_ref), (ab_ref, bb_ref))):
            _, i, a, mult = _gates(xc_bf, wr_ref[dr, 0], br_ref[dr:dr + 1, :], wi_ref[dr, 0], bi_ref[dr:dr + 1, :],
                                   lam_ref[dr:dr + 1, :])
            a_ref[...] = a
            b_ref[...] = mult * (i * xc)

    return _call(
        body, side=side, name="rnn_prep", grid=(d // LANES,),
        in_specs=[sp["zcol"], sp["conv_w"], sp["vec"], sp["gate_w"], sp["two"], sp["gate_w"], sp["two"], sp["two"]],
        out_specs=[sp["col"]] * 5, out_shape=[_sds((t, d))] * 5, sem=("parallel",),
    )(z, conv_w, conv_b, w_rg, b_rg, w_ig, b_ig, lam)


def _scan(a, b, *, order, post, n_ctx_rows, name, tc=256):
    t, d = a.shape
    tc = _tile(math.gcd(n_ctx_rows, t - n_ctx_rows), tc, SUBLANES)
    nt, nctx = t // tc, n_ctx_rows // tc
    nlat = nt - nctx
    b_lat_only = b.shape[0] != t
    up = order.endswith("up")

    def chunk(i):
        if order == "ctx_lat_up":
            return i
        if order == "lat_ctx_down":
            return nt - 1 - i
        if order == "ctx_lat_down":
            return jnp.where(i < nctx, nctx - 1 - i, nt - 1 - (i - nctx))
        return jnp.where(i < nlat, nctx + i, i - nlat)

    def body(a_ref, b_ref, o_ref, carry_ref):
        @pl.when(pl.program_id(0) == 0)
        def _():
            carry_ref[...] = jnp.zeros_like(carry_ref)

        live = jnp.where(chunk(pl.program_id(0)) >= nctx, 1.0, 0.0) if b_lat_only else None

        def group(gi, carry):
            base = pl.multiple_of((gi if up else tc // SUBLANES - 1 - gi) * SUBLANES, SUBLANES)
            for r in (range(SUBLANES) if up else range(SUBLANES - 1, -1, -1)):
                a_r = a_ref[pl.ds(base + r, 1), :]
                b_r = b_ref[pl.ds(base + r, 1), :]
                if live is not None:
                    b_r = b_r * live
                if post:
                    out = b_r + carry
                    carry = a_r * out
                else:
                    out = a_r * carry + b_r
                    carry = out
                o_ref[pl.ds(base + r, 1), :] = out
            return carry

        carry_ref[0:1, :] = lax.fori_loop(0, tc // SUBLANES, group, carry_ref[0:1, :])

    full = pl.BlockSpec((tc, d), lambda i: (chunk(i), 0))
    b_spec = pl.BlockSpec((tc, d), lambda i: (jnp.maximum(chunk(i) - nctx, 0), 0)) if b_lat_only else full
    return pl.pallas_call(
        body, name=name, grid=(nt,), in_specs=[full, b_spec], out_specs=full, out_shape=_sds((t, d)),
        scratch_shapes=[pltpu.VMEM((SUBLANES, d), F32)], compiler_params=_params("arbitrary"),
    )(a, b)


def _rnn_bwd(z, xr_off, xc, g_f, g_b, h_f, h_b, conv_w, w_rg, b_rg, w_ig, b_ig, lam, n_ctx_rows, side=None):
    t, d = xc.shape
    sp = _rnn_specs(t, xr_off)
    tn_dims = (((0,), (0,)), ((), ()))
    nt_dims = (((1,), (1,)), ((), ()))

    def body(z_ref, xc_ref, gf_ref, gb_ref, hf_ref, hb_ref, cw_ref, wr_ref, br_ref, wi_ref, bi_ref, lam_ref,
             dxr_ref, dwr_ref, dwi_ref, sums_ref):
        xc_ = xc_ref[...]
        xc_bf = xc_.astype(BF16)
        dxc = jnp.zeros_like(xc_)
        sums = [None] * 6
        for dr, (g_ref, h_ref) in enumerate(((gf_ref, hf_ref), (gb_ref, hb_ref))):
            w_r, w_i, lam_ = wr_ref[dr, 0], wi_ref[dr, 0], lam_ref[dr:dr + 1, :]
            r, i, a, mult = _gates(xc_bf, w_r, br_ref[dr:dr + 1, :], w_i, bi_ref[dr:dr + 1, :], lam_)
            g = g_ref[...]
            h = h_ref[...]
            if dr == 0:
                h_prev = jnp.where(_row_mask(h.shape, [0]), 0.0, pltpu.roll(h, 1, 0))
            else:
                h_prev = jnp.where(_row_mask(h.shape, [n_ctx_rows - 1]), 0.0, pltpu.roll(h, t - 1, 0))
            d_mult = g * i * xc_
            d_i = g * mult * xc_
            dxc = dxc + g * mult * i
            d_log_a = g * h_prev * a - d_mult * a * a / mult
            sp_ = _softplus(-lam_)
            d_r = d_log_a * (-LRU_C) * sp_
            d_sp = _colsum(d_log_a * (-LRU_C) * r)
            du_r = (d_r * r * (1.0 - r))
            du_i = (d_i * i * (1.0 - i))
            sums[dr] = _colsum(du_r)
            sums[2 + dr] = _colsum(du_i)
            sums[4 + dr] = d_sp * (-jax.nn.sigmoid(-lam_))
            du_r_bf, du_i_bf = du_r.astype(BF16), du_i.astype(BF16)
            dwr_ref[dr, 0] = lax.dot_general(xc_bf, du_r_bf, tn_dims, preferred_element_type=F32).astype(BF16)
            dwi_ref[dr, 0] = lax.dot_general(xc_bf, du_i_bf, tn_dims, preferred_element_type=F32).astype(BF16)
            dxc = dxc + lax.dot_general(du_r_bf, w_r.astype(BF16), nt_dims, preferred_element_type=F32)
            dxc = dxc + lax.dot_general(du_i_bf, w_i.astype(BF16), nt_dims, preferred_element_type=F32)
        xr = z_ref[...]
        cw = cw_ref[...]
        dxr = jnp.zeros_like(dxc)
        rows = list(sums)
        for k in range(CONV_WIDTH):
            dxr = dxr + _shift_rows(dxc, 1 - k, n_ctx_rows) * cw[k:k + 1, :]
            rows.append(_colsum(dxc * _shift_rows(xr, k - 1, n_ctx_rows)))
        rows.append(_colsum(dxc))
        dxr_ref[...] = dxr.astype(BF16)
        sums_ref[...] = jnp.zeros_like(sums_ref)
        for k, row in enumerate(rows):
            sums_ref[k:k + 1, :] = row

    return _call(
        body, side=side, name="rnn_bwd", grid=(d // LANES,),
        in_specs=[sp["zcol"]] + [sp["col"]] * 5 + [sp["conv_w"], sp["gate_w"], sp["two"], sp["gate_w"], sp["two"],
                                                  sp["two"]],
        out_specs=[sp["col"], sp["gate_w"], sp["gate_w"], pl.BlockSpec((16, LANES), lambda j: (0, j))],
        out_shape=[_sds((t, d), BF16), _sds(w_rg.shape, BF16), _sds(w_ig.shape, BF16), _sds((16, d))],
        sem=("parallel",),
    )(z, xc, g_f, g_b, h_f, h_b, conv_w, w_rg, b_rg, w_ig, b_ig, lam)


def _tiles2d(s, d, tr, tcol):
    return (s // tr, d // tcol), pl.BlockSpec((tr, tcol), lambda i, j: (i, j))


def _zspec(tr, tcol, row_off, col_off):
    rb, cb = row_off // tr, col_off // tcol
    return pl.BlockSpec((tr, tcol), lambda i, j: (rb + i, cb + j))


def _rnn_gate_fwd(h_f, h_b, z, xg_off, n_ctx_rows, tr, tcol):
    t, d = h_f.shape
    s = t - n_ctx_rows
    grid, out = _tiles2d(s, d, tr, tcol)
    hs = _zspec(tr, tcol, n_ctx_rows, 0)

    def body(hf_ref, hb_ref, xg_ref, u_ref):
        u_ref[...] = ((hf_ref[...] + hb_ref[...]) * _gelu(xg_ref[...])).astype(BF16)

    return pl.pallas_call(body, name="rnn_gate_fwd", grid=grid, in_specs=[hs, hs, _zspec(tr, tcol, n_ctx_rows, xg_off)],
                          out_specs=out, out_shape=_sds((s, d), BF16), compiler_params=_params("parallel", "parallel"),
                          )(h_f, h_b, z)


def _rnn_gate_bwd(d_u, h_f, h_b, z, xg_off, n_ctx_rows, tr, tcol):
    t, d = h_f.shape
    s = t - n_ctx_rows
    grid, out = _tiles2d(s, d, tr, tcol)
    hs = _zspec(tr, tcol, n_ctx_rows, 0)

    def body(du_ref, hf_ref, hb_ref, xg_ref, dr_ref, dxg_ref):
        du = du_ref[...]
        xg = xg_ref[...]
        dr_ref[...] = du * _gelu(xg)
        dxg_ref[...] = (du * (hf_ref[...] + hb_ref[...]) * _gelu_grad(xg)).astype(BF16)

    return pl.pallas_call(body, name="rnn_gate_bwd", grid=grid,
                          in_specs=[out, hs, hs, _zspec(tr, tcol, n_ctx_rows, xg_off)], out_specs=[out, out],
                          out_shape=[_sds((s, d)), _sds((s, d), BF16)],
                          compiler_params=_params("parallel", "parallel"))(d_u, h_f, h_b, z)


def _merge_fwd(y_attn, y_rnn, z, gl_off, n_ctx_rows, tr, tcol):
    s, d = y_attn.shape
    grid, out = _tiles2d(s, d, tr, tcol)

    def body(ya_ref, yr_ref, ga_ref, gr_ref, o_ref):
        o_ref[...] = (jax.nn.sigmoid(ga_ref[...]) * ya_ref[...] + jax.nn.sigmoid(gr_ref[...]) * yr_ref[...]).astype(BF16)

    return pl.pallas_call(
        body, name="merge_fwd", grid=grid,
        in_specs=[out, out, _zspec(tr, tcol, n_ctx_rows, gl_off), _zspec(tr, tcol, n_ctx_rows, gl_off + d)],
        out_specs=out, out_shape=_sds((s, d), BF16), compiler_params=_params("parallel", "parallel"),
    )(y_attn, y_rnn, z, z)


def _merge_bwd(d_mrg, y_attn, y_rnn, z, gl_off, n_ctx_rows, tr, tcol):
    s, d = y_attn.shape
    grid, out = _tiles2d(s, d, tr, tcol)

    def body(dm_ref, ya_ref, yr_ref, ga_ref, gr_ref, dya_ref, dyr_ref, dga_ref, dgr_ref):
        dm = dm_ref[...]
        ga = jax.nn.sigmoid(ga_ref[...])
        gr = jax.nn.sigmoid(gr_ref[...])
        dya_ref[...] = (dm * ga).astype(BF16)
        dyr_ref[...] = (dm * gr).astype(BF16)
        dga_ref[...] = (dm * ya_ref[...] * ga * (1.0 - ga)).astype(BF16)
        dgr_ref[...] = (dm * yr_ref[...] * gr * (1.0 - gr)).astype(BF16)

    return pl.pallas_call(
        body, name="merge_bwd", grid=grid,
        in_specs=[out, out, out, _zspec(tr, tcol, n_ctx_rows, gl_off), _zspec(tr, tcol, n_ctx_rows, gl_off + d)],
        out_specs=[out] * 4, out_shape=[_sds((s, d), BF16)] * 4, compiler_params=_params("parallel", "parallel"),
    )(d_mrg, y_attn, y_rnn, z, z)


def _sq_relu(up, tr, tcol, side=None):
    grid, out = _tiles2d(*up.shape, _tile(up.shape[0], 2 * tr, 16), _tile(up.shape[1], 4 * tcol, LANES))

    def body(u_ref, o_ref):
        r = jnp.maximum(u_ref[...], 0.0)
        o_ref[...] = (r * r).astype(BF16)

    return _call(body, side=side, name="sq_relu", grid=grid, in_specs=[out], out_specs=out,
                 out_shape=_sds(up.shape, BF16), sem=("parallel", "parallel"))(up)


def _sq_relu_bwd(d_act, up, tr, tcol, side=None):
    grid, out = _tiles2d(*up.shape, _tile(up.shape[0], 2 * tr, 16), _tile(up.shape[1], 4 * tcol, LANES))

    def body(d_ref, u_ref, o_ref):
        o_ref[...] = (d_ref[...] * 2.0 * jnp.maximum(u_ref[...], 0.0)).astype(BF16)

    return _call(body, side=side, name="sq_relu_bwd", grid=grid, in_specs=[out, out], out_specs=out,
                 out_shape=_sds(up.shape, BF16), sem=("parallel", "parallel"))(d_act, up)


def _cast_into_window(w, chip, col_sharded, name):
    r, c = w.shape
    tr, tcol = _tile(r, 512, 16), _tile(c, 1024, LANES)
    nrb, ncb = r // tr, c // tcol

    def body(chip_ref, w_ref, o_ref):
        o_ref[...] = w_ref[...].astype(BF16)

    if col_sharded:
        omap = lambda i, j, chip_ref: (i, chip_ref[0] * ncb + j)
    else:
        omap = lambda i, j, chip_ref: (chip_ref[0] * nrb + i, j)
    return pl.pallas_call(
        body, name=name,
        grid_spec=pltpu.PrefetchScalarGridSpec(
            num_scalar_prefetch=1, grid=(nrb, ncb),
            in_specs=[pl.BlockSpec((tr, tcol), lambda i, j, chip_ref: (i, j))], out_specs=pl.BlockSpec((tr, tcol), omap)),
        out_shape=_sds((r, c * N_CHIPS) if col_sharded else (r * N_CHIPS, c), BF16),
        compiler_params=_params("parallel", "parallel"),
    )(chip, w)


def _sum_leading(parts, name):
    n, r, c = parts.shape
    tr, tcol = _tile(r, 512, SUBLANES), _tile(c, 1024, LANES)

    def body(p_ref, o_ref):
        tot = p_ref[0]
        for k in range(1, n):
            tot = tot + p_ref[k]
        o_ref[...] = tot

    return pl.pallas_call(
        body, name=name, grid=(r // tr, c // tcol), in_specs=[pl.BlockSpec((n, tr, tcol), lambda i, j: (0, i, j))],
        out_specs=pl.BlockSpec((tr, tcol), lambda i, j: (i, j)), out_shape=_sds((r, c)),
        compiler_params=_params("parallel", "parallel"),
    )(parts)


def _add_half(full, other, core, split_rows, name):
    r, c = other.shape
    tr, tcol = _tile(r, 512, 16), _tile(c, 1024, LANES)
    nrb, ncb = r // tr, c // tcol

    def body(core_ref, f_ref, o_ref, out_ref):
        out_ref[...] = (f_ref[...].astype(F32) + o_ref[...].astype(F32)).astype(out_ref.dtype)

    if split_rows:
        fmap = lambda i, j, core_ref: (core_ref[0] * nrb + i, j)
    else:
        fmap = lambda i, j, core_ref: (i, core_ref[0] * ncb + j)
    same = lambda i, j, core_ref: (i, j)
    return pl.pallas_call(
        body, name=name,
        grid_spec=pltpu.PrefetchScalarGridSpec(
            num_scalar_prefetch=1, grid=(nrb, ncb),
            in_specs=[pl.BlockSpec((tr, tcol), fmap), pl.BlockSpec((tr, tcol), same)],
            out_specs=pl.BlockSpec((tr, tcol), same)),
        out_shape=_sds((r, c), BF16), compiler_params=_params("parallel", "parallel"),
    )(core, full, other)


def _sum_regions(pair, got, place, col_sharded, name):
    _, r, c = got.shape
    tr, tcol = _tile(r, 512, 16), _tile(c, 1024, LANES)
    nrb, ncb = r // tr, c // tcol

    def body(place_ref, p_ref, g_ref, out_ref):
        tot = p_ref[...].astype(F32)
        for k in range(N_CHIPS - 1):
            tot = tot + g_ref[k].astype(F32)
        out_ref[...] = tot

    if col_sharded:
        pmap = lambda i, j, pr: (i, pr[0] * ncb + j)
        omap = lambda i, j, pr: (pr[1] * nrb + i, j)
        out_shape = (2 * r, c)
    else:
        pmap = lambda i, j, pr: (pr[0] * nrb + i, j)
        omap = lambda i, j, pr: (i, pr[1] * ncb + j)
        out_shape = (r, 2 * c)
    return pl.pallas_call(
        body, name=name,
        grid_spec=pltpu.PrefetchScalarGridSpec(
            num_scalar_prefetch=1, grid=(nrb, ncb),
            in_specs=[pl.BlockSpec((tr, tcol), pmap), pl.BlockSpec((N_CHIPS - 1, tr, tcol), lambda i, j, pr: (0, i, j))],
            out_specs=pl.BlockSpec((tr, tcol), omap)),
        out_shape=_sds(out_shape), compiler_params=_params("parallel", "parallel"),
    )(place, pair, got)


def _adamw(w, g, m, v, name, side=None):
    r, c = w.shape
    tr, tcol = _tile(r, 512, SUBLANES), _tile(c, 1024, LANES)
    blk = pl.BlockSpec((tr, tcol), lambda i, j: (i, j))

    def body(w_ref, g_ref, m_ref, v_ref, d_ref, nm_ref, nv_ref):
        g_ = g_ref[...]
        m_ = ADAM_B1 * m_ref[...] + (1.0 - ADAM_B1) * g_
        v_ = ADAM_B2 * v_ref[...] + (1.0 - ADAM_B2) * (g_ * g_)
        m_hat = m_ / (1.0 - ADAM_B1 ** ADAM_STEP)
        v_hat = v_ / (1.0 - ADAM_B2 ** ADAM_STEP)
        d_ref[...] = -ADAM_LR * (m_hat / (jnp.sqrt(v_hat) + ADAM_EPS) + ADAM_WD * w_ref[...])
        nm_ref[...] = m_
        nv_ref[...] = v_

    return _call(body, side=side, name=name, grid=(r // tr, c // tcol), in_specs=[blk] * 4, out_specs=[blk] * 3,
                 out_shape=[_sds((r, c))] * 3, sem=("parallel", "parallel"))(w, g, m, v)


def _place():
    x, y, c = lax.axis_index("x"), lax.axis_index("y"), lax.axis_index("c")
    chips = [(1 - x, y), (x, 1 - y), (1 - x, 1 - y)]
    return x, y, c, chips


def _all_gather8(blk, name):
    m, n = blk.shape

    def body(x_ref, out_ref, send_sems, recv_sems, local_sem):
        x, y, c, chips = _place()
        me, sibling = (x, y, c), (x, y, 1 - c)

        def rows(px, py, pc):
            return out_ref.at[pl.ds((4 * px + 2 * py + pc) * m, m), :]

        def copy(k, block, to, src=None):
            return pltpu.make_async_remote_copy(
                src_ref=rows(*block) if src is None else src, dst_ref=rows(*block), send_sem=send_sems.at[k],
                recv_sem=recv_sems.at[k], device_id=to, device_id_type=MESH_ID)

        mine = pltpu.make_async_copy(x_ref, rows(*me), local_sem)
        mine.start()
        first = [copy(0, me, sibling, src=x_ref)]
        first += [copy(1 + j, me, (*chip, c), src=x_ref) for j, chip in enumerate(chips)]
        for cp in first:
            cp.start()
        passed = [copy(4 + j, (*chip, c), sibling) for j, chip in enumerate(chips)]
        for j, chip in enumerate(chips):
            copy(1 + j, (*chip, c), me).wait_recv()
            passed[j].start()
        copy(0, sibling, me).wait_recv()
        for j, chip in enumerate(chips):
            copy(4 + j, (*chip, 1 - c), me).wait_recv()
        for cp in first + passed:
            cp.wait_send()
        mine.wait()

    return pl.pallas_call(
        body, name=name, out_shape=_sds((N_DEV * m, n), blk.dtype), in_specs=[ANY], out_specs=ANY,
        scratch_shapes=[pltpu.SemaphoreType.DMA((7,)), pltpu.SemaphoreType.DMA((7,)), pltpu.SemaphoreType.DMA],
    )(blk)


def _half(ref, core, split_rows):
    r, c = ref.shape
    if split_rows:
        return ref.at[pl.ds(core * (r // 2), r // 2), :]
    return ref.at[:, pl.ds(core * (c // 2), c // 2)]


def _chip_block(ref, j, col_sharded):
    r, c = ref.shape
    if col_sharded:
        return ref.at[:, pl.ds(j * (c // N_CHIPS), c // N_CHIPS)]
    return ref.at[pl.ds(j * (r // N_CHIPS), r // N_CHIPS), :]


def _rows_part(ref, part):
    lo, hi, n = part
    r = ref.shape[0]
    return ref if (lo, hi) == (0, n) else ref.at[pl.ds(lo * (r // n), (hi - lo) * (r // n)), :]


def _copy(send_sems, recv_sems, k, src, dst, to):
    return pltpu.make_async_remote_copy(src_ref=src, dst_ref=dst, send_sem=send_sems.at[k], recv_sem=recv_sems.at[k],
                                        device_id=to, device_id_type=MESH_ID)


def _in_place(arrays):
    return tuple(arrays), tuple(_sds(a.shape, a.dtype) for a in arrays), {i: i for i in range(len(arrays))}


def _gather_ici(fulls, col_sharded, part=(0, 1, 1)):
    nw = len(fulls)

    def build(_, refs, send_sems, recv_sems, sem0):
        x, y, c, chips = _place()
        sends, recvs = [], []
        for w in range(nw):
            win = lambda j: _rows_part(_half(_chip_block(refs[w], j, col_sharded[w]), c, True), part)
            for k, (cx, cy) in enumerate(chips):
                sem = sem0 + 3 * w + k
                sends.append(_copy(send_sems, recv_sems, sem, win(2 * x + y), win(2 * x + y), (cx, cy, c)))
                recvs.append(_copy(send_sems, recv_sems, sem, win(2 * cx + cy), win(2 * cx + cy), (cx, cy, c)))
        return sends, recvs

    return _Side(*_in_place(fulls), 3 * nw, build)


def _gather_d2d(fulls, col_sharded):
    nw = len(fulls)

    def build(_, refs, send_sems, recv_sems, sem0):
        x, y, c, chips = _place()
        sends, recvs = [], []
        for w in range(nw):
            win = lambda j, core: _half(_chip_block(refs[w], j, col_sharded[w]), core, True)
            for k, (cx, cy) in enumerate(chips):
                sem = sem0 + 3 * w + k
                sends.append(_copy(send_sems, recv_sems, sem, win(2 * cx + cy, c), win(2 * cx + cy, c), (x, y, 1 - c)))
                recvs.append(_copy(send_sems, recv_sems, sem, win(2 * cx + cy, 1 - c), win(2 * cx + cy, 1 - c),
                                   (x, y, 1 - c)))
        return sends, recvs

    return _Side(*_in_place(fulls), 3 * nw, build)


def _exchange(side, name):
    return _call(None, side=side, name=name)()[1]


def _swap_halves(grads, col_sharded):
    nw = len(grads)
    out_shapes = [_sds((g.shape[0] // 2, g.shape[1]) if col else (g.shape[0], g.shape[1] // 2), g.dtype)
                  for g, col in zip(grads, col_sharded)]

    def build(g_refs, o_refs, send_sems, recv_sems, sem0):
        x, y, c, _ = _place()
        copies = [_copy(send_sems, recv_sems, sem0 + w, _half(g_refs[w], 1 - c, col_sharded[w]), o_refs[w],
                        (x, y, 1 - c)) for w in range(nw)]
        return copies, copies

    return _Side(tuple(grads), tuple(out_shapes), {}, nw, build)


def _scatter_regions(pairs, col_sharded, part=(0, 1, 1), into=None):
    nw = len(pairs)

    def region_shape(p, col):
        return (p.shape[0], p.shape[1] // N_CHIPS) if col else (p.shape[0] // N_CHIPS, p.shape[1])

    out_shapes = tuple(_sds((N_CHIPS - 1, *region_shape(p, col)), p.dtype) for p, col in zip(pairs, col_sharded))

    def build(refs, o_refs, send_sems, recv_sems, sem0):
        x, y, c, chips = _place()
        copies = []
        for w in range(nw):
            for k, (cx, cy) in enumerate(chips):
                copies.append(_copy(
                    send_sems, recv_sems, sem0 + 3 * w + k,
                    _rows_part(_chip_block(refs[w], 2 * cx + cy, col_sharded[w]), part),
                    _rows_part(o_refs[w].at[k], part), (cx, cy, c)))
        return copies, copies

    if into is None:
        return _Side(tuple(pairs), out_shapes, {}, 3 * nw, build)
    return _Side((*pairs, *into), out_shapes, {nw + w: w for w in range(nw)}, 3 * nw, build)


def _join_halves(halves, col_sharded):
    nw = len(halves)

    def build(_, refs, send_sems, recv_sems, sem0):
        x, y, c, _ = _place()
        sends, recvs = [], []
        for w in range(nw):
            mine, theirs = _half(refs[w], c, col_sharded[w]), _half(refs[w], 1 - c, col_sharded[w])
            sends.append(_copy(send_sems, recv_sems, sem0 + w, mine, mine, (x, y, 1 - c)))
            recvs.append(_copy(send_sems, recv_sems, sem0 + w, theirs, theirs, (x, y, 1 - c)))
        return sends, recvs

    return _Side(*_in_place(halves), nw, build)


def _part_rows(size):
    return -(-size // (SUBLANES * LANES)) * SUBLANES


def _pack(arrays, pad_rows_to=SUBLANES):
    flat = [jnp.pad(a.reshape(-1), (0, _part_rows(a.size) * LANES - a.size)).reshape(-1, LANES) for a in arrays]
    rows = sum(f.shape[0] for f in flat)
    pad = (-rows) % pad_rows_to
    if pad:
        flat.append(jnp.zeros((pad, LANES), F32))
    return jnp.concatenate(flat, axis=0)


def _unpack(packed, shapes):
    out, r = [], 0
    for shp in shapes:
        size = math.prod(shp)
        out.append(packed[r:r + _part_rows(size)].reshape(-1)[:size].reshape(shp))
        r += _part_rows(size)
    return out


def _rope_tables(n_ctx_rows, s):
    rows = s // GRID_W
    row_idx = jnp.repeat(jnp.arange(rows), GRID_W)
    col_idx = jnp.tile(jnp.arange(GRID_W), rows)
    n_freq = LANES // 4
    inv_freq = ROPE_THETA ** (-jnp.arange(n_freq, dtype=F32) / n_freq)
    ang = jnp.concatenate([row_idx.astype(F32)[:, None] * inv_freq, col_idx.astype(F32)[:, None] * inv_freq], axis=-1)
    cos = jnp.repeat(jnp.cos(ang), 2, axis=-1)
    sin = jnp.repeat(jnp.sin(ang), 2, axis=-1) * jnp.tile(jnp.array([-1.0, 1.0], F32), LANES // 2)
    cos = jnp.concatenate([jnp.ones((n_ctx_rows, LANES), F32), cos], axis=0)
    sin = jnp.concatenate([jnp.zeros((n_ctx_rows, LANES), F32), sin], axis=0)
    return cos, sin


WEIGHT_NAMES = ['c_ctx', 'w_mod', 'b_mod', 'g_mix', 'g_mlp', 'w_in', 'q_gain', 'k_gain', 'conv_w', 'conv_b', 'w_rg',
                'b_rg', 'w_ig', 'b_ig', 'lru_lambda', 'w_o_attn', 'w_o_rnn', 'w_out', 'w_up', 'w_down', 'g_final']
BIG = ['w_in', 'w_o_attn', 'w_o_rnn', 'w_out', 'w_up', 'w_down']
BIG_COL_SHARDED = [True, False, False, False, True, False]
GATES = ['w_rg', 'w_ig']
SMALL = ['c_ctx', 'b_mod', 'g_mix', 'g_mlp', 'q_gain', 'k_gain', 'conv_b', 'g_final',
         'conv_w', 'b_rg', 'b_ig', 'lru_lambda']


def kernel(x, c, ctx, c_ctx, w_mod, b_mod, g_mix, g_mlp, w_in, q_gain, k_gain, conv_w, conv_b, w_rg, b_rg, w_ig, b_ig, lru_lambda, w_o_attn, w_o_rnn, w_out, w_up, w_down, g_final, loss_target, m_c_ctx, m_w_mod, m_b_mod, m_g_mix, m_g_mlp, m_w_in, m_q_gain, m_k_gain, m_conv_w, m_conv_b, m_w_rg, m_b_rg, m_w_ig, m_b_ig, m_lru_lambda, m_w_o_attn, m_w_o_rnn, m_w_out, m_w_up, m_w_down, m_g_final, v_c_ctx, v_w_mod, v_b_mod, v_g_mix, v_g_mlp, v_w_in, v_q_gain, v_k_gain, v_conv_w, v_conv_b, v_w_rg, v_b_rg, v_w_ig, v_b_ig, v_lru_lambda, v_w_o_attn, v_w_o_rnn, v_w_out, v_w_up, v_w_down, v_g_final):
    given = dict(locals())
    weights = {n: given[n] for n in WEIGHT_NAMES}
    moms = {n: given["m_" + n] for n in WEIGHT_NAMES}
    vars_ = {n: given["v_" + n] for n in WEIGHT_NAMES}

    s, d = x.shape[1], x.shape[2]
    n_ctx = ctx.shape[1]
    t = n_ctx + s
    hd = q_gain.shape[1]
    assert hd == LANES and w_rg.shape[-1] == LANES
    attn_w = w_o_attn.shape[1] * N_CHIPS
    n_in = w_in.shape[2] * N_CHIPS
    kv_w = (n_in - attn_w - 4 * d) // 2
    group = attn_w // kv_w
    k_off, v_off, xr_off = attn_w, attn_w + kv_w, attn_w + 2 * kv_w
    xg_off, gl_off = xr_off + d, xr_off + 2 * d
    d_mod = N_MOD * d
    tr = _tile(math.gcd(n_ctx, s), 256, 16)
    tcol = _tile(math.gcd(d, xr_off), 512, LANES)
    xi, yi, ci = lax.axis_index("x"), lax.axis_index("y"), lax.axis_index("c")
    chip = 2 * xi + yi
    core = ci.astype(jnp.int32).reshape(1)

    sharded_small = [conv_w[0], b_rg[0], b_ig[0], lru_lambda[0]]
    pack0 = _pack([c[0]] + sharded_small)
    got0 = _all_gather8(pack0, "gather_small_inputs").reshape(N_DEV, -1, LANES)
    c_all = got0[:, :_part_rows(d)].reshape(N_DEV, -1)[:, :d]
    per_chip = [_unpack(got0[2 * j, _part_rows(d):], [a.shape for a in sharded_small]) for j in range(N_CHIPS)]
    conv_w_f, b_rg_f, b_ig_f, lam_f = (jnp.concatenate([per_chip[j][i] for j in range(N_CHIPS)], axis=-1)
                                       for i in range(4))
    c16 = jnp.concatenate([c_all, c_ctx[None, :], jnp.zeros((16 - N_DEV - 1, d), F32)], axis=0)
    b_mod_shard = lax.dynamic_slice(b_mod, (0, chip * (d_mod // N_CHIPS)), (1, d_mod // N_CHIPS))
    mod_part, silu16 = _mod_fwd(c16, w_mod[0], b_mod_shard)
    mod_all = _all_gather8(mod_part, "gather_mod").reshape(N_DEV, 16, d_mod // N_CHIPS)
    mod16 = jnp.concatenate([mod_all[2 * j] for j in range(N_CHIPS)], axis=-1)
    me = 4 * xi + 2 * yi + ci
    mod_lat = lax.dynamic_slice(mod16, (me, 0), (1, d_mod)).reshape(N_MOD, d)
    mod_ctx = mod16[N_DEV].reshape(N_MOD, d)
    mod4 = jnp.stack([mod_ctx[0], mod_ctx[1], mod_lat[0], mod_lat[1]])
    mod3 = jnp.stack([mod_lat[2], mod_lat[3], mod_lat[4]])
    gate_f = mod_lat[5][None, :]

    chip_arr = chip.astype(jnp.int32).reshape(1)
    own = {n: _cast_into_window(weights[n][0], chip_arr, col, "cast_" + n) for n, col in zip(BIG, BIG_COL_SHARDED)}
    place = jnp.stack([chip, ci]).astype(jnp.int32)
    row3 = [False] * 3

    def pair_sum(n, full, other, col):
        return _add_half(full, other, core, col, "pair_sum_" + n)

    def chip_sum(n, pair, got, col):
        return _sum_regions(pair, got, place, col, "chip_sum_" + n)

    (w_in_f,) = _exchange(_gather_ici([own['w_in']], [True]), "gather_w_in_ici")
    (w_in_f,) = _exchange(_gather_d2d([w_in_f], [True]), "gather_w_in_d2d")
    xcat = jnp.concatenate([ctx[0], x[0]], axis=0)
    cos, sin = _rope_tables(n_ctx, s)
    h = _norm_mod_fwd(xcat, g_mix, mod4, n_ctx, tr)
    z, (w_oa_f, w_or_f, w_out_f, w_down_f) = _matmul(h, w_in_f, name="mm_in", side=_sides(
        _gather_ici([own['w_o_attn'], own['w_o_rnn'], own['w_out']], row3), _gather_ici([own['w_down']], [False], (0, 1, 2))))
    qr, (w_oa_f, w_or_f, w_out_f) = _head_prep_fwd(z, 0, attn_w // LANES, q_gain, cos, sin, tr, "q_prep",
                                                   side=_gather_d2d([w_oa_f, w_or_f, w_out_f], row3))
    kr = _head_prep_fwd(z, k_off, kv_w // LANES, k_gain, cos, sin, tr, "k_prep")
    attn_o, (w_up_f,) = _attn_fwd(qr, kr, z, v_off, n_ctx, group, tr, side=_gather_ici([own['w_up']], [True]))
    (xc, a_f, bx_f, a_b, bx_b), (w_up_f,) = _rnn_prep(z, xr_off, conv_w_f, conv_b, w_rg[0], b_rg_f, w_ig[0], b_ig_f, lam_f,
                                                      n_ctx, side=_gather_d2d([w_up_f], [True]))
    h_f = _scan(a_f, bx_f, order="ctx_lat_up", post=False, n_ctx_rows=n_ctx, name="scan_f")
    h_b = _scan(a_b, bx_b, order="ctx_lat_down", post=False, n_ctx_rows=n_ctx, name="scan_b")
    u = _rnn_gate_fwd(h_f, h_b, z, xg_off, n_ctx, tr, tcol)
    y_attn = _matmul(attn_o, w_oa_f, name="mm_o_attn")
    y_rnn = _matmul(u, w_or_f, name="mm_o_rnn")
    mrg = _merge_fwd(y_attn, y_rnn, z, gl_off, n_ctx, tr, tcol)
    mix = _matmul(mrg, w_out_f, name="mm_out")
    x1, h2 = _resid_norm_mod_fwd(x[0], mix, g_mlp, mod3, tr)
    up, (w_down_f,) = _matmul(h2, w_up_f, name="mm_up", side=_gather_ici([w_down_f], [False], (1, 2, 2)))
    act, (w_down_f,) = _sq_relu(up, tr, tcol, side=_gather_d2d([w_down_f], [False]))
    down = _matmul(act, w_down_f, name="mm_down")
    dx2, d_down, sums_fin, loss_blk = _final_fwd_bwd(x1, down, loss_target[0], g_final[None, :], gate_f, tr)

    d_act = _matmul(d_down, w_down_f, tb=True, name="mm_d_act")
    g_w_down = _matmul(act, d_down, ta=True, out_dtype=BF16, name="mm_g_down")
    d_up, (got,) = _sq_relu_bwd(d_act, up, tr, tcol, side=_swap_halves([g_w_down], [False]))
    p_down = pair_sum('w_down', g_w_down, got, False)
    dh2 = _matmul(d_up, w_up_f, tb=True, name="mm_d_h2")
    g_w_up = _matmul(h2, d_up, ta=True, out_dtype=BF16, name="mm_g_up")
    dx1, d_mix, sums2 = _norm_mod_bwd2(x1, dh2, dx2, mix, g_mlp, mod3, tr)
    d_mrg, (got,) = _matmul(d_mix, w_out_f, tb=True, name="mm_d_mrg", side=_swap_halves([g_w_up], [True]))
    p_up = pair_sum('w_up', g_w_up, got, True)
    g_w_out = _matmul(mrg, d_mix, ta=True, out_dtype=BF16, name="mm_g_out")
    d_ya, d_yr, d_gla, d_glr = _merge_bwd(d_mrg, y_attn, y_rnn, z, gl_off, n_ctx, tr, tcol)
    d_o = _matmul(d_ya, w_oa_f, tb=True, out_dtype=BF16, name="mm_d_o")
    g_w_oa = _matmul(attn_o, d_ya, ta=True, out_dtype=BF16, name="mm_g_o_attn")
    d_u = _matmul(d_yr, w_or_f, tb=True, name="mm_d_u")
    g_w_or = _matmul(u, d_yr, ta=True, out_dtype=BF16, name="mm_g_o_rnn")
    d_rnn, d_xg = _rnn_gate_bwd(d_u, h_f, h_b, z, xg_off, n_ctx, tr, tcol)
    gs_f = _scan(a_f, d_rnn, order="lat_ctx_down", post=True, n_ctx_rows=n_ctx, name="scan_f_bwd")
    gs_b = _scan(a_b, d_rnn, order="lat_ctx_up", post=True, n_ctx_rows=n_ctx, name="scan_b_bwd")
    (d_xr, g_w_rg, g_w_ig, sums_rnn), (got,) = _rnn_bwd(
        z, xr_off, xc, gs_f, gs_b, h_f, h_b, conv_w_f, w_rg[0], b_rg_f, w_ig[0], b_ig_f, lam_f, n_ctx,
        side=_scatter_regions([p_down], [False]))
    hs_down = chip_sum('w_down', p_down, got, False)
    gate_cols = 8 * LANES if g_w_rg.size % (8 * LANES * N_CHIPS * 16) == 0 else 2 * LANES
    gate_rows = g_w_rg.size // gate_cols
    mid_names = ['w_o_attn', 'w_o_rnn', 'w_out'] + GATES
    mid_grads = [g_w_oa, g_w_or, g_w_out, g_w_rg.reshape(gate_rows, gate_cols), g_w_ig.reshape(gate_rows, gate_cols)]
    row5 = [False] * 5
    (dq, dk, dv), (got_up, *got_mid, gs_down) = _attn_bwd(qr, kr, z, v_off, d_o, n_ctx, group, tr, side=_sides(
        _scatter_regions([p_up], [True]), _swap_halves(mid_grads, row5), _join_halves([hs_down], [False])))
    hs_up = chip_sum('w_up', p_up, got_up, True)
    p_mid = [pair_sum(n, g, o, False) for n, g, o in zip(mid_names, mid_grads, got_mid)]
    d_q_raw, g_q_gain = _head_prep_bwd(z, 0, attn_w // LANES, q_gain, cos, sin, dq, n_ctx, tr, "q_prep_bwd")
    d_k_raw, g_k_gain = _head_prep_bwd(z, k_off, kv_w // LANES, k_gain, cos, sin, dk, 0, tr, "k_prep_bwd")
    zero_ctx = lambda w: jnp.zeros((n_ctx, w), BF16)
    dz = jnp.concatenate([
        jnp.concatenate([zero_ctx(attn_w), d_q_raw], axis=0), d_k_raw, dv.astype(BF16), d_xr,
        jnp.concatenate([zero_ctx(d), d_xg], axis=0), jnp.concatenate([zero_ctx(d), d_gla], axis=0),
        jnp.concatenate([zero_ctx(d), d_glr], axis=0)], axis=1)
    g_w_in, (*got_mid, gs_up) = _matmul(h, dz, ta=True, out_dtype=BF16, name="mm_g_in", side=_sides(
        _scatter_regions(p_mid, row5), _join_halves([hs_up], [True])))
    hs_mid = [chip_sum(n, p, o, False) for n, p, o in zip(mid_names, p_mid, got_mid)]
    (got,) = _exchange(_swap_halves([g_w_in], [True]), "swap_w_in")
    p_in = pair_sum('w_in', g_w_in, got, True)
    dh, (got_in,) = _matmul(dz, w_in_f, tb=True, name="mm_d_h", side=_scatter_regions([p_in], [True], (0, 3, 4)))
    grad_x, sums1 = _norm_mod_bwd1(xcat, dh, dx1, g_mix, mod4, n_ctx, tr)

    zeros_d = jnp.zeros((d,), F32)
    dmod_lat = jnp.concatenate([sums1[0], sums1[1], sums2[3], sums2[0], sums2[1], sums_fin[1]])
    dmod_ctx = jnp.concatenate([sums1[3], sums1[4]] + [zeros_d] * 4)
    small_parts = [dmod_lat, dmod_ctx, loss_blk[0, 0:1], sums1[2] + sums1[5], sums2[2], g_q_gain[0], g_k_gain[0],
                   sums_rnn[10], sums_fin[0], sums_rnn[6:10], sums_rnn[0:2], sums_rnn[2:4], sums_rnn[4:6]]
    pack1 = _pack(small_parts)
    got1 = _all_gather8(pack1, "gather_small_grads").reshape(N_DEV, -1, LANES)
    tot1 = _sum_leading(got1, "sum_small_grads")
    part_shapes = [a.shape for a in small_parts]
    (s_dmod_lat, s_dmod_ctx, s_loss, g_g_mix, g_g_mlp, g_q_gain, g_k_gain, g_conv_b, g_g_final,
     g_conv_w_f, g_b_rg_f, g_b_ig_f, g_lam_f) = _unpack(tot1, part_shapes)
    loss = s_loss[0]
    g_b_mod = (s_dmod_lat + s_dmod_ctx)[None, :]
    n_mod_rows = _part_rows(d_mod)
    dmod16 = jnp.concatenate([got1[:, :n_mod_rows].reshape(N_DEV, -1)[:, :d_mod], s_dmod_ctx[None, :],
                              jnp.zeros((16 - N_DEV - 1, d_mod), F32)], axis=0)
    dmod16_shard = lax.dynamic_slice(dmod16, (0, chip * (d_mod // N_CHIPS)), (16, d_mod // N_CHIPS))
    g_w_mod = _matmul(silu16, dmod16_shard, ta=True, name="mm_g_mod")
    dsilu_part = _matmul(dmod16_shard[N_DEV:], w_mod[0], tb=True, name="mm_d_silu")
    dsilu_all = _all_gather8(dsilu_part, "gather_d_silu").reshape(N_DEV, 8, d)
    g_c_ctx = _c_ctx_grad(dsilu_all, c_ctx[None, :])[0]

    def shard_of(full):
        w = full.shape[-1] // N_CHIPS
        return lax.dynamic_slice(full, (0, chip * w), (full.shape[0], w))

    grads = {
        'c_ctx': g_c_ctx, 'b_mod': g_b_mod, 'g_mix': g_g_mix[None, :], 'g_mlp': g_g_mlp[None, :],
        'q_gain': g_q_gain[None, :], 'k_gain': g_k_gain[None, :], 'conv_b': g_conv_b[None, :],
        'g_final': g_g_final,
        'conv_w': shard_of(g_conv_w_f)[None], 'b_rg': shard_of(g_b_rg_f)[None], 'b_ig': shard_of(g_b_ig_f)[None],
        'lru_lambda': shard_of(g_lam_f)[None], 'w_mod': g_w_mod[None],
    }

    delta, new_m, new_v = {}, {}, {}

    def adamw(n, side=None):
        shp = weights[n].shape
        as2d = (lambda a: a[0]) if n not in GATES else (lambda a: a.reshape(-1, LANES))
        out = _adamw(as2d(weights[n]), as2d(grads[n]), as2d(moms[n]), as2d(vars_[n]), "adamw_" + n, side=side)
        (dl, nm, nv), got = out if side is not None else (out, None)
        delta[n], new_m[n], new_v[n] = dl.reshape(shp), nm.reshape(shp), nv.reshape(shp)
        return got

    (got_in,) = adamw('w_mod', side=_scatter_regions([p_in], [True], (3, 4, 4), into=[got_in]))
    hs_in = chip_sum('w_in', p_in, got_in, True)
    rest = _exchange(_join_halves([hs_in] + hs_mid, [True] + row5), "join_rest")
    for n, g in zip(['w_in', 'w_o_attn', 'w_o_rnn', 'w_out', 'w_up', 'w_down'], [*rest[:4], gs_up, gs_down]):
        grads[n] = g[None]
    gate_all = _all_gather8(jnp.concatenate(rest[4:], axis=0), "gather_gate_grads")
    gate_all = gate_all.reshape(N_CHIPS, 2, len(GATES), gate_rows // N_CHIPS, gate_cols)[:, 0]
    for i, n in enumerate(GATES):
        grads[n] = gate_all[:, i].reshape(weights[n].shape)
    for n in BIG + GATES:
        adamw(n)
    small_shapes = [weights[n].shape for n in SMALL]
    packed = [_pack([src[n] for n in SMALL], 512) for src in (weights, grads, moms, vars_)]
    outs = _adamw(*packed, "adamw_small")
    for res, out in zip((delta, new_m, new_v), outs):
        for n, a in zip(SMALL, _unpack(out, small_shapes)):
            res[n] = a
    return (loss, grad_x[None], *[grads[n] for n in WEIGHT_NAMES], *[delta[n] for n in WEIGHT_NAMES],
            *[new_m[n] for n in WEIGHT_NAMES], *[new_v[n] for n in WEIGHT_NAMES])
```

```python
import functools
import math
from typing import Callable, NamedTuple

import jax
import jax.numpy as jnp
from jax import lax
from jax.experimental import pallas as pl
from jax.experimental.pallas import tpu as pltpu

F32 = jnp.float32
BF16 = jnp.bfloat16
MESH_ID = pl.DeviceIdType.MESH
ANY = pl.BlockSpec(memory_space=pl.ANY)

NORM_EPS = 1e-6
LRU_C = 8.0
GRID_W = 64
ROPE_THETA = 10000.0
N_MOD = 6
CONV_WIDTH = 4
ADAM_LR = 0.001
ADAM_B1 = 0.9
ADAM_B2 = 0.999
ADAM_EPS = 1e-08
ADAM_WD = 0.01
ADAM_STEP = 10

LANES = 128
SUBLANES = 8
V7X_VMEM_LIMIT = 48 * 1024 * 1024
N_CHIPS = 4
N_DEV = 8
GELU_C = math.sqrt(2.0 / math.pi)
GELU_A = 0.044715


def _tile(dim, pref, align):
    t = min(pref, dim)
    t -= t % align
    while t >= align:
        if dim % t == 0:
            return t
        t -= align
    return dim


def _params(*sem):
    return pltpu.CompilerParams(dimension_semantics=sem, vmem_limit_bytes=V7X_VMEM_LIMIT)


def _sds(shape, dtype=F32):
    return jax.ShapeDtypeStruct(shape, dtype)


class _Side(NamedTuple):
    operands: tuple
    results: tuple
    aliases: dict
    n_sems: int
    build: Callable


def _sides(*sides):
    ops, res, aliases, spans, n = [], [], {}, [], 0
    for s in sides:
        spans.append((len(ops), len(res), n))
        aliases.update({len(ops) + i: len(res) + j for i, j in s.aliases.items()})
        ops += s.operands
        res += s.results
        n += s.n_sems

    def build(op_refs, res_refs, send_sems, recv_sems, sem0):
        sends, recvs = [], []
        for s, (o, r, k) in zip(sides, spans):
            a, b = s.build(op_refs[o:o + len(s.operands)], res_refs[r:r + len(s.results)], send_sems, recv_sems,
                           sem0 + k)
            sends += a
            recvs += b
        return sends, recvs

    return _Side(tuple(ops), tuple(res), aliases, n, build)


def _call(body, *, side=None, sem=(), grid=(), in_specs=(), out_specs=(), out_shape=(), scratch_shapes=(), **kw):
    if side is None:
        return pl.pallas_call(body, grid=grid, in_specs=list(in_specs), out_specs=out_specs, out_shape=out_shape,
                              scratch_shapes=list(scratch_shapes), compiler_params=_params(*sem), **kw)
    many = isinstance(out_shape, (list, tuple))
    out_specs_l, out_shape_l = (list(out_specs), list(out_shape)) if many else ([out_specs], [out_shape])
    n_in, n_out, n_scr = len(in_specs), len(out_shape_l), len(scratch_shapes)
    n_op, n_res = len(side.operands), len(side.results)

    def hosted(*refs):
        ins, ops = refs[:n_in], refs[n_in:n_in + n_op]
        outs = refs[n_in + n_op:n_in + n_op + n_out]
        res = refs[n_in + n_op + n_out:n_in + n_op + n_out + n_res]
        scr = refs[n_in + n_op + n_out + n_res:-2]
        send_sems, recv_sems = refs[-2:]

        def start():
            for cp in side.build(ops, res, send_sems, recv_sems, 0)[0]:
                cp.start()

        def finish():
            sends, recvs = side.build(ops, res, send_sems, recv_sems, 0)
            for cp in recvs:
                cp.wait_recv()
            for cp in sends:
                cp.wait_send()

        if not grid:
            start()
            finish()
            return
        ids = [pl.program_id(a) for a in range(len(grid))]
        first = functools.reduce(jnp.logical_and, [i == 0 for i in ids])
        last = functools.reduce(jnp.logical_and, [i == g - 1 for i, g in zip(ids, grid)])
        pl.when(first)(start)
        body(*ins, *outs, *scr)
        pl.when(last)(finish)

    def run(*args):
        got = pl.pallas_call(
            hosted, grid=grid, in_specs=[*in_specs, *[ANY] * n_op], out_specs=[*out_specs_l, *[ANY] * n_res],
            out_shape=[*out_shape_l, *side.results],
            scratch_shapes=[*scratch_shapes, pltpu.SemaphoreType.DMA((side.n_sems,)),
                            pltpu.SemaphoreType.DMA((side.n_sems,))],
            input_output_aliases={n_in + i: n_out + j for i, j in side.aliases.items()},
            compiler_params=_params(*["arbitrary"] * len(grid)), **kw)(*args, *side.operands)
        own = list(got[:n_out]) if many else got[0]
        return own, list(got[n_out:])

    return run


def _matmul(a, b, *, ta=False, tb=False, out_dtype=F32, name, tm=1024, tn=1024, tk=2816, side=None):
    k_dim, m = a.shape if ta else a.shape[::-1]
    n, k2 = b.shape if tb else b.shape[::-1]
    assert k_dim == k2, (a.shape, b.shape, ta, tb)
    tm = _tile(m, tm, LANES if ta else 16)
    tn = _tile(n, tn, 16 if tb else LANES)
    tk = _tile(k_dim, tk, LANES)
    nk = k_dim // tk
    dims = (((0 if ta else 1,), (1 if tb else 0,)), ((), ()))
    if nk == 1:
        def whole(a_ref, b_ref, o_ref):
            o_ref[...] = lax.dot_general(a_ref[...].astype(BF16), b_ref[...].astype(BF16), dims,
                                         preferred_element_type=F32).astype(o_ref.dtype)

        a_spec = pl.BlockSpec((tk, tm), lambda i, j: (0, i)) if ta else pl.BlockSpec((tm, tk), lambda i, j: (i, 0))
        b_spec = pl.BlockSpec((tn, tk), lambda i, j: (j, 0)) if tb else pl.BlockSpec((tk, tn), lambda i, j: (0, j))
        return _call(
            whole, side=side, name=name, grid=(m // tm, n // tn), in_specs=[a_spec, b_spec],
            out_specs=pl.BlockSpec((tm, tn), lambda i, j: (i, j)), out_shape=_sds((m, n), out_dtype),
            sem=("parallel", "parallel"),
        )(a, b)

    def body(a_ref, b_ref, o_ref, acc_ref):
        k = pl.program_id(2)

        @pl.when(k == 0)
        def _():
            acc_ref[...] = jnp.zeros_like(acc_ref)

        acc_ref[...] += lax.dot_general(a_ref[...].astype(BF16), b_ref[...].astype(BF16), dims,
                                        preferred_element_type=F32)

        @pl.when(k == nk - 1)
        def _():
            o_ref[...] = acc_ref[...].astype(o_ref.dtype)

    a_spec = pl.BlockSpec((tk, tm), lambda i, j, k: (k, i)) if ta else pl.BlockSpec((tm, tk), lambda i, j, k: (i, k))
    b_spec = pl.BlockSpec((tn, tk), lambda i, j, k: (j, k)) if tb else pl.BlockSpec((tk, tn), lambda i, j, k: (k, j))
    return _call(
        body, side=side, name=name, grid=(m // tm, n // tn, nk), in_specs=[a_spec, b_spec],
        out_specs=pl.BlockSpec((tm, tn), lambda i, j, k: (i, j)), out_shape=_sds((m, n), out_dtype),
        scratch_shapes=[pltpu.VMEM((tm, tn), F32)], sem=("parallel", "parallel", "arbitrary"),
    )(a, b)


def _silu(x):
    return x * jax.nn.sigmoid(x)


def _gelu(x):
    return 0.5 * x * (1.0 + jnp.tanh(GELU_C * (x + GELU_A * x * x * x)))


def _gelu_grad(x):
    t = jnp.tanh(GELU_C * (x + GELU_A * x * x * x))
    return 0.5 * (1.0 + t) + 0.5 * x * (1.0 - t * t) * GELU_C * (1.0 + 3.0 * GELU_A * x * x)


def _expm1_nonpos(x):
    series = x * (1.0 + x * (1.0 / 2 + x * (1.0 / 6 + x * (1.0 / 24 + x * (1.0 / 120 + x * (1.0 / 720 + x / 5040))))))
    return jnp.where(x > -0.25, series, jnp.exp(x) - 1.0)


def _softplus(x):
    return jnp.maximum(x, 0.0) + jnp.log1p(jnp.exp(-jnp.abs(x)))


def _rms_stats(x):
    return lax.rsqrt(jnp.mean(x * x, axis=-1, keepdims=True) + NORM_EPS)


def _rms_bwd(dxhat, xhat, rstd):
    return rstd * (dxhat - xhat * jnp.mean(dxhat * xhat, axis=-1, keepdims=True))


def _colsum(v):
    return jnp.sum(v, axis=0, keepdims=True)


def _mod_fwd(c16, w_mod, b_mod_shard):
    r, d = c16.shape
    n = w_mod.shape[1]
    tn = _tile(n, 512, LANES)

    def body(c_ref, w_ref, b_ref, o_ref, s_ref):
        s = _silu(c_ref[...])
        s_ref[...] = s
        o_ref[...] = jnp.dot(s.astype(BF16), w_ref[...].astype(BF16), preferred_element_type=F32) + b_ref[...]

    return pl.pallas_call(
        body, name="mod_fwd", grid=(n // tn,),
        in_specs=[pl.BlockSpec((r, d), lambda j: (0, 0)), pl.BlockSpec((d, tn), lambda j: (0, j)),
                  pl.BlockSpec((1, tn), lambda j: (0, j))],
        out_specs=[pl.BlockSpec((r, tn), lambda j: (0, j)), pl.BlockSpec((r, d), lambda j: (0, 0))],
        out_shape=[_sds((r, n)), _sds((r, d))], compiler_params=_params("arbitrary"),
    )(c16, w_mod, b_mod_shard)


def _c_ctx_grad(parts, c_ctx_row):
    d = c_ctx_row.shape[1]

    def body(p_ref, c_ref, o_ref):
        tot = p_ref[0, 0:1, :]
        for chip in range(1, N_CHIPS):
            tot = tot + p_ref[2 * chip, 0:1, :]
        c = c_ref[...]
        sg = jax.nn.sigmoid(c)
        o_ref[...] = tot * (sg * (1.0 + c * (1.0 - sg)))

    return pl.pallas_call(body, name="c_ctx_grad", out_shape=_sds((1, d)), compiler_params=_params())(parts, c_ctx_row)


def _norm_mod_fwd(xcat, g, mod4, n_ctx_rows, tr):
    t, d = xcat.shape
    nctx = n_ctx_rows // tr

    def body(x_ref, g_ref, mod_ref, h_ref):
        x = x_ref[...]
        n = x * _rms_stats(x) * g_ref[...]
        is_ctx = pl.program_id(0) < nctx
        sh = jnp.where(is_ctx, mod_ref[0:1, :], mod_ref[2:3, :])
        sc = jnp.where(is_ctx, mod_ref[1:2, :], mod_ref[3:4, :])
        h_ref[...] = (n * (1.0 + sc) + sh).astype(BF16)

    return pl.pallas_call(
        body, name="norm_mod_fwd", grid=(t // tr,),
        in_specs=[pl.BlockSpec((tr, d), lambda i: (i, 0)), pl.BlockSpec((1, d), lambda i: (0, 0)),
                  pl.BlockSpec((4, d), lambda i: (0, 0))],
        out_specs=pl.BlockSpec((tr, d), lambda i: (i, 0)), out_shape=_sds((t, d), BF16),
        compiler_params=_params("parallel"),
    )(xcat, g, mod4)


def _norm_mod_bwd1(xcat, dh, dx1, g, mod4, n_ctx_rows, tr, side=None):
    t, d = xcat.shape
    nctx = n_ctx_rows // tr
    s = t - n_ctx_rows

    def body(x_ref, dh_ref, dx1_ref, g_ref, mod_ref, dx_ref, sums_ref):
        i = pl.program_id(0)
        is_ctx = i < nctx

        @pl.when(i == 0)
        def _():
            sums_ref[...] = jnp.zeros_like(sums_ref)

        x = x_ref[...]
        dh_ = dh_ref[...]
        rstd = _rms_stats(x)
        xhat = x * rstd
        gg = g_ref[...]
        sc = jnp.where(is_ctx, mod_ref[1:2, :], mod_ref[3:4, :])
        dxhat = dh_ * (1.0 + sc) * gg
        dx_ref[...] = dx1_ref[...] + _rms_bwd(dxhat, xhat, rstd)
        part = [_colsum(dh_), _colsum(dh_ * xhat * gg), _colsum(dh_ * (1.0 + sc) * xhat)]

        @pl.when(is_ctx)
        def _():
            for k, row in enumerate(part):
                sums_ref[3 + k:4 + k, :] += row

        @pl.when(jnp.logical_not(is_ctx))
        def _():
            for k, row in enumerate(part):
                sums_ref[k:k + 1, :] += row

    lat = lambda i: (jnp.maximum(i - nctx, 0), 0)
    return _call(
        body, side=side, name="norm_mod_bwd1", grid=(t // tr,),
        in_specs=[pl.BlockSpec((tr, d), lambda i: (i, 0)), pl.BlockSpec((tr, d), lambda i: (i, 0)),
                  pl.BlockSpec((tr, d), lat), pl.BlockSpec((1, d), lambda i: (0, 0)),
                  pl.BlockSpec((4, d), lambda i: (0, 0))],
        out_specs=[pl.BlockSpec((tr, d), lat), pl.BlockSpec((8, d), lambda i: (0, 0))],
        out_shape=[_sds((s, d)), _sds((8, d))], sem=("arbitrary",),
    )(xcat, dh, dx1, g, mod4)


def _resid_norm_mod_fwd(x, mix, g, mod3, tr):
    s, d = x.shape

    def body(x_ref, mix_ref, g_ref, mod_ref, x1_ref, h_ref):
        x1 = x_ref[...] + mod_ref[0:1, :] * mix_ref[...]
        x1_ref[...] = x1
        n = x1 * _rms_stats(x1) * g_ref[...]
        h_ref[...] = (n * (1.0 + mod_ref[2:3, :]) + mod_ref[1:2, :]).astype(BF16)

    row = pl.BlockSpec((tr, d), lambda i: (i, 0))
    return pl.pallas_call(
        body, name="resid_norm_mod_fwd", grid=(s // tr,),
        in_specs=[row, row, pl.BlockSpec((1, d), lambda i: (0, 0)), pl.BlockSpec((3, d), lambda i: (0, 0))],
        out_specs=[row, row], out_shape=[_sds((s, d)), _sds((s, d), BF16)], compiler_params=_params("parallel"),
    )(x, mix, g, mod3)


def _norm_mod_bwd2(x1, dh2, dx2, mix, g, mod3, tr, side=None):
    s, d = x1.shape

    def body(x_ref, dh_ref, dx2_ref, mix_ref, g_ref, mod_ref, dx1_ref, dmix_ref, sums_ref):
        @pl.when(pl.program_id(0) == 0)
        def _():
            sums_ref[...] = jnp.zeros_like(sums_ref)

        x = x_ref[...]
        dh_ = dh_ref[...]
        rstd = _rms_stats(x)
        xhat = x * rstd
        gg = g_ref[...]
        sc = mod_ref[2:3, :]
        dx1 = dx2_ref[...] + _rms_bwd(dh_ * (1.0 + sc) * gg, xhat, rstd)
        dx1_ref[...] = dx1
        dmix_ref[...] = (dx1 * mod_ref[0:1, :]).astype(BF16)
        part = [_colsum(dh_), _colsum(dh_ * xhat * gg), _colsum(dh_ * (1.0 + sc) * xhat), _colsum(dx1 * mix_ref[...])]
        for k, row in enumerate(part):
            sums_ref[k:k + 1, :] += row

    row = pl.BlockSpec((tr, d), lambda i: (i, 0))
    return _call(
        body, side=side, name="norm_mod_bwd2", grid=(s // tr,),
        in_specs=[row, row, row, row, pl.BlockSpec((1, d), lambda i: (0, 0)), pl.BlockSpec((3, d), lambda i: (0, 0))],
        out_specs=[row, row, pl.BlockSpec((8, d), lambda i: (0, 0))],
        out_shape=[_sds((s, d)), _sds((s, d), BF16), _sds((8, d))], sem=("arbitrary",),
    )(x1, dh2, dx2, mix, g, mod3)


def _final_fwd_bwd(x1, down, target, g_final, gate, tr):
    s, d = x1.shape

    def body(x1_ref, down_ref, t_ref, g_ref, gate_ref, dx2_ref, ddown_ref, sums_ref, loss_ref):
        @pl.when(pl.program_id(0) == 0)
        def _():
            sums_ref[...] = jnp.zeros_like(sums_ref)
            loss_ref[...] = jnp.zeros_like(loss_ref)

        down_ = down_ref[...]
        gate_ = gate_ref[...]
        x2 = x1_ref[...] + gate_ * down_
        rstd = _rms_stats(x2)
        xhat = x2 * rstd
        gg = g_ref[...]
        err = xhat * gg - t_ref[...]
        loss_ref[...] += 0.5 * jnp.sum(jnp.mean(err * err, axis=-1, keepdims=True))
        dy = err * (1.0 / d)
        dx2 = _rms_bwd(dy * gg, xhat, rstd)
        dx2_ref[...] = dx2
        ddown_ref[...] = (dx2 * gate_).astype(BF16)
        sums_ref[0:1, :] += _colsum(dy * xhat)
        sums_ref[1:2, :] += _colsum(dx2 * down_)

    row = pl.BlockSpec((tr, d), lambda i: (i, 0))
    vec = pl.BlockSpec((1, d), lambda i: (0, 0))
    return pl.pallas_call(
        body, name="final_fwd_bwd", grid=(s // tr,), in_specs=[row, row, row, vec, vec],
        out_specs=[row, row, pl.BlockSpec((8, d), lambda i: (0, 0)), pl.BlockSpec((8, LANES), lambda i: (0, 0))],
        out_shape=[_sds((s, d)), _sds((s, d), BF16), _sds((8, d)), _sds((8, LANES))],
        compiler_params=_params("arbitrary"),
    )(x1, down, target, g_final, gate)


def _swap_pairs(v):
    lane = lax.broadcasted_iota(jnp.int32, v.shape, 1)
    return jnp.where(lane % 2 == 0, pltpu.roll(v, LANES - 1, 1), pltpu.roll(v, 1, 1))


def _head_prep_fwd(z, col_off, n_heads, gain, cos, sin, tr, name, side=None):
    t = z.shape[0]
    per = math.gcd(4, n_heads, col_off // LANES)
    w = per * LANES
    hb = col_off // w

    def body(z_ref, g_ref, cos_ref, sin_ref, o_ref):
        for hh in range(per):
            cols = slice(hh * LANES, (hh + 1) * LANES)
            x = z_ref[:, cols]
            y = x * _rms_stats(x) * g_ref[...]
            o_ref[:, cols] = (y * cos_ref[...] + _swap_pairs(y) * sin_ref[...]).astype(BF16)

    tab = pl.BlockSpec((tr, LANES), lambda i, j: (i, 0))
    return _call(
        body, side=side, name=name, grid=(t // tr, n_heads // per),
        in_specs=[pl.BlockSpec((tr, w), lambda i, j: (i, hb + j)), pl.BlockSpec((1, LANES), lambda i, j: (0, 0)),
                  tab, tab],
        out_specs=pl.BlockSpec((tr, w), lambda i, j: (i, j)), out_shape=_sds((t, n_heads * LANES), BF16),
        sem=("parallel", "parallel"),
    )(z, gain, cos, sin)


def _head_prep_bwd(z, col_off, n_heads, gain, cos, sin, dout, row_off, tr, name, side=None):
    r = dout.shape[0]
    per = math.gcd(4, n_heads, col_off // LANES)
    w = per * LANES
    hb = col_off // w
    rb = row_off // tr

    def body(z_ref, g_ref, cos_ref, sin_ref, d_ref, dz_ref, dg_ref):
        @pl.when(jnp.logical_and(pl.program_id(0) == 0, pl.program_id(1) == 0))
        def _():
            dg_ref[...] = jnp.zeros_like(dg_ref)

        for hh in range(per):
            cols = slice(hh * LANES, (hh + 1) * LANES)
            x = z_ref[:, cols]
            rstd = _rms_stats(x)
            xhat = x * rstd
            dd = d_ref[:, cols]
            dy = dd * cos_ref[...] - _swap_pairs(dd) * sin_ref[...]
            dg_ref[0:1, :] += _colsum(dy * xhat)
            dz_ref[:, cols] = _rms_bwd(dy * g_ref[...], xhat, rstd).astype(BF16)

    tab = pl.BlockSpec((tr, LANES), lambda i, j: (rb + i, 0))
    return _call(
        body, side=side, name=name, grid=(r // tr, n_heads // per),
        in_specs=[pl.BlockSpec((tr, w), lambda i, j: (rb + i, hb + j)),
                  pl.BlockSpec((1, LANES), lambda i, j: (0, 0)), tab, tab,
                  pl.BlockSpec((tr, w), lambda i, j: (i, j))],
        out_specs=[pl.BlockSpec((tr, w), lambda i, j: (i, j)), pl.BlockSpec((8, LANES), lambda i, j: (0, 0))],
        out_shape=[_sds((r, n_heads * LANES), BF16), _sds((8, LANES))], sem=("arbitrary", "arbitrary"),
    )(z, gain, cos, sin, dout)


def _softmax_rows(q, k, scale):
    s = lax.dot_general(q, k, (((1,), (1,)), ((), ())), preferred_element_type=F32) * scale
    p = jnp.exp(s - jnp.max(s, axis=-1, keepdims=True))
    return p / jnp.sum(p, axis=-1, keepdims=True)


def _attn_fwd(qr, kr, z, v_off, n_ctx_rows, group, tq, side=None):
    t, kvw = kr.shape
    s = t - n_ctx_rows
    n_kv = kvw // LANES
    scale = LANES ** -0.5
    qb0 = n_ctx_rows // tq
    vb = v_off // LANES

    def body(q_ref, k_ref, v_ref, o_ref):
        k = k_ref[...]
        v = v_ref[...].astype(BF16)
        for g in range(group):
            cols = slice(g * LANES, (g + 1) * LANES)
            p = _softmax_rows(q_ref[:, cols], k, scale)
            o_ref[:, cols] = jnp.dot(p.astype(BF16), v, preferred_element_type=F32).astype(BF16)

    return _call(
        body, side=side, name="attn_fwd", grid=(n_kv, s // tq),
        in_specs=[pl.BlockSpec((tq, group * LANES), lambda h, i: (qb0 + i, h)),
                  pl.BlockSpec((t, LANES), lambda h, i: (0, h)), pl.BlockSpec((t, LANES), lambda h, i: (0, vb + h))],
        out_specs=pl.BlockSpec((tq, group * LANES), lambda h, i: (i, h)),
        out_shape=_sds((s, n_kv * group * LANES), BF16), sem=("parallel", "parallel"),
    )(qr, kr, z)


def _attn_bwd(qr, kr, z, v_off, d_o, n_ctx_rows, group, tq, side=None):
    t, kvw = kr.shape
    s = t - n_ctx_rows
    n_kv = kvw // LANES
    scale = LANES ** -0.5
    qb0 = n_ctx_rows // tq
    vb = v_off // LANES
    tn_dims = (((0,), (0,)), ((), ()))
    nt_dims = (((1,), (1,)), ((), ()))

    def body(q_ref, k_ref, v_ref, do_ref, dq_ref, dk_ref, dv_ref):
        @pl.when(pl.program_id(1) == 0)
        def _():
            dk_ref[...] = jnp.zeros_like(dk_ref)
            dv_ref[...] = jnp.zeros_like(dv_ref)

        k = k_ref[...]
        v = v_ref[...].astype(BF16)
        for g in range(group):
            cols = slice(g * LANES, (g + 1) * LANES)
            q = q_ref[:, cols]
            do_ = do_ref[:, cols]
            p = _softmax_rows(q, k, scale)
            dv_ref[...] += lax.dot_general(p.astype(BF16), do_, tn_dims, preferred_element_type=F32)
            dp = lax.dot_general(do_, v, nt_dims, preferred_element_type=F32)
            ds = (p * (dp - jnp.sum(p * dp, axis=-1, keepdims=True)) * scale).astype(BF16)
            dq_ref[:, cols] = jnp.dot(ds, k, preferred_element_type=F32)
            dk_ref[...] += lax.dot_general(ds, q, tn_dims, preferred_element_type=F32)

    qspec = pl.BlockSpec((tq, group * LANES), lambda h, i: (qb0 + i, h))
    ospec = pl.BlockSpec((tq, group * LANES), lambda h, i: (i, h))
    kspec = pl.BlockSpec((t, LANES), lambda h, i: (0, h))
    return _call(
        body, side=side, name="attn_bwd", grid=(n_kv, s // tq),
        in_specs=[qspec, kspec, pl.BlockSpec((t, LANES), lambda h, i: (0, vb + h)), ospec],
        out_specs=[ospec, kspec, kspec],
        out_shape=[_sds((s, n_kv * group * LANES)), _sds((t, kvw)), _sds((t, kvw))], sem=("parallel", "arbitrary"),
    )(qr, kr, z, d_o)


def _row_mask(shape, rows):
    r = lax.broadcasted_iota(jnp.int32, shape, 0)
    m = r == rows[0]
    for v in rows[1:]:
        m = jnp.logical_or(m, r == v)
    return m


def _shift_rows(x, k, n_ctx_rows):
    t = x.shape[0]
    if k == 0:
        return x
    rolled = pltpu.roll(x, (-k) % t, 0)
    if k > 0:
        dead = [n_ctx_rows - 1 - i for i in range(k)] + [t - 1 - i for i in range(k)]
    else:
        dead = [i for i in range(-k)] + [n_ctx_rows + i for i in range(-k)]
    return jnp.where(_row_mask(x.shape, dead), 0.0, rolled)


def _conv(x, w, b, n_ctx_rows):
    y = b
    for k in range(CONV_WIDTH):
        y = y + _shift_rows(x, k - 1, n_ctx_rows) * w[k:k + 1, :]
    return y


def _gates(xc_bf, w_r, b_r, w_i, b_i, lam):
    r = jax.nn.sigmoid(jnp.dot(xc_bf, w_r.astype(BF16), preferred_element_type=F32) + b_r)
    i = jax.nn.sigmoid(jnp.dot(xc_bf, w_i.astype(BF16), preferred_element_type=F32) + b_i)
    log_a = -LRU_C * r * _softplus(-lam)
    a = jnp.exp(log_a)
    mult = jnp.sqrt(-_expm1_nonpos(2.0 * log_a))
    return r, i, a, mult


def _rnn_specs(t, xr_off):
    xb = xr_off // LANES
    return dict(
        zcol=pl.BlockSpec((t, LANES), lambda j: (0, xb + j)), col=pl.BlockSpec((t, LANES), lambda j: (0, j)),
        conv_w=pl.BlockSpec((CONV_WIDTH, LANES), lambda j: (0, j)), vec=pl.BlockSpec((1, LANES), lambda j: (0, j)),
        gate_w=pl.BlockSpec((2, 1, LANES, LANES), lambda j: (0, j, 0, 0)), two=pl.BlockSpec((2, LANES), lambda j: (0, j)))


def _rnn_prep(z, xr_off, conv_w, conv_b, w_rg, b_rg, w_ig, b_ig, lam, n_ctx_rows, side=None):
    t = z.shape[0]
    d = conv_b.shape[1]
    sp = _rnn_specs(t, xr_off)

    def body(z_ref, cw_ref, cb_ref, wr_ref, br_ref, wi_ref, bi_ref, lam_ref, xc_ref, af_ref, bf_ref, ab_ref, bb_ref):
        xc = _conv(z_ref[...], cw_ref[...], cb_ref[...], n_ctx_rows)
        xc_ref[...] = xc
        xc_bf = xc.astype(BF16)
        for dr, (a_ref, b_ref) in enumerate(((af_ref, bf_ref), (ab_ref, bb_ref))):
            _, i, a, mult = _gates(xc_bf, wr_ref[dr, 0], br_ref[dr:dr + 1, :], wi_ref[dr, 0], bi_ref[dr:dr + 1, :],
                                   lam_ref[dr:dr + 1, :])
            a_ref[...] = a
            b_ref[...] = mult * (i * xc)

    return _call(
        body, side=side, name="rnn_prep", grid=(d // LANES,),
        in_specs=[sp["zcol"], sp["conv_w"], sp["vec"], sp["gate_w"], sp["two"], sp["gate_w"], sp["two"], sp["two"]],
        out_specs=[sp["col"]] * 5, out_shape=[_sds((t, d))] * 5, sem=("parallel",),
    )(z, conv_w, conv_b, w_rg, b_rg, w_ig, b_ig, lam)


def _scan(a, b, *, order, post, n_ctx_rows, name, tc=256, side=None):
    t, d = a.shape
    tc = _tile(math.gcd(n_ctx_rows, t - n_ctx_rows), tc, SUBLANES)
    nt, nctx = t // tc, n_ctx_rows // tc
    nlat = nt - nctx
    b_lat_only = b.shape[0] != t
    up = order.endswith("up")

    def chunk(i):
        if order == "ctx_lat_up":
            return i
        if order == "lat_ctx_down":
            return nt - 1 - i
        if order == "ctx_lat_down":
            return jnp.where(i < nctx, nctx - 1 - i, nt - 1 - (i - nctx))
        return jnp.where(i < nlat, nctx + i, i - nlat)

    def body(a_ref, b_ref, o_ref, carry_ref):
        @pl.when(pl.program_id(0) == 0)
        def _():
            carry_ref[...] = jnp.zeros_like(carry_ref)

        live = jnp.where(chunk(pl.program_id(0)) >= nctx, 1.0, 0.0) if b_lat_only else None

        def group(gi, carry):
            base = pl.multiple_of((gi if up else tc // SUBLANES - 1 - gi) * SUBLANES, SUBLANES)
            for r in (range(SUBLANES) if up else range(SUBLANES - 1, -1, -1)):
                a_r = a_ref[pl.ds(base + r, 1), :]
                b_r = b_ref[pl.ds(base + r, 1), :]
                if live is not None:
                    b_r = b_r * live
                if post:
                    out = b_r + carry
                    carry = a_r * out
                else:
                    out = a_r * carry + b_r
                    carry = out
                o_ref[pl.ds(base + r, 1), :] = out
            return carry

        carry_ref[0:1, :] = lax.fori_loop(0, tc // SUBLANES, group, carry_ref[0:1, :])

    full = pl.BlockSpec((tc, d), lambda i: (chunk(i), 0))
    b_spec = pl.BlockSpec((tc, d), lambda i: (jnp.maximum(chunk(i) - nctx, 0), 0)) if b_lat_only else full
    return _call(
        body, side=side, name=name, grid=(nt,), in_specs=[full, b_spec], out_specs=full, out_shape=_sds((t, d)),
        scratch_shapes=[pltpu.VMEM((SUBLANES, d), F32)], sem=("arbitrary",),
    )(a, b)


def _rnn_bwd(z, xr_off, xc, g_f, g_b, h_f, h_b, conv_w, w_rg, b_rg, w_ig, b_ig, lam, n_ctx_rows, side=None):
    t, d = xc.shape
    sp = _rnn_specs(t, xr_off)
    tn_dims = (((0,), (0,)), ((), ()))
    nt_dims = (((1,), (1,)), ((), ()))

    def body(z_ref, xc_ref, gf_ref, gb_ref, hf_ref, hb_ref, cw_ref, wr_ref, br_ref, wi_ref, bi_ref, lam_ref,
             dxr_ref, dwr_ref, dwi_ref, sums_ref):
        xc_ = xc_ref[...]
        xc_bf = xc_.astype(BF16)
        dxc = jnp.zeros_like(xc_)
        sums = [None] * 6
        for dr, (g_ref, h_ref) in enumerate(((gf_ref, hf_ref), (gb_ref, hb_ref))):
            w_r, w_i, lam_ = wr_ref[dr, 0], wi_ref[dr, 0], lam_ref[dr:dr + 1, :]
            r, i, a, mult = _gates(xc_bf, w_r, br_ref[dr:dr + 1, :], w_i, bi_ref[dr:dr + 1, :], lam_)
            g = g_ref[...]
            h = h_ref[...]
            if dr == 0:
                h_prev = jnp.where(_row_mask(h.shape, [0]), 0.0, pltpu.roll(h, 1, 0))
            else:
                h_prev = jnp.where(_row_mask(h.shape, [n_ctx_rows - 1]), 0.0, pltpu.roll(h, t - 1, 0))
            d_mult = g * i * xc_
            d_i = g * mult * xc_
            dxc = dxc + g * mult * i
            d_log_a = g * h_prev * a - d_mult * a * a / mult
            sp_ = _softplus(-lam_)
            d_r = d_log_a * (-LRU_C) * sp_
            d_sp = _colsum(d_log_a * (-LRU_C) * r)
            du_r = (d_r * r * (1.0 - r))
            du_i = (d_i * i * (1.0 - i))
            sums[dr] = _colsum(du_r)
            sums[2 + dr] = _colsum(du_i)
            sums[4 + dr] = d_sp * (-jax.nn.sigmoid(-lam_))
            du_r_bf, du_i_bf = du_r.astype(BF16), du_i.astype(BF16)
            dwr_ref[dr, 0] = lax.dot_general(xc_bf, du_r_bf, tn_dims, preferred_element_type=F32).astype(BF16)
            dwi_ref[dr, 0] = lax.dot_general(xc_bf, du_i_bf, tn_dims, preferred_element_type=F32).astype(BF16)
            dxc = dxc + lax.dot_general(du_r_bf, w_r.astype(BF16), nt_dims, preferred_element_type=F32)
            dxc = dxc + lax.dot_general(du_i_bf, w_i.astype(BF16), nt_dims, preferred_element_type=F32)
        xr = z_ref[...]
        cw = cw_ref[...]
        dxr = jnp.zeros_like(dxc)
        rows = list(sums)
        for k in range(CONV_WIDTH):
            dxr = dxr + _shift_rows(dxc, 1 - k, n_ctx_rows) * cw[k:k + 1, :]
            rows.append(_colsum(dxc * _shift_rows(xr, k - 1, n_ctx_rows)))
        rows.append(_colsum(dxc))
        dxr_ref[...] = dxr.astype(BF16)
        sums_ref[...] = jnp.zeros_like(sums_ref)
        for k, row in enumerate(rows):
            sums_ref[k:k + 1, :] = row

    return _call(
        body, side=side, name="rnn_bwd", grid=(d // LANES,),
        in_specs=[sp["zcol"]] + [sp["col"]] * 5 + [sp["conv_w"], sp["gate_w"], sp["two"], sp["gate_w"], sp["two"],
                                                  sp["two"]],
        out_specs=[sp["col"], sp["gate_w"], sp["gate_w"], pl.BlockSpec((16, LANES), lambda j: (0, j))],
        out_shape=[_sds((t, d), BF16), _sds(w_rg.shape, BF16), _sds(w_ig.shape, BF16), _sds((16, d))],
        sem=("parallel",),
    )(z, xc, g_f, g_b, h_f, h_b, conv_w, w_rg, b_rg, w_ig, b_ig, lam)


def _tiles2d(s, d, tr, tcol):
    return (s // tr, d // tcol), pl.BlockSpec((tr, tcol), lambda i, j: (i, j))


def _zspec(tr, tcol, row_off, col_off):
    rb, cb = row_off // tr, col_off // tcol
    return pl.BlockSpec((tr, tcol), lambda i, j: (rb + i, cb + j))


def _rnn_gate_fwd(h_f, h_b, z, xg_off, n_ctx_rows, tr, tcol, side=None):
    t, d = h_f.shape
    s = t - n_ctx_rows
    grid, out = _tiles2d(s, d, tr, tcol)
    hs = _zspec(tr, tcol, n_ctx_rows, 0)

    def body(hf_ref, hb_ref, xg_ref, u_ref):
        u_ref[...] = ((hf_ref[...] + hb_ref[...]) * _gelu(xg_ref[...])).astype(BF16)

    return _call(body, side=side, name="rnn_gate_fwd", grid=grid,
                 in_specs=[hs, hs, _zspec(tr, tcol, n_ctx_rows, xg_off)], out_specs=out, out_shape=_sds((s, d), BF16),
                 sem=("parallel", "parallel"))(h_f, h_b, z)


def _rnn_gate_bwd(d_u, h_f, h_b, z, xg_off, n_ctx_rows, tr, tcol, side=None):
    t, d = h_f.shape
    s = t - n_ctx_rows
    grid, out = _tiles2d(s, d, tr, tcol)
    hs = _zspec(tr, tcol, n_ctx_rows, 0)

    def body(du_ref, hf_ref, hb_ref, xg_ref, dr_ref, dxg_ref):
        du = du_ref[...]
        xg = xg_ref[...]
        dr_ref[...] = du * _gelu(xg)
        dxg_ref[...] = (du * (hf_ref[...] + hb_ref[...]) * _gelu_grad(xg)).astype(BF16)

    return _call(body, side=side, name="rnn_gate_bwd", grid=grid,
                 in_specs=[out, hs, hs, _zspec(tr, tcol, n_ctx_rows, xg_off)], out_specs=[out, out],
                 out_shape=[_sds((s, d)), _sds((s, d), BF16)], sem=("parallel", "parallel"))(d_u, h_f, h_b, z)


def _merge_fwd(y_attn, y_rnn, z, gl_off, n_ctx_rows, tr, tcol):
    s, d = y_attn.shape
    grid, out = _tiles2d(s, d, tr, tcol)

    def body(ya_ref, yr_ref, ga_ref, gr_ref, o_ref):
        o_ref[...] = (jax.nn.sigmoid(ga_ref[...]) * ya_ref[...] + jax.nn.sigmoid(gr_ref[...]) * yr_ref[...]).astype(BF16)

    return pl.pallas_call(
        body, name="merge_fwd", grid=grid,
        in_specs=[out, out, _zspec(tr, tcol, n_ctx_rows, gl_off), _zspec(tr, tcol, n_ctx_rows, gl_off + d)],
        out_specs=out, out_shape=_sds((s, d), BF16), compiler_params=_params("parallel", "parallel"),
    )(y_attn, y_rnn, z, z)


def _merge_bwd(d_mrg, y_attn, y_rnn, z, gl_off, n_ctx_rows, tr, tcol, side=None):
    s, d = y_attn.shape
    grid, out = _tiles2d(s, d, tr, tcol)

    def body(dm_ref, ya_ref, yr_ref, ga_ref, gr_ref, dya_ref, dyr_ref, dga_ref, dgr_ref):
        dm = dm_ref[...]
        ga = jax.nn.sigmoid(ga_ref[...])
        gr = jax.nn.sigmoid(gr_ref[...])
        dya_ref[...] = (dm * ga).astype(BF16)
        dyr_ref[...] = (dm * gr).astype(BF16)
        dga_ref[...] = (dm * ya_ref[...] * ga * (1.0 - ga)).astype(BF16)
        dgr_ref[...] = (dm * yr_ref[...] * gr * (1.0 - gr)).astype(BF16)

    return _call(
        body, side=side, name="merge_bwd", grid=grid,
        in_specs=[out, out, out, _zspec(tr, tcol, n_ctx_rows, gl_off), _zspec(tr, tcol, n_ctx_rows, gl_off + d)],
        out_specs=[out] * 4, out_shape=[_sds((s, d), BF16)] * 4, sem=("parallel", "parallel"),
    )(d_mrg, y_attn, y_rnn, z, z)


def _sq_relu(up, tr, tcol, side=None):
    grid, out = _tiles2d(*up.shape, _tile(up.shape[0], 2 * tr, 16), _tile(up.shape[1], 4 * tcol, LANES))

    def body(u_ref, o_ref):
        r = jnp.maximum(u_ref[...], 0.0)
        o_ref[...] = (r * r).astype(BF16)

    return _call(body, side=side, name="sq_relu", grid=grid, in_specs=[out], out_specs=out,
                 out_shape=_sds(up.shape, BF16), sem=("parallel", "parallel"))(up)


def _sq_relu_bwd(d_act, up, tr, tcol, side=None):
    grid, out = _tiles2d(*up.shape, _tile(up.shape[0], 2 * tr, 16), _tile(up.shape[1], 4 * tcol, LANES))

    def body(d_ref, u_ref, o_ref):
        o_ref[...] = (d_ref[...] * 2.0 * jnp.maximum(u_ref[...], 0.0)).astype(BF16)

    return _call(body, side=side, name="sq_relu_bwd", grid=grid, in_specs=[out, out], out_specs=out,
                 out_shape=_sds(up.shape, BF16), sem=("parallel", "parallel"))(d_act, up)


def _cast_into_window(w, chip, col_sharded, name):
    r, c = w.shape
    tr, tcol = _tile(r, 512, 16), _tile(c, 1024, LANES)
    nrb, ncb = r // tr, c // tcol

    def body(chip_ref, w_ref, o_ref):
        o_ref[...] = w_ref[...].astype(BF16)

    if col_sharded:
        omap = lambda i, j, chip_ref: (i, chip_ref[0] * ncb + j)
    else:
        omap = lambda i, j, chip_ref: (chip_ref[0] * nrb + i, j)
    return pl.pallas_call(
        body, name=name,
        grid_spec=pltpu.PrefetchScalarGridSpec(
            num_scalar_prefetch=1, grid=(nrb, ncb),
            in_specs=[pl.BlockSpec((tr, tcol), lambda i, j, chip_ref: (i, j))], out_specs=pl.BlockSpec((tr, tcol), omap)),
        out_shape=_sds((r, c * N_CHIPS) if col_sharded else (r * N_CHIPS, c), BF16),
        compiler_params=_params("parallel", "parallel"),
    )(chip, w)


def _sum_leading(parts, name):
    n, r, c = parts.shape
    tr, tcol = _tile(r, 512, SUBLANES), _tile(c, 1024, LANES)

    def body(p_ref, o_ref):
        tot = p_ref[0]
        for k in range(1, n):
            tot = tot + p_ref[k]
        o_ref[...] = tot

    return pl.pallas_call(
        body, name=name, grid=(r // tr, c // tcol), in_specs=[pl.BlockSpec((n, tr, tcol), lambda i, j: (0, i, j))],
        out_specs=pl.BlockSpec((tr, tcol), lambda i, j: (i, j)), out_shape=_sds((r, c)),
        compiler_params=_params("parallel", "parallel"),
    )(parts)


def _add_half(full, other, core, split_rows, name):
    r, c = other.shape
    tr, tcol = _tile(r, 512, 16), _tile(c, 1024, LANES)
    nrb, ncb = r // tr, c // tcol

    def body(core_ref, f_ref, o_ref, out_ref):
        out_ref[...] = (f_ref[...].astype(F32) + o_ref[...].astype(F32)).astype(out_ref.dtype)

    if split_rows:
        fmap = lambda i, j, core_ref: (core_ref[0] * nrb + i, j)
    else:
        fmap = lambda i, j, core_ref: (i, core_ref[0] * ncb + j)
    same = lambda i, j, core_ref: (i, j)
    return pl.pallas_call(
        body, name=name,
        grid_spec=pltpu.PrefetchScalarGridSpec(
            num_scalar_prefetch=1, grid=(nrb, ncb),
            in_specs=[pl.BlockSpec((tr, tcol), fmap), pl.BlockSpec((tr, tcol), same)],
            out_specs=pl.BlockSpec((tr, tcol), same)),
        out_shape=_sds((r, c), BF16), compiler_params=_params("parallel", "parallel"),
    )(core, full, other)


def _sum_regions(pair, got, place, col_sharded, name):
    _, r, c = got.shape
    tr, tcol = _tile(r, 512, 16), _tile(c, 1024, LANES)
    nrb, ncb = r // tr, c // tcol

    def body(place_ref, p_ref, g_ref, out_ref):
        tot = p_ref[...].astype(F32)
        for k in range(N_CHIPS - 1):
            tot = tot + g_ref[k].astype(F32)
        out_ref[...] = tot

    if col_sharded:
        pmap = lambda i, j, pr: (i, pr[0] * ncb + j)
        omap = lambda i, j, pr: (pr[1] * nrb + i, j)
        out_shape = (2 * r, c)
    else:
        pmap = lambda i, j, pr: (pr[0] * nrb + i, j)
        omap = lambda i, j, pr: (i, pr[1] * ncb + j)
        out_shape = (r, 2 * c)
    return pl.pallas_call(
        body, name=name,
        grid_spec=pltpu.PrefetchScalarGridSpec(
            num_scalar_prefetch=1, grid=(nrb, ncb),
            in_specs=[pl.BlockSpec((tr, tcol), pmap), pl.BlockSpec((N_CHIPS - 1, tr, tcol), lambda i, j, pr: (0, i, j))],
            out_specs=pl.BlockSpec((tr, tcol), omap)),
        out_shape=_sds(out_shape), compiler_params=_params("parallel", "parallel"),
    )(place, pair, got)


def _adamw(w, g, m, v, name, side=None):
    r, c = w.shape
    tr, tcol = _tile(r, 512, SUBLANES), _tile(c, 1024, LANES)
    blk = pl.BlockSpec((tr, tcol), lambda i, j: (i, j))

    def body(w_ref, g_ref, m_ref, v_ref, d_ref, nm_ref, nv_ref):
        g_ = g_ref[...]
        m_ = ADAM_B1 * m_ref[...] + (1.0 - ADAM_B1) * g_
        v_ = ADAM_B2 * v_ref[...] + (1.0 - ADAM_B2) * (g_ * g_)
        m_hat = m_ / (1.0 - ADAM_B1 ** ADAM_STEP)
        v_hat = v_ / (1.0 - ADAM_B2 ** ADAM_STEP)
        d_ref[...] = -ADAM_LR * (m_hat / (jnp.sqrt(v_hat) + ADAM_EPS) + ADAM_WD * w_ref[...])
        nm_ref[...] = m_
        nv_ref[...] = v_

    return _call(body, side=side, name=name, grid=(r // tr, c // tcol), in_specs=[blk] * 4, out_specs=[blk] * 3,
                 out_shape=[_sds((r, c))] * 3, sem=("parallel", "parallel"))(w, g, m, v)


def _place():
    x, y, c = lax.axis_index("x"), lax.axis_index("y"), lax.axis_index("c")
    chips = [(1 - x, y), (x, 1 - y), (1 - x, 1 - y)]
    return x, y, c, chips


def _all_gather8(blk, name):
    m, n = blk.shape

    def body(x_ref, out_ref, send_sems, recv_sems, local_sem):
        x, y, c, chips = _place()
        me, sibling = (x, y, c), (x, y, 1 - c)

        def rows(px, py, pc):
            return out_ref.at[pl.ds((4 * px + 2 * py + pc) * m, m), :]

        def copy(k, block, to, src=None):
            return pltpu.make_async_remote_copy(
                src_ref=rows(*block) if src is None else src, dst_ref=rows(*block), send_sem=send_sems.at[k],
                recv_sem=recv_sems.at[k], device_id=to, device_id_type=MESH_ID)

        mine = pltpu.make_async_copy(x_ref, rows(*me), local_sem)
        mine.start()
        first = [copy(0, me, sibling, src=x_ref)]
        first += [copy(1 + j, me, (*chip, c), src=x_ref) for j, chip in enumerate(chips)]
        for cp in first:
            cp.start()
        passed = [copy(4 + j, (*chip, c), sibling) for j, chip in enumerate(chips)]
        for j, chip in enumerate(chips):
            copy(1 + j, (*chip, c), me).wait_recv()
            passed[j].start()
        copy(0, sibling, me).wait_recv()
        for j, chip in enumerate(chips):
            copy(4 + j, (*chip, 1 - c), me).wait_recv()
        for cp in first + passed:
            cp.wait_send()
        mine.wait()

    return pl.pallas_call(
        body, name=name, out_shape=_sds((N_DEV * m, n), blk.dtype), in_specs=[ANY], out_specs=ANY,
        scratch_shapes=[pltpu.SemaphoreType.DMA((7,)), pltpu.SemaphoreType.DMA((7,)), pltpu.SemaphoreType.DMA],
    )(blk)


def _half(ref, core, split_rows):
    r, c = ref.shape
    if split_rows:
        return ref.at[pl.ds(core * (r // 2), r // 2), :]
    return ref.at[:, pl.ds(core * (c // 2), c // 2)]


def _chip_block(ref, j, col_sharded):
    r, c = ref.shape
    if col_sharded:
        return ref.at[:, pl.ds(j * (c // N_CHIPS), c // N_CHIPS)]
    return ref.at[pl.ds(j * (r // N_CHIPS), r // N_CHIPS), :]


def _rows_part(ref, part):
    lo, hi, n = part
    r = ref.shape[0]
    return ref if (lo, hi) == (0, n) else ref.at[pl.ds(lo * (r // n), (hi - lo) * (r // n)), :]


def _copy(send_sems, recv_sems, k, src, dst, to):
    return pltpu.make_async_remote_copy(src_ref=src, dst_ref=dst, send_sem=send_sems.at[k], recv_sem=recv_sems.at[k],
                                        device_id=to, device_id_type=MESH_ID)


def _in_place(arrays):
    return tuple(arrays), tuple(_sds(a.shape, a.dtype) for a in arrays), {i: i for i in range(len(arrays))}


def _gather_ici(fulls, col_sharded, part=(0, 1, 1)):
    nw = len(fulls)

    def build(_, refs, send_sems, recv_sems, sem0):
        x, y, c, chips = _place()
        sends, recvs = [], []
        for w in range(nw):
            win = lambda j: _rows_part(_half(_chip_block(refs[w], j, col_sharded[w]), c, True), part)
            for k, (cx, cy) in enumerate(chips):
                sem = sem0 + 3 * w + k
                sends.append(_copy(send_sems, recv_sems, sem, win(2 * x + y), win(2 * x + y), (cx, cy, c)))
                recvs.append(_copy(send_sems, recv_sems, sem, win(2 * cx + cy), win(2 * cx + cy), (cx, cy, c)))
        return sends, recvs

    return _Side(*_in_place(fulls), 3 * nw, build)


def _gather_d2d(fulls, col_sharded):
    nw = len(fulls)

    def build(_, refs, send_sems, recv_sems, sem0):
        x, y, c, chips = _place()
        sends, recvs = [], []
        for w in range(nw):
            win = lambda j, core: _half(_chip_block(refs[w], j, col_sharded[w]), core, True)
            for k, (cx, cy) in enumerate(chips):
                sem = sem0 + 3 * w + k
                sends.append(_copy(send_sems, recv_sems, sem, win(2 * cx + cy, c), win(2 * cx + cy, c), (x, y, 1 - c)))
                recvs.append(_copy(send_sems, recv_sems, sem, win(2 * cx + cy, 1 - c), win(2 * cx + cy, 1 - c),
                                   (x, y, 1 - c)))
        return sends, recvs

    return _Side(*_in_place(fulls), 3 * nw, build)


def _exchange(side, name):
    return _call(None, side=side, name=name)()[1]


def _swap_halves(grads, col_sharded):
    nw = len(grads)
    out_shapes = [_sds((g.shape[0] // 2, g.shape[1]) if col else (g.shape[0], g.shape[1] // 2), g.dtype)
                  for g, col in zip(grads, col_sharded)]

    def build(g_refs, o_refs, send_sems, recv_sems, sem0):
        x, y, c, _ = _place()
        copies = [_copy(send_sems, recv_sems, sem0 + w, _half(g_refs[w], 1 - c, col_sharded[w]), o_refs[w],
                        (x, y, 1 - c)) for w in range(nw)]
        return copies, copies

    return _Side(tuple(grads), tuple(out_shapes), {}, nw, build)


def _scatter_regions(pairs, col_sharded, part=(0, 1, 1), into=None):
    nw = len(pairs)

    def region_shape(p, col):
        return (p.shape[0], p.shape[1] // N_CHIPS) if col else (p.shape[0] // N_CHIPS, p.shape[1])

    out_shapes = tuple(_sds((N_CHIPS - 1, *region_shape(p, col)), p.dtype) for p, col in zip(pairs, col_sharded))

    def build(refs, o_refs, send_sems, recv_sems, sem0):
        x, y, c, chips = _place()
        copies = []
        for w in range(nw):
            for k, (cx, cy) in enumerate(chips):
                copies.append(_copy(
                    send_sems, recv_sems, sem0 + 3 * w + k,
                    _rows_part(_chip_block(refs[w], 2 * cx + cy, col_sharded[w]), part),
                    _rows_part(o_refs[w].at[k], part), (cx, cy, c)))
        return copies, copies

    if into is None:
        return _Side(tuple(pairs), out_shapes, {}, 3 * nw, build)
    return _Side((*pairs, *into), out_shapes, {nw + w: w for w in range(nw)}, 3 * nw, build)


def _join_halves(halves, col_sharded):
    nw = len(halves)

    def build(_, refs, send_sems, recv_sems, sem0):
        x, y, c, _ = _place()
        sends, recvs = [], []
        for w in range(nw):
            mine, theirs = _half(refs[w], c, col_sharded[w]), _half(refs[w], 1 - c, col_sharded[w])
            sends.append(_copy(send_sems, recv_sems, sem0 + w, mine, mine, (x, y, 1 - c)))
            recvs.append(_copy(send_sems, recv_sems, sem0 + w, theirs, theirs, (x, y, 1 - c)))
        return sends, recvs

    return _Side(*_in_place(halves), nw, build)


def _part_rows(size):
    return -(-size // (SUBLANES * LANES)) * SUBLANES


def _pack(arrays, pad_rows_to=SUBLANES):
    flat = [jnp.pad(a.reshape(-1), (0, _part_rows(a.size) * LANES - a.size)).reshape(-1, LANES) for a in arrays]
    rows = sum(f.shape[0] for f in flat)
    pad = (-rows) % pad_rows_to
    if pad:
        flat.append(jnp.zeros((pad, LANES), F32))
    return jnp.concatenate(flat, axis=0)


def _unpack(packed, shapes):
    out, r = [], 0
    for shp in shapes:
        size = math.prod(shp)
        out.append(packed[r:r + _part_rows(size)].reshape(-1)[:size].reshape(shp))
        r += _part_rows(size)
    return out


def _rope_tables(n_ctx_rows, s):
    rows = s // GRID_W
    row_idx = jnp.repeat(jnp.arange(rows), GRID_W)
    col_idx = jnp.tile(jnp.arange(GRID_W), rows)
    n_freq = LANES // 4
    inv_freq = ROPE_THETA ** (-jnp.arange(n_freq, dtype=F32) / n_freq)
    ang = jnp.concatenate([row_idx.astype(F32)[:, None] * inv_freq, col_idx.astype(F32)[:, None] * inv_freq], axis=-1)
    cos = jnp.repeat(jnp.cos(ang), 2, axis=-1)
    sin = jnp.repeat(jnp.sin(ang), 2, axis=-1) * jnp.tile(jnp.array([-1.0, 1.0], F32), LANES // 2)
    cos = jnp.concatenate([jnp.ones((n_ctx_rows, LANES), F32), cos], axis=0)
    sin = jnp.concatenate([jnp.zeros((n_ctx_rows, LANES), F32), sin], axis=0)
    return cos, sin


WEIGHT_NAMES = ['c_ctx', 'w_mod', 'b_mod', 'g_mix', 'g_mlp', 'w_in', 'q_gain', 'k_gain', 'conv_w', 'conv_b', 'w_rg',
                'b_rg', 'w_ig', 'b_ig', 'lru_lambda', 'w_o_attn', 'w_o_rnn', 'w_out', 'w_up', 'w_down', 'g_final']
BIG = ['w_in', 'w_o_attn', 'w_o_rnn', 'w_out', 'w_up', 'w_down']
BIG_COL_SHARDED = [True, False, False, False, True, False]
GATES = ['w_rg', 'w_ig']
SMALL = ['c_ctx', 'b_mod', 'g_mix', 'g_mlp', 'q_gain', 'k_gain', 'conv_b', 'g_final',
         'conv_w', 'b_rg', 'b_ig', 'lru_lambda']


def kernel(x, c, ctx, c_ctx, w_mod, b_mod, g_mix, g_mlp, w_in, q_gain, k_gain, conv_w, conv_b, w_rg, b_rg, w_ig, b_ig, lru_lambda, w_o_attn, w_o_rnn, w_out, w_up, w_down, g_final, loss_target, m_c_ctx, m_w_mod, m_b_mod, m_g_mix, m_g_mlp, m_w_in, m_q_gain, m_k_gain, m_conv_w, m_conv_b, m_w_rg, m_b_rg, m_w_ig, m_b_ig, m_lru_lambda, m_w_o_attn, m_w_o_rnn, m_w_out, m_w_up, m_w_down, m_g_final, v_c_ctx, v_w_mod, v_b_mod, v_g_mix, v_g_mlp, v_w_in, v_q_gain, v_k_gain, v_conv_w, v_conv_b, v_w_rg, v_b_rg, v_w_ig, v_b_ig, v_lru_lambda, v_w_o_attn, v_w_o_rnn, v_w_out, v_w_up, v_w_down, v_g_final):
    given = dict(locals())
    weights = {n: given[n] for n in WEIGHT_NAMES}
    moms = {n: given["m_" + n] for n in WEIGHT_NAMES}
    vars_ = {n: given["v_" + n] for n in WEIGHT_NAMES}

    s, d = x.shape[1], x.shape[2]
    n_ctx = ctx.shape[1]
    t = n_ctx + s
    hd = q_gain.shape[1]
    assert hd == LANES and w_rg.shape[-1] == LANES
    attn_w = w_o_attn.shape[1] * N_CHIPS
    n_in = w_in.shape[2] * N_CHIPS
    kv_w = (n_in - attn_w - 4 * d) // 2
    group = attn_w // kv_w
    k_off, v_off, xr_off = attn_w, attn_w + kv_w, attn_w + 2 * kv_w
    xg_off, gl_off = xr_off + d, xr_off + 2 * d
    d_mod = N_MOD * d
    tr = _tile(math.gcd(n_ctx, s), 256, 16)
    tcol = _tile(math.gcd(d, xr_off), 512, LANES)
    xi, yi, ci = lax.axis_index("x"), lax.axis_index("y"), lax.axis_index("c")
    chip = 2 * xi + yi
    core = ci.astype(jnp.int32).reshape(1)

    sharded_small = [conv_w[0], b_rg[0], b_ig[0], lru_lambda[0]]
    pack0 = _pack([c[0]] + sharded_small)
    got0 = _all_gather8(pack0, "gather_small_inputs").reshape(N_DEV, -1, LANES)
    c_all = got0[:, :_part_rows(d)].reshape(N_DEV, -1)[:, :d]
    per_chip = [_unpack(got0[2 * j, _part_rows(d):], [a.shape for a in sharded_small]) for j in range(N_CHIPS)]
    conv_w_f, b_rg_f, b_ig_f, lam_f = (jnp.concatenate([per_chip[j][i] for j in range(N_CHIPS)], axis=-1)
                                       for i in range(4))
    c16 = jnp.concatenate([c_all, c_ctx[None, :], jnp.zeros((16 - N_DEV - 1, d), F32)], axis=0)
    b_mod_shard = lax.dynamic_slice(b_mod, (0, chip * (d_mod // N_CHIPS)), (1, d_mod // N_CHIPS))
    mod_part, silu16 = _mod_fwd(c16, w_mod[0], b_mod_shard)
    mod_all = _all_gather8(mod_part, "gather_mod").reshape(N_DEV, 16, d_mod // N_CHIPS)
    mod16 = jnp.concatenate([mod_all[2 * j] for j in range(N_CHIPS)], axis=-1)
    me = 4 * xi + 2 * yi + ci
    mod_lat = lax.dynamic_slice(mod16, (me, 0), (1, d_mod)).reshape(N_MOD, d)
    mod_ctx = mod16[N_DEV].reshape(N_MOD, d)
    mod4 = jnp.stack([mod_ctx[0], mod_ctx[1], mod_lat[0], mod_lat[1]])
    mod3 = jnp.stack([mod_lat[2], mod_lat[3], mod_lat[4]])
    gate_f = mod_lat[5][None, :]

    chip_arr = chip.astype(jnp.int32).reshape(1)
    own = {n: _cast_into_window(weights[n][0], chip_arr, col, "cast_" + n) for n, col in zip(BIG, BIG_COL_SHARDED)}
    place = jnp.stack([chip, ci]).astype(jnp.int32)
    row3 = [False] * 3

    def pair_sum(n, full, other, col):
        return _add_half(full, other, core, col, "pair_sum_" + n)

    def chip_sum(n, pair, got, col):
        return _sum_regions(pair, got, place, col, "chip_sum_" + n)

    (w_in_f,) = _exchange(_gather_ici([own['w_in']], [True]), "gather_w_in_ici")
    (w_in_f,) = _exchange(_gather_d2d([w_in_f], [True]), "gather_w_in_d2d")
    xcat = jnp.concatenate([ctx[0], x[0]], axis=0)
    cos, sin = _rope_tables(n_ctx, s)
    h = _norm_mod_fwd(xcat, g_mix, mod4, n_ctx, tr)
    ici_up = lambda w, lo, hi: _gather_ici([w], [True], (lo, hi, 8))
    ici_down = lambda w, lo, hi: _gather_ici([w], [False], (lo, hi, 8))
    z, (w_oa_f, w_or_f, w_out_f, w_up_f) = _matmul(h, w_in_f, name="mm_in", side=_sides(
        _gather_ici([own['w_o_attn'], own['w_o_rnn'], own['w_out']], row3), ici_up(own['w_up'], 0, 1)))
    qr, (w_up_f,) = _head_prep_fwd(z, 0, attn_w // LANES, q_gain, cos, sin, tr, "q_prep", side=ici_up(w_up_f, 1, 3))
    kr, (w_oa_f, w_or_f, w_out_f) = _head_prep_fwd(z, k_off, kv_w // LANES, k_gain, cos, sin, tr, "k_prep",
                                                   side=_gather_d2d([w_oa_f, w_or_f, w_out_f], row3))
    attn_o, (w_up_f, w_down_f) = _attn_fwd(qr, kr, z, v_off, n_ctx, group, tr,
                                           side=_sides(ici_up(w_up_f, 3, 8), ici_down(own['w_down'], 0, 2)))
    (xc, a_f, bx_f, a_b, bx_b), (w_down_f, w_up_f) = _rnn_prep(
        z, xr_off, conv_w_f, conv_b, w_rg[0], b_rg_f, w_ig[0], b_ig_f, lam_f, n_ctx,
        side=_sides(ici_down(w_down_f, 2, 5), _gather_d2d([w_up_f], [True])))
    h_f, (w_down_f,) = _scan(a_f, bx_f, order="ctx_lat_up", post=False, n_ctx_rows=n_ctx, name="scan_f",
                             side=ici_down(w_down_f, 5, 6))
    h_b, (w_down_f,) = _scan(a_b, bx_b, order="ctx_lat_down", post=False, n_ctx_rows=n_ctx, name="scan_b",
                             side=ici_down(w_down_f, 6, 7))
    u, (w_down_f,) = _rnn_gate_fwd(h_f, h_b, z, xg_off, n_ctx, tr, tcol, side=ici_down(w_down_f, 7, 8))
    y_attn, (w_down_f,) = _matmul(attn_o, w_oa_f, name="mm_o_attn", side=_gather_d2d([w_down_f], [False]))
    y_rnn = _matmul(u, w_or_f, name="mm_o_rnn")
    mrg = _merge_fwd(y_attn, y_rnn, z, gl_off, n_ctx, tr, tcol)
    mix = _matmul(mrg, w_out_f, name="mm_out")
    x1, h2 = _resid_norm_mod_fwd(x[0], mix, g_mlp, mod3, tr)
    up = _matmul(h2, w_up_f, name="mm_up")
    act = _sq_relu(up, tr, tcol)
    down = _matmul(act, w_down_f, name="mm_down")
    dx2, d_down, sums_fin, loss_blk = _final_fwd_bwd(x1, down, loss_target[0], g_final[None, :], gate_f, tr)

    d_act = _matmul(d_down, w_down_f, tb=True, name="mm_d_act")
    g_w_down = _matmul(act, d_down, ta=True, out_dtype=BF16, name="mm_g_down")
    d_up, (got,) = _sq_relu_bwd(d_act, up, tr, tcol, side=_swap_halves([g_w_down], [False]))
    p_down = pair_sum('w_down', g_w_down, got, False)
    def scatter(pairs, cols, lo, hi, into=None):
        return _scatter_regions(pairs, cols, (lo, hi, 8), into)

    dh2, got_down = _matmul(d_up, w_up_f, tb=True, name="mm_d_h2", side=scatter([p_down], [False], 0, 3))
    g_w_up, got_down = _matmul(h2, d_up, ta=True, out_dtype=BF16, name="mm_g_up",
                               side=scatter([p_down], [False], 3, 6, got_down))
    (dx1, d_mix, sums2), (got,) = _norm_mod_bwd2(x1, dh2, dx2, mix, g_mlp, mod3, tr, side=_swap_halves([g_w_up], [True]))
    p_up = pair_sum('w_up', g_w_up, got, True)
    d_mrg = _matmul(d_mix, w_out_f, tb=True, name="mm_d_mrg")
    g_w_out = _matmul(mrg, d_mix, ta=True, out_dtype=BF16, name="mm_g_out")
    (d_ya, d_yr, d_gla, d_glr), got_down = _merge_bwd(d_mrg, y_attn, y_rnn, z, gl_off, n_ctx, tr, tcol,
                                                      side=scatter([p_down], [False], 6, 8, got_down))
    hs_down = chip_sum('w_down', p_down, got_down[0], False)
    d_o = _matmul(d_ya, w_oa_f, tb=True, out_dtype=BF16, name="mm_d_o")
    g_w_oa = _matmul(attn_o, d_ya, ta=True, out_dtype=BF16, name="mm_g_o_attn")
    d_u = _matmul(d_yr, w_or_f, tb=True, name="mm_d_u")
    g_w_or = _matmul(u, d_yr, ta=True, out_dtype=BF16, name="mm_g_o_rnn")
    (d_rnn, d_xg), got_up = _rnn_gate_bwd(d_u, h_f, h_b, z, xg_off, n_ctx, tr, tcol, side=scatter([p_up], [True], 0, 2))
    gs_f = _scan(a_f, d_rnn, order="lat_ctx_down", post=True, n_ctx_rows=n_ctx, name="scan_f_bwd")
    gs_b = _scan(a_b, d_rnn, order="lat_ctx_up", post=True, n_ctx_rows=n_ctx, name="scan_b_bwd")
    (d_xr, g_w_rg, g_w_ig, sums_rnn), got_up = _rnn_bwd(
        z, xr_off, xc, gs_f, gs_b, h_f, h_b, conv_w_f, w_rg[0], b_rg_f, w_ig[0], b_ig_f, lam_f, n_ctx,
        side=scatter([p_up], [True], 2, 8, got_up))
    hs_up = chip_sum('w_up', p_up, got_up[0], True)
    gate_cols = 8 * LANES if g_w_rg.size % (8 * LANES * N_CHIPS * 16) == 0 else 2 * LANES
    gate_rows = g_w_rg.size // gate_cols
    mid_names = ['w_o_attn', 'w_o_rnn', 'w_out'] + GATES
    mid_grads = [g_w_oa, g_w_or, g_w_out, g_w_rg.reshape(gate_rows, gate_cols), g_w_ig.reshape(gate_rows, gate_cols)]
    row5 = [False] * 5
    (dq, dk, dv), (*got_mid, gs_down, gs_up) = _attn_bwd(qr, kr, z, v_off, d_o, n_ctx, group, tr, side=_sides(
        _swap_halves(mid_grads, row5), _join_halves([hs_down], [False]), _join_halves([hs_up], [True])))
    p_mid = [pair_sum(n, g, o, False) for n, g, o in zip(mid_names, mid_grads, got_mid)]
    (d_q_raw, g_q_gain), got_mid = _head_prep_bwd(z, 0, attn_w // LANES, q_gain, cos, sin, dq, n_ctx, tr, "q_prep_bwd",
                                                  side=scatter(p_mid, row5, 0, 2))
    d_k_raw, g_k_gain = _head_prep_bwd(z, k_off, kv_w // LANES, k_gain, cos, sin, dk, 0, tr, "k_prep_bwd")
    zero_ctx = lambda w: jnp.zeros((n_ctx, w), BF16)
    dz = jnp.concatenate([
        jnp.concatenate([zero_ctx(attn_w), d_q_raw], axis=0), d_k_raw, dv.astype(BF16), d_xr,
        jnp.concatenate([zero_ctx(d), d_xg], axis=0), jnp.concatenate([zero_ctx(d), d_gla], axis=0),
        jnp.concatenate([zero_ctx(d), d_glr], axis=0)], axis=1)
    g_w_in, got_mid = _matmul(h, dz, ta=True, out_dtype=BF16, name="mm_g_in", side=scatter(p_mid, row5, 2, 8, got_mid))
    hs_mid = [chip_sum(n, p, o, False) for n, p, o in zip(mid_names, p_mid, got_mid)]
    (got,) = _exchange(_swap_halves([g_w_in], [True]), "swap_w_in")
    p_in = pair_sum('w_in', g_w_in, got, True)
    dh, got_in = _matmul(dz, w_in_f, tb=True, name="mm_d_h", side=scatter([p_in], [True], 0, 4))
    (grad_x, sums1), got_in = _norm_mod_bwd1(xcat, dh, dx1, g_mix, mod4, n_ctx, tr,
                                             side=scatter([p_in], [True], 4, 5, got_in))

    zeros_d = jnp.zeros((d,), F32)
    dmod_lat = jnp.concatenate([sums1[0], sums1[1], sums2[3], sums2[0], sums2[1], sums_fin[1]])
    dmod_ctx = jnp.concatenate([sums1[3], sums1[4]] + [zeros_d] * 4)
    small_parts = [dmod_lat, dmod_ctx, loss_blk[0, 0:1], sums1[2] + sums1[5], sums2[2], g_q_gain[0], g_k_gain[0],
                   sums_rnn[10], sums_fin[0], sums_rnn[6:10], sums_rnn[0:2], sums_rnn[2:4], sums_rnn[4:6]]
    pack1 = _pack(small_parts)
    got1 = _all_gather8(pack1, "gather_small_grads").reshape(N_DEV, -1, LANES)
    tot1 = _sum_leading(got1, "sum_small_grads")
    part_shapes = [a.shape for a in small_parts]
    (s_dmod_lat, s_dmod_ctx, s_loss, g_g_mix, g_g_mlp, g_q_gain, g_k_gain, g_conv_b, g_g_final,
     g_conv_w_f, g_b_rg_f, g_b_ig_f, g_lam_f) = _unpack(tot1, part_shapes)
    loss = s_loss[0]
    g_b_mod = (s_dmod_lat + s_dmod_ctx)[None, :]
    n_mod_rows = _part_rows(d_mod)
    dmod16 = jnp.concatenate([got1[:, :n_mod_rows].reshape(N_DEV, -1)[:, :d_mod], s_dmod_ctx[None, :],
                              jnp.zeros((16 - N_DEV - 1, d_mod), F32)], axis=0)
    dmod16_shard = lax.dynamic_slice(dmod16, (0, chip * (d_mod // N_CHIPS)), (16, d_mod // N_CHIPS))
    g_w_mod = _matmul(silu16, dmod16_shard, ta=True, name="mm_g_mod")
    dsilu_part = _matmul(dmod16_shard[N_DEV:], w_mod[0], tb=True, name="mm_d_silu")
    dsilu_all = _all_gather8(dsilu_part, "gather_d_silu").reshape(N_DEV, 8, d)
    g_c_ctx = _c_ctx_grad(dsilu_all, c_ctx[None, :])[0]

    def shard_of(full):
        w = full.shape[-1] // N_CHIPS
        return lax.dynamic_slice(full, (0, chip * w), (full.shape[0], w))

    grads = {
        'c_ctx': g_c_ctx, 'b_mod': g_b_mod, 'g_mix': g_g_mix[None, :], 'g_mlp': g_g_mlp[None, :],
        'q_gain': g_q_gain[None, :], 'k_gain': g_k_gain[None, :], 'conv_b': g_conv_b[None, :],
        'g_final': g_g_final,
        'conv_w': shard_of(g_conv_w_f)[None], 'b_rg': shard_of(g_b_rg_f)[None], 'b_ig': shard_of(g_b_ig_f)[None],
        'lru_lambda': shard_of(g_lam_f)[None], 'w_mod': g_w_mod[None],
    }

    delta, new_m, new_v = {}, {}, {}

    def adamw(n, side=None):
        shp = weights[n].shape
        as2d = (lambda a: a[0]) if n not in GATES else (lambda a: a.reshape(-1, LANES))
        out = _adamw(as2d(weights[n]), as2d(grads[n]), as2d(moms[n]), as2d(vars_[n]), "adamw_" + n, side=side)
        (dl, nm, nv), got = out if side is not None else (out, None)
        delta[n], new_m[n], new_v[n] = dl.reshape(shp), nm.reshape(shp), nv.reshape(shp)
        return got

    grads['w_up'], grads['w_down'] = gs_up[None], gs_down[None]
    got_in = adamw('w_mod', side=scatter([p_in], [True], 5, 7, got_in))
    got_in = adamw('w_up', side=scatter([p_in], [True], 7, 8, got_in))
    hs_in = chip_sum('w_in', p_in, got_in[0], True)
    rest = _exchange(_join_halves([hs_in] + hs_mid, [True] + row5), "join_rest")
    for n, g in zip(['w_in', 'w_o_attn', 'w_o_rnn', 'w_out'], rest[:4]):
        grads[n] = g[None]
    gate_all = _all_gather8(jnp.concatenate(rest[4:], axis=0), "gather_gate_grads")
    gate_all = gate_all.reshape(N_CHIPS, 2, len(GATES), gate_rows // N_CHIPS, gate_cols)[:, 0]
    for i, n in enumerate(GATES):
        grads[n] = gate_all[:, i].reshape(weights[n].shape)
    for n in BIG + GATES:
        if n != 'w_up':
            adamw(n)
    small_shapes = [weights[n].shape for n in SMALL]
    packed = [_pack([src[n] for n in SMALL], 512) for src in (weights, grads, moms, vars_)]
    outs = _adamw(*packed, "adamw_small")
    for res, out in zip((delta, new_m, new_v), outs):
        for n, a in zip(SMALL, _unpack(out, small_shapes)):
            res[n] = a
    return (loss, grad_x[None], *[grads[n] for n in WEIGHT_NAMES], *[delta[n] for n in WEIGHT_NAMES],
            *[new_m[n] for n in WEIGHT_NAMES], *[new_v[n] for n in WEIGHT_NAMES])
```

```python
import functools
import math
from typing import Callable, NamedTuple

import jax
import jax.numpy as jnp
from jax import lax
from jax.experimental import pallas as pl
from jax.experimental.pallas import tpu as pltpu

F32 = jnp.float32
BF16 = jnp.bfloat16
MESH_ID = pl.DeviceIdType.MESH
ANY = pl.BlockSpec(memory_space=pl.ANY)

NORM_EPS = 1e-6
LRU_C = 8.0
GRID_W = 64
ROPE_THETA = 10000.0
N_MOD = 6
CONV_WIDTH = 4
ADAM_LR = 0.001
ADAM_B1 = 0.9
ADAM_B2 = 0.999
ADAM_EPS = 1e-08
ADAM_WD = 0.01
ADAM_STEP = 10

LANES = 128
SUBLANES = 8
V7X_VMEM_LIMIT = 48 * 1024 * 1024
N_CHIPS = 4
N_DEV = 8
GELU_C = math.sqrt(2.0 / math.pi)
GELU_A = 0.044715


def _tile(dim, pref, align):
    t = min(pref, dim)
    t -= t % align
    while t >= align:
        if dim % t == 0:
            return t
        t -= align
    return dim


def _params(*sem):
    return pltpu.CompilerParams(dimension_semantics=sem, vmem_limit_bytes=V7X_VMEM_LIMIT)


def _sds(shape, dtype=F32):
    return jax.ShapeDtypeStruct(shape, dtype)


class _Side(NamedTuple):
    operands: tuple
    results: tuple
    aliases: dict
    n_sems: int
    build: Callable


def _sides(*sides):
    ops, res, aliases, spans, n = [], [], {}, [], 0
    for s in sides:
        spans.append((len(ops), len(res), n))
        aliases.update({len(ops) + i: len(res) + j for i, j in s.aliases.items()})
        ops += s.operands
        res += s.results
        n += s.n_sems

    def build(op_refs, res_refs, send_sems, recv_sems, sem0):
        sends, recvs = [], []
        for s, (o, r, k) in zip(sides, spans):
            a, b = s.build(op_refs[o:o + len(s.operands)], res_refs[r:r + len(s.results)], send_sems, recv_sems,
                           sem0 + k)
            sends += a
            recvs += b
        return sends, recvs

    return _Side(tuple(ops), tuple(res), aliases, n, build)


def _call(body, *, side=None, sem=(), grid=(), in_specs=(), out_specs=(), out_shape=(), scratch_shapes=(), **kw):
    if side is None:
        return pl.pallas_call(body, grid=grid, in_specs=list(in_specs), out_specs=out_specs, out_shape=out_shape,
                              scratch_shapes=list(scratch_shapes), compiler_params=_params(*sem), **kw)
    many = isinstance(out_shape, (list, tuple))
    out_specs_l, out_shape_l = (list(out_specs), list(out_shape)) if many else ([out_specs], [out_shape])
    n_in, n_out, n_scr = len(in_specs), len(out_shape_l), len(scratch_shapes)
    n_op, n_res = len(side.operands), len(side.results)

    def hosted(*refs):
        ins, ops = refs[:n_in], refs[n_in:n_in + n_op]
        outs = refs[n_in + n_op:n_in + n_op + n_out]
        res = refs[n_in + n_op + n_out:n_in + n_op + n_out + n_res]
        scr = refs[n_in + n_op + n_out + n_res:-2]
        send_sems, recv_sems = refs[-2:]

        def start():
            for cp in side.build(ops, res, send_sems, recv_sems, 0)[0]:
                cp.start()

        def finish():
            sends, recvs = side.build(ops, res, send_sems, recv_sems, 0)
            for cp in recvs:
                cp.wait_recv()
            for cp in sends:
                cp.wait_send()

        if not grid:
            start()
            finish()
            return
        ids = [pl.program_id(a) for a in range(len(grid))]
        first = functools.reduce(jnp.logical_and, [i == 0 for i in ids])
        last = functools.reduce(jnp.logical_and, [i == g - 1 for i, g in zip(ids, grid)])
        pl.when(first)(start)
        body(*ins, *outs, *scr)
        pl.when(last)(finish)

    def run(*args):
        got = pl.pallas_call(
            hosted, grid=grid, in_specs=[*in_specs, *[ANY] * n_op], out_specs=[*out_specs_l, *[ANY] * n_res],
            out_shape=[*out_shape_l, *side.results],
            scratch_shapes=[*scratch_shapes, pltpu.SemaphoreType.DMA((side.n_sems,)),
                            pltpu.SemaphoreType.DMA((side.n_sems,))],
            input_output_aliases={n_in + i: n_out + j for i, j in side.aliases.items()},
            compiler_params=_params(*["arbitrary"] * len(grid)), **kw)(*args, *side.operands)
        own = list(got[:n_out]) if many else got[0]
        return own, list(got[n_out:])

    return run


def _matmul(a, b, *, ta=False, tb=False, out_dtype=F32, name, tm=1024, tn=1024, tk=2816, side=None):
    k_dim, m = a.shape if ta else a.shape[::-1]
    n, k2 = b.shape if tb else b.shape[::-1]
    assert k_dim == k2, (a.shape, b.shape, ta, tb)
    tm = _tile(m, tm, LANES if ta else 16)
    tn = _tile(n, tn, 16 if tb else LANES)
    tk = _tile(k_dim, tk, LANES)
    nk = k_dim // tk
    dims = (((0 if ta else 1,), (1 if tb else 0,)), ((), ()))
    if nk == 1:
        def whole(a_ref, b_ref, o_ref):
            o_ref[...] = lax.dot_general(a_ref[...].astype(BF16), b_ref[...].astype(BF16), dims,
                                         preferred_element_type=F32).astype(o_ref.dtype)

        a_spec = pl.BlockSpec((tk, tm), lambda i, j: (0, i)) if ta else pl.BlockSpec((tm, tk), lambda i, j: (i, 0))
        b_spec = pl.BlockSpec((tn, tk), lambda i, j: (j, 0)) if tb else pl.BlockSpec((tk, tn), lambda i, j: (0, j))
        return _call(
            whole, side=side, name=name, grid=(m // tm, n // tn), in_specs=[a_spec, b_spec],
            out_specs=pl.BlockSpec((tm, tn), lambda i, j: (i, j)), out_shape=_sds((m, n), out_dtype),
            sem=("parallel", "parallel"),
        )(a, b)

    def body(a_ref, b_ref, o_ref, acc_ref):
        k = pl.program_id(2)

        @pl.when(k == 0)
        def _():
            acc_ref[...] = jnp.zeros_like(acc_ref)

        acc_ref[...] += lax.dot_general(a_ref[...].astype(BF16), b_ref[...].astype(BF16), dims,
                                        preferred_element_type=F32)

        @pl.when(k == nk - 1)
        def _():
            o_ref[...] = acc_ref[...].astype(o_ref.dtype)

    a_spec = pl.BlockSpec((tk, tm), lambda i, j, k: (k, i)) if ta else pl.BlockSpec((tm, tk), lambda i, j, k: (i, k))
    b_spec = pl.BlockSpec((tn, tk), lambda i, j, k: (j, k)) if tb else pl.BlockSpec((tk, tn), lambda i, j, k: (k, j))
    return _call(
        body, side=side, name=name, grid=(m // tm, n // tn, nk), in_specs=[a_spec, b_spec],
        out_specs=pl.BlockSpec((tm, tn), lambda i, j, k: (i, j)), out_shape=_sds((m, n), out_dtype),
        scratch_shapes=[pltpu.VMEM((tm, tn), F32)], sem=("parallel", "parallel", "arbitrary"),
    )(a, b)


def _silu(x):
    return x * jax.nn.sigmoid(x)


def _gelu(x):
    return 0.5 * x * (1.0 + jnp.tanh(GELU_C * (x + GELU_A * x * x * x)))


def _gelu_grad(x):
    t = jnp.tanh(GELU_C * (x + GELU_A * x * x * x))
    return 0.5 * (1.0 + t) + 0.5 * x * (1.0 - t * t) * GELU_C * (1.0 + 3.0 * GELU_A * x * x)


def _expm1_nonpos(x):
    series = x * (1.0 + x * (1.0 / 2 + x * (1.0 / 6 + x * (1.0 / 24 + x * (1.0 / 120 + x * (1.0 / 720 + x / 5040))))))
    return jnp.where(x > -0.25, series, jnp.exp(x) - 1.0)


def _softplus(x):
    return jnp.maximum(x, 0.0) + jnp.log1p(jnp.exp(-jnp.abs(x)))


def _rms_stats(x):
    return lax.rsqrt(jnp.mean(x * x, axis=-1, keepdims=True) + NORM_EPS)


def _rms_bwd(dxhat, xhat, rstd):
    return rstd * (dxhat - xhat * jnp.mean(dxhat * xhat, axis=-1, keepdims=True))


def _colsum(v):
    return jnp.sum(v, axis=0, keepdims=True)


def _mod_fwd(c16, w_mod, b_mod_shard):
    r, d = c16.shape
    n = w_mod.shape[1]
    tn = _tile(n, 512, LANES)

    def body(c_ref, w_ref, b_ref, o_ref, s_ref):
        s = _silu(c_ref[...])
        s_ref[...] = s
        o_ref[...] = jnp.dot(s.astype(BF16), w_ref[...].astype(BF16), preferred_element_type=F32) + b_ref[...]

    return pl.pallas_call(
        body, name="mod_fwd", grid=(n // tn,),
        in_specs=[pl.BlockSpec((r, d), lambda j: (0, 0)), pl.BlockSpec((d, tn), lambda j: (0, j)),
                  pl.BlockSpec((1, tn), lambda j: (0, j))],
        out_specs=[pl.BlockSpec((r, tn), lambda j: (0, j)), pl.BlockSpec((r, d), lambda j: (0, 0))],
        out_shape=[_sds((r, n)), _sds((r, d))], compiler_params=_params("arbitrary"),
    )(c16, w_mod, b_mod_shard)


def _c_ctx_grad(parts, c_ctx_row):
    d = c_ctx_row.shape[1]

    def body(p_ref, c_ref, o_ref):
        tot = p_ref[0, 0:1, :]
        for chip in range(1, N_CHIPS):
            tot = tot + p_ref[2 * chip, 0:1, :]
        c = c_ref[...]
        sg = jax.nn.sigmoid(c)
        o_ref[...] = tot * (sg * (1.0 + c * (1.0 - sg)))

    return pl.pallas_call(body, name="c_ctx_grad", out_shape=_sds((1, d)), compiler_params=_params())(parts, c_ctx_row)


def _norm_mod_fwd(xcat, g, mod4, n_ctx_rows, tr):
    t, d = xcat.shape
    nctx = n_ctx_rows // tr

    def body(x_ref, g_ref, mod_ref, h_ref):
        x = x_ref[...]
        n = x * _rms_stats(x) * g_ref[...]
        is_ctx = pl.program_id(0) < nctx
        sh = jnp.where(is_ctx, mod_ref[0:1, :], mod_ref[2:3, :])
        sc = jnp.where(is_ctx, mod_ref[1:2, :], mod_ref[3:4, :])
        h_ref[...] = (n * (1.0 + sc) + sh).astype(BF16)

    return pl.pallas_call(
        body, name="norm_mod_fwd", grid=(t // tr,),
        in_specs=[pl.BlockSpec((tr, d), lambda i: (i, 0)), pl.BlockSpec((1, d), lambda i: (0, 0)),
                  pl.BlockSpec((4, d), lambda i: (0, 0))],
        out_specs=pl.BlockSpec((tr, d), lambda i: (i, 0)), out_shape=_sds((t, d), BF16),
        compiler_params=_params("parallel"),
    )(xcat, g, mod4)


def _norm_mod_bwd1(xcat, dh, dx1, g, mod4, n_ctx_rows, tr, side=None):
    t, d = xcat.shape
    nctx = n_ctx_rows // tr
    s = t - n_ctx_rows

    def body(x_ref, dh_ref, dx1_ref, g_ref, mod_ref, dx_ref, sums_ref):
        i = pl.program_id(0)
        is_ctx = i < nctx

        @pl.when(i == 0)
        def _():
            sums_ref[...] = jnp.zeros_like(sums_ref)

        x = x_ref[...]
        dh_ = dh_ref[...]
        rstd = _rms_stats(x)
        xhat = x * rstd
        gg = g_ref[...]
        sc = jnp.where(is_ctx, mod_ref[1:2, :], mod_ref[3:4, :])
        dxhat = dh_ * (1.0 + sc) * gg
        dx_ref[...] = dx1_ref[...] + _rms_bwd(dxhat, xhat, rstd)
        part = [_colsum(dh_), _colsum(dh_ * xhat * gg), _colsum(dh_ * (1.0 + sc) * xhat)]

        @pl.when(is_ctx)
        def _():
            for k, row in enumerate(part):
                sums_ref[3 + k:4 + k, :] += row

        @pl.when(jnp.logical_not(is_ctx))
        def _():
            for k, row in enumerate(part):
                sums_ref[k:k + 1, :] += row

    lat = lambda i: (jnp.maximum(i - nctx, 0), 0)
    return _call(
        body, side=side, name="norm_mod_bwd1", grid=(t // tr,),
        in_specs=[pl.BlockSpec((tr, d), lambda i: (i, 0)), pl.BlockSpec((tr, d), lambda i: (i, 0)),
                  pl.BlockSpec((tr, d), lat), pl.BlockSpec((1, d), lambda i: (0, 0)),
                  pl.BlockSpec((4, d), lambda i: (0, 0))],
        out_specs=[pl.BlockSpec((tr, d), lat), pl.BlockSpec((8, d), lambda i: (0, 0))],
        out_shape=[_sds((s, d)), _sds((8, d))], sem=("arbitrary",),
    )(xcat, dh, dx1, g, mod4)


def _resid_norm_mod_fwd(x, mix, g, mod3, tr):
    s, d = x.shape

    def body(x_ref, mix_ref, g_ref, mod_ref, x1_ref, h_ref):
        x1 = x_ref[...] + mod_ref[0:1, :] * mix_ref[...]
        x1_ref[...] = x1
        n = x1 * _rms_stats(x1) * g_ref[...]
        h_ref[...] = (n * (1.0 + mod_ref[2:3, :]) + mod_ref[1:2, :]).astype(BF16)

    row = pl.BlockSpec((tr, d), lambda i: (i, 0))
    return pl.pallas_call(
        body, name="resid_norm_mod_fwd", grid=(s // tr,),
        in_specs=[row, row, pl.BlockSpec((1, d), lambda i: (0, 0)), pl.BlockSpec((3, d), lambda i: (0, 0))],
        out_specs=[row, row], out_shape=[_sds((s, d)), _sds((s, d), BF16)], compiler_params=_params("parallel"),
    )(x, mix, g, mod3)


def _norm_mod_bwd2(x1, dh2, dx2, mix, g, mod3, tr, side=None):
    s, d = x1.shape

    def body(x_ref, dh_ref, dx2_ref, mix_ref, g_ref, mod_ref, dx1_ref, dmix_ref, sums_ref):
        @pl.when(pl.program_id(0) == 0)
        def _():
            sums_ref[...] = jnp.zeros_like(sums_ref)

        x = x_ref[...]
        dh_ = dh_ref[...]
        rstd = _rms_stats(x)
        xhat = x * rstd
        gg = g_ref[...]
        sc = mod_ref[2:3, :]
        dx1 = dx2_ref[...] + _rms_bwd(dh_ * (1.0 + sc) * gg, xhat, rstd)
        dx1_ref[...] = dx1
        dmix_ref[...] = (dx1 * mod_ref[0:1, :]).astype(BF16)
        part = [_colsum(dh_), _colsum(dh_ * xhat * gg), _colsum(dh_ * (1.0 + sc) * xhat), _colsum(dx1 * mix_ref[...])]
        for k, row in enumerate(part):
            sums_ref[k:k + 1, :] += row

    row = pl.BlockSpec((tr, d), lambda i: (i, 0))
    return _call(
        body, side=side, name="norm_mod_bwd2", grid=(s // tr,),
        in_specs=[row, row, row, row, pl.BlockSpec((1, d), lambda i: (0, 0)), pl.BlockSpec((3, d), lambda i: (0, 0))],
        out_specs=[row, row, pl.BlockSpec((8, d), lambda i: (0, 0))],
        out_shape=[_sds((s, d)), _sds((s, d), BF16), _sds((8, d))], sem=("arbitrary",),
    )(x1, dh2, dx2, mix, g, mod3)


def _final_fwd_bwd(x1, down, target, g_final, gate, tr):
    s, d = x1.shape

    def body(x1_ref, down_ref, t_ref, g_ref, gate_ref, dx2_ref, ddown_ref, sums_ref, loss_ref):
        @pl.when(pl.program_id(0) == 0)
        def _():
            sums_ref[...] = jnp.zeros_like(sums_ref)
            loss_ref[...] = jnp.zeros_like(loss_ref)

        down_ = down_ref[...]
        gate_ = gate_ref[...]
        x2 = x1_ref[...] + gate_ * down_
        rstd = _rms_stats(x2)
        xhat = x2 * rstd
        gg = g_ref[...]
        err = xhat * gg - t_ref[...]
        loss_ref[...] += 0.5 * jnp.sum(jnp.mean(err * err, axis=-1, keepdims=True))
        dy = err * (1.0 / d)
        dx2 = _rms_bwd(dy * gg, xhat, rstd)
        dx2_ref[...] = dx2
        ddown_ref[...] = (dx2 * gate_).astype(BF16)
        sums_ref[0:1, :] += _colsum(dy * xhat)
        sums_ref[1:2, :] += _colsum(dx2 * down_)

    row = pl.BlockSpec((tr, d), lambda i: (i, 0))
    vec = pl.BlockSpec((1, d), lambda i: (0, 0))
    return pl.pallas_call(
        body, name="final_fwd_bwd", grid=(s // tr,), in_specs=[row, row, row, vec, vec],
        out_specs=[row, row, pl.BlockSpec((8, d), lambda i: (0, 0)), pl.BlockSpec((8, LANES), lambda i: (0, 0))],
        out_shape=[_sds((s, d)), _sds((s, d), BF16), _sds((8, d)), _sds((8, LANES))],
        compiler_params=_params("arbitrary"),
    )(x1, down, target, g_final, gate)


def _swap_pairs(v):
    lane = lax.broadcasted_iota(jnp.int32, v.shape, 1)
    return jnp.where(lane % 2 == 0, pltpu.roll(v, LANES - 1, 1), pltpu.roll(v, 1, 1))


def _head_prep_fwd(z, col_off, n_heads, gain, cos, sin, tr, name, side=None):
    t = z.shape[0]
    per = math.gcd(4, n_heads, col_off // LANES)
    w = per * LANES
    hb = col_off // w

    def body(z_ref, g_ref, cos_ref, sin_ref, o_ref):
        for hh in range(per):
            cols = slice(hh * LANES, (hh + 1) * LANES)
            x = z_ref[:, cols]
            y = x * _rms_stats(x) * g_ref[...]
            o_ref[:, cols] = (y * cos_ref[...] + _swap_pairs(y) * sin_ref[...]).astype(BF16)

    tab = pl.BlockSpec((tr, LANES), lambda i, j: (i, 0))
    return _call(
        body, side=side, name=name, grid=(t // tr, n_heads // per),
        in_specs=[pl.BlockSpec((tr, w), lambda i, j: (i, hb + j)), pl.BlockSpec((1, LANES), lambda i, j: (0, 0)),
                  tab, tab],
        out_specs=pl.BlockSpec((tr, w), lambda i, j: (i, j)), out_shape=_sds((t, n_heads * LANES), BF16),
        sem=("parallel", "parallel"),
    )(z, gain, cos, sin)


def _head_prep_bwd(z, col_off, n_heads, gain, cos, sin, dout, row_off, tr, name, side=None):
    r = dout.shape[0]
    per = math.gcd(4, n_heads, col_off // LANES)
    w = per * LANES
    hb = col_off // w
    rb = row_off // tr

    def body(z_ref, g_ref, cos_ref, sin_ref, d_ref, dz_ref, dg_ref):
        @pl.when(jnp.logical_and(pl.program_id(0) == 0, pl.program_id(1) == 0))
        def _():
            dg_ref[...] = jnp.zeros_like(dg_ref)

        for hh in range(per):
            cols = slice(hh * LANES, (hh + 1) * LANES)
            x = z_ref[:, cols]
            rstd = _rms_stats(x)
            xhat = x * rstd
            dd = d_ref[:, cols]
            dy = dd * cos_ref[...] - _swap_pairs(dd) * sin_ref[...]
            dg_ref[0:1, :] += _colsum(dy * xhat)
            dz_ref[:, cols] = _rms_bwd(dy * g_ref[...], xhat, rstd).astype(BF16)

    tab = pl.BlockSpec((tr, LANES), lambda i, j: (rb + i, 0))
    return _call(
        body, side=side, name=name, grid=(r // tr, n_heads // per),
        in_specs=[pl.BlockSpec((tr, w), lambda i, j: (rb + i, hb + j)),
                  pl.BlockSpec((1, LANES), lambda i, j: (0, 0)), tab, tab,
                  pl.BlockSpec((tr, w), lambda i, j: (i, j))],
        out_specs=[pl.BlockSpec((tr, w), lambda i, j: (i, j)), pl.BlockSpec((8, LANES), lambda i, j: (0, 0))],
        out_shape=[_sds((r, n_heads * LANES), BF16), _sds((8, LANES))], sem=("arbitrary", "arbitrary"),
    )(z, gain, cos, sin, dout)


def _softmax_rows(q, k, scale):
    s = lax.dot_general(q, k, (((1,), (1,)), ((), ())), preferred_element_type=F32) * scale
    p = jnp.exp(s - jnp.max(s, axis=-1, keepdims=True))
    return p / jnp.sum(p, axis=-1, keepdims=True)


def _attn_fwd(qr, kr, z, v_off, n_ctx_rows, group, tq, side=None):
    t, kvw = kr.shape
    s = t - n_ctx_rows
    n_kv = kvw // LANES
    scale = LANES ** -0.5
    qb0 = n_ctx_rows // tq
    vb = v_off // LANES

    def body(q_ref, k_ref, v_ref, o_ref):
        k = k_ref[...]
        v = v_ref[...].astype(BF16)
        for g in range(group):
            cols = slice(g * LANES, (g + 1) * LANES)
            p = _softmax_rows(q_ref[:, cols], k, scale)
            o_ref[:, cols] = jnp.dot(p.astype(BF16), v, preferred_element_type=F32).astype(BF16)

    return _call(
        body, side=side, name="attn_fwd", grid=(n_kv, s // tq),
        in_specs=[pl.BlockSpec((tq, group * LANES), lambda h, i: (qb0 + i, h)),
                  pl.BlockSpec((t, LANES), lambda h, i: (0, h)), pl.BlockSpec((t, LANES), lambda h, i: (0, vb + h))],
        out_specs=pl.BlockSpec((tq, group * LANES), lambda h, i: (i, h)),
        out_shape=_sds((s, n_kv * group * LANES), BF16), sem=("parallel", "parallel"),
    )(qr, kr, z)


def _attn_bwd(qr, kr, z, v_off, d_o, n_ctx_rows, group, tq, side=None):
    t, kvw = kr.shape
    s = t - n_ctx_rows
    n_kv = kvw // LANES
    scale = LANES ** -0.5
    qb0 = n_ctx_rows // tq
    vb = v_off // LANES
    tn_dims = (((0,), (0,)), ((), ()))
    nt_dims = (((1,), (1,)), ((), ()))

    def body(q_ref, k_ref, v_ref, do_ref, dq_ref, dk_ref, dv_ref):
        @pl.when(pl.program_id(1) == 0)
        def _():
            dk_ref[...] = jnp.zeros_like(dk_ref)
            dv_ref[...] = jnp.zeros_like(dv_ref)

        k = k_ref[...]
        v = v_ref[...].astype(BF16)
        for g in range(group):
            cols = slice(g * LANES, (g + 1) * LANES)
            q = q_ref[:, cols]
            do_ = do_ref[:, cols]
            p = _softmax_rows(q, k, scale)
            dv_ref[...] += lax.dot_general(p.astype(BF16), do_, tn_dims, preferred_element_type=F32)
            dp = lax.dot_general(do_, v, nt_dims, preferred_element_type=F32)
            ds = (p * (dp - jnp.sum(p * dp, axis=-1, keepdims=True)) * scale).astype(BF16)
            dq_ref[:, cols] = jnp.dot(ds, k, preferred_element_type=F32)
            dk_ref[...] += lax.dot_general(ds, q, tn_dims, preferred_element_type=F32)

    qspec = pl.BlockSpec((tq, group * LANES), lambda h, i: (qb0 + i, h))
    ospec = pl.BlockSpec((tq, group * LANES), lambda h, i: (i, h))
    kspec = pl.BlockSpec((t, LANES), lambda h, i: (0, h))
    return _call(
        body, side=side, name="attn_bwd", grid=(n_kv, s // tq),
        in_specs=[qspec, kspec, pl.BlockSpec((t, LANES), lambda h, i: (0, vb + h)), ospec],
        out_specs=[ospec, kspec, kspec],
        out_shape=[_sds((s, n_kv * group * LANES)), _sds((t, kvw)), _sds((t, kvw))], sem=("parallel", "arbitrary"),
    )(qr, kr, z, d_o)


def _row_mask(shape, rows):
    r = lax.broadcasted_iota(jnp.int32, shape, 0)
    m = r == rows[0]
    for v in rows[1:]:
        m = jnp.logical_or(m, r == v)
    return m


def _shift_rows(x, k, n_ctx_rows):
    t = x.shape[0]
    if k == 0:
        return x
    rolled = pltpu.roll(x, (-k) % t, 0)
    if k > 0:
        dead = [n_ctx_rows - 1 - i for i in range(k)] + [t - 1 - i for i in range(k)]
    else:
        dead = [i for i in range(-k)] + [n_ctx_rows + i for i in range(-k)]
    return jnp.where(_row_mask(x.shape, dead), 0.0, rolled)


def _conv(x, w, b, n_ctx_rows):
    y = b
    for k in range(CONV_WIDTH):
        y = y + _shift_rows(x, k - 1, n_ctx_rows) * w[k:k + 1, :]
    return y


def _gates(xc_bf, w_r, b_r, w_i, b_i, lam):
    r = jax.nn.sigmoid(jnp.dot(xc_bf, w_r.astype(BF16), preferred_element_type=F32) + b_r)
    i = jax.nn.sigmoid(jnp.dot(xc_bf, w_i.astype(BF16), preferred_element_type=F32) + b_i)
    log_a = -LRU_C * r * _softplus(-lam)
    a = jnp.exp(log_a)
    mult = jnp.sqrt(-_expm1_nonpos(2.0 * log_a))
    return r, i, a, mult


def _rnn_specs(t, xr_off):
    xb = xr_off // LANES
    return dict(
        zcol=pl.BlockSpec((t, LANES), lambda j: (0, xb + j)), col=pl.BlockSpec((t, LANES), lambda j: (0, j)),
        conv_w=pl.BlockSpec((CONV_WIDTH, LANES), lambda j: (0, j)), vec=pl.BlockSpec((1, LANES), lambda j: (0, j)),
        gate_w=pl.BlockSpec((2, 1, LANES, LANES), lambda j: (0, j, 0, 0)), two=pl.BlockSpec((2, LANES), lambda j: (0, j)))


def _rnn_prep(z, xr_off, conv_w, conv_b, w_rg, b_rg, w_ig, b_ig, lam, n_ctx_rows, side=None):
    t = z.shape[0]
    d = conv_b.shape[1]
    sp = _rnn_specs(t, xr_off)

    def body(z_ref, cw_ref, cb_ref, wr_ref, br_ref, wi_ref, bi_ref, lam_ref, xc_ref, af_ref, bf_ref, ab_ref, bb_ref):
        xc = _conv(z_ref[...], cw_ref[...], cb_ref[...], n_ctx_rows)
        xc_ref[...] = xc
        xc_bf = xc.astype(BF16)
        for dr, (a_ref, b_ref) in enumerate(((af_ref, bf_ref), (ab_ref, bb_ref))):
            _, i, a, mult = _gates(xc_bf, wr_ref[dr, 0], br_ref[dr:dr + 1, :], wi_ref[dr, 0], bi_ref[dr:dr + 1, :],
                                   lam_ref[dr:dr + 1, :])
            a_ref[...] = a
            b_ref[...] = mult * (i * xc)

    return _call(
        body, side=side, name="rnn_prep", grid=(d // LANES,),
        in_specs=[sp["zcol"], sp["conv_w"], sp["vec"], sp["gate_w"], sp["two"], sp["gate_w"], sp["two"], sp["two"]],
        out_specs=[sp["col"]] * 5, out_shape=[_sds((t, d))] * 5, sem=("parallel",),
    )(z, conv_w, conv_b, w_rg, b_rg, w_ig, b_ig, lam)


def _scan(a, b, *, order, post, n_ctx_rows, name, tc=256, side=None):
    t, d = a.shape
    tc = _tile(math.gcd(n_ctx_rows, t - n_ctx_rows), tc, SUBLANES)
    nt, nctx = t // tc, n_ctx_rows // tc
    nlat = nt - nctx
    b_lat_only = b.shape[0] != t
    up = order.endswith("up")

    def chunk(i):
        if order == "ctx_lat_up":
            return i
        if order == "lat_ctx_down":
            return nt - 1 - i
        if order == "ctx_lat_down":
            return jnp.where(i < nctx, nctx - 1 - i, nt - 1 - (i - nctx))
        return jnp.where(i < nlat, nctx + i, i - nlat)

    def body(a_ref, b_ref, o_ref, carry_ref):
        @pl.when(pl.program_id(0) == 0)
        def _():
            carry_ref[...] = jnp.zeros_like(carry_ref)

        live = jnp.where(chunk(pl.program_id(0)) >= nctx, 1.0, 0.0) if b_lat_only else None

        def group(gi, carry):
            base = pl.multiple_of((gi if up else tc // SUBLANES - 1 - gi) * SUBLANES, SUBLANES)
            for r in (range(SUBLANES) if up else range(SUBLANES - 1, -1, -1)):
                a_r = a_ref[pl.ds(base + r, 1), :]
                b_r = b_ref[pl.ds(base + r, 1), :]
                if live is not None:
                    b_r = b_r * live
                if post:
                    out = b_r + carry
                    carry = a_r * out
                else:
                    out = a_r * carry + b_r
                    carry = out
                o_ref[pl.ds(base + r, 1), :] = out
            return carry

        carry_ref[0:1, :] = lax.fori_loop(0, tc // SUBLANES, group, carry_ref[0:1, :])

    full = pl.BlockSpec((tc, d), lambda i: (chunk(i), 0))
    b_spec = pl.BlockSpec((tc, d), lambda i: (jnp.maximum(chunk(i) - nctx, 0), 0)) if b_lat_only else full
    return _call(
        body, side=side, name=name, grid=(nt,), in_specs=[full, b_spec], out_specs=full, out_shape=_sds((t, d)),
        scratch_shapes=[pltpu.VMEM((SUBLANES, d), F32)], sem=("arbitrary",),
    )(a, b)


def _rnn_bwd(z, xr_off, xc, g_f, g_b, h_f, h_b, conv_w, w_rg, b_rg, w_ig, b_ig, lam, n_ctx_rows, side=None):
    t, d = xc.shape
    sp = _rnn_specs(t, xr_off)
    tn_dims = (((0,), (0,)), ((), ()))
    nt_dims = (((1,), (1,)), ((), ()))

    def body(z_ref, xc_ref, gf_ref, gb_ref, hf_ref, hb_ref, cw_ref, wr_ref, br_ref, wi_ref, bi_ref, lam_ref,
             dxr_ref, dwr_ref, dwi_ref, sums_ref):
        xc_ = xc_ref[...]
        xc_bf = xc_.astype(BF16)
        dxc = jnp.zeros_like(xc_)
        sums = [None] * 6
        for dr, (g_ref, h_ref) in enumerate(((gf_ref, hf_ref), (gb_ref, hb_ref))):
            w_r, w_i, lam_ = wr_ref[dr, 0], wi_ref[dr, 0], lam_ref[dr:dr + 1, :]
            r, i, a, mult = _gates(xc_bf, w_r, br_ref[dr:dr + 1, :], w_i, bi_ref[dr:dr + 1, :], lam_)
            g = g_ref[...]
            h = h_ref[...]
            if dr == 0:
                h_prev = jnp.where(_row_mask(h.shape, [0]), 0.0, pltpu.roll(h, 1, 0))
            else:
                h_prev = jnp.where(_row_mask(h.shape, [n_ctx_rows - 1]), 0.0, pltpu.roll(h, t - 1, 0))
            d_mult = g * i * xc_
            d_i = g * mult * xc_
            dxc = dxc + g * mult * i
            d_log_a = g * h_prev * a - d_mult * a * a / mult
            sp_ = _softplus(-lam_)
            d_r = d_log_a * (-LRU_C) * sp_
            d_sp = _colsum(d_log_a * (-LRU_C) * r)
            du_r = (d_r * r * (1.0 - r))
            du_i = (d_i * i * (1.0 - i))
            sums[dr] = _colsum(du_r)
            sums[2 + dr] = _colsum(du_i)
            sums[4 + dr] = d_sp * (-jax.nn.sigmoid(-lam_))
            du_r_bf, du_i_bf = du_r.astype(BF16), du_i.astype(BF16)
            dwr_ref[dr, 0] = lax.dot_general(xc_bf, du_r_bf, tn_dims, preferred_element_type=F32).astype(BF16)
            dwi_ref[dr, 0] = lax.dot_general(xc_bf, du_i_bf, tn_dims, preferred_element_type=F32).astype(BF16)
            dxc = dxc + lax.dot_general(du_r_bf, w_r.astype(BF16), nt_dims, preferred_element_type=F32)
            dxc = dxc + lax.dot_general(du_i_bf, w_i.astype(BF16), nt_dims, preferred_element_type=F32)
        xr = z_ref[...]
        cw = cw_ref[...]
        dxr = jnp.zeros_like(dxc)
        rows = list(sums)
        for k in range(CONV_WIDTH):
            dxr = dxr + _shift_rows(dxc, 1 - k, n_ctx_rows) * cw[k:k + 1, :]
            rows.append(_colsum(dxc * _shift_rows(xr, k - 1, n_ctx_rows)))
        rows.append(_colsum(dxc))
        dxr_ref[...] = dxr.astype(BF16)
        sums_ref[...] = jnp.zeros_like(sums_ref)
        for k, row in enumerate(rows):
            sums_ref[k:k + 1, :] = row

    return _call(
        body, side=side, name="rnn_bwd", grid=(d // LANES,),
        in_specs=[sp["zcol"]] + [sp["col"]] * 5 + [sp["conv_w"], sp["gate_w"], sp["two"], sp["gate_w"], sp["two"],
                                                  sp["two"]],
        out_specs=[sp["col"], sp["gate_w"], sp["gate_w"], pl.BlockSpec((16, LANES), lambda j: (0, j))],
        out_shape=[_sds((t, d), BF16), _sds(w_rg.shape, BF16), _sds(w_ig.shape, BF16), _sds((16, d))],
        sem=("parallel",),
    )(z, xc, g_f, g_b, h_f, h_b, conv_w, w_rg, b_rg, w_ig, b_ig, lam)


def _tiles2d(s, d, tr, tcol):
    return (s // tr, d // tcol), pl.BlockSpec((tr, tcol), lambda i, j: (i, j))


def _zspec(tr, tcol, row_off, col_off):
    rb, cb = row_off // tr, col_off // tcol
    return pl.BlockSpec((tr, tcol), lambda i, j: (rb + i, cb + j))


def _rnn_gate_fwd(h_f, h_b, z, xg_off, n_ctx_rows, tr, tcol, side=None):
    t, d = h_f.shape
    s = t - n_ctx_rows
    grid, out = _tiles2d(s, d, tr, tcol)
    hs = _zspec(tr, tcol, n_ctx_rows, 0)

    def body(hf_ref, hb_ref, xg_ref, u_ref):
        u_ref[...] = ((hf_ref[...] + hb_ref[...]) * _gelu(xg_ref[...])).astype(BF16)

    return _call(body, side=side, name="rnn_gate_fwd", grid=grid,
                 in_specs=[hs, hs, _zspec(tr, tcol, n_ctx_rows, xg_off)], out_specs=out, out_shape=_sds((s, d), BF16),
                 sem=("parallel", "parallel"))(h_f, h_b, z)


def _rnn_gate_bwd(d_u, h_f, h_b, z, xg_off, n_ctx_rows, tr, tcol, side=None):
    t, d = h_f.shape
    s = t - n_ctx_rows
    grid, out = _tiles2d(s, d, tr, tcol)
    hs = _zspec(tr, tcol, n_ctx_rows, 0)

    def body(du_ref, hf_ref, hb_ref, xg_ref, dr_ref, dxg_ref):
        du = du_ref[...]
        xg = xg_ref[...]
        dr_ref[...] = du * _gelu(xg)
        dxg_ref[...] = (du * (hf_ref[...] + hb_ref[...]) * _gelu_grad(xg)).astype(BF16)

    return _call(body, side=side, name="rnn_gate_bwd", grid=grid,
                 in_specs=[out, hs, hs, _zspec(tr, tcol, n_ctx_rows, xg_off)], out_specs=[out, out],
                 out_shape=[_sds((s, d)), _sds((s, d), BF16)], sem=("parallel", "parallel"))(d_u, h_f, h_b, z)


def _merge_fwd(y_attn, y_rnn, z, gl_off, n_ctx_rows, tr, tcol):
    s, d = y_attn.shape
    grid, out = _tiles2d(s, d, tr, tcol)

    def body(ya_ref, yr_ref, ga_ref, gr_ref, o_ref):
        o_ref[...] = (jax.nn.sigmoid(ga_ref[...]) * ya_ref[...] + jax.nn.sigmoid(gr_ref[...]) * yr_ref[...]).astype(BF16)

    return pl.pallas_call(
        body, name="merge_fwd", grid=grid,
        in_specs=[out, out, _zspec(tr, tcol, n_ctx_rows, gl_off), _zspec(tr, tcol, n_ctx_rows, gl_off + d)],
        out_specs=out, out_shape=_sds((s, d), BF16), compiler_params=_params("parallel", "parallel"),
    )(y_attn, y_rnn, z, z)


def _merge_bwd(d_mrg, y_attn, y_rnn, z, gl_off, n_ctx_rows, tr, tcol, side=None):
    s, d = y_attn.shape
    grid, out = _tiles2d(s, d, tr, tcol)

    def body(dm_ref, ya_ref, yr_ref, ga_ref, gr_ref, dya_ref, dyr_ref, dga_ref, dgr_ref):
        dm = dm_ref[...]
        ga = jax.nn.sigmoid(ga_ref[...])
        gr = jax.nn.sigmoid(gr_ref[...])
        dya_ref[...] = (dm * ga).astype(BF16)
        dyr_ref[...] = (dm * gr).astype(BF16)
        dga_ref[...] = (dm * ya_ref[...] * ga * (1.0 - ga)).astype(BF16)
        dgr_ref[...] = (dm * yr_ref[...] * gr * (1.0 - gr)).astype(BF16)

    return _call(
        body, side=side, name="merge_bwd", grid=grid,
        in_specs=[out, out, out, _zspec(tr, tcol, n_ctx_rows, gl_off), _zspec(tr, tcol, n_ctx_rows, gl_off + d)],
        out_specs=[out] * 4, out_shape=[_sds((s, d), BF16)] * 4, sem=("parallel", "parallel"),
    )(d_mrg, y_attn, y_rnn, z, z)


def _sq_relu(up, tr, tcol, side=None):
    grid, out = _tiles2d(*up.shape, _tile(up.shape[0], 2 * tr, 16), _tile(up.shape[1], 4 * tcol, LANES))

    def body(u_ref, o_ref):
        r = jnp.maximum(u_ref[...], 0.0)
        o_ref[...] = (r * r).astype(BF16)

    return _call(body, side=side, name="sq_relu", grid=grid, in_specs=[out], out_specs=out,
                 out_shape=_sds(up.shape, BF16), sem=("parallel", "parallel"))(up)


def _sq_relu_bwd(d_act, up, tr, tcol, side=None):
    grid, out = _tiles2d(*up.shape, _tile(up.shape[0], 2 * tr, 16), _tile(up.shape[1], 4 * tcol, LANES))

    def body(d_ref, u_ref, o_ref):
        o_ref[...] = (d_ref[...] * 2.0 * jnp.maximum(u_ref[...], 0.0)).astype(BF16)

    return _call(body, side=side, name="sq_relu_bwd", grid=grid, in_specs=[out, out], out_specs=out,
                 out_shape=_sds(up.shape, BF16), sem=("parallel", "parallel"))(d_act, up)


def _cast_into_window(w, chip, col_sharded, name):
    r, c = w.shape
    tr, tcol = _tile(r, 512, 16), _tile(c, 1024, LANES)
    nrb, ncb = r // tr, c // tcol

    def body(chip_ref, w_ref, o_ref):
        o_ref[...] = w_ref[...].astype(BF16)

    if col_sharded:
        omap = lambda i, j, chip_ref: (i, chip_ref[0] * ncb + j)
    else:
        omap = lambda i, j, chip_ref: (chip_ref[0] * nrb + i, j)
    return pl.pallas_call(
        body, name=name,
        grid_spec=pltpu.PrefetchScalarGridSpec(
            num_scalar_prefetch=1, grid=(nrb, ncb),
            in_specs=[pl.BlockSpec((tr, tcol), lambda i, j, chip_ref: (i, j))], out_specs=pl.BlockSpec((tr, tcol), omap)),
        out_shape=_sds((r, c * N_CHIPS) if col_sharded else (r * N_CHIPS, c), BF16),
        compiler_params=_params("parallel", "parallel"),
    )(chip, w)


def _sum_leading(parts, name):
    n, r, c = parts.shape
    tr, tcol = _tile(r, 512, SUBLANES), _tile(c, 1024, LANES)

    def body(p_ref, o_ref):
        tot = p_ref[0]
        for k in range(1, n):
            tot = tot + p_ref[k]
        o_ref[...] = tot

    return pl.pallas_call(
        body, name=name, grid=(r // tr, c // tcol), in_specs=[pl.BlockSpec((n, tr, tcol), lambda i, j: (0, i, j))],
        out_specs=pl.BlockSpec((tr, tcol), lambda i, j: (i, j)), out_shape=_sds((r, c)),
        compiler_params=_params("parallel", "parallel"),
    )(parts)


def _add_half(full, other, core, split_rows, name):
    r, c = other.shape
    tr, tcol = _tile(r, 512, 16), _tile(c, 1024, LANES)
    nrb, ncb = r // tr, c // tcol

    def body(core_ref, f_ref, o_ref, out_ref):
        out_ref[...] = (f_ref[...].astype(F32) + o_ref[...].astype(F32)).astype(out_ref.dtype)

    if split_rows:
        fmap = lambda i, j, core_ref: (core_ref[0] * nrb + i, j)
    else:
        fmap = lambda i, j, core_ref: (i, core_ref[0] * ncb + j)
    same = lambda i, j, core_ref: (i, j)
    return pl.pallas_call(
        body, name=name,
        grid_spec=pltpu.PrefetchScalarGridSpec(
            num_scalar_prefetch=1, grid=(nrb, ncb),
            in_specs=[pl.BlockSpec((tr, tcol), fmap), pl.BlockSpec((tr, tcol), same)],
            out_specs=pl.BlockSpec((tr, tcol), same)),
        out_shape=_sds((r, c), BF16), compiler_params=_params("parallel", "parallel"),
    )(core, full, other)


def _sum_regions(pair, got, place, col_sharded, name):
    _, r, c = got.shape
    tr, tcol = _tile(r, 512, 16), _tile(c, 1024, LANES)
    nrb, ncb = r // tr, c // tcol

    def body(place_ref, p_ref, g_ref, out_ref):
        tot = p_ref[...].astype(F32)
        for k in range(N_CHIPS - 1):
            tot = tot + g_ref[k].astype(F32)
        out_ref[...] = tot

    if col_sharded:
        pmap = lambda i, j, pr: (i, pr[0] * ncb + j)
        omap = lambda i, j, pr: (pr[1] * nrb + i, j)
        out_shape = (2 * r, c)
    else:
        pmap = lambda i, j, pr: (pr[0] * nrb + i, j)
        omap = lambda i, j, pr: (i, pr[1] * ncb + j)
        out_shape = (r, 2 * c)
    return pl.pallas_call(
        body, name=name,
        grid_spec=pltpu.PrefetchScalarGridSpec(
            num_scalar_prefetch=1, grid=(nrb, ncb),
            in_specs=[pl.BlockSpec((tr, tcol), pmap), pl.BlockSpec((N_CHIPS - 1, tr, tcol), lambda i, j, pr: (0, i, j))],
            out_specs=pl.BlockSpec((tr, tcol), omap)),
        out_shape=_sds(out_shape), compiler_params=_params("parallel", "parallel"),
    )(place, pair, got)


def _adamw(w, g, m, v, name):
    r, c = w.shape
    tr, tcol = _tile(r, 512, SUBLANES), _tile(c, 1024, LANES)
    blk = pl.BlockSpec((tr, tcol), lambda i, j: (i, j))

    def body(w_ref, g_ref, m_ref, v_ref, d_ref, nm_ref, nv_ref, g_out_ref):
        g_ = g_ref[...]
        g_out_ref[...] = g_
        m_ = ADAM_B1 * m_ref[...] + (1.0 - ADAM_B1) * g_
        v_ = ADAM_B2 * v_ref[...] + (1.0 - ADAM_B2) * (g_ * g_)
        m_hat = m_ / (1.0 - ADAM_B1 ** ADAM_STEP)
        v_hat = v_ / (1.0 - ADAM_B2 ** ADAM_STEP)
        d_ref[...] = -ADAM_LR * (m_hat / (jnp.sqrt(v_hat) + ADAM_EPS) + ADAM_WD * w_ref[...])
        nm_ref[...] = m_
        nv_ref[...] = v_

    return _call(body, name=name, grid=(r // tr, c // tcol), in_specs=[blk] * 4, out_specs=[blk] * 4,
                 out_shape=[_sds((r, c))] * 4, sem=("parallel", "parallel"))(w, g, m, v)


def _place():
    x, y, c = lax.axis_index("x"), lax.axis_index("y"), lax.axis_index("c")
    chips = [(1 - x, y), (x, 1 - y), (1 - x, 1 - y)]
    return x, y, c, chips


def _all_gather8(blk, name):
    m, n = blk.shape

    def body(x_ref, out_ref, send_sems, recv_sems, local_sem):
        x, y, c, chips = _place()
        me, sibling = (x, y, c), (x, y, 1 - c)

        def rows(px, py, pc):
            return out_ref.at[pl.ds((4 * px + 2 * py + pc) * m, m), :]

        def copy(k, block, to, src=None):
            return pltpu.make_async_remote_copy(
                src_ref=rows(*block) if src is None else src, dst_ref=rows(*block), send_sem=send_sems.at[k],
                recv_sem=recv_sems.at[k], device_id=to, device_id_type=MESH_ID)

        mine = pltpu.make_async_copy(x_ref, rows(*me), local_sem)
        mine.start()
        first = [copy(0, me, sibling, src=x_ref)]
        first += [copy(1 + j, me, (*chip, c), src=x_ref) for j, chip in enumerate(chips)]
        for cp in first:
            cp.start()
        passed = [copy(4 + j, (*chip, c), sibling) for j, chip in enumerate(chips)]
        for j, chip in enumerate(chips):
            copy(1 + j, (*chip, c), me).wait_recv()
            passed[j].start()
        copy(0, sibling, me).wait_recv()
        for j, chip in enumerate(chips):
            copy(4 + j, (*chip, 1 - c), me).wait_recv()
        for cp in first + passed:
            cp.wait_send()
        mine.wait()

    return pl.pallas_call(
        body, name=name, out_shape=_sds((N_DEV * m, n), blk.dtype), in_specs=[ANY], out_specs=ANY,
        scratch_shapes=[pltpu.SemaphoreType.DMA((7,)), pltpu.SemaphoreType.DMA((7,)), pltpu.SemaphoreType.DMA],
    )(blk)


def _half(ref, core, split_rows):
    r, c = ref.shape
    if split_rows:
        return ref.at[pl.ds(core * (r // 2), r // 2), :]
    return ref.at[:, pl.ds(core * (c // 2), c // 2)]


def _chip_block(ref, j, col_sharded):
    r, c = ref.shape
    if col_sharded:
        return ref.at[:, pl.ds(j * (c // N_CHIPS), c // N_CHIPS)]
    return ref.at[pl.ds(j * (r // N_CHIPS), r // N_CHIPS), :]


def _rows_part(ref, part):
    lo, hi, n = part
    r = ref.shape[0]
    return ref if (lo, hi) == (0, n) else ref.at[pl.ds(lo * (r // n), (hi - lo) * (r // n)), :]


def _copy(send_sems, recv_sems, k, src, dst, to):
    return pltpu.make_async_remote_copy(src_ref=src, dst_ref=dst, send_sem=send_sems.at[k], recv_sem=recv_sems.at[k],
                                        device_id=to, device_id_type=MESH_ID)


def _in_place(arrays):
    return tuple(arrays), tuple(_sds(a.shape, a.dtype) for a in arrays), {i: i for i in range(len(arrays))}


def _gather_ici(fulls, col_sharded, part=(0, 1, 1)):
    nw = len(fulls)

    def build(_, refs, send_sems, recv_sems, sem0):
        x, y, c, chips = _place()
        sends, recvs = [], []
        for w in range(nw):
            win = lambda j: _rows_part(_half(_chip_block(refs[w], j, col_sharded[w]), c, True), part)
            for k, (cx, cy) in enumerate(chips):
                sem = sem0 + 3 * w + k
                sends.append(_copy(send_sems, recv_sems, sem, win(2 * x + y), win(2 * x + y), (cx, cy, c)))
                recvs.append(_copy(send_sems, recv_sems, sem, win(2 * cx + cy), win(2 * cx + cy), (cx, cy, c)))
        return sends, recvs

    return _Side(*_in_place(fulls), 3 * nw, build)


def _gather_d2d(fulls, col_sharded):
    nw = len(fulls)

    def build(_, refs, send_sems, recv_sems, sem0):
        x, y, c, chips = _place()
        sends, recvs = [], []
        for w in range(nw):
            win = lambda j, core: _half(_chip_block(refs[w], j, col_sharded[w]), core, True)
            for k, (cx, cy) in enumerate(chips):
                sem = sem0 + 3 * w + k
                sends.append(_copy(send_sems, recv_sems, sem, win(2 * cx + cy, c), win(2 * cx + cy, c), (x, y, 1 - c)))
                recvs.append(_copy(send_sems, recv_sems, sem, win(2 * cx + cy, 1 - c), win(2 * cx + cy, 1 - c),
                                   (x, y, 1 - c)))
        return sends, recvs

    return _Side(*_in_place(fulls), 3 * nw, build)


def _exchange(side, name):
    return _call(None, side=side, name=name)()[1]


def _swap_halves(grads, col_sharded):
    nw = len(grads)
    out_shapes = [_sds((g.shape[0] // 2, g.shape[1]) if col else (g.shape[0], g.shape[1] // 2), g.dtype)
                  for g, col in zip(grads, col_sharded)]

    def build(g_refs, o_refs, send_sems, recv_sems, sem0):
        x, y, c, _ = _place()
        copies = [_copy(send_sems, recv_sems, sem0 + w, _half(g_refs[w], 1 - c, col_sharded[w]), o_refs[w],
                        (x, y, 1 - c)) for w in range(nw)]
        return copies, copies

    return _Side(tuple(grads), tuple(out_shapes), {}, nw, build)


def _scatter_regions(pairs, col_sharded, part=(0, 1, 1), into=None):
    nw = len(pairs)

    def region_shape(p, col):
        return (p.shape[0], p.shape[1] // N_CHIPS) if col else (p.shape[0] // N_CHIPS, p.shape[1])

    out_shapes = tuple(_sds((N_CHIPS - 1, *region_shape(p, col)), p.dtype) for p, col in zip(pairs, col_sharded))

    def build(refs, o_refs, send_sems, recv_sems, sem0):
        x, y, c, chips = _place()
        copies = []
        for w in range(nw):
            for k, (cx, cy) in enumerate(chips):
                copies.append(_copy(
                    send_sems, recv_sems, sem0 + 3 * w + k,
                    _rows_part(_chip_block(refs[w], 2 * cx + cy, col_sharded[w]), part),
                    _rows_part(o_refs[w].at[k], part), (cx, cy, c)))
        return copies, copies

    if into is None:
        return _Side(tuple(pairs), out_shapes, {}, 3 * nw, build)
    return _Side((*pairs, *into), out_shapes, {nw + w: w for w in range(nw)}, 3 * nw, build)


def _join_halves(halves, col_sharded):
    nw = len(halves)

    def build(_, refs, send_sems, recv_sems, sem0):
        x, y, c, _ = _place()
        sends, recvs = [], []
        for w in range(nw):
            mine, theirs = _half(refs[w], c, col_sharded[w]), _half(refs[w], 1 - c, col_sharded[w])
            sends.append(_copy(send_sems, recv_sems, sem0 + w, mine, mine, (x, y, 1 - c)))
            recvs.append(_copy(send_sems, recv_sems, sem0 + w, theirs, theirs, (x, y, 1 - c)))
        return sends, recvs

    return _Side(*_in_place(halves), nw, build)


def _part_rows(size):
    return -(-size // (SUBLANES * LANES)) * SUBLANES


def _pack(arrays, pad_rows_to=SUBLANES):
    flat = [jnp.pad(a.reshape(-1), (0, _part_rows(a.size) * LANES - a.size)).reshape(-1, LANES) for a in arrays]
    rows = sum(f.shape[0] for f in flat)
    pad = (-rows) % pad_rows_to
    if pad:
        flat.append(jnp.zeros((pad, LANES), F32))
    return jnp.concatenate(flat, axis=0)


def _unpack(packed, shapes):
    out, r = [], 0
    for shp in shapes:
        size = math.prod(shp)
        out.append(packed[r:r + _part_rows(size)].reshape(-1)[:size].reshape(shp))
        r += _part_rows(size)
    return out


def _rope_tables(n_ctx_rows, s):
    rows = s // GRID_W
    row_idx = jnp.repeat(jnp.arange(rows), GRID_W)
    col_idx = jnp.tile(jnp.arange(GRID_W), rows)
    n_freq = LANES // 4
    inv_freq = ROPE_THETA ** (-jnp.arange(n_freq, dtype=F32) / n_freq)
    ang = jnp.concatenate([row_idx.astype(F32)[:, None] * inv_freq, col_idx.astype(F32)[:, None] * inv_freq], axis=-1)
    cos = jnp.repeat(jnp.cos(ang), 2, axis=-1)
    sin = jnp.repeat(jnp.sin(ang), 2, axis=-1) * jnp.tile(jnp.array([-1.0, 1.0], F32), LANES // 2)
    cos = jnp.concatenate([jnp.ones((n_ctx_rows, LANES), F32), cos], axis=0)
    sin = jnp.concatenate([jnp.zeros((n_ctx_rows, LANES), F32), sin], axis=0)
    return cos, sin


WEIGHT_NAMES = ['c_ctx', 'w_mod', 'b_mod', 'g_mix', 'g_mlp', 'w_in', 'q_gain', 'k_gain', 'conv_w', 'conv_b', 'w_rg',
                'b_rg', 'w_ig', 'b_ig', 'lru_lambda', 'w_o_attn', 'w_o_rnn', 'w_out', 'w_up', 'w_down', 'g_final']
BIG = ['w_in', 'w_o_attn', 'w_o_rnn', 'w_out', 'w_up', 'w_down']
BIG_COL_SHARDED = [True, False, False, False, True, False]
GATES = ['w_rg', 'w_ig']
SMALL = ['c_ctx', 'b_mod', 'g_mix', 'g_mlp', 'q_gain', 'k_gain', 'conv_b', 'g_final',
         'conv_w', 'b_rg', 'b_ig', 'lru_lambda']


def kernel(x, c, ctx, c_ctx, w_mod, b_mod, g_mix, g_mlp, w_in, q_gain, k_gain, conv_w, conv_b, w_rg, b_rg, w_ig, b_ig, lru_lambda, w_o_attn, w_o_rnn, w_out, w_up, w_down, g_final, loss_target, m_c_ctx, m_w_mod, m_b_mod, m_g_mix, m_g_mlp, m_w_in, m_q_gain, m_k_gain, m_conv_w, m_conv_b, m_w_rg, m_b_rg, m_w_ig, m_b_ig, m_lru_lambda, m_w_o_attn, m_w_o_rnn, m_w_out, m_w_up, m_w_down, m_g_final, v_c_ctx, v_w_mod, v_b_mod, v_g_mix, v_g_mlp, v_w_in, v_q_gain, v_k_gain, v_conv_w, v_conv_b, v_w_rg, v_b_rg, v_w_ig, v_b_ig, v_lru_lambda, v_w_o_attn, v_w_o_rnn, v_w_out, v_w_up, v_w_down, v_g_final):
    given = dict(locals())
    weights = {n: given[n] for n in WEIGHT_NAMES}
    moms = {n: given["m_" + n] for n in WEIGHT_NAMES}
    vars_ = {n: given["v_" + n] for n in WEIGHT_NAMES}

    s, d = x.shape[1], x.shape[2]
    n_ctx = ctx.shape[1]
    t = n_ctx + s
    hd = q_gain.shape[1]
    assert hd == LANES and w_rg.shape[-1] == LANES
    attn_w = w_o_attn.shape[1] * N_CHIPS
    n_in = w_in.shape[2] * N_CHIPS
    kv_w = (n_in - attn_w - 4 * d) // 2
    group = attn_w // kv_w
    k_off, v_off, xr_off = attn_w, attn_w + kv_w, attn_w + 2 * kv_w
    xg_off, gl_off = xr_off + d, xr_off + 2 * d
    d_mod = N_MOD * d
    tr = _tile(math.gcd(n_ctx, s), 256, 16)
    tcol = _tile(math.gcd(d, xr_off), 512, LANES)
    xi, yi, ci = lax.axis_index("x"), lax.axis_index("y"), lax.axis_index("c")
    chip = 2 * xi + yi
    core = ci.astype(jnp.int32).reshape(1)

    sharded_small = [conv_w[0], b_rg[0], b_ig[0], lru_lambda[0]]
    pack0 = _pack([c[0]] + sharded_small)
    got0 = _all_gather8(pack0, "gather_small_inputs").reshape(N_DEV, -1, LANES)
    c_all = got0[:, :_part_rows(d)].reshape(N_DEV, -1)[:, :d]
    per_chip = [_unpack(got0[2 * j, _part_rows(d):], [a.shape for a in sharded_small]) for j in range(N_CHIPS)]
    conv_w_f, b_rg_f, b_ig_f, lam_f = (jnp.concatenate([per_chip[j][i] for j in range(N_CHIPS)], axis=-1)
                                       for i in range(4))
    c16 = jnp.concatenate([c_all, c_ctx[None, :], jnp.zeros((16 - N_DEV - 1, d), F32)], axis=0)
    b_mod_shard = lax.dynamic_slice(b_mod, (0, chip * (d_mod // N_CHIPS)), (1, d_mod // N_CHIPS))
    mod_part, silu16 = _mod_fwd(c16, w_mod[0], b_mod_shard)
    mod_all = _all_gather8(mod_part, "gather_mod").reshape(N_DEV, 16, d_mod // N_CHIPS)
    mod16 = jnp.concatenate([mod_all[2 * j] for j in range(N_CHIPS)], axis=-1)
    me = 4 * xi + 2 * yi + ci
    mod_lat = lax.dynamic_slice(mod16, (me, 0), (1, d_mod)).reshape(N_MOD, d)
    mod_ctx = mod16[N_DEV].reshape(N_MOD, d)
    mod4 = jnp.stack([mod_ctx[0], mod_ctx[1], mod_lat[0], mod_lat[1]])
    mod3 = jnp.stack([mod_lat[2], mod_lat[3], mod_lat[4]])
    gate_f = mod_lat[5][None, :]

    chip_arr = chip.astype(jnp.int32).reshape(1)
    own = {n: _cast_into_window(weights[n][0], chip_arr, col, "cast_" + n) for n, col in zip(BIG, BIG_COL_SHARDED)}
    place = jnp.stack([chip, ci]).astype(jnp.int32)
    row3 = [False] * 3

    def pair_sum(n, full, other, col):
        return _add_half(full, other, core, col, "pair_sum_" + n)

    def chip_sum(n, pair, got, col):
        return _sum_regions(pair, got, place, col, "chip_sum_" + n)

    (w_in_f,) = _exchange(_gather_ici([own['w_in']], [True]), "gather_w_in_ici")
    (w_in_f,) = _exchange(_gather_d2d([w_in_f], [True]), "gather_w_in_d2d")
    xcat = jnp.concatenate([ctx[0], x[0]], axis=0)
    cos, sin = _rope_tables(n_ctx, s)
    h = _norm_mod_fwd(xcat, g_mix, mod4, n_ctx, tr)
    ici_up = lambda w, lo, hi: _gather_ici([w], [True], (lo, hi, 8))
    ici_down = lambda w, lo, hi: _gather_ici([w], [False], (lo, hi, 8))
    z, (w_oa_f, w_or_f, w_out_f) = _matmul(h, w_in_f, name="mm_in", side=_gather_ici(
        [own['w_o_attn'], own['w_o_rnn'], own['w_out']], row3))
    qr, (w_up_f, w_down_f) = _head_prep_fwd(z, 0, attn_w // LANES, q_gain, cos, sin, tr, "q_prep", side=_sides(
        ici_up(own['w_up'], 0, 1), ici_down(own['w_down'], 0, 1)))
    kr = _head_prep_fwd(z, k_off, kv_w // LANES, k_gain, cos, sin, tr, "k_prep")
    attn_o, (w_up_f, w_oa_f, w_or_f, w_out_f) = _attn_fwd(qr, kr, z, v_off, n_ctx, group, tr, side=_sides(
        ici_up(w_up_f, 1, 8), _gather_d2d([w_oa_f, w_or_f, w_out_f], row3)))
    (xc, a_f, bx_f, a_b, bx_b), (w_down_f, w_up_f) = _rnn_prep(
        z, xr_off, conv_w_f, conv_b, w_rg[0], b_rg_f, w_ig[0], b_ig_f, lam_f, n_ctx,
        side=_sides(ici_down(w_down_f, 1, 4), _gather_d2d([w_up_f], [True])))
    h_f = _scan(a_f, bx_f, order="ctx_lat_up", post=False, n_ctx_rows=n_ctx, name="scan_f")
    h_b = _scan(a_b, bx_b, order="ctx_lat_down", post=False, n_ctx_rows=n_ctx, name="scan_b")
    u = _rnn_gate_fwd(h_f, h_b, z, xg_off, n_ctx, tr, tcol)
    y_attn = _matmul(attn_o, w_oa_f, name="mm_o_attn")
    y_rnn = _matmul(u, w_or_f, name="mm_o_rnn")
    mrg = _merge_fwd(y_attn, y_rnn, z, gl_off, n_ctx, tr, tcol)
    mix = _matmul(mrg, w_out_f, name="mm_out")
    x1, h2 = _resid_norm_mod_fwd(x[0], mix, g_mlp, mod3, tr)
    up, (w_down_f,) = _matmul(h2, w_up_f, name="mm_up", side=ici_down(w_down_f, 4, 8))
    act, (w_down_f,) = _sq_relu(up, tr, tcol, side=_gather_d2d([w_down_f], [False]))
    down = _matmul(act, w_down_f, name="mm_down")
    dx2, d_down, sums_fin, loss_blk = _final_fwd_bwd(x1, down, loss_target[0], g_final[None, :], gate_f, tr)

    d_act = _matmul(d_down, w_down_f, tb=True, name="mm_d_act")
    g_w_down = _matmul(act, d_down, ta=True, out_dtype=BF16, name="mm_g_down")
    d_up, (got,) = _sq_relu_bwd(d_act, up, tr, tcol, side=_swap_halves([g_w_down], [False]))
    p_down = pair_sum('w_down', g_w_down, got, False)
    def scatter(pairs, cols, lo, hi, into=None):
        return _scatter_regions(pairs, cols, (lo, hi, 8), into)

    dh2, got_down = _matmul(d_up, w_up_f, tb=True, name="mm_d_h2", side=scatter([p_down], [False], 0, 3))
    g_w_up, got_down = _matmul(h2, d_up, ta=True, out_dtype=BF16, name="mm_g_up",
                               side=scatter([p_down], [False], 3, 6, got_down))
    (dx1, d_mix, sums2), (got,) = _norm_mod_bwd2(x1, dh2, dx2, mix, g_mlp, mod3, tr, side=_swap_halves([g_w_up], [True]))
    p_up = pair_sum('w_up', g_w_up, got, True)
    d_mrg = _matmul(d_mix, w_out_f, tb=True, name="mm_d_mrg")
    g_w_out = _matmul(mrg, d_mix, ta=True, out_dtype=BF16, name="mm_g_out")
    d_ya, d_yr, d_gla, d_glr = _merge_bwd(d_mrg, y_attn, y_rnn, z, gl_off, n_ctx, tr, tcol)
    d_o = _matmul(d_ya, w_oa_f, tb=True, out_dtype=BF16, name="mm_d_o")
    g_w_oa = _matmul(attn_o, d_ya, ta=True, out_dtype=BF16, name="mm_g_o_attn")
    d_u = _matmul(d_yr, w_or_f, tb=True, name="mm_d_u")
    g_w_or = _matmul(u, d_yr, ta=True, out_dtype=BF16, name="mm_g_o_rnn")
    d_rnn, d_xg = _rnn_gate_bwd(d_u, h_f, h_b, z, xg_off, n_ctx, tr, tcol)
    gs_f = _scan(a_f, d_rnn, order="lat_ctx_down", post=True, n_ctx_rows=n_ctx, name="scan_f_bwd")
    gs_b = _scan(a_b, d_rnn, order="lat_ctx_up", post=True, n_ctx_rows=n_ctx, name="scan_b_bwd")
    o_names, o_grads = ['w_o_attn', 'w_o_rnn', 'w_out'], [g_w_oa, g_w_or, g_w_out]
    (d_xr, g_w_rg, g_w_ig, sums_rnn), (got_down, *got_o) = _rnn_bwd(
        z, xr_off, xc, gs_f, gs_b, h_f, h_b, conv_w_f, w_rg[0], b_rg_f, w_ig[0], b_ig_f, lam_f, n_ctx,
        side=_sides(scatter([p_down], [False], 6, 8, got_down), _swap_halves(o_grads, row3)))
    hs_down = chip_sum('w_down', p_down, got_down, False)
    p_o = [pair_sum(n, g, o, False) for n, g, o in zip(o_names, o_grads, got_o)]
    gate_cols = 8 * LANES if g_w_rg.size % (8 * LANES * N_CHIPS * 16) == 0 else 2 * LANES
    gate_rows = g_w_rg.size // gate_cols
    gate_grads = [g_w_rg.reshape(gate_rows, gate_cols), g_w_ig.reshape(gate_rows, gate_cols)]
    (dq, dk, dv), (got_up, *got_o, got_rg, got_ig, gs_down) = _attn_bwd(
        qr, kr, z, v_off, d_o, n_ctx, group, tr, side=_sides(
            _scatter_regions([p_up], [True]), _scatter_regions(p_o, row3), _swap_halves(gate_grads, row3[:2]),
            _join_halves([hs_down], [False])))
    hs_up = chip_sum('w_up', p_up, got_up, True)
    hs_o = [chip_sum(n, p, o, False) for n, p, o in zip(o_names, p_o, got_o)]
    p_gate = [pair_sum(n, g, o, False) for n, g, o in zip(GATES, gate_grads, (got_rg, got_ig))]
    d_q_raw, g_q_gain = _head_prep_bwd(z, 0, attn_w // LANES, q_gain, cos, sin, dq, n_ctx, tr, "q_prep_bwd")
    d_k_raw, g_k_gain = _head_prep_bwd(z, k_off, kv_w // LANES, k_gain, cos, sin, dk, 0, tr, "k_prep_bwd")
    zero_ctx = lambda w: jnp.zeros((n_ctx, w), BF16)
    dz = jnp.concatenate([
        jnp.concatenate([zero_ctx(attn_w), d_q_raw], axis=0), d_k_raw, dv.astype(BF16), d_xr,
        jnp.concatenate([zero_ctx(d), d_xg], axis=0), jnp.concatenate([zero_ctx(d), d_gla], axis=0),
        jnp.concatenate([zero_ctx(d), d_glr], axis=0)], axis=1)
    g_w_in, (*got_gate, gs_up, gs_oa, gs_or, gs_out) = _matmul(
        h, dz, ta=True, out_dtype=BF16, name="mm_g_in", side=_sides(
            _scatter_regions(p_gate, row3[:2]), _join_halves([hs_up], [True]), _join_halves(hs_o, row3)))
    hs_gate = [chip_sum(n, p, o, False) for n, p, o in zip(GATES, p_gate, got_gate)]
    (got,) = _exchange(_swap_halves([g_w_in], [True]), "swap_w_in")
    p_in = pair_sum('w_in', g_w_in, got, True)
    dh, got_in = _matmul(dz, w_in_f, tb=True, name="mm_d_h", side=scatter([p_in], [True], 0, 4))
    (got_in,) = _exchange(scatter([p_in], [True], 4, 8, got_in), "scatter_w_in_rest")
    grad_x, sums1 = _norm_mod_bwd1(xcat, dh, dx1, g_mix, mod4, n_ctx, tr)

    zeros_d = jnp.zeros((d,), F32)
    dmod_lat = jnp.concatenate([sums1[0], sums1[1], sums2[3], sums2[0], sums2[1], sums_fin[1]])
    dmod_ctx = jnp.concatenate([sums1[3], sums1[4]] + [zeros_d] * 4)
    small_parts = [dmod_lat, dmod_ctx, loss_blk[0, 0:1], sums1[2] + sums1[5], sums2[2], g_q_gain[0], g_k_gain[0],
                   sums_rnn[10], sums_fin[0], sums_rnn[6:10], sums_rnn[0:2], sums_rnn[2:4], sums_rnn[4:6]]
    pack1 = _pack(small_parts)
    got1 = _all_gather8(pack1, "gather_small_grads").reshape(N_DEV, -1, LANES)
    tot1 = _sum_leading(got1, "sum_small_grads")
    part_shapes = [a.shape for a in small_parts]
    (s_dmod_lat, s_dmod_ctx, s_loss, g_g_mix, g_g_mlp, g_q_gain, g_k_gain, g_conv_b, g_g_final,
     g_conv_w_f, g_b_rg_f, g_b_ig_f, g_lam_f) = _unpack(tot1, part_shapes)
    loss = s_loss[0]
    g_b_mod = (s_dmod_lat + s_dmod_ctx)[None, :]
    n_mod_rows = _part_rows(d_mod)
    dmod16 = jnp.concatenate([got1[:, :n_mod_rows].reshape(N_DEV, -1)[:, :d_mod], s_dmod_ctx[None, :],
                              jnp.zeros((16 - N_DEV - 1, d_mod), F32)], axis=0)
    dmod16_shard = lax.dynamic_slice(dmod16, (0, chip * (d_mod // N_CHIPS)), (16, d_mod // N_CHIPS))
    g_w_mod = _matmul(silu16, dmod16_shard, ta=True, name="mm_g_mod")
    dsilu_part = _matmul(dmod16_shard[N_DEV:], w_mod[0], tb=True, name="mm_d_silu")
    dsilu_all = _all_gather8(dsilu_part, "gather_d_silu").reshape(N_DEV, 8, d)
    g_c_ctx = _c_ctx_grad(dsilu_all, c_ctx[None, :])[0]

    def shard_of(full):
        w = full.shape[-1] // N_CHIPS
        return lax.dynamic_slice(full, (0, chip * w), (full.shape[0], w))

    grads = {
        'c_ctx': g_c_ctx, 'b_mod': g_b_mod, 'g_mix': g_g_mix[None, :], 'g_mlp': g_g_mlp[None, :],
        'q_gain': g_q_gain[None, :], 'k_gain': g_k_gain[None, :], 'conv_b': g_conv_b[None, :],
        'g_final': g_g_final,
        'conv_w': shard_of(g_conv_w_f)[None], 'b_rg': shard_of(g_b_rg_f)[None], 'b_ig': shard_of(g_b_ig_f)[None],
        'lru_lambda': shard_of(g_lam_f)[None], 'w_mod': g_w_mod[None],
    }

    delta, new_m, new_v = {}, {}, {}

    def adamw(n):
        shp = weights[n].shape
        as2d = (lambda a: a[0]) if n not in GATES else (lambda a: a.reshape(-1, LANES))
        dl, nm, nv, g = _adamw(as2d(weights[n]), as2d(grads[n]), as2d(moms[n]), as2d(vars_[n]), "adamw_" + n)
        delta[n], new_m[n], new_v[n], grads[n] = dl.reshape(shp), nm.reshape(shp), nv.reshape(shp), g.reshape(shp)

    hs_in = chip_sum('w_in', p_in, got_in, True)
    gs_in, *gs_gate = _exchange(_join_halves([hs_in] + hs_gate, [True] + row3[:2]), "join_rest")
    for n, g in zip(['w_in', 'w_o_attn', 'w_o_rnn', 'w_out', 'w_up', 'w_down'], [gs_in, gs_oa, gs_or, gs_out, gs_up, gs_down]):
        grads[n] = g[None]
    gate_all = _all_gather8(jnp.concatenate(gs_gate, axis=0), "gather_gate_grads")
    gate_all = gate_all.reshape(N_CHIPS, 2, len(GATES), gate_rows // N_CHIPS, gate_cols)[:, 0]
    for i, n in enumerate(GATES):
        grads[n] = gate_all[:, i].reshape(weights[n].shape)
    for n in ['w_mod'] + BIG + GATES:
        adamw(n)
    small_shapes = [weights[n].shape for n in SMALL]
    packed = [_pack([src[n] for n in SMALL], 512) for src in (weights, grads, moms, vars_)]
    outs = _adamw(*packed, "adamw_small")
    for res, out in zip((delta, new_m, new_v), outs):
        for n, a in zip(SMALL, _unpack(out, small_shapes)):
            res[n] = a
    return (loss, grad_x[None], *[grads[n] for n in WEIGHT_NAMES], *[delta[n] for n in WEIGHT_NAMES],
            *[new_m[n] for n in WEIGHT_NAMES], *[new_v[n] for n in WEIGHT_NAMES])
```

```python
import functools
import math
from typing import Callable, NamedTuple

import jax
import jax.numpy as jnp
from jax import lax
from jax.experimental import pallas as pl
from jax.experimental.pallas import tpu as pltpu

F32 = jnp.float32
BF16 = jnp.bfloat16
MESH_ID = pl.DeviceIdType.MESH
ANY = pl.BlockSpec(memory_space=pl.ANY)

NORM_EPS = 1e-6
LRU_C = 8.0
GRID_W = 64
ROPE_THETA = 10000.0
N_MOD = 6
CONV_WIDTH = 4
ADAM_LR = 0.001
ADAM_B1 = 0.9
ADAM_B2 = 0.999
ADAM_EPS = 1e-08
ADAM_WD = 0.01
ADAM_STEP = 10

LANES = 128
SUBLANES = 8
V7X_VMEM_LIMIT = 48 * 1024 * 1024
N_CHIPS = 4
N_DEV = 8
GELU_C = math.sqrt(2.0 / math.pi)
GELU_A = 0.044715


def _tile(dim, pref, align):
    t = min(pref, dim)
    t -= t % align
    while t >= align:
        if dim % t == 0:
            return t
        t -= align
    return dim


def _params(*sem):
    return pltpu.CompilerParams(dimension_semantics=sem, vmem_limit_bytes=V7X_VMEM_LIMIT)


def _sds(shape, dtype=F32):
    return jax.ShapeDtypeStruct(shape, dtype)


class _Side(NamedTuple):
    operands: tuple
    results: tuple
    aliases: dict
    n_sems: int
    build: Callable


def _sides(*sides):
    ops, res, aliases, spans, n = [], [], {}, [], 0
    for s in sides:
        spans.append((len(ops), len(res), n))
        aliases.update({len(ops) + i: len(res) + j for i, j in s.aliases.items()})
        ops += s.operands
        res += s.results
        n += s.n_sems

    def build(op_refs, res_refs, send_sems, recv_sems, sem0):
        sends, recvs = [], []
        for s, (o, r, k) in zip(sides, spans):
            a, b = s.build(op_refs[o:o + len(s.operands)], res_refs[r:r + len(s.results)], send_sems, recv_sems,
                           sem0 + k)
            sends += a
            recvs += b
        return sends, recvs

    return _Side(tuple(ops), tuple(res), aliases, n, build)


def _call(body, *, side=None, sem=(), grid=(), in_specs=(), out_specs=(), out_shape=(), scratch_shapes=(), **kw):
    if side is None:
        return pl.pallas_call(body, grid=grid, in_specs=list(in_specs), out_specs=out_specs, out_shape=out_shape,
                              scratch_shapes=list(scratch_shapes), compiler_params=_params(*sem), **kw)
    many = isinstance(out_shape, (list, tuple))
    out_specs_l, out_shape_l = (list(out_specs), list(out_shape)) if many else ([out_specs], [out_shape])
    n_in, n_out, n_scr = len(in_specs), len(out_shape_l), len(scratch_shapes)
    n_op, n_res = len(side.operands), len(side.results)

    def hosted(*refs):
        ins, ops = refs[:n_in], refs[n_in:n_in + n_op]
        outs = refs[n_in + n_op:n_in + n_op + n_out]
        res = refs[n_in + n_op + n_out:n_in + n_op + n_out + n_res]
        scr = refs[n_in + n_op + n_out + n_res:-2]
        send_sems, recv_sems = refs[-2:]

        def start():
            for cp in side.build(ops, res, send_sems, recv_sems, 0)[0]:
                cp.start()

        def finish():
            sends, recvs = side.build(ops, res, send_sems, recv_sems, 0)
            for cp in recvs:
                cp.wait_recv()
            for cp in sends:
                cp.wait_send()

        if not grid:
            start()
            finish()
            return
        ids = [pl.program_id(a) for a in range(len(grid))]
        first = functools.reduce(jnp.logical_and, [i == 0 for i in ids])
        last = functools.reduce(jnp.logical_and, [i == g - 1 for i, g in zip(ids, grid)])
        pl.when(first)(start)
        body(*ins, *outs, *scr)
        pl.when(last)(finish)

    def run(*args):
        got = pl.pallas_call(
            hosted, grid=grid, in_specs=[*in_specs, *[ANY] * n_op], out_specs=[*out_specs_l, *[ANY] * n_res],
            out_shape=[*out_shape_l, *side.results],
            scratch_shapes=[*scratch_shapes, pltpu.SemaphoreType.DMA((side.n_sems,)),
                            pltpu.SemaphoreType.DMA((side.n_sems,))],
            input_output_aliases={n_in + i: n_out + j for i, j in side.aliases.items()},
            compiler_params=_params(*["arbitrary"] * len(grid)), **kw)(*args, *side.operands)
        own = list(got[:n_out]) if many else got[0]
        return own, list(got[n_out:])

    return run


def _matmul(a, b, *, ta=False, tb=False, out_dtype=F32, name, tm=1024, tn=1024, tk=2816, side=None, post=None,
            post_args=()):
    k_dim, m = a.shape if ta else a.shape[::-1]
    n, k2 = b.shape if tb else b.shape[::-1]
    assert k_dim == k2, (a.shape, b.shape, ta, tb)
    tm = _tile(m, tm, LANES if ta else 16)
    tn = _tile(n, tn, 16 if tb else LANES)
    tk = _tile(k_dim, tk, LANES)
    nk = k_dim // tk
    dims = (((0 if ta else 1,), (1 if tb else 0,)), ((), ()))
    if nk == 1:
        def whole(a_ref, b_ref, *rest):
            acc = lax.dot_general(a_ref[...].astype(BF16), b_ref[...].astype(BF16), dims, preferred_element_type=F32)
            if post is not None:
                acc = post(acc, *[r[...] for r in rest[:-1]])
            rest[-1][...] = acc.astype(rest[-1].dtype)

        a_spec = pl.BlockSpec((tk, tm), lambda i, j: (0, i)) if ta else pl.BlockSpec((tm, tk), lambda i, j: (i, 0))
        b_spec = pl.BlockSpec((tn, tk), lambda i, j: (j, 0)) if tb else pl.BlockSpec((tk, tn), lambda i, j: (0, j))
        o_spec = pl.BlockSpec((tm, tn), lambda i, j: (i, j))
        return _call(
            whole, side=side, name=name, grid=(m // tm, n // tn), in_specs=[a_spec, b_spec] + [o_spec] * len(post_args),
            out_specs=o_spec, out_shape=_sds((m, n), out_dtype), sem=("parallel", "parallel"),
        )(a, b, *post_args)
    assert post is None

    def body(a_ref, b_ref, o_ref, acc_ref):
        k = pl.program_id(2)

        @pl.when(k == 0)
        def _():
            acc_ref[...] = jnp.zeros_like(acc_ref)

        acc_ref[...] += lax.dot_general(a_ref[...].astype(BF16), b_ref[...].astype(BF16), dims,
                                        preferred_element_type=F32)

        @pl.when(k == nk - 1)
        def _():
            o_ref[...] = acc_ref[...].astype(o_ref.dtype)

    a_spec = pl.BlockSpec((tk, tm), lambda i, j, k: (k, i)) if ta else pl.BlockSpec((tm, tk), lambda i, j, k: (i, k))
    b_spec = pl.BlockSpec((tn, tk), lambda i, j, k: (j, k)) if tb else pl.BlockSpec((tk, tn), lambda i, j, k: (k, j))
    return _call(
        body, side=side, name=name, grid=(m // tm, n // tn, nk), in_specs=[a_spec, b_spec],
        out_specs=pl.BlockSpec((tm, tn), lambda i, j, k: (i, j)), out_shape=_sds((m, n), out_dtype),
        scratch_shapes=[pltpu.VMEM((tm, tn), F32)], sem=("parallel", "parallel", "arbitrary"),
    )(a, b)


def _silu(x):
    return x * jax.nn.sigmoid(x)


def _gelu(x):
    return 0.5 * x * (1.0 + jnp.tanh(GELU_C * (x + GELU_A * x * x * x)))


def _gelu_grad(x):
    t = jnp.tanh(GELU_C * (x + GELU_A * x * x * x))
    return 0.5 * (1.0 + t) + 0.5 * x * (1.0 - t * t) * GELU_C * (1.0 + 3.0 * GELU_A * x * x)


def _expm1_nonpos(x):
    series = x * (1.0 + x * (1.0 / 2 + x * (1.0 / 6 + x * (1.0 / 24 + x * (1.0 / 120 + x * (1.0 / 720 + x / 5040))))))
    return jnp.where(x > -0.25, series, jnp.exp(x) - 1.0)


def _softplus(x):
    return jnp.maximum(x, 0.0) + jnp.log1p(jnp.exp(-jnp.abs(x)))


def _rms_stats(x):
    return lax.rsqrt(jnp.mean(x * x, axis=-1, keepdims=True) + NORM_EPS)


def _rms_bwd(dxhat, xhat, rstd):
    return rstd * (dxhat - xhat * jnp.mean(dxhat * xhat, axis=-1, keepdims=True))


def _colsum(v):
    return jnp.sum(v, axis=0, keepdims=True)


def _mod_fwd(c16, w_mod, b_mod_shard):
    r, d = c16.shape
    n = w_mod.shape[1]
    tn = _tile(n, 512, LANES)

    def body(c_ref, w_ref, b_ref, o_ref, s_ref):
        s = _silu(c_ref[...])
        s_ref[...] = s
        o_ref[...] = jnp.dot(s.astype(BF16), w_ref[...].astype(BF16), preferred_element_type=F32) + b_ref[...]

    return pl.pallas_call(
        body, name="mod_fwd", grid=(n // tn,),
        in_specs=[pl.BlockSpec((r, d), lambda j: (0, 0)), pl.BlockSpec((d, tn), lambda j: (0, j)),
                  pl.BlockSpec((1, tn), lambda j: (0, j))],
        out_specs=[pl.BlockSpec((r, tn), lambda j: (0, j)), pl.BlockSpec((r, d), lambda j: (0, 0))],
        out_shape=[_sds((r, n)), _sds((r, d))], compiler_params=_params("arbitrary"),
    )(c16, w_mod, b_mod_shard)


def _c_ctx_grad(parts, c_ctx_row):
    d = c_ctx_row.shape[1]

    def body(p_ref, c_ref, o_ref):
        tot = p_ref[0, 0:1, :]
        for chip in range(1, N_CHIPS):
            tot = tot + p_ref[2 * chip, 0:1, :]
        c = c_ref[...]
        sg = jax.nn.sigmoid(c)
        o_ref[...] = tot * (sg * (1.0 + c * (1.0 - sg)))

    return pl.pallas_call(body, name="c_ctx_grad", out_shape=_sds((1, d)), compiler_params=_params())(parts, c_ctx_row)


def _norm_mod_fwd(xcat, g, mod4, n_ctx_rows, tr):
    t, d = xcat.shape
    nctx = n_ctx_rows // tr

    def body(x_ref, g_ref, mod_ref, h_ref):
        x = x_ref[...]
        n = x * _rms_stats(x) * g_ref[...]
        is_ctx = pl.program_id(0) < nctx
        sh = jnp.where(is_ctx, mod_ref[0:1, :], mod_ref[2:3, :])
        sc = jnp.where(is_ctx, mod_ref[1:2, :], mod_ref[3:4, :])
        h_ref[...] = (n * (1.0 + sc) + sh).astype(BF16)

    return pl.pallas_call(
        body, name="norm_mod_fwd", grid=(t // tr,),
        in_specs=[pl.BlockSpec((tr, d), lambda i: (i, 0)), pl.BlockSpec((1, d), lambda i: (0, 0)),
                  pl.BlockSpec((4, d), lambda i: (0, 0))],
        out_specs=pl.BlockSpec((tr, d), lambda i: (i, 0)), out_shape=_sds((t, d), BF16),
        compiler_params=_params("parallel"),
    )(xcat, g, mod4)


def _norm_mod_bwd1(xcat, dh, dx1, g, mod4, n_ctx_rows, tr, side=None):
    t, d = xcat.shape
    nctx = n_ctx_rows // tr
    s = t - n_ctx_rows

    def body(x_ref, dh_ref, dx1_ref, g_ref, mod_ref, dx_ref, sums_ref):
        i = pl.program_id(0)
        is_ctx = i < nctx

        @pl.when(i == 0)
        def _():
            sums_ref[...] = jnp.zeros_like(sums_ref)

        x = x_ref[...]
        dh_ = dh_ref[...]
        rstd = _rms_stats(x)
        xhat = x * rstd
        gg = g_ref[...]
        sc = jnp.where(is_ctx, mod_ref[1:2, :], mod_ref[3:4, :])
        dxhat = dh_ * (1.0 + sc) * gg
        dx_ref[...] = dx1_ref[...] + _rms_bwd(dxhat, xhat, rstd)
        part = [_colsum(dh_), _colsum(dh_ * xhat * gg), _colsum(dh_ * (1.0 + sc) * xhat)]

        @pl.when(is_ctx)
        def _():
            for k, row in enumerate(part):
                sums_ref[3 + k:4 + k, :] += row

        @pl.when(jnp.logical_not(is_ctx))
        def _():
            for k, row in enumerate(part):
                sums_ref[k:k + 1, :] += row

    lat = lambda i: (jnp.maximum(i - nctx, 0), 0)
    return _call(
        body, side=side, name="norm_mod_bwd1", grid=(t // tr,),
        in_specs=[pl.BlockSpec((tr, d), lambda i: (i, 0)), pl.BlockSpec((tr, d), lambda i: (i, 0)),
                  pl.BlockSpec((tr, d), lat), pl.BlockSpec((1, d), lambda i: (0, 0)),
                  pl.BlockSpec((4, d), lambda i: (0, 0))],
        out_specs=[pl.BlockSpec((tr, d), lat), pl.BlockSpec((8, d), lambda i: (0, 0))],
        out_shape=[_sds((s, d)), _sds((8, d))], sem=("arbitrary",),
    )(xcat, dh, dx1, g, mod4)


def _resid_norm_mod_fwd(x, mix, g, mod3, tr):
    s, d = x.shape

    def body(x_ref, mix_ref, g_ref, mod_ref, x1_ref, h_ref):
        x1 = x_ref[...] + mod_ref[0:1, :] * mix_ref[...]
        x1_ref[...] = x1
        n = x1 * _rms_stats(x1) * g_ref[...]
        h_ref[...] = (n * (1.0 + mod_ref[2:3, :]) + mod_ref[1:2, :]).astype(BF16)

    row = pl.BlockSpec((tr, d), lambda i: (i, 0))
    return pl.pallas_call(
        body, name="resid_norm_mod_fwd", grid=(s // tr,),
        in_specs=[row, row, pl.BlockSpec((1, d), lambda i: (0, 0)), pl.BlockSpec((3, d), lambda i: (0, 0))],
        out_specs=[row, row], out_shape=[_sds((s, d)), _sds((s, d), BF16)], compiler_params=_params("parallel"),
    )(x, mix, g, mod3)


def _norm_mod_bwd2(x1, dh2, dx2, mix, g, mod3, tr, side=None):
    s, d = x1.shape

    def body(x_ref, dh_ref, dx2_ref, mix_ref, g_ref, mod_ref, dx1_ref, dmix_ref, sums_ref):
        @pl.when(pl.program_id(0) == 0)
        def _():
            sums_ref[...] = jnp.zeros_like(sums_ref)

        x = x_ref[...]
        dh_ = dh_ref[...]
        rstd = _rms_stats(x)
        xhat = x * rstd
        gg = g_ref[...]
        sc = mod_ref[2:3, :]
        dx1 = dx2_ref[...] + _rms_bwd(dh_ * (1.0 + sc) * gg, xhat, rstd)
        dx1_ref[...] = dx1
        dmix_ref[...] = (dx1 * mod_ref[0:1, :]).astype(BF16)
        part = [_colsum(dh_), _colsum(dh_ * xhat * gg), _colsum(dh_ * (1.0 + sc) * xhat), _colsum(dx1 * mix_ref[...])]
        for k, row in enumerate(part):
            sums_ref[k:k + 1, :] += row

    row = pl.BlockSpec((tr, d), lambda i: (i, 0))
    return _call(
        body, side=side, name="norm_mod_bwd2", grid=(s // tr,),
        in_specs=[row, row, row, row, pl.BlockSpec((1, d), lambda i: (0, 0)), pl.BlockSpec((3, d), lambda i: (0, 0))],
        out_specs=[row, row, pl.BlockSpec((8, d), lambda i: (0, 0))],
        out_shape=[_sds((s, d)), _sds((s, d), BF16), _sds((8, d))], sem=("arbitrary",),
    )(x1, dh2, dx2, mix, g, mod3)


def _final_fwd_bwd(x1, down, target, g_final, gate, tr):
    s, d = x1.shape

    def body(x1_ref, down_ref, t_ref, g_ref, gate_ref, dx2_ref, ddown_ref, sums_ref, loss_ref):
        @pl.when(pl.program_id(0) == 0)
        def _():
            sums_ref[...] = jnp.zeros_like(sums_ref)
            loss_ref[...] = jnp.zeros_like(loss_ref)

        down_ = down_ref[...]
        gate_ = gate_ref[...]
        x2 = x1_ref[...] + gate_ * down_
        rstd = _rms_stats(x2)
        xhat = x2 * rstd
        gg = g_ref[...]
        err = xhat * gg - t_ref[...]
        loss_ref[...] += 0.5 * jnp.sum(jnp.mean(err * err, axis=-1, keepdims=True))
        dy = err * (1.0 / d)
        dx2 = _rms_bwd(dy * gg, xhat, rstd)
        dx2_ref[...] = dx2
        ddown_ref[...] = (dx2 * gate_).astype(BF16)
        sums_ref[0:1, :] += _colsum(dy * xhat)
        sums_ref[1:2, :] += _colsum(dx2 * down_)

    row = pl.BlockSpec((tr, d), lambda i: (i, 0))
    vec = pl.BlockSpec((1, d), lambda i: (0, 0))
    return pl.pallas_call(
        body, name="final_fwd_bwd", grid=(s // tr,), in_specs=[row, row, row, vec, vec],
        out_specs=[row, row, pl.BlockSpec((8, d), lambda i: (0, 0)), pl.BlockSpec((8, LANES), lambda i: (0, 0))],
        out_shape=[_sds((s, d)), _sds((s, d), BF16), _sds((8, d)), _sds((8, LANES))],
        compiler_params=_params("arbitrary"),
    )(x1, down, target, g_final, gate)


def _swap_pairs(v):
    lane = lax.broadcasted_iota(jnp.int32, v.shape, 1)
    return jnp.where(lane % 2 == 0, pltpu.roll(v, LANES - 1, 1), pltpu.roll(v, 1, 1))


def _head_prep_fwd(z, col_off, n_heads, gain, cos, sin, tr, name, side=None):
    t = z.shape[0]
    per = math.gcd(4, n_heads, col_off // LANES)
    w = per * LANES
    hb = col_off // w

    def body(z_ref, g_ref, cos_ref, sin_ref, o_ref):
        for hh in range(per):
            cols = slice(hh * LANES, (hh + 1) * LANES)
            x = z_ref[:, cols]
            y = x * _rms_stats(x) * g_ref[...]
            o_ref[:, cols] = (y * cos_ref[...] + _swap_pairs(y) * sin_ref[...]).astype(BF16)

    tab = pl.BlockSpec((tr, LANES), lambda i, j: (i, 0))
    return _call(
        body, side=side, name=name, grid=(t // tr, n_heads // per),
        in_specs=[pl.BlockSpec((tr, w), lambda i, j: (i, hb + j)), pl.BlockSpec((1, LANES), lambda i, j: (0, 0)),
                  tab, tab],
        out_specs=pl.BlockSpec((tr, w), lambda i, j: (i, j)), out_shape=_sds((t, n_heads * LANES), BF16),
        sem=("parallel", "parallel"),
    )(z, gain, cos, sin)


def _head_prep_bwd(z, col_off, n_heads, gain, cos, sin, dout, row_off, tr, name, side=None):
    r = dout.shape[0]
    per = math.gcd(4, n_heads, col_off // LANES)
    w = per * LANES
    hb = col_off // w
    rb = row_off // tr

    def body(z_ref, g_ref, cos_ref, sin_ref, d_ref, dz_ref, dg_ref):
        @pl.when(jnp.logical_and(pl.program_id(0) == 0, pl.program_id(1) == 0))
        def _():
            dg_ref[...] = jnp.zeros_like(dg_ref)

        for hh in range(per):
            cols = slice(hh * LANES, (hh + 1) * LANES)
            x = z_ref[:, cols]
            rstd = _rms_stats(x)
            xhat = x * rstd
            dd = d_ref[:, cols]
            dy = dd * cos_ref[...] - _swap_pairs(dd) * sin_ref[...]
            dg_ref[0:1, :] += _colsum(dy * xhat)
            dz_ref[:, cols] = _rms_bwd(dy * g_ref[...], xhat, rstd).astype(BF16)

    tab = pl.BlockSpec((tr, LANES), lambda i, j: (rb + i, 0))
    return _call(
        body, side=side, name=name, grid=(r // tr, n_heads // per),
        in_specs=[pl.BlockSpec((tr, w), lambda i, j: (rb + i, hb + j)),
                  pl.BlockSpec((1, LANES), lambda i, j: (0, 0)), tab, tab,
                  pl.BlockSpec((tr, w), lambda i, j: (i, j))],
        out_specs=[pl.BlockSpec((tr, w), lambda i, j: (i, j)), pl.BlockSpec((8, LANES), lambda i, j: (0, 0))],
        out_shape=[_sds((r, n_heads * LANES), BF16), _sds((8, LANES))], sem=("arbitrary", "arbitrary"),
    )(z, gain, cos, sin, dout)


def _attn_fwd(qr, kr, z, v_off, n_ctx_rows, group, tq, side=None):
    t, kvw = kr.shape
    s = t - n_ctx_rows
    n_kv = kvw // LANES
    scale = LANES ** -0.5
    qb0 = n_ctx_rows // tq
    vb = v_off // LANES

    def body(q_ref, k_ref, v_ref, o_ref, lse_ref):
        k = k_ref[...]
        v = v_ref[...].astype(BF16)
        lse_ref[...] = jnp.zeros_like(lse_ref)
        for g in range(group):
            cols = slice(g * LANES, (g + 1) * LANES)
            sc = lax.dot_general(q_ref[:, cols], k, (((1,), (1,)), ((), ())), preferred_element_type=F32) * scale
            m = jnp.max(sc, axis=-1, keepdims=True)
            e = jnp.exp(sc - m)
            l = jnp.sum(e, axis=-1, keepdims=True)
            p = e * (1.0 / l)
            o_ref[:, cols] = jnp.dot(p.astype(BF16), v, preferred_element_type=F32).astype(BF16)
            lse_ref[:, g:g + 1] = m + jnp.log(l)

    return _call(
        body, side=side, name="attn_fwd", grid=(n_kv, s // tq),
        in_specs=[pl.BlockSpec((tq, group * LANES), lambda h, i: (qb0 + i, h)),
                  pl.BlockSpec((t, LANES), lambda h, i: (0, h)), pl.BlockSpec((t, LANES), lambda h, i: (0, vb + h))],
        out_specs=[pl.BlockSpec((tq, group * LANES), lambda h, i: (i, h)), pl.BlockSpec((tq, LANES), lambda h, i: (i, h))],
        out_shape=[_sds((s, n_kv * group * LANES), BF16), _sds((s, kvw))], sem=("parallel", "parallel"),
    )(qr, kr, z)


def _attn_bwd(qr, kr, z, v_off, d_o, attn_o, lse, n_ctx_rows, group, tq, side=None):
    t, kvw = kr.shape
    s = t - n_ctx_rows
    n_kv = kvw // LANES
    scale = LANES ** -0.5
    qb0 = n_ctx_rows // tq
    vb = v_off // LANES
    n_q_blocks = s // tq
    tn_dims = (((0,), (0,)), ((), ()))
    nt_dims = (((1,), (1,)), ((), ()))

    def body(q_ref, k_ref, v_ref, do_ref, o_ref, lse_ref, dq_ref, dk_ref, dv_ref):
        @pl.when(pl.program_id(1) == 0)
        def _():
            dk_ref[...] = jnp.zeros_like(dk_ref)
            dv_ref[...] = jnp.zeros_like(dv_ref)

        k = k_ref[...]
        v = v_ref[...].astype(BF16)
        for g in range(group):
            cols = slice(g * LANES, (g + 1) * LANES)
            q = q_ref[:, cols]
            do_ = do_ref[:, cols]
            row_dot = jnp.sum(do_.astype(F32) * o_ref[:, cols].astype(F32), axis=-1, keepdims=True)
            sc = lax.dot_general(q, k, nt_dims, preferred_element_type=F32)
            p = jnp.exp(sc * scale - lse_ref[:, g:g + 1])
            dv_ref[...] += lax.dot_general(p.astype(BF16), do_, tn_dims, preferred_element_type=F32)
            dp = lax.dot_general(do_, v, nt_dims, preferred_element_type=F32)
            ds = (p * (dp - row_dot)).astype(BF16)
            dq_ref[:, cols] = jnp.dot(ds, k, preferred_element_type=F32) * scale
            dk_ref[...] += lax.dot_general(ds, q, tn_dims, preferred_element_type=F32)

        @pl.when(pl.program_id(1) == n_q_blocks - 1)
        def _():
            dk_ref[...] = dk_ref[...] * scale

    qspec = pl.BlockSpec((tq, group * LANES), lambda h, i: (qb0 + i, h))
    ospec = pl.BlockSpec((tq, group * LANES), lambda h, i: (i, h))
    kspec = pl.BlockSpec((t, LANES), lambda h, i: (0, h))
    return _call(
        body, side=side, name="attn_bwd", grid=(n_kv, n_q_blocks),
        in_specs=[qspec, kspec, pl.BlockSpec((t, LANES), lambda h, i: (0, vb + h)), ospec, ospec,
                  pl.BlockSpec((tq, LANES), lambda h, i: (i, h))],
        out_specs=[ospec, kspec, kspec],
        out_shape=[_sds((s, n_kv * group * LANES)), _sds((t, kvw)), _sds((t, kvw))], sem=("parallel", "arbitrary"),
    )(qr, kr, z, d_o, attn_o, lse)


def _row_mask(shape, rows):
    r = lax.broadcasted_iota(jnp.int32, shape, 0)
    m = r == rows[0]
    for v in rows[1:]:
        m = jnp.logical_or(m, r == v)
    return m


def _shift_rows(x, k, n_ctx_rows):
    t = x.shape[0]
    if k == 0:
        return x
    rolled = pltpu.roll(x, (-k) % t, 0)
    if k > 0:
        dead = [n_ctx_rows - 1 - i for i in range(k)] + [t - 1 - i for i in range(k)]
    else:
        dead = [i for i in range(-k)] + [n_ctx_rows + i for i in range(-k)]
    return jnp.where(_row_mask(x.shape, dead), 0.0, rolled)


def _conv(x, w, b, n_ctx_rows):
    y = b
    for k in range(CONV_WIDTH):
        y = y + _shift_rows(x, k - 1, n_ctx_rows) * w[k:k + 1, :]
    return y


def _gates(xc_bf, w_r, b_r, w_i, b_i, lam):
    r = jax.nn.sigmoid(jnp.dot(xc_bf, w_r.astype(BF16), preferred_element_type=F32) + b_r)
    i = jax.nn.sigmoid(jnp.dot(xc_bf, w_i.astype(BF16), preferred_element_type=F32) + b_i)
    log_a = -LRU_C * r * _softplus(-lam)
    a = jnp.exp(log_a)
    mult = jnp.sqrt(-_expm1_nonpos(2.0 * log_a))
    return r, i, a, mult


def _rnn_specs(t, xr_off):
    xb = xr_off // LANES
    return dict(
        zcol=pl.BlockSpec((t, LANES), lambda j: (0, xb + j)), col=pl.BlockSpec((t, LANES), lambda j: (0, j)),
        conv_w=pl.BlockSpec((CONV_WIDTH, LANES), lambda j: (0, j)), vec=pl.BlockSpec((1, LANES), lambda j: (0, j)),
        gate_w=pl.BlockSpec((2, 1, LANES, LANES), lambda j: (0, j, 0, 0)), two=pl.BlockSpec((2, LANES), lambda j: (0, j)))


def _rnn_prep(z, xr_off, conv_w, conv_b, w_rg, b_rg, w_ig, b_ig, lam, n_ctx_rows, side=None):
    t = z.shape[0]
    d = conv_b.shape[1]
    sp = _rnn_specs(t, xr_off)

    def body(z_ref, cw_ref, cb_ref, wr_ref, br_ref, wi_ref, bi_ref, lam_ref, xc_ref, af_ref, bf_ref, ab_ref, bb_ref):
        xc = _conv(z_ref[...], cw_ref[...], cb_ref[...], n_ctx_rows)
        xc_ref[...] = xc
        xc_bf = xc.astype(BF16)
        for dr, (a_ref, b_ref) in enumerate(((af_ref, bf_ref), (ab_ref, bb_ref))):
            _, i, a, mult = _gates(xc_bf, wr_ref[dr, 0], br_ref[dr:dr + 1, :], wi_ref[dr, 0], bi_ref[dr:dr + 1, :],
                                   lam_ref[dr:dr + 1, :])
            a_ref[...] = a
            b_ref[...] = mult * (i * xc)

    return _call(
        body, side=side, name="rnn_prep", grid=(d // LANES,),
        in_specs=[sp["zcol"], sp["conv_w"], sp["vec"], sp["gate_w"], sp["two"], sp["gate_w"], sp["two"], sp["two"]],
        out_specs=[sp["col"]] * 5, out_shape=[_sds((t, d))] * 5, sem=("parallel",),
    )(z, conv_w, conv_b, w_rg, b_rg, w_ig, b_ig, lam)


def _scan(a, b, *, order, post, n_ctx_rows, name, tc=256, side=None):
    t, d = a.shape
    tc = _tile(math.gcd(n_ctx_rows, t - n_ctx_rows), tc, SUBLANES)
    nt, nctx = t // tc, n_ctx_rows // tc
    nlat = nt - nctx
    b_lat_only = b.shape[0] != t
    up = order.endswith("up")

    def chunk(i):
        if order == "ctx_lat_up":
            return i
        if order == "lat_ctx_down":
            return nt - 1 - i
        if order == "ctx_lat_down":
            return jnp.where(i < nctx, nctx - 1 - i, nt - 1 - (i - nctx))
        return jnp.where(i < nlat, nctx + i, i - nlat)

    def body(a_ref, b_ref, o_ref, carry_ref):
        @pl.when(pl.program_id(0) == 0)
        def _():
            carry_ref[...] = jnp.zeros_like(carry_ref)

        live = jnp.where(chunk(pl.program_id(0)) >= nctx, 1.0, 0.0) if b_lat_only else None

        def group(gi, carry):
            base = pl.multiple_of((gi if up else tc // SUBLANES - 1 - gi) * SUBLANES, SUBLANES)
            for r in (range(SUBLANES) if up else range(SUBLANES - 1, -1, -1)):
                a_r = a_ref[pl.ds(base + r, 1), :]
                b_r = b_ref[pl.ds(base + r, 1), :]
                if live is not None:
                    b_r = b_r * live
                if post:
                    out = b_r + carry
                    carry = a_r * out
                else:
                    out = a_r * carry + b_r
                    carry = out
                o_ref[pl.ds(base + r, 1), :] = out
            return carry

        carry_ref[0:1, :] = lax.fori_loop(0, tc // SUBLANES, group, carry_ref[0:1, :])

    full = pl.BlockSpec((tc, d), lambda i: (chunk(i), 0))
    b_spec = pl.BlockSpec((tc, d), lambda i: (jnp.maximum(chunk(i) - nctx, 0), 0)) if b_lat_only else full
    return _call(
        body, side=side, name=name, grid=(nt,), in_specs=[full, b_spec], out_specs=full, out_shape=_sds((t, d)),
        scratch_shapes=[pltpu.VMEM((SUBLANES, d), F32)], sem=("arbitrary",),
    )(a, b)


def _rnn_bwd(z, xr_off, xc, g_f, g_b, h_f, h_b, conv_w, w_rg, b_rg, w_ig, b_ig, lam, n_ctx_rows, side=None):
    t, d = xc.shape
    sp = _rnn_specs(t, xr_off)
    tn_dims = (((0,), (0,)), ((), ()))
    nt_dims = (((1,), (1,)), ((), ()))

    def body(z_ref, xc_ref, gf_ref, gb_ref, hf_ref, hb_ref, cw_ref, wr_ref, br_ref, wi_ref, bi_ref, lam_ref,
             dxr_ref, dwr_ref, dwi_ref, sums_ref):
        xc_ = xc_ref[...]
        xc_bf = xc_.astype(BF16)
        dxc = jnp.zeros_like(xc_)
        sums = [None] * 6
        for dr, (g_ref, h_ref) in enumerate(((gf_ref, hf_ref), (gb_ref, hb_ref))):
            w_r, w_i, lam_ = wr_ref[dr, 0], wi_ref[dr, 0], lam_ref[dr:dr + 1, :]
            r, i, a, mult = _gates(xc_bf, w_r, br_ref[dr:dr + 1, :], w_i, bi_ref[dr:dr + 1, :], lam_)
            g = g_ref[...]
            h = h_ref[...]
            if dr == 0:
                h_prev = jnp.where(_row_mask(h.shape, [0]), 0.0, pltpu.roll(h, 1, 0))
            else:
                h_prev = jnp.where(_row_mask(h.shape, [n_ctx_rows - 1]), 0.0, pltpu.roll(h, t - 1, 0))
            d_mult = g * i * xc_
            d_i = g * mult * xc_
            dxc = dxc + g * mult * i
            d_log_a = g * h_prev * a - d_mult * a * a / mult
            sp_ = _softplus(-lam_)
            d_r = d_log_a * (-LRU_C) * sp_
            d_sp = _colsum(d_log_a * (-LRU_C) * r)
            du_r = (d_r * r * (1.0 - r))
            du_i = (d_i * i * (1.0 - i))
            sums[dr] = _colsum(du_r)
            sums[2 + dr] = _colsum(du_i)
            sums[4 + dr] = d_sp * (-jax.nn.sigmoid(-lam_))
            du_r_bf, du_i_bf = du_r.astype(BF16), du_i.astype(BF16)
            dwr_ref[dr, 0] = lax.dot_general(xc_bf, du_r_bf, tn_dims, preferred_element_type=F32).astype(BF16)
            dwi_ref[dr, 0] = lax.dot_general(xc_bf, du_i_bf, tn_dims, preferred_element_type=F32).astype(BF16)
            dxc = dxc + lax.dot_general(du_r_bf, w_r.astype(BF16), nt_dims, preferred_element_type=F32)
            dxc = dxc + lax.dot_general(du_i_bf, w_i.astype(BF16), nt_dims, preferred_element_type=F32)
        xr = z_ref[...]
        cw = cw_ref[...]
        dxr = jnp.zeros_like(dxc)
        rows = list(sums)
        for k in range(CONV_WIDTH):
            dxr = dxr + _shift_rows(dxc, 1 - k, n_ctx_rows) * cw[k:k + 1, :]
            rows.append(_colsum(dxc * _shift_rows(xr, k - 1, n_ctx_rows)))
        rows.append(_colsum(dxc))
        dxr_ref[...] = dxr.astype(BF16)
        sums_ref[...] = jnp.zeros_like(sums_ref)
        for k, row in enumerate(rows):
            sums_ref[k:k + 1, :] = row

    return _call(
        body, side=side, name="rnn_bwd", grid=(d // LANES,),
        in_specs=[sp["zcol"]] + [sp["col"]] * 5 + [sp["conv_w"], sp["gate_w"], sp["two"], sp["gate_w"], sp["two"],
                                                  sp["two"]],
        out_specs=[sp["col"], sp["gate_w"], sp["gate_w"], pl.BlockSpec((16, LANES), lambda j: (0, j))],
        out_shape=[_sds((t, d), BF16), _sds(w_rg.shape, BF16), _sds(w_ig.shape, BF16), _sds((16, d))],
        sem=("parallel",),
    )(z, xc, g_f, g_b, h_f, h_b, conv_w, w_rg, b_rg, w_ig, b_ig, lam)


def _tiles2d(s, d, tr, tcol):
    return (s // tr, d // tcol), pl.BlockSpec((tr, tcol), lambda i, j: (i, j))


def _zspec(tr, tcol, row_off, col_off):
    rb, cb = row_off // tr, col_off // tcol
    return pl.BlockSpec((tr, tcol), lambda i, j: (rb + i, cb + j))


def _rnn_gate_fwd(h_f, h_b, z, xg_off, n_ctx_rows, tr, tcol, side=None):
    t, d = h_f.shape
    s = t - n_ctx_rows
    grid, out = _tiles2d(s, d, tr, tcol)
    hs = _zspec(tr, tcol, n_ctx_rows, 0)

    def body(hf_ref, hb_ref, xg_ref, u_ref):
        u_ref[...] = ((hf_ref[...] + hb_ref[...]) * _gelu(xg_ref[...])).astype(BF16)

    return _call(body, side=side, name="rnn_gate_fwd", grid=grid,
                 in_specs=[hs, hs, _zspec(tr, tcol, n_ctx_rows, xg_off)], out_specs=out, out_shape=_sds((s, d), BF16),
                 sem=("parallel", "parallel"))(h_f, h_b, z)


def _rnn_gate_bwd(d_u, h_f, h_b, z, xg_off, n_ctx_rows, tr, tcol, side=None):
    t, d = h_f.shape
    s = t - n_ctx_rows
    grid, out = _tiles2d(s, d, tr, tcol)
    hs = _zspec(tr, tcol, n_ctx_rows, 0)

    def body(du_ref, hf_ref, hb_ref, xg_ref, dr_ref, dxg_ref):
        du = du_ref[...]
        xg = xg_ref[...]
        dr_ref[...] = du * _gelu(xg)
        dxg_ref[...] = (du * (hf_ref[...] + hb_ref[...]) * _gelu_grad(xg)).astype(BF16)

    return _call(body, side=side, name="rnn_gate_bwd", grid=grid,
                 in_specs=[out, hs, hs, _zspec(tr, tcol, n_ctx_rows, xg_off)], out_specs=[out, out],
                 out_shape=[_sds((s, d)), _sds((s, d), BF16)], sem=("parallel", "parallel"))(d_u, h_f, h_b, z)


def _merge_fwd(y_attn, y_rnn, z, gl_off, n_ctx_rows, tr, tcol):
    s, d = y_attn.shape
    grid, out = _tiles2d(s, d, tr, tcol)

    def body(ya_ref, yr_ref, ga_ref, gr_ref, o_ref):
        o_ref[...] = (jax.nn.sigmoid(ga_ref[...]) * ya_ref[...] + jax.nn.sigmoid(gr_ref[...]) * yr_ref[...]).astype(BF16)

    return pl.pallas_call(
        body, name="merge_fwd", grid=grid,
        in_specs=[out, out, _zspec(tr, tcol, n_ctx_rows, gl_off), _zspec(tr, tcol, n_ctx_rows, gl_off + d)],
        out_specs=out, out_shape=_sds((s, d), BF16), compiler_params=_params("parallel", "parallel"),
    )(y_attn, y_rnn, z, z)


def _merge_bwd(d_mrg, y_attn, y_rnn, z, gl_off, n_ctx_rows, tr, tcol, side=None):
    s, d = y_attn.shape
    grid, out = _tiles2d(s, d, tr, tcol)

    def body(dm_ref, ya_ref, yr_ref, ga_ref, gr_ref, dya_ref, dyr_ref, dga_ref, dgr_ref):
        dm = dm_ref[...]
        ga = jax.nn.sigmoid(ga_ref[...])
        gr = jax.nn.sigmoid(gr_ref[...])
        dya_ref[...] = (dm * ga).astype(BF16)
        dyr_ref[...] = (dm * gr).astype(BF16)
        dga_ref[...] = (dm * ya_ref[...] * ga * (1.0 - ga)).astype(BF16)
        dgr_ref[...] = (dm * yr_ref[...] * gr * (1.0 - gr)).astype(BF16)

    return _call(
        body, side=side, name="merge_bwd", grid=grid,
        in_specs=[out, out, out, _zspec(tr, tcol, n_ctx_rows, gl_off), _zspec(tr, tcol, n_ctx_rows, gl_off + d)],
        out_specs=[out] * 4, out_shape=[_sds((s, d), BF16)] * 4, sem=("parallel", "parallel"),
    )(d_mrg, y_attn, y_rnn, z, z)


def _sq_relu(up, tr, tcol, side=None):
    grid, out = _tiles2d(*up.shape, _tile(up.shape[0], 2 * tr, 16), _tile(up.shape[1], 4 * tcol, LANES))

    def body(u_ref, o_ref):
        r = jnp.maximum(u_ref[...], 0.0)
        o_ref[...] = (r * r).astype(BF16)

    return _call(body, side=side, name="sq_relu", grid=grid, in_specs=[out], out_specs=out,
                 out_shape=_sds(up.shape, BF16), sem=("parallel", "parallel"))(up)


def _cast_into_window(w, chip, col_sharded, name):
    r, c = w.shape
    tr, tcol = _tile(r, 512, 16), _tile(c, 1024, LANES)
    nrb, ncb = r // tr, c // tcol

    def body(chip_ref, w_ref, o_ref):
        o_ref[...] = w_ref[...].astype(BF16)

    if col_sharded:
        omap = lambda i, j, chip_ref: (i, chip_ref[0] * ncb + j)
    else:
        omap = lambda i, j, chip_ref: (chip_ref[0] * nrb + i, j)
    return pl.pallas_call(
        body, name=name,
        grid_spec=pltpu.PrefetchScalarGridSpec(
            num_scalar_prefetch=1, grid=(nrb, ncb),
            in_specs=[pl.BlockSpec((tr, tcol), lambda i, j, chip_ref: (i, j))], out_specs=pl.BlockSpec((tr, tcol), omap)),
        out_shape=_sds((r, c * N_CHIPS) if col_sharded else (r * N_CHIPS, c), BF16),
        compiler_params=_params("parallel", "parallel"),
    )(chip, w)


def _sum_leading(parts, name):
    n, r, c = parts.shape
    tr, tcol = _tile(r, 512, SUBLANES), _tile(c, 1024, LANES)

    def body(p_ref, o_ref):
        tot = p_ref[0]
        for k in range(1, n):
            tot = tot + p_ref[k]
        o_ref[...] = tot

    return pl.pallas_call(
        body, name=name, grid=(r // tr, c // tcol), in_specs=[pl.BlockSpec((n, tr, tcol), lambda i, j: (0, i, j))],
        out_specs=pl.BlockSpec((tr, tcol), lambda i, j: (i, j)), out_shape=_sds((r, c)),
        compiler_params=_params("parallel", "parallel"),
    )(parts)


def _add_half(full, other, core, split_rows, name):
    r, c = other.shape
    tr, tcol = _tile(r, 512, 16), _tile(c, 1024, LANES)
    nrb, ncb = r // tr, c // tcol

    def body(core_ref, f_ref, o_ref, out_ref):
        out_ref[...] = (f_ref[...].astype(F32) + o_ref[...].astype(F32)).astype(out_ref.dtype)

    if split_rows:
        fmap = lambda i, j, core_ref: (core_ref[0] * nrb + i, j)
    else:
        fmap = lambda i, j, core_ref: (i, core_ref[0] * ncb + j)
    same = lambda i, j, core_ref: (i, j)
    return pl.pallas_call(
        body, name=name,
        grid_spec=pltpu.PrefetchScalarGridSpec(
            num_scalar_prefetch=1, grid=(nrb, ncb),
            in_specs=[pl.BlockSpec((tr, tcol), fmap), pl.BlockSpec((tr, tcol), same)],
            out_specs=pl.BlockSpec((tr, tcol), same)),
        out_shape=_sds((r, c), BF16), compiler_params=_params("parallel", "parallel"),
    )(core, full, other)


def _sum_regions(pair, got, place, col_sharded, name):
    _, r, c = got.shape
    tr, tcol = _tile(r, 512, 16), _tile(c, 1024, LANES)
    nrb, ncb = r // tr, c // tcol

    def body(place_ref, p_ref, g_ref, out_ref):
        tot = p_ref[...].astype(F32)
        for k in range(N_CHIPS - 1):
            tot = tot + g_ref[k].astype(F32)
        out_ref[...] = tot

    if col_sharded:
        pmap = lambda i, j, pr: (i, pr[0] * ncb + j)
        omap = lambda i, j, pr: (pr[1] * nrb + i, j)
        out_shape = (2 * r, c)
    else:
        pmap = lambda i, j, pr: (pr[0] * nrb + i, j)
        omap = lambda i, j, pr: (i, pr[1] * ncb + j)
        out_shape = (r, 2 * c)
    return pl.pallas_call(
        body, name=name,
        grid_spec=pltpu.PrefetchScalarGridSpec(
            num_scalar_prefetch=1, grid=(nrb, ncb),
            in_specs=[pl.BlockSpec((tr, tcol), pmap), pl.BlockSpec((N_CHIPS - 1, tr, tcol), lambda i, j, pr: (0, i, j))],
            out_specs=pl.BlockSpec((tr, tcol), omap)),
        out_shape=_sds(out_shape), compiler_params=_params("parallel", "parallel"),
    )(place, pair, got)


def _adamw(w, g, m, v, name):
    r, c = w.shape
    tr, tcol = _tile(r, 512, SUBLANES), _tile(c, 1024, LANES)
    blk = pl.BlockSpec((tr, tcol), lambda i, j: (i, j))

    def body(w_ref, g_ref, m_ref, v_ref, d_ref, nm_ref, nv_ref, g_out_ref):
        g_ = g_ref[...]
        g_out_ref[...] = g_
        m_ = ADAM_B1 * m_ref[...] + (1.0 - ADAM_B1) * g_
        v_ = ADAM_B2 * v_ref[...] + (1.0 - ADAM_B2) * (g_ * g_)
        m_hat = m_ / (1.0 - ADAM_B1 ** ADAM_STEP)
        v_hat = v_ / (1.0 - ADAM_B2 ** ADAM_STEP)
        d_ref[...] = -ADAM_LR * (m_hat / (jnp.sqrt(v_hat) + ADAM_EPS) + ADAM_WD * w_ref[...])
        nm_ref[...] = m_
        nv_ref[...] = v_

    return _call(body, name=name, grid=(r // tr, c // tcol), in_specs=[blk] * 4, out_specs=[blk] * 4,
                 out_shape=[_sds((r, c))] * 4, sem=("parallel", "parallel"))(w, g, m, v)


def _place():
    x, y, c = lax.axis_index("x"), lax.axis_index("y"), lax.axis_index("c")
    chips = [(1 - x, y), (x, 1 - y), (1 - x, 1 - y)]
    return x, y, c, chips


def _all_gather8(blk, name):
    m, n = blk.shape

    def body(x_ref, out_ref, send_sems, recv_sems, local_sem):
        x, y, c, chips = _place()
        me, sibling = (x, y, c), (x, y, 1 - c)

        def rows(px, py, pc):
            return out_ref.at[pl.ds((4 * px + 2 * py + pc) * m, m), :]

        def copy(k, block, to, src=None):
            return pltpu.make_async_remote_copy(
                src_ref=rows(*block) if src is None else src, dst_ref=rows(*block), send_sem=send_sems.at[k],
                recv_sem=recv_sems.at[k], device_id=to, device_id_type=MESH_ID)

        mine = pltpu.make_async_copy(x_ref, rows(*me), local_sem)
        mine.start()
        first = [copy(0, me, sibling, src=x_ref)]
        first += [copy(1 + j, me, (*chip, c), src=x_ref) for j, chip in enumerate(chips)]
        for cp in first:
            cp.start()
        passed = [copy(4 + j, (*chip, c), sibling) for j, chip in enumerate(chips)]
        for j, chip in enumerate(chips):
            copy(1 + j, (*chip, c), me).wait_recv()
            passed[j].start()
        copy(0, sibling, me).wait_recv()
        for j, chip in enumerate(chips):
            copy(4 + j, (*chip, 1 - c), me).wait_recv()
        for cp in first + passed:
            cp.wait_send()
        mine.wait()

    return pl.pallas_call(
        body, name=name, out_shape=_sds((N_DEV * m, n), blk.dtype), in_specs=[ANY], out_specs=ANY,
        scratch_shapes=[pltpu.SemaphoreType.DMA((7,)), pltpu.SemaphoreType.DMA((7,)), pltpu.SemaphoreType.DMA],
    )(blk)


def _half(ref, core, split_rows):
    r, c = ref.shape
    if split_rows:
        return ref.at[pl.ds(core * (r // 2), r // 2), :]
    return ref.at[:, pl.ds(core * (c // 2), c // 2)]


def _chip_block(ref, j, col_sharded):
    r, c = ref.shape
    if col_sharded:
        return ref.at[:, pl.ds(j * (c // N_CHIPS), c // N_CHIPS)]
    return ref.at[pl.ds(j * (r // N_CHIPS), r // N_CHIPS), :]


def _rows_part(ref, part):
    lo, hi, n = part
    r = ref.shape[0]
    return ref if (lo, hi) == (0, n) else ref.at[pl.ds(lo * (r // n), (hi - lo) * (r // n)), :]


def _copy(send_sems, recv_sems, k, src, dst, to):
    return pltpu.make_async_remote_copy(src_ref=src, dst_ref=dst, send_sem=send_sems.at[k], recv_sem=recv_sems.at[k],
                                        device_id=to, device_id_type=MESH_ID)


def _in_place(arrays):
    return tuple(arrays), tuple(_sds(a.shape, a.dtype) for a in arrays), {i: i for i in range(len(arrays))}


def _gather_ici(fulls, col_sharded, part=(0, 1, 1)):
    nw = len(fulls)

    def build(_, refs, send_sems, recv_sems, sem0):
        x, y, c, chips = _place()
        sends, recvs = [], []
        for w in range(nw):
            win = lambda j: _rows_part(_half(_chip_block(refs[w], j, col_sharded[w]), c, True), part)
            for k, (cx, cy) in enumerate(chips):
                sem = sem0 + 3 * w + k
                sends.append(_copy(send_sems, recv_sems, sem, win(2 * x + y), win(2 * x + y), (cx, cy, c)))
                recvs.append(_copy(send_sems, recv_sems, sem, win(2 * cx + cy), win(2 * cx + cy), (cx, cy, c)))
        return sends, recvs

    return _Side(*_in_place(fulls), 3 * nw, build)


def _gather_d2d(fulls, col_sharded):
    nw = len(fulls)

    def build(_, refs, send_sems, recv_sems, sem0):
        x, y, c, chips = _place()
        sends, recvs = [], []
        for w in range(nw):
            win = lambda j, core: _half(_chip_block(refs[w], j, col_sharded[w]), core, True)
            for k, (cx, cy) in enumerate(chips):
                sem = sem0 + 3 * w + k
                sends.append(_copy(send_sems, recv_sems, sem, win(2 * cx + cy, c), win(2 * cx + cy, c), (x, y, 1 - c)))
                recvs.append(_copy(send_sems, recv_sems, sem, win(2 * cx + cy, 1 - c), win(2 * cx + cy, 1 - c),
                                   (x, y, 1 - c)))
        return sends, recvs

    return _Side(*_in_place(fulls), 3 * nw, build)


def _exchange(side, name):
    return _call(None, side=side, name=name)()[1]


def _swap_halves(grads, col_sharded):
    nw = len(grads)
    out_shapes = [_sds((g.shape[0] // 2, g.shape[1]) if col else (g.shape[0], g.shape[1] // 2), g.dtype)
                  for g, col in zip(grads, col_sharded)]

    def build(g_refs, o_refs, send_sems, recv_sems, sem0):
        x, y, c, _ = _place()
        copies = [_copy(send_sems, recv_sems, sem0 + w, _half(g_refs[w], 1 - c, col_sharded[w]), o_refs[w],
                        (x, y, 1 - c)) for w in range(nw)]
        return copies, copies

    return _Side(tuple(grads), tuple(out_shapes), {}, nw, build)


def _scatter_regions(pairs, col_sharded, part=(0, 1, 1), into=None):
    nw = len(pairs)

    def region_shape(p, col):
        return (p.shape[0], p.shape[1] // N_CHIPS) if col else (p.shape[0] // N_CHIPS, p.shape[1])

    out_shapes = tuple(_sds((N_CHIPS - 1, *region_shape(p, col)), p.dtype) for p, col in zip(pairs, col_sharded))

    def build(refs, o_refs, send_sems, recv_sems, sem0):
        x, y, c, chips = _place()
        copies = []
        for w in range(nw):
            for k, (cx, cy) in enumerate(chips):
                copies.append(_copy(
                    send_sems, recv_sems, sem0 + 3 * w + k,
                    _rows_part(_chip_block(refs[w], 2 * cx + cy, col_sharded[w]), part),
                    _rows_part(o_refs[w].at[k], part), (cx, cy, c)))
        return copies, copies

    if into is None:
        return _Side(tuple(pairs), out_shapes, {}, 3 * nw, build)
    return _Side((*pairs, *into), out_shapes, {nw + w: w for w in range(nw)}, 3 * nw, build)


def _join_halves(halves, col_sharded):
    nw = len(halves)

    def build(_, refs, send_sems, recv_sems, sem0):
        x, y, c, _ = _place()
        sends, recvs = [], []
        for w in range(nw):
            mine, theirs = _half(refs[w], c, col_sharded[w]), _half(refs[w], 1 - c, col_sharded[w])
            sends.append(_copy(send_sems, recv_sems, sem0 + w, mine, mine, (x, y, 1 - c)))
            recvs.append(_copy(send_sems, recv_sems, sem0 + w, theirs, theirs, (x, y, 1 - c)))
        return sends, recvs

    return _Side(*_in_place(halves), nw, build)


def _part_rows(size):
    return -(-size // (SUBLANES * LANES)) * SUBLANES


def _pack(arrays, pad_rows_to=SUBLANES):
    flat = [jnp.pad(a.reshape(-1), (0, _part_rows(a.size) * LANES - a.size)).reshape(-1, LANES) for a in arrays]
    rows = sum(f.shape[0] for f in flat)
    pad = (-rows) % pad_rows_to
    if pad:
        flat.append(jnp.zeros((pad, LANES), F32))
    return jnp.concatenate(flat, axis=0)


def _unpack(packed, shapes):
    out, r = [], 0
    for shp in shapes:
        size = math.prod(shp)
        out.append(packed[r:r + _part_rows(size)].reshape(-1)[:size].reshape(shp))
        r += _part_rows(size)
    return out


def _rope_tables(n_ctx_rows, s):
    rows = s // GRID_W
    row_idx = jnp.repeat(jnp.arange(rows), GRID_W)
    col_idx = jnp.tile(jnp.arange(GRID_W), rows)
    n_freq = LANES // 4
    inv_freq = ROPE_THETA ** (-jnp.arange(n_freq, dtype=F32) / n_freq)
    ang = jnp.concatenate([row_idx.astype(F32)[:, None] * inv_freq, col_idx.astype(F32)[:, None] * inv_freq], axis=-1)
    cos = jnp.repeat(jnp.cos(ang), 2, axis=-1)
    sin = jnp.repeat(jnp.sin(ang), 2, axis=-1) * jnp.tile(jnp.array([-1.0, 1.0], F32), LANES // 2)
    cos = jnp.concatenate([jnp.ones((n_ctx_rows, LANES), F32), cos], axis=0)
    sin = jnp.concatenate([jnp.zeros((n_ctx_rows, LANES), F32), sin], axis=0)
    return cos, sin


WEIGHT_NAMES = ['c_ctx', 'w_mod', 'b_mod', 'g_mix', 'g_mlp', 'w_in', 'q_gain', 'k_gain', 'conv_w', 'conv_b', 'w_rg',
                'b_rg', 'w_ig', 'b_ig', 'lru_lambda', 'w_o_attn', 'w_o_rnn', 'w_out', 'w_up', 'w_down', 'g_final']
BIG = ['w_in', 'w_o_attn', 'w_o_rnn', 'w_out', 'w_up', 'w_down']
BIG_COL_SHARDED = [True, False, False, False, True, False]
GATES = ['w_rg', 'w_ig']
SMALL = ['c_ctx', 'b_mod', 'g_mix', 'g_mlp', 'q_gain', 'k_gain', 'conv_b', 'g_final',
         'conv_w', 'b_rg', 'b_ig', 'lru_lambda']


def kernel(x, c, ctx, c_ctx, w_mod, b_mod, g_mix, g_mlp, w_in, q_gain, k_gain, conv_w, conv_b, w_rg, b_rg, w_ig, b_ig, lru_lambda, w_o_attn, w_o_rnn, w_out, w_up, w_down, g_final, loss_target, m_c_ctx, m_w_mod, m_b_mod, m_g_mix, m_g_mlp, m_w_in, m_q_gain, m_k_gain, m_conv_w, m_conv_b, m_w_rg, m_b_rg, m_w_ig, m_b_ig, m_lru_lambda, m_w_o_attn, m_w_o_rnn, m_w_out, m_w_up, m_w_down, m_g_final, v_c_ctx, v_w_mod, v_b_mod, v_g_mix, v_g_mlp, v_w_in, v_q_gain, v_k_gain, v_conv_w, v_conv_b, v_w_rg, v_b_rg, v_w_ig, v_b_ig, v_lru_lambda, v_w_o_attn, v_w_o_rnn, v_w_out, v_w_up, v_w_down, v_g_final):
    given = dict(locals())
    weights = {n: given[n] for n in WEIGHT_NAMES}
    moms = {n: given["m_" + n] for n in WEIGHT_NAMES}
    vars_ = {n: given["v_" + n] for n in WEIGHT_NAMES}

    s, d = x.shape[1], x.shape[2]
    n_ctx = ctx.shape[1]
    t = n_ctx + s
    hd = q_gain.shape[1]
    assert hd == LANES and w_rg.shape[-1] == LANES
    attn_w = w_o_attn.shape[1] * N_CHIPS
    n_in = w_in.shape[2] * N_CHIPS
    kv_w = (n_in - attn_w - 4 * d) // 2
    group = attn_w // kv_w
    k_off, v_off, xr_off = attn_w, attn_w + kv_w, attn_w + 2 * kv_w
    xg_off, gl_off = xr_off + d, xr_off + 2 * d
    d_mod = N_MOD * d
    tr = _tile(math.gcd(n_ctx, s), 256, 16)
    tcol = _tile(math.gcd(d, xr_off), 512, LANES)
    xi, yi, ci = lax.axis_index("x"), lax.axis_index("y"), lax.axis_index("c")
    chip = 2 * xi + yi
    core = ci.astype(jnp.int32).reshape(1)

    sharded_small = [conv_w[0], b_rg[0], b_ig[0], lru_lambda[0]]
    pack0 = _pack([c[0]] + sharded_small)
    got0 = _all_gather8(pack0, "gather_small_inputs").reshape(N_DEV, -1, LANES)
    c_all = got0[:, :_part_rows(d)].reshape(N_DEV, -1)[:, :d]
    per_chip = [_unpack(got0[2 * j, _part_rows(d):], [a.shape for a in sharded_small]) for j in range(N_CHIPS)]
    conv_w_f, b_rg_f, b_ig_f, lam_f = (jnp.concatenate([per_chip[j][i] for j in range(N_CHIPS)], axis=-1)
                                       for i in range(4))
    c16 = jnp.concatenate([c_all, c_ctx[None, :], jnp.zeros((16 - N_DEV - 1, d), F32)], axis=0)
    b_mod_shard = lax.dynamic_slice(b_mod, (0, chip * (d_mod // N_CHIPS)), (1, d_mod // N_CHIPS))
    mod_part, silu16 = _mod_fwd(c16, w_mod[0], b_mod_shard)
    mod_all = _all_gather8(mod_part, "gather_mod").reshape(N_DEV, 16, d_mod // N_CHIPS)
    mod16 = jnp.concatenate([mod_all[2 * j] for j in range(N_CHIPS)], axis=-1)
    me = 4 * xi + 2 * yi + ci
    mod_lat = lax.dynamic_slice(mod16, (me, 0), (1, d_mod)).reshape(N_MOD, d)
    mod_ctx = mod16[N_DEV].reshape(N_MOD, d)
    mod4 = jnp.stack([mod_ctx[0], mod_ctx[1], mod_lat[0], mod_lat[1]])
    mod3 = jnp.stack([mod_lat[2], mod_lat[3], mod_lat[4]])
    gate_f = mod_lat[5][None, :]

    chip_arr = chip.astype(jnp.int32).reshape(1)
    own = {n: _cast_into_window(weights[n][0], chip_arr, col, "cast_" + n) for n, col in zip(BIG, BIG_COL_SHARDED)}
    place = jnp.stack([chip, ci]).astype(jnp.int32)
    row3 = [False] * 3

    def pair_sum(n, full, other, col):
        return _add_half(full, other, core, col, "pair_sum_" + n)

    def chip_sum(n, pair, got, col):
        return _sum_regions(pair, got, place, col, "chip_sum_" + n)

    (w_in_f,) = _exchange(_gather_ici([own['w_in']], [True]), "gather_w_in_ici")
    (w_in_f,) = _exchange(_gather_d2d([w_in_f], [True]), "gather_w_in_d2d")
    xcat = jnp.concatenate([ctx[0], x[0]], axis=0)
    cos, sin = _rope_tables(n_ctx, s)
    h = _norm_mod_fwd(xcat, g_mix, mod4, n_ctx, tr)
    ici_up = lambda w, lo, hi: _gather_ici([w], [True], (lo, hi, 8))
    ici_down = lambda w, lo, hi: _gather_ici([w], [False], (lo, hi, 8))
    z, (w_oa_f, w_or_f, w_out_f) = _matmul(h, w_in_f, name="mm_in", side=_gather_ici(
        [own['w_o_attn'], own['w_o_rnn'], own['w_out']], row3))
    qr, (w_up_f, w_down_f) = _head_prep_fwd(z, 0, attn_w // LANES, q_gain, cos, sin, tr, "q_prep", side=_sides(
        ici_up(own['w_up'], 0, 1), ici_down(own['w_down'], 0, 1)))
    kr = _head_prep_fwd(z, k_off, kv_w // LANES, k_gain, cos, sin, tr, "k_prep")
    (attn_o, lse), (w_up_f, w_oa_f, w_or_f, w_out_f) = _attn_fwd(qr, kr, z, v_off, n_ctx, group, tr, side=_sides(
        ici_up(w_up_f, 1, 8), _gather_d2d([w_oa_f, w_or_f, w_out_f], row3)))
    (xc, a_f, bx_f, a_b, bx_b), (w_down_f, w_up_f) = _rnn_prep(
        z, xr_off, conv_w_f, conv_b, w_rg[0], b_rg_f, w_ig[0], b_ig_f, lam_f, n_ctx,
        side=_sides(ici_down(w_down_f, 1, 4), _gather_d2d([w_up_f], [True])))
    h_f = _scan(a_f, bx_f, order="ctx_lat_up", post=False, n_ctx_rows=n_ctx, name="scan_f")
    h_b = _scan(a_b, bx_b, order="ctx_lat_down", post=False, n_ctx_rows=n_ctx, name="scan_b")
    u = _rnn_gate_fwd(h_f, h_b, z, xg_off, n_ctx, tr, tcol)
    y_attn = _matmul(attn_o, w_oa_f, name="mm_o_attn")
    y_rnn = _matmul(u, w_or_f, name="mm_o_rnn")
    mrg = _merge_fwd(y_attn, y_rnn, z, gl_off, n_ctx, tr, tcol)
    mix = _matmul(mrg, w_out_f, name="mm_out")
    x1, h2 = _resid_norm_mod_fwd(x[0], mix, g_mlp, mod3, tr)
    up, (w_down_f,) = _matmul(h2, w_up_f, name="mm_up", side=ici_down(w_down_f, 4, 8))
    act, (w_down_f,) = _sq_relu(up, tr, tcol, side=_gather_d2d([w_down_f], [False]))
    down = _matmul(act, w_down_f, name="mm_down")
    dx2, d_down, sums_fin, loss_blk = _final_fwd_bwd(x1, down, loss_target[0], g_final[None, :], gate_f, tr)

    d_up = _matmul(d_down, w_down_f, tb=True, out_dtype=BF16, name="mm_d_up",
                   post=lambda d_act, up_: d_act * 2.0 * jnp.maximum(up_, 0.0), post_args=(up,))
    g_w_down = _matmul(act, d_down, ta=True, out_dtype=BF16, name="mm_g_down")

    def scatter(pairs, cols, lo, hi, into=None):
        return _scatter_regions(pairs, cols, (lo, hi, 8), into)

    dh2, (got,) = _matmul(d_up, w_up_f, tb=True, name="mm_d_h2", side=_swap_halves([g_w_down], [False]))
    p_down = pair_sum('w_down', g_w_down, got, False)
    g_w_up, got_down = _matmul(h2, d_up, ta=True, out_dtype=BF16, name="mm_g_up", side=scatter([p_down], [False], 0, 4))
    (dx1, d_mix, sums2), (got,) = _norm_mod_bwd2(x1, dh2, dx2, mix, g_mlp, mod3, tr, side=_swap_halves([g_w_up], [True]))
    p_up = pair_sum('w_up', g_w_up, got, True)
    d_mrg = _matmul(d_mix, w_out_f, tb=True, name="mm_d_mrg")
    g_w_out = _matmul(mrg, d_mix, ta=True, out_dtype=BF16, name="mm_g_out")
    d_ya, d_yr, d_gla, d_glr = _merge_bwd(d_mrg, y_attn, y_rnn, z, gl_off, n_ctx, tr, tcol)
    d_o = _matmul(d_ya, w_oa_f, tb=True, out_dtype=BF16, name="mm_d_o")
    g_w_oa = _matmul(attn_o, d_ya, ta=True, out_dtype=BF16, name="mm_g_o_attn")
    d_u = _matmul(d_yr, w_or_f, tb=True, name="mm_d_u")
    g_w_or = _matmul(u, d_yr, ta=True, out_dtype=BF16, name="mm_g_o_rnn")
    d_rnn, d_xg = _rnn_gate_bwd(d_u, h_f, h_b, z, xg_off, n_ctx, tr, tcol)
    gs_f = _scan(a_f, d_rnn, order="lat_ctx_down", post=True, n_ctx_rows=n_ctx, name="scan_f_bwd")
    gs_b = _scan(a_b, d_rnn, order="lat_ctx_up", post=True, n_ctx_rows=n_ctx, name="scan_b_bwd")
    o_names, o_grads = ['w_o_attn', 'w_o_rnn', 'w_out'], [g_w_oa, g_w_or, g_w_out]
    (d_xr, g_w_rg, g_w_ig, sums_rnn), (got_down, *got_o) = _rnn_bwd(
        z, xr_off, xc, gs_f, gs_b, h_f, h_b, conv_w_f, w_rg[0], b_rg_f, w_ig[0], b_ig_f, lam_f, n_ctx,
        side=_sides(scatter([p_down], [False], 4, 8, got_down), _swap_halves(o_grads, row3)))
    hs_down = chip_sum('w_down', p_down, got_down, False)
    p_o = [pair_sum(n, g, o, False) for n, g, o in zip(o_names, o_grads, got_o)]
    gate_cols = 8 * LANES if g_w_rg.size % (8 * LANES * N_CHIPS * 16) == 0 else 2 * LANES
    gate_rows = g_w_rg.size // gate_cols
    gate_grads = [g_w_rg.reshape(gate_rows, gate_cols), g_w_ig.reshape(gate_rows, gate_cols)]
    (dq, dk, dv), (got_up, *got_o, got_rg, got_ig, gs_down) = _attn_bwd(
        qr, kr, z, v_off, d_o, attn_o, lse, n_ctx, group, tr, side=_sides(
            _scatter_regions([p_up], [True]), _scatter_regions(p_o, row3), _swap_halves(gate_grads, row3[:2]),
            _join_halves([hs_down], [False])))
    hs_up = chip_sum('w_up', p_up, got_up, True)
    hs_o = [chip_sum(n, p, o, False) for n, p, o in zip(o_names, p_o, got_o)]
    p_gate = [pair_sum(n, g, o, False) for n, g, o in zip(GATES, gate_grads, (got_rg, got_ig))]
    d_q_raw, g_q_gain = _head_prep_bwd(z, 0, attn_w // LANES, q_gain, cos, sin, dq, n_ctx, tr, "q_prep_bwd")
    d_k_raw, g_k_gain = _head_prep_bwd(z, k_off, kv_w // LANES, k_gain, cos, sin, dk, 0, tr, "k_prep_bwd")
    zero_ctx = lambda w: jnp.zeros((n_ctx, w), BF16)
    dz = jnp.concatenate([
        jnp.concatenate([zero_ctx(attn_w), d_q_raw], axis=0), d_k_raw, dv.astype(BF16), d_xr,
        jnp.concatenate([zero_ctx(d), d_xg], axis=0), jnp.concatenate([zero_ctx(d), d_gla], axis=0),
        jnp.concatenate([zero_ctx(d), d_glr], axis=0)], axis=1)
    g_w_in, (*got_gate, gs_up, gs_oa, gs_or, gs_out) = _matmul(
        h, dz, ta=True, out_dtype=BF16, name="mm_g_in", side=_sides(
            _scatter_regions(p_gate, row3[:2]), _join_halves([hs_up], [True]), _join_halves(hs_o, row3)))
    hs_gate = [chip_sum(n, p, o, False) for n, p, o in zip(GATES, p_gate, got_gate)]
    (got,) = _exchange(_swap_halves([g_w_in], [True]), "swap_w_in")
    p_in = pair_sum('w_in', g_w_in, got, True)
    dh, got_in = _matmul(dz, w_in_f, tb=True, name="mm_d_h", side=scatter([p_in], [True], 0, 4))
    (got_in,) = _exchange(scatter([p_in], [True], 4, 8, got_in), "scatter_w_in_rest")
    grad_x, sums1 = _norm_mod_bwd1(xcat, dh, dx1, g_mix, mod4, n_ctx, tr)

    zeros_d = jnp.zeros((d,), F32)
    dmod_lat = jnp.concatenate([sums1[0], sums1[1], sums2[3], sums2[0], sums2[1], sums_fin[1]])
    dmod_ctx = jnp.concatenate([sums1[3], sums1[4]] + [zeros_d] * 4)
    small_parts = [dmod_lat, dmod_ctx, loss_blk[0, 0:1], sums1[2] + sums1[5], sums2[2], g_q_gain[0], g_k_gain[0],
                   sums_rnn[10], sums_fin[0], sums_rnn[6:10], sums_rnn[0:2], sums_rnn[2:4], sums_rnn[4:6]]
    pack1 = _pack(small_parts)
    got1 = _all_gather8(pack1, "gather_small_grads").reshape(N_DEV, -1, LANES)
    tot1 = _sum_leading(got1, "sum_small_grads")
    part_shapes = [a.shape for a in small_parts]
    (s_dmod_lat, s_dmod_ctx, s_loss, g_g_mix, g_g_mlp, g_q_gain, g_k_gain, g_conv_b, g_g_final,
     g_conv_w_f, g_b_rg_f, g_b_ig_f, g_lam_f) = _unpack(tot1, part_shapes)
    loss = s_loss[0]
    g_b_mod = (s_dmod_lat + s_dmod_ctx)[None, :]
    n_mod_rows = _part_rows(d_mod)
    dmod16 = jnp.concatenate([got1[:, :n_mod_rows].reshape(N_DEV, -1)[:, :d_mod], s_dmod_ctx[None, :],
                              jnp.zeros((16 - N_DEV - 1, d_mod), F32)], axis=0)
    dmod16_shard = lax.dynamic_slice(dmod16, (0, chip * (d_mod // N_CHIPS)), (16, d_mod // N_CHIPS))
    g_w_mod = _matmul(silu16, dmod16_shard, ta=True, name="mm_g_mod")
    dsilu_part = _matmul(dmod16_shard[N_DEV:], w_mod[0], tb=True, name="mm_d_silu")
    dsilu_all = _all_gather8(dsilu_part, "gather_d_silu").reshape(N_DEV, 8, d)
    g_c_ctx = _c_ctx_grad(dsilu_all, c_ctx[None, :])[0]

    def shard_of(full):
        w = full.shape[-1] // N_CHIPS
        return lax.dynamic_slice(full, (0, chip * w), (full.shape[0], w))

    grads = {
        'c_ctx': g_c_ctx, 'b_mod': g_b_mod, 'g_mix': g_g_mix[None, :], 'g_mlp': g_g_mlp[None, :],
        'q_gain': g_q_gain[None, :], 'k_gain': g_k_gain[None, :], 'conv_b': g_conv_b[None, :],
        'g_final': g_g_final,
        'conv_w': shard_of(g_conv_w_f)[None], 'b_rg': shard_of(g_b_rg_f)[None], 'b_ig': shard_of(g_b_ig_f)[None],
        'lru_lambda': shard_of(g_lam_f)[None], 'w_mod': g_w_mod[None],
    }

    delta, new_m, new_v = {}, {}, {}

    def adamw(n):
        shp = weights[n].shape
        as2d = (lambda a: a[0]) if n not in GATES else (lambda a: a.reshape(-1, LANES))
        dl, nm, nv, g = _adamw(as2d(weights[n]), as2d(grads[n]), as2d(moms[n]), as2d(vars_[n]), "adamw_" + n)
        delta[n], new_m[n], new_v[n], grads[n] = dl.reshape(shp), nm.reshape(shp), nv.reshape(shp), g.reshape(shp)

    hs_in = chip_sum('w_in', p_in, got_in, True)
    gs_in, *gs_gate = _exchange(_join_halves([hs_in] + hs_gate, [True] + row3[:2]), "join_rest")
    for n, g in zip(['w_in', 'w_o_attn', 'w_o_rnn', 'w_out', 'w_up', 'w_down'], [gs_in, gs_oa, gs_or, gs_out, gs_up, gs_down]):
        grads[n] = g[None]
    gate_all = _all_gather8(jnp.concatenate(gs_gate, axis=0), "gather_gate_grads")
    gate_all = gate_all.reshape(N_CHIPS, 2, len(GATES), gate_rows // N_CHIPS, gate_cols)[:, 0]
    for i, n in enumerate(GATES):
        grads[n] = gate_all[:, i].reshape(weights[n].shape)
    for n in ['w_mod'] + BIG + GATES:
        adamw(n)
    small_shapes = [weights[n].shape for n in SMALL]
    packed = [_pack([src[n] for n in SMALL], 512) for src in (weights, grads, moms, vars_)]
    outs = _adamw(*packed, "adamw_small")
    for res, out in zip((delta, new_m, new_v), outs):
        for n, a in zip(SMALL, _unpack(out, small_shapes)):
            res[n] = a
    return (loss, grad_x[None], *[grads[n] for n in WEIGHT_NAMES], *[delta[n] for n in WEIGHT_NAMES],
            *[new_m[n] for n in WEIGHT_NAMES], *[new_v[n] for n in WEIGHT_NAMES])
```

```python
import functools
import math
from typing import Callable, NamedTuple

import jax
import jax.numpy as jnp
from jax import lax
from jax.experimental import pallas as pl
from jax.experimental.pallas import tpu as pltpu

F32 = jnp.float32
BF16 = jnp.bfloat16
MESH_ID = pl.DeviceIdType.MESH
ANY = pl.BlockSpec(memory_space=pl.ANY)

NORM_EPS = 1e-6
LRU_C = 8.0
GRID_W = 64
ROPE_THETA = 10000.0
N_MOD = 6
CONV_WIDTH = 4
ADAM_LR = 0.001
ADAM_B1 = 0.9
ADAM_B2 = 0.999
ADAM_EPS = 1e-08
ADAM_WD = 0.01
ADAM_STEP = 10

LANES = 128
SUBLANES = 8
V7X_VMEM_LIMIT = 48 * 1024 * 1024
N_CHIPS = 4
N_DEV = 8
GELU_C = math.sqrt(2.0 / math.pi)
GELU_A = 0.044715


def _tile(dim, pref, align):
    t = min(pref, dim)
    t -= t % align
    while t >= align:
        if dim % t == 0:
            return t
        t -= align
    return dim


def _params(*sem):
    return pltpu.CompilerParams(dimension_semantics=sem, vmem_limit_bytes=V7X_VMEM_LIMIT)


def _sds(shape, dtype=F32):
    return jax.ShapeDtypeStruct(shape, dtype)


class _Side(NamedTuple):
    operands: tuple
    results: tuple
    aliases: dict
    n_sems: int
    build: Callable


def _sides(*sides):
    ops, res, aliases, spans, n = [], [], {}, [], 0
    for s in sides:
        spans.append((len(ops), len(res), n))
        aliases.update({len(ops) + i: len(res) + j for i, j in s.aliases.items()})
        ops += s.operands
        res += s.results
        n += s.n_sems

    def build(op_refs, res_refs, send_sems, recv_sems, sem0):
        sends, recvs = [], []
        for s, (o, r, k) in zip(sides, spans):
            a, b = s.build(op_refs[o:o + len(s.operands)], res_refs[r:r + len(s.results)], send_sems, recv_sems,
                           sem0 + k)
            sends += a
            recvs += b
        return sends, recvs

    return _Side(tuple(ops), tuple(res), aliases, n, build)


def _call(body, *, side=None, sem=(), grid=(), in_specs=(), out_specs=(), out_shape=(), scratch_shapes=(), **kw):
    if side is None:
        return pl.pallas_call(body, grid=grid, in_specs=list(in_specs), out_specs=out_specs, out_shape=out_shape,
                              scratch_shapes=list(scratch_shapes), compiler_params=_params(*sem), **kw)
    many = isinstance(out_shape, (list, tuple))
    out_specs_l, out_shape_l = (list(out_specs), list(out_shape)) if many else ([out_specs], [out_shape])
    n_in, n_out, n_scr = len(in_specs), len(out_shape_l), len(scratch_shapes)
    n_op, n_res = len(side.operands), len(side.results)

    def hosted(*refs):
        ins, ops = refs[:n_in], refs[n_in:n_in + n_op]
        outs = refs[n_in + n_op:n_in + n_op + n_out]
        res = refs[n_in + n_op + n_out:n_in + n_op + n_out + n_res]
        scr = refs[n_in + n_op + n_out + n_res:-2]
        send_sems, recv_sems = refs[-2:]

        def start():
            for cp in side.build(ops, res, send_sems, recv_sems, 0)[0]:
                cp.start()

        def finish():
            sends, recvs = side.build(ops, res, send_sems, recv_sems, 0)
            for cp in recvs:
                cp.wait_recv()
            for cp in sends:
                cp.wait_send()

        if not grid:
            start()
            finish()
            return
        ids = [pl.program_id(a) for a in range(len(grid))]
        first = functools.reduce(jnp.logical_and, [i == 0 for i in ids])
        last = functools.reduce(jnp.logical_and, [i == g - 1 for i, g in zip(ids, grid)])
        pl.when(first)(start)
        body(*ins, *outs, *scr)
        pl.when(last)(finish)

    def run(*args):
        got = pl.pallas_call(
            hosted, grid=grid, in_specs=[*in_specs, *[ANY] * n_op], out_specs=[*out_specs_l, *[ANY] * n_res],
            out_shape=[*out_shape_l, *side.results],
            scratch_shapes=[*scratch_shapes, pltpu.SemaphoreType.DMA((side.n_sems,)),
                            pltpu.SemaphoreType.DMA((side.n_sems,))],
            input_output_aliases={n_in + i: n_out + j for i, j in side.aliases.items()},
            compiler_params=_params(*["arbitrary"] * len(grid)), **kw)(*args, *side.operands)
        own = list(got[:n_out]) if many else got[0]
        return own, list(got[n_out:])

    return run


def _matmul(a, b, *, ta=False, tb=False, out_dtype=F32, name, tm=1024, tn=1024, tk=2816, side=None, post=None,
            post_args=()):
    k_dim, m = a.shape if ta else a.shape[::-1]
    n, k2 = b.shape if tb else b.shape[::-1]
    assert k_dim == k2, (a.shape, b.shape, ta, tb)
    tm = _tile(m, tm, LANES if ta else 16)
    tn = _tile(n, tn, 16 if tb else LANES)
    tk = _tile(k_dim, tk, LANES)
    nk = k_dim // tk
    dims = (((0 if ta else 1,), (1 if tb else 0,)), ((), ()))
    if nk == 1:
        def whole(a_ref, b_ref, *rest):
            acc = lax.dot_general(a_ref[...].astype(BF16), b_ref[...].astype(BF16), dims, preferred_element_type=F32)
            if post is not None:
                acc = post(acc, *[r[...] for r in rest[:-1]])
            rest[-1][...] = acc.astype(rest[-1].dtype)

        a_spec = pl.BlockSpec((tk, tm), lambda i, j: (0, i)) if ta else pl.BlockSpec((tm, tk), lambda i, j: (i, 0))
        b_spec = pl.BlockSpec((tn, tk), lambda i, j: (j, 0)) if tb else pl.BlockSpec((tk, tn), lambda i, j: (0, j))
        o_spec = pl.BlockSpec((tm, tn), lambda i, j: (i, j))
        return _call(
            whole, side=side, name=name, grid=(m // tm, n // tn), in_specs=[a_spec, b_spec] + [o_spec] * len(post_args),
            out_specs=o_spec, out_shape=_sds((m, n), out_dtype), sem=("parallel", "parallel"),
        )(a, b, *post_args)
    assert post is None

    def body(a_ref, b_ref, o_ref, acc_ref):
        k = pl.program_id(2)

        @pl.when(k == 0)
        def _():
            acc_ref[...] = jnp.zeros_like(acc_ref)

        acc_ref[...] += lax.dot_general(a_ref[...].astype(BF16), b_ref[...].astype(BF16), dims,
                                        preferred_element_type=F32)

        @pl.when(k == nk - 1)
        def _():
            o_ref[...] = acc_ref[...].astype(o_ref.dtype)

    a_spec = pl.BlockSpec((tk, tm), lambda i, j, k: (k, i)) if ta else pl.BlockSpec((tm, tk), lambda i, j, k: (i, k))
    b_spec = pl.BlockSpec((tn, tk), lambda i, j, k: (j, k)) if tb else pl.BlockSpec((tk, tn), lambda i, j, k: (k, j))
    return _call(
        body, side=side, name=name, grid=(m // tm, n // tn, nk), in_specs=[a_spec, b_spec],
        out_specs=pl.BlockSpec((tm, tn), lambda i, j, k: (i, j)), out_shape=_sds((m, n), out_dtype),
        scratch_shapes=[pltpu.VMEM((tm, tn), F32)], sem=("parallel", "parallel", "arbitrary"),
    )(a, b)


def _silu(x):
    return x * jax.nn.sigmoid(x)


def _gelu(x):
    return 0.5 * x * (1.0 + jnp.tanh(GELU_C * (x + GELU_A * x * x * x)))


def _gelu_grad(x):
    t = jnp.tanh(GELU_C * (x + GELU_A * x * x * x))
    return 0.5 * (1.0 + t) + 0.5 * x * (1.0 - t * t) * GELU_C * (1.0 + 3.0 * GELU_A * x * x)


def _expm1_nonpos(x):
    series = x * (1.0 + x * (1.0 / 2 + x * (1.0 / 6 + x * (1.0 / 24 + x * (1.0 / 120 + x * (1.0 / 720 + x / 5040))))))
    return jnp.where(x > -0.25, series, jnp.exp(x) - 1.0)


def _softplus(x):
    return jnp.maximum(x, 0.0) + jnp.log1p(jnp.exp(-jnp.abs(x)))


def _rms_stats(x):
    return lax.rsqrt(jnp.mean(x * x, axis=-1, keepdims=True) + NORM_EPS)


def _rms_bwd(dxhat, xhat, rstd):
    return rstd * (dxhat - xhat * jnp.mean(dxhat * xhat, axis=-1, keepdims=True))


def _colsum(v):
    return jnp.sum(v, axis=0, keepdims=True)


def _mod_fwd(c16, w_mod, b_mod_shard):
    r, d = c16.shape
    n = w_mod.shape[1]
    tn = _tile(n, 512, LANES)

    def body(c_ref, w_ref, b_ref, o_ref, s_ref):
        s = _silu(c_ref[...])
        s_ref[...] = s
        o_ref[...] = jnp.dot(s.astype(BF16), w_ref[...].astype(BF16), preferred_element_type=F32) + b_ref[...]

    return pl.pallas_call(
        body, name="mod_fwd", grid=(n // tn,),
        in_specs=[pl.BlockSpec((r, d), lambda j: (0, 0)), pl.BlockSpec((d, tn), lambda j: (0, j)),
                  pl.BlockSpec((1, tn), lambda j: (0, j))],
        out_specs=[pl.BlockSpec((r, tn), lambda j: (0, j)), pl.BlockSpec((r, d), lambda j: (0, 0))],
        out_shape=[_sds((r, n)), _sds((r, d))], compiler_params=_params("arbitrary"),
    )(c16, w_mod, b_mod_shard)


def _c_ctx_grad(parts, c_ctx_row):
    d = c_ctx_row.shape[1]

    def body(p_ref, c_ref, o_ref):
        tot = p_ref[0, 0:1, :]
        for chip in range(1, N_CHIPS):
            tot = tot + p_ref[2 * chip, 0:1, :]
        c = c_ref[...]
        sg = jax.nn.sigmoid(c)
        o_ref[...] = tot * (sg * (1.0 + c * (1.0 - sg)))

    return pl.pallas_call(body, name="c_ctx_grad", out_shape=_sds((1, d)), compiler_params=_params())(parts, c_ctx_row)


def _norm_mod_fwd(xcat, g, mod4, n_ctx_rows, tr):
    t, d = xcat.shape
    nctx = n_ctx_rows // tr

    def body(x_ref, g_ref, mod_ref, h_ref):
        x = x_ref[...]
        n = x * _rms_stats(x) * g_ref[...]
        is_ctx = pl.program_id(0) < nctx
        sh = jnp.where(is_ctx, mod_ref[0:1, :], mod_ref[2:3, :])
        sc = jnp.where(is_ctx, mod_ref[1:2, :], mod_ref[3:4, :])
        h_ref[...] = (n * (1.0 + sc) + sh).astype(BF16)

    return pl.pallas_call(
        body, name="norm_mod_fwd", grid=(t // tr,),
        in_specs=[pl.BlockSpec((tr, d), lambda i: (i, 0)), pl.BlockSpec((1, d), lambda i: (0, 0)),
                  pl.BlockSpec((4, d), lambda i: (0, 0))],
        out_specs=pl.BlockSpec((tr, d), lambda i: (i, 0)), out_shape=_sds((t, d), BF16),
        compiler_params=_params("parallel"),
    )(xcat, g, mod4)


def _norm_mod_bwd1(xcat, dh, dx1, g, mod4, n_ctx_rows, tr, side=None):
    t, d = xcat.shape
    nctx = n_ctx_rows // tr
    s = t - n_ctx_rows

    def body(x_ref, dh_ref, dx1_ref, g_ref, mod_ref, dx_ref, sums_ref):
        i = pl.program_id(0)
        is_ctx = i < nctx

        @pl.when(i == 0)
        def _():
            sums_ref[...] = jnp.zeros_like(sums_ref)

        x = x_ref[...]
        dh_ = dh_ref[...]
        rstd = _rms_stats(x)
        xhat = x * rstd
        gg = g_ref[...]
        sc = jnp.where(is_ctx, mod_ref[1:2, :], mod_ref[3:4, :])
        dxhat = dh_ * (1.0 + sc) * gg
        dx_ref[...] = dx1_ref[...] + _rms_bwd(dxhat, xhat, rstd)
        part = [_colsum(dh_), _colsum(dh_ * xhat * gg), _colsum(dh_ * (1.0 + sc) * xhat)]

        @pl.when(is_ctx)
        def _():
            for k, row in enumerate(part):
                sums_ref[3 + k:4 + k, :] += row

        @pl.when(jnp.logical_not(is_ctx))
        def _():
            for k, row in enumerate(part):
                sums_ref[k:k + 1, :] += row

    lat = lambda i: (jnp.maximum(i - nctx, 0), 0)
    return _call(
        body, side=side, name="norm_mod_bwd1", grid=(t // tr,),
        in_specs=[pl.BlockSpec((tr, d), lambda i: (i, 0)), pl.BlockSpec((tr, d), lambda i: (i, 0)),
                  pl.BlockSpec((tr, d), lat), pl.BlockSpec((1, d), lambda i: (0, 0)),
                  pl.BlockSpec((4, d), lambda i: (0, 0))],
        out_specs=[pl.BlockSpec((tr, d), lat), pl.BlockSpec((8, d), lambda i: (0, 0))],
        out_shape=[_sds((s, d)), _sds((8, d))], sem=("arbitrary",),
    )(xcat, dh, dx1, g, mod4)


def _resid_norm_mod_fwd(x, mix, g, mod3, tr):
    s, d = x.shape

    def body(x_ref, mix_ref, g_ref, mod_ref, x1_ref, h_ref):
        x1 = x_ref[...] + mod_ref[0:1, :] * mix_ref[...]
        x1_ref[...] = x1
        n = x1 * _rms_stats(x1) * g_ref[...]
        h_ref[...] = (n * (1.0 + mod_ref[2:3, :]) + mod_ref[1:2, :]).astype(BF16)

    row = pl.BlockSpec((tr, d), lambda i: (i, 0))
    return pl.pallas_call(
        body, name="resid_norm_mod_fwd", grid=(s // tr,),
        in_specs=[row, row, pl.BlockSpec((1, d), lambda i: (0, 0)), pl.BlockSpec((3, d), lambda i: (0, 0))],
        out_specs=[row, row], out_shape=[_sds((s, d)), _sds((s, d), BF16)], compiler_params=_params("parallel"),
    )(x, mix, g, mod3)


def _norm_mod_bwd2(x1, dh2, dx2, mix, g, mod3, tr, side=None):
    s, d = x1.shape

    def body(x_ref, dh_ref, dx2_ref, mix_ref, g_ref, mod_ref, dx1_ref, dmix_ref, sums_ref):
        @pl.when(pl.program_id(0) == 0)
        def _():
            sums_ref[...] = jnp.zeros_like(sums_ref)

        x = x_ref[...]
        dh_ = dh_ref[...]
        rstd = _rms_stats(x)
        xhat = x * rstd
        gg = g_ref[...]
        sc = mod_ref[2:3, :]
        dx1 = dx2_ref[...] + _rms_bwd(dh_ * (1.0 + sc) * gg, xhat, rstd)
        dx1_ref[...] = dx1
        dmix_ref[...] = (dx1 * mod_ref[0:1, :]).astype(BF16)
        part = [_colsum(dh_), _colsum(dh_ * xhat * gg), _colsum(dh_ * (1.0 + sc) * xhat), _colsum(dx1 * mix_ref[...])]
        for k, row in enumerate(part):
            sums_ref[k:k + 1, :] += row

    row = pl.BlockSpec((tr, d), lambda i: (i, 0))
    return _call(
        body, side=side, name="norm_mod_bwd2", grid=(s // tr,),
        in_specs=[row, row, row, row, pl.BlockSpec((1, d), lambda i: (0, 0)), pl.BlockSpec((3, d), lambda i: (0, 0))],
        out_specs=[row, row, pl.BlockSpec((8, d), lambda i: (0, 0))],
        out_shape=[_sds((s, d)), _sds((s, d), BF16), _sds((8, d))], sem=("arbitrary",),
    )(x1, dh2, dx2, mix, g, mod3)


def _final_fwd_bwd(x1, down, target, g_final, gate, tr):
    s, d = x1.shape

    def body(x1_ref, down_ref, t_ref, g_ref, gate_ref, dx2_ref, ddown_ref, sums_ref, loss_ref):
        @pl.when(pl.program_id(0) == 0)
        def _():
            sums_ref[...] = jnp.zeros_like(sums_ref)
            loss_ref[...] = jnp.zeros_like(loss_ref)

        down_ = down_ref[...]
        gate_ = gate_ref[...]
        x2 = x1_ref[...] + gate_ * down_
        rstd = _rms_stats(x2)
        xhat = x2 * rstd
        gg = g_ref[...]
        err = xhat * gg - t_ref[...]
        loss_ref[...] += 0.5 * jnp.sum(jnp.mean(err * err, axis=-1, keepdims=True))
        dy = err * (1.0 / d)
        dx2 = _rms_bwd(dy * gg, xhat, rstd)
        dx2_ref[...] = dx2
        ddown_ref[...] = (dx2 * gate_).astype(BF16)
        sums_ref[0:1, :] += _colsum(dy * xhat)
        sums_ref[1:2, :] += _colsum(dx2 * down_)

    row = pl.BlockSpec((tr, d), lambda i: (i, 0))
    vec = pl.BlockSpec((1, d), lambda i: (0, 0))
    return pl.pallas_call(
        body, name="final_fwd_bwd", grid=(s // tr,), in_specs=[row, row, row, vec, vec],
        out_specs=[row, row, pl.BlockSpec((8, d), lambda i: (0, 0)), pl.BlockSpec((8, LANES), lambda i: (0, 0))],
        out_shape=[_sds((s, d)), _sds((s, d), BF16), _sds((8, d)), _sds((8, LANES))],
        compiler_params=_params("arbitrary"),
    )(x1, down, target, g_final, gate)


def _swap_pairs(v):
    lane = lax.broadcasted_iota(jnp.int32, v.shape, 1)
    return jnp.where(lane % 2 == 0, pltpu.roll(v, LANES - 1, 1), pltpu.roll(v, 1, 1))


def _head_prep_fwd(z, col_off, n_heads, gain, cos, sin, tr, name, side=None):
    t = z.shape[0]
    per = math.gcd(4, n_heads, col_off // LANES)
    w = per * LANES
    hb = col_off // w

    def body(z_ref, g_ref, cos_ref, sin_ref, o_ref):
        for hh in range(per):
            cols = slice(hh * LANES, (hh + 1) * LANES)
            x = z_ref[:, cols]
            y = x * _rms_stats(x) * g_ref[...]
            o_ref[:, cols] = (y * cos_ref[...] + _swap_pairs(y) * sin_ref[...]).astype(BF16)

    tab = pl.BlockSpec((tr, LANES), lambda i, j: (i, 0))
    return _call(
        body, side=side, name=name, grid=(t // tr, n_heads // per),
        in_specs=[pl.BlockSpec((tr, w), lambda i, j: (i, hb + j)), pl.BlockSpec((1, LANES), lambda i, j: (0, 0)),
                  tab, tab],
        out_specs=pl.BlockSpec((tr, w), lambda i, j: (i, j)), out_shape=_sds((t, n_heads * LANES), BF16),
        sem=("parallel", "parallel"),
    )(z, gain, cos, sin)


def _head_prep_bwd(z, col_off, n_heads, gain, cos, sin, dout, row_off, tr, name, side=None):
    r = dout.shape[0]
    per = math.gcd(4, n_heads, col_off // LANES)
    w = per * LANES
    hb = col_off // w
    rb = row_off // tr

    def body(z_ref, g_ref, cos_ref, sin_ref, d_ref, dz_ref, dg_ref):
        @pl.when(jnp.logical_and(pl.program_id(0) == 0, pl.program_id(1) == 0))
        def _():
            dg_ref[...] = jnp.zeros_like(dg_ref)

        for hh in range(per):
            cols = slice(hh * LANES, (hh + 1) * LANES)
            x = z_ref[:, cols]
            rstd = _rms_stats(x)
            xhat = x * rstd
            dd = d_ref[:, cols]
            dy = dd * cos_ref[...] - _swap_pairs(dd) * sin_ref[...]
            dg_ref[0:1, :] += _colsum(dy * xhat)
            dz_ref[:, cols] = _rms_bwd(dy * g_ref[...], xhat, rstd).astype(BF16)

    tab = pl.BlockSpec((tr, LANES), lambda i, j: (rb + i, 0))
    return _call(
        body, side=side, name=name, grid=(r // tr, n_heads // per),
        in_specs=[pl.BlockSpec((tr, w), lambda i, j: (rb + i, hb + j)),
                  pl.BlockSpec((1, LANES), lambda i, j: (0, 0)), tab, tab,
                  pl.BlockSpec((tr, w), lambda i, j: (i, j))],
        out_specs=[pl.BlockSpec((tr, w), lambda i, j: (i, j)), pl.BlockSpec((8, LANES), lambda i, j: (0, 0))],
        out_shape=[_sds((r, n_heads * LANES), BF16), _sds((8, LANES))], sem=("arbitrary", "arbitrary"),
    )(z, gain, cos, sin, dout)


def _attn_fwd(qr, kr, z, v_off, n_ctx_rows, group, tq, side=None):
    t, kvw = kr.shape
    s = t - n_ctx_rows
    n_kv = kvw // LANES
    scale = LANES ** -0.5
    qb0 = n_ctx_rows // tq
    vb = v_off // LANES

    def body(q_ref, k_ref, v_ref, o_ref, lse_ref):
        k = k_ref[...]
        v = v_ref[...].astype(BF16)
        lse_ref[...] = jnp.zeros_like(lse_ref)
        for g in range(group):
            cols = slice(g * LANES, (g + 1) * LANES)
            sc = lax.dot_general(q_ref[:, cols], k, (((1,), (1,)), ((), ())), preferred_element_type=F32) * scale
            m = jnp.max(sc, axis=-1, keepdims=True)
            e = jnp.exp(sc - m)
            l = jnp.sum(e, axis=-1, keepdims=True)
            p = e * (1.0 / l)
            o_ref[:, cols] = jnp.dot(p.astype(BF16), v, preferred_element_type=F32).astype(BF16)
            lse_ref[:, g:g + 1] = m + jnp.log(l)

    return _call(
        body, side=side, name="attn_fwd", grid=(n_kv, s // tq),
        in_specs=[pl.BlockSpec((tq, group * LANES), lambda h, i: (qb0 + i, h)),
                  pl.BlockSpec((t, LANES), lambda h, i: (0, h)), pl.BlockSpec((t, LANES), lambda h, i: (0, vb + h))],
        out_specs=[pl.BlockSpec((tq, group * LANES), lambda h, i: (i, h)), pl.BlockSpec((tq, LANES), lambda h, i: (i, h))],
        out_shape=[_sds((s, n_kv * group * LANES), BF16), _sds((s, kvw))], sem=("parallel", "parallel"),
    )(qr, kr, z)


def _attn_bwd(qr, kr, z, v_off, d_o, attn_o, lse, n_ctx_rows, group, tq, side=None):
    t, kvw = kr.shape
    s = t - n_ctx_rows
    n_kv = kvw // LANES
    scale = LANES ** -0.5
    qb0 = n_ctx_rows // tq
    vb = v_off // LANES
    n_q_blocks = s // tq
    tn_dims = (((0,), (0,)), ((), ()))
    nt_dims = (((1,), (1,)), ((), ()))

    def body(q_ref, k_ref, v_ref, do_ref, o_ref, lse_ref, dq_ref, dk_ref, dv_ref):
        @pl.when(pl.program_id(1) == 0)
        def _():
            dk_ref[...] = jnp.zeros_like(dk_ref)
            dv_ref[...] = jnp.zeros_like(dv_ref)

        k = k_ref[...]
        v = v_ref[...].astype(BF16)
        for g in range(group):
            cols = slice(g * LANES, (g + 1) * LANES)
            q = q_ref[:, cols]
            do_ = do_ref[:, cols]
            row_dot = jnp.sum(do_.astype(F32) * o_ref[:, cols].astype(F32), axis=-1, keepdims=True)
            sc = lax.dot_general(q, k, nt_dims, preferred_element_type=F32)
            p = jnp.exp(sc * scale - lse_ref[:, g:g + 1])
            dv_ref[...] += lax.dot_general(p.astype(BF16), do_, tn_dims, preferred_element_type=F32)
            dp = lax.dot_general(do_, v, nt_dims, preferred_element_type=F32)
            ds = (p * (dp - row_dot)).astype(BF16)
            dq_ref[:, cols] = jnp.dot(ds, k, preferred_element_type=F32) * scale
            dk_ref[...] += lax.dot_general(ds, q, tn_dims, preferred_element_type=F32)

        @pl.when(pl.program_id(1) == n_q_blocks - 1)
        def _():
            dk_ref[...] = dk_ref[...] * scale

    qspec = pl.BlockSpec((tq, group * LANES), lambda h, i: (qb0 + i, h))
    ospec = pl.BlockSpec((tq, group * LANES), lambda h, i: (i, h))
    kspec = pl.BlockSpec((t, LANES), lambda h, i: (0, h))
    return _call(
        body, side=side, name="attn_bwd", grid=(n_kv, n_q_blocks),
        in_specs=[qspec, kspec, pl.BlockSpec((t, LANES), lambda h, i: (0, vb + h)), ospec, ospec,
                  pl.BlockSpec((tq, LANES), lambda h, i: (i, h))],
        out_specs=[ospec, kspec, kspec],
        out_shape=[_sds((s, n_kv * group * LANES)), _sds((t, kvw)), _sds((t, kvw))], sem=("parallel", "arbitrary"),
    )(qr, kr, z, d_o, attn_o, lse)


def _row_mask(shape, rows):
    r = lax.broadcasted_iota(jnp.int32, shape, 0)
    m = r == rows[0]
    for v in rows[1:]:
        m = jnp.logical_or(m, r == v)
    return m


def _shift_rows(x, k, n_ctx_rows):
    t = x.shape[0]
    if k == 0:
        return x
    rolled = pltpu.roll(x, (-k) % t, 0)
    if k > 0:
        dead = [n_ctx_rows - 1 - i for i in range(k)] + [t - 1 - i for i in range(k)]
    else:
        dead = [i for i in range(-k)] + [n_ctx_rows + i for i in range(-k)]
    return jnp.where(_row_mask(x.shape, dead), 0.0, rolled)


def _conv(x, w, b, n_ctx_rows):
    y = b
    for k in range(CONV_WIDTH):
        y = y + _shift_rows(x, k - 1, n_ctx_rows) * w[k:k + 1, :]
    return y


def _gates(xc_bf, w_r, b_r, w_i, b_i, lam):
    r = jax.nn.sigmoid(jnp.dot(xc_bf, w_r.astype(BF16), preferred_element_type=F32) + b_r)
    i = jax.nn.sigmoid(jnp.dot(xc_bf, w_i.astype(BF16), preferred_element_type=F32) + b_i)
    log_a = -LRU_C * r * _softplus(-lam)
    a = jnp.exp(log_a)
    mult = jnp.sqrt(-_expm1_nonpos(2.0 * log_a))
    return r, i, a, mult


def _rnn_specs(t, xr_off):
    xb = xr_off // LANES
    return dict(
        zcol=pl.BlockSpec((t, LANES), lambda j: (0, xb + j)), col=pl.BlockSpec((t, LANES), lambda j: (0, j)),
        conv_w=pl.BlockSpec((CONV_WIDTH, LANES), lambda j: (0, j)), vec=pl.BlockSpec((1, LANES), lambda j: (0, j)),
        gate_w=pl.BlockSpec((2, 1, LANES, LANES), lambda j: (0, j, 0, 0)), two=pl.BlockSpec((2, LANES), lambda j: (0, j)))


def _rnn_prep(z, xr_off, conv_w, conv_b, w_rg, b_rg, w_ig, b_ig, lam, n_ctx_rows, side=None):
    t = z.shape[0]
    d = conv_b.shape[1]
    sp = _rnn_specs(t, xr_off)

    def body(z_ref, cw_ref, cb_ref, wr_ref, br_ref, wi_ref, bi_ref, lam_ref, xc_ref, af_ref, bf_ref, ab_ref, bb_ref):
        xc = _conv(z_ref[...], cw_ref[...], cb_ref[...], n_ctx_rows)
        xc_ref[...] = xc
        xc_bf = xc.astype(BF16)
        for dr, (a_ref, b_ref) in enumerate(((af_ref, bf_ref), (ab_ref, bb_ref))):
            _, i, a, mult = _gates(xc_bf, wr_ref[dr, 0], br_ref[dr:dr + 1, :], wi_ref[dr, 0], bi_ref[dr:dr + 1, :],
                                   lam_ref[dr:dr + 1, :])
            a_ref[...] = a
            b_ref[...] = mult * (i * xc)

    return _call(
        body, side=side, name="rnn_prep", grid=(d // LANES,),
        in_specs=[sp["zcol"], sp["conv_w"], sp["vec"], sp["gate_w"], sp["two"], sp["gate_w"], sp["two"], sp["two"]],
        out_specs=[sp["col"]] * 5, out_shape=[_sds((t, d))] * 5, sem=("parallel",),
    )(z, conv_w, conv_b, w_rg, b_rg, w_ig, b_ig, lam)


def _scan(a, b, *, order, post, n_ctx_rows, name, tc=256, side=None):
    t, d = a.shape
    tc = _tile(math.gcd(n_ctx_rows, t - n_ctx_rows), tc, SUBLANES)
    nt, nctx = t // tc, n_ctx_rows // tc
    nlat = nt - nctx
    b_lat_only = b.shape[0] != t
    up = order.endswith("up")

    def chunk(i):
        if order == "ctx_lat_up":
            return i
        if order == "lat_ctx_down":
            return nt - 1 - i
        if order == "ctx_lat_down":
            return jnp.where(i < nctx, nctx - 1 - i, nt - 1 - (i - nctx))
        return jnp.where(i < nlat, nctx + i, i - nlat)

    def body(a_ref, b_ref, o_ref, carry_ref):
        @pl.when(pl.program_id(0) == 0)
        def _():
            carry_ref[...] = jnp.zeros_like(carry_ref)

        live = jnp.where(chunk(pl.program_id(0)) >= nctx, 1.0, 0.0) if b_lat_only else None

        def group(gi, carry):
            base = pl.multiple_of((gi if up else tc // SUBLANES - 1 - gi) * SUBLANES, SUBLANES)
            for r in (range(SUBLANES) if up else range(SUBLANES - 1, -1, -1)):
                a_r = a_ref[pl.ds(base + r, 1), :]
                b_r = b_ref[pl.ds(base + r, 1), :]
                if live is not None:
                    b_r = b_r * live
                if post:
                    out = b_r + carry
                    carry = a_r * out
                else:
                    out = a_r * carry + b_r
                    carry = out
                o_ref[pl.ds(base + r, 1), :] = out
            return carry

        carry_ref[0:1, :] = lax.fori_loop(0, tc // SUBLANES, group, carry_ref[0:1, :])

    full = pl.BlockSpec((tc, d), lambda i: (chunk(i), 0))
    b_spec = pl.BlockSpec((tc, d), lambda i: (jnp.maximum(chunk(i) - nctx, 0), 0)) if b_lat_only else full
    return _call(
        body, side=side, name=name, grid=(nt,), in_specs=[full, b_spec], out_specs=full, out_shape=_sds((t, d)),
        scratch_shapes=[pltpu.VMEM((SUBLANES, d), F32)], sem=("arbitrary",),
    )(a, b)


def _rnn_bwd(z, xr_off, xc, g_f, g_b, h_f, h_b, conv_w, w_rg, b_rg, w_ig, b_ig, lam, n_ctx_rows, side=None):
    t, d = xc.shape
    sp = _rnn_specs(t, xr_off)
    tn_dims = (((0,), (0,)), ((), ()))
    nt_dims = (((1,), (1,)), ((), ()))

    def body(z_ref, xc_ref, gf_ref, gb_ref, hf_ref, hb_ref, cw_ref, wr_ref, br_ref, wi_ref, bi_ref, lam_ref,
             dxr_ref, dwr_ref, dwi_ref, sums_ref):
        xc_ = xc_ref[...]
        xc_bf = xc_.astype(BF16)
        dxc = jnp.zeros_like(xc_)
        sums = [None] * 6
        for dr, (g_ref, h_ref) in enumerate(((gf_ref, hf_ref), (gb_ref, hb_ref))):
            w_r, w_i, lam_ = wr_ref[dr, 0], wi_ref[dr, 0], lam_ref[dr:dr + 1, :]
            r, i, a, mult = _gates(xc_bf, w_r, br_ref[dr:dr + 1, :], w_i, bi_ref[dr:dr + 1, :], lam_)
            g = g_ref[...]
            h = h_ref[...]
            if dr == 0:
                h_prev = jnp.where(_row_mask(h.shape, [0]), 0.0, pltpu.roll(h, 1, 0))
            else:
                h_prev = jnp.where(_row_mask(h.shape, [n_ctx_rows - 1]), 0.0, pltpu.roll(h, t - 1, 0))
            d_mult = g * i * xc_
            d_i = g * mult * xc_
            dxc = dxc + g * mult * i
            d_log_a = g * h_prev * a - d_mult * a * a / mult
            sp_ = _softplus(-lam_)
            d_r = d_log_a * (-LRU_C) * sp_
            d_sp = _colsum(d_log_a * (-LRU_C) * r)
            du_r = (d_r * r * (1.0 - r))
            du_i = (d_i * i * (1.0 - i))
            sums[dr] = _colsum(du_r)
            sums[2 + dr] = _colsum(du_i)
            sums[4 + dr] = d_sp * (-jax.nn.sigmoid(-lam_))
            du_r_bf, du_i_bf = du_r.astype(BF16), du_i.astype(BF16)
            dwr_ref[dr, 0] = lax.dot_general(xc_bf, du_r_bf, tn_dims, preferred_element_type=F32).astype(BF16)
            dwi_ref[dr, 0] = lax.dot_general(xc_bf, du_i_bf, tn_dims, preferred_element_type=F32).astype(BF16)
            dxc = dxc + lax.dot_general(du_r_bf, w_r.astype(BF16), nt_dims, preferred_element_type=F32)
            dxc = dxc + lax.dot_general(du_i_bf, w_i.astype(BF16), nt_dims, preferred_element_type=F32)
        xr = z_ref[...]
        cw = cw_ref[...]
        dxr = jnp.zeros_like(dxc)
        rows = list(sums)
        for k in range(CONV_WIDTH):
            dxr = dxr + _shift_rows(dxc, 1 - k, n_ctx_rows) * cw[k:k + 1, :]
            rows.append(_colsum(dxc * _shift_rows(xr, k - 1, n_ctx_rows)))
        rows.append(_colsum(dxc))
        dxr_ref[...] = dxr.astype(BF16)
        sums_ref[...] = jnp.zeros_like(sums_ref)
        for k, row in enumerate(rows):
            sums_ref[k:k + 1, :] = row

    return _call(
        body, side=side, name="rnn_bwd", grid=(d // LANES,),
        in_specs=[sp["zcol"]] + [sp["col"]] * 5 + [sp["conv_w"], sp["gate_w"], sp["two"], sp["gate_w"], sp["two"],
                                                  sp["two"]],
        out_specs=[sp["col"], sp["gate_w"], sp["gate_w"], pl.BlockSpec((16, LANES), lambda j: (0, j))],
        out_shape=[_sds((t, d), BF16), _sds(w_rg.shape, BF16), _sds(w_ig.shape, BF16), _sds((16, d))],
        sem=("parallel",),
    )(z, xc, g_f, g_b, h_f, h_b, conv_w, w_rg, b_rg, w_ig, b_ig, lam)


def _tiles2d(s, d, tr, tcol):
    return (s // tr, d // tcol), pl.BlockSpec((tr, tcol), lambda i, j: (i, j))


def _zspec(tr, tcol, row_off, col_off):
    rb, cb = row_off // tr, col_off // tcol
    return pl.BlockSpec((tr, tcol), lambda i, j: (rb + i, cb + j))


def _rnn_gate_fwd(h_f, h_b, z, xg_off, n_ctx_rows, tr, tcol, side=None):
    t, d = h_f.shape
    s = t - n_ctx_rows
    grid, out = _tiles2d(s, d, tr, tcol)
    hs = _zspec(tr, tcol, n_ctx_rows, 0)

    def body(hf_ref, hb_ref, xg_ref, u_ref):
        u_ref[...] = ((hf_ref[...] + hb_ref[...]) * _gelu(xg_ref[...])).astype(BF16)

    return _call(body, side=side, name="rnn_gate_fwd", grid=grid,
                 in_specs=[hs, hs, _zspec(tr, tcol, n_ctx_rows, xg_off)], out_specs=out, out_shape=_sds((s, d), BF16),
                 sem=("parallel", "parallel"))(h_f, h_b, z)


def _rnn_gate_bwd(d_u, h_f, h_b, z, xg_off, n_ctx_rows, tr, tcol, side=None):
    t, d = h_f.shape
    s = t - n_ctx_rows
    grid, out = _tiles2d(s, d, tr, tcol)
    hs = _zspec(tr, tcol, n_ctx_rows, 0)

    def body(du_ref, hf_ref, hb_ref, xg_ref, dr_ref, dxg_ref):
        du = du_ref[...]
        xg = xg_ref[...]
        dr_ref[...] = du * _gelu(xg)
        dxg_ref[...] = (du * (hf_ref[...] + hb_ref[...]) * _gelu_grad(xg)).astype(BF16)

    return _call(body, side=side, name="rnn_gate_bwd", grid=grid,
                 in_specs=[out, hs, hs, _zspec(tr, tcol, n_ctx_rows, xg_off)], out_specs=[out, out],
                 out_shape=[_sds((s, d)), _sds((s, d), BF16)], sem=("parallel", "parallel"))(d_u, h_f, h_b, z)


def _merge_fwd(y_attn, y_rnn, z, gl_off, n_ctx_rows, tr, tcol):
    s, d = y_attn.shape
    grid, out = _tiles2d(s, d, tr, tcol)

    def body(ya_ref, yr_ref, ga_ref, gr_ref, o_ref):
        o_ref[...] = (jax.nn.sigmoid(ga_ref[...]) * ya_ref[...] + jax.nn.sigmoid(gr_ref[...]) * yr_ref[...]).astype(BF16)

    return pl.pallas_call(
        body, name="merge_fwd", grid=grid,
        in_specs=[out, out, _zspec(tr, tcol, n_ctx_rows, gl_off), _zspec(tr, tcol, n_ctx_rows, gl_off + d)],
        out_specs=out, out_shape=_sds((s, d), BF16), compiler_params=_params("parallel", "parallel"),
    )(y_attn, y_rnn, z, z)


def _merge_bwd(d_mrg, y_attn, y_rnn, z, gl_off, n_ctx_rows, tr, tcol, side=None):
    s, d = y_attn.shape
    grid, out = _tiles2d(s, d, tr, tcol)

    def body(dm_ref, ya_ref, yr_ref, ga_ref, gr_ref, dya_ref, dyr_ref, dga_ref, dgr_ref):
        dm = dm_ref[...]
        ga = jax.nn.sigmoid(ga_ref[...])
        gr = jax.nn.sigmoid(gr_ref[...])
        dya_ref[...] = (dm * ga).astype(BF16)
        dyr_ref[...] = (dm * gr).astype(BF16)
        dga_ref[...] = (dm * ya_ref[...] * ga * (1.0 - ga)).astype(BF16)
        dgr_ref[...] = (dm * yr_ref[...] * gr * (1.0 - gr)).astype(BF16)

    return _call(
        body, side=side, name="merge_bwd", grid=grid,
        in_specs=[out, out, out, _zspec(tr, tcol, n_ctx_rows, gl_off), _zspec(tr, tcol, n_ctx_rows, gl_off + d)],
        out_specs=[out] * 4, out_shape=[_sds((s, d), BF16)] * 4, sem=("parallel", "parallel"),
    )(d_mrg, y_attn, y_rnn, z, z)


def _sq_relu(up, tr, tcol, side=None):
    grid, out = _tiles2d(*up.shape, _tile(up.shape[0], 2 * tr, 16), _tile(up.shape[1], 4 * tcol, LANES))

    def body(u_ref, o_ref):
        r = jnp.maximum(u_ref[...], 0.0)
        o_ref[...] = (r * r).astype(BF16)

    return _call(body, side=side, name="sq_relu", grid=grid, in_specs=[out], out_specs=out,
                 out_shape=_sds(up.shape, BF16), sem=("parallel", "parallel"))(up)


def _cast_into_window(w, chip, col_sharded, name):
    r, c = w.shape
    tr, tcol = _tile(r, 512, 16), _tile(c, 1024, LANES)
    nrb, ncb = r // tr, c // tcol

    def body(chip_ref, w_ref, o_ref):
        o_ref[...] = w_ref[...].astype(BF16)

    if col_sharded:
        omap = lambda i, j, chip_ref: (i, chip_ref[0] * ncb + j)
    else:
        omap = lambda i, j, chip_ref: (chip_ref[0] * nrb + i, j)
    return pl.pallas_call(
        body, name=name,
        grid_spec=pltpu.PrefetchScalarGridSpec(
            num_scalar_prefetch=1, grid=(nrb, ncb),
            in_specs=[pl.BlockSpec((tr, tcol), lambda i, j, chip_ref: (i, j))], out_specs=pl.BlockSpec((tr, tcol), omap)),
        out_shape=_sds((r, c * N_CHIPS) if col_sharded else (r * N_CHIPS, c), BF16),
        compiler_params=_params("parallel", "parallel"),
    )(chip, w)


def _sum_leading(parts, name):
    n, r, c = parts.shape
    tr, tcol = _tile(r, 512, SUBLANES), _tile(c, 1024, LANES)

    def body(p_ref, o_ref):
        tot = p_ref[0]
        for k in range(1, n):
            tot = tot + p_ref[k]
        o_ref[...] = tot

    return pl.pallas_call(
        body, name=name, grid=(r // tr, c // tcol), in_specs=[pl.BlockSpec((n, tr, tcol), lambda i, j: (0, i, j))],
        out_specs=pl.BlockSpec((tr, tcol), lambda i, j: (i, j)), out_shape=_sds((r, c)),
        compiler_params=_params("parallel", "parallel"),
    )(parts)


def _add_half(full, other, core, split_rows, name):
    r, c = other.shape
    tr, tcol = _tile(r, 512, 16), _tile(c, 1024, LANES)
    nrb, ncb = r // tr, c // tcol

    def body(core_ref, f_ref, o_ref, out_ref):
        out_ref[...] = (f_ref[...].astype(F32) + o_ref[...].astype(F32)).astype(out_ref.dtype)

    if split_rows:
        fmap = lambda i, j, core_ref: (core_ref[0] * nrb + i, j)
    else:
        fmap = lambda i, j, core_ref: (i, core_ref[0] * ncb + j)
    same = lambda i, j, core_ref: (i, j)
    return pl.pallas_call(
        body, name=name,
        grid_spec=pltpu.PrefetchScalarGridSpec(
            num_scalar_prefetch=1, grid=(nrb, ncb),
            in_specs=[pl.BlockSpec((tr, tcol), fmap), pl.BlockSpec((tr, tcol), same)],
            out_specs=pl.BlockSpec((tr, tcol), same)),
        out_shape=_sds((r, c), BF16), compiler_params=_params("parallel", "parallel"),
    )(core, full, other)


def _sum_regions(pair, got, place, col_sharded, name):
    _, r, c = got.shape
    tr, tcol = _tile(r, 512, 16), _tile(c, 1024, LANES)
    nrb, ncb = r // tr, c // tcol

    def body(place_ref, p_ref, g_ref, out_ref):
        tot = p_ref[...].astype(F32)
        for k in range(N_CHIPS - 1):
            tot = tot + g_ref[k].astype(F32)
        out_ref[...] = tot

    if col_sharded:
        pmap = lambda i, j, pr: (i, pr[0] * ncb + j)
        omap = lambda i, j, pr: (pr[1] * nrb + i, j)
        out_shape = (2 * r, c)
    else:
        pmap = lambda i, j, pr: (pr[0] * nrb + i, j)
        omap = lambda i, j, pr: (i, pr[1] * ncb + j)
        out_shape = (r, 2 * c)
    return pl.pallas_call(
        body, name=name,
        grid_spec=pltpu.PrefetchScalarGridSpec(
            num_scalar_prefetch=1, grid=(nrb, ncb),
            in_specs=[pl.BlockSpec((tr, tcol), pmap), pl.BlockSpec((N_CHIPS - 1, tr, tcol), lambda i, j, pr: (0, i, j))],
            out_specs=pl.BlockSpec((tr, tcol), omap)),
        out_shape=_sds(out_shape), compiler_params=_params("parallel", "parallel"),
    )(place, pair, got)


def _adamw(w, g, m, v, name):
    r, c = w.shape
    tr, tcol = _tile(r, 512, SUBLANES), _tile(c, 1024, LANES)
    blk = pl.BlockSpec((tr, tcol), lambda i, j: (i, j))

    def body(w_ref, g_ref, m_ref, v_ref, d_ref, nm_ref, nv_ref, g_out_ref):
        g_ = g_ref[...]
        g_out_ref[...] = g_
        m_ = ADAM_B1 * m_ref[...] + (1.0 - ADAM_B1) * g_
        v_ = ADAM_B2 * v_ref[...] + (1.0 - ADAM_B2) * (g_ * g_)
        m_hat = m_ / (1.0 - ADAM_B1 ** ADAM_STEP)
        v_hat = v_ / (1.0 - ADAM_B2 ** ADAM_STEP)
        d_ref[...] = -ADAM_LR * (m_hat / (jnp.sqrt(v_hat) + ADAM_EPS) + ADAM_WD * w_ref[...])
        nm_ref[...] = m_
        nv_ref[...] = v_

    return _call(body, name=name, grid=(r // tr, c // tcol), in_specs=[blk] * 4, out_specs=[blk] * 4,
                 out_shape=[_sds((r, c))] * 4, sem=("parallel", "parallel"))(w, g, m, v)


def _place():
    x, y, c = lax.axis_index("x"), lax.axis_index("y"), lax.axis_index("c")
    chips = [(1 - x, y), (x, 1 - y), (1 - x, 1 - y)]
    return x, y, c, chips


def _all_gather8(blk, name):
    m, n = blk.shape

    def body(x_ref, out_ref, send_sems, recv_sems, local_sem):
        x, y, c, chips = _place()
        me, sibling = (x, y, c), (x, y, 1 - c)

        def rows(px, py, pc):
            return out_ref.at[pl.ds((4 * px + 2 * py + pc) * m, m), :]

        def copy(k, block, to, src=None):
            return pltpu.make_async_remote_copy(
                src_ref=rows(*block) if src is None else src, dst_ref=rows(*block), send_sem=send_sems.at[k],
                recv_sem=recv_sems.at[k], device_id=to, device_id_type=MESH_ID)

        mine = pltpu.make_async_copy(x_ref, rows(*me), local_sem)
        mine.start()
        first = [copy(0, me, sibling, src=x_ref)]
        first += [copy(1 + j, me, (*chip, c), src=x_ref) for j, chip in enumerate(chips)]
        for cp in first:
            cp.start()
        passed = [copy(4 + j, (*chip, c), sibling) for j, chip in enumerate(chips)]
        for j, chip in enumerate(chips):
            copy(1 + j, (*chip, c), me).wait_recv()
            passed[j].start()
        copy(0, sibling, me).wait_recv()
        for j, chip in enumerate(chips):
            copy(4 + j, (*chip, 1 - c), me).wait_recv()
        for cp in first + passed:
            cp.wait_send()
        mine.wait()

    return pl.pallas_call(
        body, name=name, out_shape=_sds((N_DEV * m, n), blk.dtype), in_specs=[ANY], out_specs=ANY,
        scratch_shapes=[pltpu.SemaphoreType.DMA((7,)), pltpu.SemaphoreType.DMA((7,)), pltpu.SemaphoreType.DMA],
    )(blk)


def _half(ref, core, split_rows):
    r, c = ref.shape
    if split_rows:
        return ref.at[pl.ds(core * (r // 2), r // 2), :]
    return ref.at[:, pl.ds(core * (c // 2), c // 2)]


def _chip_block(ref, j, col_sharded):
    r, c = ref.shape
    if col_sharded:
        return ref.at[:, pl.ds(j * (c // N_CHIPS), c // N_CHIPS)]
    return ref.at[pl.ds(j * (r // N_CHIPS), r // N_CHIPS), :]


def _rows_part(ref, part):
    lo, hi, n = part
    r = ref.shape[0]
    return ref if (lo, hi) == (0, n) else ref.at[pl.ds(lo * (r // n), (hi - lo) * (r // n)), :]


def _copy(send_sems, recv_sems, k, src, dst, to):
    return pltpu.make_async_remote_copy(src_ref=src, dst_ref=dst, send_sem=send_sems.at[k], recv_sem=recv_sems.at[k],
                                        device_id=to, device_id_type=MESH_ID)


def _in_place(arrays):
    return tuple(arrays), tuple(_sds(a.shape, a.dtype) for a in arrays), {i: i for i in range(len(arrays))}


def _gather_ici(fulls, col_sharded, part=(0, 1, 1)):
    nw = len(fulls)

    def build(_, refs, send_sems, recv_sems, sem0):
        x, y, c, chips = _place()
        sends, recvs = [], []
        for w in range(nw):
            win = lambda j: _rows_part(_half(_chip_block(refs[w], j, col_sharded[w]), c, True), part)
            for k, (cx, cy) in enumerate(chips):
                sem = sem0 + 3 * w + k
                sends.append(_copy(send_sems, recv_sems, sem, win(2 * x + y), win(2 * x + y), (cx, cy, c)))
                recvs.append(_copy(send_sems, recv_sems, sem, win(2 * cx + cy), win(2 * cx + cy), (cx, cy, c)))
        return sends, recvs

    return _Side(*_in_place(fulls), 3 * nw, build)


def _gather_d2d(fulls, col_sharded):
    nw = len(fulls)

    def build(_, refs, send_sems, recv_sems, sem0):
        x, y, c, chips = _place()
        sends, recvs = [], []
        for w in range(nw):
            win = lambda j, core: _half(_chip_block(refs[w], j, col_sharded[w]), core, True)
            for k, (cx, cy) in enumerate(chips):
                sem = sem0 + 3 * w + k
                sends.append(_copy(send_sems, recv_sems, sem, win(2 * cx + cy, c), win(2 * cx + cy, c), (x, y, 1 - c)))
                recvs.append(_copy(send_sems, recv_sems, sem, win(2 * cx + cy, 1 - c), win(2 * cx + cy, 1 - c),
                                   (x, y, 1 - c)))
        return sends, recvs

    return _Side(*_in_place(fulls), 3 * nw, build)


def _exchange(side, name):
    return _call(None, side=side, name=name)()[1]


def _swap_halves(grads, col_sharded):
    nw = len(grads)
    out_shapes = [_sds((g.shape[0] // 2, g.shape[1]) if col else (g.shape[0], g.shape[1] // 2), g.dtype)
                  for g, col in zip(grads, col_sharded)]

    def build(g_refs, o_refs, send_sems, recv_sems, sem0):
        x, y, c, _ = _place()
        copies = [_copy(send_sems, recv_sems, sem0 + w, _half(g_refs[w], 1 - c, col_sharded[w]), o_refs[w],
                        (x, y, 1 - c)) for w in range(nw)]
        return copies, copies

    return _Side(tuple(grads), tuple(out_shapes), {}, nw, build)


def _scatter_regions(pairs, col_sharded, part=(0, 1, 1), into=None):
    nw = len(pairs)

    def region_shape(p, col):
        return (p.shape[0], p.shape[1] // N_CHIPS) if col else (p.shape[0] // N_CHIPS, p.shape[1])

    out_shapes = tuple(_sds((N_CHIPS - 1, *region_shape(p, col)), p.dtype) for p, col in zip(pairs, col_sharded))

    def build(refs, o_refs, send_sems, recv_sems, sem0):
        x, y, c, chips = _place()
        copies = []
        for w in range(nw):
            for k, (cx, cy) in enumerate(chips):
                copies.append(_copy(
                    send_sems, recv_sems, sem0 + 3 * w + k,
                    _rows_part(_chip_block(refs[w], 2 * cx + cy, col_sharded[w]), part),
                    _rows_part(o_refs[w].at[k], part), (cx, cy, c)))
        return copies, copies

    if into is None:
        return _Side(tuple(pairs), out_shapes, {}, 3 * nw, build)
    return _Side((*pairs, *into), out_shapes, {nw + w: w for w in range(nw)}, 3 * nw, build)


def _join_halves(halves, col_sharded):
    nw = len(halves)

    def build(_, refs, send_sems, recv_sems, sem0):
        x, y, c, _ = _place()
        sends, recvs = [], []
        for w in range(nw):
            mine, theirs = _half(refs[w], c, col_sharded[w]), _half(refs[w], 1 - c, col_sharded[w])
            sends.append(_copy(send_sems, recv_sems, sem0 + w, mine, mine, (x, y, 1 - c)))
            recvs.append(_copy(send_sems, recv_sems, sem0 + w, theirs, theirs, (x, y, 1 - c)))
        return sends, recvs

    return _Side(*_in_place(halves), nw, build)


def _part_rows(size):
    return -(-size // (SUBLANES * LANES)) * SUBLANES


def _pack(arrays, pad_rows_to=SUBLANES):
    flat = [jnp.pad(a.reshape(-1), (0, _part_rows(a.size) * LANES - a.size)).reshape(-1, LANES) for a in arrays]
    rows = sum(f.shape[0] for f in flat)
    pad = (-rows) % pad_rows_to
    if pad:
        flat.append(jnp.zeros((pad, LANES), F32))
    return jnp.concatenate(flat, axis=0)


def _unpack(packed, shapes):
    out, r = [], 0
    for shp in shapes:
        size = math.prod(shp)
        out.append(packed[r:r + _part_rows(size)].reshape(-1)[:size].reshape(shp))
        r += _part_rows(size)
    return out


def _rope_tables(n_ctx_rows, s):
    rows = s // GRID_W
    row_idx = jnp.repeat(jnp.arange(rows), GRID_W)
    col_idx = jnp.tile(jnp.arange(GRID_W), rows)
    n_freq = LANES // 4
    inv_freq = ROPE_THETA ** (-jnp.arange(n_freq, dtype=F32) / n_freq)
    ang = jnp.concatenate([row_idx.astype(F32)[:, None] * inv_freq, col_idx.astype(F32)[:, None] * inv_freq], axis=-1)
    cos = jnp.repeat(jnp.cos(ang), 2, axis=-1)
    sin = jnp.repeat(jnp.sin(ang), 2, axis=-1) * jnp.tile(jnp.array([-1.0, 1.0], F32), LANES // 2)
    cos = jnp.concatenate([jnp.ones((n_ctx_rows, LANES), F32), cos], axis=0)
    sin = jnp.concatenate([jnp.zeros((n_ctx_rows, LANES), F32), sin], axis=0)
    return cos, sin


WEIGHT_NAMES = ['c_ctx', 'w_mod', 'b_mod', 'g_mix', 'g_mlp', 'w_in', 'q_gain', 'k_gain', 'conv_w', 'conv_b', 'w_rg',
                'b_rg', 'w_ig', 'b_ig', 'lru_lambda', 'w_o_attn', 'w_o_rnn', 'w_out', 'w_up', 'w_down', 'g_final']
BIG = ['w_in', 'w_o_attn', 'w_o_rnn', 'w_out', 'w_up', 'w_down']
BIG_COL_SHARDED = [True, False, False, False, True, False]
GATES = ['w_rg', 'w_ig']
SMALL = ['c_ctx', 'b_mod', 'g_mix', 'g_mlp', 'q_gain', 'k_gain', 'conv_b', 'g_final',
         'conv_w', 'b_rg', 'b_ig', 'lru_lambda']


def kernel(x, c, ctx, c_ctx, w_mod, b_mod, g_mix, g_mlp, w_in, q_gain, k_gain, conv_w, conv_b, w_rg, b_rg, w_ig, b_ig, lru_lambda, w_o_attn, w_o_rnn, w_out, w_up, w_down, g_final, loss_target, m_c_ctx, m_w_mod, m_b_mod, m_g_mix, m_g_mlp, m_w_in, m_q_gain, m_k_gain, m_conv_w, m_conv_b, m_w_rg, m_b_rg, m_w_ig, m_b_ig, m_lru_lambda, m_w_o_attn, m_w_o_rnn, m_w_out, m_w_up, m_w_down, m_g_final, v_c_ctx, v_w_mod, v_b_mod, v_g_mix, v_g_mlp, v_w_in, v_q_gain, v_k_gain, v_conv_w, v_conv_b, v_w_rg, v_b_rg, v_w_ig, v_b_ig, v_lru_lambda, v_w_o_attn, v_w_o_rnn, v_w_out, v_w_up, v_w_down, v_g_final):
    given = dict(locals())
    weights = {n: given[n] for n in WEIGHT_NAMES}
    moms = {n: given["m_" + n] for n in WEIGHT_NAMES}
    vars_ = {n: given["v_" + n] for n in WEIGHT_NAMES}

    s, d = x.shape[1], x.shape[2]
    n_ctx = ctx.shape[1]
    t = n_ctx + s
    hd = q_gain.shape[1]
    assert hd == LANES and w_rg.shape[-1] == LANES
    attn_w = w_o_attn.shape[1] * N_CHIPS
    n_in = w_in.shape[2] * N_CHIPS
    kv_w = (n_in - attn_w - 4 * d) // 2
    group = attn_w // kv_w
    k_off, v_off, xr_off = attn_w, attn_w + kv_w, attn_w + 2 * kv_w
    xg_off, gl_off = xr_off + d, xr_off + 2 * d
    d_mod = N_MOD * d
    tr = _tile(math.gcd(n_ctx, s), 256, 16)
    tcol = _tile(math.gcd(d, xr_off), 1024, LANES)
    xi, yi, ci = lax.axis_index("x"), lax.axis_index("y"), lax.axis_index("c")
    chip = 2 * xi + yi
    core = ci.astype(jnp.int32).reshape(1)

    sharded_small = [conv_w[0], b_rg[0], b_ig[0], lru_lambda[0]]
    pack0 = _pack([c[0]] + sharded_small)
    got0 = _all_gather8(pack0, "gather_small_inputs").reshape(N_DEV, -1, LANES)
    c_all = got0[:, :_part_rows(d)].reshape(N_DEV, -1)[:, :d]
    per_chip = [_unpack(got0[2 * j, _part_rows(d):], [a.shape for a in sharded_small]) for j in range(N_CHIPS)]
    conv_w_f, b_rg_f, b_ig_f, lam_f = (jnp.concatenate([per_chip[j][i] for j in range(N_CHIPS)], axis=-1)
                                       for i in range(4))
    c16 = jnp.concatenate([c_all, c_ctx[None, :], jnp.zeros((16 - N_DEV - 1, d), F32)], axis=0)
    b_mod_shard = lax.dynamic_slice(b_mod, (0, chip * (d_mod // N_CHIPS)), (1, d_mod // N_CHIPS))
    mod_part, silu16 = _mod_fwd(c16, w_mod[0], b_mod_shard)
    mod_all = _all_gather8(mod_part, "gather_mod").reshape(N_DEV, 16, d_mod // N_CHIPS)
    mod16 = jnp.concatenate([mod_all[2 * j] for j in range(N_CHIPS)], axis=-1)
    me = 4 * xi + 2 * yi + ci
    mod_lat = lax.dynamic_slice(mod16, (me, 0), (1, d_mod)).reshape(N_MOD, d)
    mod_ctx = mod16[N_DEV].reshape(N_MOD, d)
    mod4 = jnp.stack([mod_ctx[0], mod_ctx[1], mod_lat[0], mod_lat[1]])
    mod3 = jnp.stack([mod_lat[2], mod_lat[3], mod_lat[4]])
    gate_f = mod_lat[5][None, :]

    chip_arr = chip.astype(jnp.int32).reshape(1)
    own = {n: _cast_into_window(weights[n][0], chip_arr, col, "cast_" + n) for n, col in zip(BIG, BIG_COL_SHARDED)}
    place = jnp.stack([chip, ci]).astype(jnp.int32)
    row3 = [False] * 3

    def pair_sum(n, full, other, col):
        return _add_half(full, other, core, col, "pair_sum_" + n)

    def chip_sum(n, pair, got, col):
        return _sum_regions(pair, got, place, col, "chip_sum_" + n)

    (w_in_f,) = _exchange(_gather_ici([own['w_in']], [True]), "gather_w_in_ici")
    (w_in_f,) = _exchange(_gather_d2d([w_in_f], [True]), "gather_w_in_d2d")
    xcat = jnp.concatenate([ctx[0], x[0]], axis=0)
    cos, sin = _rope_tables(n_ctx, s)
    h = _norm_mod_fwd(xcat, g_mix, mod4, n_ctx, tr)
    ici_up = lambda w, lo, hi: _gather_ici([w], [True], (lo, hi, 8))
    ici_down = lambda w, lo, hi: _gather_ici([w], [False], (lo, hi, 8))
    z, (w_oa_f, w_or_f, w_out_f) = _matmul(h, w_in_f, name="mm_in", side=_gather_ici(
        [own['w_o_attn'], own['w_o_rnn'], own['w_out']], row3))
    qr, (w_up_f, w_down_f) = _head_prep_fwd(z, 0, attn_w // LANES, q_gain, cos, sin, tr, "q_prep", side=_sides(
        ici_up(own['w_up'], 0, 1), ici_down(own['w_down'], 0, 1)))
    kr = _head_prep_fwd(z, k_off, kv_w // LANES, k_gain, cos, sin, tr, "k_prep")
    (attn_o, lse), (w_up_f, w_oa_f, w_or_f, w_out_f) = _attn_fwd(qr, kr, z, v_off, n_ctx, group, tr, side=_sides(
        ici_up(w_up_f, 1, 8), _gather_d2d([w_oa_f, w_or_f, w_out_f], row3)))
    (xc, a_f, bx_f, a_b, bx_b), (w_down_f, w_up_f) = _rnn_prep(
        z, xr_off, conv_w_f, conv_b, w_rg[0], b_rg_f, w_ig[0], b_ig_f, lam_f, n_ctx,
        side=_sides(ici_down(w_down_f, 1, 4), _gather_d2d([w_up_f], [True])))
    h_f = _scan(a_f, bx_f, order="ctx_lat_up", post=False, n_ctx_rows=n_ctx, name="scan_f")
    h_b = _scan(a_b, bx_b, order="ctx_lat_down", post=False, n_ctx_rows=n_ctx, name="scan_b")
    u = _rnn_gate_fwd(h_f, h_b, z, xg_off, n_ctx, tr, tcol)
    y_attn = _matmul(attn_o, w_oa_f, name="mm_o_attn")
    y_rnn = _matmul(u, w_or_f, name="mm_o_rnn")
    mrg = _merge_fwd(y_attn, y_rnn, z, gl_off, n_ctx, tr, tcol)
    mix = _matmul(mrg, w_out_f, name="mm_out")
    x1, h2 = _resid_norm_mod_fwd(x[0], mix, g_mlp, mod3, tr)
    up, (w_down_f,) = _matmul(h2, w_up_f, name="mm_up", side=ici_down(w_down_f, 4, 8))
    act, (w_down_f,) = _sq_relu(up, tr, tcol, side=_gather_d2d([w_down_f], [False]))
    down = _matmul(act, w_down_f, name="mm_down")
    dx2, d_down, sums_fin, loss_blk = _final_fwd_bwd(x1, down, loss_target[0], g_final[None, :], gate_f, tr)

    d_up = _matmul(d_down, w_down_f, tb=True, out_dtype=BF16, name="mm_d_up",
                   post=lambda d_act, up_: d_act * 2.0 * jnp.maximum(up_, 0.0), post_args=(up,))
    g_w_down = _matmul(act, d_down, ta=True, out_dtype=BF16, name="mm_g_down")

    def scatter(pairs, cols, lo, hi, into=None):
        return _scatter_regions(pairs, cols, (lo, hi, 8), into)

    dh2, (got,) = _matmul(d_up, w_up_f, tb=True, name="mm_d_h2", side=_swap_halves([g_w_down], [False]))
    p_down = pair_sum('w_down', g_w_down, got, False)
    g_w_up, got_down = _matmul(h2, d_up, ta=True, out_dtype=BF16, name="mm_g_up", side=scatter([p_down], [False], 0, 4))
    (dx1, d_mix, sums2), (got,) = _norm_mod_bwd2(x1, dh2, dx2, mix, g_mlp, mod3, tr, side=_swap_halves([g_w_up], [True]))
    p_up = pair_sum('w_up', g_w_up, got, True)
    d_mrg = _matmul(d_mix, w_out_f, tb=True, name="mm_d_mrg")
    g_w_out = _matmul(mrg, d_mix, ta=True, out_dtype=BF16, name="mm_g_out")
    d_ya, d_yr, d_gla, d_glr = _merge_bwd(d_mrg, y_attn, y_rnn, z, gl_off, n_ctx, tr, tcol)
    d_o = _matmul(d_ya, w_oa_f, tb=True, out_dtype=BF16, name="mm_d_o")
    g_w_oa = _matmul(attn_o, d_ya, ta=True, out_dtype=BF16, name="mm_g_o_attn")
    d_u = _matmul(d_yr, w_or_f, tb=True, name="mm_d_u")
    g_w_or = _matmul(u, d_yr, ta=True, out_dtype=BF16, name="mm_g_o_rnn")
    d_rnn, d_xg = _rnn_gate_bwd(d_u, h_f, h_b, z, xg_off, n_ctx, tr, tcol)
    gs_f = _scan(a_f, d_rnn, order="lat_ctx_down", post=True, n_ctx_rows=n_ctx, name="scan_f_bwd")
    gs_b = _scan(a_b, d_rnn, order="lat_ctx_up", post=True, n_ctx_rows=n_ctx, name="scan_b_bwd")
    o_names, o_grads = ['w_o_attn', 'w_o_rnn', 'w_out'], [g_w_oa, g_w_or, g_w_out]
    (d_xr, g_w_rg, g_w_ig, sums_rnn), (got_down, *got_o) = _rnn_bwd(
        z, xr_off, xc, gs_f, gs_b, h_f, h_b, conv_w_f, w_rg[0], b_rg_f, w_ig[0], b_ig_f, lam_f, n_ctx,
        side=_sides(scatter([p_down], [False], 4, 8, got_down), _swap_halves(o_grads, row3)))
    hs_down = chip_sum('w_down', p_down, got_down, False)
    p_o = [pair_sum(n, g, o, False) for n, g, o in zip(o_names, o_grads, got_o)]
    gate_cols = 8 * LANES if g_w_rg.size % (8 * LANES * N_CHIPS * 16) == 0 else 2 * LANES
    gate_rows = g_w_rg.size // gate_cols
    gate_grads = [g_w_rg.reshape(gate_rows, gate_cols), g_w_ig.reshape(gate_rows, gate_cols)]
    (dq, dk, dv), (got_up, got_oa, got_rg, got_ig, gs_down) = _attn_bwd(
        qr, kr, z, v_off, d_o, attn_o, lse, n_ctx, group, tr, side=_sides(
            _scatter_regions([p_up], [True]), _scatter_regions(p_o[:1], row3[:1]), _swap_halves(gate_grads, row3[:2]),
            _join_halves([hs_down], [False])))
    hs_up = chip_sum('w_up', p_up, got_up, True)
    hs_oa = chip_sum('w_o_attn', p_o[0], got_oa, False)
    p_gate = [pair_sum(n, g, o, False) for n, g, o in zip(GATES, gate_grads, (got_rg, got_ig))]
    d_q_raw, g_q_gain = _head_prep_bwd(z, 0, attn_w // LANES, q_gain, cos, sin, dq, n_ctx, tr, "q_prep_bwd")
    d_k_raw, g_k_gain = _head_prep_bwd(z, k_off, kv_w // LANES, k_gain, cos, sin, dk, 0, tr, "k_prep_bwd")
    zero_ctx = lambda w: jnp.zeros((n_ctx, w), BF16)
    dz = jnp.concatenate([
        jnp.concatenate([zero_ctx(attn_w), d_q_raw], axis=0), d_k_raw, dv.astype(BF16), d_xr,
        jnp.concatenate([zero_ctx(d), d_xg], axis=0), jnp.concatenate([zero_ctx(d), d_gla], axis=0),
        jnp.concatenate([zero_ctx(d), d_glr], axis=0)], axis=1)
    g_w_in, (got_or, got_out, got_rg, got_ig, gs_up, gs_oa) = _matmul(
        h, dz, ta=True, out_dtype=BF16, name="mm_g_in", side=_sides(
            _scatter_regions(p_o[1:] + p_gate, [False] * 4), _join_halves([hs_up], [True]), _join_halves([hs_oa], [False])))
    hs_late = [chip_sum(n, p, o, False)
               for n, p, o in zip(o_names[1:] + GATES, p_o[1:] + p_gate, (got_or, got_out, got_rg, got_ig))]
    (got,) = _exchange(_swap_halves([g_w_in], [True]), "swap_w_in")
    p_in = pair_sum('w_in', g_w_in, got, True)
    dh, got_in = _matmul(dz, w_in_f, tb=True, name="mm_d_h", side=scatter([p_in], [True], 0, 4))
    (got_in,) = _exchange(scatter([p_in], [True], 4, 8, got_in), "scatter_w_in_rest")
    grad_x, sums1 = _norm_mod_bwd1(xcat, dh, dx1, g_mix, mod4, n_ctx, tr)

    zeros_d = jnp.zeros((d,), F32)
    dmod_lat = jnp.concatenate([sums1[0], sums1[1], sums2[3], sums2[0], sums2[1], sums_fin[1]])
    dmod_ctx = jnp.concatenate([sums1[3], sums1[4]] + [zeros_d] * 4)
    small_parts = [dmod_lat, dmod_ctx, loss_blk[0, 0:1], sums1[2] + sums1[5], sums2[2], g_q_gain[0], g_k_gain[0],
                   sums_rnn[10], sums_fin[0], sums_rnn[6:10], sums_rnn[0:2], sums_rnn[2:4], sums_rnn[4:6]]
    pack1 = _pack(small_parts)
    got1 = _all_gather8(pack1, "gather_small_grads").reshape(N_DEV, -1, LANES)
    tot1 = _sum_leading(got1, "sum_small_grads")
    part_shapes = [a.shape for a in small_parts]
    (s_dmod_lat, s_dmod_ctx, s_loss, g_g_mix, g_g_mlp, g_q_gain, g_k_gain, g_conv_b, g_g_final,
     g_conv_w_f, g_b_rg_f, g_b_ig_f, g_lam_f) = _unpack(tot1, part_shapes)
    loss = s_loss[0]
    g_b_mod = (s_dmod_lat + s_dmod_ctx)[None, :]
    n_mod_rows = _part_rows(d_mod)
    dmod16 = jnp.concatenate([got1[:, :n_mod_rows].reshape(N_DEV, -1)[:, :d_mod], s_dmod_ctx[None, :],
                              jnp.zeros((16 - N_DEV - 1, d_mod), F32)], axis=0)
    dmod16_shard = lax.dynamic_slice(dmod16, (0, chip * (d_mod // N_CHIPS)), (16, d_mod // N_CHIPS))
    g_w_mod = _matmul(silu16, dmod16_shard, ta=True, name="mm_g_mod")
    dsilu_part = _matmul(dmod16_shard[N_DEV:], w_mod[0], tb=True, name="mm_d_silu")
    dsilu_all = _all_gather8(dsilu_part, "gather_d_silu").reshape(N_DEV, 8, d)
    g_c_ctx = _c_ctx_grad(dsilu_all, c_ctx[None, :])[0]

    def shard_of(full):
        w = full.shape[-1] // N_CHIPS
        return lax.dynamic_slice(full, (0, chip * w), (full.shape[0], w))

    grads = {
        'c_ctx': g_c_ctx, 'b_mod': g_b_mod, 'g_mix': g_g_mix[None, :], 'g_mlp': g_g_mlp[None, :],
        'q_gain': g_q_gain[None, :], 'k_gain': g_k_gain[None, :], 'conv_b': g_conv_b[None, :],
        'g_final': g_g_final,
        'conv_w': shard_of(g_conv_w_f)[None], 'b_rg': shard_of(g_b_rg_f)[None], 'b_ig': shard_of(g_b_ig_f)[None],
        'lru_lambda': shard_of(g_lam_f)[None], 'w_mod': g_w_mod[None],
    }

    delta, new_m, new_v = {}, {}, {}

    def adamw(n):
        shp = weights[n].shape
        as2d = (lambda a: a[0]) if n not in GATES else (lambda a: a.reshape(-1, LANES))
        dl, nm, nv, g = _adamw(as2d(weights[n]), as2d(grads[n]), as2d(moms[n]), as2d(vars_[n]), "adamw_" + n)
        delta[n], new_m[n], new_v[n], grads[n] = dl.reshape(shp), nm.reshape(shp), nv.reshape(shp), g.reshape(shp)

    hs_in = chip_sum('w_in', p_in, got_in, True)
    gs_in, gs_or, gs_out = _exchange(_join_halves([hs_in] + hs_late[:2], [True, False, False]), "join_rest")
    for n, g in zip(['w_in', 'w_o_attn', 'w_o_rnn', 'w_out', 'w_up', 'w_down'], [gs_in, gs_oa, gs_or, gs_out, gs_up, gs_down]):
        grads[n] = g[None]
    half_cols = gate_cols // 2
    mine = jnp.concatenate([lax.dynamic_slice(hs, (0, ci * half_cols), (gate_rows // N_CHIPS, half_cols))
                            for hs in hs_late[2:]], axis=0)
    gate_all = _all_gather8(mine, "gather_gate_grads")
    gate_all = gate_all.reshape(N_CHIPS, 2, len(GATES), gate_rows // N_CHIPS, half_cols)
    for i, n in enumerate(GATES):
        grads[n] = jnp.moveaxis(gate_all[:, :, i], 1, 2).reshape(weights[n].shape)
    for n in ['w_mod'] + BIG + GATES:
        adamw(n)
    small_shapes = [weights[n].shape for n in SMALL]
    packed = [_pack([src[n] for n in SMALL], 512) for src in (weights, grads, moms, vars_)]
    outs = _adamw(*packed, "adamw_small")
    for res, out in zip((delta, new_m, new_v), outs):
        for n, a in zip(SMALL, _unpack(out, small_shapes)):
            res[n] = a
    return (loss, grad_x[None], *[grads[n] for n in WEIGHT_NAMES], *[delta[n] for n in WEIGHT_NAMES],
            *[new_m[n] for n in WEIGHT_NAMES], *[new_v[n] for n in WEIGHT_NAMES])
```

```python
import functools
import math
from typing import Callable, NamedTuple

import jax
import jax.numpy as jnp
from jax import lax
from jax.experimental import pallas as pl
from jax.experimental.pallas import tpu as pltpu

F32 = jnp.float32
BF16 = jnp.bfloat16
MESH_ID = pl.DeviceIdType.MESH
ANY = pl.BlockSpec(memory_space=pl.ANY)

NORM_EPS = 1e-6
LRU_C = 8.0
GRID_W = 64
ROPE_THETA = 10000.0
N_MOD = 6
CONV_WIDTH = 4
ADAM_LR = 0.001
ADAM_B1 = 0.9
ADAM_B2 = 0.999
ADAM_EPS = 1e-08
ADAM_WD = 0.01
ADAM_STEP = 10

LANES = 128
SUBLANES = 8
V7X_VMEM_LIMIT = 48 * 1024 * 1024
N_CHIPS = 4
N_DEV = 8
GELU_C = math.sqrt(2.0 / math.pi)
GELU_A = 0.044715


def _tile(dim, pref, align):
    t = min(pref, dim)
    t -= t % align
    while t >= align:
        if dim % t == 0:
            return t
        t -= align
    return dim


def _params(*sem):
    return pltpu.CompilerParams(dimension_semantics=sem, vmem_limit_bytes=V7X_VMEM_LIMIT)


def _sds(shape, dtype=F32):
    return jax.ShapeDtypeStruct(shape, dtype)


class _Side(NamedTuple):
    operands: tuple
    results: tuple
    aliases: dict
    n_sems: int
    build: Callable


def _sides(*sides):
    ops, res, aliases, spans, n = [], [], {}, [], 0
    for s in sides:
        spans.append((len(ops), len(res), n))
        aliases.update({len(ops) + i: len(res) + j for i, j in s.aliases.items()})
        ops += s.operands
        res += s.results
        n += s.n_sems

    def build(op_refs, res_refs, send_sems, recv_sems, sem0):
        sends, recvs = [], []
        for s, (o, r, k) in zip(sides, spans):
            a, b = s.build(op_refs[o:o + len(s.operands)], res_refs[r:r + len(s.results)], send_sems, recv_sems,
                           sem0 + k)
            sends += a
            recvs += b
        return sends, recvs

    return _Side(tuple(ops), tuple(res), aliases, n, build)


def _call(body, *, side=None, sem=(), grid=(), in_specs=(), out_specs=(), out_shape=(), scratch_shapes=(), **kw):
    if side is None:
        return pl.pallas_call(body, grid=grid, in_specs=list(in_specs), out_specs=out_specs, out_shape=out_shape,
                              scratch_shapes=list(scratch_shapes), compiler_params=_params(*sem), **kw)
    aliases = kw.pop("input_output_aliases", {})
    many = isinstance(out_shape, (list, tuple))
    out_specs_l, out_shape_l = (list(out_specs), list(out_shape)) if many else ([out_specs], [out_shape])
    n_in, n_out, n_scr = len(in_specs), len(out_shape_l), len(scratch_shapes)
    n_op, n_res = len(side.operands), len(side.results)

    def hosted(*refs):
        ins, ops = refs[:n_in], refs[n_in:n_in + n_op]
        outs = refs[n_in + n_op:n_in + n_op + n_out]
        res = refs[n_in + n_op + n_out:n_in + n_op + n_out + n_res]
        scr = refs[n_in + n_op + n_out + n_res:-2]
        send_sems, recv_sems = refs[-2:]

        def start():
            for cp in side.build(ops, res, send_sems, recv_sems, 0)[0]:
                cp.start()

        def finish():
            sends, recvs = side.build(ops, res, send_sems, recv_sems, 0)
            for cp in recvs:
                cp.wait_recv()
            for cp in sends:
                cp.wait_send()

        if not grid:
            start()
            finish()
            return
        ids = [pl.program_id(a) for a in range(len(grid))]
        first = functools.reduce(jnp.logical_and, [i == 0 for i in ids])
        last = functools.reduce(jnp.logical_and, [i == g - 1 for i, g in zip(ids, grid)])
        pl.when(first)(start)
        body(*ins, *outs, *scr)
        pl.when(last)(finish)

    def run(*args):
        got = pl.pallas_call(
            hosted, grid=grid, in_specs=[*in_specs, *[ANY] * n_op], out_specs=[*out_specs_l, *[ANY] * n_res],
            out_shape=[*out_shape_l, *side.results],
            scratch_shapes=[*scratch_shapes, pltpu.SemaphoreType.DMA((side.n_sems,)),
                            pltpu.SemaphoreType.DMA((side.n_sems,))],
            input_output_aliases={**aliases, **{n_in + i: n_out + j for i, j in side.aliases.items()}},
            compiler_params=_params(*["arbitrary"] * len(grid)), **kw)(*args, *side.operands)
        own = list(got[:n_out]) if many else got[0]
        return own, list(got[n_out:])

    return run


def _matmul(a, b, *, ta=False, tb=False, out_dtype=F32, name, tm=1024, tn=1024, tk=2816, side=None, post=None,
            post_args=()):
    k_dim, m = a.shape if ta else a.shape[::-1]
    n, k2 = b.shape if tb else b.shape[::-1]
    assert k_dim == k2, (a.shape, b.shape, ta, tb)
    tm = _tile(m, tm, LANES if ta else 16)
    tn = _tile(n, tn, 16 if tb else LANES)
    tk = _tile(k_dim, tk, LANES)
    nk = k_dim // tk
    dims = (((0 if ta else 1,), (1 if tb else 0,)), ((), ()))
    if nk == 1:
        def whole(a_ref, b_ref, *rest):
            acc = lax.dot_general(a_ref[...].astype(BF16), b_ref[...].astype(BF16), dims, preferred_element_type=F32)
            if post is not None:
                acc = post(acc, *[r[...] for r in rest[:-1]])
            rest[-1][...] = acc.astype(rest[-1].dtype)

        a_spec = pl.BlockSpec((tk, tm), lambda i, j: (0, i)) if ta else pl.BlockSpec((tm, tk), lambda i, j: (i, 0))
        b_spec = pl.BlockSpec((tn, tk), lambda i, j: (j, 0)) if tb else pl.BlockSpec((tk, tn), lambda i, j: (0, j))
        o_spec = pl.BlockSpec((tm, tn), lambda i, j: (i, j))
        return _call(
            whole, side=side, name=name, grid=(m // tm, n // tn), in_specs=[a_spec, b_spec] + [o_spec] * len(post_args),
            out_specs=o_spec, out_shape=_sds((m, n), out_dtype), sem=("parallel", "parallel"),
        )(a, b, *post_args)
    assert post is None

    def body(a_ref, b_ref, o_ref, acc_ref):
        k = pl.program_id(2)

        @pl.when(k == 0)
        def _():
            acc_ref[...] = jnp.zeros_like(acc_ref)

        acc_ref[...] += lax.dot_general(a_ref[...].astype(BF16), b_ref[...].astype(BF16), dims,
                                        preferred_element_type=F32)

        @pl.when(k == nk - 1)
        def _():
            o_ref[...] = acc_ref[...].astype(o_ref.dtype)

    a_spec = pl.BlockSpec((tk, tm), lambda i, j, k: (k, i)) if ta else pl.BlockSpec((tm, tk), lambda i, j, k: (i, k))
    b_spec = pl.BlockSpec((tn, tk), lambda i, j, k: (j, k)) if tb else pl.BlockSpec((tk, tn), lambda i, j, k: (k, j))
    return _call(
        body, side=side, name=name, grid=(m // tm, n // tn, nk), in_specs=[a_spec, b_spec],
        out_specs=pl.BlockSpec((tm, tn), lambda i, j, k: (i, j)), out_shape=_sds((m, n), out_dtype),
        scratch_shapes=[pltpu.VMEM((tm, tn), F32)], sem=("parallel", "parallel", "arbitrary"),
    )(a, b)


def _silu(x):
    return x * jax.nn.sigmoid(x)


def _gelu(x):
    return 0.5 * x * (1.0 + jnp.tanh(GELU_C * (x + GELU_A * x * x * x)))


def _gelu_grad(x):
    t = jnp.tanh(GELU_C * (x + GELU_A * x * x * x))
    return 0.5 * (1.0 + t) + 0.5 * x * (1.0 - t * t) * GELU_C * (1.0 + 3.0 * GELU_A * x * x)


def _expm1_nonpos(x):
    series = x * (1.0 + x * (1.0 / 2 + x * (1.0 / 6 + x * (1.0 / 24 + x * (1.0 / 120 + x * (1.0 / 720 + x / 5040))))))
    return jnp.where(x > -0.25, series, jnp.exp(x) - 1.0)


def _softplus(x):
    return jnp.maximum(x, 0.0) + jnp.log1p(jnp.exp(-jnp.abs(x)))


def _rms_stats(x):
    return lax.rsqrt(jnp.mean(x * x, axis=-1, keepdims=True) + NORM_EPS)


def _rms_bwd(dxhat, xhat, rstd):
    return rstd * (dxhat - xhat * jnp.mean(dxhat * xhat, axis=-1, keepdims=True))


def _colsum(v):
    return jnp.sum(v, axis=0, keepdims=True)


def _mod_fwd(c16, w_mod, b_mod_shard):
    r, d = c16.shape
    n = w_mod.shape[1]
    tn = _tile(n, 512, LANES)

    def body(c_ref, w_ref, b_ref, o_ref, s_ref):
        s = _silu(c_ref[...])
        s_ref[...] = s
        o_ref[...] = jnp.dot(s.astype(BF16), w_ref[...].astype(BF16), preferred_element_type=F32) + b_ref[...]

    return pl.pallas_call(
        body, name="mod_fwd", grid=(n // tn,),
        in_specs=[pl.BlockSpec((r, d), lambda j: (0, 0)), pl.BlockSpec((d, tn), lambda j: (0, j)),
                  pl.BlockSpec((1, tn), lambda j: (0, j))],
        out_specs=[pl.BlockSpec((r, tn), lambda j: (0, j)), pl.BlockSpec((r, d), lambda j: (0, 0))],
        out_shape=[_sds((r, n)), _sds((r, d))], compiler_params=_params("arbitrary"),
    )(c16, w_mod, b_mod_shard)


def _c_ctx_grad(parts, c_ctx_row):
    d = c_ctx_row.shape[1]

    def body(p_ref, c_ref, o_ref):
        tot = p_ref[0, 0:1, :]
        for chip in range(1, N_CHIPS):
            tot = tot + p_ref[2 * chip, 0:1, :]
        c = c_ref[...]
        sg = jax.nn.sigmoid(c)
        o_ref[...] = tot * (sg * (1.0 + c * (1.0 - sg)))

    return pl.pallas_call(body, name="c_ctx_grad", out_shape=_sds((1, d)), compiler_params=_params())(parts, c_ctx_row)


def _norm_mod_fwd(xcat, g, mod4, n_ctx_rows, tr):
    t, d = xcat.shape
    nctx = n_ctx_rows // tr

    def body(x_ref, g_ref, mod_ref, h_ref):
        x = x_ref[...]
        n = x * _rms_stats(x) * g_ref[...]
        is_ctx = pl.program_id(0) < nctx
        sh = jnp.where(is_ctx, mod_ref[0:1, :], mod_ref[2:3, :])
        sc = jnp.where(is_ctx, mod_ref[1:2, :], mod_ref[3:4, :])
        h_ref[...] = (n * (1.0 + sc) + sh).astype(BF16)

    return pl.pallas_call(
        body, name="norm_mod_fwd", grid=(t // tr,),
        in_specs=[pl.BlockSpec((tr, d), lambda i: (i, 0)), pl.BlockSpec((1, d), lambda i: (0, 0)),
                  pl.BlockSpec((4, d), lambda i: (0, 0))],
        out_specs=pl.BlockSpec((tr, d), lambda i: (i, 0)), out_shape=_sds((t, d), BF16),
        compiler_params=_params("parallel"),
    )(xcat, g, mod4)


def _norm_mod_bwd1(xcat, dh, dx1, g, mod4, n_ctx_rows, tr, side=None):
    t, d = xcat.shape
    nctx = n_ctx_rows // tr
    s = t - n_ctx_rows

    def body(x_ref, dh_ref, dx1_ref, g_ref, mod_ref, dx_ref, sums_ref):
        i = pl.program_id(0)
        is_ctx = i < nctx

        @pl.when(i == 0)
        def _():
            sums_ref[...] = jnp.zeros_like(sums_ref)

        x = x_ref[...]
        dh_ = dh_ref[...]
        rstd = _rms_stats(x)
        xhat = x * rstd
        gg = g_ref[...]
        sc = jnp.where(is_ctx, mod_ref[1:2, :], mod_ref[3:4, :])
        dxhat = dh_ * (1.0 + sc) * gg
        dx_ref[...] = dx1_ref[...] + _rms_bwd(dxhat, xhat, rstd)
        part = [_colsum(dh_), _colsum(dh_ * xhat * gg), _colsum(dh_ * (1.0 + sc) * xhat)]

        @pl.when(is_ctx)
        def _():
            for k, row in enumerate(part):
                sums_ref[3 + k:4 + k, :] += row

        @pl.when(jnp.logical_not(is_ctx))
        def _():
            for k, row in enumerate(part):
                sums_ref[k:k + 1, :] += row

    lat = lambda i: (jnp.maximum(i - nctx, 0), 0)
    return _call(
        body, side=side, name="norm_mod_bwd1", grid=(t // tr,),
        in_specs=[pl.BlockSpec((tr, d), lambda i: (i, 0)), pl.BlockSpec((tr, d), lambda i: (i, 0)),
                  pl.BlockSpec((tr, d), lat), pl.BlockSpec((1, d), lambda i: (0, 0)),
                  pl.BlockSpec((4, d), lambda i: (0, 0))],
        out_specs=[pl.BlockSpec((tr, d), lat), pl.BlockSpec((8, d), lambda i: (0, 0))],
        out_shape=[_sds((s, d)), _sds((8, d))], sem=("arbitrary",),
    )(xcat, dh, dx1, g, mod4)


def _resid_norm_mod_fwd(x, mix, g, mod3, tr):
    s, d = x.shape

    def body(x_ref, mix_ref, g_ref, mod_ref, x1_ref, h_ref):
        x1 = x_ref[...] + mod_ref[0:1, :] * mix_ref[...]
        x1_ref[...] = x1
        n = x1 * _rms_stats(x1) * g_ref[...]
        h_ref[...] = (n * (1.0 + mod_ref[2:3, :]) + mod_ref[1:2, :]).astype(BF16)

    row = pl.BlockSpec((tr, d), lambda i: (i, 0))
    return pl.pallas_call(
        body, name="resid_norm_mod_fwd", grid=(s // tr,),
        in_specs=[row, row, pl.BlockSpec((1, d), lambda i: (0, 0)), pl.BlockSpec((3, d), lambda i: (0, 0))],
        out_specs=[row, row], out_shape=[_sds((s, d)), _sds((s, d), BF16)], compiler_params=_params("parallel"),
    )(x, mix, g, mod3)


def _norm_mod_bwd2(x1, dh2, dx2, mix, g, mod3, tr, side=None):
    s, d = x1.shape

    def body(x_ref, dh_ref, dx2_ref, mix_ref, g_ref, mod_ref, dx1_ref, dmix_ref, sums_ref):
        @pl.when(pl.program_id(0) == 0)
        def _():
            sums_ref[...] = jnp.zeros_like(sums_ref)

        x = x_ref[...]
        dh_ = dh_ref[...]
        rstd = _rms_stats(x)
        xhat = x * rstd
        gg = g_ref[...]
        sc = mod_ref[2:3, :]
        dx1 = dx2_ref[...] + _rms_bwd(dh_ * (1.0 + sc) * gg, xhat, rstd)
        dx1_ref[...] = dx1
        dmix_ref[...] = (dx1 * mod_ref[0:1, :]).astype(BF16)
        part = [_colsum(dh_), _colsum(dh_ * xhat * gg), _colsum(dh_ * (1.0 + sc) * xhat), _colsum(dx1 * mix_ref[...])]
        for k, row in enumerate(part):
            sums_ref[k:k + 1, :] += row

    row = pl.BlockSpec((tr, d), lambda i: (i, 0))
    return _call(
        body, side=side, name="norm_mod_bwd2", grid=(s // tr,),
        in_specs=[row, row, row, row, pl.BlockSpec((1, d), lambda i: (0, 0)), pl.BlockSpec((3, d), lambda i: (0, 0))],
        out_specs=[row, row, pl.BlockSpec((8, d), lambda i: (0, 0))],
        out_shape=[_sds((s, d)), _sds((s, d), BF16), _sds((8, d))], sem=("arbitrary",),
    )(x1, dh2, dx2, mix, g, mod3)


def _final_fwd_bwd(x1, down, target, g_final, gate, tr):
    s, d = x1.shape

    def body(x1_ref, down_ref, t_ref, g_ref, gate_ref, dx2_ref, ddown_ref, sums_ref, loss_ref):
        @pl.when(pl.program_id(0) == 0)
        def _():
            sums_ref[...] = jnp.zeros_like(sums_ref)
            loss_ref[...] = jnp.zeros_like(loss_ref)

        down_ = down_ref[...]
        gate_ = gate_ref[...]
        x2 = x1_ref[...] + gate_ * down_
        rstd = _rms_stats(x2)
        xhat = x2 * rstd
        gg = g_ref[...]
        err = xhat * gg - t_ref[...]
        loss_ref[...] += 0.5 * jnp.sum(jnp.mean(err * err, axis=-1, keepdims=True))
        dy = err * (1.0 / d)
        dx2 = _rms_bwd(dy * gg, xhat, rstd)
        dx2_ref[...] = dx2
        ddown_ref[...] = (dx2 * gate_).astype(BF16)
        sums_ref[0:1, :] += _colsum(dy * xhat)
        sums_ref[1:2, :] += _colsum(dx2 * down_)

    row = pl.BlockSpec((tr, d), lambda i: (i, 0))
    vec = pl.BlockSpec((1, d), lambda i: (0, 0))
    return pl.pallas_call(
        body, name="final_fwd_bwd", grid=(s // tr,), in_specs=[row, row, row, vec, vec],
        out_specs=[row, row, pl.BlockSpec((8, d), lambda i: (0, 0)), pl.BlockSpec((8, LANES), lambda i: (0, 0))],
        out_shape=[_sds((s, d)), _sds((s, d), BF16), _sds((8, d)), _sds((8, LANES))],
        compiler_params=_params("arbitrary"),
    )(x1, down, target, g_final, gate)


def _swap_pairs(v):
    lane = lax.broadcasted_iota(jnp.int32, v.shape, 1)
    return jnp.where(lane % 2 == 0, pltpu.roll(v, LANES - 1, 1), pltpu.roll(v, 1, 1))


def _head_prep_fwd(z, col_off, n_heads, gain, cos, sin, tr, name, side=None):
    t = z.shape[0]
    per = math.gcd(4, n_heads, col_off // LANES)
    w = per * LANES
    hb = col_off // w

    def body(z_ref, g_ref, cos_ref, sin_ref, o_ref):
        for hh in range(per):
            cols = slice(hh * LANES, (hh + 1) * LANES)
            x = z_ref[:, cols]
            y = x * _rms_stats(x) * g_ref[...]
            o_ref[:, cols] = (y * cos_ref[...] + _swap_pairs(y) * sin_ref[...]).astype(BF16)

    tab = pl.BlockSpec((tr, LANES), lambda i, j: (i, 0))
    return _call(
        body, side=side, name=name, grid=(t // tr, n_heads // per),
        in_specs=[pl.BlockSpec((tr, w), lambda i, j: (i, hb + j)), pl.BlockSpec((1, LANES), lambda i, j: (0, 0)),
                  tab, tab],
        out_specs=pl.BlockSpec((tr, w), lambda i, j: (i, j)), out_shape=_sds((t, n_heads * LANES), BF16),
        sem=("parallel", "parallel"),
    )(z, gain, cos, sin)


def _dz_start(t, n_in, n_ctx_rows):
    tcol = _tile(n_in, 1024, LANES)

    def body(o_ref):
        o_ref[...] = jnp.zeros_like(o_ref)

    return pl.pallas_call(body, name="dz_start", grid=(n_in // tcol,),
                          out_specs=pl.BlockSpec((n_ctx_rows, tcol), lambda j: (0, j)), out_shape=_sds((t, n_in), BF16),
                          compiler_params=_params("parallel"))()


def _head_prep_bwd(z, col_off, n_heads, gain, cos, sin, dout, row_off, tr, dz, name, side=None):
    r = dout.shape[0]
    per = math.gcd(4, n_heads, col_off // LANES)
    w = per * LANES
    hb = col_off // w
    rb = row_off // tr

    def body(z_ref, g_ref, cos_ref, sin_ref, d_ref, _, dz_ref, dg_ref):
        @pl.when(jnp.logical_and(pl.program_id(0) == 0, pl.program_id(1) == 0))
        def _():
            dg_ref[...] = jnp.zeros_like(dg_ref)

        for hh in range(per):
            cols = slice(hh * LANES, (hh + 1) * LANES)
            x = z_ref[:, cols]
            rstd = _rms_stats(x)
            xhat = x * rstd
            dd = d_ref[:, cols]
            dy = dd * cos_ref[...] - _swap_pairs(dd) * sin_ref[...]
            dg_ref[0:1, :] += _colsum(dy * xhat)
            dz_ref[:, cols] = _rms_bwd(dy * g_ref[...], xhat, rstd).astype(BF16)

    tab = pl.BlockSpec((tr, LANES), lambda i, j: (rb + i, 0))
    window = pl.BlockSpec((tr, w), lambda i, j: (rb + i, hb + j))
    return _call(
        body, side=side, name=name, grid=(r // tr, n_heads // per),
        in_specs=[window, pl.BlockSpec((1, LANES), lambda i, j: (0, 0)), tab, tab,
                  pl.BlockSpec((tr, w), lambda i, j: (i, j)), ANY],
        out_specs=[window, pl.BlockSpec((8, LANES), lambda i, j: (0, 0))],
        out_shape=[_sds(dz.shape, BF16), _sds((8, LANES))], input_output_aliases={5: 0}, sem=("arbitrary", "arbitrary"),
    )(z, gain, cos, sin, dout, dz)


def _attn_fwd(qr, kr, z, v_off, n_ctx_rows, group, tq, side=None):
    t, kvw = kr.shape
    s = t - n_ctx_rows
    n_kv = kvw // LANES
    scale = LANES ** -0.5
    qb0 = n_ctx_rows // tq
    vb = v_off // LANES

    def body(q_ref, k_ref, v_ref, o_ref, lse_ref):
        k = k_ref[...]
        v = v_ref[...].astype(BF16)
        lse_ref[...] = jnp.zeros_like(lse_ref)
        for g in range(group):
            cols = slice(g * LANES, (g + 1) * LANES)
            sc = lax.dot_general(q_ref[:, cols], k, (((1,), (1,)), ((), ())), preferred_element_type=F32) * scale
            m = jnp.max(sc, axis=-1, keepdims=True)
            e = jnp.exp(sc - m)
            l = jnp.sum(e, axis=-1, keepdims=True)
            p = e * (1.0 / l)
            o_ref[:, cols] = jnp.dot(p.astype(BF16), v, preferred_element_type=F32).astype(BF16)
            lse_ref[:, g:g + 1] = m + jnp.log(l)

    return _call(
        body, side=side, name="attn_fwd", grid=(n_kv, s // tq),
        in_specs=[pl.BlockSpec((tq, group * LANES), lambda h, i: (qb0 + i, h)),
                  pl.BlockSpec((t, LANES), lambda h, i: (0, h)), pl.BlockSpec((t, LANES), lambda h, i: (0, vb + h))],
        out_specs=[pl.BlockSpec((tq, group * LANES), lambda h, i: (i, h)), pl.BlockSpec((tq, LANES), lambda h, i: (i, h))],
        out_shape=[_sds((s, n_kv * group * LANES), BF16), _sds((s, kvw))], sem=("parallel", "parallel"),
    )(qr, kr, z)


def _attn_bwd(qr, kr, z, v_off, d_o, attn_o, lse, dz, n_ctx_rows, group, tq, side=None):
    t, kvw = kr.shape
    s = t - n_ctx_rows
    n_kv = kvw // LANES
    scale = LANES ** -0.5
    qb0 = n_ctx_rows // tq
    vb = v_off // LANES
    n_q_blocks = s // tq
    tn_dims = (((0,), (0,)), ((), ()))
    nt_dims = (((1,), (1,)), ((), ()))

    def body(q_ref, k_ref, v_ref, do_ref, o_ref, lse_ref, _, dq_ref, dk_ref, dz_ref, dv_ref):
        @pl.when(pl.program_id(1) == 0)
        def _():
            dk_ref[...] = jnp.zeros_like(dk_ref)
            dv_ref[...] = jnp.zeros_like(dv_ref)

        k = k_ref[...]
        v = v_ref[...].astype(BF16)
        for g in range(group):
            cols = slice(g * LANES, (g + 1) * LANES)
            q = q_ref[:, cols]
            do_ = do_ref[:, cols]
            row_dot = jnp.sum(do_.astype(F32) * o_ref[:, cols].astype(F32), axis=-1, keepdims=True)
            sc = lax.dot_general(q, k, nt_dims, preferred_element_type=F32)
            p = jnp.exp(sc * scale - lse_ref[:, g:g + 1])
            dv_ref[...] += lax.dot_general(p.astype(BF16), do_, tn_dims, preferred_element_type=F32)
            dp = lax.dot_general(do_, v, nt_dims, preferred_element_type=F32)
            ds = (p * (dp - row_dot)).astype(BF16)
            dq_ref[:, cols] = jnp.dot(ds, k, preferred_element_type=F32) * scale
            dk_ref[...] += lax.dot_general(ds, q, tn_dims, preferred_element_type=F32)

        @pl.when(pl.program_id(1) == n_q_blocks - 1)
        def _():
            dk_ref[...] = dk_ref[...] * scale
            dz_ref[...] = dv_ref[...].astype(BF16)

    qspec = pl.BlockSpec((tq, group * LANES), lambda h, i: (qb0 + i, h))
    ospec = pl.BlockSpec((tq, group * LANES), lambda h, i: (i, h))
    kspec = pl.BlockSpec((t, LANES), lambda h, i: (0, h))
    vspec = pl.BlockSpec((t, LANES), lambda h, i: (0, vb + h))
    return _call(
        body, side=side, name="attn_bwd", grid=(n_kv, n_q_blocks),
        in_specs=[qspec, kspec, vspec, ospec, ospec, pl.BlockSpec((tq, LANES), lambda h, i: (i, h)), ANY],
        out_specs=[ospec, kspec, vspec], out_shape=[_sds((s, n_kv * group * LANES)), _sds((t, kvw)), _sds(dz.shape, BF16)],
        scratch_shapes=[pltpu.VMEM((t, LANES), F32)], input_output_aliases={6: 2}, sem=("parallel", "arbitrary"),
    )(qr, kr, z, d_o, attn_o, lse, dz)


def _row_mask(shape, rows):
    r = lax.broadcasted_iota(jnp.int32, shape, 0)
    m = r == rows[0]
    for v in rows[1:]:
        m = jnp.logical_or(m, r == v)
    return m


def _shift_rows(x, k, n_ctx_rows):
    t = x.shape[0]
    if k == 0:
        return x
    rolled = pltpu.roll(x, (-k) % t, 0)
    if k > 0:
        dead = [n_ctx_rows - 1 - i for i in range(k)] + [t - 1 - i for i in range(k)]
    else:
        dead = [i for i in range(-k)] + [n_ctx_rows + i for i in range(-k)]
    return jnp.where(_row_mask(x.shape, dead), 0.0, rolled)


def _conv(x, w, b, n_ctx_rows):
    y = b
    for k in range(CONV_WIDTH):
        y = y + _shift_rows(x, k - 1, n_ctx_rows) * w[k:k + 1, :]
    return y


def _gates(xc_bf, w_r, b_r, w_i, b_i, lam):
    r = jax.nn.sigmoid(jnp.dot(xc_bf, w_r.astype(BF16), preferred_element_type=F32) + b_r)
    i = jax.nn.sigmoid(jnp.dot(xc_bf, w_i.astype(BF16), preferred_element_type=F32) + b_i)
    log_a = -LRU_C * r * _softplus(-lam)
    a = jnp.exp(log_a)
    mult = jnp.sqrt(-_expm1_nonpos(2.0 * log_a))
    return r, i, a, mult


def _rnn_specs(t, xr_off):
    xb = xr_off // LANES
    return dict(
        zcol=pl.BlockSpec((t, LANES), lambda j: (0, xb + j)), col=pl.BlockSpec((t, LANES), lambda j: (0, j)),
        conv_w=pl.BlockSpec((CONV_WIDTH, LANES), lambda j: (0, j)), vec=pl.BlockSpec((1, LANES), lambda j: (0, j)),
        gate_w=pl.BlockSpec((2, 1, LANES, LANES), lambda j: (0, j, 0, 0)), two=pl.BlockSpec((2, LANES), lambda j: (0, j)))


def _rnn_prep(z, xr_off, conv_w, conv_b, w_rg, b_rg, w_ig, b_ig, lam, n_ctx_rows, side=None):
    t = z.shape[0]
    d = conv_b.shape[1]
    sp = _rnn_specs(t, xr_off)

    def body(z_ref, cw_ref, cb_ref, wr_ref, br_ref, wi_ref, bi_ref, lam_ref, xc_ref, af_ref, bf_ref, ab_ref, bb_ref):
        xc = _conv(z_ref[...], cw_ref[...], cb_ref[...], n_ctx_rows)
        xc_ref[...] = xc
        xc_bf = xc.astype(BF16)
        for dr, (a_ref, b_ref) in enumerate(((af_ref, bf_ref), (ab_ref, bb_ref))):
            _, i, a, mult = _gates(xc_bf, wr_ref[dr, 0], br_ref[dr:dr + 1, :], wi_ref[dr, 0], bi_ref[dr:dr + 1, :],
                                   lam_ref[dr:dr + 1, :])
            a_ref[...] = a
            b_ref[...] = mult * (i * xc)

    return _call(
        body, side=side, name="rnn_prep", grid=(d // LANES,),
        in_specs=[sp["zcol"], sp["conv_w"], sp["vec"], sp["gate_w"], sp["two"], sp["gate_w"], sp["two"], sp["two"]],
        out_specs=[sp["col"]] * 5, out_shape=[_sds((t, d))] * 5, sem=("parallel",),
    )(z, conv_w, conv_b, w_rg, b_rg, w_ig, b_ig, lam)


def _scan(a, b, *, order, post, n_ctx_rows, name, tc=256, side=None):
    t, d = a.shape
    tc = _tile(math.gcd(n_ctx_rows, t - n_ctx_rows), tc, SUBLANES)
    nt, nctx = t // tc, n_ctx_rows // tc
    nlat = nt - nctx
    b_lat_only = b.shape[0] != t
    up = order.endswith("up")

    def chunk(i):
        if order == "ctx_lat_up":
            return i
        if order == "lat_ctx_down":
            return nt - 1 - i
        if order == "ctx_lat_down":
            return jnp.where(i < nctx, nctx - 1 - i, nt - 1 - (i - nctx))
        return jnp.where(i < nlat, nctx + i, i - nlat)

    def body(a_ref, b_ref, o_ref, carry_ref):
        @pl.when(pl.program_id(0) == 0)
        def _():
            carry_ref[...] = jnp.zeros_like(carry_ref)

        live = jnp.where(chunk(pl.program_id(0)) >= nctx, 1.0, 0.0) if b_lat_only else None

        def group(gi, carry):
            base = pl.multiple_of((gi if up else tc // SUBLANES - 1 - gi) * SUBLANES, SUBLANES)
            for r in (range(SUBLANES) if up else range(SUBLANES - 1, -1, -1)):
                a_r = a_ref[pl.ds(base + r, 1), :]
                b_r = b_ref[pl.ds(base + r, 1), :]
                if live is not None:
                    b_r = b_r * live
                if post:
                    out = b_r + carry
                    carry = a_r * out
                else:
                    out = a_r * carry + b_r
                    carry = out
                o_ref[pl.ds(base + r, 1), :] = out
            return carry

        carry_ref[0:1, :] = lax.fori_loop(0, tc // SUBLANES, group, carry_ref[0:1, :])

    full = pl.BlockSpec((tc, d), lambda i: (chunk(i), 0))
    b_spec = pl.BlockSpec((tc, d), lambda i: (jnp.maximum(chunk(i) - nctx, 0), 0)) if b_lat_only else full
    return _call(
        body, side=side, name=name, grid=(nt,), in_specs=[full, b_spec], out_specs=full, out_shape=_sds((t, d)),
        scratch_shapes=[pltpu.VMEM((SUBLANES, d), F32)], sem=("arbitrary",),
    )(a, b)


def _rnn_bwd(z, xr_off, xc, g_f, g_b, h_f, h_b, conv_w, w_rg, b_rg, w_ig, b_ig, lam, dz, n_ctx_rows, side=None):
    t, d = xc.shape
    sp = _rnn_specs(t, xr_off)
    tn_dims = (((0,), (0,)), ((), ()))
    nt_dims = (((1,), (1,)), ((), ()))

    def body(z_ref, xc_ref, gf_ref, gb_ref, hf_ref, hb_ref, cw_ref, wr_ref, br_ref, wi_ref, bi_ref, lam_ref, _,
             dxr_ref, dwr_ref, dwi_ref, sums_ref):
        xc_ = xc_ref[...]
        xc_bf = xc_.astype(BF16)
        dxc = jnp.zeros_like(xc_)
        sums = [None] * 6
        for dr, (g_ref, h_ref) in enumerate(((gf_ref, hf_ref), (gb_ref, hb_ref))):
            w_r, w_i, lam_ = wr_ref[dr, 0], wi_ref[dr, 0], lam_ref[dr:dr + 1, :]
            r, i, a, mult = _gates(xc_bf, w_r, br_ref[dr:dr + 1, :], w_i, bi_ref[dr:dr + 1, :], lam_)
            g = g_ref[...]
            h = h_ref[...]
            if dr == 0:
                h_prev = jnp.where(_row_mask(h.shape, [0]), 0.0, pltpu.roll(h, 1, 0))
            else:
                h_prev = jnp.where(_row_mask(h.shape, [n_ctx_rows - 1]), 0.0, pltpu.roll(h, t - 1, 0))
            d_mult = g * i * xc_
            d_i = g * mult * xc_
            dxc = dxc + g * mult * i
            d_log_a = g * h_prev * a - d_mult * a * a / mult
            sp_ = _softplus(-lam_)
            d_r = d_log_a * (-LRU_C) * sp_
            d_sp = _colsum(d_log_a * (-LRU_C) * r)
            du_r = (d_r * r * (1.0 - r))
            du_i = (d_i * i * (1.0 - i))
            sums[dr] = _colsum(du_r)
            sums[2 + dr] = _colsum(du_i)
            sums[4 + dr] = d_sp * (-jax.nn.sigmoid(-lam_))
            du_r_bf, du_i_bf = du_r.astype(BF16), du_i.astype(BF16)
            dwr_ref[dr, 0] = lax.dot_general(xc_bf, du_r_bf, tn_dims, preferred_element_type=F32).astype(BF16)
            dwi_ref[dr, 0] = lax.dot_general(xc_bf, du_i_bf, tn_dims, preferred_element_type=F32).astype(BF16)
            dxc = dxc + lax.dot_general(du_r_bf, w_r.astype(BF16), nt_dims, preferred_element_type=F32)
            dxc = dxc + lax.dot_general(du_i_bf, w_i.astype(BF16), nt_dims, preferred_element_type=F32)
        xr = z_ref[...]
        cw = cw_ref[...]
        dxr = jnp.zeros_like(dxc)
        rows = list(sums)
        for k in range(CONV_WIDTH):
            dxr = dxr + _shift_rows(dxc, 1 - k, n_ctx_rows) * cw[k:k + 1, :]
            rows.append(_colsum(dxc * _shift_rows(xr, k - 1, n_ctx_rows)))
        rows.append(_colsum(dxc))
        dxr_ref[...] = dxr.astype(BF16)
        sums_ref[...] = jnp.zeros_like(sums_ref)
        for k, row in enumerate(rows):
            sums_ref[k:k + 1, :] = row

    return _call(
        body, side=side, name="rnn_bwd", grid=(d // LANES,),
        in_specs=[sp["zcol"]] + [sp["col"]] * 5 + [sp["conv_w"], sp["gate_w"], sp["two"], sp["gate_w"], sp["two"],
                                                  sp["two"], ANY],
        out_specs=[sp["zcol"], sp["gate_w"], sp["gate_w"], pl.BlockSpec((16, LANES), lambda j: (0, j))],
        out_shape=[_sds(dz.shape, BF16), _sds(w_rg.shape, BF16), _sds(w_ig.shape, BF16), _sds((16, d))],
        input_output_aliases={12: 0}, sem=("parallel",),
    )(z, xc, g_f, g_b, h_f, h_b, conv_w, w_rg, b_rg, w_ig, b_ig, lam, dz)


def _tiles2d(s, d, tr, tcol):
    return (s // tr, d // tcol), pl.BlockSpec((tr, tcol), lambda i, j: (i, j))


def _zspec(tr, tcol, row_off, col_off):
    rb, cb = row_off // tr, col_off // tcol
    return pl.BlockSpec((tr, tcol), lambda i, j: (rb + i, cb + j))


def _rnn_gate_fwd(h_f, h_b, z, xg_off, n_ctx_rows, tr, tcol, side=None):
    t, d = h_f.shape
    s = t - n_ctx_rows
    grid, out = _tiles2d(s, d, tr, tcol)
    hs = _zspec(tr, tcol, n_ctx_rows, 0)

    def body(hf_ref, hb_ref, xg_ref, u_ref):
        u_ref[...] = ((hf_ref[...] + hb_ref[...]) * _gelu(xg_ref[...])).astype(BF16)

    return _call(body, side=side, name="rnn_gate_fwd", grid=grid,
                 in_specs=[hs, hs, _zspec(tr, tcol, n_ctx_rows, xg_off)], out_specs=out, out_shape=_sds((s, d), BF16),
                 sem=("parallel", "parallel"))(h_f, h_b, z)


def _rnn_gate_bwd(d_u, h_f, h_b, z, xg_off, dz, n_ctx_rows, tr, tcol, side=None):
    t, d = h_f.shape
    s = t - n_ctx_rows
    grid, out = _tiles2d(s, d, tr, tcol)
    hs = _zspec(tr, tcol, n_ctx_rows, 0)
    window = _zspec(tr, tcol, n_ctx_rows, xg_off)

    def body(du_ref, hf_ref, hb_ref, xg_ref, _, dr_ref, dxg_ref):
        du = du_ref[...]
        xg = xg_ref[...]
        dr_ref[...] = du * _gelu(xg)
        dxg_ref[...] = (du * (hf_ref[...] + hb_ref[...]) * _gelu_grad(xg)).astype(BF16)

    return _call(body, side=side, name="rnn_gate_bwd", grid=grid, in_specs=[out, hs, hs, window, ANY],
                 out_specs=[out, window], out_shape=[_sds((s, d)), _sds(dz.shape, BF16)], input_output_aliases={4: 1},
                 sem=("parallel", "parallel"))(d_u, h_f, h_b, z, dz)


def _merge_fwd(y_attn, y_rnn, z, gl_off, n_ctx_rows, tr, tcol):
    s, d = y_attn.shape
    grid, out = _tiles2d(s, d, tr, tcol)

    def body(ya_ref, yr_ref, ga_ref, gr_ref, o_ref):
        o_ref[...] = (jax.nn.sigmoid(ga_ref[...]) * ya_ref[...] + jax.nn.sigmoid(gr_ref[...]) * yr_ref[...]).astype(BF16)

    return pl.pallas_call(
        body, name="merge_fwd", grid=grid,
        in_specs=[out, out, _zspec(tr, tcol, n_ctx_rows, gl_off), _zspec(tr, tcol, n_ctx_rows, gl_off + d)],
        out_specs=out, out_shape=_sds((s, d), BF16), compiler_params=_params("parallel", "parallel"),
    )(y_attn, y_rnn, z, z)


def _merge_bwd(d_mrg, y, z, gl_off, dz, n_ctx_rows, tr, tcol, name):
    s, d = y.shape
    grid, out = _tiles2d(s, d, tr, tcol)
    window = _zspec(tr, tcol, n_ctx_rows, gl_off)

    def body(dm_ref, y_ref, gl_ref, _, dy_ref, dgl_ref):
        dm = dm_ref[...]
        g = jax.nn.sigmoid(gl_ref[...])
        dy_ref[...] = (dm * g).astype(BF16)
        dgl_ref[...] = (dm * y_ref[...] * g * (1.0 - g)).astype(BF16)

    return _call(body, name=name, grid=grid, in_specs=[out, out, window, ANY], out_specs=[out, window],
                 out_shape=[_sds((s, d), BF16), _sds(dz.shape, BF16)], input_output_aliases={3: 1},
                 sem=("parallel", "parallel"))(d_mrg, y, z, dz)


def _sq_relu(up, tr, tcol, side=None):
    grid, out = _tiles2d(*up.shape, _tile(up.shape[0], 2 * tr, 16), _tile(up.shape[1], 4 * tcol, LANES))

    def body(u_ref, o_ref):
        r = jnp.maximum(u_ref[...], 0.0)
        o_ref[...] = (r * r).astype(BF16)

    return _call(body, side=side, name="sq_relu", grid=grid, in_specs=[out], out_specs=out,
                 out_shape=_sds(up.shape, BF16), sem=("parallel", "parallel"))(up)


def _cast_into_window(w, chip, col_sharded, name):
    r, c = w.shape
    tr, tcol = _tile(r, 512, 16), _tile(c, 1024, LANES)
    nrb, ncb = r // tr, c // tcol

    def body(chip_ref, w_ref, o_ref):
        o_ref[...] = w_ref[...].astype(BF16)

    if col_sharded:
        omap = lambda i, j, chip_ref: (i, chip_ref[0] * ncb + j)
    else:
        omap = lambda i, j, chip_ref: (chip_ref[0] * nrb + i, j)
    return pl.pallas_call(
        body, name=name,
        grid_spec=pltpu.PrefetchScalarGridSpec(
            num_scalar_prefetch=1, grid=(nrb, ncb),
            in_specs=[pl.BlockSpec((tr, tcol), lambda i, j, chip_ref: (i, j))], out_specs=pl.BlockSpec((tr, tcol), omap)),
        out_shape=_sds((r, c * N_CHIPS) if col_sharded else (r * N_CHIPS, c), BF16),
        compiler_params=_params("parallel", "parallel"),
    )(chip, w)


def _sum_leading(parts, name):
    n, r, c = parts.shape
    tr, tcol = _tile(r, 512, SUBLANES), _tile(c, 1024, LANES)

    def body(p_ref, o_ref):
        tot = p_ref[0]
        for k in range(1, n):
            tot = tot + p_ref[k]
        o_ref[...] = tot

    return pl.pallas_call(
        body, name=name, grid=(r // tr, c // tcol), in_specs=[pl.BlockSpec((n, tr, tcol), lambda i, j: (0, i, j))],
        out_specs=pl.BlockSpec((tr, tcol), lambda i, j: (i, j)), out_shape=_sds((r, c)),
        compiler_params=_params("parallel", "parallel"),
    )(parts)


def _add_half(full, other, core, split_rows, name):
    r, c = other.shape
    tr, tcol = _tile(r, 512, 16), _tile(c, 1024, LANES)
    nrb, ncb = r // tr, c // tcol

    def body(core_ref, f_ref, o_ref, out_ref):
        out_ref[...] = (f_ref[...].astype(F32) + o_ref[...].astype(F32)).astype(out_ref.dtype)

    if split_rows:
        fmap = lambda i, j, core_ref: (core_ref[0] * nrb + i, j)
    else:
        fmap = lambda i, j, core_ref: (i, core_ref[0] * ncb + j)
    same = lambda i, j, core_ref: (i, j)
    return pl.pallas_call(
        body, name=name,
        grid_spec=pltpu.PrefetchScalarGridSpec(
            num_scalar_prefetch=1, grid=(nrb, ncb),
            in_specs=[pl.BlockSpec((tr, tcol), fmap), pl.BlockSpec((tr, tcol), same)],
            out_specs=pl.BlockSpec((tr, tcol), same)),
        out_shape=_sds((r, c), BF16), compiler_params=_params("parallel", "parallel"),
    )(core, full, other)


def _sum_regions(pair, got, place, col_sharded, name):
    _, r, c = got.shape
    tr, tcol = _tile(r, 512, 16), _tile(c, 1024, LANES)
    nrb, ncb = r // tr, c // tcol

    def body(place_ref, p_ref, g_ref, out_ref):
        tot = p_ref[...].astype(F32)
        for k in range(N_CHIPS - 1):
            tot = tot + g_ref[k].astype(F32)
        out_ref[...] = tot

    if col_sharded:
        pmap = lambda i, j, pr: (i, pr[0] * ncb + j)
        omap = lambda i, j, pr: (pr[1] * nrb + i, j)
        out_shape = (2 * r, c)
    else:
        pmap = lambda i, j, pr: (pr[0] * nrb + i, j)
        omap = lambda i, j, pr: (i, pr[1] * ncb + j)
        out_shape = (r, 2 * c)
    return pl.pallas_call(
        body, name=name,
        grid_spec=pltpu.PrefetchScalarGridSpec(
            num_scalar_prefetch=1, grid=(nrb, ncb),
            in_specs=[pl.BlockSpec((tr, tcol), pmap), pl.BlockSpec((N_CHIPS - 1, tr, tcol), lambda i, j, pr: (0, i, j))],
            out_specs=pl.BlockSpec((tr, tcol), omap)),
        out_shape=_sds(out_shape), compiler_params=_params("parallel", "parallel"),
    )(place, pair, got)


def _adamw(w, g, m, v, name):
    r, c = w.shape
    tr, tcol = _tile(r, 512, SUBLANES), _tile(c, 1024, LANES)
    blk = pl.BlockSpec((tr, tcol), lambda i, j: (i, j))

    def body(w_ref, g_ref, m_ref, v_ref, d_ref, nm_ref, nv_ref, g_out_ref):
        g_ = g_ref[...]
        g_out_ref[...] = g_
        m_ = ADAM_B1 * m_ref[...] + (1.0 - ADAM_B1) * g_
        v_ = ADAM_B2 * v_ref[...] + (1.0 - ADAM_B2) * (g_ * g_)
        m_hat = m_ / (1.0 - ADAM_B1 ** ADAM_STEP)
        v_hat = v_ / (1.0 - ADAM_B2 ** ADAM_STEP)
        d_ref[...] = -ADAM_LR * (m_hat / (jnp.sqrt(v_hat) + ADAM_EPS) + ADAM_WD * w_ref[...])
        nm_ref[...] = m_
        nv_ref[...] = v_

    return _call(body, name=name, grid=(r // tr, c // tcol), in_specs=[blk] * 4, out_specs=[blk] * 4,
                 out_shape=[_sds((r, c))] * 4, sem=("parallel", "parallel"))(w, g, m, v)


def _place():
    x, y, c = lax.axis_index("x"), lax.axis_index("y"), lax.axis_index("c")
    chips = [(1 - x, y), (x, 1 - y), (1 - x, 1 - y)]
    return x, y, c, chips


def _all_gather8(blk, name):
    m, n = blk.shape

    def body(x_ref, out_ref, send_sems, recv_sems, local_sem):
        x, y, c, chips = _place()
        me, sibling = (x, y, c), (x, y, 1 - c)

        def rows(px, py, pc):
            return out_ref.at[pl.ds((4 * px + 2 * py + pc) * m, m), :]

        def copy(k, block, to, src=None):
            return pltpu.make_async_remote_copy(
                src_ref=rows(*block) if src is None else src, dst_ref=rows(*block), send_sem=send_sems.at[k],
                recv_sem=recv_sems.at[k], device_id=to, device_id_type=MESH_ID)

        mine = pltpu.make_async_copy(x_ref, rows(*me), local_sem)
        mine.start()
        first = [copy(0, me, sibling, src=x_ref)]
        first += [copy(1 + j, me, (*chip, c), src=x_ref) for j, chip in enumerate(chips)]
        for cp in first:
            cp.start()
        passed = [copy(4 + j, (*chip, c), sibling) for j, chip in enumerate(chips)]
        for j, chip in enumerate(chips):
            copy(1 + j, (*chip, c), me).wait_recv()
            passed[j].start()
        copy(0, sibling, me).wait_recv()
        for j, chip in enumerate(chips):
            copy(4 + j, (*chip, 1 - c), me).wait_recv()
        for cp in first + passed:
            cp.wait_send()
        mine.wait()

    return pl.pallas_call(
        body, name=name, out_shape=_sds((N_DEV * m, n), blk.dtype), in_specs=[ANY], out_specs=ANY,
        scratch_shapes=[pltpu.SemaphoreType.DMA((7,)), pltpu.SemaphoreType.DMA((7,)), pltpu.SemaphoreType.DMA],
    )(blk)


def _half(ref, core, split_rows):
    r, c = ref.shape
    if split_rows:
        return ref.at[pl.ds(core * (r // 2), r // 2), :]
    return ref.at[:, pl.ds(core * (c // 2), c // 2)]


def _chip_block(ref, j, col_sharded):
    r, c = ref.shape
    if col_sharded:
        return ref.at[:, pl.ds(j * (c // N_CHIPS), c // N_CHIPS)]
    return ref.at[pl.ds(j * (r // N_CHIPS), r // N_CHIPS), :]


def _rows_part(ref, part):
    lo, hi, n = part
    r = ref.shape[0]
    return ref if (lo, hi) == (0, n) else ref.at[pl.ds(lo * (r // n), (hi - lo) * (r // n)), :]


def _copy(send_sems, recv_sems, k, src, dst, to):
    return pltpu.make_async_remote_copy(src_ref=src, dst_ref=dst, send_sem=send_sems.at[k], recv_sem=recv_sems.at[k],
                                        device_id=to, device_id_type=MESH_ID)


def _in_place(arrays):
    return tuple(arrays), tuple(_sds(a.shape, a.dtype) for a in arrays), {i: i for i in range(len(arrays))}


def _gather_ici(fulls, col_sharded, part=(0, 1, 1)):
    nw = len(fulls)

    def build(_, refs, send_sems, recv_sems, sem0):
        x, y, c, chips = _place()
        sends, recvs = [], []
        for w in range(nw):
            win = lambda j: _rows_part(_half(_chip_block(refs[w], j, col_sharded[w]), c, True), part)
            for k, (cx, cy) in enumerate(chips):
                sem = sem0 + 3 * w + k
                sends.append(_copy(send_sems, recv_sems, sem, win(2 * x + y), win(2 * x + y), (cx, cy, c)))
                recvs.append(_copy(send_sems, recv_sems, sem, win(2 * cx + cy), win(2 * cx + cy), (cx, cy, c)))
        return sends, recvs

    return _Side(*_in_place(fulls), 3 * nw, build)


def _gather_d2d(fulls, col_sharded):
    nw = len(fulls)

    def build(_, refs, send_sems, recv_sems, sem0):
        x, y, c, chips = _place()
        sends, recvs = [], []
        for w in range(nw):
            win = lambda j, core: _half(_chip_block(refs[w], j, col_sharded[w]), core, True)
            for k, (cx, cy) in enumerate(chips):
                sem = sem0 + 3 * w + k
                sends.append(_copy(send_sems, recv_sems, sem, win(2 * cx + cy, c), win(2 * cx + cy, c), (x, y, 1 - c)))
                recvs.append(_copy(send_sems, recv_sems, sem, win(2 * cx + cy, 1 - c), win(2 * cx + cy, 1 - c),
                                   (x, y, 1 - c)))
        return sends, recvs

    return _Side(*_in_place(fulls), 3 * nw, build)


def _exchange(side, name):
    return _call(None, side=side, name=name)()[1]


def _swap_halves(grads, col_sharded):
    nw = len(grads)
    out_shapes = [_sds((g.shape[0] // 2, g.shape[1]) if col else (g.shape[0], g.shape[1] // 2), g.dtype)
                  for g, col in zip(grads, col_sharded)]

    def build(g_refs, o_refs, send_sems, recv_sems, sem0):
        x, y, c, _ = _place()
        copies = [_copy(send_sems, recv_sems, sem0 + w, _half(g_refs[w], 1 - c, col_sharded[w]), o_refs[w],
                        (x, y, 1 - c)) for w in range(nw)]
        return copies, copies

    return _Side(tuple(grads), tuple(out_shapes), {}, nw, build)


def _scatter_regions(pairs, col_sharded, part=(0, 1, 1), into=None):
    nw = len(pairs)

    def region_shape(p, col):
        return (p.shape[0], p.shape[1] // N_CHIPS) if col else (p.shape[0] // N_CHIPS, p.shape[1])

    out_shapes = tuple(_sds((N_CHIPS - 1, *region_shape(p, col)), p.dtype) for p, col in zip(pairs, col_sharded))

    def build(refs, o_refs, send_sems, recv_sems, sem0):
        x, y, c, chips = _place()
        copies = []
        for w in range(nw):
            for k, (cx, cy) in enumerate(chips):
                copies.append(_copy(
                    send_sems, recv_sems, sem0 + 3 * w + k,
                    _rows_part(_chip_block(refs[w], 2 * cx + cy, col_sharded[w]), part),
                    _rows_part(o_refs[w].at[k], part), (cx, cy, c)))
        return copies, copies

    if into is None:
        return _Side(tuple(pairs), out_shapes, {}, 3 * nw, build)
    return _Side((*pairs, *into), out_shapes, {nw + w: w for w in range(nw)}, 3 * nw, build)


def _join_halves(halves, col_sharded):
    nw = len(halves)

    def build(_, refs, send_sems, recv_sems, sem0):
        x, y, c, _ = _place()
        sends, recvs = [], []
        for w in range(nw):
            mine, theirs = _half(refs[w], c, col_sharded[w]), _half(refs[w], 1 - c, col_sharded[w])
            sends.append(_copy(send_sems, recv_sems, sem0 + w, mine, mine, (x, y, 1 - c)))
            recvs.append(_copy(send_sems, recv_sems, sem0 + w, theirs, theirs, (x, y, 1 - c)))
        return sends, recvs

    return _Side(*_in_place(halves), nw, build)


def _part_rows(size):
    return -(-size // (SUBLANES * LANES)) * SUBLANES


def _pack(arrays, pad_rows_to=SUBLANES):
    flat = [jnp.pad(a.reshape(-1), (0, _part_rows(a.size) * LANES - a.size)).reshape(-1, LANES) for a in arrays]
    rows = sum(f.shape[0] for f in flat)
    pad = (-rows) % pad_rows_to
    if pad:
        flat.append(jnp.zeros((pad, LANES), F32))
    return jnp.concatenate(flat, axis=0)


def _unpack(packed, shapes):
    out, r = [], 0
    for shp in shapes:
        size = math.prod(shp)
        out.append(packed[r:r + _part_rows(size)].reshape(-1)[:size].reshape(shp))
        r += _part_rows(size)
    return out


def _rope_tables(n_ctx_rows, s):
    rows = s // GRID_W
    row_idx = jnp.repeat(jnp.arange(rows), GRID_W)
    col_idx = jnp.tile(jnp.arange(GRID_W), rows)
    n_freq = LANES // 4
    inv_freq = ROPE_THETA ** (-jnp.arange(n_freq, dtype=F32) / n_freq)
    ang = jnp.concatenate([row_idx.astype(F32)[:, None] * inv_freq, col_idx.astype(F32)[:, None] * inv_freq], axis=-1)
    cos = jnp.repeat(jnp.cos(ang), 2, axis=-1)
    sin = jnp.repeat(jnp.sin(ang), 2, axis=-1) * jnp.tile(jnp.array([-1.0, 1.0], F32), LANES // 2)
    cos = jnp.concatenate([jnp.ones((n_ctx_rows, LANES), F32), cos], axis=0)
    sin = jnp.concatenate([jnp.zeros((n_ctx_rows, LANES), F32), sin], axis=0)
    return cos, sin


WEIGHT_NAMES = ['c_ctx', 'w_mod', 'b_mod', 'g_mix', 'g_mlp', 'w_in', 'q_gain', 'k_gain', 'conv_w', 'conv_b', 'w_rg',
                'b_rg', 'w_ig', 'b_ig', 'lru_lambda', 'w_o_attn', 'w_o_rnn', 'w_out', 'w_up', 'w_down', 'g_final']
BIG = ['w_in', 'w_o_attn', 'w_o_rnn', 'w_out', 'w_up', 'w_down']
BIG_COL_SHARDED = [True, False, False, False, True, False]
GATES = ['w_rg', 'w_ig']
SMALL = ['c_ctx', 'b_mod', 'g_mix', 'g_mlp', 'q_gain', 'k_gain', 'conv_b', 'g_final',
         'conv_w', 'b_rg', 'b_ig', 'lru_lambda']


def kernel(x, c, ctx, c_ctx, w_mod, b_mod, g_mix, g_mlp, w_in, q_gain, k_gain, conv_w, conv_b, w_rg, b_rg, w_ig, b_ig, lru_lambda, w_o_attn, w_o_rnn, w_out, w_up, w_down, g_final, loss_target, m_c_ctx, m_w_mod, m_b_mod, m_g_mix, m_g_mlp, m_w_in, m_q_gain, m_k_gain, m_conv_w, m_conv_b, m_w_rg, m_b_rg, m_w_ig, m_b_ig, m_lru_lambda, m_w_o_attn, m_w_o_rnn, m_w_out, m_w_up, m_w_down, m_g_final, v_c_ctx, v_w_mod, v_b_mod, v_g_mix, v_g_mlp, v_w_in, v_q_gain, v_k_gain, v_conv_w, v_conv_b, v_w_rg, v_b_rg, v_w_ig, v_b_ig, v_lru_lambda, v_w_o_attn, v_w_o_rnn, v_w_out, v_w_up, v_w_down, v_g_final):
    given = dict(locals())
    weights = {n: given[n] for n in WEIGHT_NAMES}
    moms = {n: given["m_" + n] for n in WEIGHT_NAMES}
    vars_ = {n: given["v_" + n] for n in WEIGHT_NAMES}

    s, d = x.shape[1], x.shape[2]
    n_ctx = ctx.shape[1]
    t = n_ctx + s
    hd = q_gain.shape[1]
    assert hd == LANES and w_rg.shape[-1] == LANES
    attn_w = w_o_attn.shape[1] * N_CHIPS
    n_in = w_in.shape[2] * N_CHIPS
    kv_w = (n_in - attn_w - 4 * d) // 2
    group = attn_w // kv_w
    k_off, v_off, xr_off = attn_w, attn_w + kv_w, attn_w + 2 * kv_w
    xg_off, gl_off = xr_off + d, xr_off + 2 * d
    d_mod = N_MOD * d
    tr = _tile(math.gcd(n_ctx, s), 256, 16)
    tcol = _tile(math.gcd(d, xr_off), 1024, LANES)
    xi, yi, ci = lax.axis_index("x"), lax.axis_index("y"), lax.axis_index("c")
    chip = 2 * xi + yi
    core = ci.astype(jnp.int32).reshape(1)

    sharded_small = [conv_w[0], b_rg[0], b_ig[0], lru_lambda[0]]
    pack0 = _pack([c[0]] + sharded_small)
    got0 = _all_gather8(pack0, "gather_small_inputs").reshape(N_DEV, -1, LANES)
    c_all = got0[:, :_part_rows(d)].reshape(N_DEV, -1)[:, :d]
    per_chip = [_unpack(got0[2 * j, _part_rows(d):], [a.shape for a in sharded_small]) for j in range(N_CHIPS)]
    conv_w_f, b_rg_f, b_ig_f, lam_f = (jnp.concatenate([per_chip[j][i] for j in range(N_CHIPS)], axis=-1)
                                       for i in range(4))
    c16 = jnp.concatenate([c_all, c_ctx[None, :], jnp.zeros((16 - N_DEV - 1, d), F32)], axis=0)
    b_mod_shard = lax.dynamic_slice(b_mod, (0, chip * (d_mod // N_CHIPS)), (1, d_mod // N_CHIPS))
    mod_part, silu16 = _mod_fwd(c16, w_mod[0], b_mod_shard)
    mod_all = _all_gather8(mod_part, "gather_mod").reshape(N_DEV, 16, d_mod // N_CHIPS)
    mod16 = jnp.concatenate([mod_all[2 * j] for j in range(N_CHIPS)], axis=-1)
    me = 4 * xi + 2 * yi + ci
    mod_lat = lax.dynamic_slice(mod16, (me, 0), (1, d_mod)).reshape(N_MOD, d)
    mod_ctx = mod16[N_DEV].reshape(N_MOD, d)
    mod4 = jnp.stack([mod_ctx[0], mod_ctx[1], mod_lat[0], mod_lat[1]])
    mod3 = jnp.stack([mod_lat[2], mod_lat[3], mod_lat[4]])
    gate_f = mod_lat[5][None, :]

    chip_arr = chip.astype(jnp.int32).reshape(1)
    own = {n: _cast_into_window(weights[n][0], chip_arr, col, "cast_" + n) for n, col in zip(BIG, BIG_COL_SHARDED)}
    place = jnp.stack([chip, ci]).astype(jnp.int32)
    row3 = [False] * 3

    def pair_sum(n, full, other, col):
        return _add_half(full, other, core, col, "pair_sum_" + n)

    def chip_sum(n, pair, got, col):
        return _sum_regions(pair, got, place, col, "chip_sum_" + n)

    (w_in_f,) = _exchange(_gather_ici([own['w_in']], [True]), "gather_w_in_ici")
    (w_in_f,) = _exchange(_gather_d2d([w_in_f], [True]), "gather_w_in_d2d")
    xcat = jnp.concatenate([ctx[0], x[0]], axis=0)
    cos, sin = _rope_tables(n_ctx, s)
    h = _norm_mod_fwd(xcat, g_mix, mod4, n_ctx, tr)
    ici_up = lambda w, lo, hi: _gather_ici([w], [True], (lo, hi, 8))
    ici_down = lambda w, lo, hi: _gather_ici([w], [False], (lo, hi, 8))
    z, (w_oa_f, w_or_f, w_out_f) = _matmul(h, w_in_f, name="mm_in", side=_gather_ici(
        [own['w_o_attn'], own['w_o_rnn'], own['w_out']], row3))
    qr, (w_up_f, w_down_f) = _head_prep_fwd(z, 0, attn_w // LANES, q_gain, cos, sin, tr, "q_prep", side=_sides(
        ici_up(own['w_up'], 0, 1), ici_down(own['w_down'], 0, 1)))
    kr = _head_prep_fwd(z, k_off, kv_w // LANES, k_gain, cos, sin, tr, "k_prep")
    (attn_o, lse), (w_up_f, w_oa_f, w_or_f, w_out_f) = _attn_fwd(qr, kr, z, v_off, n_ctx, group, tr, side=_sides(
        ici_up(w_up_f, 1, 8), _gather_d2d([w_oa_f, w_or_f, w_out_f], row3)))
    (xc, a_f, bx_f, a_b, bx_b), (w_down_f, w_up_f) = _rnn_prep(
        z, xr_off, conv_w_f, conv_b, w_rg[0], b_rg_f, w_ig[0], b_ig_f, lam_f, n_ctx,
        side=_sides(ici_down(w_down_f, 1, 4), _gather_d2d([w_up_f], [True])))
    h_f = _scan(a_f, bx_f, order="ctx_lat_up", post=False, n_ctx_rows=n_ctx, name="scan_f")
    h_b = _scan(a_b, bx_b, order="ctx_lat_down", post=False, n_ctx_rows=n_ctx, name="scan_b")
    u = _rnn_gate_fwd(h_f, h_b, z, xg_off, n_ctx, tr, tcol)
    y_attn = _matmul(attn_o, w_oa_f, name="mm_o_attn")
    y_rnn = _matmul(u, w_or_f, name="mm_o_rnn")
    mrg = _merge_fwd(y_attn, y_rnn, z, gl_off, n_ctx, tr, tcol)
    mix = _matmul(mrg, w_out_f, name="mm_out")
    x1, h2 = _resid_norm_mod_fwd(x[0], mix, g_mlp, mod3, tr)
    up, (w_down_f,) = _matmul(h2, w_up_f, name="mm_up", side=ici_down(w_down_f, 4, 8))
    act, (w_down_f,) = _sq_relu(up, tr, tcol, side=_gather_d2d([w_down_f], [False]))
    down = _matmul(act, w_down_f, name="mm_down")
    dx2, d_down, sums_fin, loss_blk = _final_fwd_bwd(x1, down, loss_target[0], g_final[None, :], gate_f, tr)

    d_up = _matmul(d_down, w_down_f, tb=True, out_dtype=BF16, name="mm_d_up",
                   post=lambda d_act, up_: d_act * 2.0 * jnp.maximum(up_, 0.0), post_args=(up,))
    g_w_down = _matmul(act, d_down, ta=True, out_dtype=BF16, name="mm_g_down")

    def scatter(pairs, cols, lo, hi, into=None):
        return _scatter_regions(pairs, cols, (lo, hi, 8), into)

    dh2, (got,) = _matmul(d_up, w_up_f, tb=True, name="mm_d_h2", side=_swap_halves([g_w_down], [False]))
    p_down = pair_sum('w_down', g_w_down, got, False)
    g_w_up, got_down = _matmul(h2, d_up, ta=True, out_dtype=BF16, name="mm_g_up", side=scatter([p_down], [False], 0, 4))
    (dx1, d_mix, sums2), (got,) = _norm_mod_bwd2(x1, dh2, dx2, mix, g_mlp, mod3, tr, side=_swap_halves([g_w_up], [True]))
    p_up = pair_sum('w_up', g_w_up, got, True)
    d_mrg = _matmul(d_mix, w_out_f, tb=True, name="mm_d_mrg")
    g_w_out = _matmul(mrg, d_mix, ta=True, out_dtype=BF16, name="mm_g_out")
    dz = _dz_start(t, n_in, n_ctx)
    d_ya, dz = _merge_bwd(d_mrg, y_attn, z, gl_off, dz, n_ctx, tr, tcol, "merge_bwd_attn")
    d_yr, dz = _merge_bwd(d_mrg, y_rnn, z, gl_off + d, dz, n_ctx, tr, tcol, "merge_bwd_rnn")
    d_o = _matmul(d_ya, w_oa_f, tb=True, out_dtype=BF16, name="mm_d_o")
    g_w_oa = _matmul(attn_o, d_ya, ta=True, out_dtype=BF16, name="mm_g_o_attn")
    d_u = _matmul(d_yr, w_or_f, tb=True, name="mm_d_u")
    g_w_or = _matmul(u, d_yr, ta=True, out_dtype=BF16, name="mm_g_o_rnn")
    d_rnn, dz = _rnn_gate_bwd(d_u, h_f, h_b, z, xg_off, dz, n_ctx, tr, tcol)
    gs_f = _scan(a_f, d_rnn, order="lat_ctx_down", post=True, n_ctx_rows=n_ctx, name="scan_f_bwd")
    gs_b = _scan(a_b, d_rnn, order="lat_ctx_up", post=True, n_ctx_rows=n_ctx, name="scan_b_bwd")
    o_names, o_grads = ['w_o_attn', 'w_o_rnn', 'w_out'], [g_w_oa, g_w_or, g_w_out]
    (dz, g_w_rg, g_w_ig, sums_rnn), (got_down, *got_o) = _rnn_bwd(
        z, xr_off, xc, gs_f, gs_b, h_f, h_b, conv_w_f, w_rg[0], b_rg_f, w_ig[0], b_ig_f, lam_f, dz, n_ctx,
        side=_sides(scatter([p_down], [False], 4, 8, got_down), _swap_halves(o_grads, row3)))
    hs_down = chip_sum('w_down', p_down, got_down, False)
    p_o = [pair_sum(n, g, o, False) for n, g, o in zip(o_names, o_grads, got_o)]
    gate_cols = 8 * LANES if g_w_rg.size % (8 * LANES * N_CHIPS * 16) == 0 else 2 * LANES
    gate_rows = g_w_rg.size // gate_cols
    gate_grads = [g_w_rg.reshape(gate_rows, gate_cols), g_w_ig.reshape(gate_rows, gate_cols)]
    (dq, dk, dz), (got_up, got_oa, got_rg, got_ig, gs_down) = _attn_bwd(
        qr, kr, z, v_off, d_o, attn_o, lse, dz, n_ctx, group, tr, side=_sides(
            _scatter_regions([p_up], [True]), _scatter_regions(p_o[:1], row3[:1]), _swap_halves(gate_grads, row3[:2]),
            _join_halves([hs_down], [False])))
    hs_up = chip_sum('w_up', p_up, got_up, True)
    hs_oa = chip_sum('w_o_attn', p_o[0], got_oa, False)
    p_gate = [pair_sum(n, g, o, False) for n, g, o in zip(GATES, gate_grads, (got_rg, got_ig))]
    dz, g_q_gain = _head_prep_bwd(z, 0, attn_w // LANES, q_gain, cos, sin, dq, n_ctx, tr, dz, "q_prep_bwd")
    dz, g_k_gain = _head_prep_bwd(z, k_off, kv_w // LANES, k_gain, cos, sin, dk, 0, tr, dz, "k_prep_bwd")
    g_w_in, (got_or, got_out, got_rg, got_ig, gs_up, gs_oa) = _matmul(
        h, dz, ta=True, out_dtype=BF16, name="mm_g_in", side=_sides(
            _scatter_regions(p_o[1:] + p_gate, [False] * 4), _join_halves([hs_up], [True]), _join_halves([hs_oa], [False])))
    hs_late = [chip_sum(n, p, o, False)
               for n, p, o in zip(o_names[1:] + GATES, p_o[1:] + p_gate, (got_or, got_out, got_rg, got_ig))]
    (got,) = _exchange(_swap_halves([g_w_in], [True]), "swap_w_in")
    p_in = pair_sum('w_in', g_w_in, got, True)
    dh, got_in = _matmul(dz, w_in_f, tb=True, name="mm_d_h", side=scatter([p_in], [True], 0, 4))
    (got_in,) = _exchange(scatter([p_in], [True], 4, 8, got_in), "scatter_w_in_rest")
    grad_x, sums1 = _norm_mod_bwd1(xcat, dh, dx1, g_mix, mod4, n_ctx, tr)

    zeros_d = jnp.zeros((d,), F32)
    dmod_lat = jnp.concatenate([sums1[0], sums1[1], sums2[3], sums2[0], sums2[1], sums_fin[1]])
    dmod_ctx = jnp.concatenate([sums1[3], sums1[4]] + [zeros_d] * 4)
    small_parts = [dmod_lat, dmod_ctx, loss_blk[0, 0:1], sums1[2] + sums1[5], sums2[2], g_q_gain[0], g_k_gain[0],
                   sums_rnn[10], sums_fin[0], sums_rnn[6:10], sums_rnn[0:2], sums_rnn[2:4], sums_rnn[4:6]]
    pack1 = _pack(small_parts)
    got1 = _all_gather8(pack1, "gather_small_grads").reshape(N_DEV, -1, LANES)
    tot1 = _sum_leading(got1, "sum_small_grads")
    part_shapes = [a.shape for a in small_parts]
    (s_dmod_lat, s_dmod_ctx, s_loss, g_g_mix, g_g_mlp, g_q_gain, g_k_gain, g_conv_b, g_g_final,
     g_conv_w_f, g_b_rg_f, g_b_ig_f, g_lam_f) = _unpack(tot1, part_shapes)
    loss = s_loss[0]
    g_b_mod = (s_dmod_lat + s_dmod_ctx)[None, :]
    n_mod_rows = _part_rows(d_mod)
    dmod16 = jnp.concatenate([got1[:, :n_mod_rows].reshape(N_DEV, -1)[:, :d_mod], s_dmod_ctx[None, :],
                              jnp.zeros((16 - N_DEV - 1, d_mod), F32)], axis=0)
    dmod16_shard = lax.dynamic_slice(dmod16, (0, chip * (d_mod // N_CHIPS)), (16, d_mod // N_CHIPS))
    g_w_mod = _matmul(silu16, dmod16_shard, ta=True, name="mm_g_mod")
    dsilu_part = _matmul(dmod16_shard[N_DEV:], w_mod[0], tb=True, name="mm_d_silu")
    dsilu_all = _all_gather8(dsilu_part, "gather_d_silu").reshape(N_DEV, 8, d)
    g_c_ctx = _c_ctx_grad(dsilu_all, c_ctx[None, :])[0]

    def shard_of(full):
        w = full.shape[-1] // N_CHIPS
        return lax.dynamic_slice(full, (0, chip * w), (full.shape[0], w))

    grads = {
        'c_ctx': g_c_ctx, 'b_mod': g_b_mod, 'g_mix': g_g_mix[None, :], 'g_mlp': g_g_mlp[None, :],
        'q_gain': g_q_gain[None, :], 'k_gain': g_k_gain[None, :], 'conv_b': g_conv_b[None, :],
        'g_final': g_g_final,
        'conv_w': shard_of(g_conv_w_f)[None], 'b_rg': shard_of(g_b_rg_f)[None], 'b_ig': shard_of(g_b_ig_f)[None],
        'lru_lambda': shard_of(g_lam_f)[None], 'w_mod': g_w_mod[None],
    }

    delta, new_m, new_v = {}, {}, {}

    def adamw(n):
        shp = weights[n].shape
        as2d = (lambda a: a[0]) if n not in GATES else (lambda a: a.reshape(-1, LANES))
        dl, nm, nv, g = _adamw(as2d(weights[n]), as2d(grads[n]), as2d(moms[n]), as2d(vars_[n]), "adamw_" + n)
        delta[n], new_m[n], new_v[n], grads[n] = dl.reshape(shp), nm.reshape(shp), nv.reshape(shp), g.reshape(shp)

    hs_in = chip_sum('w_in', p_in, got_in, True)
    gs_in, gs_or, gs_out = _exchange(_join_halves([hs_in] + hs_late[:2], [True, False, False]), "join_rest")
    for n, g in zip(['w_in', 'w_o_attn', 'w_o_rnn', 'w_out', 'w_up', 'w_down'], [gs_in, gs_oa, gs_or, gs_out, gs_up, gs_down]):
        grads[n] = g[None]
    half_cols = gate_cols // 2
    mine = jnp.concatenate([lax.dynamic_slice(hs, (0, ci * half_cols), (gate_rows // N_CHIPS, half_cols))
                            for hs in hs_late[2:]], axis=0)
    gate_all = _all_gather8(mine, "gather_gate_grads")
    gate_all = gate_all.reshape(N_CHIPS, 2, len(GATES), gate_rows // N_CHIPS, half_cols)
    for i, n in enumerate(GATES):
        grads[n] = jnp.moveaxis(gate_all[:, :, i], 1, 2).reshape(weights[n].shape)
    for n in ['w_mod'] + BIG + GATES:
        adamw(n)
    small_shapes = [weights[n].shape for n in SMALL]
    packed = [_pack([src[n] for n in SMALL], 512) for src in (weights, grads, moms, vars_)]
    outs = _adamw(*packed, "adamw_small")
    for res, out in zip((delta, new_m, new_v), outs):
        for n, a in zip(SMALL, _unpack(out, small_shapes)):
            res[n] = a
    return (loss, grad_x[None], *[grads[n] for n in WEIGHT_NAMES], *[delta[n] for n in WEIGHT_NAMES],
            *[new_m[n] for n in WEIGHT_NAMES], *[new_v[n] for n in WEIGHT_NAMES])
```

```python
import functools
import math
from typing import Callable, NamedTuple

import jax
import jax.numpy as jnp
from jax import lax
from jax.experimental import pallas as pl
from jax.experimental.pallas import tpu as pltpu

F32 = jnp.float32
BF16 = jnp.bfloat16
MESH_ID = pl.DeviceIdType.MESH
ANY = pl.BlockSpec(memory_space=pl.ANY)

NORM_EPS = 1e-6
LRU_C = 8.0
GRID_W = 64
ROPE_THETA = 10000.0
N_MOD = 6
CONV_WIDTH = 4
ADAM_LR = 0.001
ADAM_B1 = 0.9
ADAM_B2 = 0.999
ADAM_EPS = 1e-08
ADAM_WD = 0.01
ADAM_STEP = 10

LANES = 128
SUBLANES = 8
V7X_VMEM_LIMIT = 48 * 1024 * 1024
N_CHIPS = 4
N_DEV = 8
GELU_C = math.sqrt(2.0 / math.pi)
GELU_A = 0.044715


def _tile(dim, pref, align):
    t = min(pref, dim)
    t -= t % align
    while t >= align:
        if dim % t == 0:
            return t
        t -= align
    return dim


def _params(*sem):
    return pltpu.CompilerParams(dimension_semantics=sem, vmem_limit_bytes=V7X_VMEM_LIMIT)


def _sds(shape, dtype=F32):
    return jax.ShapeDtypeStruct(shape, dtype)


class _Side(NamedTuple):
    operands: tuple
    results: tuple
    aliases: dict
    n_sems: int
    build: Callable


def _sides(*sides):
    ops, res, aliases, spans, n = [], [], {}, [], 0
    for s in sides:
        spans.append((len(ops), len(res), n))
        aliases.update({len(ops) + i: len(res) + j for i, j in s.aliases.items()})
        ops += s.operands
        res += s.results
        n += s.n_sems

    def build(op_refs, res_refs, send_sems, recv_sems, sem0):
        sends, recvs = [], []
        for s, (o, r, k) in zip(sides, spans):
            a, b = s.build(op_refs[o:o + len(s.operands)], res_refs[r:r + len(s.results)], send_sems, recv_sems,
                           sem0 + k)
            sends += a
            recvs += b
        return sends, recvs

    return _Side(tuple(ops), tuple(res), aliases, n, build)


def _call(body, *, side=None, sem=(), grid=(), in_specs=(), out_specs=(), out_shape=(), scratch_shapes=(), **kw):
    if side is None:
        return pl.pallas_call(body, grid=grid, in_specs=list(in_specs), out_specs=out_specs, out_shape=out_shape,
                              scratch_shapes=list(scratch_shapes), compiler_params=_params(*sem), **kw)
    aliases = kw.pop("input_output_aliases", {})
    many = isinstance(out_shape, (list, tuple))
    out_specs_l, out_shape_l = (list(out_specs), list(out_shape)) if many else ([out_specs], [out_shape])
    n_in, n_out, n_scr = len(in_specs), len(out_shape_l), len(scratch_shapes)
    n_op, n_res = len(side.operands), len(side.results)

    def hosted(*refs):
        ins, ops = refs[:n_in], refs[n_in:n_in + n_op]
        outs = refs[n_in + n_op:n_in + n_op + n_out]
        res = refs[n_in + n_op + n_out:n_in + n_op + n_out + n_res]
        scr = refs[n_in + n_op + n_out + n_res:-2]
        send_sems, recv_sems = refs[-2:]

        def start():
            for cp in side.build(ops, res, send_sems, recv_sems, 0)[0]:
                cp.start()

        def finish():
            sends, recvs = side.build(ops, res, send_sems, recv_sems, 0)
            for cp in recvs:
                cp.wait_recv()
            for cp in sends:
                cp.wait_send()

        if not grid:
            start()
            finish()
            return
        ids = [pl.program_id(a) for a in range(len(grid))]
        first = functools.reduce(jnp.logical_and, [i == 0 for i in ids])
        last = functools.reduce(jnp.logical_and, [i == g - 1 for i, g in zip(ids, grid)])
        pl.when(first)(start)
        body(*ins, *outs, *scr)
        pl.when(last)(finish)

    def run(*args):
        got = pl.pallas_call(
            hosted, grid=grid, in_specs=[*in_specs, *[ANY] * n_op], out_specs=[*out_specs_l, *[ANY] * n_res],
            out_shape=[*out_shape_l, *side.results],
            scratch_shapes=[*scratch_shapes, pltpu.SemaphoreType.DMA((side.n_sems,)),
                            pltpu.SemaphoreType.DMA((side.n_sems,))],
            input_output_aliases={**aliases, **{n_in + i: n_out + j for i, j in side.aliases.items()}},
            compiler_params=_params(*["arbitrary"] * len(grid)), **kw)(*args, *side.operands)
        own = list(got[:n_out]) if many else got[0]
        return own, list(got[n_out:])

    return run


def _matmul(a, b, *, ta=False, tb=False, out_dtype=F32, name, tm=1024, tn=1024, tk=2816, side=None, post=None,
            post_args=()):
    k_dim, m = a.shape if ta else a.shape[::-1]
    n, k2 = b.shape if tb else b.shape[::-1]
    assert k_dim == k2, (a.shape, b.shape, ta, tb)
    tm = _tile(m, tm, LANES if ta else 16)
    tn = _tile(n, tn, 16 if tb else LANES)
    tk = _tile(k_dim, tk, LANES)
    nk = k_dim // tk
    dims = (((0 if ta else 1,), (1 if tb else 0,)), ((), ()))
    if nk == 1:
        def whole(a_ref, b_ref, *rest):
            acc = lax.dot_general(a_ref[...].astype(BF16), b_ref[...].astype(BF16), dims, preferred_element_type=F32)
            if post is not None:
                acc = post(acc, *[r[...] for r in rest[:-1]])
            rest[-1][...] = acc.astype(rest[-1].dtype)

        a_spec = pl.BlockSpec((tk, tm), lambda i, j: (0, i)) if ta else pl.BlockSpec((tm, tk), lambda i, j: (i, 0))
        b_spec = pl.BlockSpec((tn, tk), lambda i, j: (j, 0)) if tb else pl.BlockSpec((tk, tn), lambda i, j: (0, j))
        o_spec = pl.BlockSpec((tm, tn), lambda i, j: (i, j))
        return _call(
            whole, side=side, name=name, grid=(m // tm, n // tn), in_specs=[a_spec, b_spec] + [o_spec] * len(post_args),
            out_specs=o_spec, out_shape=_sds((m, n), out_dtype), sem=("parallel", "parallel"),
        )(a, b, *post_args)
    assert post is None

    def body(a_ref, b_ref, o_ref, acc_ref):
        k = pl.program_id(2)

        @pl.when(k == 0)
        def _():
            acc_ref[...] = jnp.zeros_like(acc_ref)

        acc_ref[...] += lax.dot_general(a_ref[...].astype(BF16), b_ref[...].astype(BF16), dims,
                                        preferred_element_type=F32)

        @pl.when(k == nk - 1)
        def _():
            o_ref[...] = acc_ref[...].astype(o_ref.dtype)

    a_spec = pl.BlockSpec((tk, tm), lambda i, j, k: (k, i)) if ta else pl.BlockSpec((tm, tk), lambda i, j, k: (i, k))
    b_spec = pl.BlockSpec((tn, tk), lambda i, j, k: (j, k)) if tb else pl.BlockSpec((tk, tn), lambda i, j, k: (k, j))
    return _call(
        body, side=side, name=name, grid=(m // tm, n // tn, nk), in_specs=[a_spec, b_spec],
        out_specs=pl.BlockSpec((tm, tn), lambda i, j, k: (i, j)), out_shape=_sds((m, n), out_dtype),
        scratch_shapes=[pltpu.VMEM((tm, tn), F32)], sem=("parallel", "parallel", "arbitrary"),
    )(a, b)


def _silu(x):
    return x * jax.nn.sigmoid(x)


def _gelu(x):
    return 0.5 * x * (1.0 + jnp.tanh(GELU_C * (x + GELU_A * x * x * x)))


def _gelu_grad(x):
    t = jnp.tanh(GELU_C * (x + GELU_A * x * x * x))
    return 0.5 * (1.0 + t) + 0.5 * x * (1.0 - t * t) * GELU_C * (1.0 + 3.0 * GELU_A * x * x)


def _expm1_nonpos(x):
    series = x * (1.0 + x * (1.0 / 2 + x * (1.0 / 6 + x * (1.0 / 24 + x * (1.0 / 120 + x * (1.0 / 720 + x / 5040))))))
    return jnp.where(x > -0.25, series, jnp.exp(x) - 1.0)


def _softplus(x):
    return jnp.maximum(x, 0.0) + jnp.log1p(jnp.exp(-jnp.abs(x)))


def _rms_stats(x):
    return lax.rsqrt(jnp.mean(x * x, axis=-1, keepdims=True) + NORM_EPS)


def _rms_bwd(dxhat, xhat, rstd):
    return rstd * (dxhat - xhat * jnp.mean(dxhat * xhat, axis=-1, keepdims=True))


def _colsum(v):
    return jnp.sum(v, axis=0, keepdims=True)


def _mod_fwd(c16, w_mod, b_mod_shard):
    r, d = c16.shape
    n = w_mod.shape[1]
    tn = _tile(n, 512, LANES)

    def body(c_ref, w_ref, b_ref, o_ref, s_ref):
        s = _silu(c_ref[...])
        s_ref[...] = s
        o_ref[...] = jnp.dot(s.astype(BF16), w_ref[...].astype(BF16), preferred_element_type=F32) + b_ref[...]

    return pl.pallas_call(
        body, name="mod_fwd", grid=(n // tn,),
        in_specs=[pl.BlockSpec((r, d), lambda j: (0, 0)), pl.BlockSpec((d, tn), lambda j: (0, j)),
                  pl.BlockSpec((1, tn), lambda j: (0, j))],
        out_specs=[pl.BlockSpec((r, tn), lambda j: (0, j)), pl.BlockSpec((r, d), lambda j: (0, 0))],
        out_shape=[_sds((r, n)), _sds((r, d))], compiler_params=_params("arbitrary"),
    )(c16, w_mod, b_mod_shard)


def _c_ctx_grad(parts, c_ctx_row):
    d = c_ctx_row.shape[1]

    def body(p_ref, c_ref, o_ref):
        tot = p_ref[0, 0:1, :]
        for chip in range(1, N_CHIPS):
            tot = tot + p_ref[2 * chip, 0:1, :]
        c = c_ref[...]
        sg = jax.nn.sigmoid(c)
        o_ref[...] = tot * (sg * (1.0 + c * (1.0 - sg)))

    return pl.pallas_call(body, name="c_ctx_grad", out_shape=_sds((1, d)), compiler_params=_params())(parts, c_ctx_row)


def _norm_mod_fwd(xcat, g, mod4, n_ctx_rows, tr):
    t, d = xcat.shape
    nctx = n_ctx_rows // tr

    def body(x_ref, g_ref, mod_ref, h_ref):
        x = x_ref[...]
        n = x * _rms_stats(x) * g_ref[...]
        is_ctx = pl.program_id(0) < nctx
        sh = jnp.where(is_ctx, mod_ref[0:1, :], mod_ref[2:3, :])
        sc = jnp.where(is_ctx, mod_ref[1:2, :], mod_ref[3:4, :])
        h_ref[...] = (n * (1.0 + sc) + sh).astype(BF16)

    return pl.pallas_call(
        body, name="norm_mod_fwd", grid=(t // tr,),
        in_specs=[pl.BlockSpec((tr, d), lambda i: (i, 0)), pl.BlockSpec((1, d), lambda i: (0, 0)),
                  pl.BlockSpec((4, d), lambda i: (0, 0))],
        out_specs=pl.BlockSpec((tr, d), lambda i: (i, 0)), out_shape=_sds((t, d), BF16),
        compiler_params=_params("parallel"),
    )(xcat, g, mod4)


def _norm_mod_bwd1(xcat, dh, dx1, g, mod4, n_ctx_rows, tr, side=None):
    t, d = xcat.shape
    nctx = n_ctx_rows // tr
    s = t - n_ctx_rows

    def body(x_ref, dh_ref, dx1_ref, g_ref, mod_ref, dx_ref, sums_ref):
        i = pl.program_id(0)
        is_ctx = i < nctx

        @pl.when(i == 0)
        def _():
            sums_ref[...] = jnp.zeros_like(sums_ref)

        x = x_ref[...]
        dh_ = dh_ref[...]
        rstd = _rms_stats(x)
        xhat = x * rstd
        gg = g_ref[...]
        sc = jnp.where(is_ctx, mod_ref[1:2, :], mod_ref[3:4, :])
        dxhat = dh_ * (1.0 + sc) * gg
        dx_ref[...] = dx1_ref[...] + _rms_bwd(dxhat, xhat, rstd)
        part = [_colsum(dh_), _colsum(dh_ * xhat * gg), _colsum(dh_ * (1.0 + sc) * xhat)]

        @pl.when(is_ctx)
        def _():
            for k, row in enumerate(part):
                sums_ref[3 + k:4 + k, :] += row

        @pl.when(jnp.logical_not(is_ctx))
        def _():
            for k, row in enumerate(part):
                sums_ref[k:k + 1, :] += row

    lat = lambda i: (jnp.maximum(i - nctx, 0), 0)
    return _call(
        body, side=side, name="norm_mod_bwd1", grid=(t // tr,),
        in_specs=[pl.BlockSpec((tr, d), lambda i: (i, 0)), pl.BlockSpec((tr, d), lambda i: (i, 0)),
                  pl.BlockSpec((tr, d), lat), pl.BlockSpec((1, d), lambda i: (0, 0)),
                  pl.BlockSpec((4, d), lambda i: (0, 0))],
        out_specs=[pl.BlockSpec((tr, d), lat), pl.BlockSpec((8, d), lambda i: (0, 0))],
        out_shape=[_sds((s, d)), _sds((8, d))], sem=("arbitrary",),
    )(xcat, dh, dx1, g, mod4)


def _resid_norm_mod_fwd(x, mix, g, mod3, tr):
    s, d = x.shape

    def body(x_ref, mix_ref, g_ref, mod_ref, x1_ref, h_ref):
        x1 = x_ref[...] + mod_ref[0:1, :] * mix_ref[...]
        x1_ref[...] = x1
        n = x1 * _rms_stats(x1) * g_ref[...]
        h_ref[...] = (n * (1.0 + mod_ref[2:3, :]) + mod_ref[1:2, :]).astype(BF16)

    row = pl.BlockSpec((tr, d), lambda i: (i, 0))
    return pl.pallas_call(
        body, name="resid_norm_mod_fwd", grid=(s // tr,),
        in_specs=[row, row, pl.BlockSpec((1, d), lambda i: (0, 0)), pl.BlockSpec((3, d), lambda i: (0, 0))],
        out_specs=[row, row], out_shape=[_sds((s, d)), _sds((s, d), BF16)], compiler_params=_params("parallel"),
    )(x, mix, g, mod3)


def _norm_mod_bwd2(x1, dh2, dx2, mix, g, mod3, tr, side=None):
    s, d = x1.shape

    def body(x_ref, dh_ref, dx2_ref, mix_ref, g_ref, mod_ref, dx1_ref, dmix_ref, sums_ref):
        @pl.when(pl.program_id(0) == 0)
        def _():
            sums_ref[...] = jnp.zeros_like(sums_ref)

        x = x_ref[...]
        dh_ = dh_ref[...]
        rstd = _rms_stats(x)
        xhat = x * rstd
        gg = g_ref[...]
        sc = mod_ref[2:3, :]
        dx1 = dx2_ref[...] + _rms_bwd(dh_ * (1.0 + sc) * gg, xhat, rstd)
        dx1_ref[...] = dx1
        dmix_ref[...] = (dx1 * mod_ref[0:1, :]).astype(BF16)
        part = [_colsum(dh_), _colsum(dh_ * xhat * gg), _colsum(dh_ * (1.0 + sc) * xhat), _colsum(dx1 * mix_ref[...])]
        for k, row in enumerate(part):
            sums_ref[k:k + 1, :] += row

    row = pl.BlockSpec((tr, d), lambda i: (i, 0))
    return _call(
        body, side=side, name="norm_mod_bwd2", grid=(s // tr,),
        in_specs=[row, row, row, row, pl.BlockSpec((1, d), lambda i: (0, 0)), pl.BlockSpec((3, d), lambda i: (0, 0))],
        out_specs=[row, row, pl.BlockSpec((8, d), lambda i: (0, 0))],
        out_shape=[_sds((s, d)), _sds((s, d), BF16), _sds((8, d))], sem=("arbitrary",),
    )(x1, dh2, dx2, mix, g, mod3)


def _final_fwd_bwd(x1, down, target, g_final, gate, tr):
    s, d = x1.shape

    def body(x1_ref, down_ref, t_ref, g_ref, gate_ref, dx2_ref, ddown_ref, sums_ref, loss_ref):
        @pl.when(pl.program_id(0) == 0)
        def _():
            sums_ref[...] = jnp.zeros_like(sums_ref)
            loss_ref[...] = jnp.zeros_like(loss_ref)

        down_ = down_ref[...]
        gate_ = gate_ref[...]
        x2 = x1_ref[...] + gate_ * down_
        rstd = _rms_stats(x2)
        xhat = x2 * rstd
        gg = g_ref[...]
        err = xhat * gg - t_ref[...]
        loss_ref[...] += 0.5 * jnp.sum(jnp.mean(err * err, axis=-1, keepdims=True))
        dy = err * (1.0 / d)
        dx2 = _rms_bwd(dy * gg, xhat, rstd)
        dx2_ref[...] = dx2
        ddown_ref[...] = (dx2 * gate_).astype(BF16)
        sums_ref[0:1, :] += _colsum(dy * xhat)
        sums_ref[1:2, :] += _colsum(dx2 * down_)

    row = pl.BlockSpec((tr, d), lambda i: (i, 0))
    vec = pl.BlockSpec((1, d), lambda i: (0, 0))
    return pl.pallas_call(
        body, name="final_fwd_bwd", grid=(s // tr,), in_specs=[row, row, row, vec, vec],
        out_specs=[row, row, pl.BlockSpec((8, d), lambda i: (0, 0)), pl.BlockSpec((8, LANES), lambda i: (0, 0))],
        out_shape=[_sds((s, d)), _sds((s, d), BF16), _sds((8, d)), _sds((8, LANES))],
        compiler_params=_params("arbitrary"),
    )(x1, down, target, g_final, gate)


def _swap_pairs(v):
    lane = lax.broadcasted_iota(jnp.int32, v.shape, 1)
    return jnp.where(lane % 2 == 0, pltpu.roll(v, LANES - 1, 1), pltpu.roll(v, 1, 1))


def _head_prep_fwd(z, col_off, n_heads, gain, cos, sin, tr, name, side=None):
    t = z.shape[0]
    per = math.gcd(4, n_heads, col_off // LANES)
    w = per * LANES
    hb = col_off // w

    def body(z_ref, g_ref, cos_ref, sin_ref, o_ref):
        for hh in range(per):
            cols = slice(hh * LANES, (hh + 1) * LANES)
            x = z_ref[:, cols]
            y = x * _rms_stats(x) * g_ref[...]
            o_ref[:, cols] = (y * cos_ref[...] + _swap_pairs(y) * sin_ref[...]).astype(BF16)

    tab = pl.BlockSpec((tr, LANES), lambda i, j: (i, 0))
    return _call(
        body, side=side, name=name, grid=(t // tr, n_heads // per),
        in_specs=[pl.BlockSpec((tr, w), lambda i, j: (i, hb + j)), pl.BlockSpec((1, LANES), lambda i, j: (0, 0)),
                  tab, tab],
        out_specs=pl.BlockSpec((tr, w), lambda i, j: (i, j)), out_shape=_sds((t, n_heads * LANES), BF16),
        sem=("parallel", "parallel"),
    )(z, gain, cos, sin)


def _dz_start(t, n_in, n_ctx_rows):
    tcol = _tile(n_in, 1024, LANES)

    def body(o_ref):
        o_ref[...] = jnp.zeros_like(o_ref)

    return pl.pallas_call(body, name="dz_start", grid=(n_in // tcol,),
                          out_specs=pl.BlockSpec((n_ctx_rows, tcol), lambda j: (0, j)), out_shape=_sds((t, n_in), BF16),
                          compiler_params=_params("parallel"))()


def _head_prep_bwd(z, col_off, n_heads, gain, cos, sin, dout, row_off, tr, dz, name, side=None):
    r = dout.shape[0]
    per = math.gcd(4, n_heads, col_off // LANES)
    w = per * LANES
    hb = col_off // w
    rb = row_off // tr

    def body(z_ref, g_ref, cos_ref, sin_ref, d_ref, _, dz_ref, dg_ref):
        @pl.when(jnp.logical_and(pl.program_id(0) == 0, pl.program_id(1) == 0))
        def _():
            dg_ref[...] = jnp.zeros_like(dg_ref)

        for hh in range(per):
            cols = slice(hh * LANES, (hh + 1) * LANES)
            x = z_ref[:, cols]
            rstd = _rms_stats(x)
            xhat = x * rstd
            dd = d_ref[:, cols]
            dy = dd * cos_ref[...] - _swap_pairs(dd) * sin_ref[...]
            dg_ref[0:1, :] += _colsum(dy * xhat)
            dz_ref[:, cols] = _rms_bwd(dy * g_ref[...], xhat, rstd).astype(BF16)

    tab = pl.BlockSpec((tr, LANES), lambda i, j: (rb + i, 0))
    window = pl.BlockSpec((tr, w), lambda i, j: (rb + i, hb + j))
    return _call(
        body, side=side, name=name, grid=(r // tr, n_heads // per),
        in_specs=[window, pl.BlockSpec((1, LANES), lambda i, j: (0, 0)), tab, tab,
                  pl.BlockSpec((tr, w), lambda i, j: (i, j)), ANY],
        out_specs=[window, pl.BlockSpec((8, LANES), lambda i, j: (0, 0))],
        out_shape=[_sds(dz.shape, BF16), _sds((8, LANES))], input_output_aliases={5: 0}, sem=("arbitrary", "arbitrary"),
    )(z, gain, cos, sin, dout, dz)


def _attn_fwd(qr, kr, z, v_off, n_ctx_rows, group, tq, side=None):
    t, kvw = kr.shape
    s = t - n_ctx_rows
    n_kv = kvw // LANES
    scale = LANES ** -0.5
    qb0 = n_ctx_rows // tq
    vb = v_off // LANES

    def body(q_ref, k_ref, v_ref, o_ref, lse_ref):
        k = k_ref[...]
        v = v_ref[...].astype(BF16)
        lse_ref[...] = jnp.zeros_like(lse_ref)
        for g in range(group):
            cols = slice(g * LANES, (g + 1) * LANES)
            sc = lax.dot_general(q_ref[:, cols], k, (((1,), (1,)), ((), ())), preferred_element_type=F32) * scale
            m = jnp.max(sc, axis=-1, keepdims=True)
            e = jnp.exp(sc - m)
            l = jnp.sum(e, axis=-1, keepdims=True)
            p = e * (1.0 / l)
            o_ref[:, cols] = jnp.dot(p.astype(BF16), v, preferred_element_type=F32).astype(BF16)
            lse_ref[:, g:g + 1] = m + jnp.log(l)

    return _call(
        body, side=side, name="attn_fwd", grid=(n_kv, s // tq),
        in_specs=[pl.BlockSpec((tq, group * LANES), lambda h, i: (qb0 + i, h)),
                  pl.BlockSpec((t, LANES), lambda h, i: (0, h)), pl.BlockSpec((t, LANES), lambda h, i: (0, vb + h))],
        out_specs=[pl.BlockSpec((tq, group * LANES), lambda h, i: (i, h)), pl.BlockSpec((tq, LANES), lambda h, i: (i, h))],
        out_shape=[_sds((s, n_kv * group * LANES), BF16), _sds((s, kvw))], sem=("parallel", "parallel"),
    )(qr, kr, z)


def _attn_bwd(qr, kr, z, v_off, d_o, attn_o, lse, dz, n_ctx_rows, group, tq, side=None):
    t, kvw = kr.shape
    s = t - n_ctx_rows
    n_kv = kvw // LANES
    scale = LANES ** -0.5
    qb0 = n_ctx_rows // tq
    vb = v_off // LANES
    n_q_blocks = s // tq
    tn_dims = (((0,), (0,)), ((), ()))
    nt_dims = (((1,), (1,)), ((), ()))

    def body(q_ref, k_ref, v_ref, do_ref, o_ref, lse_ref, _, dq_ref, dk_ref, dz_ref, dv_ref):
        @pl.when(pl.program_id(1) == 0)
        def _():
            dk_ref[...] = jnp.zeros_like(dk_ref)
            dv_ref[...] = jnp.zeros_like(dv_ref)

        k = k_ref[...]
        v = v_ref[...].astype(BF16)
        for g in range(group):
            cols = slice(g * LANES, (g + 1) * LANES)
            q = q_ref[:, cols]
            do_ = do_ref[:, cols]
            row_dot = jnp.sum(do_.astype(F32) * o_ref[:, cols].astype(F32), axis=-1, keepdims=True)
            sc = lax.dot_general(q, k, nt_dims, preferred_element_type=F32)
            p = jnp.exp(sc * scale - lse_ref[:, g:g + 1])
            dv_ref[...] += lax.dot_general(p.astype(BF16), do_, tn_dims, preferred_element_type=F32)
            dp = lax.dot_general(do_, v, nt_dims, preferred_element_type=F32)
            ds = (p * (dp - row_dot)).astype(BF16)
            dq_ref[:, cols] = jnp.dot(ds, k, preferred_element_type=F32) * scale
            dk_ref[...] += lax.dot_general(ds, q, tn_dims, preferred_element_type=F32)

        @pl.when(pl.program_id(1) == n_q_blocks - 1)
        def _():
            dk_ref[...] = dk_ref[...] * scale
            dz_ref[...] = dv_ref[...].astype(BF16)

    qspec = pl.BlockSpec((tq, group * LANES), lambda h, i: (qb0 + i, h))
    ospec = pl.BlockSpec((tq, group * LANES), lambda h, i: (i, h))
    kspec = pl.BlockSpec((t, LANES), lambda h, i: (0, h))
    vspec = pl.BlockSpec((t, LANES), lambda h, i: (0, vb + h))
    return _call(
        body, side=side, name="attn_bwd", grid=(n_kv, n_q_blocks),
        in_specs=[qspec, kspec, vspec, ospec, ospec, pl.BlockSpec((tq, LANES), lambda h, i: (i, h)), ANY],
        out_specs=[ospec, kspec, vspec], out_shape=[_sds((s, n_kv * group * LANES)), _sds((t, kvw)), _sds(dz.shape, BF16)],
        scratch_shapes=[pltpu.VMEM((t, LANES), F32)], input_output_aliases={6: 2}, sem=("parallel", "arbitrary"),
    )(qr, kr, z, d_o, attn_o, lse, dz)


def _row_mask(shape, rows):
    r = lax.broadcasted_iota(jnp.int32, shape, 0)
    m = r == rows[0]
    for v in rows[1:]:
        m = jnp.logical_or(m, r == v)
    return m


def _shift_rows(x, k, n_ctx_rows):
    t = x.shape[0]
    if k == 0:
        return x
    rolled = pltpu.roll(x, (-k) % t, 0)
    if k > 0:
        dead = [n_ctx_rows - 1 - i for i in range(k)] + [t - 1 - i for i in range(k)]
    else:
        dead = [i for i in range(-k)] + [n_ctx_rows + i for i in range(-k)]
    return jnp.where(_row_mask(x.shape, dead), 0.0, rolled)


def _conv(x, w, b, n_ctx_rows):
    y = b
    for k in range(CONV_WIDTH):
        y = y + _shift_rows(x, k - 1, n_ctx_rows) * w[k:k + 1, :]
    return y


def _gates(xc_bf, w_r, b_r, w_i, b_i, lam):
    r = jax.nn.sigmoid(jnp.dot(xc_bf, w_r.astype(BF16), preferred_element_type=F32) + b_r)
    i = jax.nn.sigmoid(jnp.dot(xc_bf, w_i.astype(BF16), preferred_element_type=F32) + b_i)
    log_a = -LRU_C * r * _softplus(-lam)
    a = jnp.exp(log_a)
    mult = jnp.sqrt(-_expm1_nonpos(2.0 * log_a))
    return r, i, a, mult


def _rnn_specs(t, xr_off):
    xb = xr_off // LANES
    return dict(
        zcol=pl.BlockSpec((t, LANES), lambda j: (0, xb + j)), col=pl.BlockSpec((t, LANES), lambda j: (0, j)),
        conv_w=pl.BlockSpec((CONV_WIDTH, LANES), lambda j: (0, j)), vec=pl.BlockSpec((1, LANES), lambda j: (0, j)),
        gate_w=pl.BlockSpec((2, 1, LANES, LANES), lambda j: (0, j, 0, 0)), two=pl.BlockSpec((2, LANES), lambda j: (0, j)))


def _rnn_prep(z, xr_off, conv_w, conv_b, w_rg, b_rg, w_ig, b_ig, lam, n_ctx_rows, side=None):
    t = z.shape[0]
    d = conv_b.shape[1]
    sp = _rnn_specs(t, xr_off)

    def body(z_ref, cw_ref, cb_ref, wr_ref, br_ref, wi_ref, bi_ref, lam_ref, xc_ref, af_ref, bf_ref, ab_ref, bb_ref):
        xc = _conv(z_ref[...], cw_ref[...], cb_ref[...], n_ctx_rows)
        xc_ref[...] = xc
        xc_bf = xc.astype(BF16)
        for dr, (a_ref, b_ref) in enumerate(((af_ref, bf_ref), (ab_ref, bb_ref))):
            _, i, a, mult = _gates(xc_bf, wr_ref[dr, 0], br_ref[dr:dr + 1, :], wi_ref[dr, 0], bi_ref[dr:dr + 1, :],
                                   lam_ref[dr:dr + 1, :])
            a_ref[...] = a
            b_ref[...] = mult * (i * xc)

    return _call(
        body, side=side, name="rnn_prep", grid=(d // LANES,),
        in_specs=[sp["zcol"], sp["conv_w"], sp["vec"], sp["gate_w"], sp["two"], sp["gate_w"], sp["two"], sp["two"]],
        out_specs=[sp["col"]] * 5, out_shape=[_sds((t, d))] * 5, sem=("parallel",),
    )(z, conv_w, conv_b, w_rg, b_rg, w_ig, b_ig, lam)


def _scan(a, b, *, order, post, n_ctx_rows, name, tc=256, side=None):
    t, d = a.shape
    tc = _tile(math.gcd(n_ctx_rows, t - n_ctx_rows), tc, SUBLANES)
    nt, nctx = t // tc, n_ctx_rows // tc
    nlat = nt - nctx
    b_lat_only = b.shape[0] != t
    up = order.endswith("up")

    def chunk(i):
        if order == "ctx_lat_up":
            return i
        if order == "lat_ctx_down":
            return nt - 1 - i
        if order == "ctx_lat_down":
            return jnp.where(i < nctx, nctx - 1 - i, nt - 1 - (i - nctx))
        return jnp.where(i < nlat, nctx + i, i - nlat)

    def body(a_ref, b_ref, o_ref, carry_ref):
        @pl.when(pl.program_id(0) == 0)
        def _():
            carry_ref[...] = jnp.zeros_like(carry_ref)

        live = jnp.where(chunk(pl.program_id(0)) >= nctx, 1.0, 0.0) if b_lat_only else None

        def group(gi, carry):
            base = pl.multiple_of((gi if up else tc // SUBLANES - 1 - gi) * SUBLANES, SUBLANES)
            for r in (range(SUBLANES) if up else range(SUBLANES - 1, -1, -1)):
                a_r = a_ref[pl.ds(base + r, 1), :]
                b_r = b_ref[pl.ds(base + r, 1), :]
                if live is not None:
                    b_r = b_r * live
                if post:
                    out = b_r + carry
                    carry = a_r * out
                else:
                    out = a_r * carry + b_r
                    carry = out
                o_ref[pl.ds(base + r, 1), :] = out
            return carry

        carry_ref[0:1, :] = lax.fori_loop(0, tc // SUBLANES, group, carry_ref[0:1, :])

    full = pl.BlockSpec((tc, d), lambda i: (chunk(i), 0))
    b_spec = pl.BlockSpec((tc, d), lambda i: (jnp.maximum(chunk(i) - nctx, 0), 0)) if b_lat_only else full
    return _call(
        body, side=side, name=name, grid=(nt,), in_specs=[full, b_spec], out_specs=full, out_shape=_sds((t, d)),
        scratch_shapes=[pltpu.VMEM((SUBLANES, d), F32)], sem=("arbitrary",),
    )(a, b)


def _rnn_bwd(z, xr_off, xc, g_f, g_b, h_f, h_b, conv_w, w_rg, b_rg, w_ig, b_ig, lam, dz, n_ctx_rows, side=None):
    t, d = xc.shape
    sp = _rnn_specs(t, xr_off)
    tn_dims = (((0,), (0,)), ((), ()))
    nt_dims = (((1,), (1,)), ((), ()))

    def body(z_ref, xc_ref, gf_ref, gb_ref, hf_ref, hb_ref, cw_ref, wr_ref, br_ref, wi_ref, bi_ref, lam_ref, _,
             dxr_ref, dwr_ref, dwi_ref, sums_ref):
        xc_ = xc_ref[...]
        xc_bf = xc_.astype(BF16)
        dxc = jnp.zeros_like(xc_)
        sums = [None] * 6
        for dr, (g_ref, h_ref) in enumerate(((gf_ref, hf_ref), (gb_ref, hb_ref))):
            w_r, w_i, lam_ = wr_ref[dr, 0], wi_ref[dr, 0], lam_ref[dr:dr + 1, :]
            r, i, a, mult = _gates(xc_bf, w_r, br_ref[dr:dr + 1, :], w_i, bi_ref[dr:dr + 1, :], lam_)
            g = g_ref[...]
            h = h_ref[...]
            if dr == 0:
                h_prev = jnp.where(_row_mask(h.shape, [0]), 0.0, pltpu.roll(h, 1, 0))
            else:
                h_prev = jnp.where(_row_mask(h.shape, [n_ctx_rows - 1]), 0.0, pltpu.roll(h, t - 1, 0))
            d_mult = g * i * xc_
            d_i = g * mult * xc_
            dxc = dxc + g * mult * i
            d_log_a = g * h_prev * a - d_mult * a * a / mult
            sp_ = _softplus(-lam_)
            d_r = d_log_a * (-LRU_C) * sp_
            d_sp = _colsum(d_log_a * (-LRU_C) * r)
            du_r = (d_r * r * (1.0 - r))
            du_i = (d_i * i * (1.0 - i))
            sums[dr] = _colsum(du_r)
            sums[2 + dr] = _colsum(du_i)
            sums[4 + dr] = d_sp * (-jax.nn.sigmoid(-lam_))
            du_r_bf, du_i_bf = du_r.astype(BF16), du_i.astype(BF16)
            dwr_ref[dr, 0] = lax.dot_general(xc_bf, du_r_bf, tn_dims, preferred_element_type=F32).astype(BF16)
            dwi_ref[dr, 0] = lax.dot_general(xc_bf, du_i_bf, tn_dims, preferred_element_type=F32).astype(BF16)
            dxc = dxc + lax.dot_general(du_r_bf, w_r.astype(BF16), nt_dims, preferred_element_type=F32)
            dxc = dxc + lax.dot_general(du_i_bf, w_i.astype(BF16), nt_dims, preferred_element_type=F32)
        xr = z_ref[...]
        cw = cw_ref[...]
        dxr = jnp.zeros_like(dxc)
        rows = list(sums)
        for k in range(CONV_WIDTH):
            dxr = dxr + _shift_rows(dxc, 1 - k, n_ctx_rows) * cw[k:k + 1, :]
            rows.append(_colsum(dxc * _shift_rows(xr, k - 1, n_ctx_rows)))
        rows.append(_colsum(dxc))
        dxr_ref[...] = dxr.astype(BF16)
        sums_ref[...] = jnp.zeros_like(sums_ref)
        for k, row in enumerate(rows):
            sums_ref[k:k + 1, :] = row

    return _call(
        body, side=side, name="rnn_bwd", grid=(d // LANES,),
        in_specs=[sp["zcol"]] + [sp["col"]] * 5 + [sp["conv_w"], sp["gate_w"], sp["two"], sp["gate_w"], sp["two"],
                                                  sp["two"], ANY],
        out_specs=[sp["zcol"], sp["gate_w"], sp["gate_w"], pl.BlockSpec((16, LANES), lambda j: (0, j))],
        out_shape=[_sds(dz.shape, BF16), _sds(w_rg.shape, BF16), _sds(w_ig.shape, BF16), _sds((16, d))],
        input_output_aliases={12: 0}, sem=("parallel",),
    )(z, xc, g_f, g_b, h_f, h_b, conv_w, w_rg, b_rg, w_ig, b_ig, lam, dz)


def _tiles2d(s, d, tr, tcol):
    return (s // tr, d // tcol), pl.BlockSpec((tr, tcol), lambda i, j: (i, j))


def _zspec(tr, tcol, row_off, col_off):
    rb, cb = row_off // tr, col_off // tcol
    return pl.BlockSpec((tr, tcol), lambda i, j: (rb + i, cb + j))


def _rnn_gate_fwd(h_f, h_b, z, xg_off, n_ctx_rows, tr, tcol, side=None):
    t, d = h_f.shape
    s = t - n_ctx_rows
    grid, out = _tiles2d(s, d, tr, tcol)
    hs = _zspec(tr, tcol, n_ctx_rows, 0)

    def body(hf_ref, hb_ref, xg_ref, u_ref):
        u_ref[...] = ((hf_ref[...] + hb_ref[...]) * _gelu(xg_ref[...])).astype(BF16)

    return _call(body, side=side, name="rnn_gate_fwd", grid=grid,
                 in_specs=[hs, hs, _zspec(tr, tcol, n_ctx_rows, xg_off)], out_specs=out, out_shape=_sds((s, d), BF16),
                 sem=("parallel", "parallel"))(h_f, h_b, z)


def _rnn_gate_bwd(d_u, h_f, h_b, z, xg_off, dz, n_ctx_rows, tr, tcol, side=None):
    t, d = h_f.shape
    s = t - n_ctx_rows
    grid, out = _tiles2d(s, d, tr, tcol)
    hs = _zspec(tr, tcol, n_ctx_rows, 0)
    window = _zspec(tr, tcol, n_ctx_rows, xg_off)

    def body(du_ref, hf_ref, hb_ref, xg_ref, _, dr_ref, dxg_ref):
        du = du_ref[...]
        xg = xg_ref[...]
        dr_ref[...] = du * _gelu(xg)
        dxg_ref[...] = (du * (hf_ref[...] + hb_ref[...]) * _gelu_grad(xg)).astype(BF16)

    return _call(body, side=side, name="rnn_gate_bwd", grid=grid, in_specs=[out, hs, hs, window, ANY],
                 out_specs=[out, window], out_shape=[_sds((s, d)), _sds(dz.shape, BF16)], input_output_aliases={4: 1},
                 sem=("parallel", "parallel"))(d_u, h_f, h_b, z, dz)


def _merge_fwd(y_attn, y_rnn, z, gl_off, n_ctx_rows, tr, tcol):
    s, d = y_attn.shape
    grid, out = _tiles2d(s, d, tr, tcol)

    def body(ya_ref, yr_ref, ga_ref, gr_ref, o_ref):
        o_ref[...] = (jax.nn.sigmoid(ga_ref[...]) * ya_ref[...] + jax.nn.sigmoid(gr_ref[...]) * yr_ref[...]).astype(BF16)

    return pl.pallas_call(
        body, name="merge_fwd", grid=grid,
        in_specs=[out, out, _zspec(tr, tcol, n_ctx_rows, gl_off), _zspec(tr, tcol, n_ctx_rows, gl_off + d)],
        out_specs=out, out_shape=_sds((s, d), BF16), compiler_params=_params("parallel", "parallel"),
    )(y_attn, y_rnn, z, z)


def _merge_bwd(d_mrg, y, z, gl_off, dz, n_ctx_rows, tr, tcol, name):
    s, d = y.shape
    grid, out = _tiles2d(s, d, tr, tcol)
    window = _zspec(tr, tcol, n_ctx_rows, gl_off)

    def body(dm_ref, y_ref, gl_ref, _, dy_ref, dgl_ref):
        dm = dm_ref[...]
        g = jax.nn.sigmoid(gl_ref[...])
        dy_ref[...] = (dm * g).astype(BF16)
        dgl_ref[...] = (dm * y_ref[...] * g * (1.0 - g)).astype(BF16)

    return _call(body, name=name, grid=grid, in_specs=[out, out, window, ANY], out_specs=[out, window],
                 out_shape=[_sds((s, d), BF16), _sds(dz.shape, BF16)], input_output_aliases={3: 1},
                 sem=("parallel", "parallel"))(d_mrg, y, z, dz)


def _sq_relu(up, tr, tcol, side=None):
    grid, out = _tiles2d(*up.shape, _tile(up.shape[0], 2 * tr, 16), _tile(up.shape[1], 4 * tcol, LANES))

    def body(u_ref, o_ref):
        r = jnp.maximum(u_ref[...], 0.0)
        o_ref[...] = (r * r).astype(BF16)

    return _call(body, side=side, name="sq_relu", grid=grid, in_specs=[out], out_specs=out,
                 out_shape=_sds(up.shape, BF16), sem=("parallel", "parallel"))(up)


def _cast_into_window(w, chip, col_sharded, name):
    r, c = w.shape
    tr, tcol = _tile(r, 512, 16), _tile(c, 1024, LANES)
    nrb, ncb = r // tr, c // tcol

    def body(chip_ref, w_ref, o_ref):
        o_ref[...] = w_ref[...].astype(BF16)

    if col_sharded:
        omap = lambda i, j, chip_ref: (i, chip_ref[0] * ncb + j)
    else:
        omap = lambda i, j, chip_ref: (chip_ref[0] * nrb + i, j)
    return pl.pallas_call(
        body, name=name,
        grid_spec=pltpu.PrefetchScalarGridSpec(
            num_scalar_prefetch=1, grid=(nrb, ncb),
            in_specs=[pl.BlockSpec((tr, tcol), lambda i, j, chip_ref: (i, j))], out_specs=pl.BlockSpec((tr, tcol), omap)),
        out_shape=_sds((r, c * N_CHIPS) if col_sharded else (r * N_CHIPS, c), BF16),
        compiler_params=_params("parallel", "parallel"),
    )(chip, w)


def _sum_leading(parts, name):
    n, r, c = parts.shape
    tr, tcol = _tile(r, 512, SUBLANES), _tile(c, 1024, LANES)

    def body(p_ref, o_ref):
        tot = p_ref[0]
        for k in range(1, n):
            tot = tot + p_ref[k]
        o_ref[...] = tot

    return pl.pallas_call(
        body, name=name, grid=(r // tr, c // tcol), in_specs=[pl.BlockSpec((n, tr, tcol), lambda i, j: (0, i, j))],
        out_specs=pl.BlockSpec((tr, tcol), lambda i, j: (i, j)), out_shape=_sds((r, c)),
        compiler_params=_params("parallel", "parallel"),
    )(parts)


def _add_half(full, other, core, split_rows, name):
    r, c = other.shape
    tr, tcol = _tile(r, 512, 16), _tile(c, 1024, LANES)
    nrb, ncb = r // tr, c // tcol

    def body(core_ref, f_ref, o_ref, out_ref):
        out_ref[...] = (f_ref[...].astype(F32) + o_ref[...].astype(F32)).astype(out_ref.dtype)

    if split_rows:
        fmap = lambda i, j, core_ref: (core_ref[0] * nrb + i, j)
    else:
        fmap = lambda i, j, core_ref: (i, core_ref[0] * ncb + j)
    same = lambda i, j, core_ref: (i, j)
    return pl.pallas_call(
        body, name=name,
        grid_spec=pltpu.PrefetchScalarGridSpec(
            num_scalar_prefetch=1, grid=(nrb, ncb),
            in_specs=[pl.BlockSpec((tr, tcol), fmap), pl.BlockSpec((tr, tcol), same)],
            out_specs=pl.BlockSpec((tr, tcol), same)),
        out_shape=_sds((r, c), BF16), compiler_params=_params("parallel", "parallel"),
    )(core, full, other)


def _sum_regions(pair, got, place, col_sharded, name):
    n_got, r, c = got.shape
    tr, tcol = _tile(r, 512, 16), _tile(c, 1024, LANES)
    nrb, ncb = r // tr, c // tcol

    def body(place_ref, p_ref, g_ref, out_ref):
        tot = p_ref[...].astype(F32)
        for k in range(n_got):
            tot = tot + g_ref[k].astype(F32)
        out_ref[...] = tot

    if col_sharded:
        pmap = lambda i, j, pr: (i, pr[0] * ncb + j)
        omap = lambda i, j, pr: (pr[1] * nrb + i, j)
        out_shape = (2 * r, c)
    else:
        pmap = lambda i, j, pr: (pr[0] * nrb + i, j)
        omap = lambda i, j, pr: (i, pr[1] * ncb + j)
        out_shape = (r, 2 * c)
    return pl.pallas_call(
        body, name=name,
        grid_spec=pltpu.PrefetchScalarGridSpec(
            num_scalar_prefetch=1, grid=(nrb, ncb),
            in_specs=[pl.BlockSpec((tr, tcol), pmap), pl.BlockSpec((n_got, tr, tcol), lambda i, j, pr: (0, i, j))],
            out_specs=pl.BlockSpec((tr, tcol), omap)),
        out_shape=_sds(out_shape), compiler_params=_params("parallel", "parallel"),
    )(place, pair, got)


def _fold_forwarded(pair, fold, nbr_chips, name):
    _, r, c = fold.shape
    tr, tcol = _tile(r, 512, 16), _tile(c, 1024, LANES)
    nrb, ncb = r // tr, c // tcol

    def body(nbr_ref, px_ref, py_ref, f_ref, out_ref):
        out_ref[0] = (px_ref[...].astype(F32) + f_ref[1].astype(F32)).astype(out_ref.dtype)
        out_ref[1] = (py_ref[...].astype(F32) + f_ref[0].astype(F32)).astype(out_ref.dtype)

    both = pl.BlockSpec((2, tr, tcol), lambda i, j, nb: (0, i, j))
    return pl.pallas_call(
        body, name=name,
        grid_spec=pltpu.PrefetchScalarGridSpec(
            num_scalar_prefetch=1, grid=(nrb, ncb),
            in_specs=[pl.BlockSpec((tr, tcol), lambda i, j, nb: (i, nb[0] * ncb + j)),
                      pl.BlockSpec((tr, tcol), lambda i, j, nb: (nrb + i, nb[1] * ncb + j)), both],
            out_specs=both),
        out_shape=_sds(fold.shape, fold.dtype), compiler_params=_params("parallel", "parallel"),
    )(nbr_chips, pair, pair, fold)


def _adamw(w, g, m, v, name):
    r, c = w.shape
    tr, tcol = _tile(r, 512, SUBLANES), _tile(c, 1024, LANES)
    blk = pl.BlockSpec((tr, tcol), lambda i, j: (i, j))

    def body(w_ref, g_ref, m_ref, v_ref, d_ref, nm_ref, nv_ref, g_out_ref):
        g_ = g_ref[...]
        g_out_ref[...] = g_
        m_ = ADAM_B1 * m_ref[...] + (1.0 - ADAM_B1) * g_
        v_ = ADAM_B2 * v_ref[...] + (1.0 - ADAM_B2) * (g_ * g_)
        m_hat = m_ / (1.0 - ADAM_B1 ** ADAM_STEP)
        v_hat = v_ / (1.0 - ADAM_B2 ** ADAM_STEP)
        d_ref[...] = -ADAM_LR * (m_hat / (jnp.sqrt(v_hat) + ADAM_EPS) + ADAM_WD * w_ref[...])
        nm_ref[...] = m_
        nv_ref[...] = v_

    return _call(body, name=name, grid=(r // tr, c // tcol), in_specs=[blk] * 4, out_specs=[blk] * 4,
                 out_shape=[_sds((r, c))] * 4, sem=("parallel", "parallel"))(w, g, m, v)


def _place():
    x, y, c = lax.axis_index("x"), lax.axis_index("y"), lax.axis_index("c")
    chips = [(1 - x, y), (x, 1 - y), (1 - x, 1 - y)]
    return x, y, c, chips


def _all_gather8(blk, name):
    m, n = blk.shape

    def body(x_ref, out_ref, send_sems, recv_sems, local_sem):
        x, y, c, chips = _place()
        me, sibling = (x, y, c), (x, y, 1 - c)

        def rows(px, py, pc):
            return out_ref.at[pl.ds((4 * px + 2 * py + pc) * m, m), :]

        def copy(k, block, to, src=None):
            return pltpu.make_async_remote_copy(
                src_ref=rows(*block) if src is None else src, dst_ref=rows(*block), send_sem=send_sems.at[k],
                recv_sem=recv_sems.at[k], device_id=to, device_id_type=MESH_ID)

        mine = pltpu.make_async_copy(x_ref, rows(*me), local_sem)
        mine.start()
        first = [copy(0, me, sibling, src=x_ref)]
        first += [copy(1 + j, me, (*chip, c), src=x_ref) for j, chip in enumerate(chips)]
        for cp in first:
            cp.start()
        passed = [copy(4 + j, (*chip, c), sibling) for j, chip in enumerate(chips)]
        for j, chip in enumerate(chips):
            copy(1 + j, (*chip, c), me).wait_recv()
            passed[j].start()
        copy(0, sibling, me).wait_recv()
        for j, chip in enumerate(chips):
            copy(4 + j, (*chip, 1 - c), me).wait_recv()
        for cp in first + passed:
            cp.wait_send()
        mine.wait()

    return pl.pallas_call(
        body, name=name, out_shape=_sds((N_DEV * m, n), blk.dtype), in_specs=[ANY], out_specs=ANY,
        scratch_shapes=[pltpu.SemaphoreType.DMA((7,)), pltpu.SemaphoreType.DMA((7,)), pltpu.SemaphoreType.DMA],
    )(blk)


def _half(ref, core, split_rows):
    r, c = ref.shape
    if split_rows:
        return ref.at[pl.ds(core * (r // 2), r // 2), :]
    return ref.at[:, pl.ds(core * (c // 2), c // 2)]


def _chip_block(ref, j, col_sharded):
    r, c = ref.shape
    if col_sharded:
        return ref.at[:, pl.ds(j * (c // N_CHIPS), c // N_CHIPS)]
    return ref.at[pl.ds(j * (r // N_CHIPS), r // N_CHIPS), :]


def _rows_part(ref, part):
    lo, hi, n = part
    r = ref.shape[0]
    return ref if (lo, hi) == (0, n) else ref.at[pl.ds(lo * (r // n), (hi - lo) * (r // n)), :]


def _copy(send_sems, recv_sems, k, src, dst, to):
    return pltpu.make_async_remote_copy(src_ref=src, dst_ref=dst, send_sem=send_sems.at[k], recv_sem=recv_sems.at[k],
                                        device_id=to, device_id_type=MESH_ID)


def _in_place(arrays):
    return tuple(arrays), tuple(_sds(a.shape, a.dtype) for a in arrays), {i: i for i in range(len(arrays))}


def _gather_ici(fulls, col_sharded, part=(0, 1, 1)):
    nw = len(fulls)

    def build(_, refs, send_sems, recv_sems, sem0):
        x, y, c, chips = _place()
        sends, recvs = [], []
        for w in range(nw):
            win = lambda j: _rows_part(_half(_chip_block(refs[w], j, col_sharded[w]), c, True), part)
            for k, (cx, cy) in enumerate(chips):
                sem = sem0 + 3 * w + k
                sends.append(_copy(send_sems, recv_sems, sem, win(2 * x + y), win(2 * x + y), (cx, cy, c)))
                recvs.append(_copy(send_sems, recv_sems, sem, win(2 * cx + cy), win(2 * cx + cy), (cx, cy, c)))
        return sends, recvs

    return _Side(*_in_place(fulls), 3 * nw, build)


def _gather_d2d(fulls, col_sharded):
    nw = len(fulls)

    def build(_, refs, send_sems, recv_sems, sem0):
        x, y, c, chips = _place()
        sends, recvs = [], []
        for w in range(nw):
            win = lambda j, core: _half(_chip_block(refs[w], j, col_sharded[w]), core, True)
            for k, (cx, cy) in enumerate(chips):
                sem = sem0 + 3 * w + k
                sends.append(_copy(send_sems, recv_sems, sem, win(2 * cx + cy, c), win(2 * cx + cy, c), (x, y, 1 - c)))
                recvs.append(_copy(send_sems, recv_sems, sem, win(2 * cx + cy, 1 - c), win(2 * cx + cy, 1 - c),
                                   (x, y, 1 - c)))
        return sends, recvs

    return _Side(*_in_place(fulls), 3 * nw, build)


def _gather_neighbours(full, col_sharded):
    def build(_, refs, send_sems, recv_sems, sem0):
        x, y, c, chips = _place()
        win = lambda j: _half(_chip_block(refs[0], j, col_sharded), c, True)
        sends, recvs = [], []
        for k, (cx, cy) in enumerate(chips[:2]):
            sends.append(_copy(send_sems, recv_sems, sem0 + k, win(2 * x + y), win(2 * x + y), (cx, cy, c)))
            recvs.append(_copy(send_sems, recv_sems, sem0 + k, win(2 * cx + cy), win(2 * cx + cy), (cx, cy, c)))
        return sends, recvs

    return _Side(*_in_place([full]), 2, build)


def _gather_forward(full, col_sharded):
    def build(_, refs, send_sems, recv_sems, sem0):
        x, y, c, (cx_, cy_, cd_) = _place()
        win = lambda chip, core: _half(_chip_block(refs[0], 2 * chip[0] + chip[1], col_sharded), core, True)
        part = lambda ref, p: _rows_part(ref, (p, p + 1, 2))
        sends = [_copy(send_sems, recv_sems, sem0, part(win(cx_, c), 0), part(win(cx_, c), 0), (*cy_, c)),
                 _copy(send_sems, recv_sems, sem0 + 1, part(win(cy_, c), 1), part(win(cy_, c), 1), (*cx_, c))]
        recvs = [_copy(send_sems, recv_sems, sem0, part(win(cd_, c), 0), part(win(cd_, c), 0), (*cy_, c)),
                 _copy(send_sems, recv_sems, sem0 + 1, part(win(cd_, c), 1), part(win(cd_, c), 1), (*cx_, c))]
        for k, chip in enumerate((cx_, cy_)):
            sends.append(_copy(send_sems, recv_sems, sem0 + 2 + k, win(chip, c), win(chip, c), (x, y, 1 - c)))
            recvs.append(_copy(send_sems, recv_sems, sem0 + 2 + k, win(chip, 1 - c), win(chip, 1 - c), (x, y, 1 - c)))
        return sends, recvs

    return _Side(*_in_place([full]), 4, build)


def _gather_diagonal_d2d(full, col_sharded):
    def build(_, refs, send_sems, recv_sems, sem0):
        x, y, c, chips = _place()
        cx, cy = chips[2]
        win = lambda core: _half(_chip_block(refs[0], 2 * cx + cy, col_sharded), core, True)
        return ([_copy(send_sems, recv_sems, sem0, win(c), win(c), (x, y, 1 - c))],
                [_copy(send_sems, recv_sems, sem0, win(1 - c), win(1 - c), (x, y, 1 - c))])

    return _Side(*_in_place([full]), 1, build)


def _scatter_forward_1(pair):
    r, n = pair.shape
    results = (_sds((2, r, n // N_CHIPS), pair.dtype), _sds((2, r // 2, n // N_CHIPS), pair.dtype))

    def build(refs, res, send_sems, recv_sems, sem0):
        x, y, c, (cx_, cy_, cd_) = _place()
        region = lambda chip, p: _rows_part(_chip_block(refs[0], 2 * chip[0] + chip[1], True), (p, p + 1, 2))
        got, fold = res
        copies = [
            _copy(send_sems, recv_sems, sem0, region(cx_, 1), _rows_part(got.at[0], (1, 2, 2)), (*cx_, c)),
            _copy(send_sems, recv_sems, sem0 + 1, region(cd_, 1), fold.at[0], (*cx_, c)),
            _copy(send_sems, recv_sems, sem0 + 2, region(cy_, 0), _rows_part(got.at[1], (0, 1, 2)), (*cy_, c)),
            _copy(send_sems, recv_sems, sem0 + 3, region(cd_, 0), fold.at[1], (*cy_, c))]
        return copies, copies

    return _Side((pair,), results, {}, 4, build)


def _scatter_forward_2(passed, got):
    def build(refs, res, send_sems, recv_sems, sem0):
        x, y, c, (cx_, cy_, _) = _place()
        copies = [_copy(send_sems, recv_sems, sem0, refs[0].at[0], _rows_part(res[0].at[0], (0, 1, 2)), (*cx_, c)),
                  _copy(send_sems, recv_sems, sem0 + 1, refs[0].at[1], _rows_part(res[0].at[1], (1, 2, 2)), (*cy_, c))]
        return copies, copies

    return _Side((passed, got), (_sds(got.shape, got.dtype),), {1: 0}, 2, build)


def _exchange(side, name):
    return _call(None, side=side, name=name)()[1]


def _swap_halves(grads, col_sharded):
    nw = len(grads)
    out_shapes = [_sds((g.shape[0] // 2, g.shape[1]) if col else (g.shape[0], g.shape[1] // 2), g.dtype)
                  for g, col in zip(grads, col_sharded)]

    def build(g_refs, o_refs, send_sems, recv_sems, sem0):
        x, y, c, _ = _place()
        copies = [_copy(send_sems, recv_sems, sem0 + w, _half(g_refs[w], 1 - c, col_sharded[w]), o_refs[w],
                        (x, y, 1 - c)) for w in range(nw)]
        return copies, copies

    return _Side(tuple(grads), tuple(out_shapes), {}, nw, build)


def _scatter_regions(pairs, col_sharded, part=(0, 1, 1), into=None):
    nw = len(pairs)

    def region_shape(p, col):
        return (p.shape[0], p.shape[1] // N_CHIPS) if col else (p.shape[0] // N_CHIPS, p.shape[1])

    out_shapes = tuple(_sds((N_CHIPS - 1, *region_shape(p, col)), p.dtype) for p, col in zip(pairs, col_sharded))

    def build(refs, o_refs, send_sems, recv_sems, sem0):
        x, y, c, chips = _place()
        copies = []
        for w in range(nw):
            for k, (cx, cy) in enumerate(chips):
                copies.append(_copy(
                    send_sems, recv_sems, sem0 + 3 * w + k,
                    _rows_part(_chip_block(refs[w], 2 * cx + cy, col_sharded[w]), part),
                    _rows_part(o_refs[w].at[k], part), (cx, cy, c)))
        return copies, copies

    if into is None:
        return _Side(tuple(pairs), out_shapes, {}, 3 * nw, build)
    return _Side((*pairs, *into), out_shapes, {nw + w: w for w in range(nw)}, 3 * nw, build)


def _join_halves(halves, col_sharded):
    nw = len(halves)

    def build(_, refs, send_sems, recv_sems, sem0):
        x, y, c, _ = _place()
        sends, recvs = [], []
        for w in range(nw):
            mine, theirs = _half(refs[w], c, col_sharded[w]), _half(refs[w], 1 - c, col_sharded[w])
            sends.append(_copy(send_sems, recv_sems, sem0 + w, mine, mine, (x, y, 1 - c)))
            recvs.append(_copy(send_sems, recv_sems, sem0 + w, theirs, theirs, (x, y, 1 - c)))
        return sends, recvs

    return _Side(*_in_place(halves), nw, build)


def _part_rows(size):
    return -(-size // (SUBLANES * LANES)) * SUBLANES


def _pack(arrays, pad_rows_to=SUBLANES):
    flat = [jnp.pad(a.reshape(-1), (0, _part_rows(a.size) * LANES - a.size)).reshape(-1, LANES) for a in arrays]
    rows = sum(f.shape[0] for f in flat)
    pad = (-rows) % pad_rows_to
    if pad:
        flat.append(jnp.zeros((pad, LANES), F32))
    return jnp.concatenate(flat, axis=0)


def _unpack(packed, shapes):
    out, r = [], 0
    for shp in shapes:
        size = math.prod(shp)
        out.append(packed[r:r + _part_rows(size)].reshape(-1)[:size].reshape(shp))
        r += _part_rows(size)
    return out


def _rope_tables(n_ctx_rows, s):
    rows = s // GRID_W
    row_idx = jnp.repeat(jnp.arange(rows), GRID_W)
    col_idx = jnp.tile(jnp.arange(GRID_W), rows)
    n_freq = LANES // 4
    inv_freq = ROPE_THETA ** (-jnp.arange(n_freq, dtype=F32) / n_freq)
    ang = jnp.concatenate([row_idx.astype(F32)[:, None] * inv_freq, col_idx.astype(F32)[:, None] * inv_freq], axis=-1)
    cos = jnp.repeat(jnp.cos(ang), 2, axis=-1)
    sin = jnp.repeat(jnp.sin(ang), 2, axis=-1) * jnp.tile(jnp.array([-1.0, 1.0], F32), LANES // 2)
    cos = jnp.concatenate([jnp.ones((n_ctx_rows, LANES), F32), cos], axis=0)
    sin = jnp.concatenate([jnp.zeros((n_ctx_rows, LANES), F32), sin], axis=0)
    return cos, sin


WEIGHT_NAMES = ['c_ctx', 'w_mod', 'b_mod', 'g_mix', 'g_mlp', 'w_in', 'q_gain', 'k_gain', 'conv_w', 'conv_b', 'w_rg',
                'b_rg', 'w_ig', 'b_ig', 'lru_lambda', 'w_o_attn', 'w_o_rnn', 'w_out', 'w_up', 'w_down', 'g_final']
BIG = ['w_in', 'w_o_attn', 'w_o_rnn', 'w_out', 'w_up', 'w_down']
BIG_COL_SHARDED = [True, False, False, False, True, False]
GATES = ['w_rg', 'w_ig']
SMALL = ['c_ctx', 'b_mod', 'g_mix', 'g_mlp', 'q_gain', 'k_gain', 'conv_b', 'g_final',
         'conv_w', 'b_rg', 'b_ig', 'lru_lambda']


def kernel(x, c, ctx, c_ctx, w_mod, b_mod, g_mix, g_mlp, w_in, q_gain, k_gain, conv_w, conv_b, w_rg, b_rg, w_ig, b_ig, lru_lambda, w_o_attn, w_o_rnn, w_out, w_up, w_down, g_final, loss_target, m_c_ctx, m_w_mod, m_b_mod, m_g_mix, m_g_mlp, m_w_in, m_q_gain, m_k_gain, m_conv_w, m_conv_b, m_w_rg, m_b_rg, m_w_ig, m_b_ig, m_lru_lambda, m_w_o_attn, m_w_o_rnn, m_w_out, m_w_up, m_w_down, m_g_final, v_c_ctx, v_w_mod, v_b_mod, v_g_mix, v_g_mlp, v_w_in, v_q_gain, v_k_gain, v_conv_w, v_conv_b, v_w_rg, v_b_rg, v_w_ig, v_b_ig, v_lru_lambda, v_w_o_attn, v_w_o_rnn, v_w_out, v_w_up, v_w_down, v_g_final):
    given = dict(locals())
    weights = {n: given[n] for n in WEIGHT_NAMES}
    moms = {n: given["m_" + n] for n in WEIGHT_NAMES}
    vars_ = {n: given["v_" + n] for n in WEIGHT_NAMES}

    s, d = x.shape[1], x.shape[2]
    n_ctx = ctx.shape[1]
    t = n_ctx + s
    hd = q_gain.shape[1]
    assert hd == LANES and w_rg.shape[-1] == LANES
    attn_w = w_o_attn.shape[1] * N_CHIPS
    n_in = w_in.shape[2] * N_CHIPS
    kv_w = (n_in - attn_w - 4 * d) // 2
    group = attn_w // kv_w
    k_off, v_off, xr_off = attn_w, attn_w + kv_w, attn_w + 2 * kv_w
    xg_off, gl_off = xr_off + d, xr_off + 2 * d
    d_mod = N_MOD * d
    tr = _tile(math.gcd(n_ctx, s), 256, 16)
    tcol = _tile(math.gcd(d, xr_off), 1024, LANES)
    xi, yi, ci = lax.axis_index("x"), lax.axis_index("y"), lax.axis_index("c")
    chip = 2 * xi + yi
    core = ci.astype(jnp.int32).reshape(1)

    sharded_small = [conv_w[0], b_rg[0], b_ig[0], lru_lambda[0]]
    pack0 = _pack([c[0]] + sharded_small)
    got0 = _all_gather8(pack0, "gather_small_inputs").reshape(N_DEV, -1, LANES)
    c_all = got0[:, :_part_rows(d)].reshape(N_DEV, -1)[:, :d]
    per_chip = [_unpack(got0[2 * j, _part_rows(d):], [a.shape for a in sharded_small]) for j in range(N_CHIPS)]
    conv_w_f, b_rg_f, b_ig_f, lam_f = (jnp.concatenate([per_chip[j][i] for j in range(N_CHIPS)], axis=-1)
                                       for i in range(4))
    c16 = jnp.concatenate([c_all, c_ctx[None, :], jnp.zeros((16 - N_DEV - 1, d), F32)], axis=0)
    b_mod_shard = lax.dynamic_slice(b_mod, (0, chip * (d_mod // N_CHIPS)), (1, d_mod // N_CHIPS))
    mod_part, silu16 = _mod_fwd(c16, w_mod[0], b_mod_shard)
    mod_all = _all_gather8(mod_part, "gather_mod").reshape(N_DEV, 16, d_mod // N_CHIPS)
    mod16 = jnp.concatenate([mod_all[2 * j] for j in range(N_CHIPS)], axis=-1)
    me = 4 * xi + 2 * yi + ci
    mod_lat = lax.dynamic_slice(mod16, (me, 0), (1, d_mod)).reshape(N_MOD, d)
    mod_ctx = mod16[N_DEV].reshape(N_MOD, d)
    mod4 = jnp.stack([mod_ctx[0], mod_ctx[1], mod_lat[0], mod_lat[1]])
    mod3 = jnp.stack([mod_lat[2], mod_lat[3], mod_lat[4]])
    gate_f = mod_lat[5][None, :]

    chip_arr = chip.astype(jnp.int32).reshape(1)
    own = {n: _cast_into_window(weights[n][0], chip_arr, col, "cast_" + n) for n, col in zip(BIG, BIG_COL_SHARDED)}
    place = jnp.stack([chip, ci]).astype(jnp.int32)
    row3 = [False] * 3

    def pair_sum(n, full, other, col):
        return _add_half(full, other, core, col, "pair_sum_" + n)

    def chip_sum(n, pair, got, col):
        return _sum_regions(pair, got, place, col, "chip_sum_" + n)

    (w_in_f,) = _exchange(_gather_neighbours(own['w_in'], True), "gather_w_in_nbr")
    (w_in_f,) = _exchange(_gather_forward(w_in_f, True), "gather_w_in_fwd")
    (w_in_f,) = _exchange(_gather_diagonal_d2d(w_in_f, True), "gather_w_in_diag")
    xcat = jnp.concatenate([ctx[0], x[0]], axis=0)
    cos, sin = _rope_tables(n_ctx, s)
    h = _norm_mod_fwd(xcat, g_mix, mod4, n_ctx, tr)
    ici_up = lambda w, lo, hi: _gather_ici([w], [True], (lo, hi, 8))
    ici_down = lambda w, lo, hi: _gather_ici([w], [False], (lo, hi, 8))
    z, (w_oa_f, w_or_f, w_out_f) = _matmul(h, w_in_f, name="mm_in", side=_gather_ici(
        [own['w_o_attn'], own['w_o_rnn'], own['w_out']], row3))
    qr, (w_up_f, w_down_f) = _head_prep_fwd(z, 0, attn_w // LANES, q_gain, cos, sin, tr, "q_prep", side=_sides(
        ici_up(own['w_up'], 0, 1), ici_down(own['w_down'], 0, 1)))
    kr = _head_prep_fwd(z, k_off, kv_w // LANES, k_gain, cos, sin, tr, "k_prep")
    (attn_o, lse), (w_up_f, w_oa_f, w_or_f, w_out_f) = _attn_fwd(qr, kr, z, v_off, n_ctx, group, tr, side=_sides(
        ici_up(w_up_f, 1, 8), _gather_d2d([w_oa_f, w_or_f, w_out_f], row3)))
    (xc, a_f, bx_f, a_b, bx_b), (w_down_f, w_up_f) = _rnn_prep(
        z, xr_off, conv_w_f, conv_b, w_rg[0], b_rg_f, w_ig[0], b_ig_f, lam_f, n_ctx,
        side=_sides(ici_down(w_down_f, 1, 4), _gather_d2d([w_up_f], [True])))
    h_f = _scan(a_f, bx_f, order="ctx_lat_up", post=False, n_ctx_rows=n_ctx, name="scan_f")
    h_b = _scan(a_b, bx_b, order="ctx_lat_down", post=False, n_ctx_rows=n_ctx, name="scan_b")
    u = _rnn_gate_fwd(h_f, h_b, z, xg_off, n_ctx, tr, tcol)
    y_attn = _matmul(attn_o, w_oa_f, name="mm_o_attn")
    y_rnn = _matmul(u, w_or_f, name="mm_o_rnn")
    mrg = _merge_fwd(y_attn, y_rnn, z, gl_off, n_ctx, tr, tcol)
    mix = _matmul(mrg, w_out_f, name="mm_out")
    x1, h2 = _resid_norm_mod_fwd(x[0], mix, g_mlp, mod3, tr)
    up, (w_down_f,) = _matmul(h2, w_up_f, name="mm_up", side=ici_down(w_down_f, 4, 8))
    act, (w_down_f,) = _sq_relu(up, tr, tcol, side=_gather_d2d([w_down_f], [False]))
    down = _matmul(act, w_down_f, name="mm_down")
    dx2, d_down, sums_fin, loss_blk = _final_fwd_bwd(x1, down, loss_target[0], g_final[None, :], gate_f, tr)

    d_up = _matmul(d_down, w_down_f, tb=True, out_dtype=BF16, name="mm_d_up",
                   post=lambda d_act, up_: d_act * 2.0 * jnp.maximum(up_, 0.0), post_args=(up,))
    g_w_down = _matmul(act, d_down, ta=True, out_dtype=BF16, name="mm_g_down")

    def scatter(pairs, cols, lo, hi, into=None):
        return _scatter_regions(pairs, cols, (lo, hi, 8), into)

    dh2, (got,) = _matmul(d_up, w_up_f, tb=True, name="mm_d_h2", side=_swap_halves([g_w_down], [False]))
    p_down = pair_sum('w_down', g_w_down, got, False)
    g_w_up, got_down = _matmul(h2, d_up, ta=True, out_dtype=BF16, name="mm_g_up", side=scatter([p_down], [False], 0, 4))
    (dx1, d_mix, sums2), (got,) = _norm_mod_bwd2(x1, dh2, dx2, mix, g_mlp, mod3, tr, side=_swap_halves([g_w_up], [True]))
    p_up = pair_sum('w_up', g_w_up, got, True)
    d_mrg = _matmul(d_mix, w_out_f, tb=True, name="mm_d_mrg")
    g_w_out = _matmul(mrg, d_mix, ta=True, out_dtype=BF16, name="mm_g_out")
    dz = _dz_start(t, n_in, n_ctx)
    d_ya, dz = _merge_bwd(d_mrg, y_attn, z, gl_off, dz, n_ctx, tr, tcol, "merge_bwd_attn")
    d_yr, dz = _merge_bwd(d_mrg, y_rnn, z, gl_off + d, dz, n_ctx, tr, tcol, "merge_bwd_rnn")
    d_o = _matmul(d_ya, w_oa_f, tb=True, out_dtype=BF16, name="mm_d_o")
    g_w_oa = _matmul(attn_o, d_ya, ta=True, out_dtype=BF16, name="mm_g_o_attn")
    d_u = _matmul(d_yr, w_or_f, tb=True, name="mm_d_u")
    g_w_or = _matmul(u, d_yr, ta=True, out_dtype=BF16, name="mm_g_o_rnn")
    d_rnn, dz = _rnn_gate_bwd(d_u, h_f, h_b, z, xg_off, dz, n_ctx, tr, tcol)
    gs_f = _scan(a_f, d_rnn, order="lat_ctx_down", post=True, n_ctx_rows=n_ctx, name="scan_f_bwd")
    gs_b = _scan(a_b, d_rnn, order="lat_ctx_up", post=True, n_ctx_rows=n_ctx, name="scan_b_bwd")
    o_names, o_grads = ['w_o_attn', 'w_o_rnn', 'w_out'], [g_w_oa, g_w_or, g_w_out]
    (dz, g_w_rg, g_w_ig, sums_rnn), (got_down, *got_o) = _rnn_bwd(
        z, xr_off, xc, gs_f, gs_b, h_f, h_b, conv_w_f, w_rg[0], b_rg_f, w_ig[0], b_ig_f, lam_f, dz, n_ctx,
        side=_sides(scatter([p_down], [False], 4, 8, got_down), _swap_halves(o_grads, row3)))
    hs_down = chip_sum('w_down', p_down, got_down, False)
    p_o = [pair_sum(n, g, o, False) for n, g, o in zip(o_names, o_grads, got_o)]
    gate_cols = 8 * LANES if g_w_rg.size % (8 * LANES * N_CHIPS * 16) == 0 else 2 * LANES
    gate_rows = g_w_rg.size // gate_cols
    gate_grads = [g_w_rg.reshape(gate_rows, gate_cols), g_w_ig.reshape(gate_rows, gate_cols)]
    (dq, dk, dz), (got_up, got_oa, got_rg, got_ig, gs_down) = _attn_bwd(
        qr, kr, z, v_off, d_o, attn_o, lse, dz, n_ctx, group, tr, side=_sides(
            _scatter_regions([p_up], [True]), _scatter_regions(p_o[:1], row3[:1]), _swap_halves(gate_grads, row3[:2]),
            _join_halves([hs_down], [False])))
    hs_up = chip_sum('w_up', p_up, got_up, True)
    hs_oa = chip_sum('w_o_attn', p_o[0], got_oa, False)
    p_gate = [pair_sum(n, g, o, False) for n, g, o in zip(GATES, gate_grads, (got_rg, got_ig))]
    dz, g_q_gain = _head_prep_bwd(z, 0, attn_w // LANES, q_gain, cos, sin, dq, n_ctx, tr, dz, "q_prep_bwd")
    dz, g_k_gain = _head_prep_bwd(z, k_off, kv_w // LANES, k_gain, cos, sin, dk, 0, tr, dz, "k_prep_bwd")
    g_w_in, (got_or, got_out, got_rg, got_ig, gs_up, gs_oa) = _matmul(
        h, dz, ta=True, out_dtype=BF16, name="mm_g_in", side=_sides(
            _scatter_regions(p_o[1:] + p_gate, [False] * 4), _join_halves([hs_up], [True]), _join_halves([hs_oa], [False])))
    hs_late = [chip_sum(n, p, o, False)
               for n, p, o in zip(o_names[1:] + GATES, p_o[1:] + p_gate, (got_or, got_out, got_rg, got_ig))]
    (got,) = _exchange(_swap_halves([g_w_in], [True]), "swap_w_in")
    p_in = pair_sum('w_in', g_w_in, got, True)
    dh, (got_in, fold_in) = _matmul(dz, w_in_f, tb=True, name="mm_d_h", side=_scatter_forward_1(p_in))
    nbr_chips = jnp.stack([2 * (1 - xi) + yi, 2 * xi + 1 - yi]).astype(jnp.int32)
    passed = _fold_forwarded(p_in, fold_in, nbr_chips, "fold_w_in")
    (got_in,) = _exchange(_scatter_forward_2(passed, got_in), "scatter_w_in_fwd")
    grad_x, sums1 = _norm_mod_bwd1(xcat, dh, dx1, g_mix, mod4, n_ctx, tr)

    zeros_d = jnp.zeros((d,), F32)
    dmod_lat = jnp.concatenate([sums1[0], sums1[1], sums2[3], sums2[0], sums2[1], sums_fin[1]])
    dmod_ctx = jnp.concatenate([sums1[3], sums1[4]] + [zeros_d] * 4)
    small_parts = [dmod_lat, dmod_ctx, loss_blk[0, 0:1], sums1[2] + sums1[5], sums2[2], g_q_gain[0], g_k_gain[0],
                   sums_rnn[10], sums_fin[0], sums_rnn[6:10], sums_rnn[0:2], sums_rnn[2:4], sums_rnn[4:6]]
    pack1 = _pack(small_parts)
    got1 = _all_gather8(pack1, "gather_small_grads").reshape(N_DEV, -1, LANES)
    tot1 = _sum_leading(got1, "sum_small_grads")
    part_shapes = [a.shape for a in small_parts]
    (s_dmod_lat, s_dmod_ctx, s_loss, g_g_mix, g_g_mlp, g_q_gain, g_k_gain, g_conv_b, g_g_final,
     g_conv_w_f, g_b_rg_f, g_b_ig_f, g_lam_f) = _unpack(tot1, part_shapes)
    loss = s_loss[0]
    g_b_mod = (s_dmod_lat + s_dmod_ctx)[None, :]
    n_mod_rows = _part_rows(d_mod)
    dmod16 = jnp.concatenate([got1[:, :n_mod_rows].reshape(N_DEV, -1)[:, :d_mod], s_dmod_ctx[None, :],
                              jnp.zeros((16 - N_DEV - 1, d_mod), F32)], axis=0)
    dmod16_shard = lax.dynamic_slice(dmod16, (0, chip * (d_mod // N_CHIPS)), (16, d_mod // N_CHIPS))
    g_w_mod = _matmul(silu16, dmod16_shard, ta=True, name="mm_g_mod")
    dsilu_part = _matmul(dmod16_shard[N_DEV:], w_mod[0], tb=True, name="mm_d_silu")
    dsilu_all = _all_gather8(dsilu_part, "gather_d_silu").reshape(N_DEV, 8, d)
    g_c_ctx = _c_ctx_grad(dsilu_all, c_ctx[None, :])[0]

    def shard_of(full):
        w = full.shape[-1] // N_CHIPS
        return lax.dynamic_slice(full, (0, chip * w), (full.shape[0], w))

    grads = {
        'c_ctx': g_c_ctx, 'b_mod': g_b_mod, 'g_mix': g_g_mix[None, :], 'g_mlp': g_g_mlp[None, :],
        'q_gain': g_q_gain[None, :], 'k_gain': g_k_gain[None, :], 'conv_b': g_conv_b[None, :],
        'g_final': g_g_final,
        'conv_w': shard_of(g_conv_w_f)[None], 'b_rg': shard_of(g_b_rg_f)[None], 'b_ig': shard_of(g_b_ig_f)[None],
        'lru_lambda': shard_of(g_lam_f)[None], 'w_mod': g_w_mod[None],
    }

    delta, new_m, new_v = {}, {}, {}

    def adamw(n):
        shp = weights[n].shape
        as2d = (lambda a: a[0]) if n not in GATES else (lambda a: a.reshape(-1, LANES))
        dl, nm, nv, g = _adamw(as2d(weights[n]), as2d(grads[n]), as2d(moms[n]), as2d(vars_[n]), "adamw_" + n)
        delta[n], new_m[n], new_v[n], grads[n] = dl.reshape(shp), nm.reshape(shp), nv.reshape(shp), g.reshape(shp)

    hs_in = chip_sum('w_in', p_in, got_in, True)
    gs_in, gs_or, gs_out = _exchange(_join_halves([hs_in] + hs_late[:2], [True, False, False]), "join_rest")
    for n, g in zip(['w_in', 'w_o_attn', 'w_o_rnn', 'w_out', 'w_up', 'w_down'], [gs_in, gs_oa, gs_or, gs_out, gs_up, gs_down]):
        grads[n] = g[None]
    half_cols = gate_cols // 2
    mine = jnp.concatenate([lax.dynamic_slice(hs, (0, ci * half_cols), (gate_rows // N_CHIPS, half_cols))
                            for hs in hs_late[2:]], axis=0)
    gate_all = _all_gather8(mine, "gather_gate_grads")
    gate_all = gate_all.reshape(N_CHIPS, 2, len(GATES), gate_rows // N_CHIPS, half_cols)
    for i, n in enumerate(GATES):
        grads[n] = jnp.moveaxis(gate_all[:, :, i], 1, 2).reshape(weights[n].shape)
    for n in ['w_mod'] + BIG + GATES:
        adamw(n)
    small_shapes = [weights[n].shape for n in SMALL]
    packed = [_pack([src[n] for n in SMALL], 512) for src in (weights, grads, moms, vars_)]
    outs = _adamw(*packed, "adamw_small")
    for res, out in zip((delta, new_m, new_v), outs):
        for n, a in zip(SMALL, _unpack(out, small_shapes)):
            res[n] = a
    return (loss, grad_x[None], *[grads[n] for n in WEIGHT_NAMES], *[delta[n] for n in WEIGHT_NAMES],
            *[new_m[n] for n in WEIGHT_NAMES], *[new_v[n] for n in WEIGHT_NAMES])
```

```python
import functools
import math
from typing import Callable, NamedTuple

import jax
import jax.numpy as jnp
from jax import lax
from jax.experimental import pallas as pl
from jax.experimental.pallas import tpu as pltpu

F32 = jnp.float32
BF16 = jnp.bfloat16
MESH_ID = pl.DeviceIdType.MESH
ANY = pl.BlockSpec(memory_space=pl.ANY)

NORM_EPS = 1e-6
LRU_C = 8.0
GRID_W = 64
ROPE_THETA = 10000.0
N_MOD = 6
CONV_WIDTH = 4
ADAM_LR = 0.001
ADAM_B1 = 0.9
ADAM_B2 = 0.999
ADAM_EPS = 1e-08
ADAM_WD = 0.01
ADAM_STEP = 10

LANES = 128
SUBLANES = 8
V7X_VMEM_LIMIT = 48 * 1024 * 1024
N_CHIPS = 4
N_DEV = 8
GELU_C = math.sqrt(2.0 / math.pi)
GELU_A = 0.044715


def _tile(dim, pref, align):
    t = min(pref, dim)
    t -= t % align
    while t >= align:
        if dim % t == 0:
            return t
        t -= align
    return dim


def _params(*sem):
    return pltpu.CompilerParams(dimension_semantics=sem, vmem_limit_bytes=V7X_VMEM_LIMIT)


def _sds(shape, dtype=F32):
    return jax.ShapeDtypeStruct(shape, dtype)


class _Side(NamedTuple):
    operands: tuple
    results: tuple
    aliases: dict
    n_sems: int
    build: Callable


def _sides(*sides):
    ops, res, aliases, spans, n = [], [], {}, [], 0
    for s in sides:
        spans.append((len(ops), len(res), n))
        aliases.update({len(ops) + i: len(res) + j for i, j in s.aliases.items()})
        ops += s.operands
        res += s.results
        n += s.n_sems

    def build(op_refs, res_refs, send_sems, recv_sems, sem0):
        sends, recvs = [], []
        for s, (o, r, k) in zip(sides, spans):
            a, b = s.build(op_refs[o:o + len(s.operands)], res_refs[r:r + len(s.results)], send_sems, recv_sems,
                           sem0 + k)
            sends += a
            recvs += b
        return sends, recvs

    return _Side(tuple(ops), tuple(res), aliases, n, build)


def _on_same(*sides):
    def build(ops, res, send_sems, recv_sems, sem0):
        sends, recvs = [], []
        for s in sides:
            a, b = s.build(ops, res, send_sems, recv_sems, sem0)
            sends += a
            recvs += b
            sem0 += s.n_sems
        return sends, recvs

    return _Side(sides[0].operands, sides[0].results, sides[0].aliases, sum(s.n_sems for s in sides), build)


def _call(body, *, side=None, sem=(), grid=(), in_specs=(), out_specs=(), out_shape=(), scratch_shapes=(), **kw):
    if side is None:
        return pl.pallas_call(body, grid=grid, in_specs=list(in_specs), out_specs=out_specs, out_shape=out_shape,
                              scratch_shapes=list(scratch_shapes), compiler_params=_params(*sem), **kw)
    aliases = kw.pop("input_output_aliases", {})
    many = isinstance(out_shape, (list, tuple))
    out_specs_l, out_shape_l = (list(out_specs), list(out_shape)) if many else ([out_specs], [out_shape])
    n_in, n_out, n_scr = len(in_specs), len(out_shape_l), len(scratch_shapes)
    n_op, n_res = len(side.operands), len(side.results)

    def hosted(*refs):
        ins, ops = refs[:n_in], refs[n_in:n_in + n_op]
        outs = refs[n_in + n_op:n_in + n_op + n_out]
        res = refs[n_in + n_op + n_out:n_in + n_op + n_out + n_res]
        scr = refs[n_in + n_op + n_out + n_res:-2]
        send_sems, recv_sems = refs[-2:]

        def start():
            for cp in side.build(ops, res, send_sems, recv_sems, 0)[0]:
                cp.start()

        def finish():
            sends, recvs = side.build(ops, res, send_sems, recv_sems, 0)
            for cp in recvs:
                cp.wait_recv()
            for cp in sends:
                cp.wait_send()

        if not grid:
            start()
            finish()
            return
        ids = [pl.program_id(a) for a in range(len(grid))]
        first = functools.reduce(jnp.logical_and, [i == 0 for i in ids])
        last = functools.reduce(jnp.logical_and, [i == g - 1 for i, g in zip(ids, grid)])
        pl.when(first)(start)
        body(*ins, *outs, *scr)
        pl.when(last)(finish)

    def run(*args):
        got = pl.pallas_call(
            hosted, grid=grid, in_specs=[*in_specs, *[ANY] * n_op], out_specs=[*out_specs_l, *[ANY] * n_res],
            out_shape=[*out_shape_l, *side.results],
            scratch_shapes=[*scratch_shapes, pltpu.SemaphoreType.DMA((side.n_sems,)),
                            pltpu.SemaphoreType.DMA((side.n_sems,))],
            input_output_aliases={**aliases, **{n_in + i: n_out + j for i, j in side.aliases.items()}},
            compiler_params=_params(*["arbitrary"] * len(grid)), **kw)(*args, *side.operands)
        own = list(got[:n_out]) if many else got[0]
        return own, list(got[n_out:])

    return run


def _matmul(a, b, *, ta=False, tb=False, out_dtype=F32, name, tm=1024, tn=1024, tk=2816, side=None, post=None,
            post_args=()):
    k_dim, m = a.shape if ta else a.shape[::-1]
    n, k2 = b.shape if tb else b.shape[::-1]
    assert k_dim == k2, (a.shape, b.shape, ta, tb)
    tm = _tile(m, tm, LANES if ta else 16)
    tn = _tile(n, tn, 16 if tb else LANES)
    tk = _tile(k_dim, tk, LANES)
    nk = k_dim // tk
    dims = (((0 if ta else 1,), (1 if tb else 0,)), ((), ()))
    if nk == 1:
        def whole(a_ref, b_ref, *rest):
            acc = lax.dot_general(a_ref[...].astype(BF16), b_ref[...].astype(BF16), dims, preferred_element_type=F32)
            if post is not None:
                acc = post(acc, *[r[...] for r in rest[:-1]])
            rest[-1][...] = acc.astype(rest[-1].dtype)

        a_spec = pl.BlockSpec((tk, tm), lambda i, j: (0, i)) if ta else pl.BlockSpec((tm, tk), lambda i, j: (i, 0))
        b_spec = pl.BlockSpec((tn, tk), lambda i, j: (j, 0)) if tb else pl.BlockSpec((tk, tn), lambda i, j: (0, j))
        o_spec = pl.BlockSpec((tm, tn), lambda i, j: (i, j))
        return _call(
            whole, side=side, name=name, grid=(m // tm, n // tn), in_specs=[a_spec, b_spec] + [o_spec] * len(post_args),
            out_specs=o_spec, out_shape=_sds((m, n), out_dtype), sem=("parallel", "parallel"),
        )(a, b, *post_args)
    assert post is None

    def body(a_ref, b_ref, o_ref, acc_ref):
        k = pl.program_id(2)

        @pl.when(k == 0)
        def _():
            acc_ref[...] = jnp.zeros_like(acc_ref)

        acc_ref[...] += lax.dot_general(a_ref[...].astype(BF16), b_ref[...].astype(BF16), dims,
                                        preferred_element_type=F32)

        @pl.when(k == nk - 1)
        def _():
            o_ref[...] = acc_ref[...].astype(o_ref.dtype)

    a_spec = pl.BlockSpec((tk, tm), lambda i, j, k: (k, i)) if ta else pl.BlockSpec((tm, tk), lambda i, j, k: (i, k))
    b_spec = pl.BlockSpec((tn, tk), lambda i, j, k: (j, k)) if tb else pl.BlockSpec((tk, tn), lambda i, j, k: (k, j))
    return _call(
        body, side=side, name=name, grid=(m // tm, n // tn, nk), in_specs=[a_spec, b_spec],
        out_specs=pl.BlockSpec((tm, tn), lambda i, j, k: (i, j)), out_shape=_sds((m, n), out_dtype),
        scratch_shapes=[pltpu.VMEM((tm, tn), F32)], sem=("parallel", "parallel", "arbitrary"),
    )(a, b)


def _silu(x):
    return x * jax.nn.sigmoid(x)


def _gelu(x):
    return 0.5 * x * (1.0 + jnp.tanh(GELU_C * (x + GELU_A * x * x * x)))


def _gelu_grad(x):
    t = jnp.tanh(GELU_C * (x + GELU_A * x * x * x))
    return 0.5 * (1.0 + t) + 0.5 * x * (1.0 - t * t) * GELU_C * (1.0 + 3.0 * GELU_A * x * x)


def _expm1_nonpos(x):
    series = x * (1.0 + x * (1.0 / 2 + x * (1.0 / 6 + x * (1.0 / 24 + x * (1.0 / 120 + x * (1.0 / 720 + x / 5040))))))
    return jnp.where(x > -0.25, series, jnp.exp(x) - 1.0)


def _softplus(x):
    return jnp.maximum(x, 0.0) + jnp.log1p(jnp.exp(-jnp.abs(x)))


def _rms_stats(x):
    return lax.rsqrt(jnp.mean(x * x, axis=-1, keepdims=True) + NORM_EPS)


def _rms_bwd(dxhat, xhat, rstd):
    return rstd * (dxhat - xhat * jnp.mean(dxhat * xhat, axis=-1, keepdims=True))


def _colsum(v):
    return jnp.sum(v, axis=0, keepdims=True)


def _mod_fwd(c16, w_mod, b_mod_shard):
    r, d = c16.shape
    n = w_mod.shape[1]
    tn = _tile(n, 512, LANES)

    def body(c_ref, w_ref, b_ref, o_ref, s_ref):
        s = _silu(c_ref[...])
        s_ref[...] = s
        o_ref[...] = jnp.dot(s.astype(BF16), w_ref[...].astype(BF16), preferred_element_type=F32) + b_ref[...]

    return pl.pallas_call(
        body, name="mod_fwd", grid=(n // tn,),
        in_specs=[pl.BlockSpec((r, d), lambda j: (0, 0)), pl.BlockSpec((d, tn), lambda j: (0, j)),
                  pl.BlockSpec((1, tn), lambda j: (0, j))],
        out_specs=[pl.BlockSpec((r, tn), lambda j: (0, j)), pl.BlockSpec((r, d), lambda j: (0, 0))],
        out_shape=[_sds((r, n)), _sds((r, d))], compiler_params=_params("arbitrary"),
    )(c16, w_mod, b_mod_shard)


def _c_ctx_grad(parts, c_ctx_row):
    d = c_ctx_row.shape[1]

    def body(p_ref, c_ref, o_ref):
        tot = p_ref[0, 0:1, :]
        for chip in range(1, N_CHIPS):
            tot = tot + p_ref[2 * chip, 0:1, :]
        c = c_ref[...]
        sg = jax.nn.sigmoid(c)
        o_ref[...] = tot * (sg * (1.0 + c * (1.0 - sg)))

    return pl.pallas_call(body, name="c_ctx_grad", out_shape=_sds((1, d)), compiler_params=_params())(parts, c_ctx_row)


def _norm_mod_fwd(xcat, g, mod4, n_ctx_rows, tr):
    t, d = xcat.shape
    nctx = n_ctx_rows // tr

    def body(x_ref, g_ref, mod_ref, h_ref):
        x = x_ref[...]
        n = x * _rms_stats(x) * g_ref[...]
        is_ctx = pl.program_id(0) < nctx
        sh = jnp.where(is_ctx, mod_ref[0:1, :], mod_ref[2:3, :])
        sc = jnp.where(is_ctx, mod_ref[1:2, :], mod_ref[3:4, :])
        h_ref[...] = (n * (1.0 + sc) + sh).astype(BF16)

    return pl.pallas_call(
        body, name="norm_mod_fwd", grid=(t // tr,),
        in_specs=[pl.BlockSpec((tr, d), lambda i: (i, 0)), pl.BlockSpec((1, d), lambda i: (0, 0)),
                  pl.BlockSpec((4, d), lambda i: (0, 0))],
        out_specs=pl.BlockSpec((tr, d), lambda i: (i, 0)), out_shape=_sds((t, d), BF16),
        compiler_params=_params("parallel"),
    )(xcat, g, mod4)


def _norm_mod_bwd1(xcat, dh, dx1, g, mod4, n_ctx_rows, tr, side=None):
    t, d = xcat.shape
    nctx = n_ctx_rows // tr
    s = t - n_ctx_rows

    def body(x_ref, dh_ref, dx1_ref, g_ref, mod_ref, dx_ref, sums_ref):
        i = pl.program_id(0)
        is_ctx = i < nctx

        @pl.when(i == 0)
        def _():
            sums_ref[...] = jnp.zeros_like(sums_ref)

        x = x_ref[...]
        dh_ = dh_ref[...]
        rstd = _rms_stats(x)
        xhat = x * rstd
        gg = g_ref[...]
        sc = jnp.where(is_ctx, mod_ref[1:2, :], mod_ref[3:4, :])
        dxhat = dh_ * (1.0 + sc) * gg
        dx_ref[...] = dx1_ref[...] + _rms_bwd(dxhat, xhat, rstd)
        part = [_colsum(dh_), _colsum(dh_ * xhat * gg), _colsum(dh_ * (1.0 + sc) * xhat)]

        @pl.when(is_ctx)
        def _():
            for k, row in enumerate(part):
                sums_ref[3 + k:4 + k, :] += row

        @pl.when(jnp.logical_not(is_ctx))
        def _():
            for k, row in enumerate(part):
                sums_ref[k:k + 1, :] += row

    lat = lambda i: (jnp.maximum(i - nctx, 0), 0)
    return _call(
        body, side=side, name="norm_mod_bwd1", grid=(t // tr,),
        in_specs=[pl.BlockSpec((tr, d), lambda i: (i, 0)), pl.BlockSpec((tr, d), lambda i: (i, 0)),
                  pl.BlockSpec((tr, d), lat), pl.BlockSpec((1, d), lambda i: (0, 0)),
                  pl.BlockSpec((4, d), lambda i: (0, 0))],
        out_specs=[pl.BlockSpec((tr, d), lat), pl.BlockSpec((8, d), lambda i: (0, 0))],
        out_shape=[_sds((s, d)), _sds((8, d))], sem=("arbitrary",),
    )(xcat, dh, dx1, g, mod4)


def _resid_norm_mod_fwd(x, mix, g, mod3, tr):
    s, d = x.shape

    def body(x_ref, mix_ref, g_ref, mod_ref, x1_ref, h_ref):
        x1 = x_ref[...] + mod_ref[0:1, :] * mix_ref[...]
        x1_ref[...] = x1
        n = x1 * _rms_stats(x1) * g_ref[...]
        h_ref[...] = (n * (1.0 + mod_ref[2:3, :]) + mod_ref[1:2, :]).astype(BF16)

    row = pl.BlockSpec((tr, d), lambda i: (i, 0))
    return pl.pallas_call(
        body, name="resid_norm_mod_fwd", grid=(s // tr,),
        in_specs=[row, row, pl.BlockSpec((1, d), lambda i: (0, 0)), pl.BlockSpec((3, d), lambda i: (0, 0))],
        out_specs=[row, row], out_shape=[_sds((s, d)), _sds((s, d), BF16)], compiler_params=_params("parallel"),
    )(x, mix, g, mod3)


def _norm_mod_bwd2(x1, dh2, dx2, mix, g, mod3, tr, side=None):
    s, d = x1.shape

    def body(x_ref, dh_ref, dx2_ref, mix_ref, g_ref, mod_ref, dx1_ref, dmix_ref, sums_ref):
        @pl.when(pl.program_id(0) == 0)
        def _():
            sums_ref[...] = jnp.zeros_like(sums_ref)

        x = x_ref[...]
        dh_ = dh_ref[...]
        rstd = _rms_stats(x)
        xhat = x * rstd
        gg = g_ref[...]
        sc = mod_ref[2:3, :]
        dx1 = dx2_ref[...] + _rms_bwd(dh_ * (1.0 + sc) * gg, xhat, rstd)
        dx1_ref[...] = dx1
        dmix_ref[...] = (dx1 * mod_ref[0:1, :]).astype(BF16)
        part = [_colsum(dh_), _colsum(dh_ * xhat * gg), _colsum(dh_ * (1.0 + sc) * xhat), _colsum(dx1 * mix_ref[...])]
        for k, row in enumerate(part):
            sums_ref[k:k + 1, :] += row

    row = pl.BlockSpec((tr, d), lambda i: (i, 0))
    return _call(
        body, side=side, name="norm_mod_bwd2", grid=(s // tr,),
        in_specs=[row, row, row, row, pl.BlockSpec((1, d), lambda i: (0, 0)), pl.BlockSpec((3, d), lambda i: (0, 0))],
        out_specs=[row, row, pl.BlockSpec((8, d), lambda i: (0, 0))],
        out_shape=[_sds((s, d)), _sds((s, d), BF16), _sds((8, d))], sem=("arbitrary",),
    )(x1, dh2, dx2, mix, g, mod3)


def _final_fwd_bwd(x1, down, target, g_final, gate, tr):
    s, d = x1.shape

    def body(x1_ref, down_ref, t_ref, g_ref, gate_ref, dx2_ref, ddown_ref, sums_ref, loss_ref):
        @pl.when(pl.program_id(0) == 0)
        def _():
            sums_ref[...] = jnp.zeros_like(sums_ref)
            loss_ref[...] = jnp.zeros_like(loss_ref)

        down_ = down_ref[...]
        gate_ = gate_ref[...]
        x2 = x1_ref[...] + gate_ * down_
        rstd = _rms_stats(x2)
        xhat = x2 * rstd
        gg = g_ref[...]
        err = xhat * gg - t_ref[...]
        loss_ref[...] += 0.5 * jnp.sum(jnp.mean(err * err, axis=-1, keepdims=True))
        dy = err * (1.0 / d)
        dx2 = _rms_bwd(dy * gg, xhat, rstd)
        dx2_ref[...] = dx2
        ddown_ref[...] = (dx2 * gate_).astype(BF16)
        sums_ref[0:1, :] += _colsum(dy * xhat)
        sums_ref[1:2, :] += _colsum(dx2 * down_)

    row = pl.BlockSpec((tr, d), lambda i: (i, 0))
    vec = pl.BlockSpec((1, d), lambda i: (0, 0))
    return pl.pallas_call(
        body, name="final_fwd_bwd", grid=(s // tr,), in_specs=[row, row, row, vec, vec],
        out_specs=[row, row, pl.BlockSpec((8, d), lambda i: (0, 0)), pl.BlockSpec((8, LANES), lambda i: (0, 0))],
        out_shape=[_sds((s, d)), _sds((s, d), BF16), _sds((8, d)), _sds((8, LANES))],
        compiler_params=_params("arbitrary"),
    )(x1, down, target, g_final, gate)


def _swap_pairs(v):
    lane = lax.broadcasted_iota(jnp.int32, v.shape, 1)
    return jnp.where(lane % 2 == 0, pltpu.roll(v, LANES - 1, 1), pltpu.roll(v, 1, 1))


def _head_prep_fwd(z, col_off, n_heads, gain, cos, sin, tr, name, side=None):
    t = z.shape[0]
    per = math.gcd(4, n_heads, col_off // LANES)
    w = per * LANES
    hb = col_off // w

    def body(z_ref, g_ref, cos_ref, sin_ref, o_ref):
        for hh in range(per):
            cols = slice(hh * LANES, (hh + 1) * LANES)
            x = z_ref[:, cols]
            y = x * _rms_stats(x) * g_ref[...]
            o_ref[:, cols] = (y * cos_ref[...] + _swap_pairs(y) * sin_ref[...]).astype(BF16)

    tab = pl.BlockSpec((tr, LANES), lambda i, j: (i, 0))
    return _call(
        body, side=side, name=name, grid=(t // tr, n_heads // per),
        in_specs=[pl.BlockSpec((tr, w), lambda i, j: (i, hb + j)), pl.BlockSpec((1, LANES), lambda i, j: (0, 0)),
                  tab, tab],
        out_specs=pl.BlockSpec((tr, w), lambda i, j: (i, j)), out_shape=_sds((t, n_heads * LANES), BF16),
        sem=("parallel", "parallel"),
    )(z, gain, cos, sin)


def _dz_start(t, n_in, n_ctx_rows):
    tcol = _tile(n_in, 1024, LANES)

    def body(o_ref):
        o_ref[...] = jnp.zeros_like(o_ref)

    return pl.pallas_call(body, name="dz_start", grid=(n_in // tcol,),
                          out_specs=pl.BlockSpec((n_ctx_rows, tcol), lambda j: (0, j)), out_shape=_sds((t, n_in), BF16),
                          compiler_params=_params("parallel"))()


def _head_prep_bwd(z, col_off, n_heads, gain, cos, sin, dout, row_off, tr, dz, name, side=None):
    r = dout.shape[0]
    per = math.gcd(4, n_heads, col_off // LANES)
    w = per * LANES
    hb = col_off // w
    rb = row_off // tr

    def body(z_ref, g_ref, cos_ref, sin_ref, d_ref, _, dz_ref, dg_ref):
        @pl.when(jnp.logical_and(pl.program_id(0) == 0, pl.program_id(1) == 0))
        def _():
            dg_ref[...] = jnp.zeros_like(dg_ref)

        for hh in range(per):
            cols = slice(hh * LANES, (hh + 1) * LANES)
            x = z_ref[:, cols]
            rstd = _rms_stats(x)
            xhat = x * rstd
            dd = d_ref[:, cols]
            dy = dd * cos_ref[...] - _swap_pairs(dd) * sin_ref[...]
            dg_ref[0:1, :] += _colsum(dy * xhat)
            dz_ref[:, cols] = _rms_bwd(dy * g_ref[...], xhat, rstd).astype(BF16)

    tab = pl.BlockSpec((tr, LANES), lambda i, j: (rb + i, 0))
    window = pl.BlockSpec((tr, w), lambda i, j: (rb + i, hb + j))
    return _call(
        body, side=side, name=name, grid=(r // tr, n_heads // per),
        in_specs=[window, pl.BlockSpec((1, LANES), lambda i, j: (0, 0)), tab, tab,
                  pl.BlockSpec((tr, w), lambda i, j: (i, j)), ANY],
        out_specs=[window, pl.BlockSpec((8, LANES), lambda i, j: (0, 0))],
        out_shape=[_sds(dz.shape, BF16), _sds((8, LANES))], input_output_aliases={5: 0}, sem=("arbitrary", "arbitrary"),
    )(z, gain, cos, sin, dout, dz)


def _attn_fwd(qr, kr, z, v_off, n_ctx_rows, group, tq, side=None):
    t, kvw = kr.shape
    s = t - n_ctx_rows
    n_kv = kvw // LANES
    scale = LANES ** -0.5
    qb0 = n_ctx_rows // tq
    vb = v_off // LANES

    def body(q_ref, k_ref, v_ref, o_ref, lse_ref):
        k = k_ref[...]
        v = v_ref[...].astype(BF16)
        lse_ref[...] = jnp.zeros_like(lse_ref)
        for g in range(group):
            cols = slice(g * LANES, (g + 1) * LANES)
            sc = lax.dot_general(q_ref[:, cols], k, (((1,), (1,)), ((), ())), preferred_element_type=F32) * scale
            m = jnp.max(sc, axis=-1, keepdims=True)
            e = jnp.exp(sc - m)
            l = jnp.sum(e, axis=-1, keepdims=True)
            p = e * (1.0 / l)
            o_ref[:, cols] = jnp.dot(p.astype(BF16), v, preferred_element_type=F32).astype(BF16)
            lse_ref[:, g:g + 1] = m + jnp.log(l)

    return _call(
        body, side=side, name="attn_fwd", grid=(n_kv, s // tq),
        in_specs=[pl.BlockSpec((tq, group * LANES), lambda h, i: (qb0 + i, h)),
                  pl.BlockSpec((t, LANES), lambda h, i: (0, h)), pl.BlockSpec((t, LANES), lambda h, i: (0, vb + h))],
        out_specs=[pl.BlockSpec((tq, group * LANES), lambda h, i: (i, h)), pl.BlockSpec((tq, LANES), lambda h, i: (i, h))],
        out_shape=[_sds((s, n_kv * group * LANES), BF16), _sds((s, kvw))], sem=("parallel", "parallel"),
    )(qr, kr, z)


def _attn_bwd(qr, kr, z, v_off, d_o, attn_o, lse, dz, n_ctx_rows, group, tq, side=None):
    t, kvw = kr.shape
    s = t - n_ctx_rows
    n_kv = kvw // LANES
    scale = LANES ** -0.5
    qb0 = n_ctx_rows // tq
    vb = v_off // LANES
    n_q_blocks = s // tq
    tn_dims = (((0,), (0,)), ((), ()))
    nt_dims = (((1,), (1,)), ((), ()))

    def body(q_ref, k_ref, v_ref, do_ref, o_ref, lse_ref, _, dq_ref, dk_ref, dz_ref, dv_ref):
        @pl.when(pl.program_id(1) == 0)
        def _():
            dk_ref[...] = jnp.zeros_like(dk_ref)
            dv_ref[...] = jnp.zeros_like(dv_ref)

        k = k_ref[...]
        v = v_ref[...].astype(BF16)
        for g in range(group):
            cols = slice(g * LANES, (g + 1) * LANES)
            q = q_ref[:, cols]
            do_ = do_ref[:, cols]
            row_dot = jnp.sum(do_.astype(F32) * o_ref[:, cols].astype(F32), axis=-1, keepdims=True)
            sc = lax.dot_general(q, k, nt_dims, preferred_element_type=F32)
            p = jnp.exp(sc * scale - lse_ref[:, g:g + 1])
            dv_ref[...] += lax.dot_general(p.astype(BF16), do_, tn_dims, preferred_element_type=F32)
            dp = lax.dot_general(do_, v, nt_dims, preferred_element_type=F32)
            ds = (p * (dp - row_dot)).astype(BF16)
            dq_ref[:, cols] = jnp.dot(ds, k, preferred_element_type=F32) * scale
            dk_ref[...] += lax.dot_general(ds, q, tn_dims, preferred_element_type=F32)

        @pl.when(pl.program_id(1) == n_q_blocks - 1)
        def _():
            dk_ref[...] = dk_ref[...] * scale
            dz_ref[...] = dv_ref[...].astype(BF16)

    qspec = pl.BlockSpec((tq, group * LANES), lambda h, i: (qb0 + i, h))
    ospec = pl.BlockSpec((tq, group * LANES), lambda h, i: (i, h))
    kspec = pl.BlockSpec((t, LANES), lambda h, i: (0, h))
    vspec = pl.BlockSpec((t, LANES), lambda h, i: (0, vb + h))
    return _call(
        body, side=side, name="attn_bwd", grid=(n_kv, n_q_blocks),
        in_specs=[qspec, kspec, vspec, ospec, ospec, pl.BlockSpec((tq, LANES), lambda h, i: (i, h)), ANY],
        out_specs=[ospec, kspec, vspec], out_shape=[_sds((s, n_kv * group * LANES)), _sds((t, kvw)), _sds(dz.shape, BF16)],
        scratch_shapes=[pltpu.VMEM((t, LANES), F32)], input_output_aliases={6: 2}, sem=("parallel", "arbitrary"),
    )(qr, kr, z, d_o, attn_o, lse, dz)


def _row_mask(shape, rows):
    r = lax.broadcasted_iota(jnp.int32, shape, 0)
    m = r == rows[0]
    for v in rows[1:]:
        m = jnp.logical_or(m, r == v)
    return m


def _shift_rows(x, k, n_ctx_rows):
    t = x.shape[0]
    if k == 0:
        return x
    rolled = pltpu.roll(x, (-k) % t, 0)
    if k > 0:
        dead = [n_ctx_rows - 1 - i for i in range(k)] + [t - 1 - i for i in range(k)]
    else:
        dead = [i for i in range(-k)] + [n_ctx_rows + i for i in range(-k)]
    return jnp.where(_row_mask(x.shape, dead), 0.0, rolled)


def _conv(x, w, b, n_ctx_rows):
    y = b
    for k in range(CONV_WIDTH):
        y = y + _shift_rows(x, k - 1, n_ctx_rows) * w[k:k + 1, :]
    return y


def _gates(xc_bf, w_r, b_r, w_i, b_i, lam):
    r = jax.nn.sigmoid(jnp.dot(xc_bf, w_r.astype(BF16), preferred_element_type=F32) + b_r)
    i = jax.nn.sigmoid(jnp.dot(xc_bf, w_i.astype(BF16), preferred_element_type=F32) + b_i)
    log_a = -LRU_C * r * _softplus(-lam)
    a = jnp.exp(log_a)
    mult = jnp.sqrt(-_expm1_nonpos(2.0 * log_a))
    return r, i, a, mult


def _rnn_specs(t, xr_off):
    xb = xr_off // LANES
    return dict(
        zcol=pl.BlockSpec((t, LANES), lambda j: (0, xb + j)), col=pl.BlockSpec((t, LANES), lambda j: (0, j)),
        conv_w=pl.BlockSpec((CONV_WIDTH, LANES), lambda j: (0, j)), vec=pl.BlockSpec((1, LANES), lambda j: (0, j)),
        gate_w=pl.BlockSpec((2, 1, LANES, LANES), lambda j: (0, j, 0, 0)), two=pl.BlockSpec((2, LANES), lambda j: (0, j)))


def _rnn_prep(z, xr_off, conv_w, conv_b, w_rg, b_rg, w_ig, b_ig, lam, n_ctx_rows, side=None):
    t = z.shape[0]
    d = conv_b.shape[1]
    sp = _rnn_specs(t, xr_off)

    def body(z_ref, cw_ref, cb_ref, wr_ref, br_ref, wi_ref, bi_ref, lam_ref, xc_ref, af_ref, bf_ref, ab_ref, bb_ref):
        xc = _conv(z_ref[...], cw_ref[...], cb_ref[...], n_ctx_rows)
        xc_ref[...] = xc
        xc_bf = xc.astype(BF16)
        for dr, (a_ref, b_ref) in enumerate(((af_ref, bf_ref), (ab_ref, bb_ref))):
            _, i, a, mult = _gates(xc_bf, wr_ref[dr, 0], br_ref[dr:dr + 1, :], wi_ref[dr, 0], bi_ref[dr:dr + 1, :],
                                   lam_ref[dr:dr + 1, :])
            a_ref[...] = a
            b_ref[...] = mult * (i * xc)

    return _call(
        body, side=side, name="rnn_prep", grid=(d // LANES,),
        in_specs=[sp["zcol"], sp["conv_w"], sp["vec"], sp["gate_w"], sp["two"], sp["gate_w"], sp["two"], sp["two"]],
        out_specs=[sp["col"]] * 5, out_shape=[_sds((t, d))] * 5, sem=("parallel",),
    )(z, conv_w, conv_b, w_rg, b_rg, w_ig, b_ig, lam)


def _scan(a, b, *, order, post, n_ctx_rows, name, tc=256, side=None):
    t, d = a.shape
    tc = _tile(math.gcd(n_ctx_rows, t - n_ctx_rows), tc, SUBLANES)
    nt, nctx = t // tc, n_ctx_rows // tc
    nlat = nt - nctx
    b_lat_only = b.shape[0] != t
    up = order.endswith("up")

    def chunk(i):
        if order == "ctx_lat_up":
            return i
        if order == "lat_ctx_down":
            return nt - 1 - i
        if order == "ctx_lat_down":
            return jnp.where(i < nctx, nctx - 1 - i, nt - 1 - (i - nctx))
        return jnp.where(i < nlat, nctx + i, i - nlat)

    def body(a_ref, b_ref, o_ref, carry_ref):
        @pl.when(pl.program_id(0) == 0)
        def _():
            carry_ref[...] = jnp.zeros_like(carry_ref)

        live = jnp.where(chunk(pl.program_id(0)) >= nctx, 1.0, 0.0) if b_lat_only else None

        def group(gi, carry):
            base = pl.multiple_of((gi if up else tc // SUBLANES - 1 - gi) * SUBLANES, SUBLANES)
            for r in (range(SUBLANES) if up else range(SUBLANES - 1, -1, -1)):
                a_r = a_ref[pl.ds(base + r, 1), :]
                b_r = b_ref[pl.ds(base + r, 1), :]
                if live is not None:
                    b_r = b_r * live
                if post:
                    out = b_r + carry
                    carry = a_r * out
                else:
                    out = a_r * carry + b_r
                    carry = out
                o_ref[pl.ds(base + r, 1), :] = out
            return carry

        carry_ref[0:1, :] = lax.fori_loop(0, tc // SUBLANES, group, carry_ref[0:1, :])

    full = pl.BlockSpec((tc, d), lambda i: (chunk(i), 0))
    b_spec = pl.BlockSpec((tc, d), lambda i: (jnp.maximum(chunk(i) - nctx, 0), 0)) if b_lat_only else full
    return _call(
        body, side=side, name=name, grid=(nt,), in_specs=[full, b_spec], out_specs=full, out_shape=_sds((t, d)),
        scratch_shapes=[pltpu.VMEM((SUBLANES, d), F32)], sem=("arbitrary",),
    )(a, b)


def _rnn_bwd(z, xr_off, xc, g_f, g_b, h_f, h_b, conv_w, w_rg, b_rg, w_ig, b_ig, lam, dz, n_ctx_rows, side=None):
    t, d = xc.shape
    sp = _rnn_specs(t, xr_off)
    tn_dims = (((0,), (0,)), ((), ()))
    nt_dims = (((1,), (1,)), ((), ()))

    def body(z_ref, xc_ref, gf_ref, gb_ref, hf_ref, hb_ref, cw_ref, wr_ref, br_ref, wi_ref, bi_ref, lam_ref, _,
             dxr_ref, dwr_ref, dwi_ref, sums_ref):
        xc_ = xc_ref[...]
        xc_bf = xc_.astype(BF16)
        dxc = jnp.zeros_like(xc_)
        sums = [None] * 6
        for dr, (g_ref, h_ref) in enumerate(((gf_ref, hf_ref), (gb_ref, hb_ref))):
            w_r, w_i, lam_ = wr_ref[dr, 0], wi_ref[dr, 0], lam_ref[dr:dr + 1, :]
            r, i, a, mult = _gates(xc_bf, w_r, br_ref[dr:dr + 1, :], w_i, bi_ref[dr:dr + 1, :], lam_)
            g = g_ref[...]
            h = h_ref[...]
            if dr == 0:
                h_prev = jnp.where(_row_mask(h.shape, [0]), 0.0, pltpu.roll(h, 1, 0))
            else:
                h_prev = jnp.where(_row_mask(h.shape, [n_ctx_rows - 1]), 0.0, pltpu.roll(h, t - 1, 0))
            d_mult = g * i * xc_
            d_i = g * mult * xc_
            dxc = dxc + g * mult * i
            d_log_a = g * h_prev * a - d_mult * a * a / mult
            sp_ = _softplus(-lam_)
            d_r = d_log_a * (-LRU_C) * sp_
            d_sp = _colsum(d_log_a * (-LRU_C) * r)
            du_r = (d_r * r * (1.0 - r))
            du_i = (d_i * i * (1.0 - i))
            sums[dr] = _colsum(du_r)
            sums[2 + dr] = _colsum(du_i)
            sums[4 + dr] = d_sp * (-jax.nn.sigmoid(-lam_))
            du_r_bf, du_i_bf = du_r.astype(BF16), du_i.astype(BF16)
            dwr_ref[dr, 0] = lax.dot_general(xc_bf, du_r_bf, tn_dims, preferred_element_type=F32).astype(BF16)
            dwi_ref[dr, 0] = lax.dot_general(xc_bf, du_i_bf, tn_dims, preferred_element_type=F32).astype(BF16)
            dxc = dxc + lax.dot_general(du_r_bf, w_r.astype(BF16), nt_dims, preferred_element_type=F32)
            dxc = dxc + lax.dot_general(du_i_bf, w_i.astype(BF16), nt_dims, preferred_element_type=F32)
        xr = z_ref[...]
        cw = cw_ref[...]
        dxr = jnp.zeros_like(dxc)
        rows = list(sums)
        for k in range(CONV_WIDTH):
            dxr = dxr + _shift_rows(dxc, 1 - k, n_ctx_rows) * cw[k:k + 1, :]
            rows.append(_colsum(dxc * _shift_rows(xr, k - 1, n_ctx_rows)))
        rows.append(_colsum(dxc))
        dxr_ref[...] = dxr.astype(BF16)
        sums_ref[...] = jnp.zeros_like(sums_ref)
        for k, row in enumerate(rows):
            sums_ref[k:k + 1, :] = row

    return _call(
        body, side=side, name="rnn_bwd", grid=(d // LANES,),
        in_specs=[sp["zcol"]] + [sp["col"]] * 5 + [sp["conv_w"], sp["gate_w"], sp["two"], sp["gate_w"], sp["two"],
                                                  sp["two"], ANY],
        out_specs=[sp["zcol"], sp["gate_w"], sp["gate_w"], pl.BlockSpec((16, LANES), lambda j: (0, j))],
        out_shape=[_sds(dz.shape, BF16), _sds(w_rg.shape, BF16), _sds(w_ig.shape, BF16), _sds((16, d))],
        input_output_aliases={12: 0}, sem=("parallel",),
    )(z, xc, g_f, g_b, h_f, h_b, conv_w, w_rg, b_rg, w_ig, b_ig, lam, dz)


def _tiles2d(s, d, tr, tcol):
    return (s // tr, d // tcol), pl.BlockSpec((tr, tcol), lambda i, j: (i, j))


def _zspec(tr, tcol, row_off, col_off):
    rb, cb = row_off // tr, col_off // tcol
    return pl.BlockSpec((tr, tcol), lambda i, j: (rb + i, cb + j))


def _rnn_gate_fwd(h_f, h_b, z, xg_off, n_ctx_rows, tr, tcol, side=None):
    t, d = h_f.shape
    s = t - n_ctx_rows
    grid, out = _tiles2d(s, d, tr, tcol)
    hs = _zspec(tr, tcol, n_ctx_rows, 0)

    def body(hf_ref, hb_ref, xg_ref, u_ref):
        u_ref[...] = ((hf_ref[...] + hb_ref[...]) * _gelu(xg_ref[...])).astype(BF16)

    return _call(body, side=side, name="rnn_gate_fwd", grid=grid,
                 in_specs=[hs, hs, _zspec(tr, tcol, n_ctx_rows, xg_off)], out_specs=out, out_shape=_sds((s, d), BF16),
                 sem=("parallel", "parallel"))(h_f, h_b, z)


def _rnn_gate_bwd(d_u, h_f, h_b, z, xg_off, dz, n_ctx_rows, tr, tcol, side=None):
    t, d = h_f.shape
    s = t - n_ctx_rows
    grid, out = _tiles2d(s, d, tr, tcol)
    hs = _zspec(tr, tcol, n_ctx_rows, 0)
    window = _zspec(tr, tcol, n_ctx_rows, xg_off)

    def body(du_ref, hf_ref, hb_ref, xg_ref, _, dr_ref, dxg_ref):
        du = du_ref[...]
        xg = xg_ref[...]
        dr_ref[...] = du * _gelu(xg)
        dxg_ref[...] = (du * (hf_ref[...] + hb_ref[...]) * _gelu_grad(xg)).astype(BF16)

    return _call(body, side=side, name="rnn_gate_bwd", grid=grid, in_specs=[out, hs, hs, window, ANY],
                 out_specs=[out, window], out_shape=[_sds((s, d)), _sds(dz.shape, BF16)], input_output_aliases={4: 1},
                 sem=("parallel", "parallel"))(d_u, h_f, h_b, z, dz)


def _merge_fwd(y_attn, y_rnn, z, gl_off, n_ctx_rows, tr, tcol):
    s, d = y_attn.shape
    grid, out = _tiles2d(s, d, tr, tcol)

    def body(ya_ref, yr_ref, ga_ref, gr_ref, o_ref):
        o_ref[...] = (jax.nn.sigmoid(ga_ref[...]) * ya_ref[...] + jax.nn.sigmoid(gr_ref[...]) * yr_ref[...]).astype(BF16)

    return pl.pallas_call(
        body, name="merge_fwd", grid=grid,
        in_specs=[out, out, _zspec(tr, tcol, n_ctx_rows, gl_off), _zspec(tr, tcol, n_ctx_rows, gl_off + d)],
        out_specs=out, out_shape=_sds((s, d), BF16), compiler_params=_params("parallel", "parallel"),
    )(y_attn, y_rnn, z, z)


def _merge_bwd(d_mrg, y, z, gl_off, dz, n_ctx_rows, tr, tcol, name):
    s, d = y.shape
    grid, out = _tiles2d(s, d, tr, tcol)
    window = _zspec(tr, tcol, n_ctx_rows, gl_off)

    def body(dm_ref, y_ref, gl_ref, _, dy_ref, dgl_ref):
        dm = dm_ref[...]
        g = jax.nn.sigmoid(gl_ref[...])
        dy_ref[...] = (dm * g).astype(BF16)
        dgl_ref[...] = (dm * y_ref[...] * g * (1.0 - g)).astype(BF16)

    return _call(body, name=name, grid=grid, in_specs=[out, out, window, ANY], out_specs=[out, window],
                 out_shape=[_sds((s, d), BF16), _sds(dz.shape, BF16)], input_output_aliases={3: 1},
                 sem=("parallel", "parallel"))(d_mrg, y, z, dz)


def _sq_relu(up, tr, tcol, side=None):
    grid, out = _tiles2d(*up.shape, _tile(up.shape[0], 2 * tr, 16), _tile(up.shape[1], 4 * tcol, LANES))

    def body(u_ref, o_ref):
        r = jnp.maximum(u_ref[...], 0.0)
        o_ref[...] = (r * r).astype(BF16)

    return _call(body, side=side, name="sq_relu", grid=grid, in_specs=[out], out_specs=out,
                 out_shape=_sds(up.shape, BF16), sem=("parallel", "parallel"))(up)


def _cast_into_window(w, chip, col_sharded, name):
    r, c = w.shape
    tr, tcol = _tile(r, 512, 16), _tile(c, 1024, LANES)
    nrb, ncb = r // tr, c // tcol

    def body(chip_ref, w_ref, o_ref):
        o_ref[...] = w_ref[...].astype(BF16)

    if col_sharded:
        omap = lambda i, j, chip_ref: (i, chip_ref[0] * ncb + j)
    else:
        omap = lambda i, j, chip_ref: (chip_ref[0] * nrb + i, j)
    return pl.pallas_call(
        body, name=name,
        grid_spec=pltpu.PrefetchScalarGridSpec(
            num_scalar_prefetch=1, grid=(nrb, ncb),
            in_specs=[pl.BlockSpec((tr, tcol), lambda i, j, chip_ref: (i, j))], out_specs=pl.BlockSpec((tr, tcol), omap)),
        out_shape=_sds((r, c * N_CHIPS) if col_sharded else (r * N_CHIPS, c), BF16),
        compiler_params=_params("parallel", "parallel"),
    )(chip, w)


def _sum_leading(parts, name):
    n, r, c = parts.shape
    tr, tcol = _tile(r, 512, SUBLANES), _tile(c, 1024, LANES)

    def body(p_ref, o_ref):
        tot = p_ref[0]
        for k in range(1, n):
            tot = tot + p_ref[k]
        o_ref[...] = tot

    return pl.pallas_call(
        body, name=name, grid=(r // tr, c // tcol), in_specs=[pl.BlockSpec((n, tr, tcol), lambda i, j: (0, i, j))],
        out_specs=pl.BlockSpec((tr, tcol), lambda i, j: (i, j)), out_shape=_sds((r, c)),
        compiler_params=_params("parallel", "parallel"),
    )(parts)


def _add_half(full, other, core, split_rows, name):
    r, c = other.shape
    tr, tcol = _tile(r, 512, 16), _tile(c, 1024, LANES)
    nrb, ncb = r // tr, c // tcol

    def body(core_ref, f_ref, o_ref, out_ref):
        out_ref[...] = (f_ref[...].astype(F32) + o_ref[...].astype(F32)).astype(out_ref.dtype)

    if split_rows:
        fmap = lambda i, j, core_ref: (core_ref[0] * nrb + i, j)
    else:
        fmap = lambda i, j, core_ref: (i, core_ref[0] * ncb + j)
    same = lambda i, j, core_ref: (i, j)
    return pl.pallas_call(
        body, name=name,
        grid_spec=pltpu.PrefetchScalarGridSpec(
            num_scalar_prefetch=1, grid=(nrb, ncb),
            in_specs=[pl.BlockSpec((tr, tcol), fmap), pl.BlockSpec((tr, tcol), same)],
            out_specs=pl.BlockSpec((tr, tcol), same)),
        out_shape=_sds((r, c), BF16), compiler_params=_params("parallel", "parallel"),
    )(core, full, other)


def _sum_regions(pair, got, place, col_sharded, name):
    n_got, r, c = got.shape
    tr, tcol = _tile(r, 512, 16), _tile(c, 1024, LANES)
    nrb, ncb = r // tr, c // tcol

    def body(place_ref, p_ref, g_ref, out_ref):
        tot = p_ref[...].astype(F32)
        for k in range(n_got):
            tot = tot + g_ref[k].astype(F32)
        out_ref[...] = tot

    if col_sharded:
        pmap = lambda i, j, pr: (i, pr[0] * ncb + j)
        omap = lambda i, j, pr: (pr[1] * nrb + i, j)
        out_shape = (2 * r, c)
    else:
        pmap = lambda i, j, pr: (pr[0] * nrb + i, j)
        omap = lambda i, j, pr: (i, pr[1] * ncb + j)
        out_shape = (r, 2 * c)
    return pl.pallas_call(
        body, name=name,
        grid_spec=pltpu.PrefetchScalarGridSpec(
            num_scalar_prefetch=1, grid=(nrb, ncb),
            in_specs=[pl.BlockSpec((tr, tcol), pmap), pl.BlockSpec((n_got, tr, tcol), lambda i, j, pr: (0, i, j))],
            out_specs=pl.BlockSpec((tr, tcol), omap)),
        out_shape=_sds(out_shape), compiler_params=_params("parallel", "parallel"),
    )(place, pair, got)


def _fold_forwarded(pair, fold, nbr_chips, name):
    _, r, c = fold.shape
    tr, tcol = _tile(r, 512, 16), _tile(c, 1024, LANES)
    nrb, ncb = r // tr, c // tcol

    def body(nbr_ref, px_ref, py_ref, f_ref, out_ref):
        out_ref[0] = (px_ref[...].astype(F32) + f_ref[1].astype(F32)).astype(out_ref.dtype)
        out_ref[1] = (py_ref[...].astype(F32) + f_ref[0].astype(F32)).astype(out_ref.dtype)

    both = pl.BlockSpec((2, tr, tcol), lambda i, j, nb: (0, i, j))
    return pl.pallas_call(
        body, name=name,
        grid_spec=pltpu.PrefetchScalarGridSpec(
            num_scalar_prefetch=1, grid=(nrb, ncb),
            in_specs=[pl.BlockSpec((tr, tcol), lambda i, j, nb: (i, nb[0] * ncb + j)),
                      pl.BlockSpec((tr, tcol), lambda i, j, nb: (nrb + i, nb[1] * ncb + j)), both],
            out_specs=both),
        out_shape=_sds(fold.shape, fold.dtype), compiler_params=_params("parallel", "parallel"),
    )(nbr_chips, pair, pair, fold)


def _adamw(w, g, m, v, name):
    r, c = w.shape
    tr, tcol = _tile(r, 512, SUBLANES), _tile(c, 1024, LANES)
    blk = pl.BlockSpec((tr, tcol), lambda i, j: (i, j))

    def body(w_ref, g_ref, m_ref, v_ref, d_ref, nm_ref, nv_ref, g_out_ref):
        g_ = g_ref[...]
        g_out_ref[...] = g_
        m_ = ADAM_B1 * m_ref[...] + (1.0 - ADAM_B1) * g_
        v_ = ADAM_B2 * v_ref[...] + (1.0 - ADAM_B2) * (g_ * g_)
        m_hat = m_ / (1.0 - ADAM_B1 ** ADAM_STEP)
        v_hat = v_ / (1.0 - ADAM_B2 ** ADAM_STEP)
        d_ref[...] = -ADAM_LR * (m_hat / (jnp.sqrt(v_hat) + ADAM_EPS) + ADAM_WD * w_ref[...])
        nm_ref[...] = m_
        nv_ref[...] = v_

    return _call(body, name=name, grid=(r // tr, c // tcol), in_specs=[blk] * 4, out_specs=[blk] * 4,
                 out_shape=[_sds((r, c))] * 4, sem=("parallel", "parallel"))(w, g, m, v)


def _place():
    x, y, c = lax.axis_index("x"), lax.axis_index("y"), lax.axis_index("c")
    chips = [(1 - x, y), (x, 1 - y), (1 - x, 1 - y)]
    return x, y, c, chips


def _all_gather8(blk, name):
    m, n = blk.shape

    def body(x_ref, out_ref, send_sems, recv_sems, local_sem):
        x, y, c, chips = _place()
        me, sibling = (x, y, c), (x, y, 1 - c)

        def rows(px, py, pc):
            return out_ref.at[pl.ds((4 * px + 2 * py + pc) * m, m), :]

        def copy(k, block, to, src=None):
            return pltpu.make_async_remote_copy(
                src_ref=rows(*block) if src is None else src, dst_ref=rows(*block), send_sem=send_sems.at[k],
                recv_sem=recv_sems.at[k], device_id=to, device_id_type=MESH_ID)

        mine = pltpu.make_async_copy(x_ref, rows(*me), local_sem)
        mine.start()
        first = [copy(0, me, sibling, src=x_ref)]
        first += [copy(1 + j, me, (*chip, c), src=x_ref) for j, chip in enumerate(chips)]
        for cp in first:
            cp.start()
        passed = [copy(4 + j, (*chip, c), sibling) for j, chip in enumerate(chips)]
        for j, chip in enumerate(chips):
            copy(1 + j, (*chip, c), me).wait_recv()
            passed[j].start()
        copy(0, sibling, me).wait_recv()
        for j, chip in enumerate(chips):
            copy(4 + j, (*chip, 1 - c), me).wait_recv()
        for cp in first + passed:
            cp.wait_send()
        mine.wait()

    return pl.pallas_call(
        body, name=name, out_shape=_sds((N_DEV * m, n), blk.dtype), in_specs=[ANY], out_specs=ANY,
        scratch_shapes=[pltpu.SemaphoreType.DMA((7,)), pltpu.SemaphoreType.DMA((7,)), pltpu.SemaphoreType.DMA],
    )(blk)


def _half(ref, core, split_rows):
    r, c = ref.shape
    if split_rows:
        return ref.at[pl.ds(core * (r // 2), r // 2), :]
    return ref.at[:, pl.ds(core * (c // 2), c // 2)]


def _chip_block(ref, j, col_sharded):
    r, c = ref.shape
    if col_sharded:
        return ref.at[:, pl.ds(j * (c // N_CHIPS), c // N_CHIPS)]
    return ref.at[pl.ds(j * (r // N_CHIPS), r // N_CHIPS), :]


def _rows_part(ref, part):
    lo, hi, n = part
    r = ref.shape[0]
    return ref if (lo, hi) == (0, n) else ref.at[pl.ds(lo * (r // n), (hi - lo) * (r // n)), :]


def _copy(send_sems, recv_sems, k, src, dst, to):
    return pltpu.make_async_remote_copy(src_ref=src, dst_ref=dst, send_sem=send_sems.at[k], recv_sem=recv_sems.at[k],
                                        device_id=to, device_id_type=MESH_ID)


def _in_place(arrays):
    return tuple(arrays), tuple(_sds(a.shape, a.dtype) for a in arrays), {i: i for i in range(len(arrays))}


def _gather_ici(fulls, col_sharded, part=(0, 1, 1)):
    nw = len(fulls)

    def build(_, refs, send_sems, recv_sems, sem0):
        x, y, c, chips = _place()
        sends, recvs = [], []
        for w in range(nw):
            win = lambda j: _rows_part(_half(_chip_block(refs[w], j, col_sharded[w]), c, True), part)
            for k, (cx, cy) in enumerate(chips):
                sem = sem0 + 3 * w + k
                sends.append(_copy(send_sems, recv_sems, sem, win(2 * x + y), win(2 * x + y), (cx, cy, c)))
                recvs.append(_copy(send_sems, recv_sems, sem, win(2 * cx + cy), win(2 * cx + cy), (cx, cy, c)))
        return sends, recvs

    return _Side(*_in_place(fulls), 3 * nw, build)


def _gather_d2d(fulls, col_sharded):
    nw = len(fulls)

    def build(_, refs, send_sems, recv_sems, sem0):
        x, y, c, chips = _place()
        sends, recvs = [], []
        for w in range(nw):
            win = lambda j, core: _half(_chip_block(refs[w], j, col_sharded[w]), core, True)
            for k, (cx, cy) in enumerate(chips):
                sem = sem0 + 3 * w + k
                sends.append(_copy(send_sems, recv_sems, sem, win(2 * cx + cy, c), win(2 * cx + cy, c), (x, y, 1 - c)))
                recvs.append(_copy(send_sems, recv_sems, sem, win(2 * cx + cy, 1 - c), win(2 * cx + cy, 1 - c),
                                   (x, y, 1 - c)))
        return sends, recvs

    return _Side(*_in_place(fulls), 3 * nw, build)


def _gather_neighbours(fulls, col_sharded, part=(0, 1, 1)):
    nw = len(fulls)

    def build(_, refs, send_sems, recv_sems, sem0):
        x, y, c, chips = _place()
        sends, recvs = [], []
        for w in range(nw):
            win = lambda j: _rows_part(_half(_chip_block(refs[w], j, col_sharded[w]), c, True), part)
            for k, (cx, cy) in enumerate(chips[:2]):
                sem = sem0 + 2 * w + k
                sends.append(_copy(send_sems, recv_sems, sem, win(2 * x + y), win(2 * x + y), (cx, cy, c)))
                recvs.append(_copy(send_sems, recv_sems, sem, win(2 * cx + cy), win(2 * cx + cy), (cx, cy, c)))
        return sends, recvs

    return _Side(*_in_place(fulls), 2 * nw, build)


def _gather_forward(fulls, col_sharded, ways=(True, True), siblings=False):
    nw = len(fulls)

    def build(_, refs, send_sems, recv_sems, sem0):
        x, y, c, (cx_, cy_, cd_) = _place()
        sends, recvs = [], []
        for w in range(nw):
            win = lambda chip, core: _half(_chip_block(refs[w], 2 * chip[0] + chip[1], col_sharded[w]), core, True)
            part = lambda ref, p: _rows_part(ref, (p, p + 1, 2))
            for p, (src, to) in enumerate(((cx_, cy_), (cy_, cx_))):
                if ways[p]:
                    sem = sem0 + 4 * w + p
                    sends.append(_copy(send_sems, recv_sems, sem, part(win(src, c), p), part(win(src, c), p), (*to, c)))
                    recvs.append(_copy(send_sems, recv_sems, sem, part(win(cd_, c), p), part(win(cd_, c), p), (*to, c)))
            if siblings:
                for k, chip in enumerate((cx_, cy_)):
                    sem = sem0 + 4 * w + 2 + k
                    sends.append(_copy(send_sems, recv_sems, sem, win(chip, c), win(chip, c), (x, y, 1 - c)))
                    recvs.append(_copy(send_sems, recv_sems, sem, win(chip, 1 - c), win(chip, 1 - c), (x, y, 1 - c)))
        return sends, recvs

    return _Side(*_in_place(fulls), 4 * nw, build)


def _gather_diagonal_d2d(full, col_sharded):
    def build(_, refs, send_sems, recv_sems, sem0):
        x, y, c, chips = _place()
        cx, cy = chips[2]
        win = lambda core: _half(_chip_block(refs[0], 2 * cx + cy, col_sharded), core, True)
        return ([_copy(send_sems, recv_sems, sem0, win(c), win(c), (x, y, 1 - c))],
                [_copy(send_sems, recv_sems, sem0, win(1 - c), win(1 - c), (x, y, 1 - c))])

    return _Side(*_in_place([full]), 1, build)


def _scatter_forward_1(pair):
    r, n = pair.shape
    results = (_sds((2, r, n // N_CHIPS), pair.dtype), _sds((2, r // 2, n // N_CHIPS), pair.dtype))

    def build(refs, res, send_sems, recv_sems, sem0):
        x, y, c, (cx_, cy_, cd_) = _place()
        region = lambda chip, p: _rows_part(_chip_block(refs[0], 2 * chip[0] + chip[1], True), (p, p + 1, 2))
        got, fold = res
        copies = [
            _copy(send_sems, recv_sems, sem0, region(cx_, 1), _rows_part(got.at[0], (1, 2, 2)), (*cx_, c)),
            _copy(send_sems, recv_sems, sem0 + 1, region(cd_, 1), fold.at[0], (*cx_, c)),
            _copy(send_sems, recv_sems, sem0 + 2, region(cy_, 0), _rows_part(got.at[1], (0, 1, 2)), (*cy_, c)),
            _copy(send_sems, recv_sems, sem0 + 3, region(cd_, 0), fold.at[1], (*cy_, c))]
        return copies, copies

    return _Side((pair,), results, {}, 4, build)


def _scatter_forward_2(passed, got):
    def build(refs, res, send_sems, recv_sems, sem0):
        x, y, c, (cx_, cy_, _) = _place()
        copies = [_copy(send_sems, recv_sems, sem0, refs[0].at[0], _rows_part(res[0].at[0], (0, 1, 2)), (*cx_, c)),
                  _copy(send_sems, recv_sems, sem0 + 1, refs[0].at[1], _rows_part(res[0].at[1], (1, 2, 2)), (*cy_, c))]
        return copies, copies

    return _Side((passed, got), (_sds(got.shape, got.dtype),), {1: 0}, 2, build)


def _exchange(side, name):
    return _call(None, side=side, name=name)()[1]


def _swap_halves(grads, col_sharded):
    nw = len(grads)
    out_shapes = [_sds((g.shape[0] // 2, g.shape[1]) if col else (g.shape[0], g.shape[1] // 2), g.dtype)
                  for g, col in zip(grads, col_sharded)]

    def build(g_refs, o_refs, send_sems, recv_sems, sem0):
        x, y, c, _ = _place()
        copies = [_copy(send_sems, recv_sems, sem0 + w, _half(g_refs[w], 1 - c, col_sharded[w]), o_refs[w],
                        (x, y, 1 - c)) for w in range(nw)]
        return copies, copies

    return _Side(tuple(grads), tuple(out_shapes), {}, nw, build)


def _scatter_regions(pairs, col_sharded, part=(0, 1, 1), into=None):
    nw = len(pairs)

    def region_shape(p, col):
        return (p.shape[0], p.shape[1] // N_CHIPS) if col else (p.shape[0] // N_CHIPS, p.shape[1])

    out_shapes = tuple(_sds((N_CHIPS - 1, *region_shape(p, col)), p.dtype) for p, col in zip(pairs, col_sharded))

    def build(refs, o_refs, send_sems, recv_sems, sem0):
        x, y, c, chips = _place()
        copies = []
        for w in range(nw):
            for k, (cx, cy) in enumerate(chips):
                copies.append(_copy(
                    send_sems, recv_sems, sem0 + 3 * w + k,
                    _rows_part(_chip_block(refs[w], 2 * cx + cy, col_sharded[w]), part),
                    _rows_part(o_refs[w].at[k], part), (cx, cy, c)))
        return copies, copies

    if into is None:
        return _Side(tuple(pairs), out_shapes, {}, 3 * nw, build)
    return _Side((*pairs, *into), out_shapes, {nw + w: w for w in range(nw)}, 3 * nw, build)


def _join_halves(halves, col_sharded):
    nw = len(halves)

    def build(_, refs, send_sems, recv_sems, sem0):
        x, y, c, _ = _place()
        sends, recvs = [], []
        for w in range(nw):
            mine, theirs = _half(refs[w], c, col_sharded[w]), _half(refs[w], 1 - c, col_sharded[w])
            sends.append(_copy(send_sems, recv_sems, sem0 + w, mine, mine, (x, y, 1 - c)))
            recvs.append(_copy(send_sems, recv_sems, sem0 + w, theirs, theirs, (x, y, 1 - c)))
        return sends, recvs

    return _Side(*_in_place(halves), nw, build)


def _part_rows(size):
    return -(-size // (SUBLANES * LANES)) * SUBLANES


def _pack(arrays, pad_rows_to=SUBLANES):
    flat = [jnp.pad(a.reshape(-1), (0, _part_rows(a.size) * LANES - a.size)).reshape(-1, LANES) for a in arrays]
    rows = sum(f.shape[0] for f in flat)
    pad = (-rows) % pad_rows_to
    if pad:
        flat.append(jnp.zeros((pad, LANES), F32))
    return jnp.concatenate(flat, axis=0)


def _unpack(packed, shapes):
    out, r = [], 0
    for shp in shapes:
        size = math.prod(shp)
        out.append(packed[r:r + _part_rows(size)].reshape(-1)[:size].reshape(shp))
        r += _part_rows(size)
    return out


def _rope_tables(n_ctx_rows, s):
    rows = s // GRID_W
    row_idx = jnp.repeat(jnp.arange(rows), GRID_W)
    col_idx = jnp.tile(jnp.arange(GRID_W), rows)
    n_freq = LANES // 4
    inv_freq = ROPE_THETA ** (-jnp.arange(n_freq, dtype=F32) / n_freq)
    ang = jnp.concatenate([row_idx.astype(F32)[:, None] * inv_freq, col_idx.astype(F32)[:, None] * inv_freq], axis=-1)
    cos = jnp.repeat(jnp.cos(ang), 2, axis=-1)
    sin = jnp.repeat(jnp.sin(ang), 2, axis=-1) * jnp.tile(jnp.array([-1.0, 1.0], F32), LANES // 2)
    cos = jnp.concatenate([jnp.ones((n_ctx_rows, LANES), F32), cos], axis=0)
    sin = jnp.concatenate([jnp.zeros((n_ctx_rows, LANES), F32), sin], axis=0)
    return cos, sin


WEIGHT_NAMES = ['c_ctx', 'w_mod', 'b_mod', 'g_mix', 'g_mlp', 'w_in', 'q_gain', 'k_gain', 'conv_w', 'conv_b', 'w_rg',
                'b_rg', 'w_ig', 'b_ig', 'lru_lambda', 'w_o_attn', 'w_o_rnn', 'w_out', 'w_up', 'w_down', 'g_final']
BIG = ['w_in', 'w_o_attn', 'w_o_rnn', 'w_out', 'w_up', 'w_down']
BIG_COL_SHARDED = [True, False, False, False, True, False]
GATES = ['w_rg', 'w_ig']
SMALL = ['c_ctx', 'b_mod', 'g_mix', 'g_mlp', 'q_gain', 'k_gain', 'conv_b', 'g_final',
         'conv_w', 'b_rg', 'b_ig', 'lru_lambda']


def kernel(x, c, ctx, c_ctx, w_mod, b_mod, g_mix, g_mlp, w_in, q_gain, k_gain, conv_w, conv_b, w_rg, b_rg, w_ig, b_ig, lru_lambda, w_o_attn, w_o_rnn, w_out, w_up, w_down, g_final, loss_target, m_c_ctx, m_w_mod, m_b_mod, m_g_mix, m_g_mlp, m_w_in, m_q_gain, m_k_gain, m_conv_w, m_conv_b, m_w_rg, m_b_rg, m_w_ig, m_b_ig, m_lru_lambda, m_w_o_attn, m_w_o_rnn, m_w_out, m_w_up, m_w_down, m_g_final, v_c_ctx, v_w_mod, v_b_mod, v_g_mix, v_g_mlp, v_w_in, v_q_gain, v_k_gain, v_conv_w, v_conv_b, v_w_rg, v_b_rg, v_w_ig, v_b_ig, v_lru_lambda, v_w_o_attn, v_w_o_rnn, v_w_out, v_w_up, v_w_down, v_g_final):
    given = dict(locals())
    weights = {n: given[n] for n in WEIGHT_NAMES}
    moms = {n: given["m_" + n] for n in WEIGHT_NAMES}
    vars_ = {n: given["v_" + n] for n in WEIGHT_NAMES}

    s, d = x.shape[1], x.shape[2]
    n_ctx = ctx.shape[1]
    t = n_ctx + s
    hd = q_gain.shape[1]
    assert hd == LANES and w_rg.shape[-1] == LANES
    attn_w = w_o_attn.shape[1] * N_CHIPS
    n_in = w_in.shape[2] * N_CHIPS
    kv_w = (n_in - attn_w - 4 * d) // 2
    group = attn_w // kv_w
    k_off, v_off, xr_off = attn_w, attn_w + kv_w, attn_w + 2 * kv_w
    xg_off, gl_off = xr_off + d, xr_off + 2 * d
    d_mod = N_MOD * d
    tr = _tile(math.gcd(n_ctx, s), 256, 16)
    tcol = _tile(math.gcd(d, xr_off), 1024, LANES)
    xi, yi, ci = lax.axis_index("x"), lax.axis_index("y"), lax.axis_index("c")
    chip = 2 * xi + yi
    core = ci.astype(jnp.int32).reshape(1)

    sharded_small = [conv_w[0], b_rg[0], b_ig[0], lru_lambda[0]]
    pack0 = _pack([c[0]] + sharded_small)
    got0 = _all_gather8(pack0, "gather_small_inputs").reshape(N_DEV, -1, LANES)
    c_all = got0[:, :_part_rows(d)].reshape(N_DEV, -1)[:, :d]
    per_chip = [_unpack(got0[2 * j, _part_rows(d):], [a.shape for a in sharded_small]) for j in range(N_CHIPS)]
    conv_w_f, b_rg_f, b_ig_f, lam_f = (jnp.concatenate([per_chip[j][i] for j in range(N_CHIPS)], axis=-1)
                                       for i in range(4))
    c16 = jnp.concatenate([c_all, c_ctx[None, :], jnp.zeros((16 - N_DEV - 1, d), F32)], axis=0)
    b_mod_shard = lax.dynamic_slice(b_mod, (0, chip * (d_mod // N_CHIPS)), (1, d_mod // N_CHIPS))
    mod_part, silu16 = _mod_fwd(c16, w_mod[0], b_mod_shard)
    mod_all = _all_gather8(mod_part, "gather_mod").reshape(N_DEV, 16, d_mod // N_CHIPS)
    mod16 = jnp.concatenate([mod_all[2 * j] for j in range(N_CHIPS)], axis=-1)
    me = 4 * xi + 2 * yi + ci
    mod_lat = lax.dynamic_slice(mod16, (me, 0), (1, d_mod)).reshape(N_MOD, d)
    mod_ctx = mod16[N_DEV].reshape(N_MOD, d)
    mod4 = jnp.stack([mod_ctx[0], mod_ctx[1], mod_lat[0], mod_lat[1]])
    mod3 = jnp.stack([mod_lat[2], mod_lat[3], mod_lat[4]])
    gate_f = mod_lat[5][None, :]

    chip_arr = chip.astype(jnp.int32).reshape(1)
    own = {n: _cast_into_window(weights[n][0], chip_arr, col, "cast_" + n) for n, col in zip(BIG, BIG_COL_SHARDED)}
    place = jnp.stack([chip, ci]).astype(jnp.int32)
    row3 = [False] * 3

    def pair_sum(n, full, other, col):
        return _add_half(full, other, core, col, "pair_sum_" + n)

    def chip_sum(n, pair, got, col):
        return _sum_regions(pair, got, place, col, "chip_sum_" + n)

    (w_in_f,) = _exchange(_gather_neighbours([own['w_in']], [True]), "gather_w_in_nbr")
    (w_in_f,) = _exchange(_gather_forward([w_in_f], [True], siblings=True), "gather_w_in_fwd")
    (w_in_f,) = _exchange(_gather_diagonal_d2d(w_in_f, True), "gather_w_in_diag")
    xcat = jnp.concatenate([ctx[0], x[0]], axis=0)
    cos, sin = _rope_tables(n_ctx, s)
    h = _norm_mod_fwd(xcat, g_mix, mod4, n_ctx, tr)
    z, (w_oa_f, w_or_f, w_out_f, w_up_f) = _matmul(h, w_in_f, name="mm_in", side=_sides(
        _gather_neighbours([own['w_o_attn'], own['w_o_rnn'], own['w_out']], row3),
        _gather_neighbours([own['w_up']], [True], (0, 1, 2))))
    qr, (w_oa_f, w_or_f, w_out_f) = _head_prep_fwd(z, 0, attn_w // LANES, q_gain, cos, sin, tr, "q_prep",
                                                   side=_gather_forward([w_oa_f, w_or_f, w_out_f], row3))
    kr = _head_prep_fwd(z, k_off, kv_w // LANES, k_gain, cos, sin, tr, "k_prep")
    (attn_o, lse), (w_up_f, w_down_f, w_oa_f, w_or_f, w_out_f) = _attn_fwd(
        qr, kr, z, v_off, n_ctx, group, tr, side=_sides(
            _on_same(_gather_neighbours([w_up_f], [True], (1, 2, 2)), _gather_forward([w_up_f], [True], ways=(True, False))),
            _gather_neighbours([own['w_down']], [False]), _gather_d2d([w_oa_f, w_or_f, w_out_f], row3)))
    (xc, a_f, bx_f, a_b, bx_b), (w_up_f, w_down_f) = _rnn_prep(
        z, xr_off, conv_w_f, conv_b, w_rg[0], b_rg_f, w_ig[0], b_ig_f, lam_f, n_ctx,
        side=_sides(_gather_forward([w_up_f], [True], ways=(False, True)), _gather_forward([w_down_f], [False])))
    h_f, (w_up_f,) = _scan(a_f, bx_f, order="ctx_lat_up", post=False, n_ctx_rows=n_ctx, name="scan_f",
                           side=_gather_d2d([w_up_f], [True]))
    h_b = _scan(a_b, bx_b, order="ctx_lat_down", post=False, n_ctx_rows=n_ctx, name="scan_b")
    u = _rnn_gate_fwd(h_f, h_b, z, xg_off, n_ctx, tr, tcol)
    y_attn = _matmul(attn_o, w_oa_f, name="mm_o_attn")
    y_rnn = _matmul(u, w_or_f, name="mm_o_rnn")
    mrg = _merge_fwd(y_attn, y_rnn, z, gl_off, n_ctx, tr, tcol)
    mix = _matmul(mrg, w_out_f, name="mm_out")
    x1, h2 = _resid_norm_mod_fwd(x[0], mix, g_mlp, mod3, tr)
    up, (w_down_f,) = _matmul(h2, w_up_f, name="mm_up", side=_gather_d2d([w_down_f], [False]))
    act = _sq_relu(up, tr, tcol)
    down = _matmul(act, w_down_f, name="mm_down")
    dx2, d_down, sums_fin, loss_blk = _final_fwd_bwd(x1, down, loss_target[0], g_final[None, :], gate_f, tr)

    d_up = _matmul(d_down, w_down_f, tb=True, out_dtype=BF16, name="mm_d_up",
                   post=lambda d_act, up_: d_act * 2.0 * jnp.maximum(up_, 0.0), post_args=(up,))
    g_w_down = _matmul(act, d_down, ta=True, out_dtype=BF16, name="mm_g_down")

    def scatter(pairs, cols, lo, hi, into=None):
        return _scatter_regions(pairs, cols, (lo, hi, 8), into)

    dh2, (got,) = _matmul(d_up, w_up_f, tb=True, name="mm_d_h2", side=_swap_halves([g_w_down], [False]))
    p_down = pair_sum('w_down', g_w_down, got, False)
    g_w_up, got_down = _matmul(h2, d_up, ta=True, out_dtype=BF16, name="mm_g_up", side=scatter([p_down], [False], 0, 4))
    (dx1, d_mix, sums2), (got,) = _norm_mod_bwd2(x1, dh2, dx2, mix, g_mlp, mod3, tr, side=_swap_halves([g_w_up], [True]))
    p_up = pair_sum('w_up', g_w_up, got, True)
    d_mrg = _matmul(d_mix, w_out_f, tb=True, name="mm_d_mrg")
    g_w_out = _matmul(mrg, d_mix, ta=True, out_dtype=BF16, name="mm_g_out")
    dz = _dz_start(t, n_in, n_ctx)
    d_ya, dz = _merge_bwd(d_mrg, y_attn, z, gl_off, dz, n_ctx, tr, tcol, "merge_bwd_attn")
    d_yr, dz = _merge_bwd(d_mrg, y_rnn, z, gl_off + d, dz, n_ctx, tr, tcol, "merge_bwd_rnn")
    d_o = _matmul(d_ya, w_oa_f, tb=True, out_dtype=BF16, name="mm_d_o")
    g_w_oa = _matmul(attn_o, d_ya, ta=True, out_dtype=BF16, name="mm_g_o_attn")
    d_u = _matmul(d_yr, w_or_f, tb=True, name="mm_d_u")
    g_w_or = _matmul(u, d_yr, ta=True, out_dtype=BF16, name="mm_g_o_rnn")
    d_rnn, dz = _rnn_gate_bwd(d_u, h_f, h_b, z, xg_off, dz, n_ctx, tr, tcol)
    gs_f = _scan(a_f, d_rnn, order="lat_ctx_down", post=True, n_ctx_rows=n_ctx, name="scan_f_bwd")
    gs_b = _scan(a_b, d_rnn, order="lat_ctx_up", post=True, n_ctx_rows=n_ctx, name="scan_b_bwd")
    o_names, o_grads = ['w_o_attn', 'w_o_rnn', 'w_out'], [g_w_oa, g_w_or, g_w_out]
    (dz, g_w_rg, g_w_ig, sums_rnn), (got_down, *got_o) = _rnn_bwd(
        z, xr_off, xc, gs_f, gs_b, h_f, h_b, conv_w_f, w_rg[0], b_rg_f, w_ig[0], b_ig_f, lam_f, dz, n_ctx,
        side=_sides(scatter([p_down], [False], 4, 8, got_down), _swap_halves(o_grads, row3)))
    hs_down = chip_sum('w_down', p_down, got_down, False)
    p_o = [pair_sum(n, g, o, False) for n, g, o in zip(o_names, o_grads, got_o)]
    gate_cols = 8 * LANES if g_w_rg.size % (8 * LANES * N_CHIPS * 16) == 0 else 2 * LANES
    gate_rows = g_w_rg.size // gate_cols
    gate_grads = [g_w_rg.reshape(gate_rows, gate_cols), g_w_ig.reshape(gate_rows, gate_cols)]
    (dq, dk, dz), (got_up, got_oa, got_rg, got_ig, gs_down) = _attn_bwd(
        qr, kr, z, v_off, d_o, attn_o, lse, dz, n_ctx, group, tr, side=_sides(
            _scatter_regions([p_up], [True]), _scatter_regions(p_o[:1], row3[:1]), _swap_halves(gate_grads, row3[:2]),
            _join_halves([hs_down], [False])))
    hs_up = chip_sum('w_up', p_up, got_up, True)
    hs_oa = chip_sum('w_o_attn', p_o[0], got_oa, False)
    p_gate = [pair_sum(n, g, o, False) for n, g, o in zip(GATES, gate_grads, (got_rg, got_ig))]
    dz, g_q_gain = _head_prep_bwd(z, 0, attn_w // LANES, q_gain, cos, sin, dq, n_ctx, tr, dz, "q_prep_bwd")
    dz, g_k_gain = _head_prep_bwd(z, k_off, kv_w // LANES, k_gain, cos, sin, dk, 0, tr, dz, "k_prep_bwd")
    g_w_in, (got_or, got_out, got_rg, got_ig, gs_up, gs_oa) = _matmul(
        h, dz, ta=True, out_dtype=BF16, name="mm_g_in", side=_sides(
            _scatter_regions(p_o[1:] + p_gate, [False] * 4), _join_halves([hs_up], [True]), _join_halves([hs_oa], [False])))
    hs_late = [chip_sum(n, p, o, False)
               for n, p, o in zip(o_names[1:] + GATES, p_o[1:] + p_gate, (got_or, got_out, got_rg, got_ig))]
    (got,) = _exchange(_swap_halves([g_w_in], [True]), "swap_w_in")
    p_in = pair_sum('w_in', g_w_in, got, True)
    dh, (got_in, fold_in) = _matmul(dz, w_in_f, tb=True, name="mm_d_h", side=_scatter_forward_1(p_in))
    nbr_chips = jnp.stack([2 * (1 - xi) + yi, 2 * xi + 1 - yi]).astype(jnp.int32)
    passed = _fold_forwarded(p_in, fold_in, nbr_chips, "fold_w_in")
    (got_in,) = _exchange(_scatter_forward_2(passed, got_in), "scatter_w_in_fwd")
    grad_x, sums1 = _norm_mod_bwd1(xcat, dh, dx1, g_mix, mod4, n_ctx, tr)

    zeros_d = jnp.zeros((d,), F32)
    dmod_lat = jnp.concatenate([sums1[0], sums1[1], sums2[3], sums2[0], sums2[1], sums_fin[1]])
    dmod_ctx = jnp.concatenate([sums1[3], sums1[4]] + [zeros_d] * 4)
    small_parts = [dmod_lat, dmod_ctx, loss_blk[0, 0:1], sums1[2] + sums1[5], sums2[2], g_q_gain[0], g_k_gain[0],
                   sums_rnn[10], sums_fin[0], sums_rnn[6:10], sums_rnn[0:2], sums_rnn[2:4], sums_rnn[4:6]]
    pack1 = _pack(small_parts)
    got1 = _all_gather8(pack1, "gather_small_grads").reshape(N_DEV, -1, LANES)
    tot1 = _sum_leading(got1, "sum_small_grads")
    part_shapes = [a.shape for a in small_parts]
    (s_dmod_lat, s_dmod_ctx, s_loss, g_g_mix, g_g_mlp, g_q_gain, g_k_gain, g_conv_b, g_g_final,
     g_conv_w_f, g_b_rg_f, g_b_ig_f, g_lam_f) = _unpack(tot1, part_shapes)
    loss = s_loss[0]
    g_b_mod = (s_dmod_lat + s_dmod_ctx)[None, :]
    n_mod_rows = _part_rows(d_mod)
    dmod16 = jnp.concatenate([got1[:, :n_mod_rows].reshape(N_DEV, -1)[:, :d_mod], s_dmod_ctx[None, :],
                              jnp.zeros((16 - N_DEV - 1, d_mod), F32)], axis=0)
    dmod16_shard = lax.dynamic_slice(dmod16, (0, chip * (d_mod // N_CHIPS)), (16, d_mod // N_CHIPS))
    g_w_mod = _matmul(silu16, dmod16_shard, ta=True, name="mm_g_mod")
    dsilu_part = _matmul(dmod16_shard[N_DEV:], w_mod[0], tb=True, name="mm_d_silu")
    dsilu_all = _all_gather8(dsilu_part, "gather_d_silu").reshape(N_DEV, 8, d)
    g_c_ctx = _c_ctx_grad(dsilu_all, c_ctx[None, :])[0]

    def shard_of(full):
        w = full.shape[-1] // N_CHIPS
        return lax.dynamic_slice(full, (0, chip * w), (full.shape[0], w))

    grads = {
        'c_ctx': g_c_ctx, 'b_mod': g_b_mod, 'g_mix': g_g_mix[None, :], 'g_mlp': g_g_mlp[None, :],
        'q_gain': g_q_gain[None, :], 'k_gain': g_k_gain[None, :], 'conv_b': g_conv_b[None, :],
        'g_final': g_g_final,
        'conv_w': shard_of(g_conv_w_f)[None], 'b_rg': shard_of(g_b_rg_f)[None], 'b_ig': shard_of(g_b_ig_f)[None],
        'lru_lambda': shard_of(g_lam_f)[None], 'w_mod': g_w_mod[None],
    }

    delta, new_m, new_v = {}, {}, {}

    def adamw(n):
        shp = weights[n].shape
        as2d = (lambda a: a[0]) if n not in GATES else (lambda a: a.reshape(-1, LANES))
        dl, nm, nv, g = _adamw(as2d(weights[n]), as2d(grads[n]), as2d(moms[n]), as2d(vars_[n]), "adamw_" + n)
        delta[n], new_m[n], new_v[n], grads[n] = dl.reshape(shp), nm.reshape(shp), nv.reshape(shp), g.reshape(shp)

    hs_in = chip_sum('w_in', p_in, got_in, True)
    gs_in, gs_or, gs_out = _exchange(_join_halves([hs_in] + hs_late[:2], [True, False, False]), "join_rest")
    for n, g in zip(['w_in', 'w_o_attn', 'w_o_rnn', 'w_out', 'w_up', 'w_down'], [gs_in, gs_oa, gs_or, gs_out, gs_up, gs_down]):
        grads[n] = g[None]
    half_cols = gate_cols // 2
    mine = jnp.concatenate([lax.dynamic_slice(hs, (0, ci * half_cols), (gate_rows // N_CHIPS, half_cols))
                            for hs in hs_late[2:]], axis=0)
    gate_all = _all_gather8(mine, "gather_gate_grads")
    gate_all = gate_all.reshape(N_CHIPS, 2, len(GATES), gate_rows // N_CHIPS, half_cols)
    for i, n in enumerate(GATES):
        grads[n] = jnp.moveaxis(gate_all[:, :, i], 1, 2).reshape(weights[n].shape)
    for n in ['w_mod'] + BIG + GATES:
        adamw(n)
    small_shapes = [weights[n].shape for n in SMALL]
    packed = [_pack([src[n] for n in SMALL], 512) for src in (weights, grads, moms, vars_)]
    outs = _adamw(*packed, "adamw_small")
    for res, out in zip((delta, new_m, new_v), outs):
        for n, a in zip(SMALL, _unpack(out, small_shapes)):
            res[n] = a
    return (loss, grad_x[None], *[grads[n] for n in WEIGHT_NAMES], *[delta[n] for n in WEIGHT_NAMES],
            *[new_m[n] for n in WEIGHT_NAMES], *[new_v[n] for n in WEIGHT_NAMES])
```

```python
import functools
import math
from typing import Callable, NamedTuple

import jax
import jax.numpy as jnp
from jax import lax
from jax.experimental import pallas as pl
from jax.experimental.pallas import tpu as pltpu

F32 = jnp.float32
BF16 = jnp.bfloat16
MESH_ID = pl.DeviceIdType.MESH
ANY = pl.BlockSpec(memory_space=pl.ANY)

NORM_EPS = 1e-6
LRU_C = 8.0
GRID_W = 64
ROPE_THETA = 10000.0
N_MOD = 6
CONV_WIDTH = 4
ADAM_LR = 0.001
ADAM_B1 = 0.9
ADAM_B2 = 0.999
ADAM_EPS = 1e-08
ADAM_WD = 0.01
ADAM_STEP = 10

LANES = 128
SUBLANES = 8
V7X_VMEM_LIMIT = 48 * 1024 * 1024
N_CHIPS = 4
N_DEV = 8
GELU_C = math.sqrt(2.0 / math.pi)
GELU_A = 0.044715


def _tile(dim, pref, align):
    t = min(pref, dim)
    t -= t % align
    while t >= align:
        if dim % t == 0:
            return t
        t -= align
    return dim


def _params(*sem):
    return pltpu.CompilerParams(dimension_semantics=sem, vmem_limit_bytes=V7X_VMEM_LIMIT)


def _sds(shape, dtype=F32):
    return jax.ShapeDtypeStruct(shape, dtype)


class _Side(NamedTuple):
    operands: tuple
    results: tuple
    aliases: dict
    n_sems: int
    build: Callable


def _sides(*sides):
    ops, res, aliases, spans, n = [], [], {}, [], 0
    for s in sides:
        spans.append((len(ops), len(res), n))
        aliases.update({len(ops) + i: len(res) + j for i, j in s.aliases.items()})
        ops += s.operands
        res += s.results
        n += s.n_sems

    def build(op_refs, res_refs, send_sems, recv_sems, sem0):
        sends, recvs = [], []
        for s, (o, r, k) in zip(sides, spans):
            a, b = s.build(op_refs[o:o + len(s.operands)], res_refs[r:r + len(s.results)], send_sems, recv_sems,
                           sem0 + k)
            sends += a
            recvs += b
        return sends, recvs

    return _Side(tuple(ops), tuple(res), aliases, n, build)


def _on_same(*sides):
    def build(ops, res, send_sems, recv_sems, sem0):
        sends, recvs = [], []
        for s in sides:
            a, b = s.build(ops, res, send_sems, recv_sems, sem0)
            sends += a
            recvs += b
            sem0 += s.n_sems
        return sends, recvs

    return _Side(sides[0].operands, sides[0].results, sides[0].aliases, sum(s.n_sems for s in sides), build)


def _call(body, *, side=None, sem=(), grid=(), in_specs=(), out_specs=(), out_shape=(), scratch_shapes=(), **kw):
    if side is None:
        return pl.pallas_call(body, grid=grid, in_specs=list(in_specs), out_specs=out_specs, out_shape=out_shape,
                              scratch_shapes=list(scratch_shapes), compiler_params=_params(*sem), **kw)
    aliases = kw.pop("input_output_aliases", {})
    many = isinstance(out_shape, (list, tuple))
    out_specs_l, out_shape_l = (list(out_specs), list(out_shape)) if many else ([out_specs], [out_shape])
    n_in, n_out, n_scr = len(in_specs), len(out_shape_l), len(scratch_shapes)
    n_op, n_res = len(side.operands), len(side.results)

    def hosted(*refs):
        ins, ops = refs[:n_in], refs[n_in:n_in + n_op]
        outs = refs[n_in + n_op:n_in + n_op + n_out]
        res = refs[n_in + n_op + n_out:n_in + n_op + n_out + n_res]
        scr = refs[n_in + n_op + n_out + n_res:-2]
        send_sems, recv_sems = refs[-2:]

        def start():
            for cp in side.build(ops, res, send_sems, recv_sems, 0)[0]:
                cp.start()

        def finish():
            sends, recvs = side.build(ops, res, send_sems, recv_sems, 0)
            for cp in recvs:
                cp.wait_recv()
            for cp in sends:
                cp.wait_send()

        if not grid:
            start()
            finish()
            return
        ids = [pl.program_id(a) for a in range(len(grid))]
        first = functools.reduce(jnp.logical_and, [i == 0 for i in ids])
        last = functools.reduce(jnp.logical_and, [i == g - 1 for i, g in zip(ids, grid)])
        pl.when(first)(start)
        body(*ins, *outs, *scr)
        pl.when(last)(finish)

    def run(*args):
        got = pl.pallas_call(
            hosted, grid=grid, in_specs=[*in_specs, *[ANY] * n_op], out_specs=[*out_specs_l, *[ANY] * n_res],
            out_shape=[*out_shape_l, *side.results],
            scratch_shapes=[*scratch_shapes, pltpu.SemaphoreType.DMA((side.n_sems,)),
                            pltpu.SemaphoreType.DMA((side.n_sems,))],
            input_output_aliases={**aliases, **{n_in + i: n_out + j for i, j in side.aliases.items()}},
            compiler_params=_params(*["arbitrary"] * len(grid)), **kw)(*args, *side.operands)
        own = list(got[:n_out]) if many else got[0]
        return own, list(got[n_out:])

    return run


def _matmul(a, b, *, ta=False, tb=False, out_dtype=F32, name, tm=1024, tn=1024, tk=2816, side=None, post=None,
            post_args=()):
    k_dim, m = a.shape if ta else a.shape[::-1]
    n, k2 = b.shape if tb else b.shape[::-1]
    assert k_dim == k2, (a.shape, b.shape, ta, tb)
    tm = _tile(m, tm, LANES if ta else 16)
    tn = _tile(n, tn, 16 if tb else LANES)
    tk = _tile(k_dim, tk, LANES)
    nk = k_dim // tk
    dims = (((0 if ta else 1,), (1 if tb else 0,)), ((), ()))
    if nk == 1:
        def whole(a_ref, b_ref, *rest):
            acc = lax.dot_general(a_ref[...].astype(BF16), b_ref[...].astype(BF16), dims, preferred_element_type=F32)
            if post is not None:
                acc = post(acc, *[r[...] for r in rest[:-1]])
            rest[-1][...] = acc.astype(rest[-1].dtype)

        a_spec = pl.BlockSpec((tk, tm), lambda i, j: (0, i)) if ta else pl.BlockSpec((tm, tk), lambda i, j: (i, 0))
        b_spec = pl.BlockSpec((tn, tk), lambda i, j: (j, 0)) if tb else pl.BlockSpec((tk, tn), lambda i, j: (0, j))
        o_spec = pl.BlockSpec((tm, tn), lambda i, j: (i, j))
        return _call(
            whole, side=side, name=name, grid=(m // tm, n // tn), in_specs=[a_spec, b_spec] + [o_spec] * len(post_args),
            out_specs=o_spec, out_shape=_sds((m, n), out_dtype), sem=("parallel", "parallel"),
        )(a, b, *post_args)
    assert post is None

    def body(a_ref, b_ref, o_ref, acc_ref):
        k = pl.program_id(2)

        @pl.when(k == 0)
        def _():
            acc_ref[...] = jnp.zeros_like(acc_ref)

        acc_ref[...] += lax.dot_general(a_ref[...].astype(BF16), b_ref[...].astype(BF16), dims,
                                        preferred_element_type=F32)

        @pl.when(k == nk - 1)
        def _():
            o_ref[...] = acc_ref[...].astype(o_ref.dtype)

    a_spec = pl.BlockSpec((tk, tm), lambda i, j, k: (k, i)) if ta else pl.BlockSpec((tm, tk), lambda i, j, k: (i, k))
    b_spec = pl.BlockSpec((tn, tk), lambda i, j, k: (j, k)) if tb else pl.BlockSpec((tk, tn), lambda i, j, k: (k, j))
    return _call(
        body, side=side, name=name, grid=(m // tm, n // tn, nk), in_specs=[a_spec, b_spec],
        out_specs=pl.BlockSpec((tm, tn), lambda i, j, k: (i, j)), out_shape=_sds((m, n), out_dtype),
        scratch_shapes=[pltpu.VMEM((tm, tn), F32)], sem=("parallel", "parallel", "arbitrary"),
    )(a, b)


def _silu(x):
    return x * jax.nn.sigmoid(x)


def _gelu(x):
    return 0.5 * x * (1.0 + jnp.tanh(GELU_C * (x + GELU_A * x * x * x)))


def _gelu_grad(x):
    t = jnp.tanh(GELU_C * (x + GELU_A * x * x * x))
    return 0.5 * (1.0 + t) + 0.5 * x * (1.0 - t * t) * GELU_C * (1.0 + 3.0 * GELU_A * x * x)


def _expm1_nonpos(x):
    series = x * (1.0 + x * (1.0 / 2 + x * (1.0 / 6 + x * (1.0 / 24 + x * (1.0 / 120 + x * (1.0 / 720 + x / 5040))))))
    return jnp.where(x > -0.25, series, jnp.exp(x) - 1.0)


def _softplus(x):
    return jnp.maximum(x, 0.0) + jnp.log1p(jnp.exp(-jnp.abs(x)))


def _rms_stats(x):
    return lax.rsqrt(jnp.mean(x * x, axis=-1, keepdims=True) + NORM_EPS)


def _rms_bwd(dxhat, xhat, rstd):
    return rstd * (dxhat - xhat * jnp.mean(dxhat * xhat, axis=-1, keepdims=True))


def _colsum(v):
    return jnp.sum(v, axis=0, keepdims=True)


def _mod_fwd(c16, w_mod, b_mod_shard):
    r, d = c16.shape
    n = w_mod.shape[1]
    tn = _tile(n, 512, LANES)

    def body(c_ref, w_ref, b_ref, o_ref, s_ref):
        s = _silu(c_ref[...])
        s_ref[...] = s
        o_ref[...] = jnp.dot(s.astype(BF16), w_ref[...].astype(BF16), preferred_element_type=F32) + b_ref[...]

    return pl.pallas_call(
        body, name="mod_fwd", grid=(n // tn,),
        in_specs=[pl.BlockSpec((r, d), lambda j: (0, 0)), pl.BlockSpec((d, tn), lambda j: (0, j)),
                  pl.BlockSpec((1, tn), lambda j: (0, j))],
        out_specs=[pl.BlockSpec((r, tn), lambda j: (0, j)), pl.BlockSpec((r, d), lambda j: (0, 0))],
        out_shape=[_sds((r, n)), _sds((r, d))], compiler_params=_params("arbitrary"),
    )(c16, w_mod, b_mod_shard)


def _c_ctx_grad(parts, c_ctx_row):
    d = c_ctx_row.shape[1]

    def body(p_ref, c_ref, o_ref):
        tot = p_ref[0, 0:1, :]
        for chip in range(1, N_CHIPS):
            tot = tot + p_ref[2 * chip, 0:1, :]
        c = c_ref[...]
        sg = jax.nn.sigmoid(c)
        o_ref[...] = tot * (sg * (1.0 + c * (1.0 - sg)))

    return pl.pallas_call(body, name="c_ctx_grad", out_shape=_sds((1, d)), compiler_params=_params())(parts, c_ctx_row)


def _token_specs(n_ctx_rows, d, tr):
    nctx = n_ctx_rows // tr
    return (pl.BlockSpec((tr, d), lambda i: (jnp.minimum(i, nctx - 1), 0)),
            pl.BlockSpec((tr, d), lambda i: (jnp.maximum(i - nctx, 0), 0)))


def _norm_mod_fwd(ctx, x, g, mod4, tr):
    (n_ctx_rows, d), s = ctx.shape, x.shape[0]
    t = n_ctx_rows + s
    nctx = n_ctx_rows // tr

    def body(c_ref, x_ref, g_ref, mod_ref, h_ref):
        is_ctx = pl.program_id(0) < nctx
        x = jnp.where(is_ctx, c_ref[...], x_ref[...])
        n = x * _rms_stats(x) * g_ref[...]
        sh = jnp.where(is_ctx, mod_ref[0:1, :], mod_ref[2:3, :])
        sc = jnp.where(is_ctx, mod_ref[1:2, :], mod_ref[3:4, :])
        h_ref[...] = (n * (1.0 + sc) + sh).astype(BF16)

    return pl.pallas_call(
        body, name="norm_mod_fwd", grid=(t // tr,),
        in_specs=[*_token_specs(n_ctx_rows, d, tr), pl.BlockSpec((1, d), lambda i: (0, 0)),
                  pl.BlockSpec((4, d), lambda i: (0, 0))],
        out_specs=pl.BlockSpec((tr, d), lambda i: (i, 0)), out_shape=_sds((t, d), BF16),
        compiler_params=_params("parallel"),
    )(ctx, x, g, mod4)


def _norm_mod_bwd1(ctx, x, dh, dx1, g, mod4, tr, side=None):
    (n_ctx_rows, d), s = ctx.shape, x.shape[0]
    t = n_ctx_rows + s
    nctx = n_ctx_rows // tr

    def body(c_ref, x_ref, dh_ref, dx1_ref, g_ref, mod_ref, dx_ref, sums_ref):
        i = pl.program_id(0)
        is_ctx = i < nctx

        @pl.when(i == 0)
        def _():
            sums_ref[...] = jnp.zeros_like(sums_ref)

        x = jnp.where(is_ctx, c_ref[...], x_ref[...])
        dh_ = dh_ref[...]
        rstd = _rms_stats(x)
        xhat = x * rstd
        gg = g_ref[...]
        sc = jnp.where(is_ctx, mod_ref[1:2, :], mod_ref[3:4, :])
        dxhat = dh_ * (1.0 + sc) * gg
        dx_ref[...] = dx1_ref[...] + _rms_bwd(dxhat, xhat, rstd)
        part = [_colsum(dh_), _colsum(dh_ * xhat * gg), _colsum(dh_ * (1.0 + sc) * xhat)]

        @pl.when(is_ctx)
        def _():
            for k, row in enumerate(part):
                sums_ref[3 + k:4 + k, :] += row

        @pl.when(jnp.logical_not(is_ctx))
        def _():
            for k, row in enumerate(part):
                sums_ref[k:k + 1, :] += row

    lat = lambda i: (jnp.maximum(i - nctx, 0), 0)
    return _call(
        body, side=side, name="norm_mod_bwd1", grid=(t // tr,),
        in_specs=[*_token_specs(n_ctx_rows, d, tr), pl.BlockSpec((tr, d), lambda i: (i, 0)),
                  pl.BlockSpec((tr, d), lat), pl.BlockSpec((1, d), lambda i: (0, 0)),
                  pl.BlockSpec((4, d), lambda i: (0, 0))],
        out_specs=[pl.BlockSpec((tr, d), lat), pl.BlockSpec((8, d), lambda i: (0, 0))],
        out_shape=[_sds((s, d)), _sds((8, d))], sem=("arbitrary",),
    )(ctx, x, dh, dx1, g, mod4)


def _resid_norm_mod_fwd(x, mix, g, mod3, tr):
    s, d = x.shape

    def body(x_ref, mix_ref, g_ref, mod_ref, x1_ref, h_ref):
        x1 = x_ref[...] + mod_ref[0:1, :] * mix_ref[...]
        x1_ref[...] = x1
        n = x1 * _rms_stats(x1) * g_ref[...]
        h_ref[...] = (n * (1.0 + mod_ref[2:3, :]) + mod_ref[1:2, :]).astype(BF16)

    row = pl.BlockSpec((tr, d), lambda i: (i, 0))
    return pl.pallas_call(
        body, name="resid_norm_mod_fwd", grid=(s // tr,),
        in_specs=[row, row, pl.BlockSpec((1, d), lambda i: (0, 0)), pl.BlockSpec((3, d), lambda i: (0, 0))],
        out_specs=[row, row], out_shape=[_sds((s, d)), _sds((s, d), BF16)], compiler_params=_params("parallel"),
    )(x, mix, g, mod3)


def _norm_mod_bwd2(x1, dh2, dx2, mix, g, mod3, tr, side=None):
    s, d = x1.shape

    def body(x_ref, dh_ref, dx2_ref, mix_ref, g_ref, mod_ref, dx1_ref, dmix_ref, sums_ref):
        @pl.when(pl.program_id(0) == 0)
        def _():
            sums_ref[...] = jnp.zeros_like(sums_ref)

        x = x_ref[...]
        dh_ = dh_ref[...]
        rstd = _rms_stats(x)
        xhat = x * rstd
        gg = g_ref[...]
        sc = mod_ref[2:3, :]
        dx1 = dx2_ref[...] + _rms_bwd(dh_ * (1.0 + sc) * gg, xhat, rstd)
        dx1_ref[...] = dx1
        dmix_ref[...] = (dx1 * mod_ref[0:1, :]).astype(BF16)
        part = [_colsum(dh_), _colsum(dh_ * xhat * gg), _colsum(dh_ * (1.0 + sc) * xhat), _colsum(dx1 * mix_ref[...])]
        for k, row in enumerate(part):
            sums_ref[k:k + 1, :] += row

    row = pl.BlockSpec((tr, d), lambda i: (i, 0))
    return _call(
        body, side=side, name="norm_mod_bwd2", grid=(s // tr,),
        in_specs=[row, row, row, row, pl.BlockSpec((1, d), lambda i: (0, 0)), pl.BlockSpec((3, d), lambda i: (0, 0))],
        out_specs=[row, row, pl.BlockSpec((8, d), lambda i: (0, 0))],
        out_shape=[_sds((s, d)), _sds((s, d), BF16), _sds((8, d))], sem=("arbitrary",),
    )(x1, dh2, dx2, mix, g, mod3)


def _final_fwd_bwd(x1, down, target, g_final, gate, tr):
    s, d = x1.shape

    def body(x1_ref, down_ref, t_ref, g_ref, gate_ref, dx2_ref, ddown_ref, sums_ref, loss_ref):
        @pl.when(pl.program_id(0) == 0)
        def _():
            sums_ref[...] = jnp.zeros_like(sums_ref)
            loss_ref[...] = jnp.zeros_like(loss_ref)

        down_ = down_ref[...]
        gate_ = gate_ref[...]
        x2 = x1_ref[...] + gate_ * down_
        rstd = _rms_stats(x2)
        xhat = x2 * rstd
        gg = g_ref[...]
        err = xhat * gg - t_ref[...]
        loss_ref[...] += 0.5 * jnp.sum(jnp.mean(err * err, axis=-1, keepdims=True))
        dy = err * (1.0 / d)
        dx2 = _rms_bwd(dy * gg, xhat, rstd)
        dx2_ref[...] = dx2
        ddown_ref[...] = (dx2 * gate_).astype(BF16)
        sums_ref[0:1, :] += _colsum(dy * xhat)
        sums_ref[1:2, :] += _colsum(dx2 * down_)

    row = pl.BlockSpec((tr, d), lambda i: (i, 0))
    vec = pl.BlockSpec((1, d), lambda i: (0, 0))
    return pl.pallas_call(
        body, name="final_fwd_bwd", grid=(s // tr,), in_specs=[row, row, row, vec, vec],
        out_specs=[row, row, pl.BlockSpec((8, d), lambda i: (0, 0)), pl.BlockSpec((8, LANES), lambda i: (0, 0))],
        out_shape=[_sds((s, d)), _sds((s, d), BF16), _sds((8, d)), _sds((8, LANES))],
        compiler_params=_params("arbitrary"),
    )(x1, down, target, g_final, gate)


def _swap_pairs(v):
    lane = lax.broadcasted_iota(jnp.int32, v.shape, 1)
    return jnp.where(lane % 2 == 0, pltpu.roll(v, LANES - 1, 1), pltpu.roll(v, 1, 1))


def _head_prep_fwd(z, col_off, n_heads, gain, cos, sin, tr, name, side=None):
    t = z.shape[0]
    per = math.gcd(4, n_heads, col_off // LANES)
    w = per * LANES
    hb = col_off // w

    def body(z_ref, g_ref, cos_ref, sin_ref, o_ref):
        for hh in range(per):
            cols = slice(hh * LANES, (hh + 1) * LANES)
            x = z_ref[:, cols]
            y = x * _rms_stats(x) * g_ref[...]
            o_ref[:, cols] = (y * cos_ref[...] + _swap_pairs(y) * sin_ref[...]).astype(BF16)

    tab = pl.BlockSpec((tr, LANES), lambda i, j: (i, 0))
    return _call(
        body, side=side, name=name, grid=(t // tr, n_heads // per),
        in_specs=[pl.BlockSpec((tr, w), lambda i, j: (i, hb + j)), pl.BlockSpec((1, LANES), lambda i, j: (0, 0)),
                  tab, tab],
        out_specs=pl.BlockSpec((tr, w), lambda i, j: (i, j)), out_shape=_sds((t, n_heads * LANES), BF16),
        sem=("parallel", "parallel"),
    )(z, gain, cos, sin)


def _dz_start(t, n_in, n_ctx_rows):
    tcol = _tile(n_in, 1024, LANES)

    def body(o_ref):
        o_ref[...] = jnp.zeros_like(o_ref)

    return pl.pallas_call(body, name="dz_start", grid=(n_in // tcol,),
                          out_specs=pl.BlockSpec((n_ctx_rows, tcol), lambda j: (0, j)), out_shape=_sds((t, n_in), BF16),
                          compiler_params=_params("parallel"))()


def _head_prep_bwd(z, col_off, n_heads, gain, cos, sin, dout, row_off, tr, dz, name, side=None):
    r = dout.shape[0]
    per = math.gcd(4, n_heads, col_off // LANES)
    w = per * LANES
    hb = col_off // w
    rb = row_off // tr

    def body(z_ref, g_ref, cos_ref, sin_ref, d_ref, _, dz_ref, dg_ref):
        @pl.when(jnp.logical_and(pl.program_id(0) == 0, pl.program_id(1) == 0))
        def _():
            dg_ref[...] = jnp.zeros_like(dg_ref)

        for hh in range(per):
            cols = slice(hh * LANES, (hh + 1) * LANES)
            x = z_ref[:, cols]
            rstd = _rms_stats(x)
            xhat = x * rstd
            dd = d_ref[:, cols]
            dy = dd * cos_ref[...] - _swap_pairs(dd) * sin_ref[...]
            dg_ref[0:1, :] += _colsum(dy * xhat)
            dz_ref[:, cols] = _rms_bwd(dy * g_ref[...], xhat, rstd).astype(BF16)

    tab = pl.BlockSpec((tr, LANES), lambda i, j: (rb + i, 0))
    window = pl.BlockSpec((tr, w), lambda i, j: (rb + i, hb + j))
    return _call(
        body, side=side, name=name, grid=(r // tr, n_heads // per),
        in_specs=[window, pl.BlockSpec((1, LANES), lambda i, j: (0, 0)), tab, tab,
                  pl.BlockSpec((tr, w), lambda i, j: (i, j)), ANY],
        out_specs=[window, pl.BlockSpec((8, LANES), lambda i, j: (0, 0))],
        out_shape=[_sds(dz.shape, BF16), _sds((8, LANES))], input_output_aliases={5: 0}, sem=("arbitrary", "arbitrary"),
    )(z, gain, cos, sin, dout, dz)


def _attn_fwd(qr, kr, z, v_off, n_ctx_rows, group, tq, side=None):
    t, kvw = kr.shape
    s = t - n_ctx_rows
    n_kv = kvw // LANES
    scale = LANES ** -0.5
    qb0 = n_ctx_rows // tq
    vb = v_off // LANES

    def body(q_ref, k_ref, v_ref, o_ref, lse_ref):
        k = k_ref[...]
        v = v_ref[...].astype(BF16)
        lse_ref[...] = jnp.zeros_like(lse_ref)
        for g in range(group):
            cols = slice(g * LANES, (g + 1) * LANES)
            sc = lax.dot_general(q_ref[:, cols], k, (((1,), (1,)), ((), ())), preferred_element_type=F32) * scale
            m = jnp.max(sc, axis=-1, keepdims=True)
            e = jnp.exp(sc - m)
            l = jnp.sum(e, axis=-1, keepdims=True)
            p = e * (1.0 / l)
            o_ref[:, cols] = jnp.dot(p.astype(BF16), v, preferred_element_type=F32).astype(BF16)
            lse_ref[:, g:g + 1] = m + jnp.log(l)

    return _call(
        body, side=side, name="attn_fwd", grid=(n_kv, s // tq),
        in_specs=[pl.BlockSpec((tq, group * LANES), lambda h, i: (qb0 + i, h)),
                  pl.BlockSpec((t, LANES), lambda h, i: (0, h)), pl.BlockSpec((t, LANES), lambda h, i: (0, vb + h))],
        out_specs=[pl.BlockSpec((tq, group * LANES), lambda h, i: (i, h)), pl.BlockSpec((tq, LANES), lambda h, i: (i, h))],
        out_shape=[_sds((s, n_kv * group * LANES), BF16), _sds((s, kvw))], sem=("parallel", "parallel"),
    )(qr, kr, z)


def _attn_bwd(qr, kr, z, v_off, d_o, attn_o, lse, dz, n_ctx_rows, group, tq, side=None):
    t, kvw = kr.shape
    s = t - n_ctx_rows
    n_kv = kvw // LANES
    scale = LANES ** -0.5
    qb0 = n_ctx_rows // tq
    vb = v_off // LANES
    n_q_blocks = s // tq
    tn_dims = (((0,), (0,)), ((), ()))
    nt_dims = (((1,), (1,)), ((), ()))

    def body(q_ref, k_ref, v_ref, do_ref, o_ref, lse_ref, _, dq_ref, dk_ref, dz_ref, dv_ref):
        @pl.when(pl.program_id(1) == 0)
        def _():
            dk_ref[...] = jnp.zeros_like(dk_ref)
            dv_ref[...] = jnp.zeros_like(dv_ref)

        k = k_ref[...]
        v = v_ref[...].astype(BF16)
        for g in range(group):
            cols = slice(g * LANES, (g + 1) * LANES)
            q = q_ref[:, cols]
            do_ = do_ref[:, cols]
            row_dot = jnp.sum(do_.astype(F32) * o_ref[:, cols].astype(F32), axis=-1, keepdims=True)
            sc = lax.dot_general(q, k, nt_dims, preferred_element_type=F32)
            p = jnp.exp(sc * scale - lse_ref[:, g:g + 1])
            dv_ref[...] += lax.dot_general(p.astype(BF16), do_, tn_dims, preferred_element_type=F32)
            dp = lax.dot_general(do_, v, nt_dims, preferred_element_type=F32)
            ds = (p * (dp - row_dot)).astype(BF16)
            dq_ref[:, cols] = jnp.dot(ds, k, preferred_element_type=F32) * scale
            dk_ref[...] += lax.dot_general(ds, q, tn_dims, preferred_element_type=F32)

        @pl.when(pl.program_id(1) == n_q_blocks - 1)
        def _():
            dk_ref[...] = dk_ref[...] * scale
            dz_ref[...] = dv_ref[...].astype(BF16)

    qspec = pl.BlockSpec((tq, group * LANES), lambda h, i: (qb0 + i, h))
    ospec = pl.BlockSpec((tq, group * LANES), lambda h, i: (i, h))
    kspec = pl.BlockSpec((t, LANES), lambda h, i: (0, h))
    vspec = pl.BlockSpec((t, LANES), lambda h, i: (0, vb + h))
    return _call(
        body, side=side, name="attn_bwd", grid=(n_kv, n_q_blocks),
        in_specs=[qspec, kspec, vspec, ospec, ospec, pl.BlockSpec((tq, LANES), lambda h, i: (i, h)), ANY],
        out_specs=[ospec, kspec, vspec], out_shape=[_sds((s, n_kv * group * LANES)), _sds((t, kvw)), _sds(dz.shape, BF16)],
        scratch_shapes=[pltpu.VMEM((t, LANES), F32)], input_output_aliases={6: 2}, sem=("parallel", "arbitrary"),
    )(qr, kr, z, d_o, attn_o, lse, dz)


def _row_mask(shape, rows):
    r = lax.broadcasted_iota(jnp.int32, shape, 0)
    m = r == rows[0]
    for v in rows[1:]:
        m = jnp.logical_or(m, r == v)
    return m


def _shift_rows(x, k, n_ctx_rows):
    t = x.shape[0]
    if k == 0:
        return x
    rolled = pltpu.roll(x, (-k) % t, 0)
    if k > 0:
        dead = [n_ctx_rows - 1 - i for i in range(k)] + [t - 1 - i for i in range(k)]
    else:
        dead = [i for i in range(-k)] + [n_ctx_rows + i for i in range(-k)]
    return jnp.where(_row_mask(x.shape, dead), 0.0, rolled)


def _conv(x, w, b, n_ctx_rows):
    y = b
    for k in range(CONV_WIDTH):
        y = y + _shift_rows(x, k - 1, n_ctx_rows) * w[k:k + 1, :]
    return y


def _gates(xc_bf, w_r, b_r, w_i, b_i, lam):
    r = jax.nn.sigmoid(jnp.dot(xc_bf, w_r.astype(BF16), preferred_element_type=F32) + b_r)
    i = jax.nn.sigmoid(jnp.dot(xc_bf, w_i.astype(BF16), preferred_element_type=F32) + b_i)
    log_a = -LRU_C * r * _softplus(-lam)
    a = jnp.exp(log_a)
    mult = jnp.sqrt(-_expm1_nonpos(2.0 * log_a))
    return r, i, a, mult


def _rnn_specs(t, xr_off):
    xb = xr_off // LANES
    return dict(
        zcol=pl.BlockSpec((t, LANES), lambda j: (0, xb + j)), col=pl.BlockSpec((t, LANES), lambda j: (0, j)),
        conv_w=pl.BlockSpec((CONV_WIDTH, LANES), lambda j: (0, j)), vec=pl.BlockSpec((1, LANES), lambda j: (0, j)),
        gate_w=pl.BlockSpec((2, 1, LANES, LANES), lambda j: (0, j, 0, 0)), two=pl.BlockSpec((2, LANES), lambda j: (0, j)))


def _rnn_prep(z, xr_off, conv_w, conv_b, w_rg, b_rg, w_ig, b_ig, lam, n_ctx_rows, side=None):
    t = z.shape[0]
    d = conv_b.shape[1]
    sp = _rnn_specs(t, xr_off)

    def body(z_ref, cw_ref, cb_ref, wr_ref, br_ref, wi_ref, bi_ref, lam_ref, xc_ref, af_ref, bf_ref, ab_ref, bb_ref):
        xc = _conv(z_ref[...], cw_ref[...], cb_ref[...], n_ctx_rows)
        xc_ref[...] = xc
        xc_bf = xc.astype(BF16)
        for dr, (a_ref, b_ref) in enumerate(((af_ref, bf_ref), (ab_ref, bb_ref))):
            _, i, a, mult = _gates(xc_bf, wr_ref[dr, 0], br_ref[dr:dr + 1, :], wi_ref[dr, 0], bi_ref[dr:dr + 1, :],
                                   lam_ref[dr:dr + 1, :])
            a_ref[...] = a
            b_ref[...] = mult * (i * xc)

    return _call(
        body, side=side, name="rnn_prep", grid=(d // LANES,),
        in_specs=[sp["zcol"], sp["conv_w"], sp["vec"], sp["gate_w"], sp["two"], sp["gate_w"], sp["two"], sp["two"]],
        out_specs=[sp["col"]] * 5, out_shape=[_sds((t, d))] * 5, sem=("parallel",),
    )(z, conv_w, conv_b, w_rg, b_rg, w_ig, b_ig, lam)


def _scan(chains, *, post, n_ctx_rows, name, tc=256, side=None):
    t, d = chains[0][0].shape
    tc = _tile(math.gcd(n_ctx_rows, t - n_ctx_rows), tc, SUBLANES)
    nt, nctx = t // tc, n_ctx_rows // tc
    nlat = nt - nctx
    nc = len(chains)
    lat_only = [b.shape[0] != t for _, b, _ in chains]
    ups = [order.endswith("up") for _, _, order in chains]

    def chunk_of(order):
        def chunk(i):
            if order == "ctx_lat_up":
                return i
            if order == "lat_ctx_down":
                return nt - 1 - i
            if order == "ctx_lat_down":
                return jnp.where(i < nctx, nctx - 1 - i, nt - 1 - (i - nctx))
            return jnp.where(i < nlat, nctx + i, i - nlat)
        return chunk

    chunks = [chunk_of(order) for _, _, order in chains]

    def body(*refs):
        ab_refs, o_refs, carry_ref = refs[:2 * nc], refs[2 * nc:3 * nc], refs[3 * nc]

        @pl.when(pl.program_id(0) == 0)
        def _():
            carry_ref[...] = jnp.zeros_like(carry_ref)

        live = [jnp.where(chunks[n](pl.program_id(0)) >= nctx, 1.0, 0.0) if lat_only[n] else None for n in range(nc)]

        def group(gi, carries):
            carries = list(carries)
            bases = [pl.multiple_of((gi if ups[n] else tc // SUBLANES - 1 - gi) * SUBLANES, SUBLANES) for n in range(nc)]
            for step in range(SUBLANES):
                for n in range(nc):
                    row = bases[n] + (step if ups[n] else SUBLANES - 1 - step)
                    a_r = ab_refs[2 * n][pl.ds(row, 1), :]
                    b_r = ab_refs[2 * n + 1][pl.ds(row, 1), :]
                    if live[n] is not None:
                        b_r = b_r * live[n]
                    if post:
                        out = b_r + carries[n]
                        carries[n] = a_r * out
                    else:
                        out = a_r * carries[n] + b_r
                        carries[n] = out
                    o_refs[n][pl.ds(row, 1), :] = out
            return tuple(carries)

        done = lax.fori_loop(0, tc // SUBLANES, group, tuple(carry_ref[n:n + 1, :] for n in range(nc)))
        for n in range(nc):
            carry_ref[n:n + 1, :] = done[n]

    in_specs, out_specs, args = [], [], []
    for n, (a, b, _) in enumerate(chains):
        full = pl.BlockSpec((tc, d), lambda i, n=n: (chunks[n](i), 0))
        lat = pl.BlockSpec((tc, d), lambda i, n=n: (jnp.maximum(chunks[n](i) - nctx, 0), 0))
        in_specs += [full, lat if lat_only[n] else full]
        out_specs.append(full)
        args += [a, b]
    return _call(
        body, side=side, name=name, grid=(nt,), in_specs=in_specs, out_specs=out_specs, out_shape=[_sds((t, d))] * nc,
        scratch_shapes=[pltpu.VMEM((SUBLANES, d), F32)], sem=("arbitrary",),
    )(*args)


def _rnn_bwd(z, xr_off, xc, g_f, g_b, h_f, h_b, conv_w, w_rg, b_rg, w_ig, b_ig, lam, dz, n_ctx_rows, side=None):
    t, d = xc.shape
    sp = _rnn_specs(t, xr_off)
    tn_dims = (((0,), (0,)), ((), ()))
    nt_dims = (((1,), (1,)), ((), ()))

    def body(z_ref, xc_ref, gf_ref, gb_ref, hf_ref, hb_ref, cw_ref, wr_ref, br_ref, wi_ref, bi_ref, lam_ref, _,
             dxr_ref, dwr_ref, dwi_ref, sums_ref):
        xc_ = xc_ref[...]
        xc_bf = xc_.astype(BF16)
        dxc = jnp.zeros_like(xc_)
        sums = [None] * 6
        for dr, (g_ref, h_ref) in enumerate(((gf_ref, hf_ref), (gb_ref, hb_ref))):
            w_r, w_i, lam_ = wr_ref[dr, 0], wi_ref[dr, 0], lam_ref[dr:dr + 1, :]
            r, i, a, mult = _gates(xc_bf, w_r, br_ref[dr:dr + 1, :], w_i, bi_ref[dr:dr + 1, :], lam_)
            g = g_ref[...]
            h = h_ref[...]
            if dr == 0:
                h_prev = jnp.where(_row_mask(h.shape, [0]), 0.0, pltpu.roll(h, 1, 0))
            else:
                h_prev = jnp.where(_row_mask(h.shape, [n_ctx_rows - 1]), 0.0, pltpu.roll(h, t - 1, 0))
            d_mult = g * i * xc_
            d_i = g * mult * xc_
            dxc = dxc + g * mult * i
            d_log_a = g * h_prev * a - d_mult * a * a / mult
            sp_ = _softplus(-lam_)
            d_r = d_log_a * (-LRU_C) * sp_
            d_sp = _colsum(d_log_a * (-LRU_C) * r)
            du_r = (d_r * r * (1.0 - r))
            du_i = (d_i * i * (1.0 - i))
            sums[dr] = _colsum(du_r)
            sums[2 + dr] = _colsum(du_i)
            sums[4 + dr] = d_sp * (-jax.nn.sigmoid(-lam_))
            du_r_bf, du_i_bf = du_r.astype(BF16), du_i.astype(BF16)
            dwr_ref[dr, 0] = lax.dot_general(xc_bf, du_r_bf, tn_dims, preferred_element_type=F32).astype(BF16)
            dwi_ref[dr, 0] = lax.dot_general(xc_bf, du_i_bf, tn_dims, preferred_element_type=F32).astype(BF16)
            dxc = dxc + lax.dot_general(du_r_bf, w_r.astype(BF16), nt_dims, preferred_element_type=F32)
            dxc = dxc + lax.dot_general(du_i_bf, w_i.astype(BF16), nt_dims, preferred_element_type=F32)
        xr = z_ref[...]
        cw = cw_ref[...]
        dxr = jnp.zeros_like(dxc)
        rows = list(sums)
        for k in range(CONV_WIDTH):
            dxr = dxr + _shift_rows(dxc, 1 - k, n_ctx_rows) * cw[k:k + 1, :]
            rows.append(_colsum(dxc * _shift_rows(xr, k - 1, n_ctx_rows)))
        rows.append(_colsum(dxc))
        dxr_ref[...] = dxr.astype(BF16)
        sums_ref[...] = jnp.zeros_like(sums_ref)
        for k, row in enumerate(rows):
            sums_ref[k:k + 1, :] = row

    return _call(
        body, side=side, name="rnn_bwd", grid=(d // LANES,),
        in_specs=[sp["zcol"]] + [sp["col"]] * 5 + [sp["conv_w"], sp["gate_w"], sp["two"], sp["gate_w"], sp["two"],
                                                  sp["two"], ANY],
        out_specs=[sp["zcol"], sp["gate_w"], sp["gate_w"], pl.BlockSpec((16, LANES), lambda j: (0, j))],
        out_shape=[_sds(dz.shape, BF16), _sds(w_rg.shape, BF16), _sds(w_ig.shape, BF16), _sds((16, d))],
        input_output_aliases={12: 0}, sem=("parallel",),
    )(z, xc, g_f, g_b, h_f, h_b, conv_w, w_rg, b_rg, w_ig, b_ig, lam, dz)


def _tiles2d(s, d, tr, tcol):
    return (s // tr, d // tcol), pl.BlockSpec((tr, tcol), lambda i, j: (i, j))


def _zspec(tr, tcol, row_off, col_off):
    rb, cb = row_off // tr, col_off // tcol
    return pl.BlockSpec((tr, tcol), lambda i, j: (rb + i, cb + j))


def _rnn_gate_fwd(h_f, h_b, z, xg_off, n_ctx_rows, tr, tcol, side=None):
    t, d = h_f.shape
    s = t - n_ctx_rows
    grid, out = _tiles2d(s, d, tr, tcol)
    hs = _zspec(tr, tcol, n_ctx_rows, 0)

    def body(hf_ref, hb_ref, xg_ref, u_ref):
        u_ref[...] = ((hf_ref[...] + hb_ref[...]) * _gelu(xg_ref[...])).astype(BF16)

    return _call(body, side=side, name="rnn_gate_fwd", grid=grid,
                 in_specs=[hs, hs, _zspec(tr, tcol, n_ctx_rows, xg_off)], out_specs=out, out_shape=_sds((s, d), BF16),
                 sem=("parallel", "parallel"))(h_f, h_b, z)


def _rnn_gate_bwd(d_u, h_f, h_b, z, xg_off, dz, n_ctx_rows, tr, tcol, side=None):
    t, d = h_f.shape
    s = t - n_ctx_rows
    grid, out = _tiles2d(s, d, tr, tcol)
    hs = _zspec(tr, tcol, n_ctx_rows, 0)
    window = _zspec(tr, tcol, n_ctx_rows, xg_off)

    def body(du_ref, hf_ref, hb_ref, xg_ref, _, dr_ref, dxg_ref):
        du = du_ref[...]
        xg = xg_ref[...]
        dr_ref[...] = du * _gelu(xg)
        dxg_ref[...] = (du * (hf_ref[...] + hb_ref[...]) * _gelu_grad(xg)).astype(BF16)

    return _call(body, side=side, name="rnn_gate_bwd", grid=grid, in_specs=[out, hs, hs, window, ANY],
                 out_specs=[out, window], out_shape=[_sds((s, d)), _sds(dz.shape, BF16)], input_output_aliases={4: 1},
                 sem=("parallel", "parallel"))(d_u, h_f, h_b, z, dz)


def _merge_fwd(y_attn, y_rnn, z, gl_off, n_ctx_rows, tr, tcol):
    s, d = y_attn.shape
    grid, out = _tiles2d(s, d, tr, tcol)

    def body(ya_ref, yr_ref, ga_ref, gr_ref, o_ref):
        o_ref[...] = (jax.nn.sigmoid(ga_ref[...]) * ya_ref[...] + jax.nn.sigmoid(gr_ref[...]) * yr_ref[...]).astype(BF16)

    return pl.pallas_call(
        body, name="merge_fwd", grid=grid,
        in_specs=[out, out, _zspec(tr, tcol, n_ctx_rows, gl_off), _zspec(tr, tcol, n_ctx_rows, gl_off + d)],
        out_specs=out, out_shape=_sds((s, d), BF16), compiler_params=_params("parallel", "parallel"),
    )(y_attn, y_rnn, z, z)


def _merge_bwd(d_mrg, y, z, gl_off, dz, n_ctx_rows, tr, tcol, name):
    s, d = y.shape
    grid, out = _tiles2d(s, d, tr, tcol)
    window = _zspec(tr, tcol, n_ctx_rows, gl_off)

    def body(dm_ref, y_ref, gl_ref, _, dy_ref, dgl_ref):
        dm = dm_ref[...]
        g = jax.nn.sigmoid(gl_ref[...])
        dy_ref[...] = (dm * g).astype(BF16)
        dgl_ref[...] = (dm * y_ref[...] * g * (1.0 - g)).astype(BF16)

    return _call(body, name=name, grid=grid, in_specs=[out, out, window, ANY], out_specs=[out, window],
                 out_shape=[_sds((s, d), BF16), _sds(dz.shape, BF16)], input_output_aliases={3: 1},
                 sem=("parallel", "parallel"))(d_mrg, y, z, dz)


def _sq_relu(up, tr, tcol, side=None):
    grid, out = _tiles2d(*up.shape, _tile(up.shape[0], 2 * tr, 16), _tile(up.shape[1], 4 * tcol, LANES))

    def body(u_ref, o_ref):
        r = jnp.maximum(u_ref[...], 0.0)
        o_ref[...] = (r * r).astype(BF16)

    return _call(body, side=side, name="sq_relu", grid=grid, in_specs=[out], out_specs=out,
                 out_shape=_sds(up.shape, BF16), sem=("parallel", "parallel"))(up)


def _cast_into_window(w, chip, col_sharded, name):
    r, c = w.shape
    tr, tcol = _tile(r, 512, 16), _tile(c, 1024, LANES)
    nrb, ncb = r // tr, c // tcol

    def body(chip_ref, w_ref, o_ref):
        o_ref[...] = w_ref[...].astype(BF16)

    if col_sharded:
        omap = lambda i, j, chip_ref: (i, chip_ref[0] * ncb + j)
    else:
        omap = lambda i, j, chip_ref: (chip_ref[0] * nrb + i, j)
    return pl.pallas_call(
        body, name=name,
        grid_spec=pltpu.PrefetchScalarGridSpec(
            num_scalar_prefetch=1, grid=(nrb, ncb),
            in_specs=[pl.BlockSpec((tr, tcol), lambda i, j, chip_ref: (i, j))], out_specs=pl.BlockSpec((tr, tcol), omap)),
        out_shape=_sds((r, c * N_CHIPS) if col_sharded else (r * N_CHIPS, c), BF16),
        compiler_params=_params("parallel", "parallel"),
    )(chip, w)


def _sum_leading(parts, name):
    n, r, c = parts.shape
    tr, tcol = _tile(r, 512, SUBLANES), _tile(c, 1024, LANES)

    def body(p_ref, o_ref):
        tot = p_ref[0]
        for k in range(1, n):
            tot = tot + p_ref[k]
        o_ref[...] = tot

    return pl.pallas_call(
        body, name=name, grid=(r // tr, c // tcol), in_specs=[pl.BlockSpec((n, tr, tcol), lambda i, j: (0, i, j))],
        out_specs=pl.BlockSpec((tr, tcol), lambda i, j: (i, j)), out_shape=_sds((r, c)),
        compiler_params=_params("parallel", "parallel"),
    )(parts)


def _add_half(full, other, core, split_rows, name):
    r, c = other.shape
    tr, tcol = _tile(r, 512, 16), _tile(c, 1024, LANES)
    nrb, ncb = r // tr, c // tcol

    def body(core_ref, f_ref, o_ref, out_ref):
        out_ref[...] = (f_ref[...].astype(F32) + o_ref[...].astype(F32)).astype(out_ref.dtype)

    if split_rows:
        fmap = lambda i, j, core_ref: (core_ref[0] * nrb + i, j)
    else:
        fmap = lambda i, j, core_ref: (i, core_ref[0] * ncb + j)
    same = lambda i, j, core_ref: (i, j)
    return pl.pallas_call(
        body, name=name,
        grid_spec=pltpu.PrefetchScalarGridSpec(
            num_scalar_prefetch=1, grid=(nrb, ncb),
            in_specs=[pl.BlockSpec((tr, tcol), fmap), pl.BlockSpec((tr, tcol), same)],
            out_specs=pl.BlockSpec((tr, tcol), same)),
        out_shape=_sds((r, c), BF16), compiler_params=_params("parallel", "parallel"),
    )(core, full, other)


def _sum_regions(pair, got, place, col_sharded, name):
    n_got, r, c = got.shape
    tr, tcol = _tile(r, 512, 16), _tile(c, 1024, LANES)
    nrb, ncb = r // tr, c // tcol

    def body(place_ref, p_ref, g_ref, out_ref):
        tot = p_ref[...].astype(F32)
        for k in range(n_got):
            tot = tot + g_ref[k].astype(F32)
        out_ref[...] = tot

    if col_sharded:
        pmap = lambda i, j, pr: (i, pr[0] * ncb + j)
        omap = lambda i, j, pr: (pr[1] * nrb + i, j)
        out_shape = (2 * r, c)
    else:
        pmap = lambda i, j, pr: (pr[0] * nrb + i, j)
        omap = lambda i, j, pr: (i, pr[1] * ncb + j)
        out_shape = (r, 2 * c)
    return pl.pallas_call(
        body, name=name,
        grid_spec=pltpu.PrefetchScalarGridSpec(
            num_scalar_prefetch=1, grid=(nrb, ncb),
            in_specs=[pl.BlockSpec((tr, tcol), pmap), pl.BlockSpec((n_got, tr, tcol), lambda i, j, pr: (0, i, j))],
            out_specs=pl.BlockSpec((tr, tcol), omap)),
        out_shape=_sds(out_shape), compiler_params=_params("parallel", "parallel"),
    )(place, pair, got)


def _fold_forwarded(pair, fold, nbr_chips, name):
    _, r, c = fold.shape
    tr, tcol = _tile(r, 512, 16), _tile(c, 1024, LANES)
    nrb, ncb = r // tr, c // tcol

    def body(nbr_ref, px_ref, py_ref, f_ref, out_ref):
        out_ref[0] = (px_ref[...].astype(F32) + f_ref[1].astype(F32)).astype(out_ref.dtype)
        out_ref[1] = (py_ref[...].astype(F32) + f_ref[0].astype(F32)).astype(out_ref.dtype)

    both = pl.BlockSpec((2, tr, tcol), lambda i, j, nb: (0, i, j))
    return pl.pallas_call(
        body, name=name,
        grid_spec=pltpu.PrefetchScalarGridSpec(
            num_scalar_prefetch=1, grid=(nrb, ncb),
            in_specs=[pl.BlockSpec((tr, tcol), lambda i, j, nb: (i, nb[0] * ncb + j)),
                      pl.BlockSpec((tr, tcol), lambda i, j, nb: (nrb + i, nb[1] * ncb + j)), both],
            out_specs=both),
        out_shape=_sds(fold.shape, fold.dtype), compiler_params=_params("parallel", "parallel"),
    )(nbr_chips, pair, pair, fold)


def _adamw(w, g, m, v, name):
    r, c = w.shape
    tr, tcol = _tile(r, 512, SUBLANES), _tile(c, 1024, LANES)
    blk = pl.BlockSpec((tr, tcol), lambda i, j: (i, j))

    def body(w_ref, g_ref, m_ref, v_ref, d_ref, nm_ref, nv_ref, g_out_ref):
        g_ = g_ref[...]
        g_out_ref[...] = g_
        m_ = ADAM_B1 * m_ref[...] + (1.0 - ADAM_B1) * g_
        v_ = ADAM_B2 * v_ref[...] + (1.0 - ADAM_B2) * (g_ * g_)
        m_hat = m_ / (1.0 - ADAM_B1 ** ADAM_STEP)
        v_hat = v_ / (1.0 - ADAM_B2 ** ADAM_STEP)
        d_ref[...] = -ADAM_LR * (m_hat / (jnp.sqrt(v_hat) + ADAM_EPS) + ADAM_WD * w_ref[...])
        nm_ref[...] = m_
        nv_ref[...] = v_

    return _call(body, name=name, grid=(r // tr, c // tcol), in_specs=[blk] * 4, out_specs=[blk] * 4,
                 out_shape=[_sds((r, c))] * 4, sem=("parallel", "parallel"))(w, g, m, v)


def _place():
    x, y, c = lax.axis_index("x"), lax.axis_index("y"), lax.axis_index("c")
    chips = [(1 - x, y), (x, 1 - y), (1 - x, 1 - y)]
    return x, y, c, chips


def _all_gather8(blk, name):
    m, n = blk.shape

    def body(x_ref, out_ref, send_sems, recv_sems, local_sem):
        x, y, c, chips = _place()
        me, sibling = (x, y, c), (x, y, 1 - c)

        def rows(px, py, pc):
            return out_ref.at[pl.ds((4 * px + 2 * py + pc) * m, m), :]

        def copy(k, block, to, src=None):
            return pltpu.make_async_remote_copy(
                src_ref=rows(*block) if src is None else src, dst_ref=rows(*block), send_sem=send_sems.at[k],
                recv_sem=recv_sems.at[k], device_id=to, device_id_type=MESH_ID)

        mine = pltpu.make_async_copy(x_ref, rows(*me), local_sem)
        mine.start()
        first = [copy(0, me, sibling, src=x_ref)]
        first += [copy(1 + j, me, (*chip, c), src=x_ref) for j, chip in enumerate(chips)]
        for cp in first:
            cp.start()
        passed = [copy(4 + j, (*chip, c), sibling) for j, chip in enumerate(chips)]
        for j, chip in enumerate(chips):
            copy(1 + j, (*chip, c), me).wait_recv()
            passed[j].start()
        copy(0, sibling, me).wait_recv()
        for j, chip in enumerate(chips):
            copy(4 + j, (*chip, 1 - c), me).wait_recv()
        for cp in first + passed:
            cp.wait_send()
        mine.wait()

    return pl.pallas_call(
        body, name=name, out_shape=_sds((N_DEV * m, n), blk.dtype), in_specs=[ANY], out_specs=ANY,
        scratch_shapes=[pltpu.SemaphoreType.DMA((7,)), pltpu.SemaphoreType.DMA((7,)), pltpu.SemaphoreType.DMA],
    )(blk)


def _half(ref, core, split_rows):
    r, c = ref.shape
    if split_rows:
        return ref.at[pl.ds(core * (r // 2), r // 2), :]
    return ref.at[:, pl.ds(core * (c // 2), c // 2)]


def _chip_block(ref, j, col_sharded):
    r, c = ref.shape
    if col_sharded:
        return ref.at[:, pl.ds(j * (c // N_CHIPS), c // N_CHIPS)]
    return ref.at[pl.ds(j * (r // N_CHIPS), r // N_CHIPS), :]


def _rows_part(ref, part):
    lo, hi, n = part
    r = ref.shape[0]
    return ref if (lo, hi) == (0, n) else ref.at[pl.ds(lo * (r // n), (hi - lo) * (r // n)), :]


def _copy(send_sems, recv_sems, k, src, dst, to):
    return pltpu.make_async_remote_copy(src_ref=src, dst_ref=dst, send_sem=send_sems.at[k], recv_sem=recv_sems.at[k],
                                        device_id=to, device_id_type=MESH_ID)


def _in_place(arrays):
    return tuple(arrays), tuple(_sds(a.shape, a.dtype) for a in arrays), {i: i for i in range(len(arrays))}


def _gather_ici(fulls, col_sharded, part=(0, 1, 1)):
    nw = len(fulls)

    def build(_, refs, send_sems, recv_sems, sem0):
        x, y, c, chips = _place()
        sends, recvs = [], []
        for w in range(nw):
            win = lambda j: _rows_part(_half(_chip_block(refs[w], j, col_sharded[w]), c, True), part)
            for k, (cx, cy) in enumerate(chips):
                sem = sem0 + 3 * w + k
                sends.append(_copy(send_sems, recv_sems, sem, win(2 * x + y), win(2 * x + y), (cx, cy, c)))
                recvs.append(_copy(send_sems, recv_sems, sem, win(2 * cx + cy), win(2 * cx + cy), (cx, cy, c)))
        return sends, recvs

    return _Side(*_in_place(fulls), 3 * nw, build)


def _gather_d2d(fulls, col_sharded):
    nw = len(fulls)

    def build(_, refs, send_sems, recv_sems, sem0):
        x, y, c, chips = _place()
        sends, recvs = [], []
        for w in range(nw):
            win = lambda j, core: _half(_chip_block(refs[w], j, col_sharded[w]), core, True)
            for k, (cx, cy) in enumerate(chips):
                sem = sem0 + 3 * w + k
                sends.append(_copy(send_sems, recv_sems, sem, win(2 * cx + cy, c), win(2 * cx + cy, c), (x, y, 1 - c)))
                recvs.append(_copy(send_sems, recv_sems, sem, win(2 * cx + cy, 1 - c), win(2 * cx + cy, 1 - c),
                                   (x, y, 1 - c)))
        return sends, recvs

    return _Side(*_in_place(fulls), 3 * nw, build)


def _gather_neighbours(fulls, col_sharded, part=(0, 1, 1)):
    nw = len(fulls)

    def build(_, refs, send_sems, recv_sems, sem0):
        x, y, c, chips = _place()
        sends, recvs = [], []
        for w in range(nw):
            win = lambda j: _rows_part(_half(_chip_block(refs[w], j, col_sharded[w]), c, True), part)
            for k, (cx, cy) in enumerate(chips[:2]):
                sem = sem0 + 2 * w + k
                sends.append(_copy(send_sems, recv_sems, sem, win(2 * x + y), win(2 * x + y), (cx, cy, c)))
                recvs.append(_copy(send_sems, recv_sems, sem, win(2 * cx + cy), win(2 * cx + cy), (cx, cy, c)))
        return sends, recvs

    return _Side(*_in_place(fulls), 2 * nw, build)


def _gather_forward(fulls, col_sharded, ways=(True, True), siblings=False):
    nw = len(fulls)

    def build(_, refs, send_sems, recv_sems, sem0):
        x, y, c, (cx_, cy_, cd_) = _place()
        sends, recvs = [], []
        for w in range(nw):
            win = lambda chip, core: _half(_chip_block(refs[w], 2 * chip[0] + chip[1], col_sharded[w]), core, True)
            part = lambda ref, p: _rows_part(ref, (p, p + 1, 2))
            for p, (src, to) in enumerate(((cx_, cy_), (cy_, cx_))):
                if ways[p]:
                    sem = sem0 + 4 * w + p
                    sends.append(_copy(send_sems, recv_sems, sem, part(win(src, c), p), part(win(src, c), p), (*to, c)))
                    recvs.append(_copy(send_sems, recv_sems, sem, part(win(cd_, c), p), part(win(cd_, c), p), (*to, c)))
            if siblings:
                for k, chip in enumerate((cx_, cy_)):
                    sem = sem0 + 4 * w + 2 + k
                    sends.append(_copy(send_sems, recv_sems, sem, win(chip, c), win(chip, c), (x, y, 1 - c)))
                    recvs.append(_copy(send_sems, recv_sems, sem, win(chip, 1 - c), win(chip, 1 - c), (x, y, 1 - c)))
        return sends, recvs

    return _Side(*_in_place(fulls), 4 * nw, build)


def _gather_diagonal_d2d(full, col_sharded):
    def build(_, refs, send_sems, recv_sems, sem0):
        x, y, c, chips = _place()
        cx, cy = chips[2]
        win = lambda core: _half(_chip_block(refs[0], 2 * cx + cy, col_sharded), core, True)
        return ([_copy(send_sems, recv_sems, sem0, win(c), win(c), (x, y, 1 - c))],
                [_copy(send_sems, recv_sems, sem0, win(1 - c), win(1 - c), (x, y, 1 - c))])

    return _Side(*_in_place([full]), 1, build)


def _scatter_forward_1(pair):
    r, n = pair.shape
    results = (_sds((2, r, n // N_CHIPS), pair.dtype), _sds((2, r // 2, n // N_CHIPS), pair.dtype))

    def build(refs, res, send_sems, recv_sems, sem0):
        x, y, c, (cx_, cy_, cd_) = _place()
        region = lambda chip, p: _rows_part(_chip_block(refs[0], 2 * chip[0] + chip[1], True), (p, p + 1, 2))
        got, fold = res
        copies = [
            _copy(send_sems, recv_sems, sem0, region(cx_, 1), _rows_part(got.at[0], (1, 2, 2)), (*cx_, c)),
            _copy(send_sems, recv_sems, sem0 + 1, region(cd_, 1), fold.at[0], (*cx_, c)),
            _copy(send_sems, recv_sems, sem0 + 2, region(cy_, 0), _rows_part(got.at[1], (0, 1, 2)), (*cy_, c)),
            _copy(send_sems, recv_sems, sem0 + 3, region(cd_, 0), fold.at[1], (*cy_, c))]
        return copies, copies

    return _Side((pair,), results, {}, 4, build)


def _scatter_forward_2(passed, got):
    def build(refs, res, send_sems, recv_sems, sem0):
        x, y, c, (cx_, cy_, _) = _place()
        copies = [_copy(send_sems, recv_sems, sem0, refs[0].at[0], _rows_part(res[0].at[0], (0, 1, 2)), (*cx_, c)),
                  _copy(send_sems, recv_sems, sem0 + 1, refs[0].at[1], _rows_part(res[0].at[1], (1, 2, 2)), (*cy_, c))]
        return copies, copies

    return _Side((passed, got), (_sds(got.shape, got.dtype),), {1: 0}, 2, build)


def _exchange(side, name):
    return _call(None, side=side, name=name)()[1]


def _swap_halves(grads, col_sharded):
    nw = len(grads)
    out_shapes = [_sds((g.shape[0] // 2, g.shape[1]) if col else (g.shape[0], g.shape[1] // 2), g.dtype)
                  for g, col in zip(grads, col_sharded)]

    def build(g_refs, o_refs, send_sems, recv_sems, sem0):
        x, y, c, _ = _place()
        copies = [_copy(send_sems, recv_sems, sem0 + w, _half(g_refs[w], 1 - c, col_sharded[w]), o_refs[w],
                        (x, y, 1 - c)) for w in range(nw)]
        return copies, copies

    return _Side(tuple(grads), tuple(out_shapes), {}, nw, build)


def _scatter_regions(pairs, col_sharded, part=(0, 1, 1), into=None):
    nw = len(pairs)

    def region_shape(p, col):
        return (p.shape[0], p.shape[1] // N_CHIPS) if col else (p.shape[0] // N_CHIPS, p.shape[1])

    out_shapes = tuple(_sds((N_CHIPS - 1, *region_shape(p, col)), p.dtype) for p, col in zip(pairs, col_sharded))

    def build(refs, o_refs, send_sems, recv_sems, sem0):
        x, y, c, chips = _place()
        copies = []
        for w in range(nw):
            for k, (cx, cy) in enumerate(chips):
                copies.append(_copy(
                    send_sems, recv_sems, sem0 + 3 * w + k,
                    _rows_part(_chip_block(refs[w], 2 * cx + cy, col_sharded[w]), part),
                    _rows_part(o_refs[w].at[k], part), (cx, cy, c)))
        return copies, copies

    if into is None:
        return _Side(tuple(pairs), out_shapes, {}, 3 * nw, build)
    return _Side((*pairs, *into), out_shapes, {nw + w: w for w in range(nw)}, 3 * nw, build)


def _join_halves(halves, col_sharded):
    nw = len(halves)

    def build(_, refs, send_sems, recv_sems, sem0):
        x, y, c, _ = _place()
        sends, recvs = [], []
        for w in range(nw):
            mine, theirs = _half(refs[w], c, col_sharded[w]), _half(refs[w], 1 - c, col_sharded[w])
            sends.append(_copy(send_sems, recv_sems, sem0 + w, mine, mine, (x, y, 1 - c)))
            recvs.append(_copy(send_sems, recv_sems, sem0 + w, theirs, theirs, (x, y, 1 - c)))
        return sends, recvs

    return _Side(*_in_place(halves), nw, build)


def _part_rows(size):
    return -(-size // (SUBLANES * LANES)) * SUBLANES


def _pack(arrays, pad_rows_to=SUBLANES):
    flat = [jnp.pad(a.reshape(-1), (0, _part_rows(a.size) * LANES - a.size)).reshape(-1, LANES) for a in arrays]
    rows = sum(f.shape[0] for f in flat)
    pad = (-rows) % pad_rows_to
    if pad:
        flat.append(jnp.zeros((pad, LANES), F32))
    return jnp.concatenate(flat, axis=0)


def _unpack(packed, shapes):
    out, r = [], 0
    for shp in shapes:
        size = math.prod(shp)
        out.append(packed[r:r + _part_rows(size)].reshape(-1)[:size].reshape(shp))
        r += _part_rows(size)
    return out


def _rope_tables(n_ctx_rows, s):
    rows = s // GRID_W
    row_idx = jnp.repeat(jnp.arange(rows), GRID_W)
    col_idx = jnp.tile(jnp.arange(GRID_W), rows)
    n_freq = LANES // 4
    inv_freq = ROPE_THETA ** (-jnp.arange(n_freq, dtype=F32) / n_freq)
    ang = jnp.concatenate([row_idx.astype(F32)[:, None] * inv_freq, col_idx.astype(F32)[:, None] * inv_freq], axis=-1)
    cos = jnp.repeat(jnp.cos(ang), 2, axis=-1)
    sin = jnp.repeat(jnp.sin(ang), 2, axis=-1) * jnp.tile(jnp.array([-1.0, 1.0], F32), LANES // 2)
    cos = jnp.concatenate([jnp.ones((n_ctx_rows, LANES), F32), cos], axis=0)
    sin = jnp.concatenate([jnp.zeros((n_ctx_rows, LANES), F32), sin], axis=0)
    return cos, sin


WEIGHT_NAMES = ['c_ctx', 'w_mod', 'b_mod', 'g_mix', 'g_mlp', 'w_in', 'q_gain', 'k_gain', 'conv_w', 'conv_b', 'w_rg',
                'b_rg', 'w_ig', 'b_ig', 'lru_lambda', 'w_o_attn', 'w_o_rnn', 'w_out', 'w_up', 'w_down', 'g_final']
BIG = ['w_in', 'w_o_attn', 'w_o_rnn', 'w_out', 'w_up', 'w_down']
BIG_COL_SHARDED = [True, False, False, False, True, False]
GATES = ['w_rg', 'w_ig']
SMALL = ['c_ctx', 'b_mod', 'g_mix', 'g_mlp', 'q_gain', 'k_gain', 'conv_b', 'g_final',
         'conv_w', 'b_rg', 'b_ig', 'lru_lambda']


def kernel(x, c, ctx, c_ctx, w_mod, b_mod, g_mix, g_mlp, w_in, q_gain, k_gain, conv_w, conv_b, w_rg, b_rg, w_ig, b_ig, lru_lambda, w_o_attn, w_o_rnn, w_out, w_up, w_down, g_final, loss_target, m_c_ctx, m_w_mod, m_b_mod, m_g_mix, m_g_mlp, m_w_in, m_q_gain, m_k_gain, m_conv_w, m_conv_b, m_w_rg, m_b_rg, m_w_ig, m_b_ig, m_lru_lambda, m_w_o_attn, m_w_o_rnn, m_w_out, m_w_up, m_w_down, m_g_final, v_c_ctx, v_w_mod, v_b_mod, v_g_mix, v_g_mlp, v_w_in, v_q_gain, v_k_gain, v_conv_w, v_conv_b, v_w_rg, v_b_rg, v_w_ig, v_b_ig, v_lru_lambda, v_w_o_attn, v_w_o_rnn, v_w_out, v_w_up, v_w_down, v_g_final):
    given = dict(locals())
    weights = {n: given[n] for n in WEIGHT_NAMES}
    moms = {n: given["m_" + n] for n in WEIGHT_NAMES}
    vars_ = {n: given["v_" + n] for n in WEIGHT_NAMES}

    s, d = x.shape[1], x.shape[2]
    n_ctx = ctx.shape[1]
    t = n_ctx + s
    hd = q_gain.shape[1]
    assert hd == LANES and w_rg.shape[-1] == LANES
    attn_w = w_o_attn.shape[1] * N_CHIPS
    n_in = w_in.shape[2] * N_CHIPS
    kv_w = (n_in - attn_w - 4 * d) // 2
    group = attn_w // kv_w
    k_off, v_off, xr_off = attn_w, attn_w + kv_w, attn_w + 2 * kv_w
    xg_off, gl_off = xr_off + d, xr_off + 2 * d
    d_mod = N_MOD * d
    tr = _tile(math.gcd(n_ctx, s), 256, 16)
    tcol = _tile(math.gcd(d, xr_off), 1024, LANES)
    xi, yi, ci = lax.axis_index("x"), lax.axis_index("y"), lax.axis_index("c")
    chip = 2 * xi + yi
    core = ci.astype(jnp.int32).reshape(1)

    sharded_small = [conv_w[0], b_rg[0], b_ig[0], lru_lambda[0]]
    pack0 = _pack([c[0]] + sharded_small)
    got0 = _all_gather8(pack0, "gather_small_inputs").reshape(N_DEV, -1, LANES)
    c_all = got0[:, :_part_rows(d)].reshape(N_DEV, -1)[:, :d]
    per_chip = [_unpack(got0[2 * j, _part_rows(d):], [a.shape for a in sharded_small]) for j in range(N_CHIPS)]
    conv_w_f, b_rg_f, b_ig_f, lam_f = (jnp.concatenate([per_chip[j][i] for j in range(N_CHIPS)], axis=-1)
                                       for i in range(4))
    c16 = jnp.concatenate([c_all, c_ctx[None, :], jnp.zeros((16 - N_DEV - 1, d), F32)], axis=0)
    b_mod_shard = lax.dynamic_slice(b_mod, (0, chip * (d_mod // N_CHIPS)), (1, d_mod // N_CHIPS))
    mod_part, silu16 = _mod_fwd(c16, w_mod[0], b_mod_shard)
    mod_all = _all_gather8(mod_part, "gather_mod").reshape(N_DEV, 16, d_mod // N_CHIPS)
    mod16 = jnp.concatenate([mod_all[2 * j] for j in range(N_CHIPS)], axis=-1)
    me = 4 * xi + 2 * yi + ci
    mod_lat = lax.dynamic_slice(mod16, (me, 0), (1, d_mod)).reshape(N_MOD, d)
    mod_ctx = mod16[N_DEV].reshape(N_MOD, d)
    mod4 = jnp.stack([mod_ctx[0], mod_ctx[1], mod_lat[0], mod_lat[1]])
    mod3 = jnp.stack([mod_lat[2], mod_lat[3], mod_lat[4]])
    gate_f = mod_lat[5][None, :]

    chip_arr = chip.astype(jnp.int32).reshape(1)
    own = {n: _cast_into_window(weights[n][0], chip_arr, col, "cast_" + n) for n, col in zip(BIG, BIG_COL_SHARDED)}
    place = jnp.stack([chip, ci]).astype(jnp.int32)
    row3 = [False] * 3

    def pair_sum(n, full, other, col):
        return _add_half(full, other, core, col, "pair_sum_" + n)

    def chip_sum(n, pair, got, col):
        return _sum_regions(pair, got, place, col, "chip_sum_" + n)

    (w_in_f,) = _exchange(_gather_neighbours([own['w_in']], [True]), "gather_w_in_nbr")
    (w_in_f,) = _exchange(_gather_forward([w_in_f], [True], siblings=True), "gather_w_in_fwd")
    (w_in_f,) = _exchange(_gather_diagonal_d2d(w_in_f, True), "gather_w_in_diag")
    cos, sin = _rope_tables(n_ctx, s)
    h = _norm_mod_fwd(ctx[0], x[0], g_mix, mod4, tr)
    z, (w_oa_f, w_or_f, w_out_f, w_up_f) = _matmul(h, w_in_f, name="mm_in", side=_sides(
        _gather_neighbours([own['w_o_attn'], own['w_o_rnn'], own['w_out']], row3),
        _gather_neighbours([own['w_up']], [True], (0, 1, 2))))
    qr, (w_oa_f, w_or_f, w_out_f) = _head_prep_fwd(z, 0, attn_w // LANES, q_gain, cos, sin, tr, "q_prep",
                                                   side=_gather_forward([w_oa_f, w_or_f, w_out_f], row3))
    kr = _head_prep_fwd(z, k_off, kv_w // LANES, k_gain, cos, sin, tr, "k_prep")
    (attn_o, lse), (w_up_f, w_down_f, w_oa_f, w_or_f, w_out_f) = _attn_fwd(
        qr, kr, z, v_off, n_ctx, group, tr, side=_sides(
            _on_same(_gather_neighbours([w_up_f], [True], (1, 2, 2)), _gather_forward([w_up_f], [True], ways=(True, False))),
            _gather_neighbours([own['w_down']], [False]), _gather_d2d([w_oa_f, w_or_f, w_out_f], row3)))
    (xc, a_f, bx_f, a_b, bx_b), (w_up_f, w_down_f) = _rnn_prep(
        z, xr_off, conv_w_f, conv_b, w_rg[0], b_rg_f, w_ig[0], b_ig_f, lam_f, n_ctx,
        side=_sides(_gather_forward([w_up_f], [True], ways=(False, True)), _gather_forward([w_down_f], [False])))
    (h_f, h_b), (w_up_f,) = _scan([(a_f, bx_f, "ctx_lat_up"), (a_b, bx_b, "ctx_lat_down")], post=False, n_ctx_rows=n_ctx,
                                  name="scan_fwd", side=_gather_d2d([w_up_f], [True]))
    u = _rnn_gate_fwd(h_f, h_b, z, xg_off, n_ctx, tr, tcol)
    y_attn = _matmul(attn_o, w_oa_f, name="mm_o_attn")
    y_rnn = _matmul(u, w_or_f, name="mm_o_rnn")
    mrg = _merge_fwd(y_attn, y_rnn, z, gl_off, n_ctx, tr, tcol)
    mix = _matmul(mrg, w_out_f, name="mm_out")
    x1, h2 = _resid_norm_mod_fwd(x[0], mix, g_mlp, mod3, tr)
    up, (w_down_f,) = _matmul(h2, w_up_f, name="mm_up", side=_gather_d2d([w_down_f], [False]))
    act = _sq_relu(up, tr, tcol)
    down = _matmul(act, w_down_f, name="mm_down")
    dx2, d_down, sums_fin, loss_blk = _final_fwd_bwd(x1, down, loss_target[0], g_final[None, :], gate_f, tr)

    d_up = _matmul(d_down, w_down_f, tb=True, out_dtype=BF16, name="mm_d_up",
                   post=lambda d_act, up_: d_act * 2.0 * jnp.maximum(up_, 0.0), post_args=(up,))
    g_w_down = _matmul(act, d_down, ta=True, out_dtype=BF16, name="mm_g_down")

    def scatter(pairs, cols, lo, hi, into=None):
        return _scatter_regions(pairs, cols, (lo, hi, 8), into)

    dh2, (got,) = _matmul(d_up, w_up_f, tb=True, name="mm_d_h2", side=_swap_halves([g_w_down], [False]))
    p_down = pair_sum('w_down', g_w_down, got, False)
    g_w_up, got_down = _matmul(h2, d_up, ta=True, out_dtype=BF16, name="mm_g_up", side=scatter([p_down], [False], 0, 4))
    (dx1, d_mix, sums2), (got,) = _norm_mod_bwd2(x1, dh2, dx2, mix, g_mlp, mod3, tr, side=_swap_halves([g_w_up], [True]))
    p_up = pair_sum('w_up', g_w_up, got, True)
    d_mrg = _matmul(d_mix, w_out_f, tb=True, name="mm_d_mrg")
    g_w_out = _matmul(mrg, d_mix, ta=True, out_dtype=BF16, name="mm_g_out")
    dz = _dz_start(t, n_in, n_ctx)
    d_ya, dz = _merge_bwd(d_mrg, y_attn, z, gl_off, dz, n_ctx, tr, tcol, "merge_bwd_attn")
    d_yr, dz = _merge_bwd(d_mrg, y_rnn, z, gl_off + d, dz, n_ctx, tr, tcol, "merge_bwd_rnn")
    d_o = _matmul(d_ya, w_oa_f, tb=True, out_dtype=BF16, name="mm_d_o")
    g_w_oa = _matmul(attn_o, d_ya, ta=True, out_dtype=BF16, name="mm_g_o_attn")
    d_u = _matmul(d_yr, w_or_f, tb=True, name="mm_d_u")
    g_w_or = _matmul(u, d_yr, ta=True, out_dtype=BF16, name="mm_g_o_rnn")
    d_rnn, dz = _rnn_gate_bwd(d_u, h_f, h_b, z, xg_off, dz, n_ctx, tr, tcol)
    gs_f, gs_b = _scan([(a_f, d_rnn, "lat_ctx_down"), (a_b, d_rnn, "lat_ctx_up")], post=True, n_ctx_rows=n_ctx,
                       name="scan_bwd")
    o_names, o_grads = ['w_o_attn', 'w_o_rnn', 'w_out'], [g_w_oa, g_w_or, g_w_out]
    (dz, g_w_rg, g_w_ig, sums_rnn), (got_down, *got_o) = _rnn_bwd(
        z, xr_off, xc, gs_f, gs_b, h_f, h_b, conv_w_f, w_rg[0], b_rg_f, w_ig[0], b_ig_f, lam_f, dz, n_ctx,
        side=_sides(scatter([p_down], [False], 4, 8, got_down), _swap_halves(o_grads, row3)))
    hs_down = chip_sum('w_down', p_down, got_down, False)
    p_o = [pair_sum(n, g, o, False) for n, g, o in zip(o_names, o_grads, got_o)]
    gate_cols = 8 * LANES if g_w_rg.size % (8 * LANES * N_CHIPS * 16) == 0 else 2 * LANES
    gate_rows = g_w_rg.size // gate_cols
    gate_grads = [g_w_rg.reshape(gate_rows, gate_cols), g_w_ig.reshape(gate_rows, gate_cols)]
    (dq, dk, dz), (got_up, got_oa, got_rg, got_ig, gs_down) = _attn_bwd(
        qr, kr, z, v_off, d_o, attn_o, lse, dz, n_ctx, group, tr, side=_sides(
            _scatter_regions([p_up], [True]), _scatter_regions(p_o[:1], row3[:1]), _swap_halves(gate_grads, row3[:2]),
            _join_halves([hs_down], [False])))
    hs_up = chip_sum('w_up', p_up, got_up, True)
    hs_oa = chip_sum('w_o_attn', p_o[0], got_oa, False)
    p_gate = [pair_sum(n, g, o, False) for n, g, o in zip(GATES, gate_grads, (got_rg, got_ig))]
    dz, g_q_gain = _head_prep_bwd(z, 0, attn_w // LANES, q_gain, cos, sin, dq, n_ctx, tr, dz, "q_prep_bwd")
    dz, g_k_gain = _head_prep_bwd(z, k_off, kv_w // LANES, k_gain, cos, sin, dk, 0, tr, dz, "k_prep_bwd")
    g_w_in, (got_or, got_out, got_rg, got_ig, gs_up, gs_oa) = _matmul(
        h, dz, ta=True, out_dtype=BF16, name="mm_g_in", side=_sides(
            _scatter_regions(p_o[1:] + p_gate, [False] * 4), _join_halves([hs_up], [True]), _join_halves([hs_oa], [False])))
    hs_late = [chip_sum(n, p, o, False)
               for n, p, o in zip(o_names[1:] + GATES, p_o[1:] + p_gate, (got_or, got_out, got_rg, got_ig))]
    (got,) = _exchange(_swap_halves([g_w_in], [True]), "swap_w_in")
    p_in = pair_sum('w_in', g_w_in, got, True)
    dh, (got_in, fold_in) = _matmul(dz, w_in_f, tb=True, name="mm_d_h", side=_scatter_forward_1(p_in))
    nbr_chips = jnp.stack([2 * (1 - xi) + yi, 2 * xi + 1 - yi]).astype(jnp.int32)
    passed = _fold_forwarded(p_in, fold_in, nbr_chips, "fold_w_in")
    (got_in,) = _exchange(_scatter_forward_2(passed, got_in), "scatter_w_in_fwd")
    grad_x, sums1 = _norm_mod_bwd1(ctx[0], x[0], dh, dx1, g_mix, mod4, tr)

    zeros_d = jnp.zeros((d,), F32)
    dmod_lat = jnp.concatenate([sums1[0], sums1[1], sums2[3], sums2[0], sums2[1], sums_fin[1]])
    dmod_ctx = jnp.concatenate([sums1[3], sums1[4]] + [zeros_d] * 4)
    small_parts = [dmod_lat, dmod_ctx, loss_blk[0, 0:1], sums1[2] + sums1[5], sums2[2], g_q_gain[0], g_k_gain[0],
                   sums_rnn[10], sums_fin[0], sums_rnn[6:10], sums_rnn[0:2], sums_rnn[2:4], sums_rnn[4:6]]
    pack1 = _pack(small_parts)
    got1 = _all_gather8(pack1, "gather_small_grads").reshape(N_DEV, -1, LANES)
    tot1 = _sum_leading(got1, "sum_small_grads")
    part_shapes = [a.shape for a in small_parts]
    (s_dmod_lat, s_dmod_ctx, s_loss, g_g_mix, g_g_mlp, g_q_gain, g_k_gain, g_conv_b, g_g_final,
     g_conv_w_f, g_b_rg_f, g_b_ig_f, g_lam_f) = _unpack(tot1, part_shapes)
    loss = s_loss[0]
    g_b_mod = (s_dmod_lat + s_dmod_ctx)[None, :]
    n_mod_rows = _part_rows(d_mod)
    dmod16 = jnp.concatenate([got1[:, :n_mod_rows].reshape(N_DEV, -1)[:, :d_mod], s_dmod_ctx[None, :],
                              jnp.zeros((16 - N_DEV - 1, d_mod), F32)], axis=0)
    dmod16_shard = lax.dynamic_slice(dmod16, (0, chip * (d_mod // N_CHIPS)), (16, d_mod // N_CHIPS))
    g_w_mod = _matmul(silu16, dmod16_shard, ta=True, name="mm_g_mod")
    dsilu_part = _matmul(dmod16_shard[N_DEV:], w_mod[0], tb=True, name="mm_d_silu")
    dsilu_all = _all_gather8(dsilu_part, "gather_d_silu").reshape(N_DEV, 8, d)
    g_c_ctx = _c_ctx_grad(dsilu_all, c_ctx[None, :])[0]

    def shard_of(full):
        w = full.shape[-1] // N_CHIPS
        return lax.dynamic_slice(full, (0, chip * w), (full.shape[0], w))

    grads = {
        'c_ctx': g_c_ctx, 'b_mod': g_b_mod, 'g_mix': g_g_mix[None, :], 'g_mlp': g_g_mlp[None, :],
        'q_gain': g_q_gain[None, :], 'k_gain': g_k_gain[None, :], 'conv_b': g_conv_b[None, :],
        'g_final': g_g_final,
        'conv_w': shard_of(g_conv_w_f)[None], 'b_rg': shard_of(g_b_rg_f)[None], 'b_ig': shard_of(g_b_ig_f)[None],
        'lru_lambda': shard_of(g_lam_f)[None], 'w_mod': g_w_mod[None],
    }

    delta, new_m, new_v = {}, {}, {}

    def adamw(n):
        shp = weights[n].shape
        as2d = (lambda a: a[0]) if n not in GATES else (lambda a: a.reshape(-1, LANES))
        dl, nm, nv, g = _adamw(as2d(weights[n]), as2d(grads[n]), as2d(moms[n]), as2d(vars_[n]), "adamw_" + n)
        delta[n], new_m[n], new_v[n], grads[n] = dl.reshape(shp), nm.reshape(shp), nv.reshape(shp), g.reshape(shp)

    hs_in = chip_sum('w_in', p_in, got_in, True)
    gs_in, gs_or, gs_out = _exchange(_join_halves([hs_in] + hs_late[:2], [True, False, False]), "join_rest")
    for n, g in zip(['w_in', 'w_o_attn', 'w_o_rnn', 'w_out', 'w_up', 'w_down'], [gs_in, gs_oa, gs_or, gs_out, gs_up, gs_down]):
        grads[n] = g[None]
    half_cols = gate_cols // 2
    mine = jnp.concatenate([lax.dynamic_slice(hs, (0, ci * half_cols), (gate_rows // N_CHIPS, half_cols))
                            for hs in hs_late[2:]], axis=0)
    gate_all = _all_gather8(mine, "gather_gate_grads")
    gate_all = gate_all.reshape(N_CHIPS, 2, len(GATES), gate_rows // N_CHIPS, half_cols)
    for i, n in enumerate(GATES):
        grads[n] = jnp.moveaxis(gate_all[:, :, i], 1, 2).reshape(weights[n].shape)
    for n in ['w_mod'] + BIG + GATES:
        adamw(n)
    small_shapes = [weights[n].shape for n in SMALL]
    packed = [_pack([src[n] for n in SMALL], 512) for src in (weights, grads, moms, vars_)]
    outs = _adamw(*packed, "adamw_small")
    for res, out in zip((delta, new_m, new_v), outs):
        for n, a in zip(SMALL, _unpack(out, small_shapes)):
            res[n] = a
    return (loss, grad_x[None], *[grads[n] for n in WEIGHT_NAMES], *[delta[n] for n in WEIGHT_NAMES],
            *[new_m[n] for n in WEIGHT_NAMES], *[new_v[n] for n in WEIGHT_NAMES])
```

```python
import functools
import math
from typing import Callable, NamedTuple

import jax
import jax.numpy as jnp
from jax import lax
from jax.experimental import pallas as pl
from jax.experimental.pallas import tpu as pltpu

F32 = jnp.float32
BF16 = jnp.bfloat16
MESH_ID = pl.DeviceIdType.MESH
ANY = pl.BlockSpec(memory_space=pl.ANY)

NORM_EPS = 1e-6
LRU_C = 8.0
GRID_W = 64
ROPE_THETA = 10000.0
N_MOD = 6
CONV_WIDTH = 4
ADAM_LR = 0.001
ADAM_B1 = 0.9
ADAM_B2 = 0.999
ADAM_EPS = 1e-08
ADAM_WD = 0.01
ADAM_STEP = 10

LANES = 128
SUBLANES = 8
V7X_VMEM_LIMIT = 48 * 1024 * 1024
WIDE_TILE = 11 * LANES
N_CHIPS = 4
N_DEV = 8
GELU_C = math.sqrt(2.0 / math.pi)
GELU_A = 0.044715


def _tile(dim, pref, align):
    t = min(pref, dim)
    t -= t % align
    while t >= align:
        if dim % t == 0:
            return t
        t -= align
    return dim


def _params(*sem):
    return pltpu.CompilerParams(dimension_semantics=sem, vmem_limit_bytes=V7X_VMEM_LIMIT)


def _sds(shape, dtype=F32):
    return jax.ShapeDtypeStruct(shape, dtype)


class _Side(NamedTuple):
    operands: tuple
    results: tuple
    aliases: dict
    n_sems: int
    build: Callable


def _sides(*sides):
    ops, res, aliases, spans, n = [], [], {}, [], 0
    for s in sides:
        spans.append((len(ops), len(res), n))
        aliases.update({len(ops) + i: len(res) + j for i, j in s.aliases.items()})
        ops += s.operands
        res += s.results
        n += s.n_sems

    def build(op_refs, res_refs, send_sems, recv_sems, sem0):
        sends, recvs = [], []
        for s, (o, r, k) in zip(sides, spans):
            a, b = s.build(op_refs[o:o + len(s.operands)], res_refs[r:r + len(s.results)], send_sems, recv_sems,
                           sem0 + k)
            sends += a
            recvs += b
        return sends, recvs

    return _Side(tuple(ops), tuple(res), aliases, n, build)


def _on_same(*sides):
    def build(ops, res, send_sems, recv_sems, sem0):
        sends, recvs = [], []
        for s in sides:
            a, b = s.build(ops, res, send_sems, recv_sems, sem0)
            sends += a
            recvs += b
            sem0 += s.n_sems
        return sends, recvs

    return _Side(sides[0].operands, sides[0].results, sides[0].aliases, sum(s.n_sems for s in sides), build)


def _call(body, *, side=None, sem=(), grid=(), in_specs=(), out_specs=(), out_shape=(), scratch_shapes=(), **kw):
    if side is None:
        return pl.pallas_call(body, grid=grid, in_specs=list(in_specs), out_specs=out_specs, out_shape=out_shape,
                              scratch_shapes=list(scratch_shapes), compiler_params=_params(*sem), **kw)
    aliases = kw.pop("input_output_aliases", {})
    many = isinstance(out_shape, (list, tuple))
    out_specs_l, out_shape_l = (list(out_specs), list(out_shape)) if many else ([out_specs], [out_shape])
    n_in, n_out, n_scr = len(in_specs), len(out_shape_l), len(scratch_shapes)
    n_op, n_res = len(side.operands), len(side.results)

    def hosted(*refs):
        ins, ops = refs[:n_in], refs[n_in:n_in + n_op]
        outs = refs[n_in + n_op:n_in + n_op + n_out]
        res = refs[n_in + n_op + n_out:n_in + n_op + n_out + n_res]
        scr = refs[n_in + n_op + n_out + n_res:-2]
        send_sems, recv_sems = refs[-2:]

        def start():
            for cp in side.build(ops, res, send_sems, recv_sems, 0)[0]:
                cp.start()

        def finish():
            sends, recvs = side.build(ops, res, send_sems, recv_sems, 0)
            for cp in recvs:
                cp.wait_recv()
            for cp in sends:
                cp.wait_send()

        if not grid:
            start()
            finish()
            return
        ids = [pl.program_id(a) for a in range(len(grid))]
        first = functools.reduce(jnp.logical_and, [i == 0 for i in ids])
        last = functools.reduce(jnp.logical_and, [i == g - 1 for i, g in zip(ids, grid)])
        pl.when(first)(start)
        body(*ins, *outs, *scr)
        pl.when(last)(finish)

    def run(*args):
        got = pl.pallas_call(
            hosted, grid=grid, in_specs=[*in_specs, *[ANY] * n_op], out_specs=[*out_specs_l, *[ANY] * n_res],
            out_shape=[*out_shape_l, *side.results],
            scratch_shapes=[*scratch_shapes, pltpu.SemaphoreType.DMA((side.n_sems,)),
                            pltpu.SemaphoreType.DMA((side.n_sems,))],
            input_output_aliases={**aliases, **{n_in + i: n_out + j for i, j in side.aliases.items()}},
            compiler_params=_params(*["arbitrary"] * len(grid)), **kw)(*args, *side.operands)
        own = list(got[:n_out]) if many else got[0]
        return own, list(got[n_out:])

    return run


def _matmul(a, b, *, ta=False, tb=False, out_dtype=F32, name, tm=1024, tn=1024, tk=2816, side=None, post=None,
            post_args=()):
    k_dim, m = a.shape if ta else a.shape[::-1]
    n, k2 = b.shape if tb else b.shape[::-1]
    assert k_dim == k2, (a.shape, b.shape, ta, tb)
    tm = _tile(m, tm, LANES if ta else 16)
    tn = _tile(n, tn, 16 if tb else LANES)
    tk = _tile(k_dim, tk, LANES)
    nk = k_dim // tk
    dims = (((0 if ta else 1,), (1 if tb else 0,)), ((), ()))
    if nk == 1:
        def whole(a_ref, b_ref, *rest):
            acc = lax.dot_general(a_ref[...].astype(BF16), b_ref[...].astype(BF16), dims, preferred_element_type=F32)
            if post is not None:
                acc = post(acc, *[r[...] for r in rest[:-1]])
            rest[-1][...] = acc.astype(rest[-1].dtype)

        a_spec = pl.BlockSpec((tk, tm), lambda i, j: (0, i)) if ta else pl.BlockSpec((tm, tk), lambda i, j: (i, 0))
        b_spec = pl.BlockSpec((tn, tk), lambda i, j: (j, 0)) if tb else pl.BlockSpec((tk, tn), lambda i, j: (0, j))
        o_spec = pl.BlockSpec((tm, tn), lambda i, j: (i, j))
        return _call(
            whole, side=side, name=name, grid=(m // tm, n // tn), in_specs=[a_spec, b_spec] + [o_spec] * len(post_args),
            out_specs=o_spec, out_shape=_sds((m, n), out_dtype), sem=("parallel", "parallel"),
        )(a, b, *post_args)
    assert post is None

    def body(a_ref, b_ref, o_ref, acc_ref):
        k = pl.program_id(2)

        @pl.when(k == 0)
        def _():
            acc_ref[...] = jnp.zeros_like(acc_ref)

        acc_ref[...] += lax.dot_general(a_ref[...].astype(BF16), b_ref[...].astype(BF16), dims,
                                        preferred_element_type=F32)

        @pl.when(k == nk - 1)
        def _():
            o_ref[...] = acc_ref[...].astype(o_ref.dtype)

    a_spec = pl.BlockSpec((tk, tm), lambda i, j, k: (k, i)) if ta else pl.BlockSpec((tm, tk), lambda i, j, k: (i, k))
    b_spec = pl.BlockSpec((tn, tk), lambda i, j, k: (j, k)) if tb else pl.BlockSpec((tk, tn), lambda i, j, k: (k, j))
    return _call(
        body, side=side, name=name, grid=(m // tm, n // tn, nk), in_specs=[a_spec, b_spec],
        out_specs=pl.BlockSpec((tm, tn), lambda i, j, k: (i, j)), out_shape=_sds((m, n), out_dtype),
        scratch_shapes=[pltpu.VMEM((tm, tn), F32)], sem=("parallel", "parallel", "arbitrary"),
    )(a, b)


def _silu(x):
    return x * jax.nn.sigmoid(x)


def _gelu(x):
    return 0.5 * x * (1.0 + jnp.tanh(GELU_C * (x + GELU_A * x * x * x)))


def _gelu_grad(x):
    t = jnp.tanh(GELU_C * (x + GELU_A * x * x * x))
    return 0.5 * (1.0 + t) + 0.5 * x * (1.0 - t * t) * GELU_C * (1.0 + 3.0 * GELU_A * x * x)


def _expm1_nonpos(x):
    series = x * (1.0 + x * (1.0 / 2 + x * (1.0 / 6 + x * (1.0 / 24 + x * (1.0 / 120 + x * (1.0 / 720 + x / 5040))))))
    return jnp.where(x > -0.25, series, jnp.exp(x) - 1.0)


def _softplus(x):
    return jnp.maximum(x, 0.0) + jnp.log1p(jnp.exp(-jnp.abs(x)))


def _rms_stats(x):
    return lax.rsqrt(jnp.mean(x * x, axis=-1, keepdims=True) + NORM_EPS)


def _rms_bwd(dxhat, xhat, rstd):
    return rstd * (dxhat - xhat * jnp.mean(dxhat * xhat, axis=-1, keepdims=True))


def _colsum(v):
    return jnp.sum(v, axis=0, keepdims=True)


def _mod_fwd(c16, w_mod, b_mod_shard):
    r, d = c16.shape
    n = w_mod.shape[1]
    tn = _tile(n, 512, LANES)

    def body(c_ref, w_ref, b_ref, o_ref, s_ref):
        s = _silu(c_ref[...])
        s_ref[...] = s
        o_ref[...] = jnp.dot(s.astype(BF16), w_ref[...].astype(BF16), preferred_element_type=F32) + b_ref[...]

    return pl.pallas_call(
        body, name="mod_fwd", grid=(n // tn,),
        in_specs=[pl.BlockSpec((r, d), lambda j: (0, 0)), pl.BlockSpec((d, tn), lambda j: (0, j)),
                  pl.BlockSpec((1, tn), lambda j: (0, j))],
        out_specs=[pl.BlockSpec((r, tn), lambda j: (0, j)), pl.BlockSpec((r, d), lambda j: (0, 0))],
        out_shape=[_sds((r, n)), _sds((r, d))], compiler_params=_params("arbitrary"),
    )(c16, w_mod, b_mod_shard)


def _c_ctx_grad(parts, c_ctx_row):
    d = c_ctx_row.shape[1]

    def body(p_ref, c_ref, o_ref):
        tot = p_ref[0, 0:1, :]
        for chip in range(1, N_CHIPS):
            tot = tot + p_ref[2 * chip, 0:1, :]
        c = c_ref[...]
        sg = jax.nn.sigmoid(c)
        o_ref[...] = tot * (sg * (1.0 + c * (1.0 - sg)))

    return pl.pallas_call(body, name="c_ctx_grad", out_shape=_sds((1, d)), compiler_params=_params())(parts, c_ctx_row)


def _token_specs(n_ctx_rows, d, tr):
    nctx = n_ctx_rows // tr
    return (pl.BlockSpec((tr, d), lambda i: (jnp.minimum(i, nctx - 1), 0)),
            pl.BlockSpec((tr, d), lambda i: (jnp.maximum(i - nctx, 0), 0)))


def _norm_mod_fwd(ctx, x, g, mod4, tr):
    (n_ctx_rows, d), s = ctx.shape, x.shape[0]
    t = n_ctx_rows + s
    nctx = n_ctx_rows // tr

    def body(c_ref, x_ref, g_ref, mod_ref, h_ref):
        is_ctx = pl.program_id(0) < nctx
        x = jnp.where(is_ctx, c_ref[...], x_ref[...])
        n = x * _rms_stats(x) * g_ref[...]
        sh = jnp.where(is_ctx, mod_ref[0:1, :], mod_ref[2:3, :])
        sc = jnp.where(is_ctx, mod_ref[1:2, :], mod_ref[3:4, :])
        h_ref[...] = (n * (1.0 + sc) + sh).astype(BF16)

    return pl.pallas_call(
        body, name="norm_mod_fwd", grid=(t // tr,),
        in_specs=[*_token_specs(n_ctx_rows, d, tr), pl.BlockSpec((1, d), lambda i: (0, 0)),
                  pl.BlockSpec((4, d), lambda i: (0, 0))],
        out_specs=pl.BlockSpec((tr, d), lambda i: (i, 0)), out_shape=_sds((t, d), BF16),
        compiler_params=_params("parallel"),
    )(ctx, x, g, mod4)


def _norm_mod_bwd1(ctx, x, dh, dx1, g, mod4, tr, side=None):
    (n_ctx_rows, d), s = ctx.shape, x.shape[0]
    t = n_ctx_rows + s
    nctx = n_ctx_rows // tr

    def body(c_ref, x_ref, dh_ref, dx1_ref, g_ref, mod_ref, dx_ref, sums_ref):
        i = pl.program_id(0)
        is_ctx = i < nctx

        @pl.when(i == 0)
        def _():
            sums_ref[...] = jnp.zeros_like(sums_ref)

        x = jnp.where(is_ctx, c_ref[...], x_ref[...])
        dh_ = dh_ref[...]
        rstd = _rms_stats(x)
        xhat = x * rstd
        gg = g_ref[...]
        sc = jnp.where(is_ctx, mod_ref[1:2, :], mod_ref[3:4, :])
        dxhat = dh_ * (1.0 + sc) * gg
        dx_ref[...] = dx1_ref[...] + _rms_bwd(dxhat, xhat, rstd)
        part = [_colsum(dh_), _colsum(dh_ * xhat * gg), _colsum(dh_ * (1.0 + sc) * xhat)]

        @pl.when(is_ctx)
        def _():
            for k, row in enumerate(part):
                sums_ref[3 + k:4 + k, :] += row

        @pl.when(jnp.logical_not(is_ctx))
        def _():
            for k, row in enumerate(part):
                sums_ref[k:k + 1, :] += row

    lat = lambda i: (jnp.maximum(i - nctx, 0), 0)
    return _call(
        body, side=side, name="norm_mod_bwd1", grid=(t // tr,),
        in_specs=[*_token_specs(n_ctx_rows, d, tr), pl.BlockSpec((tr, d), lambda i: (i, 0)),
                  pl.BlockSpec((tr, d), lat), pl.BlockSpec((1, d), lambda i: (0, 0)),
                  pl.BlockSpec((4, d), lambda i: (0, 0))],
        out_specs=[pl.BlockSpec((tr, d), lat), pl.BlockSpec((8, d), lambda i: (0, 0))],
        out_shape=[_sds((s, d)), _sds((8, d))], sem=("arbitrary",),
    )(ctx, x, dh, dx1, g, mod4)


def _resid_norm_mod_fwd(x, mix, g, mod3, tr):
    s, d = x.shape

    def body(x_ref, mix_ref, g_ref, mod_ref, x1_ref, h_ref):
        x1 = x_ref[...] + mod_ref[0:1, :] * mix_ref[...]
        x1_ref[...] = x1
        n = x1 * _rms_stats(x1) * g_ref[...]
        h_ref[...] = (n * (1.0 + mod_ref[2:3, :]) + mod_ref[1:2, :]).astype(BF16)

    row = pl.BlockSpec((tr, d), lambda i: (i, 0))
    return pl.pallas_call(
        body, name="resid_norm_mod_fwd", grid=(s // tr,),
        in_specs=[row, row, pl.BlockSpec((1, d), lambda i: (0, 0)), pl.BlockSpec((3, d), lambda i: (0, 0))],
        out_specs=[row, row], out_shape=[_sds((s, d)), _sds((s, d), BF16)], compiler_params=_params("parallel"),
    )(x, mix, g, mod3)


def _norm_mod_bwd2(x1, dh2, dx2, mix, g, mod3, tr, side=None):
    s, d = x1.shape

    def body(x_ref, dh_ref, dx2_ref, mix_ref, g_ref, mod_ref, dx1_ref, dmix_ref, sums_ref):
        @pl.when(pl.program_id(0) == 0)
        def _():
            sums_ref[...] = jnp.zeros_like(sums_ref)

        x = x_ref[...]
        dh_ = dh_ref[...]
        rstd = _rms_stats(x)
        xhat = x * rstd
        gg = g_ref[...]
        sc = mod_ref[2:3, :]
        dx1 = dx2_ref[...] + _rms_bwd(dh_ * (1.0 + sc) * gg, xhat, rstd)
        dx1_ref[...] = dx1
        dmix_ref[...] = (dx1 * mod_ref[0:1, :]).astype(BF16)
        part = [_colsum(dh_), _colsum(dh_ * xhat * gg), _colsum(dh_ * (1.0 + sc) * xhat), _colsum(dx1 * mix_ref[...])]
        for k, row in enumerate(part):
            sums_ref[k:k + 1, :] += row

    row = pl.BlockSpec((tr, d), lambda i: (i, 0))
    return _call(
        body, side=side, name="norm_mod_bwd2", grid=(s // tr,),
        in_specs=[row, row, row, row, pl.BlockSpec((1, d), lambda i: (0, 0)), pl.BlockSpec((3, d), lambda i: (0, 0))],
        out_specs=[row, row, pl.BlockSpec((8, d), lambda i: (0, 0))],
        out_shape=[_sds((s, d)), _sds((s, d), BF16), _sds((8, d))], sem=("arbitrary",),
    )(x1, dh2, dx2, mix, g, mod3)


def _final_fwd_bwd(x1, down, target, g_final, gate, tr):
    s, d = x1.shape

    def body(x1_ref, down_ref, t_ref, g_ref, gate_ref, dx2_ref, ddown_ref, sums_ref, loss_ref):
        @pl.when(pl.program_id(0) == 0)
        def _():
            sums_ref[...] = jnp.zeros_like(sums_ref)
            loss_ref[...] = jnp.zeros_like(loss_ref)

        down_ = down_ref[...]
        gate_ = gate_ref[...]
        x2 = x1_ref[...] + gate_ * down_
        rstd = _rms_stats(x2)
        xhat = x2 * rstd
        gg = g_ref[...]
        err = xhat * gg - t_ref[...]
        loss_ref[...] += 0.5 * jnp.sum(jnp.mean(err * err, axis=-1, keepdims=True))
        dy = err * (1.0 / d)
        dx2 = _rms_bwd(dy * gg, xhat, rstd)
        dx2_ref[...] = dx2
        ddown_ref[...] = (dx2 * gate_).astype(BF16)
        sums_ref[0:1, :] += _colsum(dy * xhat)
        sums_ref[1:2, :] += _colsum(dx2 * down_)

    row = pl.BlockSpec((tr, d), lambda i: (i, 0))
    vec = pl.BlockSpec((1, d), lambda i: (0, 0))
    return pl.pallas_call(
        body, name="final_fwd_bwd", grid=(s // tr,), in_specs=[row, row, row, vec, vec],
        out_specs=[row, row, pl.BlockSpec((8, d), lambda i: (0, 0)), pl.BlockSpec((8, LANES), lambda i: (0, 0))],
        out_shape=[_sds((s, d)), _sds((s, d), BF16), _sds((8, d)), _sds((8, LANES))],
        compiler_params=_params("arbitrary"),
    )(x1, down, target, g_final, gate)


def _swap_pairs(v):
    lane = lax.broadcasted_iota(jnp.int32, v.shape, 1)
    return jnp.where(lane % 2 == 0, pltpu.roll(v, LANES - 1, 1), pltpu.roll(v, 1, 1))


def _head_prep_fwd(z, col_off, n_heads, gain, cos, sin, tr, name, side=None):
    t = z.shape[0]
    per = math.gcd(4, n_heads, col_off // LANES)
    w = per * LANES
    hb = col_off // w

    def body(z_ref, g_ref, cos_ref, sin_ref, o_ref):
        for hh in range(per):
            cols = slice(hh * LANES, (hh + 1) * LANES)
            x = z_ref[:, cols]
            y = x * _rms_stats(x) * g_ref[...]
            o_ref[:, cols] = (y * cos_ref[...] + _swap_pairs(y) * sin_ref[...]).astype(BF16)

    tab = pl.BlockSpec((tr, LANES), lambda i, j: (i, 0))
    return _call(
        body, side=side, name=name, grid=(t // tr, n_heads // per),
        in_specs=[pl.BlockSpec((tr, w), lambda i, j: (i, hb + j)), pl.BlockSpec((1, LANES), lambda i, j: (0, 0)),
                  tab, tab],
        out_specs=pl.BlockSpec((tr, w), lambda i, j: (i, j)), out_shape=_sds((t, n_heads * LANES), BF16),
        sem=("parallel", "parallel"),
    )(z, gain, cos, sin)


def _dz_start(t, n_in, n_ctx_rows):
    tcol = _tile(n_in, 1024, LANES)

    def body(o_ref):
        o_ref[...] = jnp.zeros_like(o_ref)

    return pl.pallas_call(body, name="dz_start", grid=(n_in // tcol,),
                          out_specs=pl.BlockSpec((n_ctx_rows, tcol), lambda j: (0, j)), out_shape=_sds((t, n_in), BF16),
                          compiler_params=_params("parallel"))()


def _head_prep_bwd(z, col_off, n_heads, gain, cos, sin, dout, row_off, tr, dz, name, side=None):
    r = dout.shape[0]
    per = math.gcd(4, n_heads, col_off // LANES)
    w = per * LANES
    hb = col_off // w
    rb = row_off // tr

    def body(z_ref, g_ref, cos_ref, sin_ref, d_ref, _, dz_ref, dg_ref):
        @pl.when(jnp.logical_and(pl.program_id(0) == 0, pl.program_id(1) == 0))
        def _():
            dg_ref[...] = jnp.zeros_like(dg_ref)

        for hh in range(per):
            cols = slice(hh * LANES, (hh + 1) * LANES)
            x = z_ref[:, cols]
            rstd = _rms_stats(x)
            xhat = x * rstd
            dd = d_ref[:, cols]
            dy = dd * cos_ref[...] - _swap_pairs(dd) * sin_ref[...]
            dg_ref[0:1, :] += _colsum(dy * xhat)
            dz_ref[:, cols] = _rms_bwd(dy * g_ref[...], xhat, rstd).astype(BF16)

    tab = pl.BlockSpec((tr, LANES), lambda i, j: (rb + i, 0))
    window = pl.BlockSpec((tr, w), lambda i, j: (rb + i, hb + j))
    return _call(
        body, side=side, name=name, grid=(r // tr, n_heads // per),
        in_specs=[window, pl.BlockSpec((1, LANES), lambda i, j: (0, 0)), tab, tab,
                  pl.BlockSpec((tr, w), lambda i, j: (i, j)), ANY],
        out_specs=[window, pl.BlockSpec((8, LANES), lambda i, j: (0, 0))],
        out_shape=[_sds(dz.shape, BF16), _sds((8, LANES))], input_output_aliases={5: 0}, sem=("arbitrary", "arbitrary"),
    )(z, gain, cos, sin, dout, dz)


def _attn_fwd(qr, kr, z, v_off, n_ctx_rows, group, tq, side=None):
    t, kvw = kr.shape
    s = t - n_ctx_rows
    n_kv = kvw // LANES
    scale = LANES ** -0.5
    qb0 = n_ctx_rows // tq
    vb = v_off // LANES

    def body(q_ref, k_ref, v_ref, o_ref, lse_ref):
        k = k_ref[...]
        v = v_ref[...].astype(BF16)
        lse_ref[...] = jnp.zeros_like(lse_ref)
        for g in range(group):
            cols = slice(g * LANES, (g + 1) * LANES)
            sc = lax.dot_general(q_ref[:, cols], k, (((1,), (1,)), ((), ())), preferred_element_type=F32) * scale
            m = jnp.max(sc, axis=-1, keepdims=True)
            e = jnp.exp(sc - m)
            l = jnp.sum(e, axis=-1, keepdims=True)
            p = e * (1.0 / l)
            o_ref[:, cols] = jnp.dot(p.astype(BF16), v, preferred_element_type=F32).astype(BF16)
            lse_ref[:, g:g + 1] = m + jnp.log(l)

    return _call(
        body, side=side, name="attn_fwd", grid=(n_kv, s // tq),
        in_specs=[pl.BlockSpec((tq, group * LANES), lambda h, i: (qb0 + i, h)),
                  pl.BlockSpec((t, LANES), lambda h, i: (0, h)), pl.BlockSpec((t, LANES), lambda h, i: (0, vb + h))],
        out_specs=[pl.BlockSpec((tq, group * LANES), lambda h, i: (i, h)), pl.BlockSpec((tq, LANES), lambda h, i: (i, h))],
        out_shape=[_sds((s, n_kv * group * LANES), BF16), _sds((s, kvw))], sem=("parallel", "parallel"),
    )(qr, kr, z)


def _attn_bwd(qr, kr, z, v_off, d_o, attn_o, lse, dz, n_ctx_rows, group, tq, side=None):
    t, kvw = kr.shape
    s = t - n_ctx_rows
    n_kv = kvw // LANES
    scale = LANES ** -0.5
    qb0 = n_ctx_rows // tq
    vb = v_off // LANES
    n_q_blocks = s // tq
    tn_dims = (((0,), (0,)), ((), ()))
    nt_dims = (((1,), (1,)), ((), ()))

    def body(q_ref, k_ref, v_ref, do_ref, o_ref, lse_ref, _, dq_ref, dk_ref, dz_ref, dv_ref):
        @pl.when(pl.program_id(1) == 0)
        def _():
            dk_ref[...] = jnp.zeros_like(dk_ref)
            dv_ref[...] = jnp.zeros_like(dv_ref)

        k = k_ref[...]
        v = v_ref[...].astype(BF16)
        for g in range(group):
            cols = slice(g * LANES, (g + 1) * LANES)
            q = q_ref[:, cols]
            do_ = do_ref[:, cols]
            row_dot = jnp.sum(do_.astype(F32) * o_ref[:, cols].astype(F32), axis=-1, keepdims=True)
            sc = lax.dot_general(q, k, nt_dims, preferred_element_type=F32)
            p = jnp.exp(sc * scale - lse_ref[:, g:g + 1])
            dv_ref[...] += lax.dot_general(p.astype(BF16), do_, tn_dims, preferred_element_type=F32)
            dp = lax.dot_general(do_, v, nt_dims, preferred_element_type=F32)
            ds = (p * (dp - row_dot)).astype(BF16)
            dq_ref[:, cols] = jnp.dot(ds, k, preferred_element_type=F32) * scale
            dk_ref[...] += lax.dot_general(ds, q, tn_dims, preferred_element_type=F32)

        @pl.when(pl.program_id(1) == n_q_blocks - 1)
        def _():
            dk_ref[...] = dk_ref[...] * scale
            dz_ref[...] = dv_ref[...].astype(BF16)

    qspec = pl.BlockSpec((tq, group * LANES), lambda h, i: (qb0 + i, h))
    ospec = pl.BlockSpec((tq, group * LANES), lambda h, i: (i, h))
    kspec = pl.BlockSpec((t, LANES), lambda h, i: (0, h))
    vspec = pl.BlockSpec((t, LANES), lambda h, i: (0, vb + h))
    return _call(
        body, side=side, name="attn_bwd", grid=(n_kv, n_q_blocks),
        in_specs=[qspec, kspec, vspec, ospec, ospec, pl.BlockSpec((tq, LANES), lambda h, i: (i, h)), ANY],
        out_specs=[ospec, kspec, vspec], out_shape=[_sds((s, n_kv * group * LANES)), _sds((t, kvw)), _sds(dz.shape, BF16)],
        scratch_shapes=[pltpu.VMEM((t, LANES), F32)], input_output_aliases={6: 2}, sem=("parallel", "arbitrary"),
    )(qr, kr, z, d_o, attn_o, lse, dz)


def _row_mask(shape, rows):
    r = lax.broadcasted_iota(jnp.int32, shape, 0)
    m = r == rows[0]
    for v in rows[1:]:
        m = jnp.logical_or(m, r == v)
    return m


def _shift_rows(x, k, n_ctx_rows):
    t = x.shape[0]
    if k == 0:
        return x
    rolled = pltpu.roll(x, (-k) % t, 0)
    if k > 0:
        dead = [n_ctx_rows - 1 - i for i in range(k)] + [t - 1 - i for i in range(k)]
    else:
        dead = [i for i in range(-k)] + [n_ctx_rows + i for i in range(-k)]
    return jnp.where(_row_mask(x.shape, dead), 0.0, rolled)


def _conv(x, w, b, n_ctx_rows):
    y = b
    for k in range(CONV_WIDTH):
        y = y + _shift_rows(x, k - 1, n_ctx_rows) * w[k:k + 1, :]
    return y


def _gates(xc_bf, w_r, b_r, w_i, b_i, lam):
    r = jax.nn.sigmoid(jnp.dot(xc_bf, w_r.astype(BF16), preferred_element_type=F32) + b_r)
    i = jax.nn.sigmoid(jnp.dot(xc_bf, w_i.astype(BF16), preferred_element_type=F32) + b_i)
    log_a = -LRU_C * r * _softplus(-lam)
    a = jnp.exp(log_a)
    mult = jnp.sqrt(-_expm1_nonpos(2.0 * log_a))
    return r, i, a, mult


def _rnn_specs(t, xr_off):
    xb = xr_off // LANES
    return dict(
        zcol=pl.BlockSpec((t, LANES), lambda j: (0, xb + j)), col=pl.BlockSpec((t, LANES), lambda j: (0, j)),
        conv_w=pl.BlockSpec((CONV_WIDTH, LANES), lambda j: (0, j)), vec=pl.BlockSpec((1, LANES), lambda j: (0, j)),
        gate_w=pl.BlockSpec((2, 1, LANES, LANES), lambda j: (0, j, 0, 0)), two=pl.BlockSpec((2, LANES), lambda j: (0, j)))


def _rnn_prep(z, xr_off, conv_w, conv_b, w_rg, b_rg, w_ig, b_ig, lam, n_ctx_rows, side=None):
    t = z.shape[0]
    d = conv_b.shape[1]
    sp = _rnn_specs(t, xr_off)

    def body(z_ref, cw_ref, cb_ref, wr_ref, br_ref, wi_ref, bi_ref, lam_ref, xc_ref, af_ref, bf_ref, ab_ref, bb_ref):
        xc = _conv(z_ref[...], cw_ref[...], cb_ref[...], n_ctx_rows)
        xc_ref[...] = xc
        xc_bf = xc.astype(BF16)
        for dr, (a_ref, b_ref) in enumerate(((af_ref, bf_ref), (ab_ref, bb_ref))):
            _, i, a, mult = _gates(xc_bf, wr_ref[dr, 0], br_ref[dr:dr + 1, :], wi_ref[dr, 0], bi_ref[dr:dr + 1, :],
                                   lam_ref[dr:dr + 1, :])
            a_ref[...] = a
            b_ref[...] = mult * (i * xc)

    return _call(
        body, side=side, name="rnn_prep", grid=(d // LANES,),
        in_specs=[sp["zcol"], sp["conv_w"], sp["vec"], sp["gate_w"], sp["two"], sp["gate_w"], sp["two"], sp["two"]],
        out_specs=[sp["col"]] * 5, out_shape=[_sds((t, d))] * 5, sem=("parallel",),
    )(z, conv_w, conv_b, w_rg, b_rg, w_ig, b_ig, lam)


def _scan(chains, *, post, n_ctx_rows, name, tc=256, side=None):
    t, d = chains[0][0].shape
    tc = _tile(math.gcd(n_ctx_rows, t - n_ctx_rows), tc, SUBLANES)
    nt, nctx = t // tc, n_ctx_rows // tc
    nlat = nt - nctx
    nc = len(chains)
    lat_only = [b.shape[0] != t for _, b, _ in chains]
    ups = [order.endswith("up") for _, _, order in chains]

    def chunk_of(order):
        def chunk(i):
            if order == "ctx_lat_up":
                return i
            if order == "lat_ctx_down":
                return nt - 1 - i
            if order == "ctx_lat_down":
                return jnp.where(i < nctx, nctx - 1 - i, nt - 1 - (i - nctx))
            return jnp.where(i < nlat, nctx + i, i - nlat)
        return chunk

    chunks = [chunk_of(order) for _, _, order in chains]

    def body(*refs):
        ab_refs, o_refs, carry_ref = refs[:2 * nc], refs[2 * nc:3 * nc], refs[3 * nc]

        @pl.when(pl.program_id(0) == 0)
        def _():
            carry_ref[...] = jnp.zeros_like(carry_ref)

        live = [jnp.where(chunks[n](pl.program_id(0)) >= nctx, 1.0, 0.0) if lat_only[n] else None for n in range(nc)]

        def group(gi, carries):
            carries = list(carries)
            bases = [pl.multiple_of((gi if ups[n] else tc // SUBLANES - 1 - gi) * SUBLANES, SUBLANES) for n in range(nc)]
            for step in range(SUBLANES):
                for n in range(nc):
                    row = bases[n] + (step if ups[n] else SUBLANES - 1 - step)
                    a_r = ab_refs[2 * n][pl.ds(row, 1), :]
                    b_r = ab_refs[2 * n + 1][pl.ds(row, 1), :]
                    if live[n] is not None:
                        b_r = b_r * live[n]
                    if post:
                        out = b_r + carries[n]
                        carries[n] = a_r * out
                    else:
                        out = a_r * carries[n] + b_r
                        carries[n] = out
                    o_refs[n][pl.ds(row, 1), :] = out
            return tuple(carries)

        done = lax.fori_loop(0, tc // SUBLANES, group, tuple(carry_ref[n:n + 1, :] for n in range(nc)))
        for n in range(nc):
            carry_ref[n:n + 1, :] = done[n]

    in_specs, out_specs, args = [], [], []
    for n, (a, b, _) in enumerate(chains):
        full = pl.BlockSpec((tc, d), lambda i, n=n: (chunks[n](i), 0))
        lat = pl.BlockSpec((tc, d), lambda i, n=n: (jnp.maximum(chunks[n](i) - nctx, 0), 0))
        in_specs += [full, lat if lat_only[n] else full]
        out_specs.append(full)
        args += [a, b]
    return _call(
        body, side=side, name=name, grid=(nt,), in_specs=in_specs, out_specs=out_specs, out_shape=[_sds((t, d))] * nc,
        scratch_shapes=[pltpu.VMEM((SUBLANES, d), F32)], sem=("arbitrary",),
    )(*args)


def _rnn_bwd(z, xr_off, xc, g_f, g_b, h_f, h_b, conv_w, w_rg, b_rg, w_ig, b_ig, lam, dz, n_ctx_rows, side=None):
    t, d = xc.shape
    sp = _rnn_specs(t, xr_off)
    tn_dims = (((0,), (0,)), ((), ()))
    nt_dims = (((1,), (1,)), ((), ()))

    def body(z_ref, xc_ref, gf_ref, gb_ref, hf_ref, hb_ref, cw_ref, wr_ref, br_ref, wi_ref, bi_ref, lam_ref, _,
             dxr_ref, dwr_ref, dwi_ref, sums_ref):
        xc_ = xc_ref[...]
        xc_bf = xc_.astype(BF16)
        dxc = jnp.zeros_like(xc_)
        sums = [None] * 6
        for dr, (g_ref, h_ref) in enumerate(((gf_ref, hf_ref), (gb_ref, hb_ref))):
            w_r, w_i, lam_ = wr_ref[dr, 0], wi_ref[dr, 0], lam_ref[dr:dr + 1, :]
            r, i, a, mult = _gates(xc_bf, w_r, br_ref[dr:dr + 1, :], w_i, bi_ref[dr:dr + 1, :], lam_)
            g = g_ref[...]
            h = h_ref[...]
            if dr == 0:
                h_prev = jnp.where(_row_mask(h.shape, [0]), 0.0, pltpu.roll(h, 1, 0))
            else:
                h_prev = jnp.where(_row_mask(h.shape, [n_ctx_rows - 1]), 0.0, pltpu.roll(h, t - 1, 0))
            d_mult = g * i * xc_
            d_i = g * mult * xc_
            dxc = dxc + g * mult * i
            d_log_a = g * h_prev * a - d_mult * a * a / mult
            sp_ = _softplus(-lam_)
            d_r = d_log_a * (-LRU_C) * sp_
            d_sp = _colsum(d_log_a * (-LRU_C) * r)
            du_r = (d_r * r * (1.0 - r))
            du_i = (d_i * i * (1.0 - i))
            sums[dr] = _colsum(du_r)
            sums[2 + dr] = _colsum(du_i)
            sums[4 + dr] = d_sp * (-jax.nn.sigmoid(-lam_))
            du_r_bf, du_i_bf = du_r.astype(BF16), du_i.astype(BF16)
            dwr_ref[dr, 0] = lax.dot_general(xc_bf, du_r_bf, tn_dims, preferred_element_type=F32).astype(BF16)
            dwi_ref[dr, 0] = lax.dot_general(xc_bf, du_i_bf, tn_dims, preferred_element_type=F32).astype(BF16)
            dxc = dxc + lax.dot_general(du_r_bf, w_r.astype(BF16), nt_dims, preferred_element_type=F32)
            dxc = dxc + lax.dot_general(du_i_bf, w_i.astype(BF16), nt_dims, preferred_element_type=F32)
        xr = z_ref[...]
        cw = cw_ref[...]
        dxr = jnp.zeros_like(dxc)
        rows = list(sums)
        for k in range(CONV_WIDTH):
            dxr = dxr + _shift_rows(dxc, 1 - k, n_ctx_rows) * cw[k:k + 1, :]
            rows.append(_colsum(dxc * _shift_rows(xr, k - 1, n_ctx_rows)))
        rows.append(_colsum(dxc))
        dxr_ref[...] = dxr.astype(BF16)
        sums_ref[...] = jnp.zeros_like(sums_ref)
        for k, row in enumerate(rows):
            sums_ref[k:k + 1, :] = row

    return _call(
        body, side=side, name="rnn_bwd", grid=(d // LANES,),
        in_specs=[sp["zcol"]] + [sp["col"]] * 5 + [sp["conv_w"], sp["gate_w"], sp["two"], sp["gate_w"], sp["two"],
                                                  sp["two"], ANY],
        out_specs=[sp["zcol"], sp["gate_w"], sp["gate_w"], pl.BlockSpec((16, LANES), lambda j: (0, j))],
        out_shape=[_sds(dz.shape, BF16), _sds(w_rg.shape, BF16), _sds(w_ig.shape, BF16), _sds((16, d))],
        input_output_aliases={12: 0}, sem=("parallel",),
    )(z, xc, g_f, g_b, h_f, h_b, conv_w, w_rg, b_rg, w_ig, b_ig, lam, dz)


def _tiles2d(s, d, tr, tcol):
    return (s // tr, d // tcol), pl.BlockSpec((tr, tcol), lambda i, j: (i, j))


def _zspec(tr, tcol, row_off, col_off):
    rb, cb = row_off // tr, col_off // tcol
    return pl.BlockSpec((tr, tcol), lambda i, j: (rb + i, cb + j))


def _rnn_gate_fwd(h_f, h_b, z, xg_off, n_ctx_rows, tr, tcol, side=None):
    t, d = h_f.shape
    s = t - n_ctx_rows
    grid, out = _tiles2d(s, d, tr, tcol)
    hs = _zspec(tr, tcol, n_ctx_rows, 0)

    def body(hf_ref, hb_ref, xg_ref, u_ref):
        u_ref[...] = ((hf_ref[...] + hb_ref[...]) * _gelu(xg_ref[...])).astype(BF16)

    return _call(body, side=side, name="rnn_gate_fwd", grid=grid,
                 in_specs=[hs, hs, _zspec(tr, tcol, n_ctx_rows, xg_off)], out_specs=out, out_shape=_sds((s, d), BF16),
                 sem=("parallel", "parallel"))(h_f, h_b, z)


def _rnn_gate_bwd(d_u, h_f, h_b, z, xg_off, dz, n_ctx_rows, tr, tcol, side=None):
    t, d = h_f.shape
    s = t - n_ctx_rows
    grid, out = _tiles2d(s, d, tr, tcol)
    hs = _zspec(tr, tcol, n_ctx_rows, 0)
    window = _zspec(tr, tcol, n_ctx_rows, xg_off)

    def body(du_ref, hf_ref, hb_ref, xg_ref, _, dr_ref, dxg_ref):
        du = du_ref[...]
        xg = xg_ref[...]
        dr_ref[...] = du * _gelu(xg)
        dxg_ref[...] = (du * (hf_ref[...] + hb_ref[...]) * _gelu_grad(xg)).astype(BF16)

    return _call(body, side=side, name="rnn_gate_bwd", grid=grid, in_specs=[out, hs, hs, window, ANY],
                 out_specs=[out, window], out_shape=[_sds((s, d)), _sds(dz.shape, BF16)], input_output_aliases={4: 1},
                 sem=("parallel", "parallel"))(d_u, h_f, h_b, z, dz)


def _merge_fwd(y_attn, y_rnn, z, gl_off, n_ctx_rows, tr, tcol):
    s, d = y_attn.shape
    grid, out = _tiles2d(s, d, tr, tcol)

    def body(ya_ref, yr_ref, ga_ref, gr_ref, o_ref):
        o_ref[...] = (jax.nn.sigmoid(ga_ref[...]) * ya_ref[...] + jax.nn.sigmoid(gr_ref[...]) * yr_ref[...]).astype(BF16)

    return pl.pallas_call(
        body, name="merge_fwd", grid=grid,
        in_specs=[out, out, _zspec(tr, tcol, n_ctx_rows, gl_off), _zspec(tr, tcol, n_ctx_rows, gl_off + d)],
        out_specs=out, out_shape=_sds((s, d), BF16), compiler_params=_params("parallel", "parallel"),
    )(y_attn, y_rnn, z, z)


def _merge_bwd(d_mrg, y, z, gl_off, dz, n_ctx_rows, tr, tcol, name):
    s, d = y.shape
    grid, out = _tiles2d(s, d, tr, tcol)
    window = _zspec(tr, tcol, n_ctx_rows, gl_off)

    def body(dm_ref, y_ref, gl_ref, _, dy_ref, dgl_ref):
        dm = dm_ref[...]
        g = jax.nn.sigmoid(gl_ref[...])
        dy_ref[...] = (dm * g).astype(BF16)
        dgl_ref[...] = (dm * y_ref[...] * g * (1.0 - g)).astype(BF16)

    return _call(body, name=name, grid=grid, in_specs=[out, out, window, ANY], out_specs=[out, window],
                 out_shape=[_sds((s, d), BF16), _sds(dz.shape, BF16)], input_output_aliases={3: 1},
                 sem=("parallel", "parallel"))(d_mrg, y, z, dz)


def _sq_relu(up, tr, tcol, side=None):
    grid, out = _tiles2d(*up.shape, _tile(up.shape[0], 2 * tr, 16), _tile(up.shape[1], 4 * tcol, LANES))

    def body(u_ref, o_ref):
        r = jnp.maximum(u_ref[...], 0.0)
        o_ref[...] = (r * r).astype(BF16)

    return _call(body, side=side, name="sq_relu", grid=grid, in_specs=[out], out_specs=out,
                 out_shape=_sds(up.shape, BF16), sem=("parallel", "parallel"))(up)


def _cast_into_window(w, chip, col_sharded, name):
    r, c = w.shape
    tr, tcol = _tile(r, 512, 16), _tile(c, WIDE_TILE, LANES)
    nrb, ncb = r // tr, c // tcol

    def body(chip_ref, w_ref, o_ref):
        o_ref[...] = w_ref[...].astype(BF16)

    if col_sharded:
        omap = lambda i, j, chip_ref: (i, chip_ref[0] * ncb + j)
    else:
        omap = lambda i, j, chip_ref: (chip_ref[0] * nrb + i, j)
    return pl.pallas_call(
        body, name=name,
        grid_spec=pltpu.PrefetchScalarGridSpec(
            num_scalar_prefetch=1, grid=(nrb, ncb),
            in_specs=[pl.BlockSpec((tr, tcol), lambda i, j, chip_ref: (i, j))], out_specs=pl.BlockSpec((tr, tcol), omap)),
        out_shape=_sds((r, c * N_CHIPS) if col_sharded else (r * N_CHIPS, c), BF16),
        compiler_params=_params("parallel", "parallel"),
    )(chip, w)


def _sum_leading(parts, name):
    n, r, c = parts.shape
    tr, tcol = _tile(r, 512, SUBLANES), _tile(c, 1024, LANES)

    def body(p_ref, o_ref):
        tot = p_ref[0]
        for k in range(1, n):
            tot = tot + p_ref[k]
        o_ref[...] = tot

    return pl.pallas_call(
        body, name=name, grid=(r // tr, c // tcol), in_specs=[pl.BlockSpec((n, tr, tcol), lambda i, j: (0, i, j))],
        out_specs=pl.BlockSpec((tr, tcol), lambda i, j: (i, j)), out_shape=_sds((r, c)),
        compiler_params=_params("parallel", "parallel"),
    )(parts)


def _add_half(full, other, core, split_rows, name):
    r, c = other.shape
    tr, tcol = _tile(r, 512, 16), _tile(c, 1024, LANES)
    nrb, ncb = r // tr, c // tcol

    def body(core_ref, f_ref, o_ref, out_ref):
        out_ref[...] = (f_ref[...].astype(F32) + o_ref[...].astype(F32)).astype(out_ref.dtype)

    if split_rows:
        fmap = lambda i, j, core_ref: (core_ref[0] * nrb + i, j)
    else:
        fmap = lambda i, j, core_ref: (i, core_ref[0] * ncb + j)
    same = lambda i, j, core_ref: (i, j)
    return pl.pallas_call(
        body, name=name,
        grid_spec=pltpu.PrefetchScalarGridSpec(
            num_scalar_prefetch=1, grid=(nrb, ncb),
            in_specs=[pl.BlockSpec((tr, tcol), fmap), pl.BlockSpec((tr, tcol), same)],
            out_specs=pl.BlockSpec((tr, tcol), same)),
        out_shape=_sds((r, c), BF16), compiler_params=_params("parallel", "parallel"),
    )(core, full, other)


def _sum_regions(pair, got, place, col_sharded, name):
    n_got, r, c = got.shape
    tr, tcol = _tile(r, 512, 16), _tile(c, WIDE_TILE, LANES)
    nrb, ncb = r // tr, c // tcol

    def body(place_ref, p_ref, g_ref, out_ref):
        tot = p_ref[...].astype(F32)
        for k in range(n_got):
            tot = tot + g_ref[k].astype(F32)
        out_ref[...] = tot

    if col_sharded:
        pmap = lambda i, j, pr: (i, pr[0] * ncb + j)
        omap = lambda i, j, pr: (pr[1] * nrb + i, j)
        out_shape = (2 * r, c)
    else:
        pmap = lambda i, j, pr: (pr[0] * nrb + i, j)
        omap = lambda i, j, pr: (i, pr[1] * ncb + j)
        out_shape = (r, 2 * c)
    return pl.pallas_call(
        body, name=name,
        grid_spec=pltpu.PrefetchScalarGridSpec(
            num_scalar_prefetch=1, grid=(nrb, ncb),
            in_specs=[pl.BlockSpec((tr, tcol), pmap), pl.BlockSpec((n_got, tr, tcol), lambda i, j, pr: (0, i, j))],
            out_specs=pl.BlockSpec((tr, tcol), omap)),
        out_shape=_sds(out_shape), compiler_params=_params("parallel", "parallel"),
    )(place, pair, got)


def _fold_forwarded(pair, fold, nbr_chips, name):
    _, r, c = fold.shape
    tr, tcol = _tile(r, 512, 16), _tile(c, WIDE_TILE, LANES)
    nrb, ncb = r // tr, c // tcol

    def body(nbr_ref, px_ref, py_ref, f_ref, out_ref):
        out_ref[0] = (px_ref[...].astype(F32) + f_ref[1].astype(F32)).astype(out_ref.dtype)
        out_ref[1] = (py_ref[...].astype(F32) + f_ref[0].astype(F32)).astype(out_ref.dtype)

    both = pl.BlockSpec((2, tr, tcol), lambda i, j, nb: (0, i, j))
    return pl.pallas_call(
        body, name=name,
        grid_spec=pltpu.PrefetchScalarGridSpec(
            num_scalar_prefetch=1, grid=(nrb, ncb),
            in_specs=[pl.BlockSpec((tr, tcol), lambda i, j, nb: (i, nb[0] * ncb + j)),
                      pl.BlockSpec((tr, tcol), lambda i, j, nb: (nrb + i, nb[1] * ncb + j)), both],
            out_specs=both),
        out_shape=_sds(fold.shape, fold.dtype), compiler_params=_params("parallel", "parallel"),
    )(nbr_chips, pair, pair, fold)


def _adamw(w, g, m, v, name):
    r, c = w.shape
    tr, tcol = _tile(r, 256, SUBLANES), _tile(c, WIDE_TILE, LANES)
    blk = pl.BlockSpec((tr, tcol), lambda i, j: (i, j))

    def body(w_ref, g_ref, m_ref, v_ref, d_ref, nm_ref, nv_ref, g_out_ref):
        g_ = g_ref[...]
        g_out_ref[...] = g_
        m_ = ADAM_B1 * m_ref[...] + (1.0 - ADAM_B1) * g_
        v_ = ADAM_B2 * v_ref[...] + (1.0 - ADAM_B2) * (g_ * g_)
        m_hat = m_ / (1.0 - ADAM_B1 ** ADAM_STEP)
        v_hat = v_ / (1.0 - ADAM_B2 ** ADAM_STEP)
        d_ref[...] = -ADAM_LR * (m_hat / (jnp.sqrt(v_hat) + ADAM_EPS) + ADAM_WD * w_ref[...])
        nm_ref[...] = m_
        nv_ref[...] = v_

    return _call(body, name=name, grid=(r // tr, c // tcol), in_specs=[blk] * 4, out_specs=[blk] * 4,
                 out_shape=[_sds((r, c))] * 4, sem=("parallel", "parallel"))(w, g, m, v)


def _place():
    x, y, c = lax.axis_index("x"), lax.axis_index("y"), lax.axis_index("c")
    chips = [(1 - x, y), (x, 1 - y), (1 - x, 1 - y)]
    return x, y, c, chips


def _all_gather8(blk, name):
    m, n = blk.shape

    def body(x_ref, out_ref, send_sems, recv_sems, local_sem):
        x, y, c, chips = _place()
        me, sibling = (x, y, c), (x, y, 1 - c)

        def rows(px, py, pc):
            return out_ref.at[pl.ds((4 * px + 2 * py + pc) * m, m), :]

        def copy(k, block, to, src=None):
            return pltpu.make_async_remote_copy(
                src_ref=rows(*block) if src is None else src, dst_ref=rows(*block), send_sem=send_sems.at[k],
                recv_sem=recv_sems.at[k], device_id=to, device_id_type=MESH_ID)

        mine = pltpu.make_async_copy(x_ref, rows(*me), local_sem)
        mine.start()
        first = [copy(0, me, sibling, src=x_ref)]
        first += [copy(1 + j, me, (*chip, c), src=x_ref) for j, chip in enumerate(chips)]
        for cp in first:
            cp.start()
        passed = [copy(4 + j, (*chip, c), sibling) for j, chip in enumerate(chips)]
        for j, chip in enumerate(chips):
            copy(1 + j, (*chip, c), me).wait_recv()
            passed[j].start()
        copy(0, sibling, me).wait_recv()
        for j, chip in enumerate(chips):
            copy(4 + j, (*chip, 1 - c), me).wait_recv()
        for cp in first + passed:
            cp.wait_send()
        mine.wait()

    return pl.pallas_call(
        body, name=name, out_shape=_sds((N_DEV * m, n), blk.dtype), in_specs=[ANY], out_specs=ANY,
        scratch_shapes=[pltpu.SemaphoreType.DMA((7,)), pltpu.SemaphoreType.DMA((7,)), pltpu.SemaphoreType.DMA],
    )(blk)


def _half(ref, core, split_rows):
    r, c = ref.shape
    if split_rows:
        return ref.at[pl.ds(core * (r // 2), r // 2), :]
    return ref.at[:, pl.ds(core * (c // 2), c // 2)]


def _chip_block(ref, j, col_sharded):
    r, c = ref.shape
    if col_sharded:
        return ref.at[:, pl.ds(j * (c // N_CHIPS), c // N_CHIPS)]
    return ref.at[pl.ds(j * (r // N_CHIPS), r // N_CHIPS), :]


def _rows_part(ref, part):
    lo, hi, n = part
    r = ref.shape[0]
    return ref if (lo, hi) == (0, n) else ref.at[pl.ds(lo * (r // n), (hi - lo) * (r // n)), :]


def _copy(send_sems, recv_sems, k, src, dst, to):
    return pltpu.make_async_remote_copy(src_ref=src, dst_ref=dst, send_sem=send_sems.at[k], recv_sem=recv_sems.at[k],
                                        device_id=to, device_id_type=MESH_ID)


def _in_place(arrays):
    return tuple(arrays), tuple(_sds(a.shape, a.dtype) for a in arrays), {i: i for i in range(len(arrays))}


def _gather_ici(fulls, col_sharded, part=(0, 1, 1)):
    nw = len(fulls)

    def build(_, refs, send_sems, recv_sems, sem0):
        x, y, c, chips = _place()
        sends, recvs = [], []
        for w in range(nw):
            win = lambda j: _rows_part(_half(_chip_block(refs[w], j, col_sharded[w]), c, True), part)
            for k, (cx, cy) in enumerate(chips):
                sem = sem0 + 3 * w + k
                sends.append(_copy(send_sems, recv_sems, sem, win(2 * x + y), win(2 * x + y), (cx, cy, c)))
                recvs.append(_copy(send_sems, recv_sems, sem, win(2 * cx + cy), win(2 * cx + cy), (cx, cy, c)))
        return sends, recvs

    return _Side(*_in_place(fulls), 3 * nw, build)


def _gather_d2d(fulls, col_sharded):
    nw = len(fulls)

    def build(_, refs, send_sems, recv_sems, sem0):
        x, y, c, chips = _place()
        sends, recvs = [], []
        for w in range(nw):
            win = lambda j, core: _half(_chip_block(refs[w], j, col_sharded[w]), core, True)
            for k, (cx, cy) in enumerate(chips):
                sem = sem0 + 3 * w + k
                sends.append(_copy(send_sems, recv_sems, sem, win(2 * cx + cy, c), win(2 * cx + cy, c), (x, y, 1 - c)))
                recvs.append(_copy(send_sems, recv_sems, sem, win(2 * cx + cy, 1 - c), win(2 * cx + cy, 1 - c),
                                   (x, y, 1 - c)))
        return sends, recvs

    return _Side(*_in_place(fulls), 3 * nw, build)


def _gather_neighbours(fulls, col_sharded, part=(0, 1, 1)):
    nw = len(fulls)

    def build(_, refs, send_sems, recv_sems, sem0):
        x, y, c, chips = _place()
        sends, recvs = [], []
        for w in range(nw):
            win = lambda j: _rows_part(_half(_chip_block(refs[w], j, col_sharded[w]), c, True), part)
            for k, (cx, cy) in enumerate(chips[:2]):
                sem = sem0 + 2 * w + k
                sends.append(_copy(send_sems, recv_sems, sem, win(2 * x + y), win(2 * x + y), (cx, cy, c)))
                recvs.append(_copy(send_sems, recv_sems, sem, win(2 * cx + cy), win(2 * cx + cy), (cx, cy, c)))
        return sends, recvs

    return _Side(*_in_place(fulls), 2 * nw, build)


def _gather_forward(fulls, col_sharded, ways=(True, True), siblings=False):
    nw = len(fulls)

    def build(_, refs, send_sems, recv_sems, sem0):
        x, y, c, (cx_, cy_, cd_) = _place()
        sends, recvs = [], []
        for w in range(nw):
            win = lambda chip, core: _half(_chip_block(refs[w], 2 * chip[0] + chip[1], col_sharded[w]), core, True)
            part = lambda ref, p: _rows_part(ref, (p, p + 1, 2))
            for p, (src, to) in enumerate(((cx_, cy_), (cy_, cx_))):
                if ways[p]:
                    sem = sem0 + 4 * w + p
                    sends.append(_copy(send_sems, recv_sems, sem, part(win(src, c), p), part(win(src, c), p), (*to, c)))
                    recvs.append(_copy(send_sems, recv_sems, sem, part(win(cd_, c), p), part(win(cd_, c), p), (*to, c)))
            if siblings:
                for k, chip in enumerate((cx_, cy_)):
                    sem = sem0 + 4 * w + 2 + k
                    sends.append(_copy(send_sems, recv_sems, sem, win(chip, c), win(chip, c), (x, y, 1 - c)))
                    recvs.append(_copy(send_sems, recv_sems, sem, win(chip, 1 - c), win(chip, 1 - c), (x, y, 1 - c)))
        return sends, recvs

    return _Side(*_in_place(fulls), 4 * nw, build)


def _gather_diagonal_d2d(full, col_sharded):
    def build(_, refs, send_sems, recv_sems, sem0):
        x, y, c, chips = _place()
        cx, cy = chips[2]
        win = lambda core: _half(_chip_block(refs[0], 2 * cx + cy, col_sharded), core, True)
        return ([_copy(send_sems, recv_sems, sem0, win(c), win(c), (x, y, 1 - c))],
                [_copy(send_sems, recv_sems, sem0, win(1 - c), win(1 - c), (x, y, 1 - c))])

    return _Side(*_in_place([full]), 1, build)


def _scatter_forward_1(pair):
    r, n = pair.shape
    results = (_sds((2, r, n // N_CHIPS), pair.dtype), _sds((2, r // 2, n // N_CHIPS), pair.dtype))

    def build(refs, res, send_sems, recv_sems, sem0):
        x, y, c, (cx_, cy_, cd_) = _place()
        region = lambda chip, p: _rows_part(_chip_block(refs[0], 2 * chip[0] + chip[1], True), (p, p + 1, 2))
        got, fold = res
        copies = [
            _copy(send_sems, recv_sems, sem0, region(cx_, 1), _rows_part(got.at[0], (1, 2, 2)), (*cx_, c)),
            _copy(send_sems, recv_sems, sem0 + 1, region(cd_, 1), fold.at[0], (*cx_, c)),
            _copy(send_sems, recv_sems, sem0 + 2, region(cy_, 0), _rows_part(got.at[1], (0, 1, 2)), (*cy_, c)),
            _copy(send_sems, recv_sems, sem0 + 3, region(cd_, 0), fold.at[1], (*cy_, c))]
        return copies, copies

    return _Side((pair,), results, {}, 4, build)


def _scatter_forward_2(passed, got):
    def build(refs, res, send_sems, recv_sems, sem0):
        x, y, c, (cx_, cy_, _) = _place()
        copies = [_copy(send_sems, recv_sems, sem0, refs[0].at[0], _rows_part(res[0].at[0], (0, 1, 2)), (*cx_, c)),
                  _copy(send_sems, recv_sems, sem0 + 1, refs[0].at[1], _rows_part(res[0].at[1], (1, 2, 2)), (*cy_, c))]
        return copies, copies

    return _Side((passed, got), (_sds(got.shape, got.dtype),), {1: 0}, 2, build)


def _exchange(side, name):
    return _call(None, side=side, name=name)()[1]


def _swap_halves(grads, col_sharded):
    nw = len(grads)
    out_shapes = [_sds((g.shape[0] // 2, g.shape[1]) if col else (g.shape[0], g.shape[1] // 2), g.dtype)
                  for g, col in zip(grads, col_sharded)]

    def build(g_refs, o_refs, send_sems, recv_sems, sem0):
        x, y, c, _ = _place()
        copies = [_copy(send_sems, recv_sems, sem0 + w, _half(g_refs[w], 1 - c, col_sharded[w]), o_refs[w],
                        (x, y, 1 - c)) for w in range(nw)]
        return copies, copies

    return _Side(tuple(grads), tuple(out_shapes), {}, nw, build)


def _scatter_regions(pairs, col_sharded, part=(0, 1, 1), into=None):
    nw = len(pairs)

    def region_shape(p, col):
        return (p.shape[0], p.shape[1] // N_CHIPS) if col else (p.shape[0] // N_CHIPS, p.shape[1])

    out_shapes = tuple(_sds((N_CHIPS - 1, *region_shape(p, col)), p.dtype) for p, col in zip(pairs, col_sharded))

    def build(refs, o_refs, send_sems, recv_sems, sem0):
        x, y, c, chips = _place()
        copies = []
        for w in range(nw):
            for k, (cx, cy) in enumerate(chips):
                copies.append(_copy(
                    send_sems, recv_sems, sem0 + 3 * w + k,
                    _rows_part(_chip_block(refs[w], 2 * cx + cy, col_sharded[w]), part),
                    _rows_part(o_refs[w].at[k], part), (cx, cy, c)))
        return copies, copies

    if into is None:
        return _Side(tuple(pairs), out_shapes, {}, 3 * nw, build)
    return _Side((*pairs, *into), out_shapes, {nw + w: w for w in range(nw)}, 3 * nw, build)


def _join_halves(halves, col_sharded):
    nw = len(halves)

    def build(_, refs, send_sems, recv_sems, sem0):
        x, y, c, _ = _place()
        sends, recvs = [], []
        for w in range(nw):
            mine, theirs = _half(refs[w], c, col_sharded[w]), _half(refs[w], 1 - c, col_sharded[w])
            sends.append(_copy(send_sems, recv_sems, sem0 + w, mine, mine, (x, y, 1 - c)))
            recvs.append(_copy(send_sems, recv_sems, sem0 + w, theirs, theirs, (x, y, 1 - c)))
        return sends, recvs

    return _Side(*_in_place(halves), nw, build)


def _part_rows(size):
    return -(-size // (SUBLANES * LANES)) * SUBLANES


def _pack(arrays, pad_rows_to=SUBLANES):
    flat = [jnp.pad(a.reshape(-1), (0, _part_rows(a.size) * LANES - a.size)).reshape(-1, LANES) for a in arrays]
    rows = sum(f.shape[0] for f in flat)
    pad = (-rows) % pad_rows_to
    if pad:
        flat.append(jnp.zeros((pad, LANES), F32))
    return jnp.concatenate(flat, axis=0)


def _unpack(packed, shapes):
    out, r = [], 0
    for shp in shapes:
        size = math.prod(shp)
        out.append(packed[r:r + _part_rows(size)].reshape(-1)[:size].reshape(shp))
        r += _part_rows(size)
    return out


def _rope_tables(n_ctx_rows, s):
    rows = s // GRID_W
    row_idx = jnp.repeat(jnp.arange(rows), GRID_W)
    col_idx = jnp.tile(jnp.arange(GRID_W), rows)
    n_freq = LANES // 4
    inv_freq = ROPE_THETA ** (-jnp.arange(n_freq, dtype=F32) / n_freq)
    ang = jnp.concatenate([row_idx.astype(F32)[:, None] * inv_freq, col_idx.astype(F32)[:, None] * inv_freq], axis=-1)
    cos = jnp.repeat(jnp.cos(ang), 2, axis=-1)
    sin = jnp.repeat(jnp.sin(ang), 2, axis=-1) * jnp.tile(jnp.array([-1.0, 1.0], F32), LANES // 2)
    cos = jnp.concatenate([jnp.ones((n_ctx_rows, LANES), F32), cos], axis=0)
    sin = jnp.concatenate([jnp.zeros((n_ctx_rows, LANES), F32), sin], axis=0)
    return cos, sin


WEIGHT_NAMES = ['c_ctx', 'w_mod', 'b_mod', 'g_mix', 'g_mlp', 'w_in', 'q_gain', 'k_gain', 'conv_w', 'conv_b', 'w_rg',
                'b_rg', 'w_ig', 'b_ig', 'lru_lambda', 'w_o_attn', 'w_o_rnn', 'w_out', 'w_up', 'w_down', 'g_final']
BIG = ['w_in', 'w_o_attn', 'w_o_rnn', 'w_out', 'w_up', 'w_down']
BIG_COL_SHARDED = [True, False, False, False, True, False]
GATES = ['w_rg', 'w_ig']
SMALL = ['c_ctx', 'b_mod', 'g_mix', 'g_mlp', 'q_gain', 'k_gain', 'conv_b', 'g_final',
         'conv_w', 'b_rg', 'b_ig', 'lru_lambda']


def kernel(x, c, ctx, c_ctx, w_mod, b_mod, g_mix, g_mlp, w_in, q_gain, k_gain, conv_w, conv_b, w_rg, b_rg, w_ig, b_ig, lru_lambda, w_o_attn, w_o_rnn, w_out, w_up, w_down, g_final, loss_target, m_c_ctx, m_w_mod, m_b_mod, m_g_mix, m_g_mlp, m_w_in, m_q_gain, m_k_gain, m_conv_w, m_conv_b, m_w_rg, m_b_rg, m_w_ig, m_b_ig, m_lru_lambda, m_w_o_attn, m_w_o_rnn, m_w_out, m_w_up, m_w_down, m_g_final, v_c_ctx, v_w_mod, v_b_mod, v_g_mix, v_g_mlp, v_w_in, v_q_gain, v_k_gain, v_conv_w, v_conv_b, v_w_rg, v_b_rg, v_w_ig, v_b_ig, v_lru_lambda, v_w_o_attn, v_w_o_rnn, v_w_out, v_w_up, v_w_down, v_g_final):
    given = dict(locals())
    weights = {n: given[n] for n in WEIGHT_NAMES}
    moms = {n: given["m_" + n] for n in WEIGHT_NAMES}
    vars_ = {n: given["v_" + n] for n in WEIGHT_NAMES}

    s, d = x.shape[1], x.shape[2]
    n_ctx = ctx.shape[1]
    t = n_ctx + s
    hd = q_gain.shape[1]
    assert hd == LANES and w_rg.shape[-1] == LANES
    attn_w = w_o_attn.shape[1] * N_CHIPS
    n_in = w_in.shape[2] * N_CHIPS
    kv_w = (n_in - attn_w - 4 * d) // 2
    group = attn_w // kv_w
    k_off, v_off, xr_off = attn_w, attn_w + kv_w, attn_w + 2 * kv_w
    xg_off, gl_off = xr_off + d, xr_off + 2 * d
    d_mod = N_MOD * d
    tr = _tile(math.gcd(n_ctx, s), 256, 16)
    tcol = _tile(math.gcd(d, xr_off), 1024, LANES)
    xi, yi, ci = lax.axis_index("x"), lax.axis_index("y"), lax.axis_index("c")
    chip = 2 * xi + yi
    core = ci.astype(jnp.int32).reshape(1)

    sharded_small = [conv_w[0], b_rg[0], b_ig[0], lru_lambda[0]]
    pack0 = _pack([c[0]] + sharded_small)
    got0 = _all_gather8(pack0, "gather_small_inputs").reshape(N_DEV, -1, LANES)
    c_all = got0[:, :_part_rows(d)].reshape(N_DEV, -1)[:, :d]
    per_chip = [_unpack(got0[2 * j, _part_rows(d):], [a.shape for a in sharded_small]) for j in range(N_CHIPS)]
    conv_w_f, b_rg_f, b_ig_f, lam_f = (jnp.concatenate([per_chip[j][i] for j in range(N_CHIPS)], axis=-1)
                                       for i in range(4))
    c16 = jnp.concatenate([c_all, c_ctx[None, :], jnp.zeros((16 - N_DEV - 1, d), F32)], axis=0)
    b_mod_shard = lax.dynamic_slice(b_mod, (0, chip * (d_mod // N_CHIPS)), (1, d_mod // N_CHIPS))
    mod_part, silu16 = _mod_fwd(c16, w_mod[0], b_mod_shard)
    mod_all = _all_gather8(mod_part, "gather_mod").reshape(N_DEV, 16, d_mod // N_CHIPS)
    mod16 = jnp.concatenate([mod_all[2 * j] for j in range(N_CHIPS)], axis=-1)
    me = 4 * xi + 2 * yi + ci
    mod_lat = lax.dynamic_slice(mod16, (me, 0), (1, d_mod)).reshape(N_MOD, d)
    mod_ctx = mod16[N_DEV].reshape(N_MOD, d)
    mod4 = jnp.stack([mod_ctx[0], mod_ctx[1], mod_lat[0], mod_lat[1]])
    mod3 = jnp.stack([mod_lat[2], mod_lat[3], mod_lat[4]])
    gate_f = mod_lat[5][None, :]

    chip_arr = chip.astype(jnp.int32).reshape(1)
    own = {n: _cast_into_window(weights[n][0], chip_arr, col, "cast_" + n) for n, col in zip(BIG, BIG_COL_SHARDED)}
    place = jnp.stack([chip, ci]).astype(jnp.int32)
    row3 = [False] * 3

    def pair_sum(n, full, other, col):
        return _add_half(full, other, core, col, "pair_sum_" + n)

    def chip_sum(n, pair, got, col):
        return _sum_regions(pair, got, place, col, "chip_sum_" + n)

    (w_in_f,) = _exchange(_gather_neighbours([own['w_in']], [True]), "gather_w_in_nbr")
    (w_in_f,) = _exchange(_gather_forward([w_in_f], [True], siblings=True), "gather_w_in_fwd")
    (w_in_f,) = _exchange(_gather_diagonal_d2d(w_in_f, True), "gather_w_in_diag")
    cos, sin = _rope_tables(n_ctx, s)
    h = _norm_mod_fwd(ctx[0], x[0], g_mix, mod4, tr)
    z, (w_oa_f, w_or_f, w_out_f, w_up_f) = _matmul(h, w_in_f, name="mm_in", side=_sides(
        _gather_neighbours([own['w_o_attn'], own['w_o_rnn'], own['w_out']], row3),
        _gather_neighbours([own['w_up']], [True], (0, 1, 2))))
    qr, (w_oa_f, w_or_f, w_out_f) = _head_prep_fwd(z, 0, attn_w // LANES, q_gain, cos, sin, tr, "q_prep",
                                                   side=_gather_forward([w_oa_f, w_or_f, w_out_f], row3))
    kr = _head_prep_fwd(z, k_off, kv_w // LANES, k_gain, cos, sin, tr, "k_prep")
    (attn_o, lse), (w_up_f, w_down_f, w_oa_f, w_or_f, w_out_f) = _attn_fwd(
        qr, kr, z, v_off, n_ctx, group, tr, side=_sides(
            _on_same(_gather_neighbours([w_up_f], [True], (1, 2, 2)), _gather_forward([w_up_f], [True], ways=(True, False))),
            _gather_neighbours([own['w_down']], [False]), _gather_d2d([w_oa_f, w_or_f, w_out_f], row3)))
    (xc, a_f, bx_f, a_b, bx_b), (w_up_f, w_down_f) = _rnn_prep(
        z, xr_off, conv_w_f, conv_b, w_rg[0], b_rg_f, w_ig[0], b_ig_f, lam_f, n_ctx,
        side=_sides(_gather_forward([w_up_f], [True], ways=(False, True)), _gather_forward([w_down_f], [False])))
    (h_f, h_b), (w_up_f,) = _scan([(a_f, bx_f, "ctx_lat_up"), (a_b, bx_b, "ctx_lat_down")], post=False, n_ctx_rows=n_ctx,
                                  name="scan_fwd", side=_gather_d2d([w_up_f], [True]))
    u = _rnn_gate_fwd(h_f, h_b, z, xg_off, n_ctx, tr, tcol)
    y_attn = _matmul(attn_o, w_oa_f, name="mm_o_attn")
    y_rnn = _matmul(u, w_or_f, name="mm_o_rnn")
    mrg = _merge_fwd(y_attn, y_rnn, z, gl_off, n_ctx, tr, tcol)
    mix = _matmul(mrg, w_out_f, name="mm_out")
    x1, h2 = _resid_norm_mod_fwd(x[0], mix, g_mlp, mod3, tr)
    up, (w_down_f,) = _matmul(h2, w_up_f, name="mm_up", side=_gather_d2d([w_down_f], [False]))
    act = _sq_relu(up, tr, tcol)
    down = _matmul(act, w_down_f, name="mm_down")
    dx2, d_down, sums_fin, loss_blk = _final_fwd_bwd(x1, down, loss_target[0], g_final[None, :], gate_f, tr)

    d_up = _matmul(d_down, w_down_f, tb=True, out_dtype=BF16, name="mm_d_up",
                   post=lambda d_act, up_: d_act * 2.0 * jnp.maximum(up_, 0.0), post_args=(up,))
    g_w_down = _matmul(act, d_down, ta=True, out_dtype=BF16, name="mm_g_down")

    def scatter(pairs, cols, lo, hi, into=None):
        return _scatter_regions(pairs, cols, (lo, hi, 8), into)

    dh2, (got,) = _matmul(d_up, w_up_f, tb=True, name="mm_d_h2", side=_swap_halves([g_w_down], [False]))
    p_down = pair_sum('w_down', g_w_down, got, False)
    g_w_up, got_down = _matmul(h2, d_up, ta=True, out_dtype=BF16, name="mm_g_up", side=scatter([p_down], [False], 0, 4))
    (dx1, d_mix, sums2), (got,) = _norm_mod_bwd2(x1, dh2, dx2, mix, g_mlp, mod3, tr, side=_swap_halves([g_w_up], [True]))
    p_up = pair_sum('w_up', g_w_up, got, True)
    d_mrg = _matmul(d_mix, w_out_f, tb=True, name="mm_d_mrg")
    g_w_out = _matmul(mrg, d_mix, ta=True, out_dtype=BF16, name="mm_g_out")
    dz = _dz_start(t, n_in, n_ctx)
    d_ya, dz = _merge_bwd(d_mrg, y_attn, z, gl_off, dz, n_ctx, tr, tcol, "merge_bwd_attn")
    d_yr, dz = _merge_bwd(d_mrg, y_rnn, z, gl_off + d, dz, n_ctx, tr, tcol, "merge_bwd_rnn")
    d_o = _matmul(d_ya, w_oa_f, tb=True, out_dtype=BF16, name="mm_d_o")
    g_w_oa = _matmul(attn_o, d_ya, ta=True, out_dtype=BF16, name="mm_g_o_attn")
    d_u = _matmul(d_yr, w_or_f, tb=True, name="mm_d_u")
    g_w_or = _matmul(u, d_yr, ta=True, out_dtype=BF16, name="mm_g_o_rnn")
    d_rnn, dz = _rnn_gate_bwd(d_u, h_f, h_b, z, xg_off, dz, n_ctx, tr, tcol)
    gs_f, gs_b = _scan([(a_f, d_rnn, "lat_ctx_down"), (a_b, d_rnn, "lat_ctx_up")], post=True, n_ctx_rows=n_ctx,
                       name="scan_bwd")
    o_names, o_grads = ['w_o_attn', 'w_o_rnn', 'w_out'], [g_w_oa, g_w_or, g_w_out]
    (dz, g_w_rg, g_w_ig, sums_rnn), (got_down, *got_o) = _rnn_bwd(
        z, xr_off, xc, gs_f, gs_b, h_f, h_b, conv_w_f, w_rg[0], b_rg_f, w_ig[0], b_ig_f, lam_f, dz, n_ctx,
        side=_sides(scatter([p_down], [False], 4, 8, got_down), _swap_halves(o_grads, row3)))
    hs_down = chip_sum('w_down', p_down, got_down, False)
    p_o = [pair_sum(n, g, o, False) for n, g, o in zip(o_names, o_grads, got_o)]
    gate_cols = 8 * LANES if g_w_rg.size % (8 * LANES * N_CHIPS * 16) == 0 else 2 * LANES
    gate_rows = g_w_rg.size // gate_cols
    gate_grads = [g_w_rg.reshape(gate_rows, gate_cols), g_w_ig.reshape(gate_rows, gate_cols)]
    (dq, dk, dz), (got_up, got_oa, got_rg, got_ig, gs_down) = _attn_bwd(
        qr, kr, z, v_off, d_o, attn_o, lse, dz, n_ctx, group, tr, side=_sides(
            _scatter_regions([p_up], [True]), _scatter_regions(p_o[:1], row3[:1]), _swap_halves(gate_grads, row3[:2]),
            _join_halves([hs_down], [False])))
    hs_up = chip_sum('w_up', p_up, got_up, True)
    hs_oa = chip_sum('w_o_attn', p_o[0], got_oa, False)
    p_gate = [pair_sum(n, g, o, False) for n, g, o in zip(GATES, gate_grads, (got_rg, got_ig))]
    dz, g_q_gain = _head_prep_bwd(z, 0, attn_w // LANES, q_gain, cos, sin, dq, n_ctx, tr, dz, "q_prep_bwd")
    dz, g_k_gain = _head_prep_bwd(z, k_off, kv_w // LANES, k_gain, cos, sin, dk, 0, tr, dz, "k_prep_bwd")
    g_w_in, (got_or, got_out, got_rg, got_ig, gs_up, gs_oa) = _matmul(
        h, dz, ta=True, out_dtype=BF16, name="mm_g_in", side=_sides(
            _scatter_regions(p_o[1:] + p_gate, [False] * 4), _join_halves([hs_up], [True]), _join_halves([hs_oa], [False])))
    hs_late = [chip_sum(n, p, o, False)
               for n, p, o in zip(o_names[1:] + GATES, p_o[1:] + p_gate, (got_or, got_out, got_rg, got_ig))]
    (got,) = _exchange(_swap_halves([g_w_in], [True]), "swap_w_in")
    p_in = pair_sum('w_in', g_w_in, got, True)
    dh, (got_in, fold_in) = _matmul(dz, w_in_f, tb=True, name="mm_d_h", side=_scatter_forward_1(p_in))
    nbr_chips = jnp.stack([2 * (1 - xi) + yi, 2 * xi + 1 - yi]).astype(jnp.int32)
    passed = _fold_forwarded(p_in, fold_in, nbr_chips, "fold_w_in")
    (got_in,) = _exchange(_scatter_forward_2(passed, got_in), "scatter_w_in_fwd")
    grad_x, sums1 = _norm_mod_bwd1(ctx[0], x[0], dh, dx1, g_mix, mod4, tr)

    zeros_d = jnp.zeros((d,), F32)
    dmod_lat = jnp.concatenate([sums1[0], sums1[1], sums2[3], sums2[0], sums2[1], sums_fin[1]])
    dmod_ctx = jnp.concatenate([sums1[3], sums1[4]] + [zeros_d] * 4)
    small_parts = [dmod_lat, dmod_ctx, loss_blk[0, 0:1], sums1[2] + sums1[5], sums2[2], g_q_gain[0], g_k_gain[0],
                   sums_rnn[10], sums_fin[0], sums_rnn[6:10], sums_rnn[0:2], sums_rnn[2:4], sums_rnn[4:6]]
    pack1 = _pack(small_parts)
    got1 = _all_gather8(pack1, "gather_small_grads").reshape(N_DEV, -1, LANES)
    tot1 = _sum_leading(got1, "sum_small_grads")
    part_shapes = [a.shape for a in small_parts]
    (s_dmod_lat, s_dmod_ctx, s_loss, g_g_mix, g_g_mlp, g_q_gain, g_k_gain, g_conv_b, g_g_final,
     g_conv_w_f, g_b_rg_f, g_b_ig_f, g_lam_f) = _unpack(tot1, part_shapes)
    loss = s_loss[0]
    g_b_mod = (s_dmod_lat + s_dmod_ctx)[None, :]
    n_mod_rows = _part_rows(d_mod)
    dmod16 = jnp.concatenate([got1[:, :n_mod_rows].reshape(N_DEV, -1)[:, :d_mod], s_dmod_ctx[None, :],
                              jnp.zeros((16 - N_DEV - 1, d_mod), F32)], axis=0)
    dmod16_shard = lax.dynamic_slice(dmod16, (0, chip * (d_mod // N_CHIPS)), (16, d_mod // N_CHIPS))
    g_w_mod = _matmul(silu16, dmod16_shard, ta=True, name="mm_g_mod")
    dsilu_part = _matmul(dmod16_shard[N_DEV:], w_mod[0], tb=True, name="mm_d_silu")
    dsilu_all = _all_gather8(dsilu_part, "gather_d_silu").reshape(N_DEV, 8, d)
    g_c_ctx = _c_ctx_grad(dsilu_all, c_ctx[None, :])[0]

    def shard_of(full):
        w = full.shape[-1] // N_CHIPS
        return lax.dynamic_slice(full, (0, chip * w), (full.shape[0], w))

    grads = {
        'c_ctx': g_c_ctx, 'b_mod': g_b_mod, 'g_mix': g_g_mix[None, :], 'g_mlp': g_g_mlp[None, :],
        'q_gain': g_q_gain[None, :], 'k_gain': g_k_gain[None, :], 'conv_b': g_conv_b[None, :],
        'g_final': g_g_final,
        'conv_w': shard_of(g_conv_w_f)[None], 'b_rg': shard_of(g_b_rg_f)[None], 'b_ig': shard_of(g_b_ig_f)[None],
        'lru_lambda': shard_of(g_lam_f)[None], 'w_mod': g_w_mod[None],
    }

    delta, new_m, new_v = {}, {}, {}

    def adamw(n):
        shp = weights[n].shape
        as2d = (lambda a: a[0]) if n not in GATES else (lambda a: a.reshape(-1, LANES))
        dl, nm, nv, g = _adamw(as2d(weights[n]), as2d(grads[n]), as2d(moms[n]), as2d(vars_[n]), "adamw_" + n)
        delta[n], new_m[n], new_v[n], grads[n] = dl.reshape(shp), nm.reshape(shp), nv.reshape(shp), g.reshape(shp)

    hs_in = chip_sum('w_in', p_in, got_in, True)
    gs_in, gs_or, gs_out = _exchange(_join_halves([hs_in] + hs_late[:2], [True, False, False]), "join_rest")
    for n, g in zip(['w_in', 'w_o_attn', 'w_o_rnn', 'w_out', 'w_up', 'w_down'], [gs_in, gs_oa, gs_or, gs_out, gs_up, gs_down]):
        grads[n] = g[None]
    half_cols = gate_cols // 2
    mine = jnp.concatenate([lax.dynamic_slice(hs, (0, ci * half_cols), (gate_rows // N_CHIPS, half_cols))
                            for hs in hs_late[2:]], axis=0)
    gate_all = _all_gather8(mine, "gather_gate_grads")
    gate_all = gate_all.reshape(N_CHIPS, 2, len(GATES), gate_rows // N_CHIPS, half_cols)
    for i, n in enumerate(GATES):
        grads[n] = jnp.moveaxis(gate_all[:, :, i], 1, 2).reshape(weights[n].shape)
    for n in ['w_mod'] + BIG + GATES:
        adamw(n)
    small_shapes = [weights[n].shape for n in SMALL]
    packed = [_pack([src[n] for n in SMALL], 512) for src in (weights, grads, moms, vars_)]
    outs = _adamw(*packed, "adamw_small")
    for res, out in zip((delta, new_m, new_v), outs):
        for n, a in zip(SMALL, _unpack(out, small_shapes)):
            res[n] = a
    return (loss, grad_x[None], *[grads[n] for n in WEIGHT_NAMES], *[delta[n] for n in WEIGHT_NAMES],
            *[new_m[n] for n in WEIGHT_NAMES], *[new_v[n] for n in WEIGHT_NAMES])
```

```python
import functools
import math
from typing import Callable, NamedTuple

import jax
import jax.numpy as jnp
from jax import lax
from jax.experimental import pallas as pl
from jax.experimental.pallas import tpu as pltpu

F32 = jnp.float32
BF16 = jnp.bfloat16
MESH_ID = pl.DeviceIdType.MESH
ANY = pl.BlockSpec(memory_space=pl.ANY)

NORM_EPS = 1e-6
LRU_C = 8.0
GRID_W = 64
ROPE_THETA = 10000.0
N_MOD = 6
CONV_WIDTH = 4
ADAM_LR = 0.001
ADAM_B1 = 0.9
ADAM_B2 = 0.999
ADAM_EPS = 1e-08
ADAM_WD = 0.01
ADAM_STEP = 10

LANES = 128
SUBLANES = 8
V7X_VMEM_LIMIT = 48 * 1024 * 1024
WIDE_TILE = 11 * LANES
N_CHIPS = 4
N_DEV = 8
GELU_C = math.sqrt(2.0 / math.pi)
GELU_A = 0.044715


def _tile(dim, pref, align):
    t = min(pref, dim)
    t -= t % align
    while t >= align:
        if dim % t == 0:
            return t
        t -= align
    return dim


def _params(*sem):
    return pltpu.CompilerParams(dimension_semantics=sem, vmem_limit_bytes=V7X_VMEM_LIMIT)


def _sds(shape, dtype=F32):
    return jax.ShapeDtypeStruct(shape, dtype)


class _Side(NamedTuple):
    operands: tuple
    results: tuple
    aliases: dict
    n_sems: int
    build: Callable


def _sides(*sides):
    ops, res, aliases, spans, n = [], [], {}, [], 0
    for s in sides:
        spans.append((len(ops), len(res), n))
        aliases.update({len(ops) + i: len(res) + j for i, j in s.aliases.items()})
        ops += s.operands
        res += s.results
        n += s.n_sems

    def build(op_refs, res_refs, send_sems, recv_sems, sem0):
        sends, recvs = [], []
        for s, (o, r, k) in zip(sides, spans):
            a, b = s.build(op_refs[o:o + len(s.operands)], res_refs[r:r + len(s.results)], send_sems, recv_sems,
                           sem0 + k)
            sends += a
            recvs += b
        return sends, recvs

    return _Side(tuple(ops), tuple(res), aliases, n, build)


def _on_same(*sides):
    def build(ops, res, send_sems, recv_sems, sem0):
        sends, recvs = [], []
        for s in sides:
            a, b = s.build(ops, res, send_sems, recv_sems, sem0)
            sends += a
            recvs += b
            sem0 += s.n_sems
        return sends, recvs

    return _Side(sides[0].operands, sides[0].results, sides[0].aliases, sum(s.n_sems for s in sides), build)


def _call(body, *, side=None, sem=(), grid=(), in_specs=(), out_specs=(), out_shape=(), scratch_shapes=(), **kw):
    if side is None:
        return pl.pallas_call(body, grid=grid, in_specs=list(in_specs), out_specs=out_specs, out_shape=out_shape,
                              scratch_shapes=list(scratch_shapes), compiler_params=_params(*sem), **kw)
    aliases = kw.pop("input_output_aliases", {})
    many = isinstance(out_shape, (list, tuple))
    out_specs_l, out_shape_l = (list(out_specs), list(out_shape)) if many else ([out_specs], [out_shape])
    n_in, n_out, n_scr = len(in_specs), len(out_shape_l), len(scratch_shapes)
    n_op, n_res = len(side.operands), len(side.results)

    def hosted(*refs):
        ins, ops = refs[:n_in], refs[n_in:n_in + n_op]
        outs = refs[n_in + n_op:n_in + n_op + n_out]
        res = refs[n_in + n_op + n_out:n_in + n_op + n_out + n_res]
        scr = refs[n_in + n_op + n_out + n_res:-2]
        send_sems, recv_sems = refs[-2:]

        def start():
            for cp in side.build(ops, res, send_sems, recv_sems, 0)[0]:
                cp.start()

        def finish():
            sends, recvs = side.build(ops, res, send_sems, recv_sems, 0)
            for cp in recvs:
                cp.wait_recv()
            for cp in sends:
                cp.wait_send()

        if not grid:
            start()
            finish()
            return
        ids = [pl.program_id(a) for a in range(len(grid))]
        first = functools.reduce(jnp.logical_and, [i == 0 for i in ids])
        last = functools.reduce(jnp.logical_and, [i == g - 1 for i, g in zip(ids, grid)])
        pl.when(first)(start)
        body(*ins, *outs, *scr)
        pl.when(last)(finish)

    def run(*args):
        got = pl.pallas_call(
            hosted, grid=grid, in_specs=[*in_specs, *[ANY] * n_op], out_specs=[*out_specs_l, *[ANY] * n_res],
            out_shape=[*out_shape_l, *side.results],
            scratch_shapes=[*scratch_shapes, pltpu.SemaphoreType.DMA((side.n_sems,)),
                            pltpu.SemaphoreType.DMA((side.n_sems,))],
            input_output_aliases={**aliases, **{n_in + i: n_out + j for i, j in side.aliases.items()}},
            compiler_params=_params(*["arbitrary"] * len(grid)), **kw)(*args, *side.operands)
        own = list(got[:n_out]) if many else got[0]
        return own, list(got[n_out:])

    return run


def _matmul(a, b, *, ta=False, tb=False, out_dtype=F32, name, tm=1024, tn=1024, tk=2816, side=None, post=None,
            post_args=()):
    k_dim, m = a.shape if ta else a.shape[::-1]
    n, k2 = b.shape if tb else b.shape[::-1]
    assert k_dim == k2, (a.shape, b.shape, ta, tb)
    tm = _tile(m, tm, LANES if ta else 16)
    tn = _tile(n, tn, 16 if tb else LANES)
    tk = _tile(k_dim, tk, LANES)
    nk = k_dim // tk
    dims = (((0 if ta else 1,), (1 if tb else 0,)), ((), ()))
    if nk == 1:
        def whole(a_ref, b_ref, *rest):
            acc = lax.dot_general(a_ref[...].astype(BF16), b_ref[...].astype(BF16), dims, preferred_element_type=F32)
            if post is not None:
                acc = post(acc, *[r[...] for r in rest[:-1]])
            rest[-1][...] = acc.astype(rest[-1].dtype)

        a_spec = pl.BlockSpec((tk, tm), lambda i, j: (0, i)) if ta else pl.BlockSpec((tm, tk), lambda i, j: (i, 0))
        b_spec = pl.BlockSpec((tn, tk), lambda i, j: (j, 0)) if tb else pl.BlockSpec((tk, tn), lambda i, j: (0, j))
        o_spec = pl.BlockSpec((tm, tn), lambda i, j: (i, j))
        return _call(
            whole, side=side, name=name, grid=(m // tm, n // tn), in_specs=[a_spec, b_spec] + [o_spec] * len(post_args),
            out_specs=o_spec, out_shape=_sds((m, n), out_dtype), sem=("parallel", "parallel"),
        )(a, b, *post_args)
    assert post is None

    def body(a_ref, b_ref, o_ref, acc_ref):
        k = pl.program_id(2)

        @pl.when(k == 0)
        def _():
            acc_ref[...] = jnp.zeros_like(acc_ref)

        acc_ref[...] += lax.dot_general(a_ref[...].astype(BF16), b_ref[...].astype(BF16), dims,
                                        preferred_element_type=F32)

        @pl.when(k == nk - 1)
        def _():
            o_ref[...] = acc_ref[...].astype(o_ref.dtype)

    a_spec = pl.BlockSpec((tk, tm), lambda i, j, k: (k, i)) if ta else pl.BlockSpec((tm, tk), lambda i, j, k: (i, k))
    b_spec = pl.BlockSpec((tn, tk), lambda i, j, k: (j, k)) if tb else pl.BlockSpec((tk, tn), lambda i, j, k: (k, j))
    return _call(
        body, side=side, name=name, grid=(m // tm, n // tn, nk), in_specs=[a_spec, b_spec],
        out_specs=pl.BlockSpec((tm, tn), lambda i, j, k: (i, j)), out_shape=_sds((m, n), out_dtype),
        scratch_shapes=[pltpu.VMEM((tm, tn), F32)], sem=("parallel", "parallel", "arbitrary"),
    )(a, b)


def _silu(x):
    return x * jax.nn.sigmoid(x)


def _gelu(x):
    return 0.5 * x * (1.0 + jnp.tanh(GELU_C * (x + GELU_A * x * x * x)))


def _gelu_grad(x):
    t = jnp.tanh(GELU_C * (x + GELU_A * x * x * x))
    return 0.5 * (1.0 + t) + 0.5 * x * (1.0 - t * t) * GELU_C * (1.0 + 3.0 * GELU_A * x * x)


def _expm1_nonpos(x):
    series = x * (1.0 + x * (1.0 / 2 + x * (1.0 / 6 + x * (1.0 / 24 + x * (1.0 / 120 + x * (1.0 / 720 + x / 5040))))))
    return jnp.where(x > -0.25, series, jnp.exp(x) - 1.0)


def _softplus(x):
    return jnp.maximum(x, 0.0) + jnp.log1p(jnp.exp(-jnp.abs(x)))


def _rms_stats(x):
    return lax.rsqrt(jnp.mean(x * x, axis=-1, keepdims=True) + NORM_EPS)


def _rms_bwd(dxhat, xhat, rstd):
    return rstd * (dxhat - xhat * jnp.mean(dxhat * xhat, axis=-1, keepdims=True))


def _colsum(v):
    return jnp.sum(v, axis=0, keepdims=True)


def _mod_fwd(c16, w_mod, b_mod_shard):
    r, d = c16.shape
    n = w_mod.shape[1]
    tn = _tile(n, 512, LANES)

    def body(c_ref, w_ref, b_ref, o_ref, s_ref):
        s = _silu(c_ref[...])
        s_ref[...] = s
        o_ref[...] = jnp.dot(s.astype(BF16), w_ref[...].astype(BF16), preferred_element_type=F32) + b_ref[...]

    return pl.pallas_call(
        body, name="mod_fwd", grid=(n // tn,),
        in_specs=[pl.BlockSpec((r, d), lambda j: (0, 0)), pl.BlockSpec((d, tn), lambda j: (0, j)),
                  pl.BlockSpec((1, tn), lambda j: (0, j))],
        out_specs=[pl.BlockSpec((r, tn), lambda j: (0, j)), pl.BlockSpec((r, d), lambda j: (0, 0))],
        out_shape=[_sds((r, n)), _sds((r, d))], compiler_params=_params("arbitrary"),
    )(c16, w_mod, b_mod_shard)


def _c_ctx_grad(parts, c_ctx_row):
    d = c_ctx_row.shape[1]

    def body(p_ref, c_ref, o_ref):
        tot = p_ref[0, 0:1, :]
        for chip in range(1, N_CHIPS):
            tot = tot + p_ref[2 * chip, 0:1, :]
        c = c_ref[...]
        sg = jax.nn.sigmoid(c)
        o_ref[...] = tot * (sg * (1.0 + c * (1.0 - sg)))

    return pl.pallas_call(body, name="c_ctx_grad", out_shape=_sds((1, d)), compiler_params=_params())(parts, c_ctx_row)


def _token_specs(n_ctx_rows, d, tr):
    nctx = n_ctx_rows // tr
    return (pl.BlockSpec((tr, d), lambda i: (jnp.minimum(i, nctx - 1), 0)),
            pl.BlockSpec((tr, d), lambda i: (jnp.maximum(i - nctx, 0), 0)))


def _norm_mod_fwd(ctx, x, g, mod4, tr):
    (n_ctx_rows, d), s = ctx.shape, x.shape[0]
    t = n_ctx_rows + s
    nctx = n_ctx_rows // tr

    def body(c_ref, x_ref, g_ref, mod_ref, h_ref):
        is_ctx = pl.program_id(0) < nctx
        x = jnp.where(is_ctx, c_ref[...], x_ref[...])
        n = x * _rms_stats(x) * g_ref[...]
        sh = jnp.where(is_ctx, mod_ref[0:1, :], mod_ref[2:3, :])
        sc = jnp.where(is_ctx, mod_ref[1:2, :], mod_ref[3:4, :])
        h_ref[...] = (n * (1.0 + sc) + sh).astype(BF16)

    return pl.pallas_call(
        body, name="norm_mod_fwd", grid=(t // tr,),
        in_specs=[*_token_specs(n_ctx_rows, d, tr), pl.BlockSpec((1, d), lambda i: (0, 0)),
                  pl.BlockSpec((4, d), lambda i: (0, 0))],
        out_specs=pl.BlockSpec((tr, d), lambda i: (i, 0)), out_shape=_sds((t, d), BF16),
        compiler_params=_params("parallel"),
    )(ctx, x, g, mod4)


def _norm_mod_bwd1(ctx, x, dh, dx1, g, mod4, tr, side=None):
    (n_ctx_rows, d), s = ctx.shape, x.shape[0]
    t = n_ctx_rows + s
    nctx = n_ctx_rows // tr

    def body(c_ref, x_ref, dh_ref, dx1_ref, g_ref, mod_ref, dx_ref, sums_ref):
        i = pl.program_id(0)
        is_ctx = i < nctx

        @pl.when(i == 0)
        def _():
            sums_ref[...] = jnp.zeros_like(sums_ref)

        x = jnp.where(is_ctx, c_ref[...], x_ref[...])
        dh_ = dh_ref[...]
        rstd = _rms_stats(x)
        xhat = x * rstd
        gg = g_ref[...]
        sc = jnp.where(is_ctx, mod_ref[1:2, :], mod_ref[3:4, :])
        dxhat = dh_ * (1.0 + sc) * gg
        dx_ref[...] = dx1_ref[...] + _rms_bwd(dxhat, xhat, rstd)
        part = [_colsum(dh_), _colsum(dh_ * xhat * gg), _colsum(dh_ * (1.0 + sc) * xhat)]

        @pl.when(is_ctx)
        def _():
            for k, row in enumerate(part):
                sums_ref[3 + k:4 + k, :] += row

        @pl.when(jnp.logical_not(is_ctx))
        def _():
            for k, row in enumerate(part):
                sums_ref[k:k + 1, :] += row

    lat = lambda i: (jnp.maximum(i - nctx, 0), 0)
    return _call(
        body, side=side, name="norm_mod_bwd1", grid=(t // tr,),
        in_specs=[*_token_specs(n_ctx_rows, d, tr), pl.BlockSpec((tr, d), lambda i: (i, 0)),
                  pl.BlockSpec((tr, d), lat), pl.BlockSpec((1, d), lambda i: (0, 0)),
                  pl.BlockSpec((4, d), lambda i: (0, 0))],
        out_specs=[pl.BlockSpec((tr, d), lat), pl.BlockSpec((8, d), lambda i: (0, 0))],
        out_shape=[_sds((s, d)), _sds((8, d))], sem=("arbitrary",),
    )(ctx, x, dh, dx1, g, mod4)


def _resid_norm_mod_fwd(x, mix, g, mod3, tr):
    s, d = x.shape

    def body(x_ref, mix_ref, g_ref, mod_ref, x1_ref, h_ref):
        x1 = x_ref[...] + mod_ref[0:1, :] * mix_ref[...]
        x1_ref[...] = x1
        n = x1 * _rms_stats(x1) * g_ref[...]
        h_ref[...] = (n * (1.0 + mod_ref[2:3, :]) + mod_ref[1:2, :]).astype(BF16)

    row = pl.BlockSpec((tr, d), lambda i: (i, 0))
    return pl.pallas_call(
        body, name="resid_norm_mod_fwd", grid=(s // tr,),
        in_specs=[row, row, pl.BlockSpec((1, d), lambda i: (0, 0)), pl.BlockSpec((3, d), lambda i: (0, 0))],
        out_specs=[row, row], out_shape=[_sds((s, d)), _sds((s, d), BF16)], compiler_params=_params("parallel"),
    )(x, mix, g, mod3)


def _norm_mod_bwd2(x1, dh2, dx2, mix, g, mod3, tr, side=None):
    s, d = x1.shape

    def body(x_ref, dh_ref, dx2_ref, mix_ref, g_ref, mod_ref, dx1_ref, dmix_ref, sums_ref):
        @pl.when(pl.program_id(0) == 0)
        def _():
            sums_ref[...] = jnp.zeros_like(sums_ref)

        x = x_ref[...]
        dh_ = dh_ref[...]
        rstd = _rms_stats(x)
        xhat = x * rstd
        gg = g_ref[...]
        sc = mod_ref[2:3, :]
        dx1 = dx2_ref[...] + _rms_bwd(dh_ * (1.0 + sc) * gg, xhat, rstd)
        dx1_ref[...] = dx1
        dmix_ref[...] = (dx1 * mod_ref[0:1, :]).astype(BF16)
        part = [_colsum(dh_), _colsum(dh_ * xhat * gg), _colsum(dh_ * (1.0 + sc) * xhat), _colsum(dx1 * mix_ref[...])]
        for k, row in enumerate(part):
            sums_ref[k:k + 1, :] += row

    row = pl.BlockSpec((tr, d), lambda i: (i, 0))
    return _call(
        body, side=side, name="norm_mod_bwd2", grid=(s // tr,),
        in_specs=[row, row, row, row, pl.BlockSpec((1, d), lambda i: (0, 0)), pl.BlockSpec((3, d), lambda i: (0, 0))],
        out_specs=[row, row, pl.BlockSpec((8, d), lambda i: (0, 0))],
        out_shape=[_sds((s, d)), _sds((s, d), BF16), _sds((8, d))], sem=("arbitrary",),
    )(x1, dh2, dx2, mix, g, mod3)


def _final_fwd_bwd(x1, down, target, g_final, gate, tr):
    s, d = x1.shape

    def body(x1_ref, down_ref, t_ref, g_ref, gate_ref, dx2_ref, ddown_ref, sums_ref, loss_ref):
        @pl.when(pl.program_id(0) == 0)
        def _():
            sums_ref[...] = jnp.zeros_like(sums_ref)
            loss_ref[...] = jnp.zeros_like(loss_ref)

        down_ = down_ref[...]
        gate_ = gate_ref[...]
        x2 = x1_ref[...] + gate_ * down_
        rstd = _rms_stats(x2)
        xhat = x2 * rstd
        gg = g_ref[...]
        err = xhat * gg - t_ref[...]
        loss_ref[...] += 0.5 * jnp.sum(jnp.mean(err * err, axis=-1, keepdims=True))
        dy = err * (1.0 / d)
        dx2 = _rms_bwd(dy * gg, xhat, rstd)
        dx2_ref[...] = dx2
        ddown_ref[...] = (dx2 * gate_).astype(BF16)
        sums_ref[0:1, :] += _colsum(dy * xhat)
        sums_ref[1:2, :] += _colsum(dx2 * down_)

    row = pl.BlockSpec((tr, d), lambda i: (i, 0))
    vec = pl.BlockSpec((1, d), lambda i: (0, 0))
    return pl.pallas_call(
        body, name="final_fwd_bwd", grid=(s // tr,), in_specs=[row, row, row, vec, vec],
        out_specs=[row, row, pl.BlockSpec((8, d), lambda i: (0, 0)), pl.BlockSpec((8, LANES), lambda i: (0, 0))],
        out_shape=[_sds((s, d)), _sds((s, d), BF16), _sds((8, d)), _sds((8, LANES))],
        compiler_params=_params("arbitrary"),
    )(x1, down, target, g_final, gate)


def _swap_pairs(v):
    lane = lax.broadcasted_iota(jnp.int32, v.shape, 1)
    return jnp.where(lane % 2 == 0, pltpu.roll(v, LANES - 1, 1), pltpu.roll(v, 1, 1))


def _head_prep_fwd(z, col_off, n_heads, gain, cos, sin, tr, name, side=None):
    t = z.shape[0]
    per = math.gcd(4, n_heads, col_off // LANES)
    w = per * LANES
    hb = col_off // w

    def body(z_ref, g_ref, cos_ref, sin_ref, o_ref):
        for hh in range(per):
            cols = slice(hh * LANES, (hh + 1) * LANES)
            x = z_ref[:, cols]
            y = x * _rms_stats(x) * g_ref[...]
            o_ref[:, cols] = (y * cos_ref[...] + _swap_pairs(y) * sin_ref[...]).astype(BF16)

    tab = pl.BlockSpec((tr, LANES), lambda i, j: (i, 0))
    return _call(
        body, side=side, name=name, grid=(t // tr, n_heads // per),
        in_specs=[pl.BlockSpec((tr, w), lambda i, j: (i, hb + j)), pl.BlockSpec((1, LANES), lambda i, j: (0, 0)),
                  tab, tab],
        out_specs=pl.BlockSpec((tr, w), lambda i, j: (i, j)), out_shape=_sds((t, n_heads * LANES), BF16),
        sem=("parallel", "parallel"),
    )(z, gain, cos, sin)


def _dz_start(t, n_in, n_ctx_rows):
    tcol = _tile(n_in, 1024, LANES)

    def body(o_ref):
        o_ref[...] = jnp.zeros_like(o_ref)

    return pl.pallas_call(body, name="dz_start", grid=(n_in // tcol,),
                          out_specs=pl.BlockSpec((n_ctx_rows, tcol), lambda j: (0, j)), out_shape=_sds((t, n_in), BF16),
                          compiler_params=_params("parallel"))()


def _head_prep_bwd(z, col_off, n_heads, gain, cos, sin, dout, row_off, tr, dz, name, side=None):
    r = dout.shape[0]
    per = math.gcd(4, n_heads, col_off // LANES)
    w = per * LANES
    hb = col_off // w
    rb = row_off // tr

    def body(z_ref, g_ref, cos_ref, sin_ref, d_ref, _, dz_ref, dg_ref):
        @pl.when(jnp.logical_and(pl.program_id(0) == 0, pl.program_id(1) == 0))
        def _():
            dg_ref[...] = jnp.zeros_like(dg_ref)

        for hh in range(per):
            cols = slice(hh * LANES, (hh + 1) * LANES)
            x = z_ref[:, cols]
            rstd = _rms_stats(x)
            xhat = x * rstd
            dd = d_ref[:, cols]
            dy = dd * cos_ref[...] - _swap_pairs(dd) * sin_ref[...]
            dg_ref[0:1, :] += _colsum(dy * xhat)
            dz_ref[:, cols] = _rms_bwd(dy * g_ref[...], xhat, rstd).astype(BF16)

    tab = pl.BlockSpec((tr, LANES), lambda i, j: (rb + i, 0))
    window = pl.BlockSpec((tr, w), lambda i, j: (rb + i, hb + j))
    return _call(
        body, side=side, name=name, grid=(r // tr, n_heads // per),
        in_specs=[window, pl.BlockSpec((1, LANES), lambda i, j: (0, 0)), tab, tab,
                  pl.BlockSpec((tr, w), lambda i, j: (i, j)), ANY],
        out_specs=[window, pl.BlockSpec((8, LANES), lambda i, j: (0, 0))],
        out_shape=[_sds(dz.shape, BF16), _sds((8, LANES))], input_output_aliases={5: 0}, sem=("arbitrary", "arbitrary"),
    )(z, gain, cos, sin, dout, dz)


def _attn_fwd(qr, kr, z, v_off, n_ctx_rows, group, tq, side=None):
    t, kvw = kr.shape
    s = t - n_ctx_rows
    n_kv = kvw // LANES
    scale = LANES ** -0.5
    qb0 = n_ctx_rows // tq
    vb = v_off // LANES

    def body(q_ref, k_ref, v_ref, o_ref, lse_ref):
        k = k_ref[...]
        v = v_ref[...].astype(BF16)
        lse_ref[...] = jnp.zeros_like(lse_ref)
        for g in range(group):
            cols = slice(g * LANES, (g + 1) * LANES)
            sc = lax.dot_general(q_ref[:, cols], k, (((1,), (1,)), ((), ())), preferred_element_type=F32) * scale
            m = jnp.max(sc, axis=-1, keepdims=True)
            e = jnp.exp(sc - m)
            l = jnp.sum(e, axis=-1, keepdims=True)
            p = e * (1.0 / l)
            o_ref[:, cols] = jnp.dot(p.astype(BF16), v, preferred_element_type=F32).astype(BF16)
            lse_ref[:, g:g + 1] = m + jnp.log(l)

    return _call(
        body, side=side, name="attn_fwd", grid=(n_kv, s // tq),
        in_specs=[pl.BlockSpec((tq, group * LANES), lambda h, i: (qb0 + i, h)),
                  pl.BlockSpec((t, LANES), lambda h, i: (0, h)), pl.BlockSpec((t, LANES), lambda h, i: (0, vb + h))],
        out_specs=[pl.BlockSpec((tq, group * LANES), lambda h, i: (i, h)), pl.BlockSpec((tq, LANES), lambda h, i: (i, h))],
        out_shape=[_sds((s, n_kv * group * LANES), BF16), _sds((s, kvw))], sem=("parallel", "parallel"),
    )(qr, kr, z)


def _attn_bwd(qr, kr, z, v_off, d_o, attn_o, lse, dz, n_ctx_rows, group, tq, side=None):
    t, kvw = kr.shape
    s = t - n_ctx_rows
    n_kv = kvw // LANES
    scale = LANES ** -0.5
    qb0 = n_ctx_rows // tq
    vb = v_off // LANES
    n_q_blocks = s // tq
    tn_dims = (((0,), (0,)), ((), ()))
    nt_dims = (((1,), (1,)), ((), ()))

    def body(q_ref, k_ref, v_ref, do_ref, o_ref, lse_ref, _, dq_ref, dk_ref, dz_ref, dv_ref):
        @pl.when(pl.program_id(1) == 0)
        def _():
            dk_ref[...] = jnp.zeros_like(dk_ref)
            dv_ref[...] = jnp.zeros_like(dv_ref)

        k = k_ref[...]
        v = v_ref[...].astype(BF16)
        for g in range(group):
            cols = slice(g * LANES, (g + 1) * LANES)
            q = q_ref[:, cols]
            do_ = do_ref[:, cols]
            row_dot = jnp.sum(do_.astype(F32) * o_ref[:, cols].astype(F32), axis=-1, keepdims=True)
            sc = lax.dot_general(q, k, nt_dims, preferred_element_type=F32)
            p = jnp.exp(sc * scale - lse_ref[:, g:g + 1])
            dv_ref[...] += lax.dot_general(p.astype(BF16), do_, tn_dims, preferred_element_type=F32)
            dp = lax.dot_general(do_, v, nt_dims, preferred_element_type=F32)
            ds = (p * (dp - row_dot)).astype(BF16)
            dq_ref[:, cols] = jnp.dot(ds, k, preferred_element_type=F32) * scale
            dk_ref[...] += lax.dot_general(ds, q, tn_dims, preferred_element_type=F32)

        @pl.when(pl.program_id(1) == n_q_blocks - 1)
        def _():
            dk_ref[...] = dk_ref[...] * scale
            dz_ref[...] = dv_ref[...].astype(BF16)

    qspec = pl.BlockSpec((tq, group * LANES), lambda h, i: (qb0 + i, h))
    ospec = pl.BlockSpec((tq, group * LANES), lambda h, i: (i, h))
    kspec = pl.BlockSpec((t, LANES), lambda h, i: (0, h))
    vspec = pl.BlockSpec((t, LANES), lambda h, i: (0, vb + h))
    return _call(
        body, side=side, name="attn_bwd", grid=(n_kv, n_q_blocks),
        in_specs=[qspec, kspec, vspec, ospec, ospec, pl.BlockSpec((tq, LANES), lambda h, i: (i, h)), ANY],
        out_specs=[ospec, kspec, vspec], out_shape=[_sds((s, n_kv * group * LANES)), _sds((t, kvw)), _sds(dz.shape, BF16)],
        scratch_shapes=[pltpu.VMEM((t, LANES), F32)], input_output_aliases={6: 2}, sem=("parallel", "arbitrary"),
    )(qr, kr, z, d_o, attn_o, lse, dz)


def _row_mask(shape, rows):
    r = lax.broadcasted_iota(jnp.int32, shape, 0)
    m = r == rows[0]
    for v in rows[1:]:
        m = jnp.logical_or(m, r == v)
    return m


def _shift_rows(x, k, n_ctx_rows):
    t = x.shape[0]
    if k == 0:
        return x
    rolled = pltpu.roll(x, (-k) % t, 0)
    if k > 0:
        dead = [n_ctx_rows - 1 - i for i in range(k)] + [t - 1 - i for i in range(k)]
    else:
        dead = [i for i in range(-k)] + [n_ctx_rows + i for i in range(-k)]
    return jnp.where(_row_mask(x.shape, dead), 0.0, rolled)


def _conv(x, w, b, n_ctx_rows):
    y = b
    for k in range(CONV_WIDTH):
        y = y + _shift_rows(x, k - 1, n_ctx_rows) * w[k:k + 1, :]
    return y


def _gates(xc_bf, w_r, b_r, w_i, b_i, lam):
    r = jax.nn.sigmoid(jnp.dot(xc_bf, w_r.astype(BF16), preferred_element_type=F32) + b_r)
    i = jax.nn.sigmoid(jnp.dot(xc_bf, w_i.astype(BF16), preferred_element_type=F32) + b_i)
    log_a = -LRU_C * r * _softplus(-lam)
    a = jnp.exp(log_a)
    mult = jnp.sqrt(-_expm1_nonpos(2.0 * log_a))
    return r, i, a, mult


def _rnn_specs(t, xr_off):
    xb = xr_off // LANES
    return dict(
        zcol=pl.BlockSpec((t, LANES), lambda j: (0, xb + j)), col=pl.BlockSpec((t, LANES), lambda j: (0, j)),
        conv_w=pl.BlockSpec((CONV_WIDTH, LANES), lambda j: (0, j)), vec=pl.BlockSpec((1, LANES), lambda j: (0, j)),
        gate_w=pl.BlockSpec((2, 1, LANES, LANES), lambda j: (0, j, 0, 0)), two=pl.BlockSpec((2, LANES), lambda j: (0, j)))


def _rnn_prep(z, xr_off, conv_w, conv_b, w_rg, b_rg, w_ig, b_ig, lam, n_ctx_rows, side=None):
    t = z.shape[0]
    d = conv_b.shape[1]
    sp = _rnn_specs(t, xr_off)

    def body(z_ref, cw_ref, cb_ref, wr_ref, br_ref, wi_ref, bi_ref, lam_ref, xc_ref, af_ref, bf_ref, ab_ref, bb_ref):
        xc = _conv(z_ref[...], cw_ref[...], cb_ref[...], n_ctx_rows)
        xc_ref[...] = xc
        xc_bf = xc.astype(BF16)
        for dr, (a_ref, b_ref) in enumerate(((af_ref, bf_ref), (ab_ref, bb_ref))):
            _, i, a, mult = _gates(xc_bf, wr_ref[dr, 0], br_ref[dr:dr + 1, :], wi_ref[dr, 0], bi_ref[dr:dr + 1, :],
                                   lam_ref[dr:dr + 1, :])
            a_ref[...] = a
            b_ref[...] = mult * (i * xc)

    return _call(
        body, side=side, name="rnn_prep", grid=(d // LANES,),
        in_specs=[sp["zcol"], sp["conv_w"], sp["vec"], sp["gate_w"], sp["two"], sp["gate_w"], sp["two"], sp["two"]],
        out_specs=[sp["col"]] * 5, out_shape=[_sds((t, d))] * 5, sem=("parallel",),
    )(z, conv_w, conv_b, w_rg, b_rg, w_ig, b_ig, lam)


def _scan(chains, *, post, n_ctx_rows, name, tc=256, side=None):
    t, d = chains[0][0].shape
    tc = _tile(math.gcd(n_ctx_rows, t - n_ctx_rows), tc, SUBLANES)
    nt, nctx = t // tc, n_ctx_rows // tc
    nlat = nt - nctx
    nc = len(chains)
    lat_only = [b.shape[0] != t for _, b, _ in chains]
    ups = [order.endswith("up") for _, _, order in chains]

    def chunk_of(order):
        def chunk(i):
            if order == "ctx_lat_up":
                return i
            if order == "lat_ctx_down":
                return nt - 1 - i
            if order == "ctx_lat_down":
                return jnp.where(i < nctx, nctx - 1 - i, nt - 1 - (i - nctx))
            return jnp.where(i < nlat, nctx + i, i - nlat)
        return chunk

    chunks = [chunk_of(order) for _, _, order in chains]

    def body(*refs):
        ab_refs, o_refs, carry_ref = refs[:2 * nc], refs[2 * nc:3 * nc], refs[3 * nc]

        @pl.when(pl.program_id(0) == 0)
        def _():
            carry_ref[...] = jnp.zeros_like(carry_ref)

        live = [jnp.where(chunks[n](pl.program_id(0)) >= nctx, 1.0, 0.0) if lat_only[n] else None for n in range(nc)]

        def group(gi, carries):
            carries = list(carries)
            bases = [pl.multiple_of((gi if ups[n] else tc // SUBLANES - 1 - gi) * SUBLANES, SUBLANES) for n in range(nc)]
            for step in range(SUBLANES):
                for n in range(nc):
                    row = bases[n] + (step if ups[n] else SUBLANES - 1 - step)
                    a_r = ab_refs[2 * n][pl.ds(row, 1), :]
                    b_r = ab_refs[2 * n + 1][pl.ds(row, 1), :]
                    if live[n] is not None:
                        b_r = b_r * live[n]
                    if post:
                        out = b_r + carries[n]
                        carries[n] = a_r * out
                    else:
                        out = a_r * carries[n] + b_r
                        carries[n] = out
                    o_refs[n][pl.ds(row, 1), :] = out
            return tuple(carries)

        done = lax.fori_loop(0, tc // SUBLANES, group, tuple(carry_ref[n:n + 1, :] for n in range(nc)))
        for n in range(nc):
            carry_ref[n:n + 1, :] = done[n]

    in_specs, out_specs, args = [], [], []
    for n, (a, b, _) in enumerate(chains):
        full = pl.BlockSpec((tc, d), lambda i, n=n: (chunks[n](i), 0))
        lat = pl.BlockSpec((tc, d), lambda i, n=n: (jnp.maximum(chunks[n](i) - nctx, 0), 0))
        in_specs += [full, lat if lat_only[n] else full]
        out_specs.append(full)
        args += [a, b]
    return _call(
        body, side=side, name=name, grid=(nt,), in_specs=in_specs, out_specs=out_specs, out_shape=[_sds((t, d))] * nc,
        scratch_shapes=[pltpu.VMEM((SUBLANES, d), F32)], sem=("arbitrary",),
    )(*args)


def _rnn_bwd(z, xr_off, xc, g_f, g_b, h_f, h_b, conv_w, w_rg, b_rg, w_ig, b_ig, lam, dz, n_ctx_rows, side=None):
    t, d = xc.shape
    sp = _rnn_specs(t, xr_off)
    tn_dims = (((0,), (0,)), ((), ()))
    nt_dims = (((1,), (1,)), ((), ()))

    def body(z_ref, xc_ref, gf_ref, gb_ref, hf_ref, hb_ref, cw_ref, wr_ref, br_ref, wi_ref, bi_ref, lam_ref, _,
             dxr_ref, dwr_ref, dwi_ref, sums_ref):
        xc_ = xc_ref[...]
        xc_bf = xc_.astype(BF16)
        dxc = jnp.zeros_like(xc_)
        sums = [None] * 6
        for dr, (g_ref, h_ref) in enumerate(((gf_ref, hf_ref), (gb_ref, hb_ref))):
            w_r, w_i, lam_ = wr_ref[dr, 0], wi_ref[dr, 0], lam_ref[dr:dr + 1, :]
            r, i, a, mult = _gates(xc_bf, w_r, br_ref[dr:dr + 1, :], w_i, bi_ref[dr:dr + 1, :], lam_)
            g = g_ref[...]
            h = h_ref[...]
            if dr == 0:
                h_prev = jnp.where(_row_mask(h.shape, [0]), 0.0, pltpu.roll(h, 1, 0))
            else:
                h_prev = jnp.where(_row_mask(h.shape, [n_ctx_rows - 1]), 0.0, pltpu.roll(h, t - 1, 0))
            d_mult = g * i * xc_
            d_i = g * mult * xc_
            dxc = dxc + g * mult * i
            d_log_a = g * h_prev * a - d_mult * a * a / mult
            sp_ = _softplus(-lam_)
            d_r = d_log_a * (-LRU_C) * sp_
            d_sp = _colsum(d_log_a * (-LRU_C) * r)
            du_r = (d_r * r * (1.0 - r))
            du_i = (d_i * i * (1.0 - i))
            sums[dr] = _colsum(du_r)
            sums[2 + dr] = _colsum(du_i)
            sums[4 + dr] = d_sp * (-jax.nn.sigmoid(-lam_))
            du_r_bf, du_i_bf = du_r.astype(BF16), du_i.astype(BF16)
            dwr_ref[dr, 0] = lax.dot_general(xc_bf, du_r_bf, tn_dims, preferred_element_type=F32).astype(BF16)
            dwi_ref[dr, 0] = lax.dot_general(xc_bf, du_i_bf, tn_dims, preferred_element_type=F32).astype(BF16)
            dxc = dxc + lax.dot_general(du_r_bf, w_r.astype(BF16), nt_dims, preferred_element_type=F32)
            dxc = dxc + lax.dot_general(du_i_bf, w_i.astype(BF16), nt_dims, preferred_element_type=F32)
        xr = z_ref[...]
        cw = cw_ref[...]
        dxr = jnp.zeros_like(dxc)
        rows = list(sums)
        for k in range(CONV_WIDTH):
            dxr = dxr + _shift_rows(dxc, 1 - k, n_ctx_rows) * cw[k:k + 1, :]
            rows.append(_colsum(dxc * _shift_rows(xr, k - 1, n_ctx_rows)))
        rows.append(_colsum(dxc))
        dxr_ref[...] = dxr.astype(BF16)
        sums_ref[...] = jnp.zeros_like(sums_ref)
        for k, row in enumerate(rows):
            sums_ref[k:k + 1, :] = row

    return _call(
        body, side=side, name="rnn_bwd", grid=(d // LANES,),
        in_specs=[sp["zcol"]] + [sp["col"]] * 5 + [sp["conv_w"], sp["gate_w"], sp["two"], sp["gate_w"], sp["two"],
                                                  sp["two"], ANY],
        out_specs=[sp["zcol"], sp["gate_w"], sp["gate_w"], pl.BlockSpec((16, LANES), lambda j: (0, j))],
        out_shape=[_sds(dz.shape, BF16), _sds(w_rg.shape, BF16), _sds(w_ig.shape, BF16), _sds((16, d))],
        input_output_aliases={12: 0}, sem=("parallel",),
    )(z, xc, g_f, g_b, h_f, h_b, conv_w, w_rg, b_rg, w_ig, b_ig, lam, dz)


def _tiles2d(s, d, tr, tcol):
    return (s // tr, d // tcol), pl.BlockSpec((tr, tcol), lambda i, j: (i, j))


def _zspec(tr, tcol, row_off, col_off):
    rb, cb = row_off // tr, col_off // tcol
    return pl.BlockSpec((tr, tcol), lambda i, j: (rb + i, cb + j))


def _rnn_gate_fwd(h_f, h_b, z, xg_off, n_ctx_rows, tr, tcol, side=None):
    t, d = h_f.shape
    s = t - n_ctx_rows
    grid, out = _tiles2d(s, d, tr, tcol)
    hs = _zspec(tr, tcol, n_ctx_rows, 0)

    def body(hf_ref, hb_ref, xg_ref, u_ref):
        u_ref[...] = ((hf_ref[...] + hb_ref[...]) * _gelu(xg_ref[...])).astype(BF16)

    return _call(body, side=side, name="rnn_gate_fwd", grid=grid,
                 in_specs=[hs, hs, _zspec(tr, tcol, n_ctx_rows, xg_off)], out_specs=out, out_shape=_sds((s, d), BF16),
                 sem=("parallel", "parallel"))(h_f, h_b, z)


def _rnn_gate_bwd(d_u, h_f, h_b, z, xg_off, dz, n_ctx_rows, tr, tcol, side=None):
    t, d = h_f.shape
    s = t - n_ctx_rows
    grid, out = _tiles2d(s, d, tr, tcol)
    hs = _zspec(tr, tcol, n_ctx_rows, 0)
    window = _zspec(tr, tcol, n_ctx_rows, xg_off)

    def body(du_ref, hf_ref, hb_ref, xg_ref, _, dr_ref, dxg_ref):
        du = du_ref[...]
        xg = xg_ref[...]
        dr_ref[...] = du * _gelu(xg)
        dxg_ref[...] = (du * (hf_ref[...] + hb_ref[...]) * _gelu_grad(xg)).astype(BF16)

    return _call(body, side=side, name="rnn_gate_bwd", grid=grid, in_specs=[out, hs, hs, window, ANY],
                 out_specs=[out, window], out_shape=[_sds((s, d)), _sds(dz.shape, BF16)], input_output_aliases={4: 1},
                 sem=("parallel", "parallel"))(d_u, h_f, h_b, z, dz)


def _merge_fwd(y_attn, y_rnn, z, gl_off, n_ctx_rows, tr, tcol):
    s, d = y_attn.shape
    grid, out = _tiles2d(s, d, tr, tcol)

    def body(ya_ref, yr_ref, ga_ref, gr_ref, o_ref):
        o_ref[...] = (jax.nn.sigmoid(ga_ref[...]) * ya_ref[...] + jax.nn.sigmoid(gr_ref[...]) * yr_ref[...]).astype(BF16)

    return pl.pallas_call(
        body, name="merge_fwd", grid=grid,
        in_specs=[out, out, _zspec(tr, tcol, n_ctx_rows, gl_off), _zspec(tr, tcol, n_ctx_rows, gl_off + d)],
        out_specs=out, out_shape=_sds((s, d), BF16), compiler_params=_params("parallel", "parallel"),
    )(y_attn, y_rnn, z, z)


def _merge_bwd(d_mrg, y, z, gl_off, dz, n_ctx_rows, tr, tcol, name):
    s, d = y.shape
    grid, out = _tiles2d(s, d, tr, tcol)
    window = _zspec(tr, tcol, n_ctx_rows, gl_off)

    def body(dm_ref, y_ref, gl_ref, _, dy_ref, dgl_ref):
        dm = dm_ref[...]
        g = jax.nn.sigmoid(gl_ref[...])
        dy_ref[...] = (dm * g).astype(BF16)
        dgl_ref[...] = (dm * y_ref[...] * g * (1.0 - g)).astype(BF16)

    return _call(body, name=name, grid=grid, in_specs=[out, out, window, ANY], out_specs=[out, window],
                 out_shape=[_sds((s, d), BF16), _sds(dz.shape, BF16)], input_output_aliases={3: 1},
                 sem=("parallel", "parallel"))(d_mrg, y, z, dz)


def _sq_relu(up, tr, tcol, side=None):
    grid, out = _tiles2d(*up.shape, _tile(up.shape[0], 2 * tr, 16), _tile(up.shape[1], 4 * tcol, LANES))

    def body(u_ref, o_ref):
        r = jnp.maximum(u_ref[...], 0.0)
        o_ref[...] = (r * r).astype(BF16)

    return _call(body, side=side, name="sq_relu", grid=grid, in_specs=[out], out_specs=out,
                 out_shape=_sds(up.shape, BF16), sem=("parallel", "parallel"))(up)


def _cast_into_window(w, chip, col_sharded, name):
    r, c = w.shape
    tr, tcol = _tile(r, 512, 16), _tile(c, WIDE_TILE, LANES)
    nrb, ncb = r // tr, c // tcol

    def body(chip_ref, w_ref, o_ref):
        o_ref[...] = w_ref[...].astype(BF16)

    if col_sharded:
        omap = lambda i, j, chip_ref: (i, chip_ref[0] * ncb + j)
    else:
        omap = lambda i, j, chip_ref: (chip_ref[0] * nrb + i, j)
    return pl.pallas_call(
        body, name=name,
        grid_spec=pltpu.PrefetchScalarGridSpec(
            num_scalar_prefetch=1, grid=(nrb, ncb),
            in_specs=[pl.BlockSpec((tr, tcol), lambda i, j, chip_ref: (i, j))], out_specs=pl.BlockSpec((tr, tcol), omap)),
        out_shape=_sds((r, c * N_CHIPS) if col_sharded else (r * N_CHIPS, c), BF16),
        compiler_params=_params("parallel", "parallel"),
    )(chip, w)


def _sum_leading(parts, name):
    n, r, c = parts.shape
    tr, tcol = _tile(r, 512, SUBLANES), _tile(c, 1024, LANES)

    def body(p_ref, o_ref):
        tot = p_ref[0]
        for k in range(1, n):
            tot = tot + p_ref[k]
        o_ref[...] = tot

    return pl.pallas_call(
        body, name=name, grid=(r // tr, c // tcol), in_specs=[pl.BlockSpec((n, tr, tcol), lambda i, j: (0, i, j))],
        out_specs=pl.BlockSpec((tr, tcol), lambda i, j: (i, j)), out_shape=_sds((r, c)),
        compiler_params=_params("parallel", "parallel"),
    )(parts)


def _add_half(full, other, core, split_rows, name):
    r, c = other.shape
    tr, tcol = _tile(r, 512, 16), _tile(c, 1024, LANES)
    nrb, ncb = r // tr, c // tcol

    def body(core_ref, f_ref, o_ref, out_ref):
        out_ref[...] = (f_ref[...].astype(F32) + o_ref[...].astype(F32)).astype(out_ref.dtype)

    if split_rows:
        fmap = lambda i, j, core_ref: (core_ref[0] * nrb + i, j)
    else:
        fmap = lambda i, j, core_ref: (i, core_ref[0] * ncb + j)
    same = lambda i, j, core_ref: (i, j)
    return pl.pallas_call(
        body, name=name,
        grid_spec=pltpu.PrefetchScalarGridSpec(
            num_scalar_prefetch=1, grid=(nrb, ncb),
            in_specs=[pl.BlockSpec((tr, tcol), fmap), pl.BlockSpec((tr, tcol), same)],
            out_specs=pl.BlockSpec((tr, tcol), same)),
        out_shape=_sds((r, c), BF16), compiler_params=_params("parallel", "parallel"),
    )(core, full, other)


def _sum_regions(pair, got, place, col_sharded, name):
    n_got, r, c = got.shape
    tr, tcol = _tile(r, 512, 16), _tile(c, WIDE_TILE, LANES)
    nrb, ncb = r // tr, c // tcol

    def body(place_ref, p_ref, g_ref, out_ref):
        tot = p_ref[...].astype(F32)
        for k in range(n_got):
            tot = tot + g_ref[k].astype(F32)
        out_ref[...] = tot

    if col_sharded:
        pmap = lambda i, j, pr: (i, pr[0] * ncb + j)
        omap = lambda i, j, pr: (pr[1] * nrb + i, j)
        out_shape = (2 * r, c)
    else:
        pmap = lambda i, j, pr: (pr[0] * nrb + i, j)
        omap = lambda i, j, pr: (i, pr[1] * ncb + j)
        out_shape = (r, 2 * c)
    return pl.pallas_call(
        body, name=name,
        grid_spec=pltpu.PrefetchScalarGridSpec(
            num_scalar_prefetch=1, grid=(nrb, ncb),
            in_specs=[pl.BlockSpec((tr, tcol), pmap), pl.BlockSpec((n_got, tr, tcol), lambda i, j, pr: (0, i, j))],
            out_specs=pl.BlockSpec((tr, tcol), omap)),
        out_shape=_sds(out_shape), compiler_params=_params("parallel", "parallel"),
    )(place, pair, got)


def _fold_forwarded(pair, fold, nbr_chips, name):
    _, r, c = fold.shape
    tr, tcol = _tile(r, 512, 16), _tile(c, WIDE_TILE, LANES)
    nrb, ncb = r // tr, c // tcol

    def body(nbr_ref, px_ref, py_ref, f_ref, out_ref):
        out_ref[0] = (px_ref[...].astype(F32) + f_ref[1].astype(F32)).astype(out_ref.dtype)
        out_ref[1] = (py_ref[...].astype(F32) + f_ref[0].astype(F32)).astype(out_ref.dtype)

    both = pl.BlockSpec((2, tr, tcol), lambda i, j, nb: (0, i, j))
    return pl.pallas_call(
        body, name=name,
        grid_spec=pltpu.PrefetchScalarGridSpec(
            num_scalar_prefetch=1, grid=(nrb, ncb),
            in_specs=[pl.BlockSpec((tr, tcol), lambda i, j, nb: (i, nb[0] * ncb + j)),
                      pl.BlockSpec((tr, tcol), lambda i, j, nb: (nrb + i, nb[1] * ncb + j)), both],
            out_specs=both),
        out_shape=_sds(fold.shape, fold.dtype), compiler_params=_params("parallel", "parallel"),
    )(nbr_chips, pair, pair, fold)


def _adamw(w, g, m, v, name, copy_grad=True):
    r, c = w.shape
    tr, tcol = _tile(r, 256, SUBLANES), _tile(c, WIDE_TILE, LANES)
    blk = pl.BlockSpec((tr, tcol), lambda i, j: (i, j))
    n_out = 4 if copy_grad else 3

    def body(w_ref, g_ref, m_ref, v_ref, d_ref, nm_ref, nv_ref, *g_out_ref):
        g_ = g_ref[...]
        if copy_grad:
            g_out_ref[0][...] = g_
        m_ = ADAM_B1 * m_ref[...] + (1.0 - ADAM_B1) * g_
        v_ = ADAM_B2 * v_ref[...] + (1.0 - ADAM_B2) * (g_ * g_)
        m_hat = m_ / (1.0 - ADAM_B1 ** ADAM_STEP)
        v_hat = v_ / (1.0 - ADAM_B2 ** ADAM_STEP)
        d_ref[...] = -ADAM_LR * (m_hat / (jnp.sqrt(v_hat) + ADAM_EPS) + ADAM_WD * w_ref[...])
        nm_ref[...] = m_
        nv_ref[...] = v_

    return _call(body, name=name, grid=(r // tr, c // tcol), in_specs=[blk] * 4, out_specs=[blk] * n_out,
                 out_shape=[_sds((r, c))] * n_out, sem=("parallel", "parallel"))(w, g, m, v)


def _place():
    x, y, c = lax.axis_index("x"), lax.axis_index("y"), lax.axis_index("c")
    chips = [(1 - x, y), (x, 1 - y), (1 - x, 1 - y)]
    return x, y, c, chips


def _all_gather8(blk, name):
    m, n = blk.shape

    def body(x_ref, out_ref, send_sems, recv_sems, local_sem):
        x, y, c, chips = _place()
        me, sibling = (x, y, c), (x, y, 1 - c)

        def rows(px, py, pc):
            return out_ref.at[pl.ds((4 * px + 2 * py + pc) * m, m), :]

        def copy(k, block, to, src=None):
            return pltpu.make_async_remote_copy(
                src_ref=rows(*block) if src is None else src, dst_ref=rows(*block), send_sem=send_sems.at[k],
                recv_sem=recv_sems.at[k], device_id=to, device_id_type=MESH_ID)

        mine = pltpu.make_async_copy(x_ref, rows(*me), local_sem)
        mine.start()
        first = [copy(0, me, sibling, src=x_ref)]
        first += [copy(1 + j, me, (*chip, c), src=x_ref) for j, chip in enumerate(chips)]
        for cp in first:
            cp.start()
        passed = [copy(4 + j, (*chip, c), sibling) for j, chip in enumerate(chips)]
        for j, chip in enumerate(chips):
            copy(1 + j, (*chip, c), me).wait_recv()
            passed[j].start()
        copy(0, sibling, me).wait_recv()
        for j, chip in enumerate(chips):
            copy(4 + j, (*chip, 1 - c), me).wait_recv()
        for cp in first + passed:
            cp.wait_send()
        mine.wait()

    return pl.pallas_call(
        body, name=name, out_shape=_sds((N_DEV * m, n), blk.dtype), in_specs=[ANY], out_specs=ANY,
        scratch_shapes=[pltpu.SemaphoreType.DMA((7,)), pltpu.SemaphoreType.DMA((7,)), pltpu.SemaphoreType.DMA],
    )(blk)


def _half(ref, core, split_rows):
    r, c = ref.shape
    if split_rows:
        return ref.at[pl.ds(core * (r // 2), r // 2), :]
    return ref.at[:, pl.ds(core * (c // 2), c // 2)]


def _chip_block(ref, j, col_sharded):
    r, c = ref.shape
    if col_sharded:
        return ref.at[:, pl.ds(j * (c // N_CHIPS), c // N_CHIPS)]
    return ref.at[pl.ds(j * (r // N_CHIPS), r // N_CHIPS), :]


def _rows_part(ref, part):
    lo, hi, n = part
    r = ref.shape[0]
    return ref if (lo, hi) == (0, n) else ref.at[pl.ds(lo * (r // n), (hi - lo) * (r // n)), :]


def _copy(send_sems, recv_sems, k, src, dst, to):
    return pltpu.make_async_remote_copy(src_ref=src, dst_ref=dst, send_sem=send_sems.at[k], recv_sem=recv_sems.at[k],
                                        device_id=to, device_id_type=MESH_ID)


def _in_place(arrays):
    return tuple(arrays), tuple(_sds(a.shape, a.dtype) for a in arrays), {i: i for i in range(len(arrays))}


def _gather_ici(fulls, col_sharded, part=(0, 1, 1)):
    nw = len(fulls)

    def build(_, refs, send_sems, recv_sems, sem0):
        x, y, c, chips = _place()
        sends, recvs = [], []
        for w in range(nw):
            win = lambda j: _rows_part(_half(_chip_block(refs[w], j, col_sharded[w]), c, True), part)
            for k, (cx, cy) in enumerate(chips):
                sem = sem0 + 3 * w + k
                sends.append(_copy(send_sems, recv_sems, sem, win(2 * x + y), win(2 * x + y), (cx, cy, c)))
                recvs.append(_copy(send_sems, recv_sems, sem, win(2 * cx + cy), win(2 * cx + cy), (cx, cy, c)))
        return sends, recvs

    return _Side(*_in_place(fulls), 3 * nw, build)


def _gather_d2d(fulls, col_sharded):
    nw = len(fulls)

    def build(_, refs, send_sems, recv_sems, sem0):
        x, y, c, chips = _place()
        sends, recvs = [], []
        for w in range(nw):
            win = lambda j, core: _half(_chip_block(refs[w], j, col_sharded[w]), core, True)
            for k, (cx, cy) in enumerate(chips):
                sem = sem0 + 3 * w + k
                sends.append(_copy(send_sems, recv_sems, sem, win(2 * cx + cy, c), win(2 * cx + cy, c), (x, y, 1 - c)))
                recvs.append(_copy(send_sems, recv_sems, sem, win(2 * cx + cy, 1 - c), win(2 * cx + cy, 1 - c),
                                   (x, y, 1 - c)))
        return sends, recvs

    return _Side(*_in_place(fulls), 3 * nw, build)


def _gather_neighbours(fulls, col_sharded, part=(0, 1, 1)):
    nw = len(fulls)

    def build(_, refs, send_sems, recv_sems, sem0):
        x, y, c, chips = _place()
        sends, recvs = [], []
        for w in range(nw):
            win = lambda j: _rows_part(_half(_chip_block(refs[w], j, col_sharded[w]), c, True), part)
            for k, (cx, cy) in enumerate(chips[:2]):
                sem = sem0 + 2 * w + k
                sends.append(_copy(send_sems, recv_sems, sem, win(2 * x + y), win(2 * x + y), (cx, cy, c)))
                recvs.append(_copy(send_sems, recv_sems, sem, win(2 * cx + cy), win(2 * cx + cy), (cx, cy, c)))
        return sends, recvs

    return _Side(*_in_place(fulls), 2 * nw, build)


def _gather_forward(fulls, col_sharded, ways=(True, True), siblings=False):
    nw = len(fulls)

    def build(_, refs, send_sems, recv_sems, sem0):
        x, y, c, (cx_, cy_, cd_) = _place()
        sends, recvs = [], []
        for w in range(nw):
            win = lambda chip, core: _half(_chip_block(refs[w], 2 * chip[0] + chip[1], col_sharded[w]), core, True)
            part = lambda ref, p: _rows_part(ref, (p, p + 1, 2))
            for p, (src, to) in enumerate(((cx_, cy_), (cy_, cx_))):
                if ways[p]:
                    sem = sem0 + 4 * w + p
                    sends.append(_copy(send_sems, recv_sems, sem, part(win(src, c), p), part(win(src, c), p), (*to, c)))
                    recvs.append(_copy(send_sems, recv_sems, sem, part(win(cd_, c), p), part(win(cd_, c), p), (*to, c)))
            if siblings:
                for k, chip in enumerate((cx_, cy_)):
                    sem = sem0 + 4 * w + 2 + k
                    sends.append(_copy(send_sems, recv_sems, sem, win(chip, c), win(chip, c), (x, y, 1 - c)))
                    recvs.append(_copy(send_sems, recv_sems, sem, win(chip, 1 - c), win(chip, 1 - c), (x, y, 1 - c)))
        return sends, recvs

    return _Side(*_in_place(fulls), 4 * nw, build)


def _gather_diagonal_d2d(full, col_sharded):
    def build(_, refs, send_sems, recv_sems, sem0):
        x, y, c, chips = _place()
        cx, cy = chips[2]
        win = lambda core: _half(_chip_block(refs[0], 2 * cx + cy, col_sharded), core, True)
        return ([_copy(send_sems, recv_sems, sem0, win(c), win(c), (x, y, 1 - c))],
                [_copy(send_sems, recv_sems, sem0, win(1 - c), win(1 - c), (x, y, 1 - c))])

    return _Side(*_in_place([full]), 1, build)


def _scatter_forward_1(pair):
    r, n = pair.shape
    results = (_sds((2, r, n // N_CHIPS), pair.dtype), _sds((2, r // 2, n // N_CHIPS), pair.dtype))

    def build(refs, res, send_sems, recv_sems, sem0):
        x, y, c, (cx_, cy_, cd_) = _place()
        region = lambda chip, p: _rows_part(_chip_block(refs[0], 2 * chip[0] + chip[1], True), (p, p + 1, 2))
        got, fold = res
        copies = [
            _copy(send_sems, recv_sems, sem0, region(cx_, 1), _rows_part(got.at[0], (1, 2, 2)), (*cx_, c)),
            _copy(send_sems, recv_sems, sem0 + 1, region(cd_, 1), fold.at[0], (*cx_, c)),
            _copy(send_sems, recv_sems, sem0 + 2, region(cy_, 0), _rows_part(got.at[1], (0, 1, 2)), (*cy_, c)),
            _copy(send_sems, recv_sems, sem0 + 3, region(cd_, 0), fold.at[1], (*cy_, c))]
        return copies, copies

    return _Side((pair,), results, {}, 4, build)


def _scatter_forward_2(passed, got):
    def build(refs, res, send_sems, recv_sems, sem0):
        x, y, c, (cx_, cy_, _) = _place()
        copies = [_copy(send_sems, recv_sems, sem0, refs[0].at[0], _rows_part(res[0].at[0], (0, 1, 2)), (*cx_, c)),
                  _copy(send_sems, recv_sems, sem0 + 1, refs[0].at[1], _rows_part(res[0].at[1], (1, 2, 2)), (*cy_, c))]
        return copies, copies

    return _Side((passed, got), (_sds(got.shape, got.dtype),), {1: 0}, 2, build)


def _exchange(side, name):
    return _call(None, side=side, name=name)()[1]


def _swap_halves(grads, col_sharded):
    nw = len(grads)
    out_shapes = [_sds((g.shape[0] // 2, g.shape[1]) if col else (g.shape[0], g.shape[1] // 2), g.dtype)
                  for g, col in zip(grads, col_sharded)]

    def build(g_refs, o_refs, send_sems, recv_sems, sem0):
        x, y, c, _ = _place()
        copies = [_copy(send_sems, recv_sems, sem0 + w, _half(g_refs[w], 1 - c, col_sharded[w]), o_refs[w],
                        (x, y, 1 - c)) for w in range(nw)]
        return copies, copies

    return _Side(tuple(grads), tuple(out_shapes), {}, nw, build)


def _scatter_regions(pairs, col_sharded, part=(0, 1, 1), into=None):
    nw = len(pairs)

    def region_shape(p, col):
        return (p.shape[0], p.shape[1] // N_CHIPS) if col else (p.shape[0] // N_CHIPS, p.shape[1])

    out_shapes = tuple(_sds((N_CHIPS - 1, *region_shape(p, col)), p.dtype) for p, col in zip(pairs, col_sharded))

    def build(refs, o_refs, send_sems, recv_sems, sem0):
        x, y, c, chips = _place()
        copies = []
        for w in range(nw):
            for k, (cx, cy) in enumerate(chips):
                copies.append(_copy(
                    send_sems, recv_sems, sem0 + 3 * w + k,
                    _rows_part(_chip_block(refs[w], 2 * cx + cy, col_sharded[w]), part),
                    _rows_part(o_refs[w].at[k], part), (cx, cy, c)))
        return copies, copies

    if into is None:
        return _Side(tuple(pairs), out_shapes, {}, 3 * nw, build)
    return _Side((*pairs, *into), out_shapes, {nw + w: w for w in range(nw)}, 3 * nw, build)


def _join_halves(halves, col_sharded):
    nw = len(halves)

    def build(_, refs, send_sems, recv_sems, sem0):
        x, y, c, _ = _place()
        sends, recvs = [], []
        for w in range(nw):
            mine, theirs = _half(refs[w], c, col_sharded[w]), _half(refs[w], 1 - c, col_sharded[w])
            sends.append(_copy(send_sems, recv_sems, sem0 + w, mine, mine, (x, y, 1 - c)))
            recvs.append(_copy(send_sems, recv_sems, sem0 + w, theirs, theirs, (x, y, 1 - c)))
        return sends, recvs

    return _Side(*_in_place(halves), nw, build)


def _part_rows(size):
    return -(-size // (SUBLANES * LANES)) * SUBLANES


def _pack(arrays, pad_rows_to=SUBLANES):
    flat = [jnp.pad(a.reshape(-1), (0, _part_rows(a.size) * LANES - a.size)).reshape(-1, LANES) for a in arrays]
    rows = sum(f.shape[0] for f in flat)
    pad = (-rows) % pad_rows_to
    if pad:
        flat.append(jnp.zeros((pad, LANES), F32))
    return jnp.concatenate(flat, axis=0)


def _unpack(packed, shapes):
    out, r = [], 0
    for shp in shapes:
        size = math.prod(shp)
        out.append(packed[r:r + _part_rows(size)].reshape(-1)[:size].reshape(shp))
        r += _part_rows(size)
    return out


def _rope_tables(n_ctx_rows, s):
    rows = s // GRID_W
    row_idx = jnp.repeat(jnp.arange(rows), GRID_W)
    col_idx = jnp.tile(jnp.arange(GRID_W), rows)
    n_freq = LANES // 4
    inv_freq = ROPE_THETA ** (-jnp.arange(n_freq, dtype=F32) / n_freq)
    ang = jnp.concatenate([row_idx.astype(F32)[:, None] * inv_freq, col_idx.astype(F32)[:, None] * inv_freq], axis=-1)
    cos = jnp.repeat(jnp.cos(ang), 2, axis=-1)
    sin = jnp.repeat(jnp.sin(ang), 2, axis=-1) * jnp.tile(jnp.array([-1.0, 1.0], F32), LANES // 2)
    cos = jnp.concatenate([jnp.ones((n_ctx_rows, LANES), F32), cos], axis=0)
    sin = jnp.concatenate([jnp.zeros((n_ctx_rows, LANES), F32), sin], axis=0)
    return cos, sin


WEIGHT_NAMES = ['c_ctx', 'w_mod', 'b_mod', 'g_mix', 'g_mlp', 'w_in', 'q_gain', 'k_gain', 'conv_w', 'conv_b', 'w_rg',
                'b_rg', 'w_ig', 'b_ig', 'lru_lambda', 'w_o_attn', 'w_o_rnn', 'w_out', 'w_up', 'w_down', 'g_final']
BIG = ['w_in', 'w_o_attn', 'w_o_rnn', 'w_out', 'w_up', 'w_down']
BIG_COL_SHARDED = [True, False, False, False, True, False]
GATES = ['w_rg', 'w_ig']
SMALL = ['c_ctx', 'b_mod', 'g_mix', 'g_mlp', 'q_gain', 'k_gain', 'conv_b', 'g_final',
         'conv_w', 'b_rg', 'b_ig', 'lru_lambda']


def kernel(x, c, ctx, c_ctx, w_mod, b_mod, g_mix, g_mlp, w_in, q_gain, k_gain, conv_w, conv_b, w_rg, b_rg, w_ig, b_ig, lru_lambda, w_o_attn, w_o_rnn, w_out, w_up, w_down, g_final, loss_target, m_c_ctx, m_w_mod, m_b_mod, m_g_mix, m_g_mlp, m_w_in, m_q_gain, m_k_gain, m_conv_w, m_conv_b, m_w_rg, m_b_rg, m_w_ig, m_b_ig, m_lru_lambda, m_w_o_attn, m_w_o_rnn, m_w_out, m_w_up, m_w_down, m_g_final, v_c_ctx, v_w_mod, v_b_mod, v_g_mix, v_g_mlp, v_w_in, v_q_gain, v_k_gain, v_conv_w, v_conv_b, v_w_rg, v_b_rg, v_w_ig, v_b_ig, v_lru_lambda, v_w_o_attn, v_w_o_rnn, v_w_out, v_w_up, v_w_down, v_g_final):
    given = dict(locals())
    weights = {n: given[n] for n in WEIGHT_NAMES}
    moms = {n: given["m_" + n] for n in WEIGHT_NAMES}
    vars_ = {n: given["v_" + n] for n in WEIGHT_NAMES}

    s, d = x.shape[1], x.shape[2]
    n_ctx = ctx.shape[1]
    t = n_ctx + s
    hd = q_gain.shape[1]
    assert hd == LANES and w_rg.shape[-1] == LANES
    attn_w = w_o_attn.shape[1] * N_CHIPS
    n_in = w_in.shape[2] * N_CHIPS
    kv_w = (n_in - attn_w - 4 * d) // 2
    group = attn_w // kv_w
    k_off, v_off, xr_off = attn_w, attn_w + kv_w, attn_w + 2 * kv_w
    xg_off, gl_off = xr_off + d, xr_off + 2 * d
    d_mod = N_MOD * d
    tr = _tile(math.gcd(n_ctx, s), 256, 16)
    tcol = _tile(math.gcd(d, xr_off), 1024, LANES)
    xi, yi, ci = lax.axis_index("x"), lax.axis_index("y"), lax.axis_index("c")
    chip = 2 * xi + yi
    core = ci.astype(jnp.int32).reshape(1)

    sharded_small = [conv_w[0], b_rg[0], b_ig[0], lru_lambda[0]]
    pack0 = _pack([c[0]] + sharded_small)
    got0 = _all_gather8(pack0, "gather_small_inputs").reshape(N_DEV, -1, LANES)
    c_all = got0[:, :_part_rows(d)].reshape(N_DEV, -1)[:, :d]
    per_chip = [_unpack(got0[2 * j, _part_rows(d):], [a.shape for a in sharded_small]) for j in range(N_CHIPS)]
    conv_w_f, b_rg_f, b_ig_f, lam_f = (jnp.concatenate([per_chip[j][i] for j in range(N_CHIPS)], axis=-1)
                                       for i in range(4))
    c16 = jnp.concatenate([c_all, c_ctx[None, :], jnp.zeros((16 - N_DEV - 1, d), F32)], axis=0)
    b_mod_shard = lax.dynamic_slice(b_mod, (0, chip * (d_mod // N_CHIPS)), (1, d_mod // N_CHIPS))
    mod_part, silu16 = _mod_fwd(c16, w_mod[0], b_mod_shard)
    mod_all = _all_gather8(mod_part, "gather_mod").reshape(N_DEV, 16, d_mod // N_CHIPS)
    mod16 = jnp.concatenate([mod_all[2 * j] for j in range(N_CHIPS)], axis=-1)
    me = 4 * xi + 2 * yi + ci
    mod_lat = lax.dynamic_slice(mod16, (me, 0), (1, d_mod)).reshape(N_MOD, d)
    mod_ctx = mod16[N_DEV].reshape(N_MOD, d)
    mod4 = jnp.stack([mod_ctx[0], mod_ctx[1], mod_lat[0], mod_lat[1]])
    mod3 = jnp.stack([mod_lat[2], mod_lat[3], mod_lat[4]])
    gate_f = mod_lat[5][None, :]

    chip_arr = chip.astype(jnp.int32).reshape(1)
    own = {n: _cast_into_window(weights[n][0], chip_arr, col, "cast_" + n) for n, col in zip(BIG, BIG_COL_SHARDED)}
    place = jnp.stack([chip, ci]).astype(jnp.int32)
    row3 = [False] * 3

    def pair_sum(n, full, other, col):
        return _add_half(full, other, core, col, "pair_sum_" + n)

    def chip_sum(n, pair, got, col):
        return _sum_regions(pair, got, place, col, "chip_sum_" + n)

    (w_in_f,) = _exchange(_gather_neighbours([own['w_in']], [True]), "gather_w_in_nbr")
    (w_in_f,) = _exchange(_gather_forward([w_in_f], [True], siblings=True), "gather_w_in_fwd")
    (w_in_f,) = _exchange(_gather_diagonal_d2d(w_in_f, True), "gather_w_in_diag")
    cos, sin = _rope_tables(n_ctx, s)
    h = _norm_mod_fwd(ctx[0], x[0], g_mix, mod4, tr)
    z, (w_oa_f, w_or_f, w_out_f, w_up_f) = _matmul(h, w_in_f, name="mm_in", side=_sides(
        _gather_neighbours([own['w_o_attn'], own['w_o_rnn'], own['w_out']], row3),
        _gather_neighbours([own['w_up']], [True], (0, 1, 2))))
    qr, (w_oa_f, w_or_f, w_out_f) = _head_prep_fwd(z, 0, attn_w // LANES, q_gain, cos, sin, tr, "q_prep",
                                                   side=_gather_forward([w_oa_f, w_or_f, w_out_f], row3))
    kr = _head_prep_fwd(z, k_off, kv_w // LANES, k_gain, cos, sin, tr, "k_prep")
    (attn_o, lse), (w_up_f, w_down_f, w_oa_f, w_or_f, w_out_f) = _attn_fwd(
        qr, kr, z, v_off, n_ctx, group, tr, side=_sides(
            _on_same(_gather_neighbours([w_up_f], [True], (1, 2, 2)), _gather_forward([w_up_f], [True], ways=(True, False))),
            _gather_neighbours([own['w_down']], [False]), _gather_d2d([w_oa_f, w_or_f, w_out_f], row3)))
    (xc, a_f, bx_f, a_b, bx_b), (w_up_f,) = _rnn_prep(
        z, xr_off, conv_w_f, conv_b, w_rg[0], b_rg_f, w_ig[0], b_ig_f, lam_f, n_ctx,
        side=_gather_forward([w_up_f], [True], ways=(False, True)))
    (h_f, h_b), (w_up_f, w_down_f) = _scan(
        [(a_f, bx_f, "ctx_lat_up"), (a_b, bx_b, "ctx_lat_down")], post=False, n_ctx_rows=n_ctx, name="scan_fwd",
        side=_sides(_gather_d2d([w_up_f], [True]), _gather_forward([w_down_f], [False])))
    u = _rnn_gate_fwd(h_f, h_b, z, xg_off, n_ctx, tr, tcol)
    y_attn = _matmul(attn_o, w_oa_f, name="mm_o_attn")
    y_rnn = _matmul(u, w_or_f, name="mm_o_rnn")
    mrg = _merge_fwd(y_attn, y_rnn, z, gl_off, n_ctx, tr, tcol)
    mix = _matmul(mrg, w_out_f, name="mm_out")
    x1, h2 = _resid_norm_mod_fwd(x[0], mix, g_mlp, mod3, tr)
    up, (w_down_f,) = _matmul(h2, w_up_f, name="mm_up", side=_gather_d2d([w_down_f], [False]))
    act = _sq_relu(up, tr, tcol)
    down = _matmul(act, w_down_f, name="mm_down")
    dx2, d_down, sums_fin, loss_blk = _final_fwd_bwd(x1, down, loss_target[0], g_final[None, :], gate_f, tr)

    d_up = _matmul(d_down, w_down_f, tb=True, out_dtype=BF16, name="mm_d_up",
                   post=lambda d_act, up_: d_act * 2.0 * jnp.maximum(up_, 0.0), post_args=(up,))
    g_w_down = _matmul(act, d_down, ta=True, out_dtype=BF16, name="mm_g_down")

    def scatter(pairs, cols, lo, hi, into=None):
        return _scatter_regions(pairs, cols, (lo, hi, 8), into)

    dh2, (got,) = _matmul(d_up, w_up_f, tb=True, name="mm_d_h2", side=_swap_halves([g_w_down], [False]))
    p_down = pair_sum('w_down', g_w_down, got, False)
    g_w_up, got_down = _matmul(h2, d_up, ta=True, out_dtype=BF16, name="mm_g_up", side=scatter([p_down], [False], 0, 3))
    (dx1, d_mix, sums2), (got, *got_down) = _norm_mod_bwd2(x1, dh2, dx2, mix, g_mlp, mod3, tr, side=_sides(
        _swap_halves([g_w_up], [True]), scatter([p_down], [False], 3, 4, got_down)))
    p_up = pair_sum('w_up', g_w_up, got, True)
    d_mrg = _matmul(d_mix, w_out_f, tb=True, name="mm_d_mrg")
    g_w_out = _matmul(mrg, d_mix, ta=True, out_dtype=BF16, name="mm_g_out")
    dz = _dz_start(t, n_in, n_ctx)
    d_ya, dz = _merge_bwd(d_mrg, y_attn, z, gl_off, dz, n_ctx, tr, tcol, "merge_bwd_attn")
    d_yr, dz = _merge_bwd(d_mrg, y_rnn, z, gl_off + d, dz, n_ctx, tr, tcol, "merge_bwd_rnn")
    d_o = _matmul(d_ya, w_oa_f, tb=True, out_dtype=BF16, name="mm_d_o")
    g_w_oa = _matmul(attn_o, d_ya, ta=True, out_dtype=BF16, name="mm_g_o_attn")
    d_u = _matmul(d_yr, w_or_f, tb=True, name="mm_d_u")
    g_w_or = _matmul(u, d_yr, ta=True, out_dtype=BF16, name="mm_g_o_rnn")
    d_rnn, dz = _rnn_gate_bwd(d_u, h_f, h_b, z, xg_off, dz, n_ctx, tr, tcol)
    gs_f, gs_b = _scan([(a_f, d_rnn, "lat_ctx_down"), (a_b, d_rnn, "lat_ctx_up")], post=True, n_ctx_rows=n_ctx,
                       name="scan_bwd")
    o_names, o_grads = ['w_o_attn', 'w_o_rnn', 'w_out'], [g_w_oa, g_w_or, g_w_out]
    (dz, g_w_rg, g_w_ig, sums_rnn), (got_down, *got_o) = _rnn_bwd(
        z, xr_off, xc, gs_f, gs_b, h_f, h_b, conv_w_f, w_rg[0], b_rg_f, w_ig[0], b_ig_f, lam_f, dz, n_ctx,
        side=_sides(scatter([p_down], [False], 4, 8, got_down), _swap_halves(o_grads, row3)))
    hs_down = chip_sum('w_down', p_down, got_down, False)
    p_o = [pair_sum(n, g, o, False) for n, g, o in zip(o_names, o_grads, got_o)]
    gate_cols = 8 * LANES if g_w_rg.size % (8 * LANES * N_CHIPS * 16) == 0 else 2 * LANES
    gate_rows = g_w_rg.size // gate_cols
    gate_grads = [g_w_rg.reshape(gate_rows, gate_cols), g_w_ig.reshape(gate_rows, gate_cols)]
    (dq, dk, dz), (got_up, got_oa, got_rg, got_ig, gs_down) = _attn_bwd(
        qr, kr, z, v_off, d_o, attn_o, lse, dz, n_ctx, group, tr, side=_sides(
            _scatter_regions([p_up], [True]), _scatter_regions(p_o[:1], row3[:1]), _swap_halves(gate_grads, row3[:2]),
            _join_halves([hs_down], [False])))
    hs_up = chip_sum('w_up', p_up, got_up, True)
    hs_oa = chip_sum('w_o_attn', p_o[0], got_oa, False)
    p_gate = [pair_sum(n, g, o, False) for n, g, o in zip(GATES, gate_grads, (got_rg, got_ig))]
    dz, g_q_gain = _head_prep_bwd(z, 0, attn_w // LANES, q_gain, cos, sin, dq, n_ctx, tr, dz, "q_prep_bwd")
    dz, g_k_gain = _head_prep_bwd(z, k_off, kv_w // LANES, k_gain, cos, sin, dk, 0, tr, dz, "k_prep_bwd")
    g_w_in, (got_or, got_out, got_rg, got_ig, gs_up, gs_oa) = _matmul(
        h, dz, ta=True, out_dtype=BF16, name="mm_g_in", side=_sides(
            _scatter_regions(p_o[1:] + p_gate, [False] * 4), _join_halves([hs_up], [True]), _join_halves([hs_oa], [False])))
    hs_late = [chip_sum(n, p, o, False)
               for n, p, o in zip(o_names[1:] + GATES, p_o[1:] + p_gate, (got_or, got_out, got_rg, got_ig))]
    (got,) = _exchange(_swap_halves([g_w_in], [True]), "swap_w_in")
    p_in = pair_sum('w_in', g_w_in, got, True)
    dh, (got_in, fold_in) = _matmul(dz, w_in_f, tb=True, name="mm_d_h", side=_scatter_forward_1(p_in))
    nbr_chips = jnp.stack([2 * (1 - xi) + yi, 2 * xi + 1 - yi]).astype(jnp.int32)
    passed = _fold_forwarded(p_in, fold_in, nbr_chips, "fold_w_in")
    (got_in,) = _exchange(_scatter_forward_2(passed, got_in), "scatter_w_in_fwd")
    grad_x, sums1 = _norm_mod_bwd1(ctx[0], x[0], dh, dx1, g_mix, mod4, tr)

    zeros_d = jnp.zeros((d,), F32)
    dmod_lat = jnp.concatenate([sums1[0], sums1[1], sums2[3], sums2[0], sums2[1], sums_fin[1]])
    dmod_ctx = jnp.concatenate([sums1[3], sums1[4]] + [zeros_d] * 4)
    small_parts = [dmod_lat, dmod_ctx, loss_blk[0, 0:1], sums1[2] + sums1[5], sums2[2], g_q_gain[0], g_k_gain[0],
                   sums_rnn[10], sums_fin[0], sums_rnn[6:10], sums_rnn[0:2], sums_rnn[2:4], sums_rnn[4:6]]
    pack1 = _pack(small_parts)
    got1 = _all_gather8(pack1, "gather_small_grads").reshape(N_DEV, -1, LANES)
    tot1 = _sum_leading(got1, "sum_small_grads")
    part_shapes = [a.shape for a in small_parts]
    (s_dmod_lat, s_dmod_ctx, s_loss, g_g_mix, g_g_mlp, g_q_gain, g_k_gain, g_conv_b, g_g_final,
     g_conv_w_f, g_b_rg_f, g_b_ig_f, g_lam_f) = _unpack(tot1, part_shapes)
    loss = s_loss[0]
    g_b_mod = (s_dmod_lat + s_dmod_ctx)[None, :]
    n_mod_rows = _part_rows(d_mod)
    dmod16 = jnp.concatenate([got1[:, :n_mod_rows].reshape(N_DEV, -1)[:, :d_mod], s_dmod_ctx[None, :],
                              jnp.zeros((16 - N_DEV - 1, d_mod), F32)], axis=0)
    dmod16_shard = lax.dynamic_slice(dmod16, (0, chip * (d_mod // N_CHIPS)), (16, d_mod // N_CHIPS))
    g_w_mod = _matmul(silu16, dmod16_shard, ta=True, name="mm_g_mod")
    dsilu_part = _matmul(dmod16_shard[N_DEV:], w_mod[0], tb=True, name="mm_d_silu")
    dsilu_all = _all_gather8(dsilu_part, "gather_d_silu").reshape(N_DEV, 8, d)
    g_c_ctx = _c_ctx_grad(dsilu_all, c_ctx[None, :])[0]

    def shard_of(full):
        w = full.shape[-1] // N_CHIPS
        return lax.dynamic_slice(full, (0, chip * w), (full.shape[0], w))

    grads = {
        'c_ctx': g_c_ctx, 'b_mod': g_b_mod, 'g_mix': g_g_mix[None, :], 'g_mlp': g_g_mlp[None, :],
        'q_gain': g_q_gain[None, :], 'k_gain': g_k_gain[None, :], 'conv_b': g_conv_b[None, :],
        'g_final': g_g_final,
        'conv_w': shard_of(g_conv_w_f)[None], 'b_rg': shard_of(g_b_rg_f)[None], 'b_ig': shard_of(g_b_ig_f)[None],
        'lru_lambda': shard_of(g_lam_f)[None], 'w_mod': g_w_mod[None],
    }

    delta, new_m, new_v = {}, {}, {}

    def adamw(n):
        shp = weights[n].shape
        as2d = (lambda a: a[0]) if n not in GATES else (lambda a: a.reshape(-1, LANES))
        dl, nm, nv, *g = _adamw(as2d(weights[n]), as2d(grads[n]), as2d(moms[n]), as2d(vars_[n]), "adamw_" + n,
                                copy_grad=n in BIG)
        delta[n], new_m[n], new_v[n] = dl.reshape(shp), nm.reshape(shp), nv.reshape(shp)
        if g:
            grads[n] = g[0].reshape(shp)

    hs_in = chip_sum('w_in', p_in, got_in, True)
    gs_in, gs_or, gs_out = _exchange(_join_halves([hs_in] + hs_late[:2], [True, False, False]), "join_rest")
    for n, g in zip(['w_in', 'w_o_attn', 'w_o_rnn', 'w_out', 'w_up', 'w_down'], [gs_in, gs_oa, gs_or, gs_out, gs_up, gs_down]):
        grads[n] = g[None]
    half_cols = gate_cols // 2
    mine = jnp.concatenate([lax.dynamic_slice(hs, (0, ci * half_cols), (gate_rows // N_CHIPS, half_cols))
                            for hs in hs_late[2:]], axis=0)
    gate_all = _all_gather8(mine, "gather_gate_grads")
    gate_all = gate_all.reshape(N_CHIPS, 2, len(GATES), gate_rows // N_CHIPS, half_cols)
    for i, n in enumerate(GATES):
        grads[n] = jnp.moveaxis(gate_all[:, :, i], 1, 2).reshape(weights[n].shape)
    for n in ['w_mod'] + BIG + GATES:
        adamw(n)
    small_shapes = [weights[n].shape for n in SMALL]
    packed = [_pack([src[n] for n in SMALL], 512) for src in (weights, grads, moms, vars_)]
    outs = _adamw(*packed, "adamw_small", copy_grad=False)
    for res, out in zip((delta, new_m, new_v), outs):
        for n, a in zip(SMALL, _unpack(out, small_shapes)):
            res[n] = a
    return (loss, grad_x[None], *[grads[n] for n in WEIGHT_NAMES], *[delta[n] for n in WEIGHT_NAMES],
            *[new_m[n] for n in WEIGHT_NAMES], *[new_v[n] for n in WEIGHT_NAMES])
```

```python
import functools
import math
from typing import Callable, NamedTuple

import jax
import jax.numpy as jnp
from jax import lax
from jax.experimental import pallas as pl
from jax.experimental.pallas import tpu as pltpu

F32 = jnp.float32
BF16 = jnp.bfloat16
MESH_ID = pl.DeviceIdType.MESH
ANY = pl.BlockSpec(memory_space=pl.ANY)

NORM_EPS = 1e-6
LRU_C = 8.0
GRID_W = 64
ROPE_THETA = 10000.0
N_MOD = 6
CONV_WIDTH = 4
ADAM_LR = 0.001
ADAM_B1 = 0.9
ADAM_B2 = 0.999
ADAM_EPS = 1e-08
ADAM_WD = 0.01
ADAM_STEP = 10

LANES = 128
SUBLANES = 8
V7X_VMEM_LIMIT = 48 * 1024 * 1024
WIDE_TILE = 11 * LANES
N_CHIPS = 4
N_DEV = 8
GELU_C = math.sqrt(2.0 / math.pi)
GELU_A = 0.044715


def _tile(dim, pref, align):
    t = min(pref, dim)
    t -= t % align
    while t >= align:
        if dim % t == 0:
            return t
        t -= align
    return dim


def _params(*sem):
    return pltpu.CompilerParams(dimension_semantics=sem, vmem_limit_bytes=V7X_VMEM_LIMIT)


def _sds(shape, dtype=F32):
    return jax.ShapeDtypeStruct(shape, dtype)


class _Side(NamedTuple):
    operands: tuple
    results: tuple
    aliases: dict
    n_sems: int
    build: Callable


def _sides(*sides):
    ops, res, aliases, spans, n = [], [], {}, [], 0
    for s in sides:
        spans.append((len(ops), len(res), n))
        aliases.update({len(ops) + i: len(res) + j for i, j in s.aliases.items()})
        ops += s.operands
        res += s.results
        n += s.n_sems

    def build(op_refs, res_refs, send_sems, recv_sems, sem0):
        sends, recvs = [], []
        for s, (o, r, k) in zip(sides, spans):
            a, b = s.build(op_refs[o:o + len(s.operands)], res_refs[r:r + len(s.results)], send_sems, recv_sems,
                           sem0 + k)
            sends += a
            recvs += b
        return sends, recvs

    return _Side(tuple(ops), tuple(res), aliases, n, build)


def _on_same(*sides):
    def build(ops, res, send_sems, recv_sems, sem0):
        sends, recvs = [], []
        for s in sides:
            a, b = s.build(ops, res, send_sems, recv_sems, sem0)
            sends += a
            recvs += b
            sem0 += s.n_sems
        return sends, recvs

    return _Side(sides[0].operands, sides[0].results, sides[0].aliases, sum(s.n_sems for s in sides), build)


def _call(body, *, side=None, sem=(), grid=(), in_specs=(), out_specs=(), out_shape=(), scratch_shapes=(), **kw):
    if side is None:
        return pl.pallas_call(body, grid=grid, in_specs=list(in_specs), out_specs=out_specs, out_shape=out_shape,
                              scratch_shapes=list(scratch_shapes), compiler_params=_params(*sem), **kw)
    aliases = kw.pop("input_output_aliases", {})
    many = isinstance(out_shape, (list, tuple))
    out_specs_l, out_shape_l = (list(out_specs), list(out_shape)) if many else ([out_specs], [out_shape])
    n_in, n_out, n_scr = len(in_specs), len(out_shape_l), len(scratch_shapes)
    n_op, n_res = len(side.operands), len(side.results)

    def hosted(*refs):
        ins, ops = refs[:n_in], refs[n_in:n_in + n_op]
        outs = refs[n_in + n_op:n_in + n_op + n_out]
        res = refs[n_in + n_op + n_out:n_in + n_op + n_out + n_res]
        scr = refs[n_in + n_op + n_out + n_res:-2]
        send_sems, recv_sems = refs[-2:]

        def start():
            for cp in side.build(ops, res, send_sems, recv_sems, 0)[0]:
                cp.start()

        def finish():
            sends, recvs = side.build(ops, res, send_sems, recv_sems, 0)
            for cp in recvs:
                cp.wait_recv()
            for cp in sends:
                cp.wait_send()

        if not grid:
            start()
            finish()
            return
        ids = [pl.program_id(a) for a in range(len(grid))]
        first = functools.reduce(jnp.logical_and, [i == 0 for i in ids])
        last = functools.reduce(jnp.logical_and, [i == g - 1 for i, g in zip(ids, grid)])
        pl.when(first)(start)
        body(*ins, *outs, *scr)
        pl.when(last)(finish)

    def run(*args):
        got = pl.pallas_call(
            hosted, grid=grid, in_specs=[*in_specs, *[ANY] * n_op], out_specs=[*out_specs_l, *[ANY] * n_res],
            out_shape=[*out_shape_l, *side.results],
            scratch_shapes=[*scratch_shapes, pltpu.SemaphoreType.DMA((side.n_sems,)),
                            pltpu.SemaphoreType.DMA((side.n_sems,))],
            input_output_aliases={**aliases, **{n_in + i: n_out + j for i, j in side.aliases.items()}},
            compiler_params=_params(*["arbitrary"] * len(grid)), **kw)(*args, *side.operands)
        own = list(got[:n_out]) if many else got[0]
        return own, list(got[n_out:])

    return run


def _matmul(a, b, *, ta=False, tb=False, out_dtype=F32, name, tm=1024, tn=1024, tk=2816, side=None, post=None,
            post_args=()):
    k_dim, m = a.shape if ta else a.shape[::-1]
    n, k2 = b.shape if tb else b.shape[::-1]
    assert k_dim == k2, (a.shape, b.shape, ta, tb)
    tm = _tile(m, tm, LANES if ta else 16)
    tn = _tile(n, tn, 16 if tb else LANES)
    tk = _tile(k_dim, tk, LANES)
    nk = k_dim // tk
    dims = (((0 if ta else 1,), (1 if tb else 0,)), ((), ()))
    if nk == 1:
        def whole(a_ref, b_ref, *rest):
            acc = lax.dot_general(a_ref[...].astype(BF16), b_ref[...].astype(BF16), dims, preferred_element_type=F32)
            if post is not None:
                acc = post(acc, *[r[...] for r in rest[:-1]])
            rest[-1][...] = acc.astype(rest[-1].dtype)

        a_spec = pl.BlockSpec((tk, tm), lambda i, j: (0, i)) if ta else pl.BlockSpec((tm, tk), lambda i, j: (i, 0))
        b_spec = pl.BlockSpec((tn, tk), lambda i, j: (j, 0)) if tb else pl.BlockSpec((tk, tn), lambda i, j: (0, j))
        o_spec = pl.BlockSpec((tm, tn), lambda i, j: (i, j))
        return _call(
            whole, side=side, name=name, grid=(m // tm, n // tn), in_specs=[a_spec, b_spec] + [o_spec] * len(post_args),
            out_specs=o_spec, out_shape=_sds((m, n), out_dtype), sem=("parallel", "parallel"),
        )(a, b, *post_args)
    assert post is None

    def body(a_ref, b_ref, o_ref, acc_ref):
        k = pl.program_id(2)

        @pl.when(k == 0)
        def _():
            acc_ref[...] = jnp.zeros_like(acc_ref)

        acc_ref[...] += lax.dot_general(a_ref[...].astype(BF16), b_ref[...].astype(BF16), dims,
                                        preferred_element_type=F32)

        @pl.when(k == nk - 1)
        def _():
            o_ref[...] = acc_ref[...].astype(o_ref.dtype)

    a_spec = pl.BlockSpec((tk, tm), lambda i, j, k: (k, i)) if ta else pl.BlockSpec((tm, tk), lambda i, j, k: (i, k))
    b_spec = pl.BlockSpec((tn, tk), lambda i, j, k: (j, k)) if tb else pl.BlockSpec((tk, tn), lambda i, j, k: (k, j))
    return _call(
        body, side=side, name=name, grid=(m // tm, n // tn, nk), in_specs=[a_spec, b_spec],
        out_specs=pl.BlockSpec((tm, tn), lambda i, j, k: (i, j)), out_shape=_sds((m, n), out_dtype),
        scratch_shapes=[pltpu.VMEM((tm, tn), F32)], sem=("parallel", "parallel", "arbitrary"),
    )(a, b)


def _silu(x):
    return x * jax.nn.sigmoid(x)


def _gelu(x):
    return 0.5 * x * (1.0 + jnp.tanh(GELU_C * (x + GELU_A * x * x * x)))


def _gelu_grad(x):
    t = jnp.tanh(GELU_C * (x + GELU_A * x * x * x))
    return 0.5 * (1.0 + t) + 0.5 * x * (1.0 - t * t) * GELU_C * (1.0 + 3.0 * GELU_A * x * x)


def _expm1_nonpos(x):
    series = x * (1.0 + x * (1.0 / 2 + x * (1.0 / 6 + x * (1.0 / 24 + x * (1.0 / 120 + x * (1.0 / 720 + x / 5040))))))
    return jnp.where(x > -0.25, series, jnp.exp(x) - 1.0)


def _softplus(x):
    return jnp.maximum(x, 0.0) + jnp.log1p(jnp.exp(-jnp.abs(x)))


def _rms_stats(x):
    return lax.rsqrt(jnp.mean(x * x, axis=-1, keepdims=True) + NORM_EPS)


def _rms_bwd(dxhat, xhat, rstd):
    return rstd * (dxhat - xhat * jnp.mean(dxhat * xhat, axis=-1, keepdims=True))


def _colsum(v):
    return jnp.sum(v, axis=0, keepdims=True)


def _mod_fwd(c16, w_mod, b_mod_shard):
    r, d = c16.shape
    n = w_mod.shape[1]
    tn = _tile(n, 512, LANES)

    def body(c_ref, w_ref, b_ref, o_ref, s_ref):
        s = _silu(c_ref[...])
        s_ref[...] = s
        o_ref[...] = jnp.dot(s.astype(BF16), w_ref[...].astype(BF16), preferred_element_type=F32) + b_ref[...]

    return pl.pallas_call(
        body, name="mod_fwd", grid=(n // tn,),
        in_specs=[pl.BlockSpec((r, d), lambda j: (0, 0)), pl.BlockSpec((d, tn), lambda j: (0, j)),
                  pl.BlockSpec((1, tn), lambda j: (0, j))],
        out_specs=[pl.BlockSpec((r, tn), lambda j: (0, j)), pl.BlockSpec((r, d), lambda j: (0, 0))],
        out_shape=[_sds((r, n)), _sds((r, d))], compiler_params=_params("arbitrary"),
    )(c16, w_mod, b_mod_shard)


def _c_ctx_grad(parts, c_ctx_row):
    d = c_ctx_row.shape[1]

    def body(p_ref, c_ref, o_ref):
        tot = p_ref[0, 0:1, :]
        for chip in range(1, N_CHIPS):
            tot = tot + p_ref[2 * chip, 0:1, :]
        c = c_ref[...]
        sg = jax.nn.sigmoid(c)
        o_ref[...] = tot * (sg * (1.0 + c * (1.0 - sg)))

    return pl.pallas_call(body, name="c_ctx_grad", out_shape=_sds((1, d)), compiler_params=_params())(parts, c_ctx_row)


def _token_specs(n_ctx_rows, d, tr):
    nctx = n_ctx_rows // tr
    return (pl.BlockSpec((tr, d), lambda i: (jnp.minimum(i, nctx - 1), 0)),
            pl.BlockSpec((tr, d), lambda i: (jnp.maximum(i - nctx, 0), 0)))


def _norm_mod_fwd(ctx, x, g, mod4, tr):
    (n_ctx_rows, d), s = ctx.shape, x.shape[0]
    t = n_ctx_rows + s
    nctx = n_ctx_rows // tr

    def body(c_ref, x_ref, g_ref, mod_ref, h_ref):
        is_ctx = pl.program_id(0) < nctx
        x = jnp.where(is_ctx, c_ref[...], x_ref[...])
        n = x * _rms_stats(x) * g_ref[...]
        sh = jnp.where(is_ctx, mod_ref[0:1, :], mod_ref[2:3, :])
        sc = jnp.where(is_ctx, mod_ref[1:2, :], mod_ref[3:4, :])
        h_ref[...] = (n * (1.0 + sc) + sh).astype(BF16)

    return pl.pallas_call(
        body, name="norm_mod_fwd", grid=(t // tr,),
        in_specs=[*_token_specs(n_ctx_rows, d, tr), pl.BlockSpec((1, d), lambda i: (0, 0)),
                  pl.BlockSpec((4, d), lambda i: (0, 0))],
        out_specs=pl.BlockSpec((tr, d), lambda i: (i, 0)), out_shape=_sds((t, d), BF16),
        compiler_params=_params("parallel"),
    )(ctx, x, g, mod4)


def _norm_mod_bwd1(ctx, x, dh, dx1, g, mod4, tr, side=None):
    (n_ctx_rows, d), s = ctx.shape, x.shape[0]
    t = n_ctx_rows + s
    nctx = n_ctx_rows // tr

    def body(c_ref, x_ref, dh_ref, dx1_ref, g_ref, mod_ref, dx_ref, sums_ref):
        i = pl.program_id(0)
        is_ctx = i < nctx

        @pl.when(i == 0)
        def _():
            sums_ref[...] = jnp.zeros_like(sums_ref)

        x = jnp.where(is_ctx, c_ref[...], x_ref[...])
        dh_ = dh_ref[...]
        rstd = _rms_stats(x)
        xhat = x * rstd
        gg = g_ref[...]
        sc = jnp.where(is_ctx, mod_ref[1:2, :], mod_ref[3:4, :])
        dxhat = dh_ * (1.0 + sc) * gg
        dx_ref[...] = dx1_ref[...] + _rms_bwd(dxhat, xhat, rstd)
        part = [_colsum(dh_), _colsum(dh_ * xhat * gg), _colsum(dh_ * (1.0 + sc) * xhat)]

        @pl.when(is_ctx)
        def _():
            for k, row in enumerate(part):
                sums_ref[3 + k:4 + k, :] += row

        @pl.when(jnp.logical_not(is_ctx))
        def _():
            for k, row in enumerate(part):
                sums_ref[k:k + 1, :] += row

    lat = lambda i: (jnp.maximum(i - nctx, 0), 0)
    return _call(
        body, side=side, name="norm_mod_bwd1", grid=(t // tr,),
        in_specs=[*_token_specs(n_ctx_rows, d, tr), pl.BlockSpec((tr, d), lambda i: (i, 0)),
                  pl.BlockSpec((tr, d), lat), pl.BlockSpec((1, d), lambda i: (0, 0)),
                  pl.BlockSpec((4, d), lambda i: (0, 0))],
        out_specs=[pl.BlockSpec((tr, d), lat), pl.BlockSpec((8, d), lambda i: (0, 0))],
        out_shape=[_sds((s, d)), _sds((8, d))], sem=("arbitrary",),
    )(ctx, x, dh, dx1, g, mod4)


def _resid_norm_mod_fwd(x, mix, g, mod3, tr):
    s, d = x.shape

    def body(x_ref, mix_ref, g_ref, mod_ref, x1_ref, h_ref):
        x1 = x_ref[...] + mod_ref[0:1, :] * mix_ref[...]
        x1_ref[...] = x1
        n = x1 * _rms_stats(x1) * g_ref[...]
        h_ref[...] = (n * (1.0 + mod_ref[2:3, :]) + mod_ref[1:2, :]).astype(BF16)

    row = pl.BlockSpec((tr, d), lambda i: (i, 0))
    return pl.pallas_call(
        body, name="resid_norm_mod_fwd", grid=(s // tr,),
        in_specs=[row, row, pl.BlockSpec((1, d), lambda i: (0, 0)), pl.BlockSpec((3, d), lambda i: (0, 0))],
        out_specs=[row, row], out_shape=[_sds((s, d)), _sds((s, d), BF16)], compiler_params=_params("parallel"),
    )(x, mix, g, mod3)


def _norm_mod_bwd2(x1, dh2, dx2, mix, g, mod3, tr, side=None):
    s, d = x1.shape

    def body(x_ref, dh_ref, dx2_ref, mix_ref, g_ref, mod_ref, dx1_ref, dmix_ref, sums_ref):
        @pl.when(pl.program_id(0) == 0)
        def _():
            sums_ref[...] = jnp.zeros_like(sums_ref)

        x = x_ref[...]
        dh_ = dh_ref[...]
        rstd = _rms_stats(x)
        xhat = x * rstd
        gg = g_ref[...]
        sc = mod_ref[2:3, :]
        dx1 = dx2_ref[...] + _rms_bwd(dh_ * (1.0 + sc) * gg, xhat, rstd)
        dx1_ref[...] = dx1
        dmix_ref[...] = (dx1 * mod_ref[0:1, :]).astype(BF16)
        part = [_colsum(dh_), _colsum(dh_ * xhat * gg), _colsum(dh_ * (1.0 + sc) * xhat), _colsum(dx1 * mix_ref[...])]
        for k, row in enumerate(part):
            sums_ref[k:k + 1, :] += row

    row = pl.BlockSpec((tr, d), lambda i: (i, 0))
    return _call(
        body, side=side, name="norm_mod_bwd2", grid=(s // tr,),
        in_specs=[row, row, row, row, pl.BlockSpec((1, d), lambda i: (0, 0)), pl.BlockSpec((3, d), lambda i: (0, 0))],
        out_specs=[row, row, pl.BlockSpec((8, d), lambda i: (0, 0))],
        out_shape=[_sds((s, d)), _sds((s, d), BF16), _sds((8, d))], sem=("arbitrary",),
    )(x1, dh2, dx2, mix, g, mod3)


def _final_fwd_bwd(x1, down, target, g_final, gate, tr):
    s, d = x1.shape

    def body(x1_ref, down_ref, t_ref, g_ref, gate_ref, dx2_ref, ddown_ref, sums_ref, loss_ref):
        @pl.when(pl.program_id(0) == 0)
        def _():
            sums_ref[...] = jnp.zeros_like(sums_ref)
            loss_ref[...] = jnp.zeros_like(loss_ref)

        down_ = down_ref[...]
        gate_ = gate_ref[...]
        x2 = x1_ref[...] + gate_ * down_
        rstd = _rms_stats(x2)
        xhat = x2 * rstd
        gg = g_ref[...]
        err = xhat * gg - t_ref[...]
        loss_ref[...] += 0.5 * jnp.sum(jnp.mean(err * err, axis=-1, keepdims=True))
        dy = err * (1.0 / d)
        dx2 = _rms_bwd(dy * gg, xhat, rstd)
        dx2_ref[...] = dx2
        ddown_ref[...] = (dx2 * gate_).astype(BF16)
        sums_ref[0:1, :] += _colsum(dy * xhat)
        sums_ref[1:2, :] += _colsum(dx2 * down_)

    row = pl.BlockSpec((tr, d), lambda i: (i, 0))
    vec = pl.BlockSpec((1, d), lambda i: (0, 0))
    return pl.pallas_call(
        body, name="final_fwd_bwd", grid=(s // tr,), in_specs=[row, row, row, vec, vec],
        out_specs=[row, row, pl.BlockSpec((8, d), lambda i: (0, 0)), pl.BlockSpec((8, LANES), lambda i: (0, 0))],
        out_shape=[_sds((s, d)), _sds((s, d), BF16), _sds((8, d)), _sds((8, LANES))],
        compiler_params=_params("arbitrary"),
    )(x1, down, target, g_final, gate)


def _swap_pairs(v):
    lane = lax.broadcasted_iota(jnp.int32, v.shape, 1)
    return jnp.where(lane % 2 == 0, pltpu.roll(v, LANES - 1, 1), pltpu.roll(v, 1, 1))


def _head_prep_fwd(z, col_off, n_heads, gain, cos, sin, tr, name, side=None):
    t = z.shape[0]
    per = math.gcd(4, n_heads, col_off // LANES)
    w = per * LANES
    hb = col_off // w

    def body(z_ref, g_ref, cos_ref, sin_ref, o_ref):
        for hh in range(per):
            cols = slice(hh * LANES, (hh + 1) * LANES)
            x = z_ref[:, cols]
            y = x * _rms_stats(x) * g_ref[...]
            o_ref[:, cols] = (y * cos_ref[...] + _swap_pairs(y) * sin_ref[...]).astype(BF16)

    tab = pl.BlockSpec((tr, LANES), lambda i, j: (i, 0))
    return _call(
        body, side=side, name=name, grid=(t // tr, n_heads // per),
        in_specs=[pl.BlockSpec((tr, w), lambda i, j: (i, hb + j)), pl.BlockSpec((1, LANES), lambda i, j: (0, 0)),
                  tab, tab],
        out_specs=pl.BlockSpec((tr, w), lambda i, j: (i, j)), out_shape=_sds((t, n_heads * LANES), BF16),
        sem=("parallel", "parallel"),
    )(z, gain, cos, sin)


def _dz_start(t, n_in, n_ctx_rows):
    tcol = _tile(n_in, 1024, LANES)

    def body(o_ref):
        o_ref[...] = jnp.zeros_like(o_ref)

    return pl.pallas_call(body, name="dz_start", grid=(n_in // tcol,),
                          out_specs=pl.BlockSpec((n_ctx_rows, tcol), lambda j: (0, j)), out_shape=_sds((t, n_in), BF16),
                          compiler_params=_params("parallel"))()


def _head_prep_bwd(z, col_off, n_heads, gain, cos, sin, dout, row_off, tr, dz, name, side=None):
    r = dout.shape[0]
    per = math.gcd(4, n_heads, col_off // LANES)
    w = per * LANES
    hb = col_off // w
    rb = row_off // tr

    def body(z_ref, g_ref, cos_ref, sin_ref, d_ref, _, dz_ref, dg_ref):
        @pl.when(jnp.logical_and(pl.program_id(0) == 0, pl.program_id(1) == 0))
        def _():
            dg_ref[...] = jnp.zeros_like(dg_ref)

        for hh in range(per):
            cols = slice(hh * LANES, (hh + 1) * LANES)
            x = z_ref[:, cols]
            rstd = _rms_stats(x)
            xhat = x * rstd
            dd = d_ref[:, cols]
            dy = dd * cos_ref[...] - _swap_pairs(dd) * sin_ref[...]
            dg_ref[0:1, :] += _colsum(dy * xhat)
            dz_ref[:, cols] = _rms_bwd(dy * g_ref[...], xhat, rstd).astype(BF16)

    tab = pl.BlockSpec((tr, LANES), lambda i, j: (rb + i, 0))
    window = pl.BlockSpec((tr, w), lambda i, j: (rb + i, hb + j))
    return _call(
        body, side=side, name=name, grid=(r // tr, n_heads // per),
        in_specs=[window, pl.BlockSpec((1, LANES), lambda i, j: (0, 0)), tab, tab,
                  pl.BlockSpec((tr, w), lambda i, j: (i, j)), ANY],
        out_specs=[window, pl.BlockSpec((8, LANES), lambda i, j: (0, 0))],
        out_shape=[_sds(dz.shape, BF16), _sds((8, LANES))], input_output_aliases={5: 0}, sem=("arbitrary", "arbitrary"),
    )(z, gain, cos, sin, dout, dz)


def _attn_fwd(qr, kr, z, v_off, n_ctx_rows, group, tq, side=None):
    t, kvw = kr.shape
    s = t - n_ctx_rows
    n_kv = kvw // LANES
    scale = LANES ** -0.5
    qb0 = n_ctx_rows // tq
    vb = v_off // LANES

    def body(q_ref, k_ref, v_ref, o_ref, lse_ref):
        k = k_ref[...]
        v = v_ref[...].astype(BF16)
        lse_ref[...] = jnp.zeros_like(lse_ref)
        for g in range(group):
            cols = slice(g * LANES, (g + 1) * LANES)
            sc = lax.dot_general(q_ref[:, cols], k, (((1,), (1,)), ((), ())), preferred_element_type=F32) * scale
            m = jnp.max(sc, axis=-1, keepdims=True)
            e = jnp.exp(sc - m)
            l = jnp.sum(e, axis=-1, keepdims=True)
            p = e * (1.0 / l)
            o_ref[:, cols] = jnp.dot(p.astype(BF16), v, preferred_element_type=F32).astype(BF16)
            lse_ref[:, g:g + 1] = m + jnp.log(l)

    return _call(
        body, side=side, name="attn_fwd", grid=(n_kv, s // tq),
        in_specs=[pl.BlockSpec((tq, group * LANES), lambda h, i: (qb0 + i, h)),
                  pl.BlockSpec((t, LANES), lambda h, i: (0, h)), pl.BlockSpec((t, LANES), lambda h, i: (0, vb + h))],
        out_specs=[pl.BlockSpec((tq, group * LANES), lambda h, i: (i, h)), pl.BlockSpec((tq, LANES), lambda h, i: (i, h))],
        out_shape=[_sds((s, n_kv * group * LANES), BF16), _sds((s, kvw))], sem=("parallel", "parallel"),
    )(qr, kr, z)


def _attn_bwd(qr, kr, z, v_off, d_o, attn_o, lse, dz, n_ctx_rows, group, tq, side=None):
    t, kvw = kr.shape
    s = t - n_ctx_rows
    n_kv = kvw // LANES
    scale = LANES ** -0.5
    qb0 = n_ctx_rows // tq
    vb = v_off // LANES
    n_q_blocks = s // tq
    tn_dims = (((0,), (0,)), ((), ()))
    nt_dims = (((1,), (1,)), ((), ()))

    def body(q_ref, k_ref, v_ref, do_ref, o_ref, lse_ref, _, dq_ref, dk_ref, dz_ref, dv_ref):
        @pl.when(pl.program_id(1) == 0)
        def _():
            dk_ref[...] = jnp.zeros_like(dk_ref)
            dv_ref[...] = jnp.zeros_like(dv_ref)

        k = k_ref[...]
        v = v_ref[...].astype(BF16)
        for g in range(group):
            cols = slice(g * LANES, (g + 1) * LANES)
            q = q_ref[:, cols]
            do_ = do_ref[:, cols]
            row_dot = jnp.sum(do_.astype(F32) * o_ref[:, cols].astype(F32), axis=-1, keepdims=True)
            sc = lax.dot_general(q, k, nt_dims, preferred_element_type=F32)
            p = jnp.exp(sc * scale - lse_ref[:, g:g + 1])
            dv_ref[...] += lax.dot_general(p.astype(BF16), do_, tn_dims, preferred_element_type=F32)
            dp = lax.dot_general(do_, v, nt_dims, preferred_element_type=F32)
            ds = (p * (dp - row_dot)).astype(BF16)
            dq_ref[:, cols] = jnp.dot(ds, k, preferred_element_type=F32) * scale
            dk_ref[...] += lax.dot_general(ds, q, tn_dims, preferred_element_type=F32)

        @pl.when(pl.program_id(1) == n_q_blocks - 1)
        def _():
            dk_ref[...] = dk_ref[...] * scale
            dz_ref[...] = dv_ref[...].astype(BF16)

    qspec = pl.BlockSpec((tq, group * LANES), lambda h, i: (qb0 + i, h))
    ospec = pl.BlockSpec((tq, group * LANES), lambda h, i: (i, h))
    kspec = pl.BlockSpec((t, LANES), lambda h, i: (0, h))
    vspec = pl.BlockSpec((t, LANES), lambda h, i: (0, vb + h))
    return _call(
        body, side=side, name="attn_bwd", grid=(n_kv, n_q_blocks),
        in_specs=[qspec, kspec, vspec, ospec, ospec, pl.BlockSpec((tq, LANES), lambda h, i: (i, h)), ANY],
        out_specs=[ospec, kspec, vspec], out_shape=[_sds((s, n_kv * group * LANES)), _sds((t, kvw)), _sds(dz.shape, BF16)],
        scratch_shapes=[pltpu.VMEM((t, LANES), F32)], input_output_aliases={6: 2}, sem=("parallel", "arbitrary"),
    )(qr, kr, z, d_o, attn_o, lse, dz)


def _row_mask(shape, rows):
    r = lax.broadcasted_iota(jnp.int32, shape, 0)
    m = r == rows[0]
    for v in rows[1:]:
        m = jnp.logical_or(m, r == v)
    return m


def _shift_rows(x, k, n_ctx_rows):
    t = x.shape[0]
    if k == 0:
        return x
    rolled = pltpu.roll(x, (-k) % t, 0)
    if k > 0:
        dead = [n_ctx_rows - 1 - i for i in range(k)] + [t - 1 - i for i in range(k)]
    else:
        dead = [i for i in range(-k)] + [n_ctx_rows + i for i in range(-k)]
    return jnp.where(_row_mask(x.shape, dead), 0.0, rolled)


def _conv(x, w, b, n_ctx_rows):
    y = b
    for k in range(CONV_WIDTH):
        y = y + _shift_rows(x, k - 1, n_ctx_rows) * w[k:k + 1, :]
    return y


def _gates(xc_bf, w_r, b_r, w_i, b_i, lam):
    r = jax.nn.sigmoid(jnp.dot(xc_bf, w_r.astype(BF16), preferred_element_type=F32) + b_r)
    i = jax.nn.sigmoid(jnp.dot(xc_bf, w_i.astype(BF16), preferred_element_type=F32) + b_i)
    log_a = -LRU_C * r * _softplus(-lam)
    a = jnp.exp(log_a)
    mult = jnp.sqrt(-_expm1_nonpos(2.0 * log_a))
    return r, i, a, mult


def _rnn_specs(t, xr_off):
    xb = xr_off // LANES
    return dict(
        zcol=pl.BlockSpec((t, LANES), lambda j: (0, xb + j)), col=pl.BlockSpec((t, LANES), lambda j: (0, j)),
        conv_w=pl.BlockSpec((CONV_WIDTH, LANES), lambda j: (0, j)), vec=pl.BlockSpec((1, LANES), lambda j: (0, j)),
        gate_w=pl.BlockSpec((2, 1, LANES, LANES), lambda j: (0, j, 0, 0)), two=pl.BlockSpec((2, LANES), lambda j: (0, j)))


def _rnn_prep(z, xr_off, conv_w, conv_b, w_rg, b_rg, w_ig, b_ig, lam, n_ctx_rows, side=None):
    t = z.shape[0]
    d = conv_b.shape[1]
    sp = _rnn_specs(t, xr_off)

    def body(z_ref, cw_ref, cb_ref, wr_ref, br_ref, wi_ref, bi_ref, lam_ref, xc_ref, af_ref, bf_ref, ab_ref, bb_ref):
        xc = _conv(z_ref[...], cw_ref[...], cb_ref[...], n_ctx_rows)
        xc_ref[...] = xc
        xc_bf = xc.astype(BF16)
        for dr, (a_ref, b_ref) in enumerate(((af_ref, bf_ref), (ab_ref, bb_ref))):
            _, i, a, mult = _gates(xc_bf, wr_ref[dr, 0], br_ref[dr:dr + 1, :], wi_ref[dr, 0], bi_ref[dr:dr + 1, :],
                                   lam_ref[dr:dr + 1, :])
            a_ref[...] = a
            b_ref[...] = mult * (i * xc)

    return _call(
        body, side=side, name="rnn_prep", grid=(d // LANES,),
        in_specs=[sp["zcol"], sp["conv_w"], sp["vec"], sp["gate_w"], sp["two"], sp["gate_w"], sp["two"], sp["two"]],
        out_specs=[sp["col"]] * 5, out_shape=[_sds((t, d))] * 5, sem=("parallel",),
    )(z, conv_w, conv_b, w_rg, b_rg, w_ig, b_ig, lam)


def _scan(chains, *, post, n_ctx_rows, name, tc=256, side=None):
    t, d = chains[0][0].shape
    tc = _tile(math.gcd(n_ctx_rows, t - n_ctx_rows), tc, SUBLANES)
    nt, nctx = t // tc, n_ctx_rows // tc
    nlat = nt - nctx
    nc = len(chains)
    lat_only = [b.shape[0] != t for _, b, _ in chains]
    ups = [order.endswith("up") for _, _, order in chains]

    def chunk_of(order):
        def chunk(i):
            if order == "ctx_lat_up":
                return i
            if order == "lat_ctx_down":
                return nt - 1 - i
            if order == "ctx_lat_down":
                return jnp.where(i < nctx, nctx - 1 - i, nt - 1 - (i - nctx))
            return jnp.where(i < nlat, nctx + i, i - nlat)
        return chunk

    chunks = [chunk_of(order) for _, _, order in chains]

    def body(*refs):
        ab_refs, o_refs, carry_ref = refs[:2 * nc], refs[2 * nc:3 * nc], refs[3 * nc]

        @pl.when(pl.program_id(0) == 0)
        def _():
            carry_ref[...] = jnp.zeros_like(carry_ref)

        live = [jnp.where(chunks[n](pl.program_id(0)) >= nctx, 1.0, 0.0) if lat_only[n] else None for n in range(nc)]

        def group(gi, carries):
            carries = list(carries)
            bases = [pl.multiple_of((gi if ups[n] else tc // SUBLANES - 1 - gi) * SUBLANES, SUBLANES) for n in range(nc)]
            for step in range(SUBLANES):
                for n in range(nc):
                    row = bases[n] + (step if ups[n] else SUBLANES - 1 - step)
                    a_r = ab_refs[2 * n][pl.ds(row, 1), :]
                    b_r = ab_refs[2 * n + 1][pl.ds(row, 1), :]
                    if live[n] is not None:
                        b_r = b_r * live[n]
                    if post:
                        out = b_r + carries[n]
                        carries[n] = a_r * out
                    else:
                        out = a_r * carries[n] + b_r
                        carries[n] = out
                    o_refs[n][pl.ds(row, 1), :] = out
            return tuple(carries)

        done = lax.fori_loop(0, tc // SUBLANES, group, tuple(carry_ref[n:n + 1, :] for n in range(nc)))
        for n in range(nc):
            carry_ref[n:n + 1, :] = done[n]

    in_specs, out_specs, args = [], [], []
    for n, (a, b, _) in enumerate(chains):
        full = pl.BlockSpec((tc, d), lambda i, n=n: (chunks[n](i), 0))
        lat = pl.BlockSpec((tc, d), lambda i, n=n: (jnp.maximum(chunks[n](i) - nctx, 0), 0))
        in_specs += [full, lat if lat_only[n] else full]
        out_specs.append(full)
        args += [a, b]
    return _call(
        body, side=side, name=name, grid=(nt,), in_specs=in_specs, out_specs=out_specs, out_shape=[_sds((t, d))] * nc,
        scratch_shapes=[pltpu.VMEM((SUBLANES, d), F32)], sem=("arbitrary",),
    )(*args)


def _rnn_bwd(z, xr_off, xc, g_f, g_b, h_f, h_b, conv_w, w_rg, b_rg, w_ig, b_ig, lam, dz, n_ctx_rows, side=None):
    t, d = xc.shape
    sp = _rnn_specs(t, xr_off)
    tn_dims = (((0,), (0,)), ((), ()))
    nt_dims = (((1,), (1,)), ((), ()))

    def body(z_ref, xc_ref, gf_ref, gb_ref, hf_ref, hb_ref, cw_ref, wr_ref, br_ref, wi_ref, bi_ref, lam_ref, _,
             dxr_ref, dwr_ref, dwi_ref, sums_ref):
        xc_ = xc_ref[...]
        xc_bf = xc_.astype(BF16)
        dxc = jnp.zeros_like(xc_)
        sums = [None] * 6
        for dr, (g_ref, h_ref) in enumerate(((gf_ref, hf_ref), (gb_ref, hb_ref))):
            w_r, w_i, lam_ = wr_ref[dr, 0], wi_ref[dr, 0], lam_ref[dr:dr + 1, :]
            r, i, a, mult = _gates(xc_bf, w_r, br_ref[dr:dr + 1, :], w_i, bi_ref[dr:dr + 1, :], lam_)
            g = g_ref[...]
            h = h_ref[...]
            if dr == 0:
                h_prev = jnp.where(_row_mask(h.shape, [0]), 0.0, pltpu.roll(h, 1, 0))
            else:
                h_prev = jnp.where(_row_mask(h.shape, [n_ctx_rows - 1]), 0.0, pltpu.roll(h, t - 1, 0))
            d_mult = g * i * xc_
            d_i = g * mult * xc_
            dxc = dxc + g * mult * i
            d_log_a = g * h_prev * a - d_mult * a * a / mult
            sp_ = _softplus(-lam_)
            d_r = d_log_a * (-LRU_C) * sp_
            d_sp = _colsum(d_log_a * (-LRU_C) * r)
            du_r = (d_r * r * (1.0 - r))
            du_i = (d_i * i * (1.0 - i))
            sums[dr] = _colsum(du_r)
            sums[2 + dr] = _colsum(du_i)
            sums[4 + dr] = d_sp * (-jax.nn.sigmoid(-lam_))
            du_r_bf, du_i_bf = du_r.astype(BF16), du_i.astype(BF16)
            dwr_ref[dr, 0] = lax.dot_general(xc_bf, du_r_bf, tn_dims, preferred_element_type=F32).astype(BF16)
            dwi_ref[dr, 0] = lax.dot_general(xc_bf, du_i_bf, tn_dims, preferred_element_type=F32).astype(BF16)
            dxc = dxc + lax.dot_general(du_r_bf, w_r.astype(BF16), nt_dims, preferred_element_type=F32)
            dxc = dxc + lax.dot_general(du_i_bf, w_i.astype(BF16), nt_dims, preferred_element_type=F32)
        xr = z_ref[...]
        cw = cw_ref[...]
        dxr = jnp.zeros_like(dxc)
        rows = list(sums)
        for k in range(CONV_WIDTH):
            dxr = dxr + _shift_rows(dxc, 1 - k, n_ctx_rows) * cw[k:k + 1, :]
            rows.append(_colsum(dxc * _shift_rows(xr, k - 1, n_ctx_rows)))
        rows.append(_colsum(dxc))
        dxr_ref[...] = dxr.astype(BF16)
        sums_ref[...] = jnp.zeros_like(sums_ref)
        for k, row in enumerate(rows):
            sums_ref[k:k + 1, :] = row

    return _call(
        body, side=side, name="rnn_bwd", grid=(d // LANES,),
        in_specs=[sp["zcol"]] + [sp["col"]] * 5 + [sp["conv_w"], sp["gate_w"], sp["two"], sp["gate_w"], sp["two"],
                                                  sp["two"], ANY],
        out_specs=[sp["zcol"], sp["gate_w"], sp["gate_w"], pl.BlockSpec((16, LANES), lambda j: (0, j))],
        out_shape=[_sds(dz.shape, BF16), _sds(w_rg.shape, BF16), _sds(w_ig.shape, BF16), _sds((16, d))],
        input_output_aliases={12: 0}, sem=("parallel",),
    )(z, xc, g_f, g_b, h_f, h_b, conv_w, w_rg, b_rg, w_ig, b_ig, lam, dz)


def _tiles2d(s, d, tr, tcol):
    return (s // tr, d // tcol), pl.BlockSpec((tr, tcol), lambda i, j: (i, j))


def _zspec(tr, tcol, row_off, col_off):
    rb, cb = row_off // tr, col_off // tcol
    return pl.BlockSpec((tr, tcol), lambda i, j: (rb + i, cb + j))


def _rnn_gate_fwd(h_f, h_b, z, xg_off, n_ctx_rows, tr, tcol, side=None):
    t, d = h_f.shape
    s = t - n_ctx_rows
    grid, out = _tiles2d(s, d, tr, tcol)
    hs = _zspec(tr, tcol, n_ctx_rows, 0)

    def body(hf_ref, hb_ref, xg_ref, u_ref):
        u_ref[...] = ((hf_ref[...] + hb_ref[...]) * _gelu(xg_ref[...])).astype(BF16)

    return _call(body, side=side, name="rnn_gate_fwd", grid=grid,
                 in_specs=[hs, hs, _zspec(tr, tcol, n_ctx_rows, xg_off)], out_specs=out, out_shape=_sds((s, d), BF16),
                 sem=("parallel", "parallel"))(h_f, h_b, z)


def _rnn_gate_bwd(d_u, h_f, h_b, z, xg_off, dz, n_ctx_rows, tr, tcol, side=None):
    t, d = h_f.shape
    s = t - n_ctx_rows
    grid, out = _tiles2d(s, d, tr, tcol)
    hs = _zspec(tr, tcol, n_ctx_rows, 0)
    window = _zspec(tr, tcol, n_ctx_rows, xg_off)

    def body(du_ref, hf_ref, hb_ref, xg_ref, _, dr_ref, dxg_ref):
        du = du_ref[...]
        xg = xg_ref[...]
        dr_ref[...] = du * _gelu(xg)
        dxg_ref[...] = (du * (hf_ref[...] + hb_ref[...]) * _gelu_grad(xg)).astype(BF16)

    return _call(body, side=side, name="rnn_gate_bwd", grid=grid, in_specs=[out, hs, hs, window, ANY],
                 out_specs=[out, window], out_shape=[_sds((s, d)), _sds(dz.shape, BF16)], input_output_aliases={4: 1},
                 sem=("parallel", "parallel"))(d_u, h_f, h_b, z, dz)


def _merge_fwd(y_attn, y_rnn, z, gl_off, n_ctx_rows, tr, tcol):
    s, d = y_attn.shape
    grid, out = _tiles2d(s, d, tr, tcol)

    def body(ya_ref, yr_ref, ga_ref, gr_ref, o_ref):
        o_ref[...] = (jax.nn.sigmoid(ga_ref[...]) * ya_ref[...] + jax.nn.sigmoid(gr_ref[...]) * yr_ref[...]).astype(BF16)

    return pl.pallas_call(
        body, name="merge_fwd", grid=grid,
        in_specs=[out, out, _zspec(tr, tcol, n_ctx_rows, gl_off), _zspec(tr, tcol, n_ctx_rows, gl_off + d)],
        out_specs=out, out_shape=_sds((s, d), BF16), compiler_params=_params("parallel", "parallel"),
    )(y_attn, y_rnn, z, z)


def _merge_bwd(d_mrg, y, z, gl_off, dz, n_ctx_rows, tr, tcol, name):
    s, d = y.shape
    grid, out = _tiles2d(s, d, tr, tcol)
    window = _zspec(tr, tcol, n_ctx_rows, gl_off)

    def body(dm_ref, y_ref, gl_ref, _, dy_ref, dgl_ref):
        dm = dm_ref[...]
        g = jax.nn.sigmoid(gl_ref[...])
        dy_ref[...] = (dm * g).astype(BF16)
        dgl_ref[...] = (dm * y_ref[...] * g * (1.0 - g)).astype(BF16)

    return _call(body, name=name, grid=grid, in_specs=[out, out, window, ANY], out_specs=[out, window],
                 out_shape=[_sds((s, d), BF16), _sds(dz.shape, BF16)], input_output_aliases={3: 1},
                 sem=("parallel", "parallel"))(d_mrg, y, z, dz)


def _sq_relu(up, tr, tcol, side=None):
    grid, out = _tiles2d(*up.shape, _tile(up.shape[0], 2 * tr, 16), _tile(up.shape[1], 4 * tcol, LANES))

    def body(u_ref, o_ref):
        r = jnp.maximum(u_ref[...], 0.0)
        o_ref[...] = (r * r).astype(BF16)

    return _call(body, side=side, name="sq_relu", grid=grid, in_specs=[out], out_specs=out,
                 out_shape=_sds(up.shape, BF16), sem=("parallel", "parallel"))(up)


def _cast_into_window(w, chip, col_sharded, name):
    r, c = w.shape
    tr, tcol = _tile(r, 512, 16), _tile(c, WIDE_TILE, LANES)
    nrb, ncb = r // tr, c // tcol

    def body(chip_ref, w_ref, o_ref):
        o_ref[...] = w_ref[...].astype(BF16)

    if col_sharded:
        omap = lambda i, j, chip_ref: (i, chip_ref[0] * ncb + j)
    else:
        omap = lambda i, j, chip_ref: (chip_ref[0] * nrb + i, j)
    return pl.pallas_call(
        body, name=name,
        grid_spec=pltpu.PrefetchScalarGridSpec(
            num_scalar_prefetch=1, grid=(nrb, ncb),
            in_specs=[pl.BlockSpec((tr, tcol), lambda i, j, chip_ref: (i, j))], out_specs=pl.BlockSpec((tr, tcol), omap)),
        out_shape=_sds((r, c * N_CHIPS) if col_sharded else (r * N_CHIPS, c), BF16),
        compiler_params=_params("parallel", "parallel"),
    )(chip, w)


def _sum_leading(parts, name):
    n, r, c = parts.shape
    tr, tcol = _tile(r, 512, SUBLANES), _tile(c, 1024, LANES)

    def body(p_ref, o_ref):
        tot = p_ref[0]
        for k in range(1, n):
            tot = tot + p_ref[k]
        o_ref[...] = tot

    return pl.pallas_call(
        body, name=name, grid=(r // tr, c // tcol), in_specs=[pl.BlockSpec((n, tr, tcol), lambda i, j: (0, i, j))],
        out_specs=pl.BlockSpec((tr, tcol), lambda i, j: (i, j)), out_shape=_sds((r, c)),
        compiler_params=_params("parallel", "parallel"),
    )(parts)


def _add_half(full, other, core, split_rows, name):
    r, c = other.shape
    tr, tcol = _tile(r, 512, 16), _tile(c, 1024, LANES)
    nrb, ncb = r // tr, c // tcol

    def body(core_ref, f_ref, o_ref, out_ref):
        out_ref[...] = (f_ref[...].astype(F32) + o_ref[...].astype(F32)).astype(out_ref.dtype)

    if split_rows:
        fmap = lambda i, j, core_ref: (core_ref[0] * nrb + i, j)
    else:
        fmap = lambda i, j, core_ref: (i, core_ref[0] * ncb + j)
    same = lambda i, j, core_ref: (i, j)
    return pl.pallas_call(
        body, name=name,
        grid_spec=pltpu.PrefetchScalarGridSpec(
            num_scalar_prefetch=1, grid=(nrb, ncb),
            in_specs=[pl.BlockSpec((tr, tcol), fmap), pl.BlockSpec((tr, tcol), same)],
            out_specs=pl.BlockSpec((tr, tcol), same)),
        out_shape=_sds((r, c), BF16), compiler_params=_params("parallel", "parallel"),
    )(core, full, other)


def _add_pair(a, b, name):
    r, c = a.shape
    tr, tcol = _tile(r, 512, 16), _tile(c, WIDE_TILE, LANES)
    blk = pl.BlockSpec((tr, tcol), lambda i, j: (i, j))

    def body(a_ref, b_ref, o_ref):
        o_ref[...] = (a_ref[...].astype(F32) + b_ref[...].astype(F32)).astype(BF16)

    return pl.pallas_call(body, name=name, grid=(r // tr, c // tcol), in_specs=[blk, blk], out_specs=blk,
                          out_shape=_sds((r, c), BF16), compiler_params=_params("parallel", "parallel"))(a, b)


def _sum_regions(pair, got, place, col_sharded, name):
    n_got, r, c = got.shape
    tr, tcol = _tile(r, 512, 16), _tile(c, WIDE_TILE, LANES)
    nrb, ncb = r // tr, c // tcol

    def body(place_ref, p_ref, g_ref, out_ref):
        tot = p_ref[...].astype(F32)
        for k in range(n_got):
            tot = tot + g_ref[k].astype(F32)
        out_ref[...] = tot

    if col_sharded:
        pmap = lambda i, j, pr: (i, pr[0] * ncb + j)
        omap = lambda i, j, pr: (pr[1] * nrb + i, j)
        out_shape = (2 * r, c)
    else:
        pmap = lambda i, j, pr: (pr[0] * nrb + i, j)
        omap = lambda i, j, pr: (i, pr[1] * ncb + j)
        out_shape = (r, 2 * c)
    return pl.pallas_call(
        body, name=name,
        grid_spec=pltpu.PrefetchScalarGridSpec(
            num_scalar_prefetch=1, grid=(nrb, ncb),
            in_specs=[pl.BlockSpec((tr, tcol), pmap), pl.BlockSpec((n_got, tr, tcol), lambda i, j, pr: (0, i, j))],
            out_specs=pl.BlockSpec((tr, tcol), omap)),
        out_shape=_sds(out_shape), compiler_params=_params("parallel", "parallel"),
    )(place, pair, got)


def _fold_forwarded(pair, fold, nbr_chips, name):
    _, r, c = fold.shape
    tr, tcol = _tile(r, 512, 16), _tile(c, WIDE_TILE, LANES)
    nrb, ncb = r // tr, c // tcol

    def body(nbr_ref, px_ref, py_ref, f_ref, out_ref):
        out_ref[0] = (px_ref[...].astype(F32) + f_ref[1].astype(F32)).astype(out_ref.dtype)
        out_ref[1] = (py_ref[...].astype(F32) + f_ref[0].astype(F32)).astype(out_ref.dtype)

    both = pl.BlockSpec((2, tr, tcol), lambda i, j, nb: (0, i, j))
    return pl.pallas_call(
        body, name=name,
        grid_spec=pltpu.PrefetchScalarGridSpec(
            num_scalar_prefetch=1, grid=(nrb, ncb),
            in_specs=[pl.BlockSpec((tr, tcol), lambda i, j, nb: (i, nb[0] * ncb + j)),
                      pl.BlockSpec((tr, tcol), lambda i, j, nb: (nrb + i, nb[1] * ncb + j)), both],
            out_specs=both),
        out_shape=_sds(fold.shape, fold.dtype), compiler_params=_params("parallel", "parallel"),
    )(nbr_chips, pair, pair, fold)


def _adamw(w, g, m, v, name, copy_grad=True):
    r, c = w.shape
    tr, tcol = _tile(r, 256, SUBLANES), _tile(c, WIDE_TILE, LANES)
    blk = pl.BlockSpec((tr, tcol), lambda i, j: (i, j))
    n_out = 4 if copy_grad else 3

    def body(w_ref, g_ref, m_ref, v_ref, d_ref, nm_ref, nv_ref, *g_out_ref):
        g_ = g_ref[...]
        if copy_grad:
            g_out_ref[0][...] = g_
        m_ = ADAM_B1 * m_ref[...] + (1.0 - ADAM_B1) * g_
        v_ = ADAM_B2 * v_ref[...] + (1.0 - ADAM_B2) * (g_ * g_)
        m_hat = m_ / (1.0 - ADAM_B1 ** ADAM_STEP)
        v_hat = v_ / (1.0 - ADAM_B2 ** ADAM_STEP)
        d_ref[...] = -ADAM_LR * (m_hat / (jnp.sqrt(v_hat) + ADAM_EPS) + ADAM_WD * w_ref[...])
        nm_ref[...] = m_
        nv_ref[...] = v_

    return _call(body, name=name, grid=(r // tr, c // tcol), in_specs=[blk] * 4, out_specs=[blk] * n_out,
                 out_shape=[_sds((r, c))] * n_out, sem=("parallel", "parallel"))(w, g, m, v)


def _place():
    x, y, c = lax.axis_index("x"), lax.axis_index("y"), lax.axis_index("c")
    chips = [(1 - x, y), (x, 1 - y), (1 - x, 1 - y)]
    return x, y, c, chips


def _all_gather8(blk, name):
    m, n = blk.shape

    def body(x_ref, out_ref, send_sems, recv_sems, local_sem):
        x, y, c, chips = _place()
        me, sibling = (x, y, c), (x, y, 1 - c)

        def rows(px, py, pc):
            return out_ref.at[pl.ds((4 * px + 2 * py + pc) * m, m), :]

        def copy(k, block, to, src=None):
            return pltpu.make_async_remote_copy(
                src_ref=rows(*block) if src is None else src, dst_ref=rows(*block), send_sem=send_sems.at[k],
                recv_sem=recv_sems.at[k], device_id=to, device_id_type=MESH_ID)

        mine = pltpu.make_async_copy(x_ref, rows(*me), local_sem)
        mine.start()
        first = [copy(0, me, sibling, src=x_ref)]
        first += [copy(1 + j, me, (*chip, c), src=x_ref) for j, chip in enumerate(chips)]
        for cp in first:
            cp.start()
        passed = [copy(4 + j, (*chip, c), sibling) for j, chip in enumerate(chips)]
        for j, chip in enumerate(chips):
            copy(1 + j, (*chip, c), me).wait_recv()
            passed[j].start()
        copy(0, sibling, me).wait_recv()
        for j, chip in enumerate(chips):
            copy(4 + j, (*chip, 1 - c), me).wait_recv()
        for cp in first + passed:
            cp.wait_send()
        mine.wait()

    return pl.pallas_call(
        body, name=name, out_shape=_sds((N_DEV * m, n), blk.dtype), in_specs=[ANY], out_specs=ANY,
        scratch_shapes=[pltpu.SemaphoreType.DMA((7,)), pltpu.SemaphoreType.DMA((7,)), pltpu.SemaphoreType.DMA],
    )(blk)


def _half(ref, core, split_rows):
    r, c = ref.shape
    if split_rows:
        return ref.at[pl.ds(core * (r // 2), r // 2), :]
    return ref.at[:, pl.ds(core * (c // 2), c // 2)]


def _chip_block(ref, j, col_sharded):
    r, c = ref.shape
    if col_sharded:
        return ref.at[:, pl.ds(j * (c // N_CHIPS), c // N_CHIPS)]
    return ref.at[pl.ds(j * (r // N_CHIPS), r // N_CHIPS), :]


def _rows_part(ref, part):
    lo, hi, n = part
    r = ref.shape[0]
    return ref if (lo, hi) == (0, n) else ref.at[pl.ds(lo * (r // n), (hi - lo) * (r // n)), :]


def _copy(send_sems, recv_sems, k, src, dst, to):
    return pltpu.make_async_remote_copy(src_ref=src, dst_ref=dst, send_sem=send_sems.at[k], recv_sem=recv_sems.at[k],
                                        device_id=to, device_id_type=MESH_ID)


def _in_place(arrays):
    return tuple(arrays), tuple(_sds(a.shape, a.dtype) for a in arrays), {i: i for i in range(len(arrays))}


def _gather_ici(fulls, col_sharded, part=(0, 1, 1)):
    nw = len(fulls)

    def build(_, refs, send_sems, recv_sems, sem0):
        x, y, c, chips = _place()
        sends, recvs = [], []
        for w in range(nw):
            win = lambda j: _rows_part(_half(_chip_block(refs[w], j, col_sharded[w]), c, True), part)
            for k, (cx, cy) in enumerate(chips):
                sem = sem0 + 3 * w + k
                sends.append(_copy(send_sems, recv_sems, sem, win(2 * x + y), win(2 * x + y), (cx, cy, c)))
                recvs.append(_copy(send_sems, recv_sems, sem, win(2 * cx + cy), win(2 * cx + cy), (cx, cy, c)))
        return sends, recvs

    return _Side(*_in_place(fulls), 3 * nw, build)


def _gather_d2d(fulls, col_sharded):
    nw = len(fulls)

    def build(_, refs, send_sems, recv_sems, sem0):
        x, y, c, chips = _place()
        sends, recvs = [], []
        for w in range(nw):
            win = lambda j, core: _half(_chip_block(refs[w], j, col_sharded[w]), core, True)
            for k, (cx, cy) in enumerate(chips):
                sem = sem0 + 3 * w + k
                sends.append(_copy(send_sems, recv_sems, sem, win(2 * cx + cy, c), win(2 * cx + cy, c), (x, y, 1 - c)))
                recvs.append(_copy(send_sems, recv_sems, sem, win(2 * cx + cy, 1 - c), win(2 * cx + cy, 1 - c),
                                   (x, y, 1 - c)))
        return sends, recvs

    return _Side(*_in_place(fulls), 3 * nw, build)


def _gather_neighbours(fulls, col_sharded, part=(0, 1, 1)):
    nw = len(fulls)

    def build(_, refs, send_sems, recv_sems, sem0):
        x, y, c, chips = _place()
        sends, recvs = [], []
        for w in range(nw):
            win = lambda j: _rows_part(_half(_chip_block(refs[w], j, col_sharded[w]), c, True), part)
            for k, (cx, cy) in enumerate(chips[:2]):
                sem = sem0 + 2 * w + k
                sends.append(_copy(send_sems, recv_sems, sem, win(2 * x + y), win(2 * x + y), (cx, cy, c)))
                recvs.append(_copy(send_sems, recv_sems, sem, win(2 * cx + cy), win(2 * cx + cy), (cx, cy, c)))
        return sends, recvs

    return _Side(*_in_place(fulls), 2 * nw, build)


def _gather_forward(fulls, col_sharded, ways=(True, True), siblings=False):
    nw = len(fulls)

    def build(_, refs, send_sems, recv_sems, sem0):
        x, y, c, (cx_, cy_, cd_) = _place()
        sends, recvs = [], []
        for w in range(nw):
            win = lambda chip, core: _half(_chip_block(refs[w], 2 * chip[0] + chip[1], col_sharded[w]), core, True)
            part = lambda ref, p: _rows_part(ref, (p, p + 1, 2))
            for p, (src, to) in enumerate(((cx_, cy_), (cy_, cx_))):
                if ways[p]:
                    sem = sem0 + 4 * w + p
                    sends.append(_copy(send_sems, recv_sems, sem, part(win(src, c), p), part(win(src, c), p), (*to, c)))
                    recvs.append(_copy(send_sems, recv_sems, sem, part(win(cd_, c), p), part(win(cd_, c), p), (*to, c)))
            if siblings:
                for k, chip in enumerate((cx_, cy_)):
                    sem = sem0 + 4 * w + 2 + k
                    sends.append(_copy(send_sems, recv_sems, sem, win(chip, c), win(chip, c), (x, y, 1 - c)))
                    recvs.append(_copy(send_sems, recv_sems, sem, win(chip, 1 - c), win(chip, 1 - c), (x, y, 1 - c)))
        return sends, recvs

    return _Side(*_in_place(fulls), 4 * nw, build)


def _gather_diagonal_d2d(full, col_sharded):
    def build(_, refs, send_sems, recv_sems, sem0):
        x, y, c, chips = _place()
        cx, cy = chips[2]
        win = lambda core: _half(_chip_block(refs[0], 2 * cx + cy, col_sharded), core, True)
        return ([_copy(send_sems, recv_sems, sem0, win(c), win(c), (x, y, 1 - c))],
                [_copy(send_sems, recv_sems, sem0, win(1 - c), win(1 - c), (x, y, 1 - c))])

    return _Side(*_in_place([full]), 1, build)


def _scatter_forward_1(pair):
    r, n = pair.shape
    results = (_sds((2, r, n // N_CHIPS), pair.dtype), _sds((2, r // 2, n // N_CHIPS), pair.dtype))

    def build(refs, res, send_sems, recv_sems, sem0):
        x, y, c, (cx_, cy_, cd_) = _place()
        region = lambda chip, p: _rows_part(_chip_block(refs[0], 2 * chip[0] + chip[1], True), (p, p + 1, 2))
        got, fold = res
        copies = [
            _copy(send_sems, recv_sems, sem0, region(cx_, 1), _rows_part(got.at[0], (1, 2, 2)), (*cx_, c)),
            _copy(send_sems, recv_sems, sem0 + 1, region(cd_, 1), fold.at[0], (*cx_, c)),
            _copy(send_sems, recv_sems, sem0 + 2, region(cy_, 0), _rows_part(got.at[1], (0, 1, 2)), (*cy_, c)),
            _copy(send_sems, recv_sems, sem0 + 3, region(cd_, 0), fold.at[1], (*cy_, c))]
        return copies, copies

    return _Side((pair,), results, {}, 4, build)


def _scatter_forward_2(passed, got):
    def build(refs, res, send_sems, recv_sems, sem0):
        x, y, c, (cx_, cy_, _) = _place()
        copies = [_copy(send_sems, recv_sems, sem0, refs[0].at[0], _rows_part(res[0].at[0], (0, 1, 2)), (*cx_, c)),
                  _copy(send_sems, recv_sems, sem0 + 1, refs[0].at[1], _rows_part(res[0].at[1], (1, 2, 2)), (*cy_, c))]
        return copies, copies

    return _Side((passed, got), (_sds(got.shape, got.dtype),), {1: 0}, 2, build)


def _exchange(side, name):
    return _call(None, side=side, name=name)()[1]


def _swap_halves(grads, col_sharded):
    nw = len(grads)
    out_shapes = [_sds((g.shape[0] // 2, g.shape[1]) if col else (g.shape[0], g.shape[1] // 2), g.dtype)
                  for g, col in zip(grads, col_sharded)]

    def build(g_refs, o_refs, send_sems, recv_sems, sem0):
        x, y, c, _ = _place()
        copies = [_copy(send_sems, recv_sems, sem0 + w, _half(g_refs[w], 1 - c, col_sharded[w]), o_refs[w],
                        (x, y, 1 - c)) for w in range(nw)]
        return copies, copies

    return _Side(tuple(grads), tuple(out_shapes), {}, nw, build)


def _send_to_sibling(arrays):
    nw = len(arrays)

    def build(refs, o_refs, send_sems, recv_sems, sem0):
        x, y, c, _ = _place()
        copies = [_copy(send_sems, recv_sems, sem0 + w, refs[w], o_refs[w], (x, y, 1 - c)) for w in range(nw)]
        return copies, copies

    return _Side(tuple(arrays), tuple(_sds(a.shape, a.dtype) for a in arrays), {}, nw, build)


def _scatter_regions(pairs, col_sharded, part=(0, 1, 1), into=None):
    nw = len(pairs)

    def region_shape(p, col):
        return (p.shape[0], p.shape[1] // N_CHIPS) if col else (p.shape[0] // N_CHIPS, p.shape[1])

    out_shapes = tuple(_sds((N_CHIPS - 1, *region_shape(p, col)), p.dtype) for p, col in zip(pairs, col_sharded))

    def build(refs, o_refs, send_sems, recv_sems, sem0):
        x, y, c, chips = _place()
        copies = []
        for w in range(nw):
            for k, (cx, cy) in enumerate(chips):
                copies.append(_copy(
                    send_sems, recv_sems, sem0 + 3 * w + k,
                    _rows_part(_chip_block(refs[w], 2 * cx + cy, col_sharded[w]), part),
                    _rows_part(o_refs[w].at[k], part), (cx, cy, c)))
        return copies, copies

    if into is None:
        return _Side(tuple(pairs), out_shapes, {}, 3 * nw, build)
    return _Side((*pairs, *into), out_shapes, {nw + w: w for w in range(nw)}, 3 * nw, build)


def _join_halves(halves, col_sharded):
    nw = len(halves)

    def build(_, refs, send_sems, recv_sems, sem0):
        x, y, c, _ = _place()
        sends, recvs = [], []
        for w in range(nw):
            mine, theirs = _half(refs[w], c, col_sharded[w]), _half(refs[w], 1 - c, col_sharded[w])
            sends.append(_copy(send_sems, recv_sems, sem0 + w, mine, mine, (x, y, 1 - c)))
            recvs.append(_copy(send_sems, recv_sems, sem0 + w, theirs, theirs, (x, y, 1 - c)))
        return sends, recvs

    return _Side(*_in_place(halves), nw, build)


def _part_rows(size):
    return -(-size // (SUBLANES * LANES)) * SUBLANES


def _pack(arrays, pad_rows_to=SUBLANES):
    flat = [jnp.pad(a.reshape(-1), (0, _part_rows(a.size) * LANES - a.size)).reshape(-1, LANES) for a in arrays]
    rows = sum(f.shape[0] for f in flat)
    pad = (-rows) % pad_rows_to
    if pad:
        flat.append(jnp.zeros((pad, LANES), F32))
    return jnp.concatenate(flat, axis=0)


def _unpack(packed, shapes):
    out, r = [], 0
    for shp in shapes:
        size = math.prod(shp)
        out.append(packed[r:r + _part_rows(size)].reshape(-1)[:size].reshape(shp))
        r += _part_rows(size)
    return out


def _rope_tables(n_ctx_rows, s):
    rows = s // GRID_W
    row_idx = jnp.repeat(jnp.arange(rows), GRID_W)
    col_idx = jnp.tile(jnp.arange(GRID_W), rows)
    n_freq = LANES // 4
    inv_freq = ROPE_THETA ** (-jnp.arange(n_freq, dtype=F32) / n_freq)
    ang = jnp.concatenate([row_idx.astype(F32)[:, None] * inv_freq, col_idx.astype(F32)[:, None] * inv_freq], axis=-1)
    cos = jnp.repeat(jnp.cos(ang), 2, axis=-1)
    sin = jnp.repeat(jnp.sin(ang), 2, axis=-1) * jnp.tile(jnp.array([-1.0, 1.0], F32), LANES // 2)
    cos = jnp.concatenate([jnp.ones((n_ctx_rows, LANES), F32), cos], axis=0)
    sin = jnp.concatenate([jnp.zeros((n_ctx_rows, LANES), F32), sin], axis=0)
    return cos, sin


WEIGHT_NAMES = ['c_ctx', 'w_mod', 'b_mod', 'g_mix', 'g_mlp', 'w_in', 'q_gain', 'k_gain', 'conv_w', 'conv_b', 'w_rg',
                'b_rg', 'w_ig', 'b_ig', 'lru_lambda', 'w_o_attn', 'w_o_rnn', 'w_out', 'w_up', 'w_down', 'g_final']
BIG = ['w_in', 'w_o_attn', 'w_o_rnn', 'w_out', 'w_up', 'w_down']
BIG_COL_SHARDED = [True, False, False, False, True, False]
GATES = ['w_rg', 'w_ig']
SMALL = ['c_ctx', 'b_mod', 'g_mix', 'g_mlp', 'q_gain', 'k_gain', 'conv_b', 'g_final',
         'conv_w', 'b_rg', 'b_ig', 'lru_lambda']


def kernel(x, c, ctx, c_ctx, w_mod, b_mod, g_mix, g_mlp, w_in, q_gain, k_gain, conv_w, conv_b, w_rg, b_rg, w_ig, b_ig, lru_lambda, w_o_attn, w_o_rnn, w_out, w_up, w_down, g_final, loss_target, m_c_ctx, m_w_mod, m_b_mod, m_g_mix, m_g_mlp, m_w_in, m_q_gain, m_k_gain, m_conv_w, m_conv_b, m_w_rg, m_b_rg, m_w_ig, m_b_ig, m_lru_lambda, m_w_o_attn, m_w_o_rnn, m_w_out, m_w_up, m_w_down, m_g_final, v_c_ctx, v_w_mod, v_b_mod, v_g_mix, v_g_mlp, v_w_in, v_q_gain, v_k_gain, v_conv_w, v_conv_b, v_w_rg, v_b_rg, v_w_ig, v_b_ig, v_lru_lambda, v_w_o_attn, v_w_o_rnn, v_w_out, v_w_up, v_w_down, v_g_final):
    given = dict(locals())
    weights = {n: given[n] for n in WEIGHT_NAMES}
    moms = {n: given["m_" + n] for n in WEIGHT_NAMES}
    vars_ = {n: given["v_" + n] for n in WEIGHT_NAMES}

    s, d = x.shape[1], x.shape[2]
    n_ctx = ctx.shape[1]
    t = n_ctx + s
    hd = q_gain.shape[1]
    assert hd == LANES and w_rg.shape[-1] == LANES
    attn_w = w_o_attn.shape[1] * N_CHIPS
    n_in = w_in.shape[2] * N_CHIPS
    kv_w = (n_in - attn_w - 4 * d) // 2
    group = attn_w // kv_w
    k_off, v_off, xr_off = attn_w, attn_w + kv_w, attn_w + 2 * kv_w
    xg_off, gl_off = xr_off + d, xr_off + 2 * d
    d_mod = N_MOD * d
    tr = _tile(math.gcd(n_ctx, s), 256, 16)
    tcol = _tile(math.gcd(d, xr_off), 1024, LANES)
    xi, yi, ci = lax.axis_index("x"), lax.axis_index("y"), lax.axis_index("c")
    chip = 2 * xi + yi
    core = ci.astype(jnp.int32).reshape(1)

    sharded_small = [conv_w[0], b_rg[0], b_ig[0], lru_lambda[0]]
    pack0 = _pack([c[0]] + sharded_small)
    got0 = _all_gather8(pack0, "gather_small_inputs").reshape(N_DEV, -1, LANES)
    c_all = got0[:, :_part_rows(d)].reshape(N_DEV, -1)[:, :d]
    per_chip = [_unpack(got0[2 * j, _part_rows(d):], [a.shape for a in sharded_small]) for j in range(N_CHIPS)]
    conv_w_f, b_rg_f, b_ig_f, lam_f = (jnp.concatenate([per_chip[j][i] for j in range(N_CHIPS)], axis=-1)
                                       for i in range(4))
    c16 = jnp.concatenate([c_all, c_ctx[None, :], jnp.zeros((16 - N_DEV - 1, d), F32)], axis=0)
    b_mod_shard = lax.dynamic_slice(b_mod, (0, chip * (d_mod // N_CHIPS)), (1, d_mod // N_CHIPS))
    mod_part, silu16 = _mod_fwd(c16, w_mod[0], b_mod_shard)
    mod_all = _all_gather8(mod_part, "gather_mod").reshape(N_DEV, 16, d_mod // N_CHIPS)
    mod16 = jnp.concatenate([mod_all[2 * j] for j in range(N_CHIPS)], axis=-1)
    me = 4 * xi + 2 * yi + ci
    mod_lat = lax.dynamic_slice(mod16, (me, 0), (1, d_mod)).reshape(N_MOD, d)
    mod_ctx = mod16[N_DEV].reshape(N_MOD, d)
    mod4 = jnp.stack([mod_ctx[0], mod_ctx[1], mod_lat[0], mod_lat[1]])
    mod3 = jnp.stack([mod_lat[2], mod_lat[3], mod_lat[4]])
    gate_f = mod_lat[5][None, :]

    chip_arr = chip.astype(jnp.int32).reshape(1)
    own = {n: _cast_into_window(weights[n][0], chip_arr, col, "cast_" + n) for n, col in zip(BIG, BIG_COL_SHARDED)}
    place = jnp.stack([chip, ci]).astype(jnp.int32)
    row3 = [False] * 3

    def pair_sum(n, full, other, col):
        return _add_half(full, other, core, col, "pair_sum_" + n)

    def chip_sum(n, pair, got, col):
        return _sum_regions(pair, got, place, col, "chip_sum_" + n)

    (w_in_f,) = _exchange(_gather_neighbours([own['w_in']], [True]), "gather_w_in_nbr")
    (w_in_f,) = _exchange(_gather_forward([w_in_f], [True], siblings=True), "gather_w_in_fwd")
    (w_in_f,) = _exchange(_gather_diagonal_d2d(w_in_f, True), "gather_w_in_diag")
    cos, sin = _rope_tables(n_ctx, s)
    h = _norm_mod_fwd(ctx[0], x[0], g_mix, mod4, tr)
    z, (w_oa_f, w_or_f, w_out_f, w_up_f) = _matmul(h, w_in_f, name="mm_in", side=_sides(
        _gather_neighbours([own['w_o_attn'], own['w_o_rnn'], own['w_out']], row3),
        _gather_neighbours([own['w_up']], [True], (0, 1, 2))))
    qr, (w_oa_f, w_or_f, w_out_f) = _head_prep_fwd(z, 0, attn_w // LANES, q_gain, cos, sin, tr, "q_prep",
                                                   side=_gather_forward([w_oa_f, w_or_f, w_out_f], row3))
    kr = _head_prep_fwd(z, k_off, kv_w // LANES, k_gain, cos, sin, tr, "k_prep")
    (attn_o, lse), (w_up_f, w_down_f, w_oa_f, w_or_f, w_out_f) = _attn_fwd(
        qr, kr, z, v_off, n_ctx, group, tr, side=_sides(
            _on_same(_gather_neighbours([w_up_f], [True], (1, 2, 2)), _gather_forward([w_up_f], [True], ways=(True, False))),
            _gather_neighbours([own['w_down']], [False], (0, 1, 2)), _gather_d2d([w_oa_f, w_or_f, w_out_f], row3)))
    (xc, a_f, bx_f, a_b, bx_b), (w_up_f, w_down_f) = _rnn_prep(
        z, xr_off, conv_w_f, conv_b, w_rg[0], b_rg_f, w_ig[0], b_ig_f, lam_f, n_ctx,
        side=_sides(_gather_forward([w_up_f], [True], ways=(False, True)),
                    _gather_neighbours([w_down_f], [False], (1, 2, 2))))
    (h_f, h_b), (w_up_f, w_down_f) = _scan(
        [(a_f, bx_f, "ctx_lat_up"), (a_b, bx_b, "ctx_lat_down")], post=False, n_ctx_rows=n_ctx, name="scan_fwd",
        side=_sides(_gather_d2d([w_up_f], [True]), _gather_forward([w_down_f], [False])))
    u = _rnn_gate_fwd(h_f, h_b, z, xg_off, n_ctx, tr, tcol)
    y_attn = _matmul(attn_o, w_oa_f, name="mm_o_attn")
    y_rnn = _matmul(u, w_or_f, name="mm_o_rnn")
    mrg = _merge_fwd(y_attn, y_rnn, z, gl_off, n_ctx, tr, tcol)
    mix = _matmul(mrg, w_out_f, name="mm_out")
    x1, h2 = _resid_norm_mod_fwd(x[0], mix, g_mlp, mod3, tr)
    up, (w_down_f,) = _matmul(h2, w_up_f, name="mm_up", side=_gather_d2d([w_down_f], [False]))
    act = _sq_relu(up, tr, tcol)
    down = _matmul(act, w_down_f, name="mm_down")
    dx2, d_down, sums_fin, loss_blk = _final_fwd_bwd(x1, down, loss_target[0], g_final[None, :], gate_f, tr)

    d_up = _matmul(d_down, w_down_f, tb=True, out_dtype=BF16, name="mm_d_up",
                   post=lambda d_act, up_: d_act * 2.0 * jnp.maximum(up_, 0.0), post_args=(up,))
    g_w_down = _matmul(act, d_down, ta=True, out_dtype=BF16, name="mm_g_down")

    def scatter(pairs, cols, lo, hi, into=None):
        return _scatter_regions(pairs, cols, (lo, hi, 8), into)

    dh2, (got,) = _matmul(d_up, w_up_f, tb=True, name="mm_d_h2", side=_swap_halves([g_w_down], [False]))
    p_down = pair_sum('w_down', g_w_down, got, False)
    g_w_up, got_down = _matmul(h2, d_up, ta=True, out_dtype=BF16, name="mm_g_up", side=scatter([p_down], [False], 0, 3))
    (dx1, d_mix, sums2), (got, *got_down) = _norm_mod_bwd2(x1, dh2, dx2, mix, g_mlp, mod3, tr, side=_sides(
        _swap_halves([g_w_up], [True]), scatter([p_down], [False], 3, 4, got_down)))
    p_up = pair_sum('w_up', g_w_up, got, True)
    d_mrg = _matmul(d_mix, w_out_f, tb=True, name="mm_d_mrg")
    g_w_out = _matmul(mrg, d_mix, ta=True, out_dtype=BF16, name="mm_g_out")
    dz = _dz_start(t, n_in, n_ctx)
    d_ya, dz = _merge_bwd(d_mrg, y_attn, z, gl_off, dz, n_ctx, tr, tcol, "merge_bwd_attn")
    d_yr, dz = _merge_bwd(d_mrg, y_rnn, z, gl_off + d, dz, n_ctx, tr, tcol, "merge_bwd_rnn")
    d_o = _matmul(d_ya, w_oa_f, tb=True, out_dtype=BF16, name="mm_d_o")
    g_w_oa = _matmul(attn_o, d_ya, ta=True, out_dtype=BF16, name="mm_g_o_attn")
    d_u = _matmul(d_yr, w_or_f, tb=True, name="mm_d_u")
    g_w_or = _matmul(u, d_yr, ta=True, out_dtype=BF16, name="mm_g_o_rnn")
    d_rnn, dz = _rnn_gate_bwd(d_u, h_f, h_b, z, xg_off, dz, n_ctx, tr, tcol)
    gs_f, gs_b = _scan([(a_f, d_rnn, "lat_ctx_down"), (a_b, d_rnn, "lat_ctx_up")], post=True, n_ctx_rows=n_ctx,
                       name="scan_bwd")
    o_names, o_grads = ['w_o_attn', 'w_o_rnn', 'w_out'], [g_w_oa, g_w_or, g_w_out]
    (dz, g_w_rg, g_w_ig, sums_rnn), (got_down, *got_o) = _rnn_bwd(
        z, xr_off, xc, gs_f, gs_b, h_f, h_b, conv_w_f, w_rg[0], b_rg_f, w_ig[0], b_ig_f, lam_f, dz, n_ctx,
        side=_sides(scatter([p_down], [False], 4, 8, got_down), _swap_halves(o_grads, row3)))
    hs_down = chip_sum('w_down', p_down, got_down, False)
    p_o = [pair_sum(n, g, o, False) for n, g, o in zip(o_names, o_grads, got_o)]
    gate_cols = 8 * LANES if g_w_rg.size % (8 * LANES * N_CHIPS * 16) == 0 else 2 * LANES
    gate_rows = g_w_rg.size // gate_cols
    gate_grads = [g_w_rg.reshape(gate_rows, gate_cols), g_w_ig.reshape(gate_rows, gate_cols)]
    (dq, dk, dz), (got_up, got_oa, got_rg, got_ig, gs_down) = _attn_bwd(
        qr, kr, z, v_off, d_o, attn_o, lse, dz, n_ctx, group, tr, side=_sides(
            _scatter_regions([p_up], [True]), _scatter_regions(p_o[:1], row3[:1]), _swap_halves(gate_grads, row3[:2]),
            _join_halves([hs_down], [False])))
    hs_up = chip_sum('w_up', p_up, got_up, True)
    hs_oa = chip_sum('w_o_attn', p_o[0], got_oa, False)
    p_gate = [pair_sum(n, g, o, False) for n, g, o in zip(GATES, gate_grads, (got_rg, got_ig))]
    dz, g_q_gain = _head_prep_bwd(z, 0, attn_w // LANES, q_gain, cos, sin, dq, n_ctx, tr, dz, "q_prep_bwd")
    dz, g_k_gain = _head_prep_bwd(z, k_off, kv_w // LANES, k_gain, cos, sin, dk, 0, tr, dz, "k_prep_bwd")
    half = d // 2
    h_sibling = lax.dynamic_slice(h, (0, (1 - ci) * half), (t, half))
    h_own = lax.dynamic_slice(h, (0, ci * half), (t, half))
    late = p_o[1:] + p_gate
    g_sibling, (got_rg, got_ig, *got_late) = _matmul(
        h_sibling, dz, ta=True, out_dtype=BF16, name="mm_g_in_sibling", side=_sides(
            _scatter_regions(p_gate, row3[:2]), scatter(p_o[1:], row3[:2], 0, 4)))
    g_own, (got_or, got_out, got, gs_up, gs_oa) = _matmul(
        h_own, dz, ta=True, out_dtype=BF16, name="mm_g_in_own", side=_sides(
            scatter(p_o[1:], row3[:2], 4, 8, got_late), _send_to_sibling([g_sibling]), _join_halves([hs_up], [True]),
            _join_halves([hs_oa], [False])))
    hs_late = [chip_sum(n, p, o, False) for n, p, o in zip(o_names[1:] + GATES, late, (got_or, got_out, got_rg, got_ig))]
    p_in = _add_pair(g_own, got, "pair_sum_w_in")
    dh, (got_in, fold_in, gs_or, gs_out) = _matmul(dz, w_in_f, tb=True, name="mm_d_h", side=_sides(
        _scatter_forward_1(p_in), _join_halves(hs_late[:2], row3[:2])))
    nbr_chips = jnp.stack([2 * (1 - xi) + yi, 2 * xi + 1 - yi]).astype(jnp.int32)
    passed = _fold_forwarded(p_in, fold_in, nbr_chips, "fold_w_in")
    (got_in,) = _exchange(_scatter_forward_2(passed, got_in), "scatter_w_in_fwd")
    grad_x, sums1 = _norm_mod_bwd1(ctx[0], x[0], dh, dx1, g_mix, mod4, tr)

    zeros_d = jnp.zeros((d,), F32)
    dmod_lat = jnp.concatenate([sums1[0], sums1[1], sums2[3], sums2[0], sums2[1], sums_fin[1]])
    dmod_ctx = jnp.concatenate([sums1[3], sums1[4]] + [zeros_d] * 4)
    small_parts = [dmod_lat, dmod_ctx, loss_blk[0, 0:1], sums1[2] + sums1[5], sums2[2], g_q_gain[0], g_k_gain[0],
                   sums_rnn[10], sums_fin[0], sums_rnn[6:10], sums_rnn[0:2], sums_rnn[2:4], sums_rnn[4:6]]
    pack1 = _pack(small_parts)
    got1 = _all_gather8(pack1, "gather_small_grads").reshape(N_DEV, -1, LANES)
    tot1 = _sum_leading(got1, "sum_small_grads")
    part_shapes = [a.shape for a in small_parts]
    (s_dmod_lat, s_dmod_ctx, s_loss, g_g_mix, g_g_mlp, g_q_gain, g_k_gain, g_conv_b, g_g_final,
     g_conv_w_f, g_b_rg_f, g_b_ig_f, g_lam_f) = _unpack(tot1, part_shapes)
    loss = s_loss[0]
    g_b_mod = (s_dmod_lat + s_dmod_ctx)[None, :]
    n_mod_rows = _part_rows(d_mod)
    dmod16 = jnp.concatenate([got1[:, :n_mod_rows].reshape(N_DEV, -1)[:, :d_mod], s_dmod_ctx[None, :],
                              jnp.zeros((16 - N_DEV - 1, d_mod), F32)], axis=0)
    dmod16_shard = lax.dynamic_slice(dmod16, (0, chip * (d_mod // N_CHIPS)), (16, d_mod // N_CHIPS))
    g_w_mod = _matmul(silu16, dmod16_shard, ta=True, name="mm_g_mod")
    dsilu_part = _matmul(dmod16_shard[N_DEV:], w_mod[0], tb=True, name="mm_d_silu")
    dsilu_all = _all_gather8(dsilu_part, "gather_d_silu").reshape(N_DEV, 8, d)
    g_c_ctx = _c_ctx_grad(dsilu_all, c_ctx[None, :])[0]

    def shard_of(full):
        w = full.shape[-1] // N_CHIPS
        return lax.dynamic_slice(full, (0, chip * w), (full.shape[0], w))

    grads = {
        'c_ctx': g_c_ctx, 'b_mod': g_b_mod, 'g_mix': g_g_mix[None, :], 'g_mlp': g_g_mlp[None, :],
        'q_gain': g_q_gain[None, :], 'k_gain': g_k_gain[None, :], 'conv_b': g_conv_b[None, :],
        'g_final': g_g_final,
        'conv_w': shard_of(g_conv_w_f)[None], 'b_rg': shard_of(g_b_rg_f)[None], 'b_ig': shard_of(g_b_ig_f)[None],
        'lru_lambda': shard_of(g_lam_f)[None], 'w_mod': g_w_mod[None],
    }

    delta, new_m, new_v = {}, {}, {}

    def adamw(n):
        shp = weights[n].shape
        as2d = (lambda a: a[0]) if n not in GATES else (lambda a: a.reshape(-1, LANES))
        dl, nm, nv, *g = _adamw(as2d(weights[n]), as2d(grads[n]), as2d(moms[n]), as2d(vars_[n]), "adamw_" + n,
                                copy_grad=n in BIG)
        delta[n], new_m[n], new_v[n] = dl.reshape(shp), nm.reshape(shp), nv.reshape(shp)
        if g:
            grads[n] = g[0].reshape(shp)

    hs_in = chip_sum('w_in', p_in, got_in, True)
    (gs_in,) = _exchange(_join_halves([hs_in], [True]), "join_w_in")
    for n, g in zip(['w_in', 'w_o_attn', 'w_o_rnn', 'w_out', 'w_up', 'w_down'], [gs_in, gs_oa, gs_or, gs_out, gs_up, gs_down]):
        grads[n] = g[None]
    half_cols = gate_cols // 2
    mine = jnp.concatenate([lax.dynamic_slice(hs, (0, ci * half_cols), (gate_rows // N_CHIPS, half_cols))
                            for hs in hs_late[2:]], axis=0)
    gate_all = _all_gather8(mine, "gather_gate_grads")
    gate_all = gate_all.reshape(N_CHIPS, 2, len(GATES), gate_rows // N_CHIPS, half_cols)
    for i, n in enumerate(GATES):
        grads[n] = jnp.moveaxis(gate_all[:, :, i], 1, 2).reshape(weights[n].shape)
    for n in ['w_mod'] + BIG + GATES:
        adamw(n)
    small_shapes = [weights[n].shape for n in SMALL]
    packed = [_pack([src[n] for n in SMALL], 512) for src in (weights, grads, moms, vars_)]
    outs = _adamw(*packed, "adamw_small", copy_grad=False)
    for res, out in zip((delta, new_m, new_v), outs):
        for n, a in zip(SMALL, _unpack(out, small_shapes)):
            res[n] = a
    return (loss, grad_x[None], *[grads[n] for n in WEIGHT_NAMES], *[delta[n] for n in WEIGHT_NAMES],
            *[new_m[n] for n in WEIGHT_NAMES], *[new_v[n] for n in WEIGHT_NAMES])
```

```python
import functools
import math
from typing import Callable, NamedTuple

import jax
import jax.numpy as jnp
from jax import lax
from jax.experimental import pallas as pl
from jax.experimental.pallas import tpu as pltpu

F32 = jnp.float32
BF16 = jnp.bfloat16
MESH_ID = pl.DeviceIdType.MESH
ANY = pl.BlockSpec(memory_space=pl.ANY)

NORM_EPS = 1e-6
LRU_C = 8.0
GRID_W = 64
ROPE_THETA = 10000.0
N_MOD = 6
CONV_WIDTH = 4
ADAM_LR = 0.001
ADAM_B1 = 0.9
ADAM_B2 = 0.999
ADAM_EPS = 1e-08
ADAM_WD = 0.01
ADAM_STEP = 10

LANES = 128
SUBLANES = 8
V7X_VMEM_LIMIT = 48 * 1024 * 1024
WIDE_TILE = 11 * LANES
N_CHIPS = 4
N_DEV = 8
GELU_C = math.sqrt(2.0 / math.pi)
GELU_A = 0.044715


def _tile(dim, pref, align):
    t = min(pref, dim)
    t -= t % align
    while t >= align:
        if dim % t == 0:
            return t
        t -= align
    return dim


def _params(*sem):
    return pltpu.CompilerParams(dimension_semantics=sem, vmem_limit_bytes=V7X_VMEM_LIMIT)


def _sds(shape, dtype=F32):
    return jax.ShapeDtypeStruct(shape, dtype)


class _Side(NamedTuple):
    operands: tuple
    results: tuple
    aliases: dict
    n_sems: int
    build: Callable


def _sides(*sides):
    ops, res, aliases, spans, n = [], [], {}, [], 0
    for s in sides:
        spans.append((len(ops), len(res), n))
        aliases.update({len(ops) + i: len(res) + j for i, j in s.aliases.items()})
        ops += s.operands
        res += s.results
        n += s.n_sems

    def build(op_refs, res_refs, send_sems, recv_sems, sem0):
        sends, recvs = [], []
        for s, (o, r, k) in zip(sides, spans):
            a, b = s.build(op_refs[o:o + len(s.operands)], res_refs[r:r + len(s.results)], send_sems, recv_sems,
                           sem0 + k)
            sends += a
            recvs += b
        return sends, recvs

    return _Side(tuple(ops), tuple(res), aliases, n, build)


def _on_same(*sides):
    def build(ops, res, send_sems, recv_sems, sem0):
        sends, recvs = [], []
        for s in sides:
            a, b = s.build(ops, res, send_sems, recv_sems, sem0)
            sends += a
            recvs += b
            sem0 += s.n_sems
        return sends, recvs

    return _Side(sides[0].operands, sides[0].results, sides[0].aliases, sum(s.n_sems for s in sides), build)


def _call(body, *, side=None, sem=(), grid=(), in_specs=(), out_specs=(), out_shape=(), scratch_shapes=(), **kw):
    if side is None:
        return pl.pallas_call(body, grid=grid, in_specs=list(in_specs), out_specs=out_specs, out_shape=out_shape,
                              scratch_shapes=list(scratch_shapes), compiler_params=_params(*sem), **kw)
    aliases = kw.pop("input_output_aliases", {})
    many = isinstance(out_shape, (list, tuple))
    out_specs_l, out_shape_l = (list(out_specs), list(out_shape)) if many else ([out_specs], [out_shape])
    n_in, n_out, n_scr = len(in_specs), len(out_shape_l), len(scratch_shapes)
    n_op, n_res = len(side.operands), len(side.results)

    def hosted(*refs):
        ins, ops = refs[:n_in], refs[n_in:n_in + n_op]
        outs = refs[n_in + n_op:n_in + n_op + n_out]
        res = refs[n_in + n_op + n_out:n_in + n_op + n_out + n_res]
        scr = refs[n_in + n_op + n_out + n_res:-2]
        send_sems, recv_sems = refs[-2:]

        def start():
            for cp in side.build(ops, res, send_sems, recv_sems, 0)[0]:
                cp.start()

        def finish():
            sends, recvs = side.build(ops, res, send_sems, recv_sems, 0)
            for cp in recvs:
                cp.wait_recv()
            for cp in sends:
                cp.wait_send()

        if not grid:
            start()
            finish()
            return
        ids = [pl.program_id(a) for a in range(len(grid))]
        first = functools.reduce(jnp.logical_and, [i == 0 for i in ids])
        last = functools.reduce(jnp.logical_and, [i == g - 1 for i, g in zip(ids, grid)])
        pl.when(first)(start)
        body(*ins, *outs, *scr)
        pl.when(last)(finish)

    def run(*args):
        got = pl.pallas_call(
            hosted, grid=grid, in_specs=[*in_specs, *[ANY] * n_op], out_specs=[*out_specs_l, *[ANY] * n_res],
            out_shape=[*out_shape_l, *side.results],
            scratch_shapes=[*scratch_shapes, pltpu.SemaphoreType.DMA((side.n_sems,)),
                            pltpu.SemaphoreType.DMA((side.n_sems,))],
            input_output_aliases={**aliases, **{n_in + i: n_out + j for i, j in side.aliases.items()}},
            compiler_params=_params(*["arbitrary"] * len(grid)), **kw)(*args, *side.operands)
        own = list(got[:n_out]) if many else got[0]
        return own, list(got[n_out:])

    return run


def _matmul(a, b, *, ta=False, tb=False, out_dtype=F32, name, tm=1024, tn=1024, tk=2816, side=None, post=None,
            post_args=()):
    k_dim, m = a.shape if ta else a.shape[::-1]
    n, k2 = b.shape if tb else b.shape[::-1]
    assert k_dim == k2, (a.shape, b.shape, ta, tb)
    tm = _tile(m, tm, LANES if ta else 16)
    tn = _tile(n, tn, 16 if tb else LANES)
    tk = _tile(k_dim, tk, LANES)
    nk = k_dim // tk
    dims = (((0 if ta else 1,), (1 if tb else 0,)), ((), ()))
    if nk == 1:
        def whole(a_ref, b_ref, *rest):
            acc = lax.dot_general(a_ref[...].astype(BF16), b_ref[...].astype(BF16), dims, preferred_element_type=F32)
            if post is not None:
                acc = post(acc, *[r[...] for r in rest[:-1]])
            rest[-1][...] = acc.astype(rest[-1].dtype)

        a_spec = pl.BlockSpec((tk, tm), lambda i, j: (0, i)) if ta else pl.BlockSpec((tm, tk), lambda i, j: (i, 0))
        b_spec = pl.BlockSpec((tn, tk), lambda i, j: (j, 0)) if tb else pl.BlockSpec((tk, tn), lambda i, j: (0, j))
        o_spec = pl.BlockSpec((tm, tn), lambda i, j: (i, j))
        return _call(
            whole, side=side, name=name, grid=(m // tm, n // tn), in_specs=[a_spec, b_spec] + [o_spec] * len(post_args),
            out_specs=o_spec, out_shape=_sds((m, n), out_dtype), sem=("parallel", "parallel"),
        )(a, b, *post_args)
    assert post is None

    def body(a_ref, b_ref, o_ref, acc_ref):
        k = pl.program_id(2)

        @pl.when(k == 0)
        def _():
            acc_ref[...] = jnp.zeros_like(acc_ref)

        acc_ref[...] += lax.dot_general(a_ref[...].astype(BF16), b_ref[...].astype(BF16), dims,
                                        preferred_element_type=F32)

        @pl.when(k == nk - 1)
        def _():
            o_ref[...] = acc_ref[...].astype(o_ref.dtype)

    a_spec = pl.BlockSpec((tk, tm), lambda i, j, k: (k, i)) if ta else pl.BlockSpec((tm, tk), lambda i, j, k: (i, k))
    b_spec = pl.BlockSpec((tn, tk), lambda i, j, k: (j, k)) if tb else pl.BlockSpec((tk, tn), lambda i, j, k: (k, j))
    return _call(
        body, side=side, name=name, grid=(m // tm, n // tn, nk), in_specs=[a_spec, b_spec],
        out_specs=pl.BlockSpec((tm, tn), lambda i, j, k: (i, j)), out_shape=_sds((m, n), out_dtype),
        scratch_shapes=[pltpu.VMEM((tm, tn), F32)], sem=("parallel", "parallel", "arbitrary"),
    )(a, b)


def _silu(x):
    return x * jax.nn.sigmoid(x)


def _gelu(x):
    return 0.5 * x * (1.0 + jnp.tanh(GELU_C * (x + GELU_A * x * x * x)))


def _gelu_grad(x):
    t = jnp.tanh(GELU_C * (x + GELU_A * x * x * x))
    return 0.5 * (1.0 + t) + 0.5 * x * (1.0 - t * t) * GELU_C * (1.0 + 3.0 * GELU_A * x * x)


def _expm1_nonpos(x):
    series = x * (1.0 + x * (1.0 / 2 + x * (1.0 / 6 + x * (1.0 / 24 + x * (1.0 / 120 + x * (1.0 / 720 + x / 5040))))))
    return jnp.where(x > -0.25, series, jnp.exp(x) - 1.0)


def _softplus(x):
    return jnp.maximum(x, 0.0) + jnp.log1p(jnp.exp(-jnp.abs(x)))


def _rms_stats(x):
    return lax.rsqrt(jnp.mean(x * x, axis=-1, keepdims=True) + NORM_EPS)


def _rms_bwd(dxhat, xhat, rstd):
    return rstd * (dxhat - xhat * jnp.mean(dxhat * xhat, axis=-1, keepdims=True))


def _colsum(v):
    return jnp.sum(v, axis=0, keepdims=True)


def _mod_fwd(c16, w_mod, b_mod_shard):
    r, d = c16.shape
    n = w_mod.shape[1]
    tn = _tile(n, 512, LANES)

    def body(c_ref, w_ref, b_ref, o_ref, s_ref):
        s = _silu(c_ref[...])
        s_ref[...] = s
        o_ref[...] = jnp.dot(s.astype(BF16), w_ref[...].astype(BF16), preferred_element_type=F32) + b_ref[...]

    return pl.pallas_call(
        body, name="mod_fwd", grid=(n // tn,),
        in_specs=[pl.BlockSpec((r, d), lambda j: (0, 0)), pl.BlockSpec((d, tn), lambda j: (0, j)),
                  pl.BlockSpec((1, tn), lambda j: (0, j))],
        out_specs=[pl.BlockSpec((r, tn), lambda j: (0, j)), pl.BlockSpec((r, d), lambda j: (0, 0))],
        out_shape=[_sds((r, n)), _sds((r, d))], compiler_params=_params("arbitrary"),
    )(c16, w_mod, b_mod_shard)


def _c_ctx_grad(parts, c_ctx_row):
    d = c_ctx_row.shape[1]

    def body(p_ref, c_ref, o_ref):
        tot = p_ref[0, 0:1, :]
        for chip in range(1, N_CHIPS):
            tot = tot + p_ref[2 * chip, 0:1, :]
        c = c_ref[...]
        sg = jax.nn.sigmoid(c)
        o_ref[...] = tot * (sg * (1.0 + c * (1.0 - sg)))

    return pl.pallas_call(body, name="c_ctx_grad", out_shape=_sds((1, d)), compiler_params=_params())(parts, c_ctx_row)


def _token_specs(n_ctx_rows, d, tr):
    nctx = n_ctx_rows // tr
    return (pl.BlockSpec((tr, d), lambda i: (jnp.minimum(i, nctx - 1), 0)),
            pl.BlockSpec((tr, d), lambda i: (jnp.maximum(i - nctx, 0), 0)))


def _norm_mod_fwd(ctx, x, g, mod4, tr):
    (n_ctx_rows, d), s = ctx.shape, x.shape[0]
    t = n_ctx_rows + s
    nctx = n_ctx_rows // tr

    def body(c_ref, x_ref, g_ref, mod_ref, h_ref):
        is_ctx = pl.program_id(0) < nctx
        x = jnp.where(is_ctx, c_ref[...], x_ref[...])
        n = x * _rms_stats(x) * g_ref[...]
        sh = jnp.where(is_ctx, mod_ref[0:1, :], mod_ref[2:3, :])
        sc = jnp.where(is_ctx, mod_ref[1:2, :], mod_ref[3:4, :])
        h_ref[...] = (n * (1.0 + sc) + sh).astype(BF16)

    return pl.pallas_call(
        body, name="norm_mod_fwd", grid=(t // tr,),
        in_specs=[*_token_specs(n_ctx_rows, d, tr), pl.BlockSpec((1, d), lambda i: (0, 0)),
                  pl.BlockSpec((4, d), lambda i: (0, 0))],
        out_specs=pl.BlockSpec((tr, d), lambda i: (i, 0)), out_shape=_sds((t, d), BF16),
        compiler_params=_params("parallel"),
    )(ctx, x, g, mod4)


def _norm_mod_bwd1(ctx, x, dh, dx1, g, mod4, tr, side=None):
    (n_ctx_rows, d), s = ctx.shape, x.shape[0]
    t = n_ctx_rows + s
    nctx = n_ctx_rows // tr

    def body(c_ref, x_ref, dh_ref, dx1_ref, g_ref, mod_ref, dx_ref, sums_ref):
        i = pl.program_id(0)
        is_ctx = i < nctx

        @pl.when(i == 0)
        def _():
            sums_ref[...] = jnp.zeros_like(sums_ref)

        x = jnp.where(is_ctx, c_ref[...], x_ref[...])
        dh_ = dh_ref[...]
        rstd = _rms_stats(x)
        xhat = x * rstd
        gg = g_ref[...]
        sc = jnp.where(is_ctx, mod_ref[1:2, :], mod_ref[3:4, :])
        dxhat = dh_ * (1.0 + sc) * gg
        dx_ref[...] = dx1_ref[...] + _rms_bwd(dxhat, xhat, rstd)
        part = [_colsum(dh_), _colsum(dh_ * xhat * gg), _colsum(dh_ * (1.0 + sc) * xhat)]

        @pl.when(is_ctx)
        def _():
            for k, row in enumerate(part):
                sums_ref[3 + k:4 + k, :] += row

        @pl.when(jnp.logical_not(is_ctx))
        def _():
            for k, row in enumerate(part):
                sums_ref[k:k + 1, :] += row

    lat = lambda i: (jnp.maximum(i - nctx, 0), 0)
    return _call(
        body, side=side, name="norm_mod_bwd1", grid=(t // tr,),
        in_specs=[*_token_specs(n_ctx_rows, d, tr), pl.BlockSpec((tr, d), lambda i: (i, 0)),
                  pl.BlockSpec((tr, d), lat), pl.BlockSpec((1, d), lambda i: (0, 0)),
                  pl.BlockSpec((4, d), lambda i: (0, 0))],
        out_specs=[pl.BlockSpec((tr, d), lat), pl.BlockSpec((8, d), lambda i: (0, 0))],
        out_shape=[_sds((s, d)), _sds((8, d))], sem=("arbitrary",),
    )(ctx, x, dh, dx1, g, mod4)


def _resid_norm_mod_fwd(x, mix, g, mod3, tr):
    s, d = x.shape

    def body(x_ref, mix_ref, g_ref, mod_ref, x1_ref, h_ref):
        x1 = x_ref[...] + mod_ref[0:1, :] * mix_ref[...]
        x1_ref[...] = x1
        n = x1 * _rms_stats(x1) * g_ref[...]
        h_ref[...] = (n * (1.0 + mod_ref[2:3, :]) + mod_ref[1:2, :]).astype(BF16)

    row = pl.BlockSpec((tr, d), lambda i: (i, 0))
    return pl.pallas_call(
        body, name="resid_norm_mod_fwd", grid=(s // tr,),
        in_specs=[row, row, pl.BlockSpec((1, d), lambda i: (0, 0)), pl.BlockSpec((3, d), lambda i: (0, 0))],
        out_specs=[row, row], out_shape=[_sds((s, d)), _sds((s, d), BF16)], compiler_params=_params("parallel"),
    )(x, mix, g, mod3)


def _norm_mod_bwd2(x1, dh2, dx2, mix, g, mod3, tr, side=None):
    s, d = x1.shape

    def body(x_ref, dh_ref, dx2_ref, mix_ref, g_ref, mod_ref, dx1_ref, dmix_ref, sums_ref):
        @pl.when(pl.program_id(0) == 0)
        def _():
            sums_ref[...] = jnp.zeros_like(sums_ref)

        x = x_ref[...]
        dh_ = dh_ref[...]
        rstd = _rms_stats(x)
        xhat = x * rstd
        gg = g_ref[...]
        sc = mod_ref[2:3, :]
        dx1 = dx2_ref[...] + _rms_bwd(dh_ * (1.0 + sc) * gg, xhat, rstd)
        dx1_ref[...] = dx1
        dmix_ref[...] = (dx1 * mod_ref[0:1, :]).astype(BF16)
        part = [_colsum(dh_), _colsum(dh_ * xhat * gg), _colsum(dh_ * (1.0 + sc) * xhat), _colsum(dx1 * mix_ref[...])]
        for k, row in enumerate(part):
            sums_ref[k:k + 1, :] += row

    row = pl.BlockSpec((tr, d), lambda i: (i, 0))
    return _call(
        body, side=side, name="norm_mod_bwd2", grid=(s // tr,),
        in_specs=[row, row, row, row, pl.BlockSpec((1, d), lambda i: (0, 0)), pl.BlockSpec((3, d), lambda i: (0, 0))],
        out_specs=[row, row, pl.BlockSpec((8, d), lambda i: (0, 0))],
        out_shape=[_sds((s, d)), _sds((s, d), BF16), _sds((8, d))], sem=("arbitrary",),
    )(x1, dh2, dx2, mix, g, mod3)


def _final_fwd_bwd(x1, down, target, g_final, gate, tr):
    s, d = x1.shape

    def body(x1_ref, down_ref, t_ref, g_ref, gate_ref, dx2_ref, ddown_ref, sums_ref, loss_ref):
        @pl.when(pl.program_id(0) == 0)
        def _():
            sums_ref[...] = jnp.zeros_like(sums_ref)
            loss_ref[...] = jnp.zeros_like(loss_ref)

        down_ = down_ref[...]
        gate_ = gate_ref[...]
        x2 = x1_ref[...] + gate_ * down_
        rstd = _rms_stats(x2)
        xhat = x2 * rstd
        gg = g_ref[...]
        err = xhat * gg - t_ref[...]
        loss_ref[...] += 0.5 * jnp.sum(jnp.mean(err * err, axis=-1, keepdims=True))
        dy = err * (1.0 / d)
        dx2 = _rms_bwd(dy * gg, xhat, rstd)
        dx2_ref[...] = dx2
        ddown_ref[...] = (dx2 * gate_).astype(BF16)
        sums_ref[0:1, :] += _colsum(dy * xhat)
        sums_ref[1:2, :] += _colsum(dx2 * down_)

    row = pl.BlockSpec((tr, d), lambda i: (i, 0))
    vec = pl.BlockSpec((1, d), lambda i: (0, 0))
    return pl.pallas_call(
        body, name="final_fwd_bwd", grid=(s // tr,), in_specs=[row, row, row, vec, vec],
        out_specs=[row, row, pl.BlockSpec((8, d), lambda i: (0, 0)), pl.BlockSpec((8, LANES), lambda i: (0, 0))],
        out_shape=[_sds((s, d)), _sds((s, d), BF16), _sds((8, d)), _sds((8, LANES))],
        compiler_params=_params("arbitrary"),
    )(x1, down, target, g_final, gate)


def _swap_pairs(v):
    lane = lax.broadcasted_iota(jnp.int32, v.shape, 1)
    return jnp.where(lane % 2 == 0, pltpu.roll(v, LANES - 1, 1), pltpu.roll(v, 1, 1))


def _head_prep_fwd(z, col_off, n_heads, gain, cos, sin, tr, name, side=None):
    t = z.shape[0]
    per = math.gcd(4, n_heads, col_off // LANES)
    w = per * LANES
    hb = col_off // w

    def body(z_ref, g_ref, cos_ref, sin_ref, o_ref):
        for hh in range(per):
            cols = slice(hh * LANES, (hh + 1) * LANES)
            x = z_ref[:, cols]
            y = x * _rms_stats(x) * g_ref[...]
            o_ref[:, cols] = (y * cos_ref[...] + _swap_pairs(y) * sin_ref[...]).astype(BF16)

    tab = pl.BlockSpec((tr, LANES), lambda i, j: (i, 0))
    return _call(
        body, side=side, name=name, grid=(t // tr, n_heads // per),
        in_specs=[pl.BlockSpec((tr, w), lambda i, j: (i, hb + j)), pl.BlockSpec((1, LANES), lambda i, j: (0, 0)),
                  tab, tab],
        out_specs=pl.BlockSpec((tr, w), lambda i, j: (i, j)), out_shape=_sds((t, n_heads * LANES), BF16),
        sem=("parallel", "parallel"),
    )(z, gain, cos, sin)


def _dz_start(t, n_in, n_ctx_rows):
    tcol = _tile(n_in, 1024, LANES)

    def body(o_ref):
        o_ref[...] = jnp.zeros_like(o_ref)

    return pl.pallas_call(body, name="dz_start", grid=(n_in // tcol,),
                          out_specs=pl.BlockSpec((n_ctx_rows, tcol), lambda j: (0, j)), out_shape=_sds((t, n_in), BF16),
                          compiler_params=_params("parallel"))()


def _head_prep_bwd(z, col_off, n_heads, gain, cos, sin, dout, row_off, tr, dz, name, side=None):
    r = dout.shape[0]
    per = math.gcd(4, n_heads, col_off // LANES)
    w = per * LANES
    hb = col_off // w
    rb = row_off // tr

    def body(z_ref, g_ref, cos_ref, sin_ref, d_ref, _, dz_ref, dg_ref):
        @pl.when(jnp.logical_and(pl.program_id(0) == 0, pl.program_id(1) == 0))
        def _():
            dg_ref[...] = jnp.zeros_like(dg_ref)

        for hh in range(per):
            cols = slice(hh * LANES, (hh + 1) * LANES)
            x = z_ref[:, cols]
            rstd = _rms_stats(x)
            xhat = x * rstd
            dd = d_ref[:, cols]
            dy = dd * cos_ref[...] - _swap_pairs(dd) * sin_ref[...]
            dg_ref[0:1, :] += _colsum(dy * xhat)
            dz_ref[:, cols] = _rms_bwd(dy * g_ref[...], xhat, rstd).astype(BF16)

    tab = pl.BlockSpec((tr, LANES), lambda i, j: (rb + i, 0))
    window = pl.BlockSpec((tr, w), lambda i, j: (rb + i, hb + j))
    return _call(
        body, side=side, name=name, grid=(r // tr, n_heads // per),
        in_specs=[window, pl.BlockSpec((1, LANES), lambda i, j: (0, 0)), tab, tab,
                  pl.BlockSpec((tr, w), lambda i, j: (i, j)), ANY],
        out_specs=[window, pl.BlockSpec((8, LANES), lambda i, j: (0, 0))],
        out_shape=[_sds(dz.shape, BF16), _sds((8, LANES))], input_output_aliases={5: 0}, sem=("arbitrary", "arbitrary"),
    )(z, gain, cos, sin, dout, dz)


def _attn_fwd(qr, kr, z, v_off, n_ctx_rows, group, tq, side=None):
    t, kvw = kr.shape
    s = t - n_ctx_rows
    n_kv = kvw // LANES
    scale = LANES ** -0.5
    qb0 = n_ctx_rows // tq
    vb = v_off // LANES

    def body(q_ref, k_ref, v_ref, o_ref, lse_ref):
        k = k_ref[...]
        v = v_ref[...].astype(BF16)
        lse_ref[...] = jnp.zeros_like(lse_ref)
        for g in range(group):
            cols = slice(g * LANES, (g + 1) * LANES)
            sc = lax.dot_general(q_ref[:, cols], k, (((1,), (1,)), ((), ())), preferred_element_type=F32) * scale
            m = jnp.max(sc, axis=-1, keepdims=True)
            e = jnp.exp(sc - m)
            l = jnp.sum(e, axis=-1, keepdims=True)
            p = e * (1.0 / l)
            o_ref[:, cols] = jnp.dot(p.astype(BF16), v, preferred_element_type=F32).astype(BF16)
            lse_ref[:, g:g + 1] = m + jnp.log(l)

    return _call(
        body, side=side, name="attn_fwd", grid=(n_kv, s // tq),
        in_specs=[pl.BlockSpec((tq, group * LANES), lambda h, i: (qb0 + i, h)),
                  pl.BlockSpec((t, LANES), lambda h, i: (0, h)), pl.BlockSpec((t, LANES), lambda h, i: (0, vb + h))],
        out_specs=[pl.BlockSpec((tq, group * LANES), lambda h, i: (i, h)), pl.BlockSpec((tq, LANES), lambda h, i: (i, h))],
        out_shape=[_sds((s, n_kv * group * LANES), BF16), _sds((s, kvw))], sem=("parallel", "parallel"),
    )(qr, kr, z)


def _attn_bwd(qr, kr, z, v_off, d_o, attn_o, lse, dz, n_ctx_rows, group, tq, side=None):
    t, kvw = kr.shape
    s = t - n_ctx_rows
    n_kv = kvw // LANES
    scale = LANES ** -0.5
    qb0 = n_ctx_rows // tq
    vb = v_off // LANES
    n_q_blocks = s // tq
    tn_dims = (((0,), (0,)), ((), ()))
    nt_dims = (((1,), (1,)), ((), ()))

    def body(q_ref, k_ref, v_ref, do_ref, o_ref, lse_ref, _, dq_ref, dk_ref, dz_ref, dv_ref):
        @pl.when(pl.program_id(1) == 0)
        def _():
            dk_ref[...] = jnp.zeros_like(dk_ref)
            dv_ref[...] = jnp.zeros_like(dv_ref)

        k = k_ref[...]
        v = v_ref[...].astype(BF16)
        for g in range(group):
            cols = slice(g * LANES, (g + 1) * LANES)
            q = q_ref[:, cols]
            do_ = do_ref[:, cols]
            row_dot = jnp.sum(do_.astype(F32) * o_ref[:, cols].astype(F32), axis=-1, keepdims=True)
            sc = lax.dot_general(q, k, nt_dims, preferred_element_type=F32)
            p = jnp.exp(sc * scale - lse_ref[:, g:g + 1])
            dv_ref[...] += lax.dot_general(p.astype(BF16), do_, tn_dims, preferred_element_type=F32)
            dp = lax.dot_general(do_, v, nt_dims, preferred_element_type=F32)
            ds = (p * (dp - row_dot)).astype(BF16)
            dq_ref[:, cols] = jnp.dot(ds, k, preferred_element_type=F32) * scale
            dk_ref[...] += lax.dot_general(ds, q, tn_dims, preferred_element_type=F32)

        @pl.when(pl.program_id(1) == n_q_blocks - 1)
        def _():
            dk_ref[...] = dk_ref[...] * scale
            dz_ref[...] = dv_ref[...].astype(BF16)

    qspec = pl.BlockSpec((tq, group * LANES), lambda h, i: (qb0 + i, h))
    ospec = pl.BlockSpec((tq, group * LANES), lambda h, i: (i, h))
    kspec = pl.BlockSpec((t, LANES), lambda h, i: (0, h))
    vspec = pl.BlockSpec((t, LANES), lambda h, i: (0, vb + h))
    return _call(
        body, side=side, name="attn_bwd", grid=(n_kv, n_q_blocks),
        in_specs=[qspec, kspec, vspec, ospec, ospec, pl.BlockSpec((tq, LANES), lambda h, i: (i, h)), ANY],
        out_specs=[ospec, kspec, vspec], out_shape=[_sds((s, n_kv * group * LANES)), _sds((t, kvw)), _sds(dz.shape, BF16)],
        scratch_shapes=[pltpu.VMEM((t, LANES), F32)], input_output_aliases={6: 2}, sem=("parallel", "arbitrary"),
    )(qr, kr, z, d_o, attn_o, lse, dz)


def _row_mask(shape, rows):
    r = lax.broadcasted_iota(jnp.int32, shape, 0)
    m = r == rows[0]
    for v in rows[1:]:
        m = jnp.logical_or(m, r == v)
    return m


def _shift_rows(x, k, n_ctx_rows):
    t = x.shape[0]
    if k == 0:
        return x
    rolled = pltpu.roll(x, (-k) % t, 0)
    if k > 0:
        dead = [n_ctx_rows - 1 - i for i in range(k)] + [t - 1 - i for i in range(k)]
    else:
        dead = [i for i in range(-k)] + [n_ctx_rows + i for i in range(-k)]
    return jnp.where(_row_mask(x.shape, dead), 0.0, rolled)


def _conv(x, w, b, n_ctx_rows):
    y = b
    for k in range(CONV_WIDTH):
        y = y + _shift_rows(x, k - 1, n_ctx_rows) * w[k:k + 1, :]
    return y


def _gates(xc_bf, w_r, b_r, w_i, b_i, lam):
    r = jax.nn.sigmoid(jnp.dot(xc_bf, w_r.astype(BF16), preferred_element_type=F32) + b_r)
    i = jax.nn.sigmoid(jnp.dot(xc_bf, w_i.astype(BF16), preferred_element_type=F32) + b_i)
    log_a = -LRU_C * r * _softplus(-lam)
    a = jnp.exp(log_a)
    mult = jnp.sqrt(-_expm1_nonpos(2.0 * log_a))
    return r, i, a, mult


def _rnn_specs(t, xr_off):
    xb = xr_off // LANES
    return dict(
        zcol=pl.BlockSpec((t, LANES), lambda j: (0, xb + j)), col=pl.BlockSpec((t, LANES), lambda j: (0, j)),
        conv_w=pl.BlockSpec((CONV_WIDTH, LANES), lambda j: (0, j)), vec=pl.BlockSpec((1, LANES), lambda j: (0, j)),
        gate_w=pl.BlockSpec((2, 1, LANES, LANES), lambda j: (0, j, 0, 0)), two=pl.BlockSpec((2, LANES), lambda j: (0, j)))


def _rnn_prep(z, xr_off, conv_w, conv_b, w_rg, b_rg, w_ig, b_ig, lam, n_ctx_rows, side=None):
    t = z.shape[0]
    d = conv_b.shape[1]
    sp = _rnn_specs(t, xr_off)

    def body(z_ref, cw_ref, cb_ref, wr_ref, br_ref, wi_ref, bi_ref, lam_ref, xc_ref, af_ref, bf_ref, ab_ref, bb_ref):
        xc = _conv(z_ref[...], cw_ref[...], cb_ref[...], n_ctx_rows)
        xc_ref[...] = xc
        xc_bf = xc.astype(BF16)
        for dr, (a_ref, b_ref) in enumerate(((af_ref, bf_ref), (ab_ref, bb_ref))):
            _, i, a, mult = _gates(xc_bf, wr_ref[dr, 0], br_ref[dr:dr + 1, :], wi_ref[dr, 0], bi_ref[dr:dr + 1, :],
                                   lam_ref[dr:dr + 1, :])
            a_ref[...] = a
            b_ref[...] = mult * (i * xc)

    return _call(
        body, side=side, name="rnn_prep", grid=(d // LANES,),
        in_specs=[sp["zcol"], sp["conv_w"], sp["vec"], sp["gate_w"], sp["two"], sp["gate_w"], sp["two"], sp["two"]],
        out_specs=[sp["col"]] * 5, out_shape=[_sds((t, d))] * 5, sem=("parallel",),
    )(z, conv_w, conv_b, w_rg, b_rg, w_ig, b_ig, lam)


def _scan(chains, *, post, n_ctx_rows, name, tc=256, side=None):
    t, d = chains[0][0].shape
    tc = _tile(math.gcd(n_ctx_rows, t - n_ctx_rows), tc, SUBLANES)
    nt, nctx = t // tc, n_ctx_rows // tc
    nlat = nt - nctx
    nc = len(chains)
    lat_only = [b.shape[0] != t for _, b, _ in chains]
    ups = [order.endswith("up") for _, _, order in chains]

    def chunk_of(order):
        def chunk(i):
            if order == "ctx_lat_up":
                return i
            if order == "lat_ctx_down":
                return nt - 1 - i
            if order == "ctx_lat_down":
                return jnp.where(i < nctx, nctx - 1 - i, nt - 1 - (i - nctx))
            return jnp.where(i < nlat, nctx + i, i - nlat)
        return chunk

    chunks = [chunk_of(order) for _, _, order in chains]

    def body(*refs):
        ab_refs, o_refs, carry_ref = refs[:2 * nc], refs[2 * nc:3 * nc], refs[3 * nc]

        @pl.when(pl.program_id(0) == 0)
        def _():
            carry_ref[...] = jnp.zeros_like(carry_ref)

        live = [jnp.where(chunks[n](pl.program_id(0)) >= nctx, 1.0, 0.0) if lat_only[n] else None for n in range(nc)]

        def group(gi, carries):
            carries = list(carries)
            bases = [pl.multiple_of((gi if ups[n] else tc // SUBLANES - 1 - gi) * SUBLANES, SUBLANES) for n in range(nc)]
            for step in range(SUBLANES):
                for n in range(nc):
                    row = bases[n] + (step if ups[n] else SUBLANES - 1 - step)
                    a_r = ab_refs[2 * n][pl.ds(row, 1), :]
                    b_r = ab_refs[2 * n + 1][pl.ds(row, 1), :]
                    if live[n] is not None:
                        b_r = b_r * live[n]
                    if post:
                        out = b_r + carries[n]
                        carries[n] = a_r * out
                    else:
                        out = a_r * carries[n] + b_r
                        carries[n] = out
                    o_refs[n][pl.ds(row, 1), :] = out
            return tuple(carries)

        done = lax.fori_loop(0, tc // SUBLANES, group, tuple(carry_ref[n:n + 1, :] for n in range(nc)))
        for n in range(nc):
            carry_ref[n:n + 1, :] = done[n]

    in_specs, out_specs, args = [], [], []
    for n, (a, b, _) in enumerate(chains):
        full = pl.BlockSpec((tc, d), lambda i, n=n: (chunks[n](i), 0))
        lat = pl.BlockSpec((tc, d), lambda i, n=n: (jnp.maximum(chunks[n](i) - nctx, 0), 0))
        in_specs += [full, lat if lat_only[n] else full]
        out_specs.append(full)
        args += [a, b]
    return _call(
        body, side=side, name=name, grid=(nt,), in_specs=in_specs, out_specs=out_specs, out_shape=[_sds((t, d))] * nc,
        scratch_shapes=[pltpu.VMEM((SUBLANES, d), F32)], sem=("arbitrary",),
    )(*args)


def _rnn_bwd(z, xr_off, xc, g_f, g_b, h_f, h_b, conv_w, w_rg, b_rg, w_ig, b_ig, lam, dz, n_ctx_rows, side=None):
    t, d = xc.shape
    sp = _rnn_specs(t, xr_off)
    tn_dims = (((0,), (0,)), ((), ()))
    nt_dims = (((1,), (1,)), ((), ()))

    def body(z_ref, xc_ref, gf_ref, gb_ref, hf_ref, hb_ref, cw_ref, wr_ref, br_ref, wi_ref, bi_ref, lam_ref, _,
             dxr_ref, dwr_ref, dwi_ref, sums_ref):
        xc_ = xc_ref[...]
        xc_bf = xc_.astype(BF16)
        dxc = jnp.zeros_like(xc_)
        sums = [None] * 6
        for dr, (g_ref, h_ref) in enumerate(((gf_ref, hf_ref), (gb_ref, hb_ref))):
            w_r, w_i, lam_ = wr_ref[dr, 0], wi_ref[dr, 0], lam_ref[dr:dr + 1, :]
            r, i, a, mult = _gates(xc_bf, w_r, br_ref[dr:dr + 1, :], w_i, bi_ref[dr:dr + 1, :], lam_)
            g = g_ref[...]
            h = h_ref[...]
            if dr == 0:
                h_prev = jnp.where(_row_mask(h.shape, [0]), 0.0, pltpu.roll(h, 1, 0))
            else:
                h_prev = jnp.where(_row_mask(h.shape, [n_ctx_rows - 1]), 0.0, pltpu.roll(h, t - 1, 0))
            d_mult = g * i * xc_
            d_i = g * mult * xc_
            dxc = dxc + g * mult * i
            d_log_a = g * h_prev * a - d_mult * a * a / mult
            sp_ = _softplus(-lam_)
            d_r = d_log_a * (-LRU_C) * sp_
            d_sp = _colsum(d_log_a * (-LRU_C) * r)
            du_r = (d_r * r * (1.0 - r))
            du_i = (d_i * i * (1.0 - i))
            sums[dr] = _colsum(du_r)
            sums[2 + dr] = _colsum(du_i)
            sums[4 + dr] = d_sp * (-jax.nn.sigmoid(-lam_))
            du_r_bf, du_i_bf = du_r.astype(BF16), du_i.astype(BF16)
            dwr_ref[dr, 0] = lax.dot_general(xc_bf, du_r_bf, tn_dims, preferred_element_type=F32).astype(BF16)
            dwi_ref[dr, 0] = lax.dot_general(xc_bf, du_i_bf, tn_dims, preferred_element_type=F32).astype(BF16)
            dxc = dxc + lax.dot_general(du_r_bf, w_r.astype(BF16), nt_dims, preferred_element_type=F32)
            dxc = dxc + lax.dot_general(du_i_bf, w_i.astype(BF16), nt_dims, preferred_element_type=F32)
        xr = z_ref[...]
        cw = cw_ref[...]
        dxr = jnp.zeros_like(dxc)
        rows = list(sums)
        for k in range(CONV_WIDTH):
            dxr = dxr + _shift_rows(dxc, 1 - k, n_ctx_rows) * cw[k:k + 1, :]
            rows.append(_colsum(dxc * _shift_rows(xr, k - 1, n_ctx_rows)))
        rows.append(_colsum(dxc))
        dxr_ref[...] = dxr.astype(BF16)
        sums_ref[...] = jnp.zeros_like(sums_ref)
        for k, row in enumerate(rows):
            sums_ref[k:k + 1, :] = row

    return _call(
        body, side=side, name="rnn_bwd", grid=(d // LANES,),
        in_specs=[sp["zcol"]] + [sp["col"]] * 5 + [sp["conv_w"], sp["gate_w"], sp["two"], sp["gate_w"], sp["two"],
                                                  sp["two"], ANY],
        out_specs=[sp["zcol"], sp["gate_w"], sp["gate_w"], pl.BlockSpec((16, LANES), lambda j: (0, j))],
        out_shape=[_sds(dz.shape, BF16), _sds(w_rg.shape, BF16), _sds(w_ig.shape, BF16), _sds((16, d))],
        input_output_aliases={12: 0}, sem=("parallel",),
    )(z, xc, g_f, g_b, h_f, h_b, conv_w, w_rg, b_rg, w_ig, b_ig, lam, dz)


def _tiles2d(s, d, tr, tcol):
    return (s // tr, d // tcol), pl.BlockSpec((tr, tcol), lambda i, j: (i, j))


def _zspec(tr, tcol, row_off, col_off):
    rb, cb = row_off // tr, col_off // tcol
    return pl.BlockSpec((tr, tcol), lambda i, j: (rb + i, cb + j))


def _rnn_gate_fwd(h_f, h_b, z, xg_off, n_ctx_rows, tr, tcol, side=None):
    t, d = h_f.shape
    s = t - n_ctx_rows
    grid, out = _tiles2d(s, d, tr, tcol)
    hs = _zspec(tr, tcol, n_ctx_rows, 0)

    def body(hf_ref, hb_ref, xg_ref, u_ref):
        u_ref[...] = ((hf_ref[...] + hb_ref[...]) * _gelu(xg_ref[...])).astype(BF16)

    return _call(body, side=side, name="rnn_gate_fwd", grid=grid,
                 in_specs=[hs, hs, _zspec(tr, tcol, n_ctx_rows, xg_off)], out_specs=out, out_shape=_sds((s, d), BF16),
                 sem=("parallel", "parallel"))(h_f, h_b, z)


def _rnn_gate_bwd(d_u, h_f, h_b, z, xg_off, dz, n_ctx_rows, tr, tcol, side=None):
    t, d = h_f.shape
    s = t - n_ctx_rows
    grid, out = _tiles2d(s, d, tr, tcol)
    hs = _zspec(tr, tcol, n_ctx_rows, 0)
    window = _zspec(tr, tcol, n_ctx_rows, xg_off)

    def body(du_ref, hf_ref, hb_ref, xg_ref, _, dr_ref, dxg_ref):
        du = du_ref[...]
        xg = xg_ref[...]
        dr_ref[...] = du * _gelu(xg)
        dxg_ref[...] = (du * (hf_ref[...] + hb_ref[...]) * _gelu_grad(xg)).astype(BF16)

    return _call(body, side=side, name="rnn_gate_bwd", grid=grid, in_specs=[out, hs, hs, window, ANY],
                 out_specs=[out, window], out_shape=[_sds((s, d)), _sds(dz.shape, BF16)], input_output_aliases={4: 1},
                 sem=("parallel", "parallel"))(d_u, h_f, h_b, z, dz)


def _merge_fwd(y_attn, y_rnn, z, gl_off, n_ctx_rows, tr, tcol):
    s, d = y_attn.shape
    grid, out = _tiles2d(s, d, tr, tcol)

    def body(ya_ref, yr_ref, ga_ref, gr_ref, o_ref):
        o_ref[...] = (jax.nn.sigmoid(ga_ref[...]) * ya_ref[...] + jax.nn.sigmoid(gr_ref[...]) * yr_ref[...]).astype(BF16)

    return pl.pallas_call(
        body, name="merge_fwd", grid=grid,
        in_specs=[out, out, _zspec(tr, tcol, n_ctx_rows, gl_off), _zspec(tr, tcol, n_ctx_rows, gl_off + d)],
        out_specs=out, out_shape=_sds((s, d), BF16), compiler_params=_params("parallel", "parallel"),
    )(y_attn, y_rnn, z, z)


def _merge_bwd(d_mrg, y, z, gl_off, dz, n_ctx_rows, tr, tcol, name):
    s, d = y.shape
    grid, out = _tiles2d(s, d, tr, tcol)
    window = _zspec(tr, tcol, n_ctx_rows, gl_off)

    def body(dm_ref, y_ref, gl_ref, _, dy_ref, dgl_ref):
        dm = dm_ref[...]
        g = jax.nn.sigmoid(gl_ref[...])
        dy_ref[...] = (dm * g).astype(BF16)
        dgl_ref[...] = (dm * y_ref[...] * g * (1.0 - g)).astype(BF16)

    return _call(body, name=name, grid=grid, in_specs=[out, out, window, ANY], out_specs=[out, window],
                 out_shape=[_sds((s, d), BF16), _sds(dz.shape, BF16)], input_output_aliases={3: 1},
                 sem=("parallel", "parallel"))(d_mrg, y, z, dz)


def _sq_relu(up, tr, tcol, side=None):
    grid, out = _tiles2d(*up.shape, _tile(up.shape[0], 2 * tr, 16), _tile(up.shape[1], 4 * tcol, LANES))

    def body(u_ref, o_ref):
        r = jnp.maximum(u_ref[...], 0.0)
        o_ref[...] = (r * r).astype(BF16)

    return _call(body, side=side, name="sq_relu", grid=grid, in_specs=[out], out_specs=out,
                 out_shape=_sds(up.shape, BF16), sem=("parallel", "parallel"))(up)


def _cast_into_window(w, chip, col_sharded, name):
    r, c = w.shape
    tr, tcol = _tile(r, 512, 16), _tile(c, WIDE_TILE, LANES)
    nrb, ncb = r // tr, c // tcol

    def body(chip_ref, w_ref, o_ref):
        o_ref[...] = w_ref[...].astype(BF16)

    if col_sharded:
        omap = lambda i, j, chip_ref: (i, chip_ref[0] * ncb + j)
    else:
        omap = lambda i, j, chip_ref: (chip_ref[0] * nrb + i, j)
    return pl.pallas_call(
        body, name=name,
        grid_spec=pltpu.PrefetchScalarGridSpec(
            num_scalar_prefetch=1, grid=(nrb, ncb),
            in_specs=[pl.BlockSpec((tr, tcol), lambda i, j, chip_ref: (i, j))], out_specs=pl.BlockSpec((tr, tcol), omap)),
        out_shape=_sds((r, c * N_CHIPS) if col_sharded else (r * N_CHIPS, c), BF16),
        compiler_params=_params("parallel", "parallel"),
    )(chip, w)


def _sum_leading(parts, name):
    n, r, c = parts.shape
    tr, tcol = _tile(r, 512, SUBLANES), _tile(c, 1024, LANES)

    def body(p_ref, o_ref):
        tot = p_ref[0]
        for k in range(1, n):
            tot = tot + p_ref[k]
        o_ref[...] = tot

    return pl.pallas_call(
        body, name=name, grid=(r // tr, c // tcol), in_specs=[pl.BlockSpec((n, tr, tcol), lambda i, j: (0, i, j))],
        out_specs=pl.BlockSpec((tr, tcol), lambda i, j: (i, j)), out_shape=_sds((r, c)),
        compiler_params=_params("parallel", "parallel"),
    )(parts)


def _add_half(full, other, core, split_rows, name):
    r, c = other.shape
    tr, tcol = _tile(r, 512, 16), _tile(c, 1024, LANES)
    nrb, ncb = r // tr, c // tcol

    def body(core_ref, f_ref, o_ref, out_ref):
        out_ref[...] = (f_ref[...].astype(F32) + o_ref[...].astype(F32)).astype(out_ref.dtype)

    if split_rows:
        fmap = lambda i, j, core_ref: (core_ref[0] * nrb + i, j)
    else:
        fmap = lambda i, j, core_ref: (i, core_ref[0] * ncb + j)
    same = lambda i, j, core_ref: (i, j)
    return pl.pallas_call(
        body, name=name,
        grid_spec=pltpu.PrefetchScalarGridSpec(
            num_scalar_prefetch=1, grid=(nrb, ncb),
            in_specs=[pl.BlockSpec((tr, tcol), fmap), pl.BlockSpec((tr, tcol), same)],
            out_specs=pl.BlockSpec((tr, tcol), same)),
        out_shape=_sds((r, c), BF16), compiler_params=_params("parallel", "parallel"),
    )(core, full, other)


def _add_pair(a, b, name):
    r, c = a.shape
    tr, tcol = _tile(r, 512, 16), _tile(c, WIDE_TILE, LANES)
    blk = pl.BlockSpec((tr, tcol), lambda i, j: (i, j))

    def body(a_ref, b_ref, o_ref):
        o_ref[...] = (a_ref[...].astype(F32) + b_ref[...].astype(F32)).astype(BF16)

    return pl.pallas_call(body, name=name, grid=(r // tr, c // tcol), in_specs=[blk, blk], out_specs=blk,
                          out_shape=_sds((r, c), BF16), compiler_params=_params("parallel", "parallel"))(a, b)


def _sum_regions(pair, got, place, col_sharded, name):
    n_got, r, c = got.shape
    tr, tcol = _tile(r, 512, 16), _tile(c, WIDE_TILE, LANES)
    nrb, ncb = r // tr, c // tcol

    def body(place_ref, p_ref, g_ref, out_ref):
        tot = p_ref[...].astype(F32)
        for k in range(n_got):
            tot = tot + g_ref[k].astype(F32)
        out_ref[...] = tot

    if col_sharded:
        pmap = lambda i, j, pr: (i, pr[0] * ncb + j)
        omap = lambda i, j, pr: (pr[1] * nrb + i, j)
        out_shape = (2 * r, c)
    else:
        pmap = lambda i, j, pr: (pr[0] * nrb + i, j)
        omap = lambda i, j, pr: (i, pr[1] * ncb + j)
        out_shape = (r, 2 * c)
    return pl.pallas_call(
        body, name=name,
        grid_spec=pltpu.PrefetchScalarGridSpec(
            num_scalar_prefetch=1, grid=(nrb, ncb),
            in_specs=[pl.BlockSpec((tr, tcol), pmap), pl.BlockSpec((n_got, tr, tcol), lambda i, j, pr: (0, i, j))],
            out_specs=pl.BlockSpec((tr, tcol), omap)),
        out_shape=_sds(out_shape), compiler_params=_params("parallel", "parallel"),
    )(place, pair, got)


def _fold_forwarded(pair, fold, nbr_chips, name):
    _, r, c = fold.shape
    tr, tcol = _tile(r, 512, 16), _tile(c, WIDE_TILE, LANES)
    nrb, ncb = r // tr, c // tcol

    def body(nbr_ref, px_ref, py_ref, f_ref, out_ref):
        out_ref[0] = (px_ref[...].astype(F32) + f_ref[1].astype(F32)).astype(out_ref.dtype)
        out_ref[1] = (py_ref[...].astype(F32) + f_ref[0].astype(F32)).astype(out_ref.dtype)

    both = pl.BlockSpec((2, tr, tcol), lambda i, j, nb: (0, i, j))
    return pl.pallas_call(
        body, name=name,
        grid_spec=pltpu.PrefetchScalarGridSpec(
            num_scalar_prefetch=1, grid=(nrb, ncb),
            in_specs=[pl.BlockSpec((tr, tcol), lambda i, j, nb: (i, nb[0] * ncb + j)),
                      pl.BlockSpec((tr, tcol), lambda i, j, nb: (nrb + i, nb[1] * ncb + j)), both],
            out_specs=both),
        out_shape=_sds(fold.shape, fold.dtype), compiler_params=_params("parallel", "parallel"),
    )(nbr_chips, pair, pair, fold)


def _adamw(w, g, m, v, name, copy_grad=True):
    r, c = w.shape
    tr, tcol = _tile(r, 256, SUBLANES), _tile(c, WIDE_TILE, LANES)
    blk = pl.BlockSpec((tr, tcol), lambda i, j: (i, j))
    n_out = 4 if copy_grad else 3

    def body(w_ref, g_ref, m_ref, v_ref, d_ref, nm_ref, nv_ref, *g_out_ref):
        g_ = g_ref[...]
        if copy_grad:
            g_out_ref[0][...] = g_
        m_ = ADAM_B1 * m_ref[...] + (1.0 - ADAM_B1) * g_
        v_ = ADAM_B2 * v_ref[...] + (1.0 - ADAM_B2) * (g_ * g_)
        m_hat = m_ / (1.0 - ADAM_B1 ** ADAM_STEP)
        v_hat = v_ / (1.0 - ADAM_B2 ** ADAM_STEP)
        d_ref[...] = -ADAM_LR * (m_hat / (jnp.sqrt(v_hat) + ADAM_EPS) + ADAM_WD * w_ref[...])
        nm_ref[...] = m_
        nv_ref[...] = v_

    return _call(body, name=name, grid=(r // tr, c // tcol), in_specs=[blk] * 4, out_specs=[blk] * n_out,
                 out_shape=[_sds((r, c))] * n_out, sem=("parallel", "parallel"))(w, g, m, v)


def _place():
    x, y, c = lax.axis_index("x"), lax.axis_index("y"), lax.axis_index("c")
    chips = [(1 - x, y), (x, 1 - y), (1 - x, 1 - y)]
    return x, y, c, chips


def _all_gather8(blk, name):
    m, n = blk.shape

    def body(x_ref, out_ref, send_sems, recv_sems, local_sem):
        x, y, c, chips = _place()
        me, sibling = (x, y, c), (x, y, 1 - c)

        def rows(px, py, pc):
            return out_ref.at[pl.ds((4 * px + 2 * py + pc) * m, m), :]

        def copy(k, block, to, src=None):
            return pltpu.make_async_remote_copy(
                src_ref=rows(*block) if src is None else src, dst_ref=rows(*block), send_sem=send_sems.at[k],
                recv_sem=recv_sems.at[k], device_id=to, device_id_type=MESH_ID)

        mine = pltpu.make_async_copy(x_ref, rows(*me), local_sem)
        mine.start()
        first = [copy(0, me, sibling, src=x_ref)]
        first += [copy(1 + j, me, (*chip, c), src=x_ref) for j, chip in enumerate(chips)]
        for cp in first:
            cp.start()
        passed = [copy(4 + j, (*chip, c), sibling) for j, chip in enumerate(chips)]
        for j, chip in enumerate(chips):
            copy(1 + j, (*chip, c), me).wait_recv()
            passed[j].start()
        copy(0, sibling, me).wait_recv()
        for j, chip in enumerate(chips):
            copy(4 + j, (*chip, 1 - c), me).wait_recv()
        for cp in first + passed:
            cp.wait_send()
        mine.wait()

    return pl.pallas_call(
        body, name=name, out_shape=_sds((N_DEV * m, n), blk.dtype), in_specs=[ANY], out_specs=ANY,
        scratch_shapes=[pltpu.SemaphoreType.DMA((7,)), pltpu.SemaphoreType.DMA((7,)), pltpu.SemaphoreType.DMA],
    )(blk)


def _half(ref, core, split_rows):
    r, c = ref.shape
    if split_rows:
        return ref.at[pl.ds(core * (r // 2), r // 2), :]
    return ref.at[:, pl.ds(core * (c // 2), c // 2)]


def _chip_block(ref, j, col_sharded):
    r, c = ref.shape
    if col_sharded:
        return ref.at[:, pl.ds(j * (c // N_CHIPS), c // N_CHIPS)]
    return ref.at[pl.ds(j * (r // N_CHIPS), r // N_CHIPS), :]


def _rows_part(ref, part):
    lo, hi, n = part
    r = ref.shape[0]
    return ref if (lo, hi) == (0, n) else ref.at[pl.ds(lo * (r // n), (hi - lo) * (r // n)), :]


def _copy(send_sems, recv_sems, k, src, dst, to):
    return pltpu.make_async_remote_copy(src_ref=src, dst_ref=dst, send_sem=send_sems.at[k], recv_sem=recv_sems.at[k],
                                        device_id=to, device_id_type=MESH_ID)


def _in_place(arrays):
    return tuple(arrays), tuple(_sds(a.shape, a.dtype) for a in arrays), {i: i for i in range(len(arrays))}


def _gather_d2d(fulls, col_sharded):
    nw = len(fulls)

    def build(_, refs, send_sems, recv_sems, sem0):
        x, y, c, chips = _place()
        sends, recvs = [], []
        for w in range(nw):
            win = lambda j, core: _half(_chip_block(refs[w], j, col_sharded[w]), core, True)
            for k, (cx, cy) in enumerate(chips):
                sem = sem0 + 3 * w + k
                sends.append(_copy(send_sems, recv_sems, sem, win(2 * cx + cy, c), win(2 * cx + cy, c), (x, y, 1 - c)))
                recvs.append(_copy(send_sems, recv_sems, sem, win(2 * cx + cy, 1 - c), win(2 * cx + cy, 1 - c),
                                   (x, y, 1 - c)))
        return sends, recvs

    return _Side(*_in_place(fulls), 3 * nw, build)


def _gather_neighbours(fulls, col_sharded, part=(0, 1, 1)):
    nw = len(fulls)

    def build(_, refs, send_sems, recv_sems, sem0):
        x, y, c, chips = _place()
        sends, recvs = [], []
        for w in range(nw):
            win = lambda j: _rows_part(_half(_chip_block(refs[w], j, col_sharded[w]), c, True), part)
            for k, (cx, cy) in enumerate(chips[:2]):
                sem = sem0 + 2 * w + k
                sends.append(_copy(send_sems, recv_sems, sem, win(2 * x + y), win(2 * x + y), (cx, cy, c)))
                recvs.append(_copy(send_sems, recv_sems, sem, win(2 * cx + cy), win(2 * cx + cy), (cx, cy, c)))
        return sends, recvs

    return _Side(*_in_place(fulls), 2 * nw, build)


def _gather_forward(fulls, col_sharded, ways=(True, True), siblings=False):
    nw = len(fulls)

    def build(_, refs, send_sems, recv_sems, sem0):
        x, y, c, (cx_, cy_, cd_) = _place()
        sends, recvs = [], []
        for w in range(nw):
            win = lambda chip, core: _half(_chip_block(refs[w], 2 * chip[0] + chip[1], col_sharded[w]), core, True)
            part = lambda ref, p: _rows_part(ref, (p, p + 1, 2))
            for p, (src, to) in enumerate(((cx_, cy_), (cy_, cx_))):
                if ways[p]:
                    sem = sem0 + 4 * w + p
                    sends.append(_copy(send_sems, recv_sems, sem, part(win(src, c), p), part(win(src, c), p), (*to, c)))
                    recvs.append(_copy(send_sems, recv_sems, sem, part(win(cd_, c), p), part(win(cd_, c), p), (*to, c)))
            if siblings:
                for k, chip in enumerate((cx_, cy_)):
                    sem = sem0 + 4 * w + 2 + k
                    sends.append(_copy(send_sems, recv_sems, sem, win(chip, c), win(chip, c), (x, y, 1 - c)))
                    recvs.append(_copy(send_sems, recv_sems, sem, win(chip, 1 - c), win(chip, 1 - c), (x, y, 1 - c)))
        return sends, recvs

    return _Side(*_in_place(fulls), 4 * nw, build)


def _gather_diagonal_d2d(full, col_sharded):
    def build(_, refs, send_sems, recv_sems, sem0):
        x, y, c, chips = _place()
        cx, cy = chips[2]
        win = lambda core: _half(_chip_block(refs[0], 2 * cx + cy, col_sharded), core, True)
        return ([_copy(send_sems, recv_sems, sem0, win(c), win(c), (x, y, 1 - c))],
                [_copy(send_sems, recv_sems, sem0, win(1 - c), win(1 - c), (x, y, 1 - c))])

    return _Side(*_in_place([full]), 1, build)


def _scatter_forward_1(pair):
    r, n = pair.shape
    results = (_sds((2, r, n // N_CHIPS), pair.dtype), _sds((2, r // 2, n // N_CHIPS), pair.dtype))

    def build(refs, res, send_sems, recv_sems, sem0):
        x, y, c, (cx_, cy_, cd_) = _place()
        region = lambda chip, p: _rows_part(_chip_block(refs[0], 2 * chip[0] + chip[1], True), (p, p + 1, 2))
        got, fold = res
        copies = [
            _copy(send_sems, recv_sems, sem0, region(cx_, 1), _rows_part(got.at[0], (1, 2, 2)), (*cx_, c)),
            _copy(send_sems, recv_sems, sem0 + 1, region(cd_, 1), fold.at[0], (*cx_, c)),
            _copy(send_sems, recv_sems, sem0 + 2, region(cy_, 0), _rows_part(got.at[1], (0, 1, 2)), (*cy_, c)),
            _copy(send_sems, recv_sems, sem0 + 3, region(cd_, 0), fold.at[1], (*cy_, c))]
        return copies, copies

    return _Side((pair,), results, {}, 4, build)


def _scatter_forward_2(passed, got):
    def build(refs, res, send_sems, recv_sems, sem0):
        x, y, c, (cx_, cy_, _) = _place()
        copies = [_copy(send_sems, recv_sems, sem0, refs[0].at[0], _rows_part(res[0].at[0], (0, 1, 2)), (*cx_, c)),
                  _copy(send_sems, recv_sems, sem0 + 1, refs[0].at[1], _rows_part(res[0].at[1], (1, 2, 2)), (*cy_, c))]
        return copies, copies

    return _Side((passed, got), (_sds(got.shape, got.dtype),), {1: 0}, 2, build)


def _exchange(side, name):
    return _call(None, side=side, name=name)()[1]


def _swap_halves(grads, col_sharded):
    nw = len(grads)
    out_shapes = [_sds((g.shape[0] // 2, g.shape[1]) if col else (g.shape[0], g.shape[1] // 2), g.dtype)
                  for g, col in zip(grads, col_sharded)]

    def build(g_refs, o_refs, send_sems, recv_sems, sem0):
        x, y, c, _ = _place()
        copies = [_copy(send_sems, recv_sems, sem0 + w, _half(g_refs[w], 1 - c, col_sharded[w]), o_refs[w],
                        (x, y, 1 - c)) for w in range(nw)]
        return copies, copies

    return _Side(tuple(grads), tuple(out_shapes), {}, nw, build)


def _send_to_sibling(arrays):
    nw = len(arrays)

    def build(refs, o_refs, send_sems, recv_sems, sem0):
        x, y, c, _ = _place()
        copies = [_copy(send_sems, recv_sems, sem0 + w, refs[w], o_refs[w], (x, y, 1 - c)) for w in range(nw)]
        return copies, copies

    return _Side(tuple(arrays), tuple(_sds(a.shape, a.dtype) for a in arrays), {}, nw, build)


def _scatter_regions(pairs, col_sharded, part=(0, 1, 1), into=None):
    nw = len(pairs)

    def region_shape(p, col):
        return (p.shape[0], p.shape[1] // N_CHIPS) if col else (p.shape[0] // N_CHIPS, p.shape[1])

    out_shapes = tuple(_sds((N_CHIPS - 1, *region_shape(p, col)), p.dtype) for p, col in zip(pairs, col_sharded))

    def build(refs, o_refs, send_sems, recv_sems, sem0):
        x, y, c, chips = _place()
        copies = []
        for w in range(nw):
            for k, (cx, cy) in enumerate(chips):
                copies.append(_copy(
                    send_sems, recv_sems, sem0 + 3 * w + k,
                    _rows_part(_chip_block(refs[w], 2 * cx + cy, col_sharded[w]), part),
                    _rows_part(o_refs[w].at[k], part), (cx, cy, c)))
        return copies, copies

    if into is None:
        return _Side(tuple(pairs), out_shapes, {}, 3 * nw, build)
    return _Side((*pairs, *into), out_shapes, {nw + w: w for w in range(nw)}, 3 * nw, build)


def _join_halves(halves, col_sharded):
    nw = len(halves)

    def build(_, refs, send_sems, recv_sems, sem0):
        x, y, c, _ = _place()
        sends, recvs = [], []
        for w in range(nw):
            mine, theirs = _half(refs[w], c, col_sharded[w]), _half(refs[w], 1 - c, col_sharded[w])
            sends.append(_copy(send_sems, recv_sems, sem0 + w, mine, mine, (x, y, 1 - c)))
            recvs.append(_copy(send_sems, recv_sems, sem0 + w, theirs, theirs, (x, y, 1 - c)))
        return sends, recvs

    return _Side(*_in_place(halves), nw, build)


def _part_rows(size):
    return -(-size // (SUBLANES * LANES)) * SUBLANES


def _pack(arrays, pad_rows_to=SUBLANES):
    flat = [jnp.pad(a.reshape(-1), (0, _part_rows(a.size) * LANES - a.size)).reshape(-1, LANES) for a in arrays]
    rows = sum(f.shape[0] for f in flat)
    pad = (-rows) % pad_rows_to
    if pad:
        flat.append(jnp.zeros((pad, LANES), F32))
    return jnp.concatenate(flat, axis=0)


def _unpack(packed, shapes):
    out, r = [], 0
    for shp in shapes:
        size = math.prod(shp)
        out.append(packed[r:r + _part_rows(size)].reshape(-1)[:size].reshape(shp))
        r += _part_rows(size)
    return out


def _rope_tables(n_ctx_rows, s):
    rows = s // GRID_W
    row_idx = jnp.repeat(jnp.arange(rows), GRID_W)
    col_idx = jnp.tile(jnp.arange(GRID_W), rows)
    n_freq = LANES // 4
    inv_freq = ROPE_THETA ** (-jnp.arange(n_freq, dtype=F32) / n_freq)
    ang = jnp.concatenate([row_idx.astype(F32)[:, None] * inv_freq, col_idx.astype(F32)[:, None] * inv_freq], axis=-1)
    cos = jnp.repeat(jnp.cos(ang), 2, axis=-1)
    sin = jnp.repeat(jnp.sin(ang), 2, axis=-1) * jnp.tile(jnp.array([-1.0, 1.0], F32), LANES // 2)
    cos = jnp.concatenate([jnp.ones((n_ctx_rows, LANES), F32), cos], axis=0)
    sin = jnp.concatenate([jnp.zeros((n_ctx_rows, LANES), F32), sin], axis=0)
    return cos, sin


WEIGHT_NAMES = ['c_ctx', 'w_mod', 'b_mod', 'g_mix', 'g_mlp', 'w_in', 'q_gain', 'k_gain', 'conv_w', 'conv_b', 'w_rg',
                'b_rg', 'w_ig', 'b_ig', 'lru_lambda', 'w_o_attn', 'w_o_rnn', 'w_out', 'w_up', 'w_down', 'g_final']
BIG = ['w_in', 'w_o_attn', 'w_o_rnn', 'w_out', 'w_up', 'w_down']
BIG_COL_SHARDED = [True, False, False, False, True, False]
GATES = ['w_rg', 'w_ig']
SMALL = ['c_ctx', 'b_mod', 'g_mix', 'g_mlp', 'q_gain', 'k_gain', 'conv_b', 'g_final',
         'conv_w', 'b_rg', 'b_ig', 'lru_lambda']


def kernel(x, c, ctx, c_ctx, w_mod, b_mod, g_mix, g_mlp, w_in, q_gain, k_gain, conv_w, conv_b, w_rg, b_rg, w_ig, b_ig, lru_lambda, w_o_attn, w_o_rnn, w_out, w_up, w_down, g_final, loss_target, m_c_ctx, m_w_mod, m_b_mod, m_g_mix, m_g_mlp, m_w_in, m_q_gain, m_k_gain, m_conv_w, m_conv_b, m_w_rg, m_b_rg, m_w_ig, m_b_ig, m_lru_lambda, m_w_o_attn, m_w_o_rnn, m_w_out, m_w_up, m_w_down, m_g_final, v_c_ctx, v_w_mod, v_b_mod, v_g_mix, v_g_mlp, v_w_in, v_q_gain, v_k_gain, v_conv_w, v_conv_b, v_w_rg, v_b_rg, v_w_ig, v_b_ig, v_lru_lambda, v_w_o_attn, v_w_o_rnn, v_w_out, v_w_up, v_w_down, v_g_final):
    given = dict(locals())
    weights = {n: given[n] for n in WEIGHT_NAMES}
    moms = {n: given["m_" + n] for n in WEIGHT_NAMES}
    vars_ = {n: given["v_" + n] for n in WEIGHT_NAMES}

    s, d = x.shape[1], x.shape[2]
    n_ctx = ctx.shape[1]
    t = n_ctx + s
    hd = q_gain.shape[1]
    assert hd == LANES and w_rg.shape[-1] == LANES
    attn_w = w_o_attn.shape[1] * N_CHIPS
    n_in = w_in.shape[2] * N_CHIPS
    kv_w = (n_in - attn_w - 4 * d) // 2
    group = attn_w // kv_w
    k_off, v_off, xr_off = attn_w, attn_w + kv_w, attn_w + 2 * kv_w
    xg_off, gl_off = xr_off + d, xr_off + 2 * d
    d_mod = N_MOD * d
    tr = _tile(math.gcd(n_ctx, s), 256, 16)
    tcol = _tile(math.gcd(d, xr_off), 1024, LANES)
    xi, yi, ci = lax.axis_index("x"), lax.axis_index("y"), lax.axis_index("c")
    chip = 2 * xi + yi
    core = ci.astype(jnp.int32).reshape(1)

    sharded_small = [conv_w[0], b_rg[0], b_ig[0], lru_lambda[0]]
    pack0 = _pack([c[0]] + sharded_small)
    got0 = _all_gather8(pack0, "gather_small_inputs").reshape(N_DEV, -1, LANES)
    c_all = got0[:, :_part_rows(d)].reshape(N_DEV, -1)[:, :d]
    per_chip = [_unpack(got0[2 * j, _part_rows(d):], [a.shape for a in sharded_small]) for j in range(N_CHIPS)]
    conv_w_f, b_rg_f, b_ig_f, lam_f = (jnp.concatenate([per_chip[j][i] for j in range(N_CHIPS)], axis=-1)
                                       for i in range(4))
    c16 = jnp.concatenate([c_all, c_ctx[None, :], jnp.zeros((16 - N_DEV - 1, d), F32)], axis=0)
    b_mod_shard = lax.dynamic_slice(b_mod, (0, chip * (d_mod // N_CHIPS)), (1, d_mod // N_CHIPS))
    mod_part, silu16 = _mod_fwd(c16, w_mod[0], b_mod_shard)
    mod_all = _all_gather8(mod_part, "gather_mod").reshape(N_DEV, 16, d_mod // N_CHIPS)
    mod16 = jnp.concatenate([mod_all[2 * j] for j in range(N_CHIPS)], axis=-1)
    me = 4 * xi + 2 * yi + ci
    mod_lat = lax.dynamic_slice(mod16, (me, 0), (1, d_mod)).reshape(N_MOD, d)
    mod_ctx = mod16[N_DEV].reshape(N_MOD, d)
    mod4 = jnp.stack([mod_ctx[0], mod_ctx[1], mod_lat[0], mod_lat[1]])
    mod3 = jnp.stack([mod_lat[2], mod_lat[3], mod_lat[4]])
    gate_f = mod_lat[5][None, :]

    chip_arr = chip.astype(jnp.int32).reshape(1)
    own = {n: _cast_into_window(weights[n][0], chip_arr, col, "cast_" + n) for n, col in zip(BIG, BIG_COL_SHARDED)}
    place = jnp.stack([chip, ci]).astype(jnp.int32)
    row3 = [False] * 3

    def pair_sum(n, full, other, col):
        return _add_half(full, other, core, col, "pair_sum_" + n)

    def chip_sum(n, pair, got, col):
        return _sum_regions(pair, got, place, col, "chip_sum_" + n)

    (w_in_f,) = _exchange(_gather_neighbours([own['w_in']], [True]), "gather_w_in_nbr")
    (w_in_f,) = _exchange(_gather_forward([w_in_f], [True], siblings=True), "gather_w_in_fwd")
    (w_in_f,) = _exchange(_gather_diagonal_d2d(w_in_f, True), "gather_w_in_diag")
    cos, sin = _rope_tables(n_ctx, s)
    h = _norm_mod_fwd(ctx[0], x[0], g_mix, mod4, tr)
    z, (w_oa_f, w_or_f, w_out_f, w_up_f) = _matmul(h, w_in_f, name="mm_in", side=_sides(
        _gather_neighbours([own['w_o_attn'], own['w_o_rnn'], own['w_out']], row3),
        _gather_neighbours([own['w_up']], [True], (0, 1, 2))))
    qr, (w_oa_f, w_or_f, w_out_f) = _head_prep_fwd(z, 0, attn_w // LANES, q_gain, cos, sin, tr, "q_prep",
                                                   side=_gather_forward([w_oa_f, w_or_f, w_out_f], row3))
    kr = _head_prep_fwd(z, k_off, kv_w // LANES, k_gain, cos, sin, tr, "k_prep")
    (attn_o, lse), (w_up_f, w_down_f, w_oa_f, w_or_f, w_out_f) = _attn_fwd(
        qr, kr, z, v_off, n_ctx, group, tr, side=_sides(
            _on_same(_gather_neighbours([w_up_f], [True], (1, 2, 2)), _gather_forward([w_up_f], [True], ways=(True, False))),
            _gather_neighbours([own['w_down']], [False]), _gather_d2d([w_oa_f, w_or_f, w_out_f], row3)))
    (xc, a_f, bx_f, a_b, bx_b), (w_up_f,) = _rnn_prep(
        z, xr_off, conv_w_f, conv_b, w_rg[0], b_rg_f, w_ig[0], b_ig_f, lam_f, n_ctx,
        side=_gather_forward([w_up_f], [True], ways=(False, True)))
    (h_f, h_b), (w_up_f, w_down_f) = _scan(
        [(a_f, bx_f, "ctx_lat_up"), (a_b, bx_b, "ctx_lat_down")], post=False, n_ctx_rows=n_ctx, name="scan_fwd",
        side=_sides(_gather_d2d([w_up_f], [True]), _gather_forward([w_down_f], [False])))
    u = _rnn_gate_fwd(h_f, h_b, z, xg_off, n_ctx, tr, tcol)
    y_attn = _matmul(attn_o, w_oa_f, name="mm_o_attn")
    y_rnn = _matmul(u, w_or_f, name="mm_o_rnn")
    mrg = _merge_fwd(y_attn, y_rnn, z, gl_off, n_ctx, tr, tcol)
    mix = _matmul(mrg, w_out_f, name="mm_out")
    x1, h2 = _resid_norm_mod_fwd(x[0], mix, g_mlp, mod3, tr)
    up, (w_down_f,) = _matmul(h2, w_up_f, name="mm_up", side=_gather_d2d([w_down_f], [False]))
    act = _sq_relu(up, tr, tcol)
    down = _matmul(act, w_down_f, name="mm_down")
    dx2, d_down, sums_fin, loss_blk = _final_fwd_bwd(x1, down, loss_target[0], g_final[None, :], gate_f, tr)

    d_up = _matmul(d_down, w_down_f, tb=True, out_dtype=BF16, name="mm_d_up",
                   post=lambda d_act, up_: d_act * 2.0 * jnp.maximum(up_, 0.0), post_args=(up,))
    g_w_down = _matmul(act, d_down, ta=True, out_dtype=BF16, name="mm_g_down")

    def scatter(pairs, cols, lo, hi, into=None):
        return _scatter_regions(pairs, cols, (lo, hi, 8), into)

    dh2, (got,) = _matmul(d_up, w_up_f, tb=True, name="mm_d_h2", side=_swap_halves([g_w_down], [False]))
    p_down = pair_sum('w_down', g_w_down, got, False)
    g_w_up, got_down = _matmul(h2, d_up, ta=True, out_dtype=BF16, name="mm_g_up", side=scatter([p_down], [False], 0, 3))
    (dx1, d_mix, sums2), (got, *got_down) = _norm_mod_bwd2(x1, dh2, dx2, mix, g_mlp, mod3, tr, side=_sides(
        _swap_halves([g_w_up], [True]), scatter([p_down], [False], 3, 4, got_down)))
    p_up = pair_sum('w_up', g_w_up, got, True)
    d_mrg = _matmul(d_mix, w_out_f, tb=True, name="mm_d_mrg")
    g_w_out = _matmul(mrg, d_mix, ta=True, out_dtype=BF16, name="mm_g_out")
    dz = _dz_start(t, n_in, n_ctx)
    d_ya, dz = _merge_bwd(d_mrg, y_attn, z, gl_off, dz, n_ctx, tr, tcol, "merge_bwd_attn")
    d_yr, dz = _merge_bwd(d_mrg, y_rnn, z, gl_off + d, dz, n_ctx, tr, tcol, "merge_bwd_rnn")
    d_o = _matmul(d_ya, w_oa_f, tb=True, out_dtype=BF16, name="mm_d_o")
    g_w_oa = _matmul(attn_o, d_ya, ta=True, out_dtype=BF16, name="mm_g_o_attn")
    d_u = _matmul(d_yr, w_or_f, tb=True, name="mm_d_u")
    g_w_or = _matmul(u, d_yr, ta=True, out_dtype=BF16, name="mm_g_o_rnn")
    d_rnn, dz = _rnn_gate_bwd(d_u, h_f, h_b, z, xg_off, dz, n_ctx, tr, tcol)
    gs_f, gs_b = _scan([(a_f, d_rnn, "lat_ctx_down"), (a_b, d_rnn, "lat_ctx_up")], post=True, n_ctx_rows=n_ctx,
                       name="scan_bwd")
    o_names, o_grads = ['w_o_attn', 'w_o_rnn', 'w_out'], [g_w_oa, g_w_or, g_w_out]
    (dz, g_w_rg, g_w_ig, sums_rnn), (got_down, *got_o) = _rnn_bwd(
        z, xr_off, xc, gs_f, gs_b, h_f, h_b, conv_w_f, w_rg[0], b_rg_f, w_ig[0], b_ig_f, lam_f, dz, n_ctx,
        side=_sides(scatter([p_down], [False], 4, 8, got_down), _swap_halves(o_grads, row3)))
    hs_down = chip_sum('w_down', p_down, got_down, False)
    p_o = [pair_sum(n, g, o, False) for n, g, o in zip(o_names, o_grads, got_o)]
    gate_cols = 8 * LANES if g_w_rg.size % (8 * LANES * N_CHIPS * 16) == 0 else 2 * LANES
    gate_rows = g_w_rg.size // gate_cols
    gate_grads = [g_w_rg.reshape(gate_rows, gate_cols), g_w_ig.reshape(gate_rows, gate_cols)]
    (dq, dk, dz), (got_up, got_oa, got_rg, got_ig, gs_down) = _attn_bwd(
        qr, kr, z, v_off, d_o, attn_o, lse, dz, n_ctx, group, tr, side=_sides(
            _scatter_regions([p_up], [True]), _scatter_regions(p_o[:1], row3[:1]), _swap_halves(gate_grads, row3[:2]),
            _join_halves([hs_down], [False])))
    hs_up = chip_sum('w_up', p_up, got_up, True)
    hs_oa = chip_sum('w_o_attn', p_o[0], got_oa, False)
    p_gate = [pair_sum(n, g, o, False) for n, g, o in zip(GATES, gate_grads, (got_rg, got_ig))]
    dz, g_q_gain = _head_prep_bwd(z, 0, attn_w // LANES, q_gain, cos, sin, dq, n_ctx, tr, dz, "q_prep_bwd")
    dz, g_k_gain = _head_prep_bwd(z, k_off, kv_w // LANES, k_gain, cos, sin, dk, 0, tr, dz, "k_prep_bwd")
    half = d // 2
    h_sibling = lax.dynamic_slice(h, (0, (1 - ci) * half), (t, half))
    h_own = lax.dynamic_slice(h, (0, ci * half), (t, half))
    late = p_o[1:] + p_gate
    g_sibling, (got_rg, got_ig, *got_late) = _matmul(
        h_sibling, dz, ta=True, out_dtype=BF16, name="mm_g_in_sibling", side=_sides(
            _scatter_regions(p_gate, row3[:2]), scatter(p_o[1:], row3[:2], 0, 4)))
    g_own, (got_or, got_out, got, gs_up, gs_oa) = _matmul(
        h_own, dz, ta=True, out_dtype=BF16, name="mm_g_in_own", side=_sides(
            scatter(p_o[1:], row3[:2], 4, 8, got_late), _send_to_sibling([g_sibling]), _join_halves([hs_up], [True]),
            _join_halves([hs_oa], [False])))
    hs_late = [chip_sum(n, p, o, False) for n, p, o in zip(o_names[1:] + GATES, late, (got_or, got_out, got_rg, got_ig))]
    p_in = _add_pair(g_own, got, "pair_sum_w_in")
    dh, (got_in, fold_in, gs_or, gs_out) = _matmul(dz, w_in_f, tb=True, name="mm_d_h", side=_sides(
        _scatter_forward_1(p_in), _join_halves(hs_late[:2], row3[:2])))
    nbr_chips = jnp.stack([2 * (1 - xi) + yi, 2 * xi + 1 - yi]).astype(jnp.int32)
    passed = _fold_forwarded(p_in, fold_in, nbr_chips, "fold_w_in")
    (got_in,) = _exchange(_scatter_forward_2(passed, got_in), "scatter_w_in_fwd")
    grad_x, sums1 = _norm_mod_bwd1(ctx[0], x[0], dh, dx1, g_mix, mod4, tr)

    zeros_d = jnp.zeros((d,), F32)
    dmod_lat = jnp.concatenate([sums1[0], sums1[1], sums2[3], sums2[0], sums2[1], sums_fin[1]])
    dmod_ctx = jnp.concatenate([sums1[3], sums1[4]] + [zeros_d] * 4)
    small_parts = [dmod_lat, dmod_ctx, loss_blk[0, 0:1], sums1[2] + sums1[5], sums2[2], g_q_gain[0], g_k_gain[0],
                   sums_rnn[10], sums_fin[0], sums_rnn[6:10], sums_rnn[0:2], sums_rnn[2:4], sums_rnn[4:6]]
    pack1 = _pack(small_parts)
    got1 = _all_gather8(pack1, "gather_small_grads").reshape(N_DEV, -1, LANES)
    tot1 = _sum_leading(got1, "sum_small_grads")
    part_shapes = [a.shape for a in small_parts]
    (s_dmod_lat, s_dmod_ctx, s_loss, g_g_mix, g_g_mlp, g_q_gain, g_k_gain, g_conv_b, g_g_final,
     g_conv_w_f, g_b_rg_f, g_b_ig_f, g_lam_f) = _unpack(tot1, part_shapes)
    loss = s_loss[0]
    g_b_mod = (s_dmod_lat + s_dmod_ctx)[None, :]
    n_mod_rows = _part_rows(d_mod)
    dmod16 = jnp.concatenate([got1[:, :n_mod_rows].reshape(N_DEV, -1)[:, :d_mod], s_dmod_ctx[None, :],
                              jnp.zeros((16 - N_DEV - 1, d_mod), F32)], axis=0)
    dmod16_shard = lax.dynamic_slice(dmod16, (0, chip * (d_mod // N_CHIPS)), (16, d_mod // N_CHIPS))
    g_w_mod = _matmul(silu16, dmod16_shard, ta=True, name="mm_g_mod")
    dsilu_part = _matmul(dmod16_shard[N_DEV:], w_mod[0], tb=True, name="mm_d_silu")
    dsilu_all = _all_gather8(dsilu_part, "gather_d_silu").reshape(N_DEV, 8, d)
    g_c_ctx = _c_ctx_grad(dsilu_all, c_ctx[None, :])[0]

    def shard_of(full):
        w = full.shape[-1] // N_CHIPS
        return lax.dynamic_slice(full, (0, chip * w), (full.shape[0], w))

    grads = {
        'c_ctx': g_c_ctx, 'b_mod': g_b_mod, 'g_mix': g_g_mix[None, :], 'g_mlp': g_g_mlp[None, :],
        'q_gain': g_q_gain[None, :], 'k_gain': g_k_gain[None, :], 'conv_b': g_conv_b[None, :],
        'g_final': g_g_final,
        'conv_w': shard_of(g_conv_w_f)[None], 'b_rg': shard_of(g_b_rg_f)[None], 'b_ig': shard_of(g_b_ig_f)[None],
        'lru_lambda': shard_of(g_lam_f)[None], 'w_mod': g_w_mod[None],
    }

    delta, new_m, new_v = {}, {}, {}

    def adamw(n):
        shp = weights[n].shape
        as2d = (lambda a: a[0]) if n not in GATES else (lambda a: a.reshape(-1, LANES))
        dl, nm, nv, *g = _adamw(as2d(weights[n]), as2d(grads[n]), as2d(moms[n]), as2d(vars_[n]), "adamw_" + n,
                                copy_grad=n in BIG)
        delta[n], new_m[n], new_v[n] = dl.reshape(shp), nm.reshape(shp), nv.reshape(shp)
        if g:
            grads[n] = g[0].reshape(shp)

    hs_in = chip_sum('w_in', p_in, got_in, True)
    (gs_in,) = _exchange(_join_halves([hs_in], [True]), "join_w_in")
    for n, g in zip(['w_in', 'w_o_attn', 'w_o_rnn', 'w_out', 'w_up', 'w_down'], [gs_in, gs_oa, gs_or, gs_out, gs_up, gs_down]):
        grads[n] = g[None]
    half_cols = gate_cols // 2
    mine = jnp.concatenate([lax.dynamic_slice(hs, (0, ci * half_cols), (gate_rows // N_CHIPS, half_cols))
                            for hs in hs_late[2:]], axis=0)
    gate_all = _all_gather8(mine, "gather_gate_grads")
    gate_all = gate_all.reshape(N_CHIPS, 2, len(GATES), gate_rows // N_CHIPS, half_cols)
    for i, n in enumerate(GATES):
        grads[n] = jnp.moveaxis(gate_all[:, :, i], 1, 2).reshape(weights[n].shape)
    for n in ['w_mod'] + BIG + GATES:
        adamw(n)
    small_shapes = [weights[n].shape for n in SMALL]
    packed = [_pack([src[n] for n in SMALL], 512) for src in (weights, grads, moms, vars_)]
    outs = _adamw(*packed, "adamw_small", copy_grad=False)
    for res, out in zip((delta, new_m, new_v), outs):
        for n, a in zip(SMALL, _unpack(out, small_shapes)):
            res[n] = a
    return (loss, grad_x[None], *[grads[n] for n in WEIGHT_NAMES], *[delta[n] for n in WEIGHT_NAMES],
            *[new_m[n] for n in WEIGHT_NAMES], *[new_v[n] for n in WEIGHT_NAMES])
```

```python
import functools
import math
from typing import Callable, NamedTuple

import jax
import jax.numpy as jnp
from jax import lax
from jax.experimental import pallas as pl
from jax.experimental.pallas import tpu as pltpu

F32 = jnp.float32
BF16 = jnp.bfloat16
MESH_ID = pl.DeviceIdType.MESH
ANY = pl.BlockSpec(memory_space=pl.ANY)

NORM_EPS = 1e-6
LRU_C = 8.0
GRID_W = 64
ROPE_THETA = 10000.0
N_MOD = 6
CONV_WIDTH = 4
ADAM_LR = 0.001
ADAM_B1 = 0.9
ADAM_B2 = 0.999
ADAM_EPS = 1e-08
ADAM_WD = 0.01
ADAM_STEP = 10

LANES = 128
SUBLANES = 8
V7X_VMEM_LIMIT = 48 * 1024 * 1024
WIDE_TILE = 11 * LANES
N_CHIPS = 4
N_DEV = 8
GELU_C = math.sqrt(2.0 / math.pi)
GELU_A = 0.044715


def _tile(dim, pref, align):
    t = min(pref, dim)
    t -= t % align
    while t >= align:
        if dim % t == 0:
            return t
        t -= align
    return dim


def _params(*sem):
    return pltpu.CompilerParams(dimension_semantics=sem, vmem_limit_bytes=V7X_VMEM_LIMIT)


def _sds(shape, dtype=F32):
    return jax.ShapeDtypeStruct(shape, dtype)


class _Side(NamedTuple):
    operands: tuple
    results: tuple
    aliases: dict
    n_sems: int
    build: Callable


def _sides(*sides):
    ops, res, aliases, spans, n = [], [], {}, [], 0
    for s in sides:
        spans.append((len(ops), len(res), n))
        aliases.update({len(ops) + i: len(res) + j for i, j in s.aliases.items()})
        ops += s.operands
        res += s.results
        n += s.n_sems

    def build(op_refs, res_refs, send_sems, recv_sems, sem0):
        sends, recvs = [], []
        for s, (o, r, k) in zip(sides, spans):
            a, b = s.build(op_refs[o:o + len(s.operands)], res_refs[r:r + len(s.results)], send_sems, recv_sems,
                           sem0 + k)
            sends += a
            recvs += b
        return sends, recvs

    return _Side(tuple(ops), tuple(res), aliases, n, build)


def _on_same(*sides):
    def build(ops, res, send_sems, recv_sems, sem0):
        sends, recvs = [], []
        for s in sides:
            a, b = s.build(ops, res, send_sems, recv_sems, sem0)
            sends += a
            recvs += b
            sem0 += s.n_sems
        return sends, recvs

    return _Side(sides[0].operands, sides[0].results, sides[0].aliases, sum(s.n_sems for s in sides), build)


def _call(body, *, side=None, sem=(), grid=(), in_specs=(), out_specs=(), out_shape=(), scratch_shapes=(), **kw):
    if side is None:
        return pl.pallas_call(body, grid=grid, in_specs=list(in_specs), out_specs=out_specs, out_shape=out_shape,
                              scratch_shapes=list(scratch_shapes), compiler_params=_params(*sem), **kw)
    aliases = kw.pop("input_output_aliases", {})
    many = isinstance(out_shape, (list, tuple))
    out_specs_l, out_shape_l = (list(out_specs), list(out_shape)) if many else ([out_specs], [out_shape])
    n_in, n_out, n_scr = len(in_specs), len(out_shape_l), len(scratch_shapes)
    n_op, n_res = len(side.operands), len(side.results)

    def hosted(*refs):
        ins, ops = refs[:n_in], refs[n_in:n_in + n_op]
        outs = refs[n_in + n_op:n_in + n_op + n_out]
        res = refs[n_in + n_op + n_out:n_in + n_op + n_out + n_res]
        scr = refs[n_in + n_op + n_out + n_res:-2]
        send_sems, recv_sems = refs[-2:]

        def start():
            for cp in side.build(ops, res, send_sems, recv_sems, 0)[0]:
                cp.start()

        def finish():
            sends, recvs = side.build(ops, res, send_sems, recv_sems, 0)
            for cp in recvs:
                cp.wait_recv()
            for cp in sends:
                cp.wait_send()

        if not grid:
            start()
            finish()
            return
        ids = [pl.program_id(a) for a in range(len(grid))]
        first = functools.reduce(jnp.logical_and, [i == 0 for i in ids])
        last = functools.reduce(jnp.logical_and, [i == g - 1 for i, g in zip(ids, grid)])
        pl.when(first)(start)
        body(*ins, *outs, *scr)
        pl.when(last)(finish)

    def run(*args):
        got = pl.pallas_call(
            hosted, grid=grid, in_specs=[*in_specs, *[ANY] * n_op], out_specs=[*out_specs_l, *[ANY] * n_res],
            out_shape=[*out_shape_l, *side.results],
            scratch_shapes=[*scratch_shapes, pltpu.SemaphoreType.DMA((side.n_sems,)),
                            pltpu.SemaphoreType.DMA((side.n_sems,))],
            input_output_aliases={**aliases, **{n_in + i: n_out + j for i, j in side.aliases.items()}},
            compiler_params=_params(*["arbitrary"] * len(grid)), **kw)(*args, *side.operands)
        own = list(got[:n_out]) if many else got[0]
        return own, list(got[n_out:])

    return run


def _matmul(a, b, *, ta=False, tb=False, out_dtype=F32, name, tm=1024, tn=1024, tk=2816, side=None, post=None,
            post_args=()):
    k_dim, m = a.shape if ta else a.shape[::-1]
    n, k2 = b.shape if tb else b.shape[::-1]
    assert k_dim == k2, (a.shape, b.shape, ta, tb)
    tm = _tile(m, tm, LANES if ta else 16)
    tn = _tile(n, tn, 16 if tb else LANES)
    tk = _tile(k_dim, tk, LANES)
    nk = k_dim // tk
    dims = (((0 if ta else 1,), (1 if tb else 0,)), ((), ()))
    if nk == 1:
        several = isinstance(out_dtype, tuple)

        def whole(a_ref, b_ref, *rest):
            acc = lax.dot_general(a_ref[...].astype(BF16), b_ref[...].astype(BF16), dims, preferred_element_type=F32)
            outs = rest[len(post_args):]
            if post is not None:
                acc = post(acc, *[r[...] for r in rest[:len(post_args)]])
            for o_ref, val in zip(outs, acc if several else (acc,)):
                o_ref[...] = val.astype(o_ref.dtype)

        a_spec = pl.BlockSpec((tk, tm), lambda i, j: (0, i)) if ta else pl.BlockSpec((tm, tk), lambda i, j: (i, 0))
        b_spec = pl.BlockSpec((tn, tk), lambda i, j: (j, 0)) if tb else pl.BlockSpec((tk, tn), lambda i, j: (0, j))
        o_spec = pl.BlockSpec((tm, tn), lambda i, j: (i, j))
        return _call(
            whole, side=side, name=name, grid=(m // tm, n // tn), in_specs=[a_spec, b_spec] + [o_spec] * len(post_args),
            out_specs=[o_spec] * len(out_dtype) if several else o_spec,
            out_shape=[_sds((m, n), dt) for dt in out_dtype] if several else _sds((m, n), out_dtype),
            sem=("parallel", "parallel"),
        )(a, b, *post_args)
    assert post is None

    def body(a_ref, b_ref, o_ref, acc_ref):
        k = pl.program_id(2)

        @pl.when(k == 0)
        def _():
            acc_ref[...] = jnp.zeros_like(acc_ref)

        acc_ref[...] += lax.dot_general(a_ref[...].astype(BF16), b_ref[...].astype(BF16), dims,
                                        preferred_element_type=F32)

        @pl.when(k == nk - 1)
        def _():
            o_ref[...] = acc_ref[...].astype(o_ref.dtype)

    a_spec = pl.BlockSpec((tk, tm), lambda i, j, k: (k, i)) if ta else pl.BlockSpec((tm, tk), lambda i, j, k: (i, k))
    b_spec = pl.BlockSpec((tn, tk), lambda i, j, k: (j, k)) if tb else pl.BlockSpec((tk, tn), lambda i, j, k: (k, j))
    return _call(
        body, side=side, name=name, grid=(m // tm, n // tn, nk), in_specs=[a_spec, b_spec],
        out_specs=pl.BlockSpec((tm, tn), lambda i, j, k: (i, j)), out_shape=_sds((m, n), out_dtype),
        scratch_shapes=[pltpu.VMEM((tm, tn), F32)], sem=("parallel", "parallel", "arbitrary"),
    )(a, b)


def _silu(x):
    return x * jax.nn.sigmoid(x)


def _gelu(x):
    return 0.5 * x * (1.0 + jnp.tanh(GELU_C * (x + GELU_A * x * x * x)))


def _gelu_grad(x):
    t = jnp.tanh(GELU_C * (x + GELU_A * x * x * x))
    return 0.5 * (1.0 + t) + 0.5 * x * (1.0 - t * t) * GELU_C * (1.0 + 3.0 * GELU_A * x * x)


def _expm1_nonpos(x):
    series = x * (1.0 + x * (1.0 / 2 + x * (1.0 / 6 + x * (1.0 / 24 + x * (1.0 / 120 + x * (1.0 / 720 + x / 5040))))))
    return jnp.where(x > -0.25, series, jnp.exp(x) - 1.0)


def _softplus(x):
    return jnp.maximum(x, 0.0) + jnp.log1p(jnp.exp(-jnp.abs(x)))


def _rms_stats(x):
    return lax.rsqrt(jnp.mean(x * x, axis=-1, keepdims=True) + NORM_EPS)


def _rms_bwd(dxhat, xhat, rstd):
    return rstd * (dxhat - xhat * jnp.mean(dxhat * xhat, axis=-1, keepdims=True))


def _colsum(v):
    return jnp.sum(v, axis=0, keepdims=True)


def _mod_fwd(c16, w_mod, b_mod_shard):
    r, d = c16.shape
    n = w_mod.shape[1]
    tn = _tile(n, 512, LANES)

    def body(c_ref, w_ref, b_ref, o_ref, s_ref):
        s = _silu(c_ref[...])
        s_ref[...] = s
        o_ref[...] = jnp.dot(s.astype(BF16), w_ref[...].astype(BF16), preferred_element_type=F32) + b_ref[...]

    return pl.pallas_call(
        body, name="mod_fwd", grid=(n // tn,),
        in_specs=[pl.BlockSpec((r, d), lambda j: (0, 0)), pl.BlockSpec((d, tn), lambda j: (0, j)),
                  pl.BlockSpec((1, tn), lambda j: (0, j))],
        out_specs=[pl.BlockSpec((r, tn), lambda j: (0, j)), pl.BlockSpec((r, d), lambda j: (0, 0))],
        out_shape=[_sds((r, n)), _sds((r, d))], compiler_params=_params("arbitrary"),
    )(c16, w_mod, b_mod_shard)


def _c_ctx_grad(parts, c_ctx_row):
    d = c_ctx_row.shape[1]

    def body(p_ref, c_ref, o_ref):
        tot = p_ref[0, 0:1, :]
        for chip in range(1, N_CHIPS):
            tot = tot + p_ref[2 * chip, 0:1, :]
        c = c_ref[...]
        sg = jax.nn.sigmoid(c)
        o_ref[...] = tot * (sg * (1.0 + c * (1.0 - sg)))

    return pl.pallas_call(body, name="c_ctx_grad", out_shape=_sds((1, d)), compiler_params=_params())(parts, c_ctx_row)


def _token_specs(n_ctx_rows, d, tr):
    nctx = n_ctx_rows // tr
    return (pl.BlockSpec((tr, d), lambda i: (jnp.minimum(i, nctx - 1), 0)),
            pl.BlockSpec((tr, d), lambda i: (jnp.maximum(i - nctx, 0), 0)))


def _norm_mod_fwd(ctx, x, g, mod4, tr):
    (n_ctx_rows, d), s = ctx.shape, x.shape[0]
    t = n_ctx_rows + s
    nctx = n_ctx_rows // tr

    def body(c_ref, x_ref, g_ref, mod_ref, h_ref):
        is_ctx = pl.program_id(0) < nctx
        x = jnp.where(is_ctx, c_ref[...], x_ref[...])
        n = x * _rms_stats(x) * g_ref[...]
        sh = jnp.where(is_ctx, mod_ref[0:1, :], mod_ref[2:3, :])
        sc = jnp.where(is_ctx, mod_ref[1:2, :], mod_ref[3:4, :])
        h_ref[...] = (n * (1.0 + sc) + sh).astype(BF16)

    return pl.pallas_call(
        body, name="norm_mod_fwd", grid=(t // tr,),
        in_specs=[*_token_specs(n_ctx_rows, d, tr), pl.BlockSpec((1, d), lambda i: (0, 0)),
                  pl.BlockSpec((4, d), lambda i: (0, 0))],
        out_specs=pl.BlockSpec((tr, d), lambda i: (i, 0)), out_shape=_sds((t, d), BF16),
        compiler_params=_params("parallel"),
    )(ctx, x, g, mod4)


def _norm_mod_bwd1(ctx, x, dh, dx1, g, mod4, tr, side=None):
    (n_ctx_rows, d), s = ctx.shape, x.shape[0]
    t = n_ctx_rows + s
    nctx = n_ctx_rows // tr

    def body(c_ref, x_ref, dh_ref, dx1_ref, g_ref, mod_ref, dx_ref, sums_ref):
        i = pl.program_id(0)
        is_ctx = i < nctx

        @pl.when(i == 0)
        def _():
            sums_ref[...] = jnp.zeros_like(sums_ref)

        x = jnp.where(is_ctx, c_ref[...], x_ref[...])
        dh_ = dh_ref[...]
        rstd = _rms_stats(x)
        xhat = x * rstd
        gg = g_ref[...]
        sc = jnp.where(is_ctx, mod_ref[1:2, :], mod_ref[3:4, :])
        dxhat = dh_ * (1.0 + sc) * gg
        dx_ref[...] = dx1_ref[...] + _rms_bwd(dxhat, xhat, rstd)
        part = [_colsum(dh_), _colsum(dh_ * xhat * gg), _colsum(dh_ * (1.0 + sc) * xhat)]

        @pl.when(is_ctx)
        def _():
            for k, row in enumerate(part):
                sums_ref[3 + k:4 + k, :] += row

        @pl.when(jnp.logical_not(is_ctx))
        def _():
            for k, row in enumerate(part):
                sums_ref[k:k + 1, :] += row

    lat = lambda i: (jnp.maximum(i - nctx, 0), 0)
    return _call(
        body, side=side, name="norm_mod_bwd1", grid=(t // tr,),
        in_specs=[*_token_specs(n_ctx_rows, d, tr), pl.BlockSpec((tr, d), lambda i: (i, 0)),
                  pl.BlockSpec((tr, d), lat), pl.BlockSpec((1, d), lambda i: (0, 0)),
                  pl.BlockSpec((4, d), lambda i: (0, 0))],
        out_specs=[pl.BlockSpec((tr, d), lat), pl.BlockSpec((8, d), lambda i: (0, 0))],
        out_shape=[_sds((s, d)), _sds((8, d))], sem=("arbitrary",),
    )(ctx, x, dh, dx1, g, mod4)


def _resid_norm_mod_fwd(x, mix, g, mod3, tr):
    s, d = x.shape

    def body(x_ref, mix_ref, g_ref, mod_ref, x1_ref, h_ref):
        x1 = x_ref[...] + mod_ref[0:1, :] * mix_ref[...]
        x1_ref[...] = x1
        n = x1 * _rms_stats(x1) * g_ref[...]
        h_ref[...] = (n * (1.0 + mod_ref[2:3, :]) + mod_ref[1:2, :]).astype(BF16)

    row = pl.BlockSpec((tr, d), lambda i: (i, 0))
    return pl.pallas_call(
        body, name="resid_norm_mod_fwd", grid=(s // tr,),
        in_specs=[row, row, pl.BlockSpec((1, d), lambda i: (0, 0)), pl.BlockSpec((3, d), lambda i: (0, 0))],
        out_specs=[row, row], out_shape=[_sds((s, d)), _sds((s, d), BF16)], compiler_params=_params("parallel"),
    )(x, mix, g, mod3)


def _norm_mod_bwd2(x1, dh2, dx2, mix, g, mod3, tr, side=None):
    s, d = x1.shape

    def body(x_ref, dh_ref, dx2_ref, mix_ref, g_ref, mod_ref, dx1_ref, dmix_ref, sums_ref):
        @pl.when(pl.program_id(0) == 0)
        def _():
            sums_ref[...] = jnp.zeros_like(sums_ref)

        x = x_ref[...]
        dh_ = dh_ref[...]
        rstd = _rms_stats(x)
        xhat = x * rstd
        gg = g_ref[...]
        sc = mod_ref[2:3, :]
        dx1 = dx2_ref[...] + _rms_bwd(dh_ * (1.0 + sc) * gg, xhat, rstd)
        dx1_ref[...] = dx1
        dmix_ref[...] = (dx1 * mod_ref[0:1, :]).astype(BF16)
        part = [_colsum(dh_), _colsum(dh_ * xhat * gg), _colsum(dh_ * (1.0 + sc) * xhat), _colsum(dx1 * mix_ref[...])]
        for k, row in enumerate(part):
            sums_ref[k:k + 1, :] += row

    row = pl.BlockSpec((tr, d), lambda i: (i, 0))
    return _call(
        body, side=side, name="norm_mod_bwd2", grid=(s // tr,),
        in_specs=[row, row, row, row, pl.BlockSpec((1, d), lambda i: (0, 0)), pl.BlockSpec((3, d), lambda i: (0, 0))],
        out_specs=[row, row, pl.BlockSpec((8, d), lambda i: (0, 0))],
        out_shape=[_sds((s, d)), _sds((s, d), BF16), _sds((8, d))], sem=("arbitrary",),
    )(x1, dh2, dx2, mix, g, mod3)


def _final_fwd_bwd(x1, down, target, g_final, gate, tr):
    s, d = x1.shape

    def body(x1_ref, down_ref, t_ref, g_ref, gate_ref, dx2_ref, ddown_ref, sums_ref, loss_ref):
        @pl.when(pl.program_id(0) == 0)
        def _():
            sums_ref[...] = jnp.zeros_like(sums_ref)
            loss_ref[...] = jnp.zeros_like(loss_ref)

        down_ = down_ref[...]
        gate_ = gate_ref[...]
        x2 = x1_ref[...] + gate_ * down_
        rstd = _rms_stats(x2)
        xhat = x2 * rstd
        gg = g_ref[...]
        err = xhat * gg - t_ref[...]
        loss_ref[...] += 0.5 * jnp.sum(jnp.mean(err * err, axis=-1, keepdims=True))
        dy = err * (1.0 / d)
        dx2 = _rms_bwd(dy * gg, xhat, rstd)
        dx2_ref[...] = dx2
        ddown_ref[...] = (dx2 * gate_).astype(BF16)
        sums_ref[0:1, :] += _colsum(dy * xhat)
        sums_ref[1:2, :] += _colsum(dx2 * down_)

    row = pl.BlockSpec((tr, d), lambda i: (i, 0))
    vec = pl.BlockSpec((1, d), lambda i: (0, 0))
    return pl.pallas_call(
        body, name="final_fwd_bwd", grid=(s // tr,), in_specs=[row, row, row, vec, vec],
        out_specs=[row, row, pl.BlockSpec((8, d), lambda i: (0, 0)), pl.BlockSpec((8, LANES), lambda i: (0, 0))],
        out_shape=[_sds((s, d)), _sds((s, d), BF16), _sds((8, d)), _sds((8, LANES))],
        compiler_params=_params("arbitrary"),
    )(x1, down, target, g_final, gate)


def _swap_pairs(v):
    lane = lax.broadcasted_iota(jnp.int32, v.shape, 1)
    return jnp.where(lane % 2 == 0, pltpu.roll(v, LANES - 1, 1), pltpu.roll(v, 1, 1))


def _head_prep_fwd(z, col_off, n_heads, gain, cos, sin, tr, name, side=None):
    t = z.shape[0]
    per = math.gcd(4, n_heads, col_off // LANES)
    w = per * LANES
    hb = col_off // w

    def body(z_ref, g_ref, cos_ref, sin_ref, o_ref):
        for hh in range(per):
            cols = slice(hh * LANES, (hh + 1) * LANES)
            x = z_ref[:, cols]
            y = x * _rms_stats(x) * g_ref[...]
            o_ref[:, cols] = (y * cos_ref[...] + _swap_pairs(y) * sin_ref[...]).astype(BF16)

    tab = pl.BlockSpec((tr, LANES), lambda i, j: (i, 0))
    return _call(
        body, side=side, name=name, grid=(t // tr, n_heads // per),
        in_specs=[pl.BlockSpec((tr, w), lambda i, j: (i, hb + j)), pl.BlockSpec((1, LANES), lambda i, j: (0, 0)),
                  tab, tab],
        out_specs=pl.BlockSpec((tr, w), lambda i, j: (i, j)), out_shape=_sds((t, n_heads * LANES), BF16),
        sem=("parallel", "parallel"),
    )(z, gain, cos, sin)


def _dz_start(t, n_in, n_ctx_rows):
    tcol = _tile(n_in, 1024, LANES)

    def body(o_ref):
        o_ref[...] = jnp.zeros_like(o_ref)

    return pl.pallas_call(body, name="dz_start", grid=(n_in // tcol,),
                          out_specs=pl.BlockSpec((n_ctx_rows, tcol), lambda j: (0, j)), out_shape=_sds((t, n_in), BF16),
                          compiler_params=_params("parallel"))()


def _head_prep_bwd(z, col_off, n_heads, gain, cos, sin, dout, row_off, tr, dz, name, side=None):
    r = dout.shape[0]
    per = math.gcd(4, n_heads, col_off // LANES)
    w = per * LANES
    hb = col_off // w
    rb = row_off // tr

    def body(z_ref, g_ref, cos_ref, sin_ref, d_ref, _, dz_ref, dg_ref):
        @pl.when(jnp.logical_and(pl.program_id(0) == 0, pl.program_id(1) == 0))
        def _():
            dg_ref[...] = jnp.zeros_like(dg_ref)

        for hh in range(per):
            cols = slice(hh * LANES, (hh + 1) * LANES)
            x = z_ref[:, cols]
            rstd = _rms_stats(x)
            xhat = x * rstd
            dd = d_ref[:, cols]
            dy = dd * cos_ref[...] - _swap_pairs(dd) * sin_ref[...]
            dg_ref[0:1, :] += _colsum(dy * xhat)
            dz_ref[:, cols] = _rms_bwd(dy * g_ref[...], xhat, rstd).astype(BF16)

    tab = pl.BlockSpec((tr, LANES), lambda i, j: (rb + i, 0))
    window = pl.BlockSpec((tr, w), lambda i, j: (rb + i, hb + j))
    return _call(
        body, side=side, name=name, grid=(r // tr, n_heads // per),
        in_specs=[window, pl.BlockSpec((1, LANES), lambda i, j: (0, 0)), tab, tab,
                  pl.BlockSpec((tr, w), lambda i, j: (i, j)), ANY],
        out_specs=[window, pl.BlockSpec((8, LANES), lambda i, j: (0, 0))],
        out_shape=[_sds(dz.shape, BF16), _sds((8, LANES))], input_output_aliases={5: 0}, sem=("arbitrary", "arbitrary"),
    )(z, gain, cos, sin, dout, dz)


def _attn_fwd(qr, kr, z, v_off, n_ctx_rows, group, tq, side=None):
    t, kvw = kr.shape
    s = t - n_ctx_rows
    n_kv = kvw // LANES
    scale = LANES ** -0.5
    qb0 = n_ctx_rows // tq
    vb = v_off // LANES

    def body(q_ref, k_ref, v_ref, o_ref, lse_ref):
        k = k_ref[...]
        v = v_ref[...].astype(BF16)
        lse_ref[...] = jnp.zeros_like(lse_ref)
        for g in range(group):
            cols = slice(g * LANES, (g + 1) * LANES)
            sc = lax.dot_general(q_ref[:, cols], k, (((1,), (1,)), ((), ())), preferred_element_type=F32) * scale
            m = jnp.max(sc, axis=-1, keepdims=True)
            e = jnp.exp(sc - m)
            l = jnp.sum(e, axis=-1, keepdims=True)
            p = e * (1.0 / l)
            o_ref[:, cols] = jnp.dot(p.astype(BF16), v, preferred_element_type=F32).astype(BF16)
            lse_ref[:, g:g + 1] = m + jnp.log(l)

    return _call(
        body, side=side, name="attn_fwd", grid=(n_kv, s // tq),
        in_specs=[pl.BlockSpec((tq, group * LANES), lambda h, i: (qb0 + i, h)),
                  pl.BlockSpec((t, LANES), lambda h, i: (0, h)), pl.BlockSpec((t, LANES), lambda h, i: (0, vb + h))],
        out_specs=[pl.BlockSpec((tq, group * LANES), lambda h, i: (i, h)), pl.BlockSpec((tq, LANES), lambda h, i: (i, h))],
        out_shape=[_sds((s, n_kv * group * LANES), BF16), _sds((s, kvw))], sem=("parallel", "parallel"),
    )(qr, kr, z)


def _attn_bwd(qr, kr, z, v_off, d_o, attn_o, lse, dz, n_ctx_rows, group, tq, side=None):
    t, kvw = kr.shape
    s = t - n_ctx_rows
    n_kv = kvw // LANES
    scale = LANES ** -0.5
    qb0 = n_ctx_rows // tq
    vb = v_off // LANES
    n_q_blocks = s // tq
    tn_dims = (((0,), (0,)), ((), ()))
    nt_dims = (((1,), (1,)), ((), ()))

    def body(q_ref, k_ref, v_ref, do_ref, o_ref, lse_ref, _, dq_ref, dk_ref, dz_ref, dv_ref):
        @pl.when(pl.program_id(1) == 0)
        def _():
            dk_ref[...] = jnp.zeros_like(dk_ref)
            dv_ref[...] = jnp.zeros_like(dv_ref)

        k = k_ref[...]
        v = v_ref[...].astype(BF16)
        for g in range(group):
            cols = slice(g * LANES, (g + 1) * LANES)
            q = q_ref[:, cols]
            do_ = do_ref[:, cols]
            row_dot = jnp.sum(do_.astype(F32) * o_ref[:, cols].astype(F32), axis=-1, keepdims=True)
            sc = lax.dot_general(q, k, nt_dims, preferred_element_type=F32)
            p = jnp.exp(sc * scale - lse_ref[:, g:g + 1])
            dv_ref[...] += lax.dot_general(p.astype(BF16), do_, tn_dims, preferred_element_type=F32)
            dp = lax.dot_general(do_, v, nt_dims, preferred_element_type=F32)
            ds = (p * (dp - row_dot)).astype(BF16)
            dq_ref[:, cols] = jnp.dot(ds, k, preferred_element_type=F32) * scale
            dk_ref[...] += lax.dot_general(ds, q, tn_dims, preferred_element_type=F32)

        @pl.when(pl.program_id(1) == n_q_blocks - 1)
        def _():
            dk_ref[...] = dk_ref[...] * scale
            dz_ref[...] = dv_ref[...].astype(BF16)

    qspec = pl.BlockSpec((tq, group * LANES), lambda h, i: (qb0 + i, h))
    ospec = pl.BlockSpec((tq, group * LANES), lambda h, i: (i, h))
    kspec = pl.BlockSpec((t, LANES), lambda h, i: (0, h))
    vspec = pl.BlockSpec((t, LANES), lambda h, i: (0, vb + h))
    return _call(
        body, side=side, name="attn_bwd", grid=(n_kv, n_q_blocks),
        in_specs=[qspec, kspec, vspec, ospec, ospec, pl.BlockSpec((tq, LANES), lambda h, i: (i, h)), ANY],
        out_specs=[ospec, kspec, vspec], out_shape=[_sds((s, n_kv * group * LANES)), _sds((t, kvw)), _sds(dz.shape, BF16)],
        scratch_shapes=[pltpu.VMEM((t, LANES), F32)], input_output_aliases={6: 2}, sem=("parallel", "arbitrary"),
    )(qr, kr, z, d_o, attn_o, lse, dz)


def _row_mask(shape, rows):
    r = lax.broadcasted_iota(jnp.int32, shape, 0)
    m = r == rows[0]
    for v in rows[1:]:
        m = jnp.logical_or(m, r == v)
    return m


def _shift_rows(x, k, n_ctx_rows):
    t = x.shape[0]
    if k == 0:
        return x
    rolled = pltpu.roll(x, (-k) % t, 0)
    if k > 0:
        dead = [n_ctx_rows - 1 - i for i in range(k)] + [t - 1 - i for i in range(k)]
    else:
        dead = [i for i in range(-k)] + [n_ctx_rows + i for i in range(-k)]
    return jnp.where(_row_mask(x.shape, dead), 0.0, rolled)


def _conv(x, w, b, n_ctx_rows):
    y = b
    for k in range(CONV_WIDTH):
        y = y + _shift_rows(x, k - 1, n_ctx_rows) * w[k:k + 1, :]
    return y


def _gates(xc_bf, w_r, b_r, w_i, b_i, lam):
    r = jax.nn.sigmoid(jnp.dot(xc_bf, w_r.astype(BF16), preferred_element_type=F32) + b_r)
    i = jax.nn.sigmoid(jnp.dot(xc_bf, w_i.astype(BF16), preferred_element_type=F32) + b_i)
    log_a = -LRU_C * r * _softplus(-lam)
    a = jnp.exp(log_a)
    mult = jnp.sqrt(-_expm1_nonpos(2.0 * log_a))
    return r, i, a, mult


def _rnn_specs(t, xr_off):
    xb = xr_off // LANES
    return dict(
        zcol=pl.BlockSpec((t, LANES), lambda j: (0, xb + j)), col=pl.BlockSpec((t, LANES), lambda j: (0, j)),
        conv_w=pl.BlockSpec((CONV_WIDTH, LANES), lambda j: (0, j)), vec=pl.BlockSpec((1, LANES), lambda j: (0, j)),
        gate_w=pl.BlockSpec((2, 1, LANES, LANES), lambda j: (0, j, 0, 0)), two=pl.BlockSpec((2, LANES), lambda j: (0, j)))


def _rnn_prep(z, xr_off, conv_w, conv_b, w_rg, b_rg, w_ig, b_ig, lam, n_ctx_rows, side=None):
    t = z.shape[0]
    d = conv_b.shape[1]
    sp = _rnn_specs(t, xr_off)

    def body(z_ref, cw_ref, cb_ref, wr_ref, br_ref, wi_ref, bi_ref, lam_ref, xc_ref, af_ref, bf_ref, ab_ref, bb_ref):
        xc = _conv(z_ref[...], cw_ref[...], cb_ref[...], n_ctx_rows)
        xc_ref[...] = xc
        xc_bf = xc.astype(BF16)
        for dr, (a_ref, b_ref) in enumerate(((af_ref, bf_ref), (ab_ref, bb_ref))):
            _, i, a, mult = _gates(xc_bf, wr_ref[dr, 0], br_ref[dr:dr + 1, :], wi_ref[dr, 0], bi_ref[dr:dr + 1, :],
                                   lam_ref[dr:dr + 1, :])
            a_ref[...] = a
            b_ref[...] = mult * (i * xc)

    return _call(
        body, side=side, name="rnn_prep", grid=(d // LANES,),
        in_specs=[sp["zcol"], sp["conv_w"], sp["vec"], sp["gate_w"], sp["two"], sp["gate_w"], sp["two"], sp["two"]],
        out_specs=[sp["col"]] * 5, out_shape=[_sds((t, d))] * 5, sem=("parallel",),
    )(z, conv_w, conv_b, w_rg, b_rg, w_ig, b_ig, lam)


def _scan(chains, *, post, n_ctx_rows, name, tc=256, side=None):
    t, d = chains[0][0].shape
    tc = _tile(math.gcd(n_ctx_rows, t - n_ctx_rows), tc, SUBLANES)
    nt, nctx = t // tc, n_ctx_rows // tc
    nlat = nt - nctx
    nc = len(chains)
    lat_only = [b.shape[0] != t for _, b, _ in chains]
    ups = [order.endswith("up") for _, _, order in chains]

    def chunk_of(order):
        def chunk(i):
            if order == "ctx_lat_up":
                return i
            if order == "lat_ctx_down":
                return nt - 1 - i
            if order == "ctx_lat_down":
                return jnp.where(i < nctx, nctx - 1 - i, nt - 1 - (i - nctx))
            return jnp.where(i < nlat, nctx + i, i - nlat)
        return chunk

    chunks = [chunk_of(order) for _, _, order in chains]

    def body(*refs):
        ab_refs, o_refs, carry_ref = refs[:2 * nc], refs[2 * nc:3 * nc], refs[3 * nc]

        @pl.when(pl.program_id(0) == 0)
        def _():
            carry_ref[...] = jnp.zeros_like(carry_ref)

        live = [jnp.where(chunks[n](pl.program_id(0)) >= nctx, 1.0, 0.0) if lat_only[n] else None for n in range(nc)]

        def group(gi, carries):
            carries = list(carries)
            bases = [pl.multiple_of((gi if ups[n] else tc // SUBLANES - 1 - gi) * SUBLANES, SUBLANES) for n in range(nc)]
            for step in range(SUBLANES):
                for n in range(nc):
                    row = bases[n] + (step if ups[n] else SUBLANES - 1 - step)
                    a_r = ab_refs[2 * n][pl.ds(row, 1), :]
                    b_r = ab_refs[2 * n + 1][pl.ds(row, 1), :]
                    if live[n] is not None:
                        b_r = b_r * live[n]
                    if post:
                        out = b_r + carries[n]
                        carries[n] = a_r * out
                    else:
                        out = a_r * carries[n] + b_r
                        carries[n] = out
                    o_refs[n][pl.ds(row, 1), :] = out
            return tuple(carries)

        done = lax.fori_loop(0, tc // SUBLANES, group, tuple(carry_ref[n:n + 1, :] for n in range(nc)))
        for n in range(nc):
            carry_ref[n:n + 1, :] = done[n]

    in_specs, out_specs, args = [], [], []
    for n, (a, b, _) in enumerate(chains):
        full = pl.BlockSpec((tc, d), lambda i, n=n: (chunks[n](i), 0))
        lat = pl.BlockSpec((tc, d), lambda i, n=n: (jnp.maximum(chunks[n](i) - nctx, 0), 0))
        in_specs += [full, lat if lat_only[n] else full]
        out_specs.append(full)
        args += [a, b]
    return _call(
        body, side=side, name=name, grid=(nt,), in_specs=in_specs, out_specs=out_specs, out_shape=[_sds((t, d))] * nc,
        scratch_shapes=[pltpu.VMEM((SUBLANES, d), F32)], sem=("arbitrary",),
    )(*args)


def _rnn_bwd(z, xr_off, xc, g_f, g_b, h_f, h_b, conv_w, w_rg, b_rg, w_ig, b_ig, lam, dz, n_ctx_rows, side=None):
    t, d = xc.shape
    sp = _rnn_specs(t, xr_off)
    tn_dims = (((0,), (0,)), ((), ()))
    nt_dims = (((1,), (1,)), ((), ()))

    def body(z_ref, xc_ref, gf_ref, gb_ref, hf_ref, hb_ref, cw_ref, wr_ref, br_ref, wi_ref, bi_ref, lam_ref, _,
             dxr_ref, dwr_ref, dwi_ref, sums_ref):
        xc_ = xc_ref[...]
        xc_bf = xc_.astype(BF16)
        dxc = jnp.zeros_like(xc_)
        sums = [None] * 6
        for dr, (g_ref, h_ref) in enumerate(((gf_ref, hf_ref), (gb_ref, hb_ref))):
            w_r, w_i, lam_ = wr_ref[dr, 0], wi_ref[dr, 0], lam_ref[dr:dr + 1, :]
            r, i, a, mult = _gates(xc_bf, w_r, br_ref[dr:dr + 1, :], w_i, bi_ref[dr:dr + 1, :], lam_)
            g = g_ref[...]
            h = h_ref[...]
            if dr == 0:
                h_prev = jnp.where(_row_mask(h.shape, [0]), 0.0, pltpu.roll(h, 1, 0))
            else:
                h_prev = jnp.where(_row_mask(h.shape, [n_ctx_rows - 1]), 0.0, pltpu.roll(h, t - 1, 0))
            d_mult = g * i * xc_
            d_i = g * mult * xc_
            dxc = dxc + g * mult * i
            d_log_a = g * h_prev * a - d_mult * a * a / mult
            sp_ = _softplus(-lam_)
            d_r = d_log_a * (-LRU_C) * sp_
            d_sp = _colsum(d_log_a * (-LRU_C) * r)
            du_r = (d_r * r * (1.0 - r))
            du_i = (d_i * i * (1.0 - i))
            sums[dr] = _colsum(du_r)
            sums[2 + dr] = _colsum(du_i)
            sums[4 + dr] = d_sp * (-jax.nn.sigmoid(-lam_))
            du_r_bf, du_i_bf = du_r.astype(BF16), du_i.astype(BF16)
            dwr_ref[dr, 0] = lax.dot_general(xc_bf, du_r_bf, tn_dims, preferred_element_type=F32).astype(BF16)
            dwi_ref[dr, 0] = lax.dot_general(xc_bf, du_i_bf, tn_dims, preferred_element_type=F32).astype(BF16)
            dxc = dxc + lax.dot_general(du_r_bf, w_r.astype(BF16), nt_dims, preferred_element_type=F32)
            dxc = dxc + lax.dot_general(du_i_bf, w_i.astype(BF16), nt_dims, preferred_element_type=F32)
        xr = z_ref[...]
        cw = cw_ref[...]
        dxr = jnp.zeros_like(dxc)
        rows = list(sums)
        for k in range(CONV_WIDTH):
            dxr = dxr + _shift_rows(dxc, 1 - k, n_ctx_rows) * cw[k:k + 1, :]
            rows.append(_colsum(dxc * _shift_rows(xr, k - 1, n_ctx_rows)))
        rows.append(_colsum(dxc))
        dxr_ref[...] = dxr.astype(BF16)
        sums_ref[...] = jnp.zeros_like(sums_ref)
        for k, row in enumerate(rows):
            sums_ref[k:k + 1, :] = row

    return _call(
        body, side=side, name="rnn_bwd", grid=(d // LANES,),
        in_specs=[sp["zcol"]] + [sp["col"]] * 5 + [sp["conv_w"], sp["gate_w"], sp["two"], sp["gate_w"], sp["two"],
                                                  sp["two"], ANY],
        out_specs=[sp["zcol"], sp["gate_w"], sp["gate_w"], pl.BlockSpec((16, LANES), lambda j: (0, j))],
        out_shape=[_sds(dz.shape, BF16), _sds(w_rg.shape, BF16), _sds(w_ig.shape, BF16), _sds((16, d))],
        input_output_aliases={12: 0}, sem=("parallel",),
    )(z, xc, g_f, g_b, h_f, h_b, conv_w, w_rg, b_rg, w_ig, b_ig, lam, dz)


def _tiles2d(s, d, tr, tcol):
    return (s // tr, d // tcol), pl.BlockSpec((tr, tcol), lambda i, j: (i, j))


def _zspec(tr, tcol, row_off, col_off):
    rb, cb = row_off // tr, col_off // tcol
    return pl.BlockSpec((tr, tcol), lambda i, j: (rb + i, cb + j))


def _rnn_gate_fwd(h_f, h_b, z, xg_off, n_ctx_rows, tr, tcol, side=None):
    t, d = h_f.shape
    s = t - n_ctx_rows
    grid, out = _tiles2d(s, d, tr, tcol)
    hs = _zspec(tr, tcol, n_ctx_rows, 0)

    def body(hf_ref, hb_ref, xg_ref, u_ref):
        u_ref[...] = ((hf_ref[...] + hb_ref[...]) * _gelu(xg_ref[...])).astype(BF16)

    return _call(body, side=side, name="rnn_gate_fwd", grid=grid,
                 in_specs=[hs, hs, _zspec(tr, tcol, n_ctx_rows, xg_off)], out_specs=out, out_shape=_sds((s, d), BF16),
                 sem=("parallel", "parallel"))(h_f, h_b, z)


def _rnn_gate_bwd(d_u, h_f, h_b, z, xg_off, dz, n_ctx_rows, tr, tcol, side=None):
    t, d = h_f.shape
    s = t - n_ctx_rows
    grid, out = _tiles2d(s, d, tr, tcol)
    hs = _zspec(tr, tcol, n_ctx_rows, 0)
    window = _zspec(tr, tcol, n_ctx_rows, xg_off)

    def body(du_ref, hf_ref, hb_ref, xg_ref, _, dr_ref, dxg_ref):
        du = du_ref[...]
        xg = xg_ref[...]
        dr_ref[...] = du * _gelu(xg)
        dxg_ref[...] = (du * (hf_ref[...] + hb_ref[...]) * _gelu_grad(xg)).astype(BF16)

    return _call(body, side=side, name="rnn_gate_bwd", grid=grid, in_specs=[out, hs, hs, window, ANY],
                 out_specs=[out, window], out_shape=[_sds((s, d)), _sds(dz.shape, BF16)], input_output_aliases={4: 1},
                 sem=("parallel", "parallel"))(d_u, h_f, h_b, z, dz)


def _merge_fwd(y_attn, y_rnn, z, gl_off, n_ctx_rows, tr, tcol):
    s, d = y_attn.shape
    grid, out = _tiles2d(s, d, tr, tcol)

    def body(ya_ref, yr_ref, ga_ref, gr_ref, o_ref):
        o_ref[...] = (jax.nn.sigmoid(ga_ref[...]) * ya_ref[...] + jax.nn.sigmoid(gr_ref[...]) * yr_ref[...]).astype(BF16)

    return pl.pallas_call(
        body, name="merge_fwd", grid=grid,
        in_specs=[out, out, _zspec(tr, tcol, n_ctx_rows, gl_off), _zspec(tr, tcol, n_ctx_rows, gl_off + d)],
        out_specs=out, out_shape=_sds((s, d), BF16), compiler_params=_params("parallel", "parallel"),
    )(y_attn, y_rnn, z, z)


def _merge_bwd(d_mrg, y, z, gl_off, dz, n_ctx_rows, tr, tcol, name):
    s, d = y.shape
    grid, out = _tiles2d(s, d, tr, tcol)
    window = _zspec(tr, tcol, n_ctx_rows, gl_off)

    def body(dm_ref, y_ref, gl_ref, _, dy_ref, dgl_ref):
        dm = dm_ref[...]
        g = jax.nn.sigmoid(gl_ref[...])
        dy_ref[...] = (dm * g).astype(BF16)
        dgl_ref[...] = (dm * y_ref[...] * g * (1.0 - g)).astype(BF16)

    return _call(body, name=name, grid=grid, in_specs=[out, out, window, ANY], out_specs=[out, window],
                 out_shape=[_sds((s, d), BF16), _sds(dz.shape, BF16)], input_output_aliases={3: 1},
                 sem=("parallel", "parallel"))(d_mrg, y, z, dz)


def _cast_into_window(w, chip, col_sharded, name):
    r, c = w.shape
    tr, tcol = _tile(r, 512, 16), _tile(c, WIDE_TILE, LANES)
    nrb, ncb = r // tr, c // tcol

    def body(chip_ref, w_ref, o_ref):
        o_ref[...] = w_ref[...].astype(BF16)

    if col_sharded:
        omap = lambda i, j, chip_ref: (i, chip_ref[0] * ncb + j)
    else:
        omap = lambda i, j, chip_ref: (chip_ref[0] * nrb + i, j)
    return pl.pallas_call(
        body, name=name,
        grid_spec=pltpu.PrefetchScalarGridSpec(
            num_scalar_prefetch=1, grid=(nrb, ncb),
            in_specs=[pl.BlockSpec((tr, tcol), lambda i, j, chip_ref: (i, j))], out_specs=pl.BlockSpec((tr, tcol), omap)),
        out_shape=_sds((r, c * N_CHIPS) if col_sharded else (r * N_CHIPS, c), BF16),
        compiler_params=_params("parallel", "parallel"),
    )(chip, w)


def _sum_leading(parts, name):
    n, r, c = parts.shape
    tr, tcol = _tile(r, 512, SUBLANES), _tile(c, 1024, LANES)

    def body(p_ref, o_ref):
        tot = p_ref[0]
        for k in range(1, n):
            tot = tot + p_ref[k]
        o_ref[...] = tot

    return pl.pallas_call(
        body, name=name, grid=(r // tr, c // tcol), in_specs=[pl.BlockSpec((n, tr, tcol), lambda i, j: (0, i, j))],
        out_specs=pl.BlockSpec((tr, tcol), lambda i, j: (i, j)), out_shape=_sds((r, c)),
        compiler_params=_params("parallel", "parallel"),
    )(parts)


def _add_half(full, other, core, split_rows, name):
    r, c = other.shape
    tr, tcol = _tile(r, 512, 16), _tile(c, 1024, LANES)
    nrb, ncb = r // tr, c // tcol

    def body(core_ref, f_ref, o_ref, out_ref):
        out_ref[...] = (f_ref[...].astype(F32) + o_ref[...].astype(F32)).astype(out_ref.dtype)

    if split_rows:
        fmap = lambda i, j, core_ref: (core_ref[0] * nrb + i, j)
    else:
        fmap = lambda i, j, core_ref: (i, core_ref[0] * ncb + j)
    same = lambda i, j, core_ref: (i, j)
    return pl.pallas_call(
        body, name=name,
        grid_spec=pltpu.PrefetchScalarGridSpec(
            num_scalar_prefetch=1, grid=(nrb, ncb),
            in_specs=[pl.BlockSpec((tr, tcol), fmap), pl.BlockSpec((tr, tcol), same)],
            out_specs=pl.BlockSpec((tr, tcol), same)),
        out_shape=_sds((r, c), BF16), compiler_params=_params("parallel", "parallel"),
    )(core, full, other)


def _add_pair(a, b, name):
    r, c = a.shape
    tr, tcol = _tile(r, 512, 16), _tile(c, WIDE_TILE, LANES)
    blk = pl.BlockSpec((tr, tcol), lambda i, j: (i, j))

    def body(a_ref, b_ref, o_ref):
        o_ref[...] = (a_ref[...].astype(F32) + b_ref[...].astype(F32)).astype(BF16)

    return pl.pallas_call(body, name=name, grid=(r // tr, c // tcol), in_specs=[blk, blk], out_specs=blk,
                          out_shape=_sds((r, c), BF16), compiler_params=_params("parallel", "parallel"))(a, b)


def _sum_regions(pair, got, place, col_sharded, name):
    n_got, r, c = got.shape
    tr, tcol = _tile(r, 512, 16), _tile(c, WIDE_TILE, LANES)
    nrb, ncb = r // tr, c // tcol

    def body(place_ref, p_ref, g_ref, out_ref):
        tot = p_ref[...].astype(F32)
        for k in range(n_got):
            tot = tot + g_ref[k].astype(F32)
        out_ref[...] = tot

    if col_sharded:
        pmap = lambda i, j, pr: (i, pr[0] * ncb + j)
        omap = lambda i, j, pr: (pr[1] * nrb + i, j)
        out_shape = (2 * r, c)
    else:
        pmap = lambda i, j, pr: (pr[0] * nrb + i, j)
        omap = lambda i, j, pr: (i, pr[1] * ncb + j)
        out_shape = (r, 2 * c)
    return pl.pallas_call(
        body, name=name,
        grid_spec=pltpu.PrefetchScalarGridSpec(
            num_scalar_prefetch=1, grid=(nrb, ncb),
            in_specs=[pl.BlockSpec((tr, tcol), pmap), pl.BlockSpec((n_got, tr, tcol), lambda i, j, pr: (0, i, j))],
            out_specs=pl.BlockSpec((tr, tcol), omap)),
        out_shape=_sds(out_shape), compiler_params=_params("parallel", "parallel"),
    )(place, pair, got)


def _fold_forwarded(pair, fold, nbr_chips, name):
    _, r, c = fold.shape
    tr, tcol = _tile(r, 512, 16), _tile(c, WIDE_TILE, LANES)
    nrb, ncb = r // tr, c // tcol

    def body(nbr_ref, px_ref, py_ref, f_ref, out_ref):
        out_ref[0] = (px_ref[...].astype(F32) + f_ref[1].astype(F32)).astype(out_ref.dtype)
        out_ref[1] = (py_ref[...].astype(F32) + f_ref[0].astype(F32)).astype(out_ref.dtype)

    both = pl.BlockSpec((2, tr, tcol), lambda i, j, nb: (0, i, j))
    return pl.pallas_call(
        body, name=name,
        grid_spec=pltpu.PrefetchScalarGridSpec(
            num_scalar_prefetch=1, grid=(nrb, ncb),
            in_specs=[pl.BlockSpec((tr, tcol), lambda i, j, nb: (i, nb[0] * ncb + j)),
                      pl.BlockSpec((tr, tcol), lambda i, j, nb: (nrb + i, nb[1] * ncb + j)), both],
            out_specs=both),
        out_shape=_sds(fold.shape, fold.dtype), compiler_params=_params("parallel", "parallel"),
    )(nbr_chips, pair, pair, fold)


def _adamw(w, g, m, v, name, copy_grad=True):
    r, c = w.shape
    tr, tcol = _tile(r, 256, SUBLANES), _tile(c, WIDE_TILE, LANES)
    blk = pl.BlockSpec((tr, tcol), lambda i, j: (i, j))
    n_out = 4 if copy_grad else 3

    def body(w_ref, g_ref, m_ref, v_ref, d_ref, nm_ref, nv_ref, *g_out_ref):
        g_ = g_ref[...]
        if copy_grad:
            g_out_ref[0][...] = g_
        m_ = ADAM_B1 * m_ref[...] + (1.0 - ADAM_B1) * g_
        v_ = ADAM_B2 * v_ref[...] + (1.0 - ADAM_B2) * (g_ * g_)
        m_hat = m_ / (1.0 - ADAM_B1 ** ADAM_STEP)
        v_hat = v_ / (1.0 - ADAM_B2 ** ADAM_STEP)
        d_ref[...] = -ADAM_LR * (m_hat / (jnp.sqrt(v_hat) + ADAM_EPS) + ADAM_WD * w_ref[...])
        nm_ref[...] = m_
        nv_ref[...] = v_

    return _call(body, name=name, grid=(r // tr, c // tcol), in_specs=[blk] * 4, out_specs=[blk] * n_out,
                 out_shape=[_sds((r, c))] * n_out, sem=("parallel", "parallel"))(w, g, m, v)


def _place():
    x, y, c = lax.axis_index("x"), lax.axis_index("y"), lax.axis_index("c")
    chips = [(1 - x, y), (x, 1 - y), (1 - x, 1 - y)]
    return x, y, c, chips


def _all_gather8(blk, name):
    m, n = blk.shape

    def body(x_ref, out_ref, send_sems, recv_sems, local_sem):
        x, y, c, chips = _place()
        me, sibling = (x, y, c), (x, y, 1 - c)

        def rows(px, py, pc):
            return out_ref.at[pl.ds((4 * px + 2 * py + pc) * m, m), :]

        def copy(k, block, to, src=None):
            return pltpu.make_async_remote_copy(
                src_ref=rows(*block) if src is None else src, dst_ref=rows(*block), send_sem=send_sems.at[k],
                recv_sem=recv_sems.at[k], device_id=to, device_id_type=MESH_ID)

        mine = pltpu.make_async_copy(x_ref, rows(*me), local_sem)
        mine.start()
        first = [copy(0, me, sibling, src=x_ref)]
        first += [copy(1 + j, me, (*chip, c), src=x_ref) for j, chip in enumerate(chips)]
        for cp in first:
            cp.start()
        passed = [copy(4 + j, (*chip, c), sibling) for j, chip in enumerate(chips)]
        for j, chip in enumerate(chips):
            copy(1 + j, (*chip, c), me).wait_recv()
            passed[j].start()
        copy(0, sibling, me).wait_recv()
        for j, chip in enumerate(chips):
            copy(4 + j, (*chip, 1 - c), me).wait_recv()
        for cp in first + passed:
            cp.wait_send()
        mine.wait()

    return pl.pallas_call(
        body, name=name, out_shape=_sds((N_DEV * m, n), blk.dtype), in_specs=[ANY], out_specs=ANY,
        scratch_shapes=[pltpu.SemaphoreType.DMA((7,)), pltpu.SemaphoreType.DMA((7,)), pltpu.SemaphoreType.DMA],
    )(blk)


def _half(ref, core, split_rows):
    r, c = ref.shape
    if split_rows:
        return ref.at[pl.ds(core * (r // 2), r // 2), :]
    return ref.at[:, pl.ds(core * (c // 2), c // 2)]


def _chip_block(ref, j, col_sharded):
    r, c = ref.shape
    if col_sharded:
        return ref.at[:, pl.ds(j * (c // N_CHIPS), c // N_CHIPS)]
    return ref.at[pl.ds(j * (r // N_CHIPS), r // N_CHIPS), :]


def _rows_part(ref, part):
    lo, hi, n = part
    r = ref.shape[0]
    return ref if (lo, hi) == (0, n) else ref.at[pl.ds(lo * (r // n), (hi - lo) * (r // n)), :]


def _copy(send_sems, recv_sems, k, src, dst, to):
    return pltpu.make_async_remote_copy(src_ref=src, dst_ref=dst, send_sem=send_sems.at[k], recv_sem=recv_sems.at[k],
                                        device_id=to, device_id_type=MESH_ID)


def _in_place(arrays):
    return tuple(arrays), tuple(_sds(a.shape, a.dtype) for a in arrays), {i: i for i in range(len(arrays))}


def _gather_ici(fulls, col_sharded, part=(0, 1, 1)):
    nw = len(fulls)

    def build(_, refs, send_sems, recv_sems, sem0):
        x, y, c, chips = _place()
        sends, recvs = [], []
        for w in range(nw):
            win = lambda j: _rows_part(_half(_chip_block(refs[w], j, col_sharded[w]), c, True), part)
            for k, (cx, cy) in enumerate(chips):
                sem = sem0 + 3 * w + k
                sends.append(_copy(send_sems, recv_sems, sem, win(2 * x + y), win(2 * x + y), (cx, cy, c)))
                recvs.append(_copy(send_sems, recv_sems, sem, win(2 * cx + cy), win(2 * cx + cy), (cx, cy, c)))
        return sends, recvs

    return _Side(*_in_place(fulls), 3 * nw, build)


def _gather_d2d(fulls, col_sharded):
    nw = len(fulls)

    def build(_, refs, send_sems, recv_sems, sem0):
        x, y, c, chips = _place()
        sends, recvs = [], []
        for w in range(nw):
            win = lambda j, core: _half(_chip_block(refs[w], j, col_sharded[w]), core, True)
            for k, (cx, cy) in enumerate(chips):
                sem = sem0 + 3 * w + k
                sends.append(_copy(send_sems, recv_sems, sem, win(2 * cx + cy, c), win(2 * cx + cy, c), (x, y, 1 - c)))
                recvs.append(_copy(send_sems, recv_sems, sem, win(2 * cx + cy, 1 - c), win(2 * cx + cy, 1 - c),
                                   (x, y, 1 - c)))
        return sends, recvs

    return _Side(*_in_place(fulls), 3 * nw, build)


def _gather_neighbours(fulls, col_sharded, part=(0, 1, 1)):
    nw = len(fulls)

    def build(_, refs, send_sems, recv_sems, sem0):
        x, y, c, chips = _place()
        sends, recvs = [], []
        for w in range(nw):
            win = lambda j: _rows_part(_half(_chip_block(refs[w], j, col_sharded[w]), c, True), part)
            for k, (cx, cy) in enumerate(chips[:2]):
                sem = sem0 + 2 * w + k
                sends.append(_copy(send_sems, recv_sems, sem, win(2 * x + y), win(2 * x + y), (cx, cy, c)))
                recvs.append(_copy(send_sems, recv_sems, sem, win(2 * cx + cy), win(2 * cx + cy), (cx, cy, c)))
        return sends, recvs

    return _Side(*_in_place(fulls), 2 * nw, build)


def _gather_forward(fulls, col_sharded, ways=(True, True), siblings=False):
    nw = len(fulls)

    def build(_, refs, send_sems, recv_sems, sem0):
        x, y, c, (cx_, cy_, cd_) = _place()
        sends, recvs = [], []
        for w in range(nw):
            win = lambda chip, core: _half(_chip_block(refs[w], 2 * chip[0] + chip[1], col_sharded[w]), core, True)
            part = lambda ref, p: _rows_part(ref, (p, p + 1, 2))
            for p, (src, to) in enumerate(((cx_, cy_), (cy_, cx_))):
                if ways[p]:
                    sem = sem0 + 4 * w + p
                    sends.append(_copy(send_sems, recv_sems, sem, part(win(src, c), p), part(win(src, c), p), (*to, c)))
                    recvs.append(_copy(send_sems, recv_sems, sem, part(win(cd_, c), p), part(win(cd_, c), p), (*to, c)))
            if siblings:
                for k, chip in enumerate((cx_, cy_)):
                    sem = sem0 + 4 * w + 2 + k
                    sends.append(_copy(send_sems, recv_sems, sem, win(chip, c), win(chip, c), (x, y, 1 - c)))
                    recvs.append(_copy(send_sems, recv_sems, sem, win(chip, 1 - c), win(chip, 1 - c), (x, y, 1 - c)))
        return sends, recvs

    return _Side(*_in_place(fulls), 4 * nw, build)


def _gather_diagonal_d2d(full, col_sharded):
    def build(_, refs, send_sems, recv_sems, sem0):
        x, y, c, chips = _place()
        cx, cy = chips[2]
        win = lambda core: _half(_chip_block(refs[0], 2 * cx + cy, col_sharded), core, True)
        return ([_copy(send_sems, recv_sems, sem0, win(c), win(c), (x, y, 1 - c))],
                [_copy(send_sems, recv_sems, sem0, win(1 - c), win(1 - c), (x, y, 1 - c))])

    return _Side(*_in_place([full]), 1, build)


def _scatter_forward_1(pair):
    r, n = pair.shape
    results = (_sds((2, r, n // N_CHIPS), pair.dtype), _sds((2, r // 2, n // N_CHIPS), pair.dtype))

    def build(refs, res, send_sems, recv_sems, sem0):
        x, y, c, (cx_, cy_, cd_) = _place()
        region = lambda chip, p: _rows_part(_chip_block(refs[0], 2 * chip[0] + chip[1], True), (p, p + 1, 2))
        got, fold = res
        copies = [
            _copy(send_sems, recv_sems, sem0, region(cx_, 1), _rows_part(got.at[0], (1, 2, 2)), (*cx_, c)),
            _copy(send_sems, recv_sems, sem0 + 1, region(cd_, 1), fold.at[0], (*cx_, c)),
            _copy(send_sems, recv_sems, sem0 + 2, region(cy_, 0), _rows_part(got.at[1], (0, 1, 2)), (*cy_, c)),
            _copy(send_sems, recv_sems, sem0 + 3, region(cd_, 0), fold.at[1], (*cy_, c))]
        return copies, copies

    return _Side((pair,), results, {}, 4, build)


def _scatter_forward_2(passed, got):
    def build(refs, res, send_sems, recv_sems, sem0):
        x, y, c, (cx_, cy_, _) = _place()
        copies = [_copy(send_sems, recv_sems, sem0, refs[0].at[0], _rows_part(res[0].at[0], (0, 1, 2)), (*cx_, c)),
                  _copy(send_sems, recv_sems, sem0 + 1, refs[0].at[1], _rows_part(res[0].at[1], (1, 2, 2)), (*cy_, c))]
        return copies, copies

    return _Side((passed, got), (_sds(got.shape, got.dtype),), {1: 0}, 2, build)


def _exchange(side, name):
    return _call(None, side=side, name=name)()[1]


def _swap_halves(grads, col_sharded):
    nw = len(grads)
    out_shapes = [_sds((g.shape[0] // 2, g.shape[1]) if col else (g.shape[0], g.shape[1] // 2), g.dtype)
                  for g, col in zip(grads, col_sharded)]

    def build(g_refs, o_refs, send_sems, recv_sems, sem0):
        x, y, c, _ = _place()
        copies = [_copy(send_sems, recv_sems, sem0 + w, _half(g_refs[w], 1 - c, col_sharded[w]), o_refs[w],
                        (x, y, 1 - c)) for w in range(nw)]
        return copies, copies

    return _Side(tuple(grads), tuple(out_shapes), {}, nw, build)


def _send_to_sibling(arrays):
    nw = len(arrays)

    def build(refs, o_refs, send_sems, recv_sems, sem0):
        x, y, c, _ = _place()
        copies = [_copy(send_sems, recv_sems, sem0 + w, refs[w], o_refs[w], (x, y, 1 - c)) for w in range(nw)]
        return copies, copies

    return _Side(tuple(arrays), tuple(_sds(a.shape, a.dtype) for a in arrays), {}, nw, build)


def _scatter_regions(pairs, col_sharded, part=(0, 1, 1), into=None):
    nw = len(pairs)

    def region_shape(p, col):
        return (p.shape[0], p.shape[1] // N_CHIPS) if col else (p.shape[0] // N_CHIPS, p.shape[1])

    out_shapes = tuple(_sds((N_CHIPS - 1, *region_shape(p, col)), p.dtype) for p, col in zip(pairs, col_sharded))

    def build(refs, o_refs, send_sems, recv_sems, sem0):
        x, y, c, chips = _place()
        copies = []
        for w in range(nw):
            for k, (cx, cy) in enumerate(chips):
                copies.append(_copy(
                    send_sems, recv_sems, sem0 + 3 * w + k,
                    _rows_part(_chip_block(refs[w], 2 * cx + cy, col_sharded[w]), part),
                    _rows_part(o_refs[w].at[k], part), (cx, cy, c)))
        return copies, copies

    if into is None:
        return _Side(tuple(pairs), out_shapes, {}, 3 * nw, build)
    return _Side((*pairs, *into), out_shapes, {nw + w: w for w in range(nw)}, 3 * nw, build)


def _join_halves(halves, col_sharded):
    nw = len(halves)

    def build(_, refs, send_sems, recv_sems, sem0):
        x, y, c, _ = _place()
        sends, recvs = [], []
        for w in range(nw):
            mine, theirs = _half(refs[w], c, col_sharded[w]), _half(refs[w], 1 - c, col_sharded[w])
            sends.append(_copy(send_sems, recv_sems, sem0 + w, mine, mine, (x, y, 1 - c)))
            recvs.append(_copy(send_sems, recv_sems, sem0 + w, theirs, theirs, (x, y, 1 - c)))
        return sends, recvs

    return _Side(*_in_place(halves), nw, build)


def _part_rows(size):
    return -(-size // (SUBLANES * LANES)) * SUBLANES


def _pack(arrays, pad_rows_to=SUBLANES):
    flat = [jnp.pad(a.reshape(-1), (0, _part_rows(a.size) * LANES - a.size)).reshape(-1, LANES) for a in arrays]
    rows = sum(f.shape[0] for f in flat)
    pad = (-rows) % pad_rows_to
    if pad:
        flat.append(jnp.zeros((pad, LANES), F32))
    return jnp.concatenate(flat, axis=0)


def _unpack(packed, shapes):
    out, r = [], 0
    for shp in shapes:
        size = math.prod(shp)
        out.append(packed[r:r + _part_rows(size)].reshape(-1)[:size].reshape(shp))
        r += _part_rows(size)
    return out


def _rope_tables(n_ctx_rows, s):
    rows = s // GRID_W
    row_idx = jnp.repeat(jnp.arange(rows), GRID_W)
    col_idx = jnp.tile(jnp.arange(GRID_W), rows)
    n_freq = LANES // 4
    inv_freq = ROPE_THETA ** (-jnp.arange(n_freq, dtype=F32) / n_freq)
    ang = jnp.concatenate([row_idx.astype(F32)[:, None] * inv_freq, col_idx.astype(F32)[:, None] * inv_freq], axis=-1)
    cos = jnp.repeat(jnp.cos(ang), 2, axis=-1)
    sin = jnp.repeat(jnp.sin(ang), 2, axis=-1) * jnp.tile(jnp.array([-1.0, 1.0], F32), LANES // 2)
    cos = jnp.concatenate([jnp.ones((n_ctx_rows, LANES), F32), cos], axis=0)
    sin = jnp.concatenate([jnp.zeros((n_ctx_rows, LANES), F32), sin], axis=0)
    return cos, sin


WEIGHT_NAMES = ['c_ctx', 'w_mod', 'b_mod', 'g_mix', 'g_mlp', 'w_in', 'q_gain', 'k_gain', 'conv_w', 'conv_b', 'w_rg',
                'b_rg', 'w_ig', 'b_ig', 'lru_lambda', 'w_o_attn', 'w_o_rnn', 'w_out', 'w_up', 'w_down', 'g_final']
BIG = ['w_in', 'w_o_attn', 'w_o_rnn', 'w_out', 'w_up', 'w_down']
BIG_COL_SHARDED = [True, False, False, False, True, False]
GATES = ['w_rg', 'w_ig']
SMALL = ['c_ctx', 'b_mod', 'g_mix', 'g_mlp', 'q_gain', 'k_gain', 'conv_b', 'g_final',
         'conv_w', 'b_rg', 'b_ig', 'lru_lambda']


def kernel(x, c, ctx, c_ctx, w_mod, b_mod, g_mix, g_mlp, w_in, q_gain, k_gain, conv_w, conv_b, w_rg, b_rg, w_ig, b_ig, lru_lambda, w_o_attn, w_o_rnn, w_out, w_up, w_down, g_final, loss_target, m_c_ctx, m_w_mod, m_b_mod, m_g_mix, m_g_mlp, m_w_in, m_q_gain, m_k_gain, m_conv_w, m_conv_b, m_w_rg, m_b_rg, m_w_ig, m_b_ig, m_lru_lambda, m_w_o_attn, m_w_o_rnn, m_w_out, m_w_up, m_w_down, m_g_final, v_c_ctx, v_w_mod, v_b_mod, v_g_mix, v_g_mlp, v_w_in, v_q_gain, v_k_gain, v_conv_w, v_conv_b, v_w_rg, v_b_rg, v_w_ig, v_b_ig, v_lru_lambda, v_w_o_attn, v_w_o_rnn, v_w_out, v_w_up, v_w_down, v_g_final):
    given = dict(locals())
    weights = {n: given[n] for n in WEIGHT_NAMES}
    moms = {n: given["m_" + n] for n in WEIGHT_NAMES}
    vars_ = {n: given["v_" + n] for n in WEIGHT_NAMES}

    s, d = x.shape[1], x.shape[2]
    n_ctx = ctx.shape[1]
    t = n_ctx + s
    hd = q_gain.shape[1]
    assert hd == LANES and w_rg.shape[-1] == LANES
    attn_w = w_o_attn.shape[1] * N_CHIPS
    n_in = w_in.shape[2] * N_CHIPS
    kv_w = (n_in - attn_w - 4 * d) // 2
    group = attn_w // kv_w
    k_off, v_off, xr_off = attn_w, attn_w + kv_w, attn_w + 2 * kv_w
    xg_off, gl_off = xr_off + d, xr_off + 2 * d
    d_mod = N_MOD * d
    tr = _tile(math.gcd(n_ctx, s), 256, 16)
    tcol = _tile(math.gcd(d, xr_off), 1024, LANES)
    xi, yi, ci = lax.axis_index("x"), lax.axis_index("y"), lax.axis_index("c")
    chip = 2 * xi + yi
    core = ci.astype(jnp.int32).reshape(1)

    sharded_small = [conv_w[0], b_rg[0], b_ig[0], lru_lambda[0]]
    pack0 = _pack([c[0]] + sharded_small)
    got0 = _all_gather8(pack0, "gather_small_inputs").reshape(N_DEV, -1, LANES)
    c_all = got0[:, :_part_rows(d)].reshape(N_DEV, -1)[:, :d]
    per_chip = [_unpack(got0[2 * j, _part_rows(d):], [a.shape for a in sharded_small]) for j in range(N_CHIPS)]
    conv_w_f, b_rg_f, b_ig_f, lam_f = (jnp.concatenate([per_chip[j][i] for j in range(N_CHIPS)], axis=-1)
                                       for i in range(4))
    c16 = jnp.concatenate([c_all, c_ctx[None, :], jnp.zeros((16 - N_DEV - 1, d), F32)], axis=0)
    b_mod_shard = lax.dynamic_slice(b_mod, (0, chip * (d_mod // N_CHIPS)), (1, d_mod // N_CHIPS))
    mod_part, silu16 = _mod_fwd(c16, w_mod[0], b_mod_shard)
    mod_all = _all_gather8(mod_part, "gather_mod").reshape(N_DEV, 16, d_mod // N_CHIPS)
    mod16 = jnp.concatenate([mod_all[2 * j] for j in range(N_CHIPS)], axis=-1)
    me = 4 * xi + 2 * yi + ci
    mod_lat = lax.dynamic_slice(mod16, (me, 0), (1, d_mod)).reshape(N_MOD, d)
    mod_ctx = mod16[N_DEV].reshape(N_MOD, d)
    mod4 = jnp.stack([mod_ctx[0], mod_ctx[1], mod_lat[0], mod_lat[1]])
    mod3 = jnp.stack([mod_lat[2], mod_lat[3], mod_lat[4]])
    gate_f = mod_lat[5][None, :]

    chip_arr = chip.astype(jnp.int32).reshape(1)
    own = {n: _cast_into_window(weights[n][0], chip_arr, col, "cast_" + n) for n, col in zip(BIG, BIG_COL_SHARDED)}
    place = jnp.stack([chip, ci]).astype(jnp.int32)
    row3 = [False] * 3

    def pair_sum(n, full, other, col):
        return _add_half(full, other, core, col, "pair_sum_" + n)

    def chip_sum(n, pair, got, col):
        return _sum_regions(pair, got, place, col, "chip_sum_" + n)

    (w_in_f,) = _exchange(_gather_neighbours([own['w_in']], [True]), "gather_w_in_nbr")
    (w_in_f,) = _exchange(_gather_forward([w_in_f], [True], siblings=True), "gather_w_in_fwd")
    (w_in_f,) = _exchange(_gather_diagonal_d2d(w_in_f, True), "gather_w_in_diag")
    cos, sin = _rope_tables(n_ctx, s)
    h = _norm_mod_fwd(ctx[0], x[0], g_mix, mod4, tr)
    z, (w_oa_f, w_or_f, w_out_f, w_up_f) = _matmul(h, w_in_f, name="mm_in", side=_sides(
        _gather_neighbours([own['w_o_attn'], own['w_o_rnn'], own['w_out']], row3),
        _gather_neighbours([own['w_up']], [True], (0, 1, 2))))
    qr, (w_oa_f, w_or_f, w_out_f) = _head_prep_fwd(z, 0, attn_w // LANES, q_gain, cos, sin, tr, "q_prep",
                                                   side=_gather_forward([w_oa_f, w_or_f, w_out_f], row3))
    kr = _head_prep_fwd(z, k_off, kv_w // LANES, k_gain, cos, sin, tr, "k_prep")
    (attn_o, lse), (w_up_f, w_down_f, w_oa_f, w_or_f, w_out_f) = _attn_fwd(
        qr, kr, z, v_off, n_ctx, group, tr, side=_sides(
            _on_same(_gather_neighbours([w_up_f], [True], (1, 2, 2)), _gather_forward([w_up_f], [True], ways=(True, False))),
            _gather_neighbours([own['w_down']], [False], (0, 1, 2)), _gather_d2d([w_oa_f, w_or_f, w_out_f], row3)))
    (xc, a_f, bx_f, a_b, bx_b), (w_up_f, w_down_f) = _rnn_prep(
        z, xr_off, conv_w_f, conv_b, w_rg[0], b_rg_f, w_ig[0], b_ig_f, lam_f, n_ctx,
        side=_sides(_gather_forward([w_up_f], [True], ways=(False, True)),
                    _gather_neighbours([w_down_f], [False], (1, 2, 2))))
    (h_f, h_b), (w_up_f, w_down_f) = _scan(
        [(a_f, bx_f, "ctx_lat_up"), (a_b, bx_b, "ctx_lat_down")], post=False, n_ctx_rows=n_ctx, name="scan_fwd",
        side=_sides(_gather_d2d([w_up_f], [True]), _gather_forward([w_down_f], [False])))
    u = _rnn_gate_fwd(h_f, h_b, z, xg_off, n_ctx, tr, tcol)
    y_attn = _matmul(attn_o, w_oa_f, name="mm_o_attn")
    y_rnn = _matmul(u, w_or_f, name="mm_o_rnn")
    mrg = _merge_fwd(y_attn, y_rnn, z, gl_off, n_ctx, tr, tcol)
    mix = _matmul(mrg, w_out_f, name="mm_out")
    x1, h2 = _resid_norm_mod_fwd(x[0], mix, g_mlp, mod3, tr)
    (up, act), (w_down_f,) = _matmul(
        h2, w_up_f, name="mm_up", out_dtype=(F32, BF16), side=_gather_d2d([w_down_f], [False]),
        post=lambda u: (u, jnp.square(jnp.maximum(u, 0.0))))
    down = _matmul(act, w_down_f, name="mm_down")
    dx2, d_down, sums_fin, loss_blk = _final_fwd_bwd(x1, down, loss_target[0], g_final[None, :], gate_f, tr)

    d_up = _matmul(d_down, w_down_f, tb=True, out_dtype=BF16, name="mm_d_up",
                   post=lambda d_act, up_: d_act * 2.0 * jnp.maximum(up_, 0.0), post_args=(up,))
    g_w_down = _matmul(act, d_down, ta=True, out_dtype=BF16, name="mm_g_down")

    def scatter(pairs, cols, lo, hi, into=None):
        return _scatter_regions(pairs, cols, (lo, hi, 8), into)

    dh2, (got,) = _matmul(d_up, w_up_f, tb=True, name="mm_d_h2", side=_swap_halves([g_w_down], [False]))
    p_down = pair_sum('w_down', g_w_down, got, False)
    g_w_up, got_down = _matmul(h2, d_up, ta=True, out_dtype=BF16, name="mm_g_up", side=scatter([p_down], [False], 0, 3))
    (dx1, d_mix, sums2), (got, *got_down) = _norm_mod_bwd2(x1, dh2, dx2, mix, g_mlp, mod3, tr, side=_sides(
        _swap_halves([g_w_up], [True]), scatter([p_down], [False], 3, 4, got_down)))
    p_up = pair_sum('w_up', g_w_up, got, True)
    d_mrg = _matmul(d_mix, w_out_f, tb=True, name="mm_d_mrg")
    g_w_out = _matmul(mrg, d_mix, ta=True, out_dtype=BF16, name="mm_g_out")
    dz = _dz_start(t, n_in, n_ctx)
    d_ya, dz = _merge_bwd(d_mrg, y_attn, z, gl_off, dz, n_ctx, tr, tcol, "merge_bwd_attn")
    d_yr, dz = _merge_bwd(d_mrg, y_rnn, z, gl_off + d, dz, n_ctx, tr, tcol, "merge_bwd_rnn")
    d_o = _matmul(d_ya, w_oa_f, tb=True, out_dtype=BF16, name="mm_d_o")
    g_w_oa = _matmul(attn_o, d_ya, ta=True, out_dtype=BF16, name="mm_g_o_attn")
    d_u = _matmul(d_yr, w_or_f, tb=True, name="mm_d_u")
    g_w_or = _matmul(u, d_yr, ta=True, out_dtype=BF16, name="mm_g_o_rnn")
    d_rnn, dz = _rnn_gate_bwd(d_u, h_f, h_b, z, xg_off, dz, n_ctx, tr, tcol)
    gs_f, gs_b = _scan([(a_f, d_rnn, "lat_ctx_down"), (a_b, d_rnn, "lat_ctx_up")], post=True, n_ctx_rows=n_ctx,
                       name="scan_bwd")
    o_names, o_grads = ['w_o_attn', 'w_o_rnn', 'w_out'], [g_w_oa, g_w_or, g_w_out]
    (dz, g_w_rg, g_w_ig, sums_rnn), (got_down, *got_o) = _rnn_bwd(
        z, xr_off, xc, gs_f, gs_b, h_f, h_b, conv_w_f, w_rg[0], b_rg_f, w_ig[0], b_ig_f, lam_f, dz, n_ctx,
        side=_sides(scatter([p_down], [False], 4, 8, got_down), _swap_halves(o_grads, row3)))
    hs_down = chip_sum('w_down', p_down, got_down, False)
    p_o = [pair_sum(n, g, o, False) for n, g, o in zip(o_names, o_grads, got_o)]
    gate_cols = 8 * LANES if g_w_rg.size % (8 * LANES * N_CHIPS * 16) == 0 else 2 * LANES
    gate_rows = g_w_rg.size // gate_cols
    gate_grads = [g_w_rg.reshape(gate_rows, gate_cols), g_w_ig.reshape(gate_rows, gate_cols)]
    (dq, dk, dz), (got_up, got_oa, got_rg, got_ig, gs_down) = _attn_bwd(
        qr, kr, z, v_off, d_o, attn_o, lse, dz, n_ctx, group, tr, side=_sides(
            _scatter_regions([p_up], [True]), _scatter_regions(p_o[:1], row3[:1]), _swap_halves(gate_grads, row3[:2]),
            _join_halves([hs_down], [False])))
    hs_up = chip_sum('w_up', p_up, got_up, True)
    hs_oa = chip_sum('w_o_attn', p_o[0], got_oa, False)
    p_gate = [pair_sum(n, g, o, False) for n, g, o in zip(GATES, gate_grads, (got_rg, got_ig))]
    dz, g_q_gain = _head_prep_bwd(z, 0, attn_w // LANES, q_gain, cos, sin, dq, n_ctx, tr, dz, "q_prep_bwd")
    dz, g_k_gain = _head_prep_bwd(z, k_off, kv_w // LANES, k_gain, cos, sin, dk, 0, tr, dz, "k_prep_bwd")
    half = d // 2
    h_sibling = lax.dynamic_slice(h, (0, (1 - ci) * half), (t, half))
    h_own = lax.dynamic_slice(h, (0, ci * half), (t, half))
    late = p_o[1:] + p_gate
    g_sibling, (got_rg, got_ig, *got_late) = _matmul(
        h_sibling, dz, ta=True, out_dtype=BF16, name="mm_g_in_sibling", side=_sides(
            _scatter_regions(p_gate, row3[:2]), scatter(p_o[1:], row3[:2], 0, 4)))
    g_own, (got_or, got_out, got, gs_up, gs_oa) = _matmul(
        h_own, dz, ta=True, out_dtype=BF16, name="mm_g_in_own", side=_sides(
            scatter(p_o[1:], row3[:2], 4, 8, got_late), _send_to_sibling([g_sibling]), _join_halves([hs_up], [True]),
            _join_halves([hs_oa], [False])))
    hs_late = [chip_sum(n, p, o, False) for n, p, o in zip(o_names[1:] + GATES, late, (got_or, got_out, got_rg, got_ig))]
    p_in = _add_pair(g_own, got, "pair_sum_w_in")
    dh, (got_in, fold_in, gs_or, gs_out) = _matmul(dz, w_in_f, tb=True, name="mm_d_h", side=_sides(
        _scatter_forward_1(p_in), _join_halves(hs_late[:2], row3[:2])))
    nbr_chips = jnp.stack([2 * (1 - xi) + yi, 2 * xi + 1 - yi]).astype(jnp.int32)
    passed = _fold_forwarded(p_in, fold_in, nbr_chips, "fold_w_in")
    (got_in,) = _exchange(_scatter_forward_2(passed, got_in), "scatter_w_in_fwd")
    grad_x, sums1 = _norm_mod_bwd1(ctx[0], x[0], dh, dx1, g_mix, mod4, tr)

    zeros_d = jnp.zeros((d,), F32)
    dmod_lat = jnp.concatenate([sums1[0], sums1[1], sums2[3], sums2[0], sums2[1], sums_fin[1]])
    dmod_ctx = jnp.concatenate([sums1[3], sums1[4]] + [zeros_d] * 4)
    small_parts = [dmod_lat, dmod_ctx, loss_blk[0, 0:1], sums1[2] + sums1[5], sums2[2], g_q_gain[0], g_k_gain[0],
                   sums_rnn[10], sums_fin[0], sums_rnn[6:10], sums_rnn[0:2], sums_rnn[2:4], sums_rnn[4:6]]
    pack1 = _pack(small_parts)
    got1 = _all_gather8(pack1, "gather_small_grads").reshape(N_DEV, -1, LANES)
    tot1 = _sum_leading(got1, "sum_small_grads")
    part_shapes = [a.shape for a in small_parts]
    (s_dmod_lat, s_dmod_ctx, s_loss, g_g_mix, g_g_mlp, g_q_gain, g_k_gain, g_conv_b, g_g_final,
     g_conv_w_f, g_b_rg_f, g_b_ig_f, g_lam_f) = _unpack(tot1, part_shapes)
    loss = s_loss[0]
    g_b_mod = (s_dmod_lat + s_dmod_ctx)[None, :]
    n_mod_rows = _part_rows(d_mod)
    dmod16 = jnp.concatenate([got1[:, :n_mod_rows].reshape(N_DEV, -1)[:, :d_mod], s_dmod_ctx[None, :],
                              jnp.zeros((16 - N_DEV - 1, d_mod), F32)], axis=0)
    dmod16_shard = lax.dynamic_slice(dmod16, (0, chip * (d_mod // N_CHIPS)), (16, d_mod // N_CHIPS))
    g_w_mod = _matmul(silu16, dmod16_shard, ta=True, name="mm_g_mod")
    dsilu_part = _matmul(dmod16_shard[N_DEV:], w_mod[0], tb=True, name="mm_d_silu")
    dsilu_all = _all_gather8(dsilu_part, "gather_d_silu").reshape(N_DEV, 8, d)
    g_c_ctx = _c_ctx_grad(dsilu_all, c_ctx[None, :])[0]

    def shard_of(full):
        w = full.shape[-1] // N_CHIPS
        return lax.dynamic_slice(full, (0, chip * w), (full.shape[0], w))

    grads = {
        'c_ctx': g_c_ctx, 'b_mod': g_b_mod, 'g_mix': g_g_mix[None, :], 'g_mlp': g_g_mlp[None, :],
        'q_gain': g_q_gain[None, :], 'k_gain': g_k_gain[None, :], 'conv_b': g_conv_b[None, :],
        'g_final': g_g_final,
        'conv_w': shard_of(g_conv_w_f)[None], 'b_rg': shard_of(g_b_rg_f)[None], 'b_ig': shard_of(g_b_ig_f)[None],
        'lru_lambda': shard_of(g_lam_f)[None], 'w_mod': g_w_mod[None],
    }

    delta, new_m, new_v = {}, {}, {}

    def adamw(n):
        shp = weights[n].shape
        as2d = (lambda a: a[0]) if n not in GATES else (lambda a: a.reshape(-1, LANES))
        dl, nm, nv, *g = _adamw(as2d(weights[n]), as2d(grads[n]), as2d(moms[n]), as2d(vars_[n]), "adamw_" + n,
                                copy_grad=n in BIG)
        delta[n], new_m[n], new_v[n] = dl.reshape(shp), nm.reshape(shp), nv.reshape(shp)
        if g:
            grads[n] = g[0].reshape(shp)

    hs_in = chip_sum('w_in', p_in, got_in, True)
    (gs_in,) = _exchange(_join_halves([hs_in], [True]), "join_w_in")
    for n, g in zip(['w_in', 'w_o_attn', 'w_o_rnn', 'w_out', 'w_up', 'w_down'], [gs_in, gs_oa, gs_or, gs_out, gs_up, gs_down]):
        grads[n] = g[None]
    half_cols = gate_cols // 2
    mine = jnp.concatenate([lax.dynamic_slice(hs, (0, ci * half_cols), (gate_rows // N_CHIPS, half_cols))
                            for hs in hs_late[2:]], axis=0)
    gate_all = _all_gather8(mine, "gather_gate_grads")
    gate_all = gate_all.reshape(N_CHIPS, 2, len(GATES), gate_rows // N_CHIPS, half_cols)
    for i, n in enumerate(GATES):
        grads[n] = jnp.moveaxis(gate_all[:, :, i], 1, 2).reshape(weights[n].shape)
    for n in ['w_mod'] + BIG + GATES:
        adamw(n)
    small_shapes = [weights[n].shape for n in SMALL]
    packed = [_pack([src[n] for n in SMALL], 512) for src in (weights, grads, moms, vars_)]
    outs = _adamw(*packed, "adamw_small", copy_grad=False)
    for res, out in zip((delta, new_m, new_v), outs):
        for n, a in zip(SMALL, _unpack(out, small_shapes)):
            res[n] = a
    return (loss, grad_x[None], *[grads[n] for n in WEIGHT_NAMES], *[delta[n] for n in WEIGHT_NAMES],
            *[new_m[n] for n in WEIGHT_NAMES], *[new_v[n] for n in WEIGHT_NAMES])
```

```python
import functools
import math
from typing import Callable, NamedTuple

import jax
import jax.numpy as jnp
from jax import lax
from jax.experimental import pallas as pl
from jax.experimental.pallas import tpu as pltpu

F32 = jnp.float32
BF16 = jnp.bfloat16
MESH_ID = pl.DeviceIdType.MESH
ANY = pl.BlockSpec(memory_space=pl.ANY)

NORM_EPS = 1e-6
LRU_C = 8.0
GRID_W = 64
ROPE_THETA = 10000.0
N_MOD = 6
CONV_WIDTH = 4
ADAM_LR = 0.001
ADAM_B1 = 0.9
ADAM_B2 = 0.999
ADAM_EPS = 1e-08
ADAM_WD = 0.01
ADAM_STEP = 10

LANES = 128
SUBLANES = 8
V7X_VMEM_LIMIT = 48 * 1024 * 1024
WIDE_TILE = 11 * LANES
N_CHIPS = 4
N_DEV = 8
GELU_C = math.sqrt(2.0 / math.pi)
GELU_A = 0.044715


def _tile(dim, pref, align):
    t = min(pref, dim)
    t -= t % align
    while t >= align:
        if dim % t == 0:
            return t
        t -= align
    return dim


def _params(*sem):
    return pltpu.CompilerParams(dimension_semantics=sem, vmem_limit_bytes=V7X_VMEM_LIMIT)


def _sds(shape, dtype=F32):
    return jax.ShapeDtypeStruct(shape, dtype)


class _Side(NamedTuple):
    operands: tuple
    results: tuple
    aliases: dict
    n_sems: int
    build: Callable


def _sides(*sides):
    ops, res, aliases, spans, n = [], [], {}, [], 0
    for s in sides:
        spans.append((len(ops), len(res), n))
        aliases.update({len(ops) + i: len(res) + j for i, j in s.aliases.items()})
        ops += s.operands
        res += s.results
        n += s.n_sems

    def build(op_refs, res_refs, send_sems, recv_sems, sem0):
        sends, recvs = [], []
        for s, (o, r, k) in zip(sides, spans):
            a, b = s.build(op_refs[o:o + len(s.operands)], res_refs[r:r + len(s.results)], send_sems, recv_sems,
                           sem0 + k)
            sends += a
            recvs += b
        return sends, recvs

    return _Side(tuple(ops), tuple(res), aliases, n, build)


def _on_same(*sides):
    def build(ops, res, send_sems, recv_sems, sem0):
        sends, recvs = [], []
        for s in sides:
            a, b = s.build(ops, res, send_sems, recv_sems, sem0)
            sends += a
            recvs += b
            sem0 += s.n_sems
        return sends, recvs

    return _Side(sides[0].operands, sides[0].results, sides[0].aliases, sum(s.n_sems for s in sides), build)


def _call(body, *, side=None, sem=(), grid=(), in_specs=(), out_specs=(), out_shape=(), scratch_shapes=(), **kw):
    if side is None:
        return pl.pallas_call(body, grid=grid, in_specs=list(in_specs), out_specs=out_specs, out_shape=out_shape,
                              scratch_shapes=list(scratch_shapes), compiler_params=_params(*sem), **kw)
    aliases = kw.pop("input_output_aliases", {})
    many = isinstance(out_shape, (list, tuple))
    out_specs_l, out_shape_l = (list(out_specs), list(out_shape)) if many else ([out_specs], [out_shape])
    n_in, n_out, n_scr = len(in_specs), len(out_shape_l), len(scratch_shapes)
    n_op, n_res = len(side.operands), len(side.results)

    def hosted(*refs):
        ins, ops = refs[:n_in], refs[n_in:n_in + n_op]
        outs = refs[n_in + n_op:n_in + n_op + n_out]
        res = refs[n_in + n_op + n_out:n_in + n_op + n_out + n_res]
        scr = refs[n_in + n_op + n_out + n_res:-2]
        send_sems, recv_sems = refs[-2:]

        def start():
            for cp in side.build(ops, res, send_sems, recv_sems, 0)[0]:
                cp.start()

        def finish():
            sends, recvs = side.build(ops, res, send_sems, recv_sems, 0)
            for cp in recvs:
                cp.wait_recv()
            for cp in sends:
                cp.wait_send()

        if not grid:
            start()
            finish()
            return
        ids = [pl.program_id(a) for a in range(len(grid))]
        first = functools.reduce(jnp.logical_and, [i == 0 for i in ids])
        last = functools.reduce(jnp.logical_and, [i == g - 1 for i, g in zip(ids, grid)])
        pl.when(first)(start)
        body(*ins, *outs, *scr)
        pl.when(last)(finish)

    def run(*args):
        got = pl.pallas_call(
            hosted, grid=grid, in_specs=[*in_specs, *[ANY] * n_op], out_specs=[*out_specs_l, *[ANY] * n_res],
            out_shape=[*out_shape_l, *side.results],
            scratch_shapes=[*scratch_shapes, pltpu.SemaphoreType.DMA((side.n_sems,)),
                            pltpu.SemaphoreType.DMA((side.n_sems,))],
            input_output_aliases={**aliases, **{n_in + i: n_out + j for i, j in side.aliases.items()}},
            compiler_params=_params(*["arbitrary"] * len(grid)), **kw)(*args, *side.operands)
        own = list(got[:n_out]) if many else got[0]
        return own, list(got[n_out:])

    return run


def _matmul(a, b, *, ta=False, tb=False, out_dtype=F32, name, tm=1024, tn=1024, tk=2816, side=None, post=None,
            post_args=()):
    k_dim, m = a.shape if ta else a.shape[::-1]
    n, k2 = b.shape if tb else b.shape[::-1]
    assert k_dim == k2, (a.shape, b.shape, ta, tb)
    tm = _tile(m, tm, LANES if ta else 16)
    tn = _tile(n, tn, 16 if tb else LANES)
    tk = _tile(k_dim, tk, LANES)
    nk = k_dim // tk
    dims = (((0 if ta else 1,), (1 if tb else 0,)), ((), ()))
    if nk == 1:
        several = isinstance(out_dtype, tuple)

        def whole(a_ref, b_ref, *rest):
            acc = lax.dot_general(a_ref[...].astype(BF16), b_ref[...].astype(BF16), dims, preferred_element_type=F32)
            outs = rest[len(post_args):]
            if post is not None:
                acc = post(acc, *[r[...] for r in rest[:len(post_args)]])
            for o_ref, val in zip(outs, acc if several else (acc,)):
                o_ref[...] = val.astype(o_ref.dtype)

        a_spec = pl.BlockSpec((tk, tm), lambda i, j: (0, i)) if ta else pl.BlockSpec((tm, tk), lambda i, j: (i, 0))
        b_spec = pl.BlockSpec((tn, tk), lambda i, j: (j, 0)) if tb else pl.BlockSpec((tk, tn), lambda i, j: (0, j))
        o_spec = pl.BlockSpec((tm, tn), lambda i, j: (i, j))
        return _call(
            whole, side=side, name=name, grid=(m // tm, n // tn), in_specs=[a_spec, b_spec] + [o_spec] * len(post_args),
            out_specs=[o_spec] * len(out_dtype) if several else o_spec,
            out_shape=[_sds((m, n), dt) for dt in out_dtype] if several else _sds((m, n), out_dtype),
            sem=("parallel", "parallel"),
        )(a, b, *post_args)
    assert post is None

    def body(a_ref, b_ref, o_ref, acc_ref):
        k = pl.program_id(2)

        @pl.when(k == 0)
        def _():
            acc_ref[...] = jnp.zeros_like(acc_ref)

        acc_ref[...] += lax.dot_general(a_ref[...].astype(BF16), b_ref[...].astype(BF16), dims,
                                        preferred_element_type=F32)

        @pl.when(k == nk - 1)
        def _():
            o_ref[...] = acc_ref[...].astype(o_ref.dtype)

    a_spec = pl.BlockSpec((tk, tm), lambda i, j, k: (k, i)) if ta else pl.BlockSpec((tm, tk), lambda i, j, k: (i, k))
    b_spec = pl.BlockSpec((tn, tk), lambda i, j, k: (j, k)) if tb else pl.BlockSpec((tk, tn), lambda i, j, k: (k, j))
    return _call(
        body, side=side, name=name, grid=(m // tm, n // tn, nk), in_specs=[a_spec, b_spec],
        out_specs=pl.BlockSpec((tm, tn), lambda i, j, k: (i, j)), out_shape=_sds((m, n), out_dtype),
        scratch_shapes=[pltpu.VMEM((tm, tn), F32)], sem=("parallel", "parallel", "arbitrary"),
    )(a, b)


def _silu(x):
    return x * jax.nn.sigmoid(x)


def _gelu(x):
    return 0.5 * x * (1.0 + jnp.tanh(GELU_C * (x + GELU_A * x * x * x)))


def _gelu_grad(x):
    t = jnp.tanh(GELU_C * (x + GELU_A * x * x * x))
    return 0.5 * (1.0 + t) + 0.5 * x * (1.0 - t * t) * GELU_C * (1.0 + 3.0 * GELU_A * x * x)


def _expm1_nonpos(x):
    series = x * (1.0 + x * (1.0 / 2 + x * (1.0 / 6 + x * (1.0 / 24 + x * (1.0 / 120 + x * (1.0 / 720 + x / 5040))))))
    return jnp.where(x > -0.25, series, jnp.exp(x) - 1.0)


def _softplus(x):
    return jnp.maximum(x, 0.0) + jnp.log1p(jnp.exp(-jnp.abs(x)))


def _rms_stats(x):
    return lax.rsqrt(jnp.mean(x * x, axis=-1, keepdims=True) + NORM_EPS)


def _rms_bwd(dxhat, xhat, rstd):
    return rstd * (dxhat - xhat * jnp.mean(dxhat * xhat, axis=-1, keepdims=True))


def _colsum(v):
    return jnp.sum(v, axis=0, keepdims=True)


def _mod_fwd(c16, w_mod, b_mod_shard):
    r, d = c16.shape
    n = w_mod.shape[1]
    tn = _tile(n, 512, LANES)

    def body(c_ref, w_ref, b_ref, o_ref, s_ref):
        s = _silu(c_ref[...])
        s_ref[...] = s
        o_ref[...] = jnp.dot(s.astype(BF16), w_ref[...].astype(BF16), preferred_element_type=F32) + b_ref[...]

    return pl.pallas_call(
        body, name="mod_fwd", grid=(n // tn,),
        in_specs=[pl.BlockSpec((r, d), lambda j: (0, 0)), pl.BlockSpec((d, tn), lambda j: (0, j)),
                  pl.BlockSpec((1, tn), lambda j: (0, j))],
        out_specs=[pl.BlockSpec((r, tn), lambda j: (0, j)), pl.BlockSpec((r, d), lambda j: (0, 0))],
        out_shape=[_sds((r, n)), _sds((r, d))], compiler_params=_params("arbitrary"),
    )(c16, w_mod, b_mod_shard)


def _c_ctx_grad(parts, c_ctx_row):
    d = c_ctx_row.shape[1]

    def body(p_ref, c_ref, o_ref):
        tot = p_ref[0, 0:1, :]
        for chip in range(1, N_CHIPS):
            tot = tot + p_ref[2 * chip, 0:1, :]
        c = c_ref[...]
        sg = jax.nn.sigmoid(c)
        o_ref[...] = tot * (sg * (1.0 + c * (1.0 - sg)))

    return pl.pallas_call(body, name="c_ctx_grad", out_shape=_sds((1, d)), compiler_params=_params())(parts, c_ctx_row)


def _token_specs(n_ctx_rows, d, tr):
    nctx = n_ctx_rows // tr
    return (pl.BlockSpec((tr, d), lambda i: (jnp.minimum(i, nctx - 1), 0)),
            pl.BlockSpec((tr, d), lambda i: (jnp.maximum(i - nctx, 0), 0)))


def _norm_mod_fwd(ctx, x, g, mod4, tr):
    (n_ctx_rows, d), s = ctx.shape, x.shape[0]
    t = n_ctx_rows + s
    nctx = n_ctx_rows // tr

    def body(c_ref, x_ref, g_ref, mod_ref, h_ref):
        is_ctx = pl.program_id(0) < nctx
        x = jnp.where(is_ctx, c_ref[...], x_ref[...])
        n = x * _rms_stats(x) * g_ref[...]
        sh = jnp.where(is_ctx, mod_ref[0:1, :], mod_ref[2:3, :])
        sc = jnp.where(is_ctx, mod_ref[1:2, :], mod_ref[3:4, :])
        h_ref[...] = (n * (1.0 + sc) + sh).astype(BF16)

    return pl.pallas_call(
        body, name="norm_mod_fwd", grid=(t // tr,),
        in_specs=[*_token_specs(n_ctx_rows, d, tr), pl.BlockSpec((1, d), lambda i: (0, 0)),
                  pl.BlockSpec((4, d), lambda i: (0, 0))],
        out_specs=pl.BlockSpec((tr, d), lambda i: (i, 0)), out_shape=_sds((t, d), BF16),
        compiler_params=_params("parallel"),
    )(ctx, x, g, mod4)


def _norm_mod_bwd1(ctx, x, dh, dx1, g, mod4, tr, side=None):
    (n_ctx_rows, d), s = ctx.shape, x.shape[0]
    t = n_ctx_rows + s
    nctx = n_ctx_rows // tr

    def body(c_ref, x_ref, dh_ref, dx1_ref, g_ref, mod_ref, dx_ref, sums_ref):
        i = pl.program_id(0)
        is_ctx = i < nctx

        @pl.when(i == 0)
        def _():
            sums_ref[...] = jnp.zeros_like(sums_ref)

        x = jnp.where(is_ctx, c_ref[...], x_ref[...])
        dh_ = dh_ref[...]
        rstd = _rms_stats(x)
        xhat = x * rstd
        gg = g_ref[...]
        sc = jnp.where(is_ctx, mod_ref[1:2, :], mod_ref[3:4, :])
        dxhat = dh_ * (1.0 + sc) * gg
        dx_ref[...] = dx1_ref[...] + _rms_bwd(dxhat, xhat, rstd)
        part = [_colsum(dh_), _colsum(dh_ * xhat * gg), _colsum(dh_ * (1.0 + sc) * xhat)]

        @pl.when(is_ctx)
        def _():
            for k, row in enumerate(part):
                sums_ref[3 + k:4 + k, :] += row

        @pl.when(jnp.logical_not(is_ctx))
        def _():
            for k, row in enumerate(part):
                sums_ref[k:k + 1, :] += row

    lat = lambda i: (jnp.maximum(i - nctx, 0), 0)
    return _call(
        body, side=side, name="norm_mod_bwd1", grid=(t // tr,),
        in_specs=[*_token_specs(n_ctx_rows, d, tr), pl.BlockSpec((tr, d), lambda i: (i, 0)),
                  pl.BlockSpec((tr, d), lat), pl.BlockSpec((1, d), lambda i: (0, 0)),
                  pl.BlockSpec((4, d), lambda i: (0, 0))],
        out_specs=[pl.BlockSpec((tr, d), lat), pl.BlockSpec((8, d), lambda i: (0, 0))],
        out_shape=[_sds((s, d)), _sds((8, d))], sem=("arbitrary",),
    )(ctx, x, dh, dx1, g, mod4)


def _resid_norm_mod_fwd(x, mix, g, mod3, tr):
    s, d = x.shape

    def body(x_ref, mix_ref, g_ref, mod_ref, x1_ref, h_ref):
        x1 = x_ref[...] + mod_ref[0:1, :] * mix_ref[...]
        x1_ref[...] = x1
        n = x1 * _rms_stats(x1) * g_ref[...]
        h_ref[...] = (n * (1.0 + mod_ref[2:3, :]) + mod_ref[1:2, :]).astype(BF16)

    row = pl.BlockSpec((tr, d), lambda i: (i, 0))
    return pl.pallas_call(
        body, name="resid_norm_mod_fwd", grid=(s // tr,),
        in_specs=[row, row, pl.BlockSpec((1, d), lambda i: (0, 0)), pl.BlockSpec((3, d), lambda i: (0, 0))],
        out_specs=[row, row], out_shape=[_sds((s, d)), _sds((s, d), BF16)], compiler_params=_params("parallel"),
    )(x, mix, g, mod3)


def _norm_mod_bwd2(x1, dh2, dx2, mix, g, mod3, tr, side=None):
    s, d = x1.shape

    def body(x_ref, dh_ref, dx2_ref, mix_ref, g_ref, mod_ref, dx1_ref, dmix_ref, sums_ref):
        @pl.when(pl.program_id(0) == 0)
        def _():
            sums_ref[...] = jnp.zeros_like(sums_ref)

        x = x_ref[...]
        dh_ = dh_ref[...]
        rstd = _rms_stats(x)
        xhat = x * rstd
        gg = g_ref[...]
        sc = mod_ref[2:3, :]
        dx1 = dx2_ref[...] + _rms_bwd(dh_ * (1.0 + sc) * gg, xhat, rstd)
        dx1_ref[...] = dx1
        dmix_ref[...] = (dx1 * mod_ref[0:1, :]).astype(BF16)
        part = [_colsum(dh_), _colsum(dh_ * xhat * gg), _colsum(dh_ * (1.0 + sc) * xhat), _colsum(dx1 * mix_ref[...])]
        for k, row in enumerate(part):
            sums_ref[k:k + 1, :] += row

    row = pl.BlockSpec((tr, d), lambda i: (i, 0))
    return _call(
        body, side=side, name="norm_mod_bwd2", grid=(s // tr,),
        in_specs=[row, row, row, row, pl.BlockSpec((1, d), lambda i: (0, 0)), pl.BlockSpec((3, d), lambda i: (0, 0))],
        out_specs=[row, row, pl.BlockSpec((8, d), lambda i: (0, 0))],
        out_shape=[_sds((s, d)), _sds((s, d), BF16), _sds((8, d))], sem=("arbitrary",),
    )(x1, dh2, dx2, mix, g, mod3)


def _final_fwd_bwd(x1, down, target, g_final, gate, tr):
    s, d = x1.shape

    def body(x1_ref, down_ref, t_ref, g_ref, gate_ref, dx2_ref, ddown_ref, sums_ref, loss_ref):
        @pl.when(pl.program_id(0) == 0)
        def _():
            sums_ref[...] = jnp.zeros_like(sums_ref)
            loss_ref[...] = jnp.zeros_like(loss_ref)

        down_ = down_ref[...]
        gate_ = gate_ref[...]
        x2 = x1_ref[...] + gate_ * down_
        rstd = _rms_stats(x2)
        xhat = x2 * rstd
        gg = g_ref[...]
        err = xhat * gg - t_ref[...]
        loss_ref[...] += 0.5 * jnp.sum(jnp.mean(err * err, axis=-1, keepdims=True))
        dy = err * (1.0 / d)
        dx2 = _rms_bwd(dy * gg, xhat, rstd)
        dx2_ref[...] = dx2
        ddown_ref[...] = (dx2 * gate_).astype(BF16)
        sums_ref[0:1, :] += _colsum(dy * xhat)
        sums_ref[1:2, :] += _colsum(dx2 * down_)

    row = pl.BlockSpec((tr, d), lambda i: (i, 0))
    vec = pl.BlockSpec((1, d), lambda i: (0, 0))
    return pl.pallas_call(
        body, name="final_fwd_bwd", grid=(s // tr,), in_specs=[row, row, row, vec, vec],
        out_specs=[row, row, pl.BlockSpec((8, d), lambda i: (0, 0)), pl.BlockSpec((8, LANES), lambda i: (0, 0))],
        out_shape=[_sds((s, d)), _sds((s, d), BF16), _sds((8, d)), _sds((8, LANES))],
        compiler_params=_params("arbitrary"),
    )(x1, down, target, g_final, gate)


def _swap_pairs(v):
    lane = lax.broadcasted_iota(jnp.int32, v.shape, 1)
    return jnp.where(lane % 2 == 0, pltpu.roll(v, LANES - 1, 1), pltpu.roll(v, 1, 1))


def _head_prep_fwd(z, col_off, n_heads, gain, cos, sin, tr, name, side=None):
    t = z.shape[0]
    per = math.gcd(4, n_heads, col_off // LANES)
    w = per * LANES
    hb = col_off // w

    def body(z_ref, g_ref, cos_ref, sin_ref, o_ref):
        for hh in range(per):
            cols = slice(hh * LANES, (hh + 1) * LANES)
            x = z_ref[:, cols]
            y = x * _rms_stats(x) * g_ref[...]
            o_ref[:, cols] = (y * cos_ref[...] + _swap_pairs(y) * sin_ref[...]).astype(BF16)

    tab = pl.BlockSpec((tr, LANES), lambda i, j: (i, 0))
    return _call(
        body, side=side, name=name, grid=(t // tr, n_heads // per),
        in_specs=[pl.BlockSpec((tr, w), lambda i, j: (i, hb + j)), pl.BlockSpec((1, LANES), lambda i, j: (0, 0)),
                  tab, tab],
        out_specs=pl.BlockSpec((tr, w), lambda i, j: (i, j)), out_shape=_sds((t, n_heads * LANES), BF16),
        sem=("parallel", "parallel"),
    )(z, gain, cos, sin)


def _dz_start(t, n_in, n_ctx_rows):
    tcol = _tile(n_in, 1024, LANES)

    def body(o_ref):
        o_ref[...] = jnp.zeros_like(o_ref)

    return pl.pallas_call(body, name="dz_start", grid=(n_in // tcol,),
                          out_specs=pl.BlockSpec((n_ctx_rows, tcol), lambda j: (0, j)), out_shape=_sds((t, n_in), BF16),
                          compiler_params=_params("parallel"))()


def _head_prep_bwd(z, col_off, n_heads, gain, cos, sin, dout, row_off, tr, dz, name, side=None):
    r = dout.shape[0]
    per = math.gcd(4, n_heads, col_off // LANES)
    w = per * LANES
    hb = col_off // w
    rb = row_off // tr

    def body(z_ref, g_ref, cos_ref, sin_ref, d_ref, _, dz_ref, dg_ref):
        @pl.when(jnp.logical_and(pl.program_id(0) == 0, pl.program_id(1) == 0))
        def _():
            dg_ref[...] = jnp.zeros_like(dg_ref)

        for hh in range(per):
            cols = slice(hh * LANES, (hh + 1) * LANES)
            x = z_ref[:, cols]
            rstd = _rms_stats(x)
            xhat = x * rstd
            dd = d_ref[:, cols]
            dy = dd * cos_ref[...] - _swap_pairs(dd) * sin_ref[...]
            dg_ref[0:1, :] += _colsum(dy * xhat)
            dz_ref[:, cols] = _rms_bwd(dy * g_ref[...], xhat, rstd).astype(BF16)

    tab = pl.BlockSpec((tr, LANES), lambda i, j: (rb + i, 0))
    window = pl.BlockSpec((tr, w), lambda i, j: (rb + i, hb + j))
    return _call(
        body, side=side, name=name, grid=(r // tr, n_heads // per),
        in_specs=[window, pl.BlockSpec((1, LANES), lambda i, j: (0, 0)), tab, tab,
                  pl.BlockSpec((tr, w), lambda i, j: (i, j)), ANY],
        out_specs=[window, pl.BlockSpec((8, LANES), lambda i, j: (0, 0))],
        out_shape=[_sds(dz.shape, BF16), _sds((8, LANES))], input_output_aliases={5: 0}, sem=("arbitrary", "arbitrary"),
    )(z, gain, cos, sin, dout, dz)


def _attn_fwd(qr, kr, z, v_off, n_ctx_rows, group, tq, side=None):
    t, kvw = kr.shape
    s = t - n_ctx_rows
    n_kv = kvw // LANES
    scale = LANES ** -0.5
    qb0 = n_ctx_rows // tq
    vb = v_off // LANES

    def body(q_ref, k_ref, v_ref, o_ref, lse_ref):
        k = k_ref[...]
        v = v_ref[...].astype(BF16)
        lse_ref[...] = jnp.zeros_like(lse_ref)
        for g in range(group):
            cols = slice(g * LANES, (g + 1) * LANES)
            sc = lax.dot_general(q_ref[:, cols], k, (((1,), (1,)), ((), ())), preferred_element_type=F32) * scale
            m = jnp.max(sc, axis=-1, keepdims=True)
            e = jnp.exp(sc - m)
            l = jnp.sum(e, axis=-1, keepdims=True)
            p = e * (1.0 / l)
            o_ref[:, cols] = jnp.dot(p.astype(BF16), v, preferred_element_type=F32).astype(BF16)
            lse_ref[:, g:g + 1] = m + jnp.log(l)

    return _call(
        body, side=side, name="attn_fwd", grid=(n_kv, s // tq),
        in_specs=[pl.BlockSpec((tq, group * LANES), lambda h, i: (qb0 + i, h)),
                  pl.BlockSpec((t, LANES), lambda h, i: (0, h)), pl.BlockSpec((t, LANES), lambda h, i: (0, vb + h))],
        out_specs=[pl.BlockSpec((tq, group * LANES), lambda h, i: (i, h)), pl.BlockSpec((tq, LANES), lambda h, i: (i, h))],
        out_shape=[_sds((s, n_kv * group * LANES), BF16), _sds((s, kvw))], sem=("parallel", "parallel"),
    )(qr, kr, z)


def _attn_bwd(qr, kr, z, v_off, d_o, attn_o, lse, dz, n_ctx_rows, group, tq, side=None):
    t, kvw = kr.shape
    s = t - n_ctx_rows
    n_kv = kvw // LANES
    scale = LANES ** -0.5
    qb0 = n_ctx_rows // tq
    vb = v_off // LANES
    n_q_blocks = s // tq
    tn_dims = (((0,), (0,)), ((), ()))
    nt_dims = (((1,), (1,)), ((), ()))

    def body(q_ref, k_ref, v_ref, do_ref, o_ref, lse_ref, _, dq_ref, dk_ref, dz_ref, dv_ref):
        @pl.when(pl.program_id(1) == 0)
        def _():
            dk_ref[...] = jnp.zeros_like(dk_ref)
            dv_ref[...] = jnp.zeros_like(dv_ref)

        k = k_ref[...]
        v = v_ref[...].astype(BF16)
        for g in range(group):
            cols = slice(g * LANES, (g + 1) * LANES)
            q = q_ref[:, cols]
            do_ = do_ref[:, cols]
            row_dot = jnp.sum(do_.astype(F32) * o_ref[:, cols].astype(F32), axis=-1, keepdims=True)
            sc = lax.dot_general(q, k, nt_dims, preferred_element_type=F32)
            p = jnp.exp(sc * scale - lse_ref[:, g:g + 1])
            dv_ref[...] += lax.dot_general(p.astype(BF16), do_, tn_dims, preferred_element_type=F32)
            dp = lax.dot_general(do_, v, nt_dims, preferred_element_type=F32)
            ds = (p * (dp - row_dot)).astype(BF16)
            dq_ref[:, cols] = jnp.dot(ds, k, preferred_element_type=F32) * scale
            dk_ref[...] += lax.dot_general(ds, q, tn_dims, preferred_element_type=F32)

        @pl.when(pl.program_id(1) == n_q_blocks - 1)
        def _():
            dk_ref[...] = dk_ref[...] * scale
            dz_ref[...] = dv_ref[...].astype(BF16)

    qspec = pl.BlockSpec((tq, group * LANES), lambda h, i: (qb0 + i, h))
    ospec = pl.BlockSpec((tq, group * LANES), lambda h, i: (i, h))
    kspec = pl.BlockSpec((t, LANES), lambda h, i: (0, h))
    vspec = pl.BlockSpec((t, LANES), lambda h, i: (0, vb + h))
    return _call(
        body, side=side, name="attn_bwd", grid=(n_kv, n_q_blocks),
        in_specs=[qspec, kspec, vspec, ospec, ospec, pl.BlockSpec((tq, LANES), lambda h, i: (i, h)), ANY],
        out_specs=[ospec, kspec, vspec], out_shape=[_sds((s, n_kv * group * LANES)), _sds((t, kvw)), _sds(dz.shape, BF16)],
        scratch_shapes=[pltpu.VMEM((t, LANES), F32)], input_output_aliases={6: 2}, sem=("parallel", "arbitrary"),
    )(qr, kr, z, d_o, attn_o, lse, dz)


def _row_mask(shape, rows):
    r = lax.broadcasted_iota(jnp.int32, shape, 0)
    m = r == rows[0]
    for v in rows[1:]:
        m = jnp.logical_or(m, r == v)
    return m


def _shift_rows(x, k, n_ctx_rows):
    t = x.shape[0]
    if k == 0:
        return x
    rolled = pltpu.roll(x, (-k) % t, 0)
    if k > 0:
        dead = [n_ctx_rows - 1 - i for i in range(k)] + [t - 1 - i for i in range(k)]
    else:
        dead = [i for i in range(-k)] + [n_ctx_rows + i for i in range(-k)]
    return jnp.where(_row_mask(x.shape, dead), 0.0, rolled)


def _conv(x, w, b, n_ctx_rows):
    y = b
    for k in range(CONV_WIDTH):
        y = y + _shift_rows(x, k - 1, n_ctx_rows) * w[k:k + 1, :]
    return y


def _gates(xc_bf, w_r, b_r, w_i, b_i, lam):
    r = jax.nn.sigmoid(jnp.dot(xc_bf, w_r.astype(BF16), preferred_element_type=F32) + b_r)
    i = jax.nn.sigmoid(jnp.dot(xc_bf, w_i.astype(BF16), preferred_element_type=F32) + b_i)
    log_a = -LRU_C * r * _softplus(-lam)
    a = jnp.exp(log_a)
    mult = jnp.sqrt(-_expm1_nonpos(2.0 * log_a))
    return r, i, a, mult


def _rnn_specs(t, xr_off):
    xb = xr_off // LANES
    return dict(
        zcol=pl.BlockSpec((t, LANES), lambda j: (0, xb + j)), col=pl.BlockSpec((t, LANES), lambda j: (0, j)),
        conv_w=pl.BlockSpec((CONV_WIDTH, LANES), lambda j: (0, j)), vec=pl.BlockSpec((1, LANES), lambda j: (0, j)),
        gate_w=pl.BlockSpec((2, 1, LANES, LANES), lambda j: (0, j, 0, 0)), two=pl.BlockSpec((2, LANES), lambda j: (0, j)))


def _rnn_prep(z, xr_off, conv_w, conv_b, w_rg, b_rg, w_ig, b_ig, lam, n_ctx_rows, side=None):
    t = z.shape[0]
    d = conv_b.shape[1]
    sp = _rnn_specs(t, xr_off)

    def body(z_ref, cw_ref, cb_ref, wr_ref, br_ref, wi_ref, bi_ref, lam_ref, xc_ref, af_ref, bf_ref, ab_ref, bb_ref):
        xc = _conv(z_ref[...], cw_ref[...], cb_ref[...], n_ctx_rows)
        xc_ref[...] = xc
        xc_bf = xc.astype(BF16)
        for dr, (a_ref, b_ref) in enumerate(((af_ref, bf_ref), (ab_ref, bb_ref))):
            _, i, a, mult = _gates(xc_bf, wr_ref[dr, 0], br_ref[dr:dr + 1, :], wi_ref[dr, 0], bi_ref[dr:dr + 1, :],
                                   lam_ref[dr:dr + 1, :])
            a_ref[...] = a
            b_ref[...] = mult * (i * xc)

    return _call(
        body, side=side, name="rnn_prep", grid=(d // LANES,),
        in_specs=[sp["zcol"], sp["conv_w"], sp["vec"], sp["gate_w"], sp["two"], sp["gate_w"], sp["two"], sp["two"]],
        out_specs=[sp["col"]] * 5, out_shape=[_sds((t, d))] * 5, sem=("parallel",),
    )(z, conv_w, conv_b, w_rg, b_rg, w_ig, b_ig, lam)


def _scan(chains, *, post, n_ctx_rows, name, tc=256, side=None):
    t, d = chains[0][0].shape
    tc = _tile(math.gcd(n_ctx_rows, t - n_ctx_rows), tc, SUBLANES)
    nt, nctx = t // tc, n_ctx_rows // tc
    nlat = nt - nctx
    nc = len(chains)
    lat_only = [b.shape[0] != t for _, b, _ in chains]
    ups = [order.endswith("up") for _, _, order in chains]

    def chunk_of(order):
        def chunk(i):
            if order == "ctx_lat_up":
                return i
            if order == "lat_ctx_down":
                return nt - 1 - i
            if order == "ctx_lat_down":
                return jnp.where(i < nctx, nctx - 1 - i, nt - 1 - (i - nctx))
            return jnp.where(i < nlat, nctx + i, i - nlat)
        return chunk

    chunks = [chunk_of(order) for _, _, order in chains]

    def body(*refs):
        ab_refs, o_refs, carry_ref = refs[:2 * nc], refs[2 * nc:3 * nc], refs[3 * nc]

        @pl.when(pl.program_id(0) == 0)
        def _():
            carry_ref[...] = jnp.zeros_like(carry_ref)

        live = [jnp.where(chunks[n](pl.program_id(0)) >= nctx, 1.0, 0.0) if lat_only[n] else None for n in range(nc)]

        def group(gi, carries):
            carries = list(carries)
            bases = [pl.multiple_of((gi if ups[n] else tc // SUBLANES - 1 - gi) * SUBLANES, SUBLANES) for n in range(nc)]
            for step in range(SUBLANES):
                for n in range(nc):
                    row = bases[n] + (step if ups[n] else SUBLANES - 1 - step)
                    a_r = ab_refs[2 * n][pl.ds(row, 1), :]
                    b_r = ab_refs[2 * n + 1][pl.ds(row, 1), :]
                    if live[n] is not None:
                        b_r = b_r * live[n]
                    if post:
                        out = b_r + carries[n]
                        carries[n] = a_r * out
                    else:
                        out = a_r * carries[n] + b_r
                        carries[n] = out
                    o_refs[n][pl.ds(row, 1), :] = out
            return tuple(carries)

        done = lax.fori_loop(0, tc // SUBLANES, group, tuple(carry_ref[n:n + 1, :] for n in range(nc)))
        for n in range(nc):
            carry_ref[n:n + 1, :] = done[n]

    in_specs, out_specs, args = [], [], []
    for n, (a, b, _) in enumerate(chains):
        full = pl.BlockSpec((tc, d), lambda i, n=n: (chunks[n](i), 0))
        lat = pl.BlockSpec((tc, d), lambda i, n=n: (jnp.maximum(chunks[n](i) - nctx, 0), 0))
        in_specs += [full, lat if lat_only[n] else full]
        out_specs.append(full)
        args += [a, b]
    return _call(
        body, side=side, name=name, grid=(nt,), in_specs=in_specs, out_specs=out_specs, out_shape=[_sds((t, d))] * nc,
        scratch_shapes=[pltpu.VMEM((SUBLANES, d), F32)], sem=("arbitrary",),
    )(*args)


def _rnn_bwd(z, xr_off, xc, g_f, g_b, h_f, h_b, conv_w, w_rg, b_rg, w_ig, b_ig, lam, dz, n_ctx_rows, side=None):
    t, d = xc.shape
    sp = _rnn_specs(t, xr_off)
    tn_dims = (((0,), (0,)), ((), ()))
    nt_dims = (((1,), (1,)), ((), ()))

    def body(z_ref, xc_ref, gf_ref, gb_ref, hf_ref, hb_ref, cw_ref, wr_ref, br_ref, wi_ref, bi_ref, lam_ref, _,
             dxr_ref, dwr_ref, dwi_ref, sums_ref):
        xc_ = xc_ref[...]
        xc_bf = xc_.astype(BF16)
        dxc = jnp.zeros_like(xc_)
        sums = [None] * 6
        for dr, (g_ref, h_ref) in enumerate(((gf_ref, hf_ref), (gb_ref, hb_ref))):
            w_r, w_i, lam_ = wr_ref[dr, 0], wi_ref[dr, 0], lam_ref[dr:dr + 1, :]
            r, i, a, mult = _gates(xc_bf, w_r, br_ref[dr:dr + 1, :], w_i, bi_ref[dr:dr + 1, :], lam_)
            g = g_ref[...]
            h = h_ref[...]
            if dr == 0:
                h_prev = jnp.where(_row_mask(h.shape, [0]), 0.0, pltpu.roll(h, 1, 0))
            else:
                h_prev = jnp.where(_row_mask(h.shape, [n_ctx_rows - 1]), 0.0, pltpu.roll(h, t - 1, 0))
            d_mult = g * i * xc_
            d_i = g * mult * xc_
            dxc = dxc + g * mult * i
            d_log_a = g * h_prev * a - d_mult * a * a / mult
            sp_ = _softplus(-lam_)
            d_r = d_log_a * (-LRU_C) * sp_
            d_sp = _colsum(d_log_a * (-LRU_C) * r)
            du_r = (d_r * r * (1.0 - r))
            du_i = (d_i * i * (1.0 - i))
            sums[dr] = _colsum(du_r)
            sums[2 + dr] = _colsum(du_i)
            sums[4 + dr] = d_sp * (-jax.nn.sigmoid(-lam_))
            du_r_bf, du_i_bf = du_r.astype(BF16), du_i.astype(BF16)
            dwr_ref[dr, 0] = lax.dot_general(xc_bf, du_r_bf, tn_dims, preferred_element_type=F32).astype(BF16)
            dwi_ref[dr, 0] = lax.dot_general(xc_bf, du_i_bf, tn_dims, preferred_element_type=F32).astype(BF16)
            dxc = dxc + lax.dot_general(du_r_bf, w_r.astype(BF16), nt_dims, preferred_element_type=F32)
            dxc = dxc + lax.dot_general(du_i_bf, w_i.astype(BF16), nt_dims, preferred_element_type=F32)
        xr = z_ref[...]
        cw = cw_ref[...]
        dxr = jnp.zeros_like(dxc)
        rows = list(sums)
        for k in range(CONV_WIDTH):
            dxr = dxr + _shift_rows(dxc, 1 - k, n_ctx_rows) * cw[k:k + 1, :]
            rows.append(_colsum(dxc * _shift_rows(xr, k - 1, n_ctx_rows)))
        rows.append(_colsum(dxc))
        dxr_ref[...] = dxr.astype(BF16)
        sums_ref[...] = jnp.zeros_like(sums_ref)
        for k, row in enumerate(rows):
            sums_ref[k:k + 1, :] = row

    return _call(
        body, side=side, name="rnn_bwd", grid=(d // LANES,),
        in_specs=[sp["zcol"]] + [sp["col"]] * 5 + [sp["conv_w"], sp["gate_w"], sp["two"], sp["gate_w"], sp["two"],
                                                  sp["two"], ANY],
        out_specs=[sp["zcol"], sp["gate_w"], sp["gate_w"], pl.BlockSpec((16, LANES), lambda j: (0, j))],
        out_shape=[_sds(dz.shape, BF16), _sds(w_rg.shape, BF16), _sds(w_ig.shape, BF16), _sds((16, d))],
        input_output_aliases={12: 0}, sem=("parallel",),
    )(z, xc, g_f, g_b, h_f, h_b, conv_w, w_rg, b_rg, w_ig, b_ig, lam, dz)


def _tiles2d(s, d, tr, tcol):
    return (s // tr, d // tcol), pl.BlockSpec((tr, tcol), lambda i, j: (i, j))


def _zspec(tr, tcol, row_off, col_off):
    rb, cb = row_off // tr, col_off // tcol
    return pl.BlockSpec((tr, tcol), lambda i, j: (rb + i, cb + j))


def _rnn_gate_fwd(h_f, h_b, z, xg_off, n_ctx_rows, tr, tcol, side=None):
    t, d = h_f.shape
    s = t - n_ctx_rows
    grid, out = _tiles2d(s, d, tr, tcol)
    hs = _zspec(tr, tcol, n_ctx_rows, 0)

    def body(hf_ref, hb_ref, xg_ref, u_ref):
        u_ref[...] = ((hf_ref[...] + hb_ref[...]) * _gelu(xg_ref[...])).astype(BF16)

    return _call(body, side=side, name="rnn_gate_fwd", grid=grid,
                 in_specs=[hs, hs, _zspec(tr, tcol, n_ctx_rows, xg_off)], out_specs=out, out_shape=_sds((s, d), BF16),
                 sem=("parallel", "parallel"))(h_f, h_b, z)


def _rnn_gate_bwd(d_u, h_f, h_b, z, xg_off, dz, n_ctx_rows, tr, tcol, side=None):
    t, d = h_f.shape
    s = t - n_ctx_rows
    grid, out = _tiles2d(s, d, tr, tcol)
    hs = _zspec(tr, tcol, n_ctx_rows, 0)
    window = _zspec(tr, tcol, n_ctx_rows, xg_off)

    def body(du_ref, hf_ref, hb_ref, xg_ref, _, dr_ref, dxg_ref):
        du = du_ref[...]
        xg = xg_ref[...]
        dr_ref[...] = du * _gelu(xg)
        dxg_ref[...] = (du * (hf_ref[...] + hb_ref[...]) * _gelu_grad(xg)).astype(BF16)

    return _call(body, side=side, name="rnn_gate_bwd", grid=grid, in_specs=[out, hs, hs, window, ANY],
                 out_specs=[out, window], out_shape=[_sds((s, d)), _sds(dz.shape, BF16)], input_output_aliases={4: 1},
                 sem=("parallel", "parallel"))(d_u, h_f, h_b, z, dz)


def _merge_fwd(y_attn, y_rnn, z, gl_off, n_ctx_rows, tr, tcol):
    s, d = y_attn.shape
    grid, out = _tiles2d(s, d, tr, tcol)

    def body(ya_ref, yr_ref, ga_ref, gr_ref, o_ref):
        o_ref[...] = (jax.nn.sigmoid(ga_ref[...]) * ya_ref[...] + jax.nn.sigmoid(gr_ref[...]) * yr_ref[...]).astype(BF16)

    return pl.pallas_call(
        body, name="merge_fwd", grid=grid,
        in_specs=[out, out, _zspec(tr, tcol, n_ctx_rows, gl_off), _zspec(tr, tcol, n_ctx_rows, gl_off + d)],
        out_specs=out, out_shape=_sds((s, d), BF16), compiler_params=_params("parallel", "parallel"),
    )(y_attn, y_rnn, z, z)


def _merge_bwd(d_mrg, y, z, gl_off, dz, n_ctx_rows, tr, tcol, name):
    s, d = y.shape
    grid, out = _tiles2d(s, d, tr, tcol)
    window = _zspec(tr, tcol, n_ctx_rows, gl_off)

    def body(dm_ref, y_ref, gl_ref, _, dy_ref, dgl_ref):
        dm = dm_ref[...]
        g = jax.nn.sigmoid(gl_ref[...])
        dy_ref[...] = (dm * g).astype(BF16)
        dgl_ref[...] = (dm * y_ref[...] * g * (1.0 - g)).astype(BF16)

    return _call(body, name=name, grid=grid, in_specs=[out, out, window, ANY], out_specs=[out, window],
                 out_shape=[_sds((s, d), BF16), _sds(dz.shape, BF16)], input_output_aliases={3: 1},
                 sem=("parallel", "parallel"))(d_mrg, y, z, dz)


def _cast_into_window(w, chip, col_sharded, name):
    r, c = w.shape
    tr, tcol = _tile(r, 512, 16), _tile(c, WIDE_TILE, LANES)
    nrb, ncb = r // tr, c // tcol

    def body(chip_ref, w_ref, o_ref):
        o_ref[...] = w_ref[...].astype(BF16)

    if col_sharded:
        omap = lambda i, j, chip_ref: (i, chip_ref[0] * ncb + j)
    else:
        omap = lambda i, j, chip_ref: (chip_ref[0] * nrb + i, j)
    return pl.pallas_call(
        body, name=name,
        grid_spec=pltpu.PrefetchScalarGridSpec(
            num_scalar_prefetch=1, grid=(nrb, ncb),
            in_specs=[pl.BlockSpec((tr, tcol), lambda i, j, chip_ref: (i, j))], out_specs=pl.BlockSpec((tr, tcol), omap)),
        out_shape=_sds((r, c * N_CHIPS) if col_sharded else (r * N_CHIPS, c), BF16),
        compiler_params=_params("parallel", "parallel"),
    )(chip, w)


def _sum_leading(parts, name):
    n, r, c = parts.shape
    tr, tcol = _tile(r, 512, SUBLANES), _tile(c, 1024, LANES)

    def body(p_ref, o_ref):
        tot = p_ref[0]
        for k in range(1, n):
            tot = tot + p_ref[k]
        o_ref[...] = tot

    return pl.pallas_call(
        body, name=name, grid=(r // tr, c // tcol), in_specs=[pl.BlockSpec((n, tr, tcol), lambda i, j: (0, i, j))],
        out_specs=pl.BlockSpec((tr, tcol), lambda i, j: (i, j)), out_shape=_sds((r, c)),
        compiler_params=_params("parallel", "parallel"),
    )(parts)


def _add_half(full, other, core, split_rows, name):
    r, c = other.shape
    tr, tcol = _tile(r, 512, 16), _tile(c, 1024, LANES)
    nrb, ncb = r // tr, c // tcol

    def body(core_ref, f_ref, o_ref, out_ref):
        out_ref[...] = (f_ref[...].astype(F32) + o_ref[...].astype(F32)).astype(out_ref.dtype)

    if split_rows:
        fmap = lambda i, j, core_ref: (core_ref[0] * nrb + i, j)
    else:
        fmap = lambda i, j, core_ref: (i, core_ref[0] * ncb + j)
    same = lambda i, j, core_ref: (i, j)
    return pl.pallas_call(
        body, name=name,
        grid_spec=pltpu.PrefetchScalarGridSpec(
            num_scalar_prefetch=1, grid=(nrb, ncb),
            in_specs=[pl.BlockSpec((tr, tcol), fmap), pl.BlockSpec((tr, tcol), same)],
            out_specs=pl.BlockSpec((tr, tcol), same)),
        out_shape=_sds((r, c), BF16), compiler_params=_params("parallel", "parallel"),
    )(core, full, other)


def _add_pair(a, b, name):
    r, c = a.shape
    tr, tcol = _tile(r, 512, 16), _tile(c, WIDE_TILE, LANES)
    blk = pl.BlockSpec((tr, tcol), lambda i, j: (i, j))

    def body(a_ref, b_ref, o_ref):
        o_ref[...] = (a_ref[...].astype(F32) + b_ref[...].astype(F32)).astype(BF16)

    return pl.pallas_call(body, name=name, grid=(r // tr, c // tcol), in_specs=[blk, blk], out_specs=blk,
                          out_shape=_sds((r, c), BF16), compiler_params=_params("parallel", "parallel"))(a, b)


def _sum_regions(pair, got, place, col_sharded, name):
    n_got, r, c = got.shape
    tr, tcol = _tile(r, 512, 16), _tile(c, WIDE_TILE, LANES)
    nrb, ncb = r // tr, c // tcol

    def body(place_ref, p_ref, g_ref, out_ref):
        tot = p_ref[...].astype(F32)
        for k in range(n_got):
            tot = tot + g_ref[k].astype(F32)
        out_ref[...] = tot

    if col_sharded:
        pmap = lambda i, j, pr: (i, pr[0] * ncb + j)
        omap = lambda i, j, pr: (pr[1] * nrb + i, j)
        out_shape = (2 * r, c)
    else:
        pmap = lambda i, j, pr: (pr[0] * nrb + i, j)
        omap = lambda i, j, pr: (i, pr[1] * ncb + j)
        out_shape = (r, 2 * c)
    return pl.pallas_call(
        body, name=name,
        grid_spec=pltpu.PrefetchScalarGridSpec(
            num_scalar_prefetch=1, grid=(nrb, ncb),
            in_specs=[pl.BlockSpec((tr, tcol), pmap), pl.BlockSpec((n_got, tr, tcol), lambda i, j, pr: (0, i, j))],
            out_specs=pl.BlockSpec((tr, tcol), omap)),
        out_shape=_sds(out_shape), compiler_params=_params("parallel", "parallel"),
    )(place, pair, got)


def _fold_forwarded(pair, fold, nbr_chips, name):
    _, r, c = fold.shape
    tr, tcol = _tile(r, 512, 16), _tile(c, WIDE_TILE, LANES)
    nrb, ncb = r // tr, c // tcol

    def body(nbr_ref, px_ref, py_ref, f_ref, out_ref):
        out_ref[0] = (px_ref[...].astype(F32) + f_ref[1].astype(F32)).astype(out_ref.dtype)
        out_ref[1] = (py_ref[...].astype(F32) + f_ref[0].astype(F32)).astype(out_ref.dtype)

    both = pl.BlockSpec((2, tr, tcol), lambda i, j, nb: (0, i, j))
    return pl.pallas_call(
        body, name=name,
        grid_spec=pltpu.PrefetchScalarGridSpec(
            num_scalar_prefetch=1, grid=(nrb, ncb),
            in_specs=[pl.BlockSpec((tr, tcol), lambda i, j, nb: (i, nb[0] * ncb + j)),
                      pl.BlockSpec((tr, tcol), lambda i, j, nb: (nrb + i, nb[1] * ncb + j)), both],
            out_specs=both),
        out_shape=_sds(fold.shape, fold.dtype), compiler_params=_params("parallel", "parallel"),
    )(nbr_chips, pair, pair, fold)


def _adamw(w, g, m, v, name, copy_grad=True):
    r, c = w.shape
    tr, tcol = _tile(r, 256, SUBLANES), _tile(c, WIDE_TILE, LANES)
    blk = pl.BlockSpec((tr, tcol), lambda i, j: (i, j))
    n_out = 4 if copy_grad else 3

    def body(w_ref, g_ref, m_ref, v_ref, d_ref, nm_ref, nv_ref, *g_out_ref):
        g_ = g_ref[...]
        if copy_grad:
            g_out_ref[0][...] = g_
        m_ = ADAM_B1 * m_ref[...] + (1.0 - ADAM_B1) * g_
        v_ = ADAM_B2 * v_ref[...] + (1.0 - ADAM_B2) * (g_ * g_)
        m_hat = m_ / (1.0 - ADAM_B1 ** ADAM_STEP)
        v_hat = v_ / (1.0 - ADAM_B2 ** ADAM_STEP)
        d_ref[...] = -ADAM_LR * (m_hat / (jnp.sqrt(v_hat) + ADAM_EPS) + ADAM_WD * w_ref[...])
        nm_ref[...] = m_
        nv_ref[...] = v_

    return _call(body, name=name, grid=(r // tr, c // tcol), in_specs=[blk] * 4, out_specs=[blk] * n_out,
                 out_shape=[_sds((r, c))] * n_out, sem=("parallel", "parallel"))(w, g, m, v)


def _place():
    x, y, c = lax.axis_index("x"), lax.axis_index("y"), lax.axis_index("c")
    chips = [(1 - x, y), (x, 1 - y), (1 - x, 1 - y)]
    return x, y, c, chips


def _all_gather8(blk, name):
    m, n = blk.shape

    def body(x_ref, out_ref, send_sems, recv_sems, local_sem):
        x, y, c, chips = _place()
        me, sibling = (x, y, c), (x, y, 1 - c)

        def rows(px, py, pc):
            return out_ref.at[pl.ds((4 * px + 2 * py + pc) * m, m), :]

        def copy(k, block, to, src=None):
            return pltpu.make_async_remote_copy(
                src_ref=rows(*block) if src is None else src, dst_ref=rows(*block), send_sem=send_sems.at[k],
                recv_sem=recv_sems.at[k], device_id=to, device_id_type=MESH_ID)

        mine = pltpu.make_async_copy(x_ref, rows(*me), local_sem)
        mine.start()
        first = [copy(0, me, sibling, src=x_ref)]
        first += [copy(1 + j, me, (*chip, c), src=x_ref) for j, chip in enumerate(chips)]
        for cp in first:
            cp.start()
        passed = [copy(4 + j, (*chip, c), sibling) for j, chip in enumerate(chips)]
        for j, chip in enumerate(chips):
            copy(1 + j, (*chip, c), me).wait_recv()
            passed[j].start()
        copy(0, sibling, me).wait_recv()
        for j, chip in enumerate(chips):
            copy(4 + j, (*chip, 1 - c), me).wait_recv()
        for cp in first + passed:
            cp.wait_send()
        mine.wait()

    return pl.pallas_call(
        body, name=name, out_shape=_sds((N_DEV * m, n), blk.dtype), in_specs=[ANY], out_specs=ANY,
        scratch_shapes=[pltpu.SemaphoreType.DMA((7,)), pltpu.SemaphoreType.DMA((7,)), pltpu.SemaphoreType.DMA],
    )(blk)


def _half(ref, core, split_rows):
    r, c = ref.shape
    if split_rows:
        return ref.at[pl.ds(core * (r // 2), r // 2), :]
    return ref.at[:, pl.ds(core * (c // 2), c // 2)]


def _chip_block(ref, j, col_sharded):
    r, c = ref.shape
    if col_sharded:
        return ref.at[:, pl.ds(j * (c // N_CHIPS), c // N_CHIPS)]
    return ref.at[pl.ds(j * (r // N_CHIPS), r // N_CHIPS), :]


def _rows_part(ref, part):
    lo, hi, n = part
    r = ref.shape[0]
    return ref if (lo, hi) == (0, n) else ref.at[pl.ds(lo * (r // n), (hi - lo) * (r // n)), :]


def _copy(send_sems, recv_sems, k, src, dst, to):
    return pltpu.make_async_remote_copy(src_ref=src, dst_ref=dst, send_sem=send_sems.at[k], recv_sem=recv_sems.at[k],
                                        device_id=to, device_id_type=MESH_ID)


def _in_place(arrays):
    return tuple(arrays), tuple(_sds(a.shape, a.dtype) for a in arrays), {i: i for i in range(len(arrays))}


def _gather_d2d(fulls, col_sharded):
    nw = len(fulls)

    def build(_, refs, send_sems, recv_sems, sem0):
        x, y, c, chips = _place()
        sends, recvs = [], []
        for w in range(nw):
            win = lambda j, core: _half(_chip_block(refs[w], j, col_sharded[w]), core, True)
            for k, (cx, cy) in enumerate(chips):
                sem = sem0 + 3 * w + k
                sends.append(_copy(send_sems, recv_sems, sem, win(2 * cx + cy, c), win(2 * cx + cy, c), (x, y, 1 - c)))
                recvs.append(_copy(send_sems, recv_sems, sem, win(2 * cx + cy, 1 - c), win(2 * cx + cy, 1 - c),
                                   (x, y, 1 - c)))
        return sends, recvs

    return _Side(*_in_place(fulls), 3 * nw, build)


def _gather_neighbours(fulls, col_sharded, part=(0, 1, 1)):
    nw = len(fulls)

    def build(_, refs, send_sems, recv_sems, sem0):
        x, y, c, chips = _place()
        sends, recvs = [], []
        for w in range(nw):
            win = lambda j: _rows_part(_half(_chip_block(refs[w], j, col_sharded[w]), c, True), part)
            for k, (cx, cy) in enumerate(chips[:2]):
                sem = sem0 + 2 * w + k
                sends.append(_copy(send_sems, recv_sems, sem, win(2 * x + y), win(2 * x + y), (cx, cy, c)))
                recvs.append(_copy(send_sems, recv_sems, sem, win(2 * cx + cy), win(2 * cx + cy), (cx, cy, c)))
        return sends, recvs

    return _Side(*_in_place(fulls), 2 * nw, build)


def _gather_forward(fulls, col_sharded, ways=(True, True), siblings=False):
    nw = len(fulls)

    def build(_, refs, send_sems, recv_sems, sem0):
        x, y, c, (cx_, cy_, cd_) = _place()
        sends, recvs = [], []
        for w in range(nw):
            win = lambda chip, core: _half(_chip_block(refs[w], 2 * chip[0] + chip[1], col_sharded[w]), core, True)
            part = lambda ref, p: _rows_part(ref, (p, p + 1, 2))
            for p, (src, to) in enumerate(((cx_, cy_), (cy_, cx_))):
                if ways[p]:
                    sem = sem0 + 4 * w + p
                    sends.append(_copy(send_sems, recv_sems, sem, part(win(src, c), p), part(win(src, c), p), (*to, c)))
                    recvs.append(_copy(send_sems, recv_sems, sem, part(win(cd_, c), p), part(win(cd_, c), p), (*to, c)))
            if siblings:
                for k, chip in enumerate((cx_, cy_)):
                    sem = sem0 + 4 * w + 2 + k
                    sends.append(_copy(send_sems, recv_sems, sem, win(chip, c), win(chip, c), (x, y, 1 - c)))
                    recvs.append(_copy(send_sems, recv_sems, sem, win(chip, 1 - c), win(chip, 1 - c), (x, y, 1 - c)))
        return sends, recvs

    return _Side(*_in_place(fulls), 4 * nw, build)


def _gather_diagonal_d2d(full, col_sharded):
    def build(_, refs, send_sems, recv_sems, sem0):
        x, y, c, chips = _place()
        cx, cy = chips[2]
        win = lambda core: _half(_chip_block(refs[0], 2 * cx + cy, col_sharded), core, True)
        return ([_copy(send_sems, recv_sems, sem0, win(c), win(c), (x, y, 1 - c))],
                [_copy(send_sems, recv_sems, sem0, win(1 - c), win(1 - c), (x, y, 1 - c))])

    return _Side(*_in_place([full]), 1, build)


def _scatter_forward_1(pair):
    r, n = pair.shape
    results = (_sds((2, r, n // N_CHIPS), pair.dtype), _sds((2, r // 2, n // N_CHIPS), pair.dtype))

    def build(refs, res, send_sems, recv_sems, sem0):
        x, y, c, (cx_, cy_, cd_) = _place()
        region = lambda chip, p: _rows_part(_chip_block(refs[0], 2 * chip[0] + chip[1], True), (p, p + 1, 2))
        got, fold = res
        copies = [
            _copy(send_sems, recv_sems, sem0, region(cx_, 1), _rows_part(got.at[0], (1, 2, 2)), (*cx_, c)),
            _copy(send_sems, recv_sems, sem0 + 1, region(cd_, 1), fold.at[0], (*cx_, c)),
            _copy(send_sems, recv_sems, sem0 + 2, region(cy_, 0), _rows_part(got.at[1], (0, 1, 2)), (*cy_, c)),
            _copy(send_sems, recv_sems, sem0 + 3, region(cd_, 0), fold.at[1], (*cy_, c))]
        return copies, copies

    return _Side((pair,), results, {}, 4, build)


def _scatter_forward_2(passed, got):
    def build(refs, res, send_sems, recv_sems, sem0):
        x, y, c, (cx_, cy_, _) = _place()
        copies = [_copy(send_sems, recv_sems, sem0, refs[0].at[0], _rows_part(res[0].at[0], (0, 1, 2)), (*cx_, c)),
                  _copy(send_sems, recv_sems, sem0 + 1, refs[0].at[1], _rows_part(res[0].at[1], (1, 2, 2)), (*cy_, c))]
        return copies, copies

    return _Side((passed, got), (_sds(got.shape, got.dtype),), {1: 0}, 2, build)


def _exchange(side, name):
    return _call(None, side=side, name=name)()[1]


def _swap_halves(grads, col_sharded):
    nw = len(grads)
    out_shapes = [_sds((g.shape[0] // 2, g.shape[1]) if col else (g.shape[0], g.shape[1] // 2), g.dtype)
                  for g, col in zip(grads, col_sharded)]

    def build(g_refs, o_refs, send_sems, recv_sems, sem0):
        x, y, c, _ = _place()
        copies = [_copy(send_sems, recv_sems, sem0 + w, _half(g_refs[w], 1 - c, col_sharded[w]), o_refs[w],
                        (x, y, 1 - c)) for w in range(nw)]
        return copies, copies

    return _Side(tuple(grads), tuple(out_shapes), {}, nw, build)


def _send_to_sibling(arrays):
    nw = len(arrays)

    def build(refs, o_refs, send_sems, recv_sems, sem0):
        x, y, c, _ = _place()
        copies = [_copy(send_sems, recv_sems, sem0 + w, refs[w], o_refs[w], (x, y, 1 - c)) for w in range(nw)]
        return copies, copies

    return _Side(tuple(arrays), tuple(_sds(a.shape, a.dtype) for a in arrays), {}, nw, build)


def _scatter_regions(pairs, col_sharded, part=(0, 1, 1), into=None):
    nw = len(pairs)

    def region_shape(p, col):
        return (p.shape[0], p.shape[1] // N_CHIPS) if col else (p.shape[0] // N_CHIPS, p.shape[1])

    out_shapes = tuple(_sds((N_CHIPS - 1, *region_shape(p, col)), p.dtype) for p, col in zip(pairs, col_sharded))

    def build(refs, o_refs, send_sems, recv_sems, sem0):
        x, y, c, chips = _place()
        copies = []
        for w in range(nw):
            for k, (cx, cy) in enumerate(chips):
                copies.append(_copy(
                    send_sems, recv_sems, sem0 + 3 * w + k,
                    _rows_part(_chip_block(refs[w], 2 * cx + cy, col_sharded[w]), part),
                    _rows_part(o_refs[w].at[k], part), (cx, cy, c)))
        return copies, copies

    if into is None:
        return _Side(tuple(pairs), out_shapes, {}, 3 * nw, build)
    return _Side((*pairs, *into), out_shapes, {nw + w: w for w in range(nw)}, 3 * nw, build)


def _join_halves(halves, col_sharded):
    nw = len(halves)

    def build(_, refs, send_sems, recv_sems, sem0):
        x, y, c, _ = _place()
        sends, recvs = [], []
        for w in range(nw):
            mine, theirs = _half(refs[w], c, col_sharded[w]), _half(refs[w], 1 - c, col_sharded[w])
            sends.append(_copy(send_sems, recv_sems, sem0 + w, mine, mine, (x, y, 1 - c)))
            recvs.append(_copy(send_sems, recv_sems, sem0 + w, theirs, theirs, (x, y, 1 - c)))
        return sends, recvs

    return _Side(*_in_place(halves), nw, build)


def _part_rows(size):
    return -(-size // (SUBLANES * LANES)) * SUBLANES


def _pack(arrays, pad_rows_to=SUBLANES):
    flat = [jnp.pad(a.reshape(-1), (0, _part_rows(a.size) * LANES - a.size)).reshape(-1, LANES) for a in arrays]
    rows = sum(f.shape[0] for f in flat)
    pad = (-rows) % pad_rows_to
    if pad:
        flat.append(jnp.zeros((pad, LANES), F32))
    return jnp.concatenate(flat, axis=0)


def _unpack(packed, shapes):
    out, r = [], 0
    for shp in shapes:
        size = math.prod(shp)
        out.append(packed[r:r + _part_rows(size)].reshape(-1)[:size].reshape(shp))
        r += _part_rows(size)
    return out


def _rope_tables(n_ctx_rows, s):
    rows = s // GRID_W
    row_idx = jnp.repeat(jnp.arange(rows), GRID_W)
    col_idx = jnp.tile(jnp.arange(GRID_W), rows)
    n_freq = LANES // 4
    inv_freq = ROPE_THETA ** (-jnp.arange(n_freq, dtype=F32) / n_freq)
    ang = jnp.concatenate([row_idx.astype(F32)[:, None] * inv_freq, col_idx.astype(F32)[:, None] * inv_freq], axis=-1)
    cos = jnp.repeat(jnp.cos(ang), 2, axis=-1)
    sin = jnp.repeat(jnp.sin(ang), 2, axis=-1) * jnp.tile(jnp.array([-1.0, 1.0], F32), LANES // 2)
    cos = jnp.concatenate([jnp.ones((n_ctx_rows, LANES), F32), cos], axis=0)
    sin = jnp.concatenate([jnp.zeros((n_ctx_rows, LANES), F32), sin], axis=0)
    return cos, sin


WEIGHT_NAMES = ['c_ctx', 'w_mod', 'b_mod', 'g_mix', 'g_mlp', 'w_in', 'q_gain', 'k_gain', 'conv_w', 'conv_b', 'w_rg',
                'b_rg', 'w_ig', 'b_ig', 'lru_lambda', 'w_o_attn', 'w_o_rnn', 'w_out', 'w_up', 'w_down', 'g_final']
BIG = ['w_in', 'w_o_attn', 'w_o_rnn', 'w_out', 'w_up', 'w_down']
BIG_COL_SHARDED = [True, False, False, False, True, False]
GATES = ['w_rg', 'w_ig']
SMALL = ['c_ctx', 'b_mod', 'g_mix', 'g_mlp', 'q_gain', 'k_gain', 'conv_b', 'g_final',
         'conv_w', 'b_rg', 'b_ig', 'lru_lambda']


def kernel(x, c, ctx, c_ctx, w_mod, b_mod, g_mix, g_mlp, w_in, q_gain, k_gain, conv_w, conv_b, w_rg, b_rg, w_ig, b_ig, lru_lambda, w_o_attn, w_o_rnn, w_out, w_up, w_down, g_final, loss_target, m_c_ctx, m_w_mod, m_b_mod, m_g_mix, m_g_mlp, m_w_in, m_q_gain, m_k_gain, m_conv_w, m_conv_b, m_w_rg, m_b_rg, m_w_ig, m_b_ig, m_lru_lambda, m_w_o_attn, m_w_o_rnn, m_w_out, m_w_up, m_w_down, m_g_final, v_c_ctx, v_w_mod, v_b_mod, v_g_mix, v_g_mlp, v_w_in, v_q_gain, v_k_gain, v_conv_w, v_conv_b, v_w_rg, v_b_rg, v_w_ig, v_b_ig, v_lru_lambda, v_w_o_attn, v_w_o_rnn, v_w_out, v_w_up, v_w_down, v_g_final):
    given = dict(locals())
    weights = {n: given[n] for n in WEIGHT_NAMES}
    moms = {n: given["m_" + n] for n in WEIGHT_NAMES}
    vars_ = {n: given["v_" + n] for n in WEIGHT_NAMES}

    s, d = x.shape[1], x.shape[2]
    n_ctx = ctx.shape[1]
    t = n_ctx + s
    hd = q_gain.shape[1]
    assert hd == LANES and w_rg.shape[-1] == LANES
    attn_w = w_o_attn.shape[1] * N_CHIPS
    n_in = w_in.shape[2] * N_CHIPS
    kv_w = (n_in - attn_w - 4 * d) // 2
    group = attn_w // kv_w
    k_off, v_off, xr_off = attn_w, attn_w + kv_w, attn_w + 2 * kv_w
    xg_off, gl_off = xr_off + d, xr_off + 2 * d
    d_mod = N_MOD * d
    tr = _tile(math.gcd(n_ctx, s), 256, 16)
    tcol = _tile(math.gcd(d, xr_off), 1024, LANES)
    xi, yi, ci = lax.axis_index("x"), lax.axis_index("y"), lax.axis_index("c")
    chip = 2 * xi + yi
    core = ci.astype(jnp.int32).reshape(1)

    sharded_small = [conv_w[0], b_rg[0], b_ig[0], lru_lambda[0]]
    pack0 = _pack([c[0]] + sharded_small)
    got0 = _all_gather8(pack0, "gather_small_inputs").reshape(N_DEV, -1, LANES)
    c_all = got0[:, :_part_rows(d)].reshape(N_DEV, -1)[:, :d]
    per_chip = [_unpack(got0[2 * j, _part_rows(d):], [a.shape for a in sharded_small]) for j in range(N_CHIPS)]
    conv_w_f, b_rg_f, b_ig_f, lam_f = (jnp.concatenate([per_chip[j][i] for j in range(N_CHIPS)], axis=-1)
                                       for i in range(4))
    c16 = jnp.concatenate([c_all, c_ctx[None, :], jnp.zeros((16 - N_DEV - 1, d), F32)], axis=0)
    b_mod_shard = lax.dynamic_slice(b_mod, (0, chip * (d_mod // N_CHIPS)), (1, d_mod // N_CHIPS))
    mod_part, silu16 = _mod_fwd(c16, w_mod[0], b_mod_shard)
    mod_all = _all_gather8(mod_part, "gather_mod").reshape(N_DEV, 16, d_mod // N_CHIPS)
    mod16 = jnp.concatenate([mod_all[2 * j] for j in range(N_CHIPS)], axis=-1)
    me = 4 * xi + 2 * yi + ci
    mod_lat = lax.dynamic_slice(mod16, (me, 0), (1, d_mod)).reshape(N_MOD, d)
    mod_ctx = mod16[N_DEV].reshape(N_MOD, d)
    mod4 = jnp.stack([mod_ctx[0], mod_ctx[1], mod_lat[0], mod_lat[1]])
    mod3 = jnp.stack([mod_lat[2], mod_lat[3], mod_lat[4]])
    gate_f = mod_lat[5][None, :]

    chip_arr = chip.astype(jnp.int32).reshape(1)
    own = {n: _cast_into_window(weights[n][0], chip_arr, col, "cast_" + n) for n, col in zip(BIG, BIG_COL_SHARDED)}
    place = jnp.stack([chip, ci]).astype(jnp.int32)
    row3 = [False] * 3

    def pair_sum(n, full, other, col):
        return _add_half(full, other, core, col, "pair_sum_" + n)

    def chip_sum(n, pair, got, col):
        return _sum_regions(pair, got, place, col, "chip_sum_" + n)

    (w_in_f,) = _exchange(_gather_neighbours([own['w_in']], [True]), "gather_w_in_nbr")
    (w_in_f,) = _exchange(_gather_forward([w_in_f], [True], siblings=True), "gather_w_in_fwd")
    (w_in_f,) = _exchange(_gather_diagonal_d2d(w_in_f, True), "gather_w_in_diag")
    cos, sin = _rope_tables(n_ctx, s)
    h = _norm_mod_fwd(ctx[0], x[0], g_mix, mod4, tr)
    z, (w_oa_f, w_or_f, w_out_f, w_up_f) = _matmul(h, w_in_f, name="mm_in", side=_sides(
        _gather_neighbours([own['w_o_attn'], own['w_o_rnn'], own['w_out']], row3),
        _gather_neighbours([own['w_up']], [True], (0, 1, 2))))
    qr, (w_oa_f, w_or_f, w_out_f) = _head_prep_fwd(z, 0, attn_w // LANES, q_gain, cos, sin, tr, "q_prep",
                                                   side=_gather_forward([w_oa_f, w_or_f, w_out_f], row3))
    kr = _head_prep_fwd(z, k_off, kv_w // LANES, k_gain, cos, sin, tr, "k_prep")
    (attn_o, lse), (w_up_f, w_down_f, w_oa_f, w_or_f, w_out_f) = _attn_fwd(
        qr, kr, z, v_off, n_ctx, group, tr, side=_sides(
            _on_same(_gather_neighbours([w_up_f], [True], (1, 2, 2)), _gather_forward([w_up_f], [True], ways=(True, False))),
            _gather_neighbours([own['w_down']], [False]), _gather_d2d([w_oa_f, w_or_f, w_out_f], row3)))
    (xc, a_f, bx_f, a_b, bx_b), (w_up_f,) = _rnn_prep(
        z, xr_off, conv_w_f, conv_b, w_rg[0], b_rg_f, w_ig[0], b_ig_f, lam_f, n_ctx,
        side=_gather_forward([w_up_f], [True], ways=(False, True)))
    (h_f, h_b), (w_up_f, w_down_f) = _scan(
        [(a_f, bx_f, "ctx_lat_up"), (a_b, bx_b, "ctx_lat_down")], post=False, n_ctx_rows=n_ctx, name="scan_fwd",
        side=_sides(_gather_d2d([w_up_f], [True]), _gather_forward([w_down_f], [False])))
    u = _rnn_gate_fwd(h_f, h_b, z, xg_off, n_ctx, tr, tcol)
    y_attn = _matmul(attn_o, w_oa_f, name="mm_o_attn")
    y_rnn = _matmul(u, w_or_f, name="mm_o_rnn")
    mrg = _merge_fwd(y_attn, y_rnn, z, gl_off, n_ctx, tr, tcol)
    mix = _matmul(mrg, w_out_f, name="mm_out")
    x1, h2 = _resid_norm_mod_fwd(x[0], mix, g_mlp, mod3, tr)
    (up, act), (w_down_f,) = _matmul(
        h2, w_up_f, name="mm_up", out_dtype=(F32, BF16), side=_gather_d2d([w_down_f], [False]),
        post=lambda u: (u, jnp.square(jnp.maximum(u, 0.0))))
    down = _matmul(act, w_down_f, name="mm_down")
    dx2, d_down, sums_fin, loss_blk = _final_fwd_bwd(x1, down, loss_target[0], g_final[None, :], gate_f, tr)

    d_up = _matmul(d_down, w_down_f, tb=True, out_dtype=BF16, name="mm_d_up",
                   post=lambda d_act, up_: d_act * 2.0 * jnp.maximum(up_, 0.0), post_args=(up,))
    g_w_down = _matmul(act, d_down, ta=True, out_dtype=BF16, name="mm_g_down")

    def scatter(pairs, cols, lo, hi, into=None):
        return _scatter_regions(pairs, cols, (lo, hi, 8), into)

    dh2, (got,) = _matmul(d_up, w_up_f, tb=True, name="mm_d_h2", side=_swap_halves([g_w_down], [False]))
    p_down = pair_sum('w_down', g_w_down, got, False)
    g_w_up, got_down = _matmul(h2, d_up, ta=True, out_dtype=BF16, name="mm_g_up", side=scatter([p_down], [False], 0, 3))
    (dx1, d_mix, sums2), (got, *got_down) = _norm_mod_bwd2(x1, dh2, dx2, mix, g_mlp, mod3, tr, side=_sides(
        _swap_halves([g_w_up], [True]), scatter([p_down], [False], 3, 4, got_down)))
    p_up = pair_sum('w_up', g_w_up, got, True)
    d_mrg = _matmul(d_mix, w_out_f, tb=True, name="mm_d_mrg")
    g_w_out = _matmul(mrg, d_mix, ta=True, out_dtype=BF16, name="mm_g_out")
    dz = _dz_start(t, n_in, n_ctx)
    d_ya, dz = _merge_bwd(d_mrg, y_attn, z, gl_off, dz, n_ctx, tr, tcol, "merge_bwd_attn")
    d_yr, dz = _merge_bwd(d_mrg, y_rnn, z, gl_off + d, dz, n_ctx, tr, tcol, "merge_bwd_rnn")
    d_o = _matmul(d_ya, w_oa_f, tb=True, out_dtype=BF16, name="mm_d_o")
    g_w_oa = _matmul(attn_o, d_ya, ta=True, out_dtype=BF16, name="mm_g_o_attn")
    d_u = _matmul(d_yr, w_or_f, tb=True, name="mm_d_u")
    g_w_or = _matmul(u, d_yr, ta=True, out_dtype=BF16, name="mm_g_o_rnn")
    d_rnn, dz = _rnn_gate_bwd(d_u, h_f, h_b, z, xg_off, dz, n_ctx, tr, tcol)
    gs_f, gs_b = _scan([(a_f, d_rnn, "lat_ctx_down"), (a_b, d_rnn, "lat_ctx_up")], post=True, n_ctx_rows=n_ctx,
                       name="scan_bwd")
    o_names, o_grads = ['w_o_attn', 'w_o_rnn', 'w_out'], [g_w_oa, g_w_or, g_w_out]
    (dz, g_w_rg, g_w_ig, sums_rnn), (got_down, *got_o) = _rnn_bwd(
        z, xr_off, xc, gs_f, gs_b, h_f, h_b, conv_w_f, w_rg[0], b_rg_f, w_ig[0], b_ig_f, lam_f, dz, n_ctx,
        side=_sides(scatter([p_down], [False], 4, 8, got_down), _swap_halves(o_grads, row3)))
    hs_down = chip_sum('w_down', p_down, got_down, False)
    p_o = [pair_sum(n, g, o, False) for n, g, o in zip(o_names, o_grads, got_o)]
    gate_cols = 8 * LANES if g_w_rg.size % (8 * LANES * N_CHIPS * 16) == 0 else 2 * LANES
    gate_rows = g_w_rg.size // gate_cols
    gate_grads = [g_w_rg.reshape(gate_rows, gate_cols), g_w_ig.reshape(gate_rows, gate_cols)]
    (dq, dk, dz), (got_up, got_oa, got_rg, got_ig, gs_down) = _attn_bwd(
        qr, kr, z, v_off, d_o, attn_o, lse, dz, n_ctx, group, tr, side=_sides(
            _scatter_regions([p_up], [True]), _scatter_regions(p_o[:1], row3[:1]), _swap_halves(gate_grads, row3[:2]),
            _join_halves([hs_down], [False])))
    hs_up = chip_sum('w_up', p_up, got_up, True)
    hs_oa = chip_sum('w_o_attn', p_o[0], got_oa, False)
    p_gate = [pair_sum(n, g, o, False) for n, g, o in zip(GATES, gate_grads, (got_rg, got_ig))]
    dz, g_q_gain = _head_prep_bwd(z, 0, attn_w // LANES, q_gain, cos, sin, dq, n_ctx, tr, dz, "q_prep_bwd")
    dz, g_k_gain = _head_prep_bwd(z, k_off, kv_w // LANES, k_gain, cos, sin, dk, 0, tr, dz, "k_prep_bwd")
    half = d // 2
    h_sibling = lax.dynamic_slice(h, (0, (1 - ci) * half), (t, half))
    h_own = lax.dynamic_slice(h, (0, ci * half), (t, half))
    late = p_o[1:] + p_gate
    g_sibling, (got_rg, got_ig, *got_late) = _matmul(
        h_sibling, dz, ta=True, out_dtype=BF16, name="mm_g_in_sibling", side=_sides(
            _scatter_regions(p_gate, row3[:2]), scatter(p_o[1:], row3[:2], 0, 4)))
    g_own, (got_or, got_out, got, gs_up, gs_oa) = _matmul(
        h_own, dz, ta=True, out_dtype=BF16, name="mm_g_in_own", side=_sides(
            scatter(p_o[1:], row3[:2], 4, 8, got_late), _send_to_sibling([g_sibling]), _join_halves([hs_up], [True]),
            _join_halves([hs_oa], [False])))
    hs_late = [chip_sum(n, p, o, False) for n, p, o in zip(o_names[1:] + GATES, late, (got_or, got_out, got_rg, got_ig))]
    p_in = _add_pair(g_own, got, "pair_sum_w_in")
    dh, (got_in, fold_in, gs_or, gs_out) = _matmul(dz, w_in_f, tb=True, name="mm_d_h", side=_sides(
        _scatter_forward_1(p_in), _join_halves(hs_late[:2], row3[:2])))
    nbr_chips = jnp.stack([2 * (1 - xi) + yi, 2 * xi + 1 - yi]).astype(jnp.int32)
    passed = _fold_forwarded(p_in, fold_in, nbr_chips, "fold_w_in")
    (got_in,) = _exchange(_scatter_forward_2(passed, got_in), "scatter_w_in_fwd")
    grad_x, sums1 = _norm_mod_bwd1(ctx[0], x[0], dh, dx1, g_mix, mod4, tr)

    zeros_d = jnp.zeros((d,), F32)
    dmod_lat = jnp.concatenate([sums1[0], sums1[1], sums2[3], sums2[0], sums2[1], sums_fin[1]])
    dmod_ctx = jnp.concatenate([sums1[3], sums1[4]] + [zeros_d] * 4)
    small_parts = [dmod_lat, dmod_ctx, loss_blk[0, 0:1], sums1[2] + sums1[5], sums2[2], g_q_gain[0], g_k_gain[0],
                   sums_rnn[10], sums_fin[0], sums_rnn[6:10], sums_rnn[0:2], sums_rnn[2:4], sums_rnn[4:6]]
    pack1 = _pack(small_parts)
    got1 = _all_gather8(pack1, "gather_small_grads").reshape(N_DEV, -1, LANES)
    tot1 = _sum_leading(got1, "sum_small_grads")
    part_shapes = [a.shape for a in small_parts]
    (s_dmod_lat, s_dmod_ctx, s_loss, g_g_mix, g_g_mlp, g_q_gain, g_k_gain, g_conv_b, g_g_final,
     g_conv_w_f, g_b_rg_f, g_b_ig_f, g_lam_f) = _unpack(tot1, part_shapes)
    loss = s_loss[0]
    g_b_mod = (s_dmod_lat + s_dmod_ctx)[None, :]
    n_mod_rows = _part_rows(d_mod)
    dmod16 = jnp.concatenate([got1[:, :n_mod_rows].reshape(N_DEV, -1)[:, :d_mod], s_dmod_ctx[None, :],
                              jnp.zeros((16 - N_DEV - 1, d_mod), F32)], axis=0)
    dmod16_shard = lax.dynamic_slice(dmod16, (0, chip * (d_mod // N_CHIPS)), (16, d_mod // N_CHIPS))
    g_w_mod = _matmul(silu16, dmod16_shard, ta=True, name="mm_g_mod")
    dsilu_part = _matmul(dmod16_shard[N_DEV:], w_mod[0], tb=True, name="mm_d_silu")
    dsilu_all = _all_gather8(dsilu_part, "gather_d_silu").reshape(N_DEV, 8, d)
    g_c_ctx = _c_ctx_grad(dsilu_all, c_ctx[None, :])[0]

    def shard_of(full):
        w = full.shape[-1] // N_CHIPS
        return lax.dynamic_slice(full, (0, chip * w), (full.shape[0], w))

    grads = {
        'c_ctx': g_c_ctx, 'b_mod': g_b_mod, 'g_mix': g_g_mix[None, :], 'g_mlp': g_g_mlp[None, :],
        'q_gain': g_q_gain[None, :], 'k_gain': g_k_gain[None, :], 'conv_b': g_conv_b[None, :],
        'g_final': g_g_final,
        'conv_w': shard_of(g_conv_w_f)[None], 'b_rg': shard_of(g_b_rg_f)[None], 'b_ig': shard_of(g_b_ig_f)[None],
        'lru_lambda': shard_of(g_lam_f)[None], 'w_mod': g_w_mod[None],
    }

    delta, new_m, new_v = {}, {}, {}

    def adamw(n):
        shp = weights[n].shape
        as2d = (lambda a: a[0]) if n not in GATES else (lambda a: a.reshape(-1, LANES))
        dl, nm, nv, *g = _adamw(as2d(weights[n]), as2d(grads[n]), as2d(moms[n]), as2d(vars_[n]), "adamw_" + n,
                                copy_grad=n in BIG)
        delta[n], new_m[n], new_v[n] = dl.reshape(shp), nm.reshape(shp), nv.reshape(shp)
        if g:
            grads[n] = g[0].reshape(shp)

    hs_in = chip_sum('w_in', p_in, got_in, True)
    (gs_in,) = _exchange(_join_halves([hs_in], [True]), "join_w_in")
    for n, g in zip(['w_in', 'w_o_attn', 'w_o_rnn', 'w_out', 'w_up', 'w_down'], [gs_in, gs_oa, gs_or, gs_out, gs_up, gs_down]):
        grads[n] = g[None]
    half_cols = gate_cols // 2
    mine = jnp.concatenate([lax.dynamic_slice(hs, (0, ci * half_cols), (gate_rows // N_CHIPS, half_cols))
                            for hs in hs_late[2:]], axis=0)
    gate_all = _all_gather8(mine, "gather_gate_grads")
    gate_all = gate_all.reshape(N_CHIPS, 2, len(GATES), gate_rows // N_CHIPS, half_cols)
    for i, n in enumerate(GATES):
        grads[n] = jnp.moveaxis(gate_all[:, :, i], 1, 2).reshape(weights[n].shape)
    for n in ['w_mod'] + BIG + GATES:
        adamw(n)
    small_shapes = [weights[n].shape for n in SMALL]
    packed = [_pack([src[n] for n in SMALL], 512) for src in (weights, grads, moms, vars_)]
    outs = _adamw(*packed, "adamw_small", copy_grad=False)
    for res, out in zip((delta, new_m, new_v), outs):
        for n, a in zip(SMALL, _unpack(out, small_shapes)):
            res[n] = a
    return (loss, grad_x[None], *[grads[n] for n in WEIGHT_NAMES], *[delta[n] for n in WEIGHT_NAMES],
            *[new_m[n] for n in WEIGHT_NAMES], *[new_v[n] for n in WEIGHT_NAMES])
```

```python
import functools
import math
from typing import Callable, NamedTuple

import jax
import jax.numpy as jnp
from jax import lax
from jax.experimental import pallas as pl
from jax.experimental.pallas import tpu as pltpu

F32 = jnp.float32
BF16 = jnp.bfloat16
MESH_ID = pl.DeviceIdType.MESH
ANY = pl.BlockSpec(memory_space=pl.ANY)

NORM_EPS = 1e-6
LRU_C = 8.0
GRID_W = 64
ROPE_THETA = 10000.0
N_MOD = 6
CONV_WIDTH = 4
ADAM_LR = 0.001
ADAM_B1 = 0.9
ADAM_B2 = 0.999
ADAM_EPS = 1e-08
ADAM_WD = 0.01
ADAM_STEP = 10

LANES = 128
SUBLANES = 8
V7X_VMEM_LIMIT = 48 * 1024 * 1024
WIDE_TILE = 11 * LANES
N_CHIPS = 4
N_DEV = 8
GELU_C = math.sqrt(2.0 / math.pi)
GELU_A = 0.044715


def _tile(dim, pref, align):
    t = min(pref, dim)
    t -= t % align
    while t >= align:
        if dim % t == 0:
            return t
        t -= align
    return dim


def _params(*sem):
    return pltpu.CompilerParams(dimension_semantics=sem, vmem_limit_bytes=V7X_VMEM_LIMIT)


def _sds(shape, dtype=F32):
    return jax.ShapeDtypeStruct(shape, dtype)


class _Side(NamedTuple):
    operands: tuple
    results: tuple
    aliases: dict
    n_sems: int
    build: Callable


def _sides(*sides):
    ops, res, aliases, spans, n = [], [], {}, [], 0
    for s in sides:
        spans.append((len(ops), len(res), n))
        aliases.update({len(ops) + i: len(res) + j for i, j in s.aliases.items()})
        ops += s.operands
        res += s.results
        n += s.n_sems

    def build(op_refs, res_refs, send_sems, recv_sems, sem0):
        sends, recvs = [], []
        for s, (o, r, k) in zip(sides, spans):
            a, b = s.build(op_refs[o:o + len(s.operands)], res_refs[r:r + len(s.results)], send_sems, recv_sems,
                           sem0 + k)
            sends += a
            recvs += b
        return sends, recvs

    return _Side(tuple(ops), tuple(res), aliases, n, build)


def _on_same(*sides):
    def build(ops, res, send_sems, recv_sems, sem0):
        sends, recvs = [], []
        for s in sides:
            a, b = s.build(ops, res, send_sems, recv_sems, sem0)
            sends += a
            recvs += b
            sem0 += s.n_sems
        return sends, recvs

    return _Side(sides[0].operands, sides[0].results, sides[0].aliases, sum(s.n_sems for s in sides), build)


def _call(body, *, side=None, sem=(), grid=(), in_specs=(), out_specs=(), out_shape=(), scratch_shapes=(), **kw):
    if side is None:
        return pl.pallas_call(body, grid=grid, in_specs=list(in_specs), out_specs=out_specs, out_shape=out_shape,
                              scratch_shapes=list(scratch_shapes), compiler_params=_params(*sem), **kw)
    aliases = kw.pop("input_output_aliases", {})
    many = isinstance(out_shape, (list, tuple))
    out_specs_l, out_shape_l = (list(out_specs), list(out_shape)) if many else ([out_specs], [out_shape])
    n_in, n_out, n_scr = len(in_specs), len(out_shape_l), len(scratch_shapes)
    n_op, n_res = len(side.operands), len(side.results)

    def hosted(*refs):
        ins, ops = refs[:n_in], refs[n_in:n_in + n_op]
        outs = refs[n_in + n_op:n_in + n_op + n_out]
        res = refs[n_in + n_op + n_out:n_in + n_op + n_out + n_res]
        scr = refs[n_in + n_op + n_out + n_res:-2]
        send_sems, recv_sems = refs[-2:]

        def start():
            for cp in side.build(ops, res, send_sems, recv_sems, 0)[0]:
                cp.start()

        def finish():
            sends, recvs = side.build(ops, res, send_sems, recv_sems, 0)
            for cp in recvs:
                cp.wait_recv()
            for cp in sends:
                cp.wait_send()

        if not grid:
            start()
            finish()
            return
        ids = [pl.program_id(a) for a in range(len(grid))]
        first = functools.reduce(jnp.logical_and, [i == 0 for i in ids])
        last = functools.reduce(jnp.logical_and, [i == g - 1 for i, g in zip(ids, grid)])
        pl.when(first)(start)
        body(*ins, *outs, *scr)
        pl.when(last)(finish)

    def run(*args):
        got = pl.pallas_call(
            hosted, grid=grid, in_specs=[*in_specs, *[ANY] * n_op], out_specs=[*out_specs_l, *[ANY] * n_res],
            out_shape=[*out_shape_l, *side.results],
            scratch_shapes=[*scratch_shapes, pltpu.SemaphoreType.DMA((side.n_sems,)),
                            pltpu.SemaphoreType.DMA((side.n_sems,))],
            input_output_aliases={**aliases, **{n_in + i: n_out + j for i, j in side.aliases.items()}},
            compiler_params=_params(*["arbitrary"] * len(grid)), **kw)(*args, *side.operands)
        own = list(got[:n_out]) if many else got[0]
        return own, list(got[n_out:])

    return run


def _matmul(a, b, *, ta=False, tb=False, out_dtype=F32, name, tm=1024, tn=1024, tk=2816, side=None, post=None,
            post_args=()):
    k_dim, m = a.shape if ta else a.shape[::-1]
    n, k2 = b.shape if tb else b.shape[::-1]
    assert k_dim == k2, (a.shape, b.shape, ta, tb)
    tm = _tile(m, tm, LANES if ta else 16)
    tn = _tile(n, tn, 16 if tb else LANES)
    tk = _tile(k_dim, tk, LANES)
    nk = k_dim // tk
    dims = (((0 if ta else 1,), (1 if tb else 0,)), ((), ()))
    if nk == 1:
        several = isinstance(out_dtype, tuple)

        def whole(a_ref, b_ref, *rest):
            acc = lax.dot_general(a_ref[...].astype(BF16), b_ref[...].astype(BF16), dims, preferred_element_type=F32)
            outs = rest[len(post_args):]
            if post is not None:
                acc = post(acc, *[r[...] for r in rest[:len(post_args)]])
            for o_ref, val in zip(outs, acc if several else (acc,)):
                o_ref[...] = val.astype(o_ref.dtype)

        a_spec = pl.BlockSpec((tk, tm), lambda i, j: (0, i)) if ta else pl.BlockSpec((tm, tk), lambda i, j: (i, 0))
        b_spec = pl.BlockSpec((tn, tk), lambda i, j: (j, 0)) if tb else pl.BlockSpec((tk, tn), lambda i, j: (0, j))
        o_spec = pl.BlockSpec((tm, tn), lambda i, j: (i, j))
        return _call(
            whole, side=side, name=name, grid=(m // tm, n // tn), in_specs=[a_spec, b_spec] + [o_spec] * len(post_args),
            out_specs=[o_spec] * len(out_dtype) if several else o_spec,
            out_shape=[_sds((m, n), dt) for dt in out_dtype] if several else _sds((m, n), out_dtype),
            sem=("parallel", "parallel"),
        )(a, b, *post_args)
    assert post is None

    def body(a_ref, b_ref, o_ref, acc_ref):
        k = pl.program_id(2)

        @pl.when(k == 0)
        def _():
            acc_ref[...] = jnp.zeros_like(acc_ref)

        acc_ref[...] += lax.dot_general(a_ref[...].astype(BF16), b_ref[...].astype(BF16), dims,
                                        preferred_element_type=F32)

        @pl.when(k == nk - 1)
        def _():
            o_ref[...] = acc_ref[...].astype(o_ref.dtype)

    a_spec = pl.BlockSpec((tk, tm), lambda i, j, k: (k, i)) if ta else pl.BlockSpec((tm, tk), lambda i, j, k: (i, k))
    b_spec = pl.BlockSpec((tn, tk), lambda i, j, k: (j, k)) if tb else pl.BlockSpec((tk, tn), lambda i, j, k: (k, j))
    return _call(
        body, side=side, name=name, grid=(m // tm, n // tn, nk), in_specs=[a_spec, b_spec],
        out_specs=pl.BlockSpec((tm, tn), lambda i, j, k: (i, j)), out_shape=_sds((m, n), out_dtype),
        scratch_shapes=[pltpu.VMEM((tm, tn), F32)], sem=("parallel", "parallel", "arbitrary"),
    )(a, b)


def _silu(x):
    return x * jax.nn.sigmoid(x)


def _gelu(x):
    return 0.5 * x * (1.0 + jnp.tanh(GELU_C * (x + GELU_A * x * x * x)))


def _gelu_grad(x):
    t = jnp.tanh(GELU_C * (x + GELU_A * x * x * x))
    return 0.5 * (1.0 + t) + 0.5 * x * (1.0 - t * t) * GELU_C * (1.0 + 3.0 * GELU_A * x * x)


def _expm1_nonpos(x):
    series = x * (1.0 + x * (1.0 / 2 + x * (1.0 / 6 + x * (1.0 / 24 + x * (1.0 / 120 + x * (1.0 / 720 + x / 5040))))))
    return jnp.where(x > -0.25, series, jnp.exp(x) - 1.0)


def _softplus(x):
    return jnp.maximum(x, 0.0) + jnp.log1p(jnp.exp(-jnp.abs(x)))


def _rms_stats(x):
    return lax.rsqrt(jnp.mean(x * x, axis=-1, keepdims=True) + NORM_EPS)


def _rms_bwd(dxhat, xhat, rstd):
    return rstd * (dxhat - xhat * jnp.mean(dxhat * xhat, axis=-1, keepdims=True))


def _colsum(v):
    return jnp.sum(v, axis=0, keepdims=True)


def _mod_fwd(c16, w_mod, b_mod_shard):
    r, d = c16.shape
    n = w_mod.shape[1]
    tn = _tile(n, 512, LANES)

    def body(c_ref, w_ref, b_ref, o_ref, s_ref):
        s = _silu(c_ref[...])
        s_ref[...] = s
        o_ref[...] = jnp.dot(s.astype(BF16), w_ref[...].astype(BF16), preferred_element_type=F32) + b_ref[...]

    return pl.pallas_call(
        body, name="mod_fwd", grid=(n // tn,),
        in_specs=[pl.BlockSpec((r, d), lambda j: (0, 0)), pl.BlockSpec((d, tn), lambda j: (0, j)),
                  pl.BlockSpec((1, tn), lambda j: (0, j))],
        out_specs=[pl.BlockSpec((r, tn), lambda j: (0, j)), pl.BlockSpec((r, d), lambda j: (0, 0))],
        out_shape=[_sds((r, n)), _sds((r, d))], compiler_params=_params("arbitrary"),
    )(c16, w_mod, b_mod_shard)


def _c_ctx_grad(parts, c_ctx_row):
    d = c_ctx_row.shape[1]

    def body(p_ref, c_ref, o_ref):
        tot = p_ref[0, 0:1, :]
        for chip in range(1, N_CHIPS):
            tot = tot + p_ref[2 * chip, 0:1, :]
        c = c_ref[...]
        sg = jax.nn.sigmoid(c)
        o_ref[...] = tot * (sg * (1.0 + c * (1.0 - sg)))

    return pl.pallas_call(body, name="c_ctx_grad", out_shape=_sds((1, d)), compiler_params=_params())(parts, c_ctx_row)


def _token_specs(n_ctx_rows, d, tr):
    nctx = n_ctx_rows // tr
    return (pl.BlockSpec((tr, d), lambda i: (jnp.minimum(i, nctx - 1), 0)),
            pl.BlockSpec((tr, d), lambda i: (jnp.maximum(i - nctx, 0), 0)))


def _norm_mod_fwd(ctx, x, g, mod4, tr):
    (n_ctx_rows, d), s = ctx.shape, x.shape[0]
    t = n_ctx_rows + s
    nctx = n_ctx_rows // tr

    def body(c_ref, x_ref, g_ref, mod_ref, h_ref):
        is_ctx = pl.program_id(0) < nctx
        x = jnp.where(is_ctx, c_ref[...], x_ref[...])
        n = x * _rms_stats(x) * g_ref[...]
        sh = jnp.where(is_ctx, mod_ref[0:1, :], mod_ref[2:3, :])
        sc = jnp.where(is_ctx, mod_ref[1:2, :], mod_ref[3:4, :])
        h_ref[...] = (n * (1.0 + sc) + sh).astype(BF16)

    return pl.pallas_call(
        body, name="norm_mod_fwd", grid=(t // tr,),
        in_specs=[*_token_specs(n_ctx_rows, d, tr), pl.BlockSpec((1, d), lambda i: (0, 0)),
                  pl.BlockSpec((4, d), lambda i: (0, 0))],
        out_specs=pl.BlockSpec((tr, d), lambda i: (i, 0)), out_shape=_sds((t, d), BF16),
        compiler_params=_params("parallel"),
    )(ctx, x, g, mod4)


def _norm_mod_bwd1(ctx, x, dh, dx1, g, mod4, tr, side=None):
    (n_ctx_rows, d), s = ctx.shape, x.shape[0]
    t = n_ctx_rows + s
    nctx = n_ctx_rows // tr

    def body(c_ref, x_ref, dh_ref, dx1_ref, g_ref, mod_ref, dx_ref, sums_ref):
        i = pl.program_id(0)
        is_ctx = i < nctx

        @pl.when(i == 0)
        def _():
            sums_ref[...] = jnp.zeros_like(sums_ref)

        x = jnp.where(is_ctx, c_ref[...], x_ref[...])
        dh_ = dh_ref[...]
        rstd = _rms_stats(x)
        xhat = x * rstd
        gg = g_ref[...]
        sc = jnp.where(is_ctx, mod_ref[1:2, :], mod_ref[3:4, :])
        dxhat = dh_ * (1.0 + sc) * gg
        dx_ref[...] = dx1_ref[...] + _rms_bwd(dxhat, xhat, rstd)
        part = [_colsum(dh_), _colsum(dh_ * xhat * gg), _colsum(dh_ * (1.0 + sc) * xhat)]

        @pl.when(is_ctx)
        def _():
            for k, row in enumerate(part):
                sums_ref[3 + k:4 + k, :] += row

        @pl.when(jnp.logical_not(is_ctx))
        def _():
            for k, row in enumerate(part):
                sums_ref[k:k + 1, :] += row

    lat = lambda i: (jnp.maximum(i - nctx, 0), 0)
    return _call(
        body, side=side, name="norm_mod_bwd1", grid=(t // tr,),
        in_specs=[*_token_specs(n_ctx_rows, d, tr), pl.BlockSpec((tr, d), lambda i: (i, 0)),
                  pl.BlockSpec((tr, d), lat), pl.BlockSpec((1, d), lambda i: (0, 0)),
                  pl.BlockSpec((4, d), lambda i: (0, 0))],
        out_specs=[pl.BlockSpec((tr, d), lat), pl.BlockSpec((8, d), lambda i: (0, 0))],
        out_shape=[_sds((s, d)), _sds((8, d))], sem=("arbitrary",),
    )(ctx, x, dh, dx1, g, mod4)


def _resid_norm_mod_fwd(x, mix, g, mod3, tr):
    s, d = x.shape

    def body(x_ref, mix_ref, g_ref, mod_ref, x1_ref, h_ref):
        x1 = x_ref[...] + mod_ref[0:1, :] * mix_ref[...]
        x1_ref[...] = x1
        n = x1 * _rms_stats(x1) * g_ref[...]
        h_ref[...] = (n * (1.0 + mod_ref[2:3, :]) + mod_ref[1:2, :]).astype(BF16)

    row = pl.BlockSpec((tr, d), lambda i: (i, 0))
    return pl.pallas_call(
        body, name="resid_norm_mod_fwd", grid=(s // tr,),
        in_specs=[row, row, pl.BlockSpec((1, d), lambda i: (0, 0)), pl.BlockSpec((3, d), lambda i: (0, 0))],
        out_specs=[row, row], out_shape=[_sds((s, d)), _sds((s, d), BF16)], compiler_params=_params("parallel"),
    )(x, mix, g, mod3)


def _norm_mod_bwd2(x1, dh2, dx2, mix, g, mod3, tr, side=None):
    s, d = x1.shape

    def body(x_ref, dh_ref, dx2_ref, mix_ref, g_ref, mod_ref, dx1_ref, dmix_ref, sums_ref):
        @pl.when(pl.program_id(0) == 0)
        def _():
            sums_ref[...] = jnp.zeros_like(sums_ref)

        x = x_ref[...]
        dh_ = dh_ref[...]
        rstd = _rms_stats(x)
        xhat = x * rstd
        gg = g_ref[...]
        sc = mod_ref[2:3, :]
        dx1 = dx2_ref[...] + _rms_bwd(dh_ * (1.0 + sc) * gg, xhat, rstd)
        dx1_ref[...] = dx1
        dmix_ref[...] = (dx1 * mod_ref[0:1, :]).astype(BF16)
        part = [_colsum(dh_), _colsum(dh_ * xhat * gg), _colsum(dh_ * (1.0 + sc) * xhat), _colsum(dx1 * mix_ref[...])]
        for k, row in enumerate(part):
            sums_ref[k:k + 1, :] += row

    row = pl.BlockSpec((tr, d), lambda i: (i, 0))
    return _call(
        body, side=side, name="norm_mod_bwd2", grid=(s // tr,),
        in_specs=[row, row, row, row, pl.BlockSpec((1, d), lambda i: (0, 0)), pl.BlockSpec((3, d), lambda i: (0, 0))],
        out_specs=[row, row, pl.BlockSpec((8, d), lambda i: (0, 0))],
        out_shape=[_sds((s, d)), _sds((s, d), BF16), _sds((8, d))], sem=("arbitrary",),
    )(x1, dh2, dx2, mix, g, mod3)


def _final_fwd_bwd(x1, down, target, g_final, gate, tr):
    s, d = x1.shape

    def body(x1_ref, down_ref, t_ref, g_ref, gate_ref, dx2_ref, ddown_ref, sums_ref, loss_ref):
        @pl.when(pl.program_id(0) == 0)
        def _():
            sums_ref[...] = jnp.zeros_like(sums_ref)
            loss_ref[...] = jnp.zeros_like(loss_ref)

        down_ = down_ref[...]
        gate_ = gate_ref[...]
        x2 = x1_ref[...] + gate_ * down_
        rstd = _rms_stats(x2)
        xhat = x2 * rstd
        gg = g_ref[...]
        err = xhat * gg - t_ref[...]
        loss_ref[...] += 0.5 * jnp.sum(jnp.mean(err * err, axis=-1, keepdims=True))
        dy = err * (1.0 / d)
        dx2 = _rms_bwd(dy * gg, xhat, rstd)
        dx2_ref[...] = dx2
        ddown_ref[...] = (dx2 * gate_).astype(BF16)
        sums_ref[0:1, :] += _colsum(dy * xhat)
        sums_ref[1:2, :] += _colsum(dx2 * down_)

    row = pl.BlockSpec((tr, d), lambda i: (i, 0))
    vec = pl.BlockSpec((1, d), lambda i: (0, 0))
    return pl.pallas_call(
        body, name="final_fwd_bwd", grid=(s // tr,), in_specs=[row, row, row, vec, vec],
        out_specs=[row, row, pl.BlockSpec((8, d), lambda i: (0, 0)), pl.BlockSpec((8, LANES), lambda i: (0, 0))],
        out_shape=[_sds((s, d)), _sds((s, d), BF16), _sds((8, d)), _sds((8, LANES))],
        compiler_params=_params("arbitrary"),
    )(x1, down, target, g_final, gate)


def _swap_pairs(v):
    lane = lax.broadcasted_iota(jnp.int32, v.shape, 1)
    return jnp.where(lane % 2 == 0, pltpu.roll(v, LANES - 1, 1), pltpu.roll(v, 1, 1))


def _head_prep_fwd(z, col_off, n_heads, gain, cos, sin, tr, name, side=None):
    t = z.shape[0]
    per = math.gcd(4, n_heads, col_off // LANES)
    w = per * LANES
    hb = col_off // w

    def body(z_ref, g_ref, cos_ref, sin_ref, o_ref):
        for hh in range(per):
            cols = slice(hh * LANES, (hh + 1) * LANES)
            x = z_ref[:, cols]
            y = x * _rms_stats(x) * g_ref[...]
            o_ref[:, cols] = (y * cos_ref[...] + _swap_pairs(y) * sin_ref[...]).astype(BF16)

    tab = pl.BlockSpec((tr, LANES), lambda i, j: (i, 0))
    return _call(
        body, side=side, name=name, grid=(t // tr, n_heads // per),
        in_specs=[pl.BlockSpec((tr, w), lambda i, j: (i, hb + j)), pl.BlockSpec((1, LANES), lambda i, j: (0, 0)),
                  tab, tab],
        out_specs=pl.BlockSpec((tr, w), lambda i, j: (i, j)), out_shape=_sds((t, n_heads * LANES), BF16),
        sem=("parallel", "parallel"),
    )(z, gain, cos, sin)


def _dz_start(t, n_in, n_ctx_rows):
    tcol = _tile(n_in, 1024, LANES)

    def body(o_ref):
        o_ref[...] = jnp.zeros_like(o_ref)

    return pl.pallas_call(body, name="dz_start", grid=(n_in // tcol,),
                          out_specs=pl.BlockSpec((n_ctx_rows, tcol), lambda j: (0, j)), out_shape=_sds((t, n_in), BF16),
                          compiler_params=_params("parallel"))()


def _head_prep_bwd(z, col_off, n_heads, gain, cos, sin, dout, row_off, tr, dz, name, side=None):
    r = dout.shape[0]
    per = math.gcd(4, n_heads, col_off // LANES)
    w = per * LANES
    hb = col_off // w
    rb = row_off // tr

    def body(z_ref, g_ref, cos_ref, sin_ref, d_ref, _, dz_ref, dg_ref):
        @pl.when(jnp.logical_and(pl.program_id(0) == 0, pl.program_id(1) == 0))
        def _():
            dg_ref[...] = jnp.zeros_like(dg_ref)

        for hh in range(per):
            cols = slice(hh * LANES, (hh + 1) * LANES)
            x = z_ref[:, cols]
            rstd = _rms_stats(x)
            xhat = x * rstd
            dd = d_ref[:, cols]
            dy = dd * cos_ref[...] - _swap_pairs(dd) * sin_ref[...]
            dg_ref[0:1, :] += _colsum(dy * xhat)
            dz_ref[:, cols] = _rms_bwd(dy * g_ref[...], xhat, rstd).astype(BF16)

    tab = pl.BlockSpec((tr, LANES), lambda i, j: (rb + i, 0))
    window = pl.BlockSpec((tr, w), lambda i, j: (rb + i, hb + j))
    return _call(
        body, side=side, name=name, grid=(r // tr, n_heads // per),
        in_specs=[window, pl.BlockSpec((1, LANES), lambda i, j: (0, 0)), tab, tab,
                  pl.BlockSpec((tr, w), lambda i, j: (i, j)), ANY],
        out_specs=[window, pl.BlockSpec((8, LANES), lambda i, j: (0, 0))],
        out_shape=[_sds(dz.shape, BF16), _sds((8, LANES))], input_output_aliases={5: 0}, sem=("arbitrary", "arbitrary"),
    )(z, gain, cos, sin, dout, dz)


def _attn_fwd(qr, kr, z, v_off, n_ctx_rows, group, tq, side=None):
    t, kvw = kr.shape
    s = t - n_ctx_rows
    n_kv = kvw // LANES
    scale = LANES ** -0.5
    qb0 = n_ctx_rows // tq
    vb = v_off // LANES

    def body(q_ref, k_ref, v_ref, o_ref, lse_ref):
        k = k_ref[...]
        v = v_ref[...].astype(BF16)
        lse_ref[...] = jnp.zeros_like(lse_ref)
        for g in range(group):
            cols = slice(g * LANES, (g + 1) * LANES)
            sc = lax.dot_general(q_ref[:, cols], k, (((1,), (1,)), ((), ())), preferred_element_type=F32) * scale
            m = jnp.max(sc, axis=-1, keepdims=True)
            e = jnp.exp(sc - m)
            l = jnp.sum(e, axis=-1, keepdims=True)
            p = e * (1.0 / l)
            o_ref[:, cols] = jnp.dot(p.astype(BF16), v, preferred_element_type=F32).astype(BF16)
            lse_ref[:, g:g + 1] = m + jnp.log(l)

    return _call(
        body, side=side, name="attn_fwd", grid=(n_kv, s // tq),
        in_specs=[pl.BlockSpec((tq, group * LANES), lambda h, i: (qb0 + i, h)),
                  pl.BlockSpec((t, LANES), lambda h, i: (0, h)), pl.BlockSpec((t, LANES), lambda h, i: (0, vb + h))],
        out_specs=[pl.BlockSpec((tq, group * LANES), lambda h, i: (i, h)), pl.BlockSpec((tq, LANES), lambda h, i: (i, h))],
        out_shape=[_sds((s, n_kv * group * LANES), BF16), _sds((s, kvw))], sem=("parallel", "parallel"),
    )(qr, kr, z)


def _attn_bwd(qr, kr, z, v_off, d_o, attn_o, lse, dz, n_ctx_rows, group, tq, side=None):
    t, kvw = kr.shape
    s = t - n_ctx_rows
    n_kv = kvw // LANES
    scale = LANES ** -0.5
    qb0 = n_ctx_rows // tq
    vb = v_off // LANES
    n_q_blocks = s // tq
    tn_dims = (((0,), (0,)), ((), ()))
    nt_dims = (((1,), (1,)), ((), ()))

    def body(q_ref, k_ref, v_ref, do_ref, o_ref, lse_ref, _, dq_ref, dk_ref, dz_ref, dv_ref):
        @pl.when(pl.program_id(1) == 0)
        def _():
            dk_ref[...] = jnp.zeros_like(dk_ref)
            dv_ref[...] = jnp.zeros_like(dv_ref)

        k = k_ref[...]
        v = v_ref[...].astype(BF16)
        for g in range(group):
            cols = slice(g * LANES, (g + 1) * LANES)
            q = q_ref[:, cols]
            do_ = do_ref[:, cols]
            row_dot = jnp.sum(do_.astype(F32) * o_ref[:, cols].astype(F32), axis=-1, keepdims=True)
            sc = lax.dot_general(q, k, nt_dims, preferred_element_type=F32)
            p = jnp.exp(sc * scale - lse_ref[:, g:g + 1])
            dv_ref[...] += lax.dot_general(p.astype(BF16), do_, tn_dims, preferred_element_type=F32)
            dp = lax.dot_general(do_, v, nt_dims, preferred_element_type=F32)
            ds = (p * (dp - row_dot)).astype(BF16)
            dq_ref[:, cols] = jnp.dot(ds, k, preferred_element_type=F32) * scale
            dk_ref[...] += lax.dot_general(ds, q, tn_dims, preferred_element_type=F32)

        @pl.when(pl.program_id(1) == n_q_blocks - 1)
        def _():
            dk_ref[...] = dk_ref[...] * scale
            dz_ref[...] = dv_ref[...].astype(BF16)

    qspec = pl.BlockSpec((tq, group * LANES), lambda h, i: (qb0 + i, h))
    ospec = pl.BlockSpec((tq, group * LANES), lambda h, i: (i, h))
    kspec = pl.BlockSpec((t, LANES), lambda h, i: (0, h))
    vspec = pl.BlockSpec((t, LANES), lambda h, i: (0, vb + h))
    return _call(
        body, side=side, name="attn_bwd", grid=(n_kv, n_q_blocks),
        in_specs=[qspec, kspec, vspec, ospec, ospec, pl.BlockSpec((tq, LANES), lambda h, i: (i, h)), ANY],
        out_specs=[ospec, kspec, vspec], out_shape=[_sds((s, n_kv * group * LANES)), _sds((t, kvw)), _sds(dz.shape, BF16)],
        scratch_shapes=[pltpu.VMEM((t, LANES), F32)], input_output_aliases={6: 2}, sem=("parallel", "arbitrary"),
    )(qr, kr, z, d_o, attn_o, lse, dz)


def _row_mask(shape, rows):
    r = lax.broadcasted_iota(jnp.int32, shape, 0)
    m = r == rows[0]
    for v in rows[1:]:
        m = jnp.logical_or(m, r == v)
    return m


def _shift_rows(x, k, n_ctx_rows):
    t = x.shape[0]
    if k == 0:
        return x
    rolled = pltpu.roll(x, (-k) % t, 0)
    if k > 0:
        dead = [n_ctx_rows - 1 - i for i in range(k)] + [t - 1 - i for i in range(k)]
    else:
        dead = [i for i in range(-k)] + [n_ctx_rows + i for i in range(-k)]
    return jnp.where(_row_mask(x.shape, dead), 0.0, rolled)


def _conv(x, w, b, n_ctx_rows):
    y = b
    for k in range(CONV_WIDTH):
        y = y + _shift_rows(x, k - 1, n_ctx_rows) * w[k:k + 1, :]
    return y


def _gates(xc_bf, w_r, b_r, w_i, b_i, lam):
    r = 0.5 + 0.5 * jnp.tanh(0.5 * (jnp.dot(xc_bf, w_r.astype(BF16), preferred_element_type=F32) + b_r))
    i = 0.5 + 0.5 * jnp.tanh(0.5 * (jnp.dot(xc_bf, w_i.astype(BF16), preferred_element_type=F32) + b_i))
    log_a = (-LRU_C * _softplus(-lam)) * r
    a = jnp.exp(log_a)
    mult = jnp.sqrt(-_expm1_nonpos(2.0 * log_a))
    return r, i, a, mult


def _rnn_specs(t, xr_off):
    xb = xr_off // LANES
    return dict(
        zcol=pl.BlockSpec((t, LANES), lambda j: (0, xb + j)), col=pl.BlockSpec((t, LANES), lambda j: (0, j)),
        conv_w=pl.BlockSpec((CONV_WIDTH, LANES), lambda j: (0, j)), vec=pl.BlockSpec((1, LANES), lambda j: (0, j)),
        gate_w=pl.BlockSpec((2, 1, LANES, LANES), lambda j: (0, j, 0, 0)), two=pl.BlockSpec((2, LANES), lambda j: (0, j)))


def _rnn_prep(z, xr_off, conv_w, conv_b, w_rg, b_rg, w_ig, b_ig, lam, n_ctx_rows, side=None):
    t = z.shape[0]
    d = conv_b.shape[1]
    sp = _rnn_specs(t, xr_off)

    def body(z_ref, cw_ref, cb_ref, wr_ref, br_ref, wi_ref, bi_ref, lam_ref, xc_ref, af_ref, bf_ref, ab_ref, bb_ref):
        xc = _conv(z_ref[...], cw_ref[...], cb_ref[...], n_ctx_rows)
        xc_ref[...] = xc
        xc_bf = xc.astype(BF16)
        for dr, (a_ref, b_ref) in enumerate(((af_ref, bf_ref), (ab_ref, bb_ref))):
            _, i, a, mult = _gates(xc_bf, wr_ref[dr, 0], br_ref[dr:dr + 1, :], wi_ref[dr, 0], bi_ref[dr:dr + 1, :],
                                   lam_ref[dr:dr + 1, :])
            a_ref[...] = a
            b_ref[...] = mult * (i * xc)

    return _call(
        body, side=side, name="rnn_prep", grid=(d // LANES,),
        in_specs=[sp["zcol"], sp["conv_w"], sp["vec"], sp["gate_w"], sp["two"], sp["gate_w"], sp["two"], sp["two"]],
        out_specs=[sp["col"]] * 5, out_shape=[_sds((t, d))] * 5, sem=("parallel",),
    )(z, conv_w, conv_b, w_rg, b_rg, w_ig, b_ig, lam)


def _scan(chains, *, post, n_ctx_rows, name, tc=256, side=None):
    t, d = chains[0][0].shape
    tc = _tile(math.gcd(n_ctx_rows, t - n_ctx_rows), tc, SUBLANES)
    nt, nctx = t // tc, n_ctx_rows // tc
    nlat = nt - nctx
    nc = len(chains)
    lat_only = [b.shape[0] != t for _, b, _ in chains]
    ups = [order.endswith("up") for _, _, order in chains]

    def chunk_of(order):
        def chunk(i):
            if order == "ctx_lat_up":
                return i
            if order == "lat_ctx_down":
                return nt - 1 - i
            if order == "ctx_lat_down":
                return jnp.where(i < nctx, nctx - 1 - i, nt - 1 - (i - nctx))
            return jnp.where(i < nlat, nctx + i, i - nlat)
        return chunk

    chunks = [chunk_of(order) for _, _, order in chains]

    def body(*refs):
        ab_refs, o_refs, carry_ref = refs[:2 * nc], refs[2 * nc:3 * nc], refs[3 * nc]

        @pl.when(pl.program_id(0) == 0)
        def _():
            carry_ref[...] = jnp.zeros_like(carry_ref)

        live = [jnp.where(chunks[n](pl.program_id(0)) >= nctx, 1.0, 0.0) if lat_only[n] else None for n in range(nc)]

        def group(gi, carries):
            carries = list(carries)
            bases = [pl.multiple_of((gi if ups[n] else tc // SUBLANES - 1 - gi) * SUBLANES, SUBLANES) for n in range(nc)]
            for step in range(SUBLANES):
                for n in range(nc):
                    row = bases[n] + (step if ups[n] else SUBLANES - 1 - step)
                    a_r = ab_refs[2 * n][pl.ds(row, 1), :]
                    b_r = ab_refs[2 * n + 1][pl.ds(row, 1), :]
                    if live[n] is not None:
                        b_r = b_r * live[n]
                    if post:
                        out = b_r + carries[n]
                        carries[n] = a_r * out
                    else:
                        out = a_r * carries[n] + b_r
                        carries[n] = out
                    o_refs[n][pl.ds(row, 1), :] = out
            return tuple(carries)

        done = lax.fori_loop(0, tc // SUBLANES, group, tuple(carry_ref[n:n + 1, :] for n in range(nc)))
        for n in range(nc):
            carry_ref[n:n + 1, :] = done[n]

    in_specs, out_specs, args = [], [], []
    for n, (a, b, _) in enumerate(chains):
        full = pl.BlockSpec((tc, d), lambda i, n=n: (chunks[n](i), 0))
        lat = pl.BlockSpec((tc, d), lambda i, n=n: (jnp.maximum(chunks[n](i) - nctx, 0), 0))
        in_specs += [full, lat if lat_only[n] else full]
        out_specs.append(full)
        args += [a, b]
    return _call(
        body, side=side, name=name, grid=(nt,), in_specs=in_specs, out_specs=out_specs, out_shape=[_sds((t, d))] * nc,
        scratch_shapes=[pltpu.VMEM((SUBLANES, d), F32)], sem=("arbitrary",),
    )(*args)


def _rnn_bwd(z, xr_off, xc, g_f, g_b, h_f, h_b, conv_w, w_rg, b_rg, w_ig, b_ig, lam, dz, n_ctx_rows, side=None):
    t, d = xc.shape
    sp = _rnn_specs(t, xr_off)
    tn_dims = (((0,), (0,)), ((), ()))
    nt_dims = (((1,), (1,)), ((), ()))

    def body(z_ref, xc_ref, gf_ref, gb_ref, hf_ref, hb_ref, cw_ref, wr_ref, br_ref, wi_ref, bi_ref, lam_ref, _,
             dxr_ref, dwr_ref, dwi_ref, sums_ref):
        xc_ = xc_ref[...]
        xc_bf = xc_.astype(BF16)
        dxc = jnp.zeros_like(xc_)
        sums = [None] * 6
        for dr, (g_ref, h_ref) in enumerate(((gf_ref, hf_ref), (gb_ref, hb_ref))):
            w_r, w_i, lam_ = wr_ref[dr, 0], wi_ref[dr, 0], lam_ref[dr:dr + 1, :]
            r, i, a, mult = _gates(xc_bf, w_r, br_ref[dr:dr + 1, :], w_i, bi_ref[dr:dr + 1, :], lam_)
            g = g_ref[...]
            h = h_ref[...]
            if dr == 0:
                h_prev = jnp.where(_row_mask(h.shape, [0]), 0.0, pltpu.roll(h, 1, 0))
            else:
                h_prev = jnp.where(_row_mask(h.shape, [n_ctx_rows - 1]), 0.0, pltpu.roll(h, t - 1, 0))
            d_mult = g * i * xc_
            d_i = g * mult * xc_
            dxc = dxc + g * mult * i
            d_log_a = g * h_prev * a - d_mult * a * a / mult
            sp_ = _softplus(-lam_)
            d_r = d_log_a * (-LRU_C) * sp_
            d_sp = _colsum(d_log_a * (-LRU_C) * r)
            du_r = (d_r * r * (1.0 - r))
            du_i = (d_i * i * (1.0 - i))
            sums[dr] = _colsum(du_r)
            sums[2 + dr] = _colsum(du_i)
            sums[4 + dr] = d_sp * (-jax.nn.sigmoid(-lam_))
            du_r_bf, du_i_bf = du_r.astype(BF16), du_i.astype(BF16)
            dwr_ref[dr, 0] = lax.dot_general(xc_bf, du_r_bf, tn_dims, preferred_element_type=F32).astype(BF16)
            dwi_ref[dr, 0] = lax.dot_general(xc_bf, du_i_bf, tn_dims, preferred_element_type=F32).astype(BF16)
            dxc = dxc + lax.dot_general(du_r_bf, w_r.astype(BF16), nt_dims, preferred_element_type=F32)
            dxc = dxc + lax.dot_general(du_i_bf, w_i.astype(BF16), nt_dims, preferred_element_type=F32)
        xr = z_ref[...]
        cw = cw_ref[...]
        dxr = jnp.zeros_like(dxc)
        rows = list(sums)
        for k in range(CONV_WIDTH):
            dxr = dxr + _shift_rows(dxc, 1 - k, n_ctx_rows) * cw[k:k + 1, :]
            rows.append(_colsum(dxc * _shift_rows(xr, k - 1, n_ctx_rows)))
        rows.append(_colsum(dxc))
        dxr_ref[...] = dxr.astype(BF16)
        sums_ref[...] = jnp.zeros_like(sums_ref)
        for k, row in enumerate(rows):
            sums_ref[k:k + 1, :] = row

    return _call(
        body, side=side, name="rnn_bwd", grid=(d // LANES,),
        in_specs=[sp["zcol"]] + [sp["col"]] * 5 + [sp["conv_w"], sp["gate_w"], sp["two"], sp["gate_w"], sp["two"],
                                                  sp["two"], ANY],
        out_specs=[sp["zcol"], sp["gate_w"], sp["gate_w"], pl.BlockSpec((16, LANES), lambda j: (0, j))],
        out_shape=[_sds(dz.shape, BF16), _sds(w_rg.shape, BF16), _sds(w_ig.shape, BF16), _sds((16, d))],
        input_output_aliases={12: 0}, sem=("parallel",),
    )(z, xc, g_f, g_b, h_f, h_b, conv_w, w_rg, b_rg, w_ig, b_ig, lam, dz)


def _tiles2d(s, d, tr, tcol):
    return (s // tr, d // tcol), pl.BlockSpec((tr, tcol), lambda i, j: (i, j))


def _zspec(tr, tcol, row_off, col_off):
    rb, cb = row_off // tr, col_off // tcol
    return pl.BlockSpec((tr, tcol), lambda i, j: (rb + i, cb + j))


def _rnn_gate_fwd(h_f, h_b, z, xg_off, n_ctx_rows, tr, tcol, side=None):
    t, d = h_f.shape
    s = t - n_ctx_rows
    grid, out = _tiles2d(s, d, tr, tcol)
    hs = _zspec(tr, tcol, n_ctx_rows, 0)

    def body(hf_ref, hb_ref, xg_ref, u_ref):
        u_ref[...] = ((hf_ref[...] + hb_ref[...]) * _gelu(xg_ref[...])).astype(BF16)

    return _call(body, side=side, name="rnn_gate_fwd", grid=grid,
                 in_specs=[hs, hs, _zspec(tr, tcol, n_ctx_rows, xg_off)], out_specs=out, out_shape=_sds((s, d), BF16),
                 sem=("parallel", "parallel"))(h_f, h_b, z)


def _rnn_gate_bwd(d_u, h_f, h_b, z, xg_off, dz, n_ctx_rows, tr, tcol, side=None):
    t, d = h_f.shape
    s = t - n_ctx_rows
    grid, out = _tiles2d(s, d, tr, tcol)
    hs = _zspec(tr, tcol, n_ctx_rows, 0)
    window = _zspec(tr, tcol, n_ctx_rows, xg_off)

    def body(du_ref, hf_ref, hb_ref, xg_ref, _, dr_ref, dxg_ref):
        du = du_ref[...]
        xg = xg_ref[...]
        dr_ref[...] = du * _gelu(xg)
        dxg_ref[...] = (du * (hf_ref[...] + hb_ref[...]) * _gelu_grad(xg)).astype(BF16)

    return _call(body, side=side, name="rnn_gate_bwd", grid=grid, in_specs=[out, hs, hs, window, ANY],
                 out_specs=[out, window], out_shape=[_sds((s, d)), _sds(dz.shape, BF16)], input_output_aliases={4: 1},
                 sem=("parallel", "parallel"))(d_u, h_f, h_b, z, dz)


def _merge_fwd(y_attn, y_rnn, z, gl_off, n_ctx_rows, tr, tcol):
    s, d = y_attn.shape
    grid, out = _tiles2d(s, d, tr, tcol)

    def body(ya_ref, yr_ref, ga_ref, gr_ref, o_ref):
        o_ref[...] = (jax.nn.sigmoid(ga_ref[...]) * ya_ref[...] + jax.nn.sigmoid(gr_ref[...]) * yr_ref[...]).astype(BF16)

    return pl.pallas_call(
        body, name="merge_fwd", grid=grid,
        in_specs=[out, out, _zspec(tr, tcol, n_ctx_rows, gl_off), _zspec(tr, tcol, n_ctx_rows, gl_off + d)],
        out_specs=out, out_shape=_sds((s, d), BF16), compiler_params=_params("parallel", "parallel"),
    )(y_attn, y_rnn, z, z)


def _merge_bwd(d_mrg, y, z, gl_off, dz, n_ctx_rows, tr, tcol, name):
    s, d = y.shape
    grid, out = _tiles2d(s, d, tr, tcol)
    window = _zspec(tr, tcol, n_ctx_rows, gl_off)

    def body(dm_ref, y_ref, gl_ref, _, dy_ref, dgl_ref):
        dm = dm_ref[...]
        g = jax.nn.sigmoid(gl_ref[...])
        dy_ref[...] = (dm * g).astype(BF16)
        dgl_ref[...] = (dm * y_ref[...] * g * (1.0 - g)).astype(BF16)

    return _call(body, name=name, grid=grid, in_specs=[out, out, window, ANY], out_specs=[out, window],
                 out_shape=[_sds((s, d), BF16), _sds(dz.shape, BF16)], input_output_aliases={3: 1},
                 sem=("parallel", "parallel"))(d_mrg, y, z, dz)


def _cast_into_window(w, chip, col_sharded, name):
    r, c = w.shape
    tr, tcol = _tile(r, 512, 16), _tile(c, WIDE_TILE, LANES)
    nrb, ncb = r // tr, c // tcol

    def body(chip_ref, w_ref, o_ref):
        o_ref[...] = w_ref[...].astype(BF16)

    if col_sharded:
        omap = lambda i, j, chip_ref: (i, chip_ref[0] * ncb + j)
    else:
        omap = lambda i, j, chip_ref: (chip_ref[0] * nrb + i, j)
    return pl.pallas_call(
        body, name=name,
        grid_spec=pltpu.PrefetchScalarGridSpec(
            num_scalar_prefetch=1, grid=(nrb, ncb),
            in_specs=[pl.BlockSpec((tr, tcol), lambda i, j, chip_ref: (i, j))], out_specs=pl.BlockSpec((tr, tcol), omap)),
        out_shape=_sds((r, c * N_CHIPS) if col_sharded else (r * N_CHIPS, c), BF16),
        compiler_params=_params("parallel", "parallel"),
    )(chip, w)


def _sum_leading(parts, name):
    n, r, c = parts.shape
    tr, tcol = _tile(r, 512, SUBLANES), _tile(c, 1024, LANES)

    def body(p_ref, o_ref):
        tot = p_ref[0]
        for k in range(1, n):
            tot = tot + p_ref[k]
        o_ref[...] = tot

    return pl.pallas_call(
        body, name=name, grid=(r // tr, c // tcol), in_specs=[pl.BlockSpec((n, tr, tcol), lambda i, j: (0, i, j))],
        out_specs=pl.BlockSpec((tr, tcol), lambda i, j: (i, j)), out_shape=_sds((r, c)),
        compiler_params=_params("parallel", "parallel"),
    )(parts)


def _add_half(full, other, core, split_rows, name):
    r, c = other.shape
    tr, tcol = _tile(r, 512, 16), _tile(c, 1024, LANES)
    nrb, ncb = r // tr, c // tcol

    def body(core_ref, f_ref, o_ref, out_ref):
        out_ref[...] = (f_ref[...].astype(F32) + o_ref[...].astype(F32)).astype(out_ref.dtype)

    if split_rows:
        fmap = lambda i, j, core_ref: (core_ref[0] * nrb + i, j)
    else:
        fmap = lambda i, j, core_ref: (i, core_ref[0] * ncb + j)
    same = lambda i, j, core_ref: (i, j)
    return pl.pallas_call(
        body, name=name,
        grid_spec=pltpu.PrefetchScalarGridSpec(
            num_scalar_prefetch=1, grid=(nrb, ncb),
            in_specs=[pl.BlockSpec((tr, tcol), fmap), pl.BlockSpec((tr, tcol), same)],
            out_specs=pl.BlockSpec((tr, tcol), same)),
        out_shape=_sds((r, c), BF16), compiler_params=_params("parallel", "parallel"),
    )(core, full, other)


def _add_pair(a, b, name):
    r, c = a.shape
    tr, tcol = _tile(r, 512, 16), _tile(c, WIDE_TILE, LANES)
    blk = pl.BlockSpec((tr, tcol), lambda i, j: (i, j))

    def body(a_ref, b_ref, o_ref):
        o_ref[...] = (a_ref[...].astype(F32) + b_ref[...].astype(F32)).astype(BF16)

    return pl.pallas_call(body, name=name, grid=(r // tr, c // tcol), in_specs=[blk, blk], out_specs=blk,
                          out_shape=_sds((r, c), BF16), compiler_params=_params("parallel", "parallel"))(a, b)


def _sum_regions(pair, got, place, col_sharded, name):
    n_got, r, c = got.shape
    tr, tcol = _tile(r, 512, 16), _tile(c, WIDE_TILE, LANES)
    nrb, ncb = r // tr, c // tcol

    def body(place_ref, p_ref, g_ref, out_ref):
        tot = p_ref[...].astype(F32)
        for k in range(n_got):
            tot = tot + g_ref[k].astype(F32)
        out_ref[...] = tot

    if col_sharded:
        pmap = lambda i, j, pr: (i, pr[0] * ncb + j)
        omap = lambda i, j, pr: (pr[1] * nrb + i, j)
        out_shape = (2 * r, c)
    else:
        pmap = lambda i, j, pr: (pr[0] * nrb + i, j)
        omap = lambda i, j, pr: (i, pr[1] * ncb + j)
        out_shape = (r, 2 * c)
    return pl.pallas_call(
        body, name=name,
        grid_spec=pltpu.PrefetchScalarGridSpec(
            num_scalar_prefetch=1, grid=(nrb, ncb),
            in_specs=[pl.BlockSpec((tr, tcol), pmap), pl.BlockSpec((n_got, tr, tcol), lambda i, j, pr: (0, i, j))],
            out_specs=pl.BlockSpec((tr, tcol), omap)),
        out_shape=_sds(out_shape), compiler_params=_params("parallel", "parallel"),
    )(place, pair, got)


def _fold_forwarded(pair, fold, nbr_chips, name):
    _, r, c = fold.shape
    tr, tcol = _tile(r, 512, 16), _tile(c, WIDE_TILE, LANES)
    nrb, ncb = r // tr, c // tcol

    def body(nbr_ref, px_ref, py_ref, f_ref, out_ref):
        out_ref[0] = (px_ref[...].astype(F32) + f_ref[1].astype(F32)).astype(out_ref.dtype)
        out_ref[1] = (py_ref[...].astype(F32) + f_ref[0].astype(F32)).astype(out_ref.dtype)

    both = pl.BlockSpec((2, tr, tcol), lambda i, j, nb: (0, i, j))
    return pl.pallas_call(
        body, name=name,
        grid_spec=pltpu.PrefetchScalarGridSpec(
            num_scalar_prefetch=1, grid=(nrb, ncb),
            in_specs=[pl.BlockSpec((tr, tcol), lambda i, j, nb: (i, nb[0] * ncb + j)),
                      pl.BlockSpec((tr, tcol), lambda i, j, nb: (nrb + i, nb[1] * ncb + j)), both],
            out_specs=both),
        out_shape=_sds(fold.shape, fold.dtype), compiler_params=_params("parallel", "parallel"),
    )(nbr_chips, pair, pair, fold)


def _adamw(w, g, m, v, name, copy_grad=True):
    r, c = w.shape
    tr, tcol = _tile(r, 256, SUBLANES), _tile(c, WIDE_TILE, LANES)
    blk = pl.BlockSpec((tr, tcol), lambda i, j: (i, j))
    n_out = 4 if copy_grad else 3

    def body(w_ref, g_ref, m_ref, v_ref, d_ref, nm_ref, nv_ref, *g_out_ref):
        g_ = g_ref[...]
        if copy_grad:
            g_out_ref[0][...] = g_
        m_ = ADAM_B1 * m_ref[...] + (1.0 - ADAM_B1) * g_
        v_ = ADAM_B2 * v_ref[...] + (1.0 - ADAM_B2) * (g_ * g_)
        m_hat = m_ / (1.0 - ADAM_B1 ** ADAM_STEP)
        v_hat = v_ / (1.0 - ADAM_B2 ** ADAM_STEP)
        d_ref[...] = -ADAM_LR * (m_hat / (jnp.sqrt(v_hat) + ADAM_EPS) + ADAM_WD * w_ref[...])
        nm_ref[...] = m_
        nv_ref[...] = v_

    return _call(body, name=name, grid=(r // tr, c // tcol), in_specs=[blk] * 4, out_specs=[blk] * n_out,
                 out_shape=[_sds((r, c))] * n_out, sem=("parallel", "parallel"))(w, g, m, v)


def _place():
    x, y, c = lax.axis_index("x"), lax.axis_index("y"), lax.axis_index("c")
    chips = [(1 - x, y), (x, 1 - y), (1 - x, 1 - y)]
    return x, y, c, chips


def _all_gather8(blk, name):
    m, n = blk.shape

    def body(x_ref, out_ref, send_sems, recv_sems, local_sem):
        x, y, c, chips = _place()
        me, sibling = (x, y, c), (x, y, 1 - c)

        def rows(px, py, pc):
            return out_ref.at[pl.ds((4 * px + 2 * py + pc) * m, m), :]

        def copy(k, block, to, src=None):
            return pltpu.make_async_remote_copy(
                src_ref=rows(*block) if src is None else src, dst_ref=rows(*block), send_sem=send_sems.at[k],
                recv_sem=recv_sems.at[k], device_id=to, device_id_type=MESH_ID)

        mine = pltpu.make_async_copy(x_ref, rows(*me), local_sem)
        mine.start()
        first = [copy(0, me, sibling, src=x_ref)]
        first += [copy(1 + j, me, (*chip, c), src=x_ref) for j, chip in enumerate(chips)]
        for cp in first:
            cp.start()
        passed = [copy(4 + j, (*chip, c), sibling) for j, chip in enumerate(chips)]
        for j, chip in enumerate(chips):
            copy(1 + j, (*chip, c), me).wait_recv()
            passed[j].start()
        copy(0, sibling, me).wait_recv()
        for j, chip in enumerate(chips):
            copy(4 + j, (*chip, 1 - c), me).wait_recv()
        for cp in first + passed:
            cp.wait_send()
        mine.wait()

    return pl.pallas_call(
        body, name=name, out_shape=_sds((N_DEV * m, n), blk.dtype), in_specs=[ANY], out_specs=ANY,
        scratch_shapes=[pltpu.SemaphoreType.DMA((7,)), pltpu.SemaphoreType.DMA((7,)), pltpu.SemaphoreType.DMA],
    )(blk)


def _half(ref, core, split_rows):
    r, c = ref.shape
    if split_rows:
        return ref.at[pl.ds(core * (r // 2), r // 2), :]
    return ref.at[:, pl.ds(core * (c // 2), c // 2)]


def _chip_block(ref, j, col_sharded):
    r, c = ref.shape
    if col_sharded:
        return ref.at[:, pl.ds(j * (c // N_CHIPS), c // N_CHIPS)]
    return ref.at[pl.ds(j * (r // N_CHIPS), r // N_CHIPS), :]


def _rows_part(ref, part):
    lo, hi, n = part
    r = ref.shape[0]
    return ref if (lo, hi) == (0, n) else ref.at[pl.ds(lo * (r // n), (hi - lo) * (r // n)), :]


def _copy(send_sems, recv_sems, k, src, dst, to):
    return pltpu.make_async_remote_copy(src_ref=src, dst_ref=dst, send_sem=send_sems.at[k], recv_sem=recv_sems.at[k],
                                        device_id=to, device_id_type=MESH_ID)


def _in_place(arrays):
    return tuple(arrays), tuple(_sds(a.shape, a.dtype) for a in arrays), {i: i for i in range(len(arrays))}


def _gather_d2d(fulls, col_sharded):
    nw = len(fulls)

    def build(_, refs, send_sems, recv_sems, sem0):
        x, y, c, chips = _place()
        sends, recvs = [], []
        for w in range(nw):
            win = lambda j, core: _half(_chip_block(refs[w], j, col_sharded[w]), core, True)
            for k, (cx, cy) in enumerate(chips):
                sem = sem0 + 3 * w + k
                sends.append(_copy(send_sems, recv_sems, sem, win(2 * cx + cy, c), win(2 * cx + cy, c), (x, y, 1 - c)))
                recvs.append(_copy(send_sems, recv_sems, sem, win(2 * cx + cy, 1 - c), win(2 * cx + cy, 1 - c),
                                   (x, y, 1 - c)))
        return sends, recvs

    return _Side(*_in_place(fulls), 3 * nw, build)


def _gather_neighbours(fulls, col_sharded, part=(0, 1, 1)):
    nw = len(fulls)

    def build(_, refs, send_sems, recv_sems, sem0):
        x, y, c, chips = _place()
        sends, recvs = [], []
        for w in range(nw):
            win = lambda j: _rows_part(_half(_chip_block(refs[w], j, col_sharded[w]), c, True), part)
            for k, (cx, cy) in enumerate(chips[:2]):
                sem = sem0 + 2 * w + k
                sends.append(_copy(send_sems, recv_sems, sem, win(2 * x + y), win(2 * x + y), (cx, cy, c)))
                recvs.append(_copy(send_sems, recv_sems, sem, win(2 * cx + cy), win(2 * cx + cy), (cx, cy, c)))
        return sends, recvs

    return _Side(*_in_place(fulls), 2 * nw, build)


def _gather_forward(fulls, col_sharded, ways=(True, True), siblings=False):
    nw = len(fulls)

    def build(_, refs, send_sems, recv_sems, sem0):
        x, y, c, (cx_, cy_, cd_) = _place()
        sends, recvs = [], []
        for w in range(nw):
            win = lambda chip, core: _half(_chip_block(refs[w], 2 * chip[0] + chip[1], col_sharded[w]), core, True)
            part = lambda ref, p: _rows_part(ref, (p, p + 1, 2))
            for p, (src, to) in enumerate(((cx_, cy_), (cy_, cx_))):
                if ways[p]:
                    sem = sem0 + 4 * w + p
                    sends.append(_copy(send_sems, recv_sems, sem, part(win(src, c), p), part(win(src, c), p), (*to, c)))
                    recvs.append(_copy(send_sems, recv_sems, sem, part(win(cd_, c), p), part(win(cd_, c), p), (*to, c)))
            if siblings:
                for k, chip in enumerate((cx_, cy_)):
                    sem = sem0 + 4 * w + 2 + k
                    sends.append(_copy(send_sems, recv_sems, sem, win(chip, c), win(chip, c), (x, y, 1 - c)))
                    recvs.append(_copy(send_sems, recv_sems, sem, win(chip, 1 - c), win(chip, 1 - c), (x, y, 1 - c)))
        return sends, recvs

    return _Side(*_in_place(fulls), 4 * nw, build)


def _gather_diagonal_d2d(full, col_sharded):
    def build(_, refs, send_sems, recv_sems, sem0):
        x, y, c, chips = _place()
        cx, cy = chips[2]
        win = lambda core: _half(_chip_block(refs[0], 2 * cx + cy, col_sharded), core, True)
        return ([_copy(send_sems, recv_sems, sem0, win(c), win(c), (x, y, 1 - c))],
                [_copy(send_sems, recv_sems, sem0, win(1 - c), win(1 - c), (x, y, 1 - c))])

    return _Side(*_in_place([full]), 1, build)


def _scatter_forward_1(pair):
    r, n = pair.shape
    results = (_sds((2, r, n // N_CHIPS), pair.dtype), _sds((2, r // 2, n // N_CHIPS), pair.dtype))

    def build(refs, res, send_sems, recv_sems, sem0):
        x, y, c, (cx_, cy_, cd_) = _place()
        region = lambda chip, p: _rows_part(_chip_block(refs[0], 2 * chip[0] + chip[1], True), (p, p + 1, 2))
        got, fold = res
        copies = [
            _copy(send_sems, recv_sems, sem0, region(cx_, 1), _rows_part(got.at[0], (1, 2, 2)), (*cx_, c)),
            _copy(send_sems, recv_sems, sem0 + 1, region(cd_, 1), fold.at[0], (*cx_, c)),
            _copy(send_sems, recv_sems, sem0 + 2, region(cy_, 0), _rows_part(got.at[1], (0, 1, 2)), (*cy_, c)),
            _copy(send_sems, recv_sems, sem0 + 3, region(cd_, 0), fold.at[1], (*cy_, c))]
        return copies, copies

    return _Side((pair,), results, {}, 4, build)


def _scatter_forward_2(passed, got):
    def build(refs, res, send_sems, recv_sems, sem0):
        x, y, c, (cx_, cy_, _) = _place()
        copies = [_copy(send_sems, recv_sems, sem0, refs[0].at[0], _rows_part(res[0].at[0], (0, 1, 2)), (*cx_, c)),
                  _copy(send_sems, recv_sems, sem0 + 1, refs[0].at[1], _rows_part(res[0].at[1], (1, 2, 2)), (*cy_, c))]
        return copies, copies

    return _Side((passed, got), (_sds(got.shape, got.dtype),), {1: 0}, 2, build)


def _exchange(side, name):
    return _call(None, side=side, name=name)()[1]


def _swap_halves(grads, col_sharded):
    nw = len(grads)
    out_shapes = [_sds((g.shape[0] // 2, g.shape[1]) if col else (g.shape[0], g.shape[1] // 2), g.dtype)
                  for g, col in zip(grads, col_sharded)]

    def build(g_refs, o_refs, send_sems, recv_sems, sem0):
        x, y, c, _ = _place()
        copies = [_copy(send_sems, recv_sems, sem0 + w, _half(g_refs[w], 1 - c, col_sharded[w]), o_refs[w],
                        (x, y, 1 - c)) for w in range(nw)]
        return copies, copies

    return _Side(tuple(grads), tuple(out_shapes), {}, nw, build)


def _send_to_sibling(arrays):
    nw = len(arrays)

    def build(refs, o_refs, send_sems, recv_sems, sem0):
        x, y, c, _ = _place()
        copies = [_copy(send_sems, recv_sems, sem0 + w, refs[w], o_refs[w], (x, y, 1 - c)) for w in range(nw)]
        return copies, copies

    return _Side(tuple(arrays), tuple(_sds(a.shape, a.dtype) for a in arrays), {}, nw, build)


def _scatter_regions(pairs, col_sharded, part=(0, 1, 1), into=None):
    nw = len(pairs)

    def region_shape(p, col):
        return (p.shape[0], p.shape[1] // N_CHIPS) if col else (p.shape[0] // N_CHIPS, p.shape[1])

    out_shapes = tuple(_sds((N_CHIPS - 1, *region_shape(p, col)), p.dtype) for p, col in zip(pairs, col_sharded))

    def build(refs, o_refs, send_sems, recv_sems, sem0):
        x, y, c, chips = _place()
        copies = []
        for w in range(nw):
            for k, (cx, cy) in enumerate(chips):
                copies.append(_copy(
                    send_sems, recv_sems, sem0 + 3 * w + k,
                    _rows_part(_chip_block(refs[w], 2 * cx + cy, col_sharded[w]), part),
                    _rows_part(o_refs[w].at[k], part), (cx, cy, c)))
        return copies, copies

    if into is None:
        return _Side(tuple(pairs), out_shapes, {}, 3 * nw, build)
    return _Side((*pairs, *into), out_shapes, {nw + w: w for w in range(nw)}, 3 * nw, build)


def _join_halves(halves, col_sharded):
    nw = len(halves)

    def build(_, refs, send_sems, recv_sems, sem0):
        x, y, c, _ = _place()
        sends, recvs = [], []
        for w in range(nw):
            mine, theirs = _half(refs[w], c, col_sharded[w]), _half(refs[w], 1 - c, col_sharded[w])
            sends.append(_copy(send_sems, recv_sems, sem0 + w, mine, mine, (x, y, 1 - c)))
            recvs.append(_copy(send_sems, recv_sems, sem0 + w, theirs, theirs, (x, y, 1 - c)))
        return sends, recvs

    return _Side(*_in_place(halves), nw, build)


def _part_rows(size):
    return -(-size // (SUBLANES * LANES)) * SUBLANES


def _pack(arrays, pad_rows_to=SUBLANES):
    flat = [jnp.pad(a.reshape(-1), (0, _part_rows(a.size) * LANES - a.size)).reshape(-1, LANES) for a in arrays]
    rows = sum(f.shape[0] for f in flat)
    pad = (-rows) % pad_rows_to
    if pad:
        flat.append(jnp.zeros((pad, LANES), F32))
    return jnp.concatenate(flat, axis=0)


def _unpack(packed, shapes):
    out, r = [], 0
    for shp in shapes:
        size = math.prod(shp)
        out.append(packed[r:r + _part_rows(size)].reshape(-1)[:size].reshape(shp))
        r += _part_rows(size)
    return out


def _rope_tables(n_ctx_rows, s):
    rows = s // GRID_W
    row_idx = jnp.repeat(jnp.arange(rows), GRID_W)
    col_idx = jnp.tile(jnp.arange(GRID_W), rows)
    n_freq = LANES // 4
    inv_freq = ROPE_THETA ** (-jnp.arange(n_freq, dtype=F32) / n_freq)
    ang = jnp.concatenate([row_idx.astype(F32)[:, None] * inv_freq, col_idx.astype(F32)[:, None] * inv_freq], axis=-1)
    cos = jnp.repeat(jnp.cos(ang), 2, axis=-1)
    sin = jnp.repeat(jnp.sin(ang), 2, axis=-1) * jnp.tile(jnp.array([-1.0, 1.0], F32), LANES // 2)
    cos = jnp.concatenate([jnp.ones((n_ctx_rows, LANES), F32), cos], axis=0)
    sin = jnp.concatenate([jnp.zeros((n_ctx_rows, LANES), F32), sin], axis=0)
    return cos, sin


WEIGHT_NAMES = ['c_ctx', 'w_mod', 'b_mod', 'g_mix', 'g_mlp', 'w_in', 'q_gain', 'k_gain', 'conv_w', 'conv_b', 'w_rg',
                'b_rg', 'w_ig', 'b_ig', 'lru_lambda', 'w_o_attn', 'w_o_rnn', 'w_out', 'w_up', 'w_down', 'g_final']
BIG = ['w_in', 'w_o_attn', 'w_o_rnn', 'w_out', 'w_up', 'w_down']
BIG_COL_SHARDED = [True, False, False, False, True, False]
GATES = ['w_rg', 'w_ig']
SMALL = ['c_ctx', 'b_mod', 'g_mix', 'g_mlp', 'q_gain', 'k_gain', 'conv_b', 'g_final',
         'conv_w', 'b_rg', 'b_ig', 'lru_lambda']


def kernel(x, c, ctx, c_ctx, w_mod, b_mod, g_mix, g_mlp, w_in, q_gain, k_gain, conv_w, conv_b, w_rg, b_rg, w_ig, b_ig, lru_lambda, w_o_attn, w_o_rnn, w_out, w_up, w_down, g_final, loss_target, m_c_ctx, m_w_mod, m_b_mod, m_g_mix, m_g_mlp, m_w_in, m_q_gain, m_k_gain, m_conv_w, m_conv_b, m_w_rg, m_b_rg, m_w_ig, m_b_ig, m_lru_lambda, m_w_o_attn, m_w_o_rnn, m_w_out, m_w_up, m_w_down, m_g_final, v_c_ctx, v_w_mod, v_b_mod, v_g_mix, v_g_mlp, v_w_in, v_q_gain, v_k_gain, v_conv_w, v_conv_b, v_w_rg, v_b_rg, v_w_ig, v_b_ig, v_lru_lambda, v_w_o_attn, v_w_o_rnn, v_w_out, v_w_up, v_w_down, v_g_final):
    given = dict(locals())
    weights = {n: given[n] for n in WEIGHT_NAMES}
    moms = {n: given["m_" + n] for n in WEIGHT_NAMES}
    vars_ = {n: given["v_" + n] for n in WEIGHT_NAMES}

    s, d = x.shape[1], x.shape[2]
    n_ctx = ctx.shape[1]
    t = n_ctx + s
    hd = q_gain.shape[1]
    assert hd == LANES and w_rg.shape[-1] == LANES
    attn_w = w_o_attn.shape[1] * N_CHIPS
    n_in = w_in.shape[2] * N_CHIPS
    kv_w = (n_in - attn_w - 4 * d) // 2
    group = attn_w // kv_w
    k_off, v_off, xr_off = attn_w, attn_w + kv_w, attn_w + 2 * kv_w
    xg_off, gl_off = xr_off + d, xr_off + 2 * d
    d_mod = N_MOD * d
    tr = _tile(math.gcd(n_ctx, s), 256, 16)
    tcol = _tile(math.gcd(d, xr_off), 1024, LANES)
    xi, yi, ci = lax.axis_index("x"), lax.axis_index("y"), lax.axis_index("c")
    chip = 2 * xi + yi
    core = ci.astype(jnp.int32).reshape(1)

    sharded_small = [conv_w[0], b_rg[0], b_ig[0], lru_lambda[0]]
    pack0 = _pack([c[0]] + sharded_small)
    got0 = _all_gather8(pack0, "gather_small_inputs").reshape(N_DEV, -1, LANES)
    c_all = got0[:, :_part_rows(d)].reshape(N_DEV, -1)[:, :d]
    per_chip = [_unpack(got0[2 * j, _part_rows(d):], [a.shape for a in sharded_small]) for j in range(N_CHIPS)]
    conv_w_f, b_rg_f, b_ig_f, lam_f = (jnp.concatenate([per_chip[j][i] for j in range(N_CHIPS)], axis=-1)
                                       for i in range(4))
    c16 = jnp.concatenate([c_all, c_ctx[None, :], jnp.zeros((16 - N_DEV - 1, d), F32)], axis=0)
    b_mod_shard = lax.dynamic_slice(b_mod, (0, chip * (d_mod // N_CHIPS)), (1, d_mod // N_CHIPS))
    mod_part, silu16 = _mod_fwd(c16, w_mod[0], b_mod_shard)
    mod_all = _all_gather8(mod_part, "gather_mod").reshape(N_DEV, 16, d_mod // N_CHIPS)
    mod16 = jnp.concatenate([mod_all[2 * j] for j in range(N_CHIPS)], axis=-1)
    me = 4 * xi + 2 * yi + ci
    mod_lat = lax.dynamic_slice(mod16, (me, 0), (1, d_mod)).reshape(N_MOD, d)
    mod_ctx = mod16[N_DEV].reshape(N_MOD, d)
    mod4 = jnp.stack([mod_ctx[0], mod_ctx[1], mod_lat[0], mod_lat[1]])
    mod3 = jnp.stack([mod_lat[2], mod_lat[3], mod_lat[4]])
    gate_f = mod_lat[5][None, :]

    chip_arr = chip.astype(jnp.int32).reshape(1)
    own = {n: _cast_into_window(weights[n][0], chip_arr, col, "cast_" + n) for n, col in zip(BIG, BIG_COL_SHARDED)}
    place = jnp.stack([chip, ci]).astype(jnp.int32)
    row3 = [False] * 3

    def pair_sum(n, full, other, col):
        return _add_half(full, other, core, col, "pair_sum_" + n)

    def chip_sum(n, pair, got, col):
        return _sum_regions(pair, got, place, col, "chip_sum_" + n)

    (w_in_f,) = _exchange(_gather_neighbours([own['w_in']], [True]), "gather_w_in_nbr")
    (w_in_f,) = _exchange(_gather_forward([w_in_f], [True], siblings=True), "gather_w_in_fwd")
    (w_in_f,) = _exchange(_gather_diagonal_d2d(w_in_f, True), "gather_w_in_diag")
    cos, sin = _rope_tables(n_ctx, s)
    h = _norm_mod_fwd(ctx[0], x[0], g_mix, mod4, tr)
    z, (w_oa_f, w_or_f, w_out_f, w_up_f) = _matmul(h, w_in_f, name="mm_in", side=_sides(
        _gather_neighbours([own['w_o_attn'], own['w_o_rnn'], own['w_out']], row3),
        _gather_neighbours([own['w_up']], [True], (0, 1, 2))))
    qr, (w_oa_f, w_or_f, w_out_f) = _head_prep_fwd(z, 0, attn_w // LANES, q_gain, cos, sin, tr, "q_prep",
                                                   side=_gather_forward([w_oa_f, w_or_f, w_out_f], row3))
    kr = _head_prep_fwd(z, k_off, kv_w // LANES, k_gain, cos, sin, tr, "k_prep")
    (attn_o, lse), (w_up_f, w_down_f, w_oa_f, w_or_f, w_out_f) = _attn_fwd(
        qr, kr, z, v_off, n_ctx, group, tr, side=_sides(
            _on_same(_gather_neighbours([w_up_f], [True], (1, 2, 2)), _gather_forward([w_up_f], [True], ways=(True, False))),
            _gather_neighbours([own['w_down']], [False]), _gather_d2d([w_oa_f, w_or_f, w_out_f], row3)))
    (xc, a_f, bx_f, a_b, bx_b), (w_up_f,) = _rnn_prep(
        z, xr_off, conv_w_f, conv_b, w_rg[0], b_rg_f, w_ig[0], b_ig_f, lam_f, n_ctx,
        side=_gather_forward([w_up_f], [True], ways=(False, True)))
    (h_f, h_b), (w_up_f, w_down_f) = _scan(
        [(a_f, bx_f, "ctx_lat_up"), (a_b, bx_b, "ctx_lat_down")], post=False, n_ctx_rows=n_ctx, name="scan_fwd",
        side=_sides(_gather_d2d([w_up_f], [True]), _gather_forward([w_down_f], [False])))
    u = _rnn_gate_fwd(h_f, h_b, z, xg_off, n_ctx, tr, tcol)
    y_attn = _matmul(attn_o, w_oa_f, name="mm_o_attn")
    y_rnn = _matmul(u, w_or_f, name="mm_o_rnn")
    mrg = _merge_fwd(y_attn, y_rnn, z, gl_off, n_ctx, tr, tcol)
    mix = _matmul(mrg, w_out_f, name="mm_out")
    x1, h2 = _resid_norm_mod_fwd(x[0], mix, g_mlp, mod3, tr)
    (up, act), (w_down_f,) = _matmul(
        h2, w_up_f, name="mm_up", out_dtype=(F32, BF16), side=_gather_d2d([w_down_f], [False]),
        post=lambda u: (u, jnp.square(jnp.maximum(u, 0.0))))
    down = _matmul(act, w_down_f, name="mm_down")
    dx2, d_down, sums_fin, loss_blk = _final_fwd_bwd(x1, down, loss_target[0], g_final[None, :], gate_f, tr)

    d_up = _matmul(d_down, w_down_f, tb=True, out_dtype=BF16, name="mm_d_up",
                   post=lambda d_act, up_: d_act * 2.0 * jnp.maximum(up_, 0.0), post_args=(up,))
    g_w_down = _matmul(act, d_down, ta=True, out_dtype=BF16, name="mm_g_down")

    def scatter(pairs, cols, lo, hi, into=None):
        return _scatter_regions(pairs, cols, (lo, hi, 8), into)

    dh2, (got,) = _matmul(d_up, w_up_f, tb=True, name="mm_d_h2", side=_swap_halves([g_w_down], [False]))
    p_down = pair_sum('w_down', g_w_down, got, False)
    g_w_up, got_down = _matmul(h2, d_up, ta=True, out_dtype=BF16, name="mm_g_up", side=scatter([p_down], [False], 0, 3))
    (dx1, d_mix, sums2), (got, *got_down) = _norm_mod_bwd2(x1, dh2, dx2, mix, g_mlp, mod3, tr, side=_sides(
        _swap_halves([g_w_up], [True]), scatter([p_down], [False], 3, 4, got_down)))
    p_up = pair_sum('w_up', g_w_up, got, True)
    d_mrg = _matmul(d_mix, w_out_f, tb=True, name="mm_d_mrg")
    g_w_out = _matmul(mrg, d_mix, ta=True, out_dtype=BF16, name="mm_g_out")
    dz = _dz_start(t, n_in, n_ctx)
    d_ya, dz = _merge_bwd(d_mrg, y_attn, z, gl_off, dz, n_ctx, tr, tcol, "merge_bwd_attn")
    d_yr, dz = _merge_bwd(d_mrg, y_rnn, z, gl_off + d, dz, n_ctx, tr, tcol, "merge_bwd_rnn")
    d_o = _matmul(d_ya, w_oa_f, tb=True, out_dtype=BF16, name="mm_d_o")
    g_w_oa = _matmul(attn_o, d_ya, ta=True, out_dtype=BF16, name="mm_g_o_attn")
    d_u = _matmul(d_yr, w_or_f, tb=True, name="mm_d_u")
    g_w_or = _matmul(u, d_yr, ta=True, out_dtype=BF16, name="mm_g_o_rnn")
    d_rnn, dz = _rnn_gate_bwd(d_u, h_f, h_b, z, xg_off, dz, n_ctx, tr, tcol)
    gs_f, gs_b = _scan([(a_f, d_rnn, "lat_ctx_down"), (a_b, d_rnn, "lat_ctx_up")], post=True, n_ctx_rows=n_ctx,
                       name="scan_bwd")
    o_names, o_grads = ['w_o_attn', 'w_o_rnn', 'w_out'], [g_w_oa, g_w_or, g_w_out]
    (dz, g_w_rg, g_w_ig, sums_rnn), (got_down, *got_o) = _rnn_bwd(
        z, xr_off, xc, gs_f, gs_b, h_f, h_b, conv_w_f, w_rg[0], b_rg_f, w_ig[0], b_ig_f, lam_f, dz, n_ctx,
        side=_sides(scatter([p_down], [False], 4, 8, got_down), _swap_halves(o_grads, row3)))
    hs_down = chip_sum('w_down', p_down, got_down, False)
    p_o = [pair_sum(n, g, o, False) for n, g, o in zip(o_names, o_grads, got_o)]
    gate_cols = 8 * LANES if g_w_rg.size % (8 * LANES * N_CHIPS * 16) == 0 else 2 * LANES
    gate_rows = g_w_rg.size // gate_cols
    gate_grads = [g_w_rg.reshape(gate_rows, gate_cols), g_w_ig.reshape(gate_rows, gate_cols)]
    (dq, dk, dz), (got_up, got_oa, got_rg, got_ig, gs_down) = _attn_bwd(
        qr, kr, z, v_off, d_o, attn_o, lse, dz, n_ctx, group, tr, side=_sides(
            _scatter_regions([p_up], [True]), _scatter_regions(p_o[:1], row3[:1]), _swap_halves(gate_grads, row3[:2]),
            _join_halves([hs_down], [False])))
    hs_up = chip_sum('w_up', p_up, got_up, True)
    hs_oa = chip_sum('w_o_attn', p_o[0], got_oa, False)
    p_gate = [pair_sum(n, g, o, False) for n, g, o in zip(GATES, gate_grads, (got_rg, got_ig))]
    dz, g_q_gain = _head_prep_bwd(z, 0, attn_w // LANES, q_gain, cos, sin, dq, n_ctx, tr, dz, "q_prep_bwd")
    dz, g_k_gain = _head_prep_bwd(z, k_off, kv_w // LANES, k_gain, cos, sin, dk, 0, tr, dz, "k_prep_bwd")
    half = d // 2
    h_sibling = lax.dynamic_slice(h, (0, (1 - ci) * half), (t, half))
    h_own = lax.dynamic_slice(h, (0, ci * half), (t, half))
    late = p_o[1:] + p_gate
    g_sibling, (got_rg, got_ig, *got_late) = _matmul(
        h_sibling, dz, ta=True, out_dtype=BF16, name="mm_g_in_sibling", side=_sides(
            _scatter_regions(p_gate, row3[:2]), scatter(p_o[1:], row3[:2], 0, 4)))
    g_own, (got_or, got_out, got, gs_up, gs_oa) = _matmul(
        h_own, dz, ta=True, out_dtype=BF16, name="mm_g_in_own", side=_sides(
            scatter(p_o[1:], row3[:2], 4, 8, got_late), _send_to_sibling([g_sibling]), _join_halves([hs_up], [True]),
            _join_halves([hs_oa], [False])))
    hs_late = [chip_sum(n, p, o, False) for n, p, o in zip(o_names[1:] + GATES, late, (got_or, got_out, got_rg, got_ig))]
    p_in = _add_pair(g_own, got, "pair_sum_w_in")
    dh, (got_in, fold_in, gs_or, gs_out) = _matmul(dz, w_in_f, tb=True, name="mm_d_h", side=_sides(
        _scatter_forward_1(p_in), _join_halves(hs_late[:2], row3[:2])))
    nbr_chips = jnp.stack([2 * (1 - xi) + yi, 2 * xi + 1 - yi]).astype(jnp.int32)
    passed = _fold_forwarded(p_in, fold_in, nbr_chips, "fold_w_in")
    (got_in,) = _exchange(_scatter_forward_2(passed, got_in), "scatter_w_in_fwd")
    grad_x, sums1 = _norm_mod_bwd1(ctx[0], x[0], dh, dx1, g_mix, mod4, tr)

    zeros_d = jnp.zeros((d,), F32)
    dmod_lat = jnp.concatenate([sums1[0], sums1[1], sums2[3], sums2[0], sums2[1], sums_fin[1]])
    dmod_ctx = jnp.concatenate([sums1[3], sums1[4]] + [zeros_d] * 4)
    small_parts = [dmod_lat, dmod_ctx, loss_blk[0, 0:1], sums1[2] + sums1[5], sums2[2], g_q_gain[0], g_k_gain[0],
                   sums_rnn[10], sums_fin[0], sums_rnn[6:10], sums_rnn[0:2], sums_rnn[2:4], sums_rnn[4:6]]
    pack1 = _pack(small_parts)
    got1 = _all_gather8(pack1, "gather_small_grads").reshape(N_DEV, -1, LANES)
    tot1 = _sum_leading(got1, "sum_small_grads")
    part_shapes = [a.shape for a in small_parts]
    (s_dmod_lat, s_dmod_ctx, s_loss, g_g_mix, g_g_mlp, g_q_gain, g_k_gain, g_conv_b, g_g_final,
     g_conv_w_f, g_b_rg_f, g_b_ig_f, g_lam_f) = _unpack(tot1, part_shapes)
    loss = s_loss[0]
    g_b_mod = (s_dmod_lat + s_dmod_ctx)[None, :]
    n_mod_rows = _part_rows(d_mod)
    dmod16 = jnp.concatenate([got1[:, :n_mod_rows].reshape(N_DEV, -1)[:, :d_mod], s_dmod_ctx[None, :],
                              jnp.zeros((16 - N_DEV - 1, d_mod), F32)], axis=0)
    dmod16_shard = lax.dynamic_slice(dmod16, (0, chip * (d_mod // N_CHIPS)), (16, d_mod // N_CHIPS))
    g_w_mod = _matmul(silu16, dmod16_shard, ta=True, name="mm_g_mod")
    dsilu_part = _matmul(dmod16_shard[N_DEV:], w_mod[0], tb=True, name="mm_d_silu")
    dsilu_all = _all_gather8(dsilu_part, "gather_d_silu").reshape(N_DEV, 8, d)
    g_c_ctx = _c_ctx_grad(dsilu_all, c_ctx[None, :])[0]

    def shard_of(full):
        w = full.shape[-1] // N_CHIPS
        return lax.dynamic_slice(full, (0, chip * w), (full.shape[0], w))

    grads = {
        'c_ctx': g_c_ctx, 'b_mod': g_b_mod, 'g_mix': g_g_mix[None, :], 'g_mlp': g_g_mlp[None, :],
        'q_gain': g_q_gain[None, :], 'k_gain': g_k_gain[None, :], 'conv_b': g_conv_b[None, :],
        'g_final': g_g_final,
        'conv_w': shard_of(g_conv_w_f)[None], 'b_rg': shard_of(g_b_rg_f)[None], 'b_ig': shard_of(g_b_ig_f)[None],
        'lru_lambda': shard_of(g_lam_f)[None], 'w_mod': g_w_mod[None],
    }

    delta, new_m, new_v = {}, {}, {}

    def adamw(n):
        shp = weights[n].shape
        as2d = (lambda a: a[0]) if n not in GATES else (lambda a: a.reshape(-1, LANES))
        dl, nm, nv, *g = _adamw(as2d(weights[n]), as2d(grads[n]), as2d(moms[n]), as2d(vars_[n]), "adamw_" + n,
                                copy_grad=n in BIG)
        delta[n], new_m[n], new_v[n] = dl.reshape(shp), nm.reshape(shp), nv.reshape(shp)
        if g:
            grads[n] = g[0].reshape(shp)

    hs_in = chip_sum('w_in', p_in, got_in, True)
    (gs_in,) = _exchange(_join_halves([hs_in], [True]), "join_w_in")
    for n, g in zip(['w_in', 'w_o_attn', 'w_o_rnn', 'w_out', 'w_up', 'w_down'], [gs_in, gs_oa, gs_or, gs_out, gs_up, gs_down]):
        grads[n] = g[None]
    half_cols = gate_cols // 2
    mine = jnp.concatenate([lax.dynamic_slice(hs, (0, ci * half_cols), (gate_rows // N_CHIPS, half_cols))
                            for hs in hs_late[2:]], axis=0)
    gate_all = _all_gather8(mine, "gather_gate_grads")
    gate_all = gate_all.reshape(N_CHIPS, 2, len(GATES), gate_rows // N_CHIPS, half_cols)
    for i, n in enumerate(GATES):
        grads[n] = jnp.moveaxis(gate_all[:, :, i], 1, 2).reshape(weights[n].shape)
    for n in ['w_mod'] + BIG + GATES:
        adamw(n)
    small_shapes = [weights[n].shape for n in SMALL]
    packed = [_pack([src[n] for n in SMALL], 512) for src in (weights, grads, moms, vars_)]
    outs = _adamw(*packed, "adamw_small", copy_grad=False)
    for res, out in zip((delta, new_m, new_v), outs):
        for n, a in zip(SMALL, _unpack(out, small_shapes)):
            res[n] = a
    return (loss, grad_x[None], *[grads[n] for n in WEIGHT_NAMES], *[delta[n] for n in WEIGHT_NAMES],
            *[new_m[n] for n in WEIGHT_NAMES], *[new_v[n] for n in WEIGHT_NAMES])
```
